```python
import jax, jax.numpy as jnp
from jax import lax
import numpy as np

D_MODEL = 1024
BATCH = 8
SEQ = 2048
DEPTH = 2

N_EVEN = (DEPTH + 1) // 2
N_ODD = DEPTH // 2
EPS = 1e-6

CONV_CH = 512
CONV_WIDTH = 31
HEAD_DIM = 64
HEADS_PER_GROUP = 8
DILATED_PAIRS = ((128, 1), (512, 4), (2048, 16))
N_GROUPS = len(DILATED_PAIRS)
ATTN_HEADS = N_GROUPS * HEADS_PER_GROUP
ATTN_WIDTH = ATTN_HEADS * HEAD_DIM
ATTN_OUT = HEADS_PER_GROUP * HEAD_DIM
ROPE_THETA = 10000.0
EVEN_IN = 2 * CONV_CH + 3 * ATTN_WIDTH
EVEN_OUT = CONV_CH + ATTN_OUT
SCONV_CH = 512
SCONV_WIDTH = 3
SG_GROUPS = 4
SG_HEAD = 128
SG_CH = SG_GROUPS * SG_HEAD
CHUNK = 128
ODD_IN = 3 * SCONV_CH + 2 * SG_CH
ODD_OUT = SCONV_CH + SG_CH
D_FF = 4 * D_MODEL

kernel_name = "hybrid_conformer_dilated_shortconv_gmlp_trunk"


def rms_norm(x, g):
    xf = x.astype(jnp.float32)
    y = xf * lax.rsqrt(jnp.mean(xf * xf, axis=-1, keepdims=True) + EPS)
    return (y * g.astype(jnp.float32)).astype(x.dtype)


def layer_norm(x, g, b):
    xf = x.astype(jnp.float32)
    mu = jnp.mean(xf, axis=-1, keepdims=True)
    xc = xf - mu
    y = xc * lax.rsqrt(jnp.mean(xc * xc, axis=-1, keepdims=True) + EPS)
    return (y * g.astype(jnp.float32) + b.astype(jnp.float32)).astype(x.dtype)


def rope_tables(seq):
    half = HEAD_DIM // 2
    inv = ROPE_THETA ** (-jnp.arange(half, dtype=jnp.float32) / half)
    ang = jnp.arange(seq, dtype=jnp.float32)[:, None] * inv[None, :]
    return jnp.cos(ang), jnp.sin(ang)


def apply_rope(x, cos, sin):
    x1, x2 = jnp.split(x, 2, axis=-1)
    c = cos[None, :, None, :]
    s = sin[None, :, None, :]
    return jnp.concatenate([x1 * c - x2 * s, x2 * c + x1 * s], axis=-1).astype(x.dtype)


def causal_depthwise_conv(x, kern):
    w = kern.shape[0]
    return lax.conv_general_dilated(
        x, kern[:, None, :].astype(x.dtype), window_strides=(1,),
        padding=[(w - 1, 0)], dimension_numbers=('NWC', 'WIO', 'NWC'),
        feature_group_count=x.shape[-1])


def banded_causal_attention(q, k, v, band):
    b, r, l, h, dh = q.shape
    nb = -(-l // band)
    lp = nb * band
    pad = ((0, 0), (0, 0), (0, lp - l), (0, 0), (0, 0))
    qb = jnp.pad(q, pad).reshape(b, r, nb, band, h, dh)
    kb = jnp.pad(k, pad).reshape(b, r, nb, band, h, dh)
    vb = jnp.pad(v, pad).reshape(b, r, nb, band, h, dh)
    zk = jnp.zeros_like(kb[:, :, :1])
    k2 = jnp.concatenate([jnp.concatenate([zk, kb[:, :, :-1]], axis=2), kb], axis=3)
    v2 = jnp.concatenate([jnp.concatenate([zk, vb[:, :, :-1]], axis=2), vb], axis=3)
    s = jnp.einsum('brnqhd,brnkhd->brnhqk', qb, k2,
                   preferred_element_type=jnp.float32) * (HEAD_DIM ** -0.5)
    qi = jnp.arange(band)[:, None] + band
    ki = jnp.arange(2 * band)[None, :]
    dist = qi - ki
    local = (dist >= 0) & (dist <= band)
    kvalid = (jnp.arange(nb)[:, None] * band - band + ki) >= 0
    mask = local[None, :, :] & kvalid[:, None, :]
    s = jnp.where(mask[None, None, :, None], s, -jnp.inf)
    lse = jax.nn.logsumexp(s, axis=-1, keepdims=True)
    p = jnp.exp(s - lse)
    o = jnp.einsum('brnhqk,brnkhd->brnqhd', p, v2.astype(jnp.float32))
    o = o.reshape(b, r, lp, h, dh)[:, :, :l]
    lse = lse[..., 0].transpose(0, 1, 2, 4, 3).reshape(b, r, lp, h)[:, :, :l]
    return o, lse


def dilated_group_attention(q, k, v, window, dilation):
    b, s, h, dh = q.shape
    l = s // dilation

    def to_res(t):
        return t.reshape(b, l, dilation, h, dh).transpose(0, 2, 1, 3, 4)

    o, lse = banded_causal_attention(to_res(q), to_res(k), to_res(v), window // dilation)
    o = o.transpose(0, 2, 1, 3, 4).reshape(b, s, h, dh)
    lse = lse.transpose(0, 2, 1, 3).reshape(b, s, h)
    return o, lse


def even_mixer(h, w_in, conv_k, conv_b, ln_g, ln_b, w_out, cos, sin):
    b, s, _ = h.shape
    z = h @ w_in
    a_lin, a_gate, qkv = jnp.split(z, [CONV_CH, 2 * CONV_CH], axis=-1)
    a = a_lin * jax.nn.sigmoid(a_gate)
    a = causal_depthwise_conv(a, conv_k) + conv_b.astype(a.dtype)
    a = jax.nn.silu(layer_norm(a, ln_g, ln_b))
    q, k, v = jnp.split(qkv, 3, axis=-1)
    q = apply_rope(q.reshape(b, s, ATTN_HEADS, HEAD_DIM), cos, sin)
    k = apply_rope(k.reshape(b, s, ATTN_HEADS, HEAD_DIM), cos, sin)
    v = v.reshape(b, s, ATTN_HEADS, HEAD_DIM)
    outs, lses = [], []
    for g, (window, dilation) in enumerate(DILATED_PAIRS):
        sl = slice(g * HEADS_PER_GROUP, (g + 1) * HEADS_PER_GROUP)
        o, lse = dilated_group_attention(q[:, :, sl], k[:, :, sl], v[:, :, sl], window, dilation)
        outs.append(o)
        lses.append(lse)
    wts = jax.nn.softmax(jnp.stack(lses, axis=0), axis=0)
    att = jnp.sum(wts[..., None] * jnp.stack(outs, axis=0), axis=0)
    att = att.reshape(b, s, ATTN_OUT).astype(a.dtype)
    return jnp.concatenate([a, att], axis=-1) @ w_out


def odd_mixer(h, w_in, sconv_k, sg_ln_g, sg_ln_b, sg_w, sg_b, w_out):
    b, s, _ = h.shape
    z = h @ w_in
    gb, gc, xs, uv = jnp.split(z, [SCONV_CH, 2 * SCONV_CH, 3 * SCONV_CH], axis=-1)
    c_out = gb * causal_depthwise_conv(gc * xs, sconv_k)
    u, v = jnp.split(jax.nn.gelu(uv), 2, axis=-1)
    v = layer_norm(v, sg_ln_g, sg_ln_b)
    v = v.reshape(b, s // CHUNK, CHUNK, SG_GROUPS, SG_HEAD)
    ws = sg_w * jnp.tril(jnp.ones((CHUNK, CHUNK), dtype=sg_w.dtype))[None]
    v = jnp.einsum('gts,bnsgc->bntgc', ws.astype(v.dtype), v) + sg_b.T.astype(v.dtype)[None, None, :, :, None]
    d_out = u * v.reshape(b, s, SG_CH)
    return jnp.concatenate([c_out, d_out], axis=-1) @ w_out


def channel_mixer(h, w1, w2):
    return jnp.square(jax.nn.relu(h @ w1)) @ w2


def _fwd_setup_inputs(seed: int = 0) -> dict:
    key = jax.random.key(seed)
    ks = jax.random.split(key, 20)
    f32 = jnp.float32

    def nrm(k, shape, scale):
        return jax.random.normal(k, shape, f32) * scale

    return {
        "x": nrm(ks[0], (BATCH, SEQ, D_MODEL), 1.0),
        "norm_mix_g": 1.0 + nrm(ks[1], (DEPTH, D_MODEL), 0.02),
        "norm_ffn_g": 1.0 + nrm(ks[2], (DEPTH, D_MODEL), 0.02),
        "even_w_in": nrm(ks[3], (N_EVEN, D_MODEL, EVEN_IN), D_MODEL ** -0.5),
        "even_conv_k": nrm(ks[4], (N_EVEN, CONV_WIDTH, CONV_CH), CONV_WIDTH ** -0.5),
        "even_conv_b": nrm(ks[5], (N_EVEN, CONV_CH), 0.02),
        "even_ln_g": 1.0 + nrm(ks[6], (N_EVEN, CONV_CH), 0.02),
        "even_ln_b": nrm(ks[7], (N_EVEN, CONV_CH), 0.02),
        "even_w_out": nrm(ks[8], (N_EVEN, EVEN_OUT, D_MODEL), EVEN_OUT ** -0.5),
        "odd_w_in": nrm(ks[9], (N_ODD, D_MODEL, ODD_IN), D_MODEL ** -0.5),
        "odd_conv_k": nrm(ks[10], (N_ODD, SCONV_WIDTH, SCONV_CH), SCONV_WIDTH ** -0.5),
        "odd_ln_g": 1.0 + nrm(ks[11], (N_ODD, SG_CH), 0.02),
        "odd_ln_b": nrm(ks[12], (N_ODD, SG_CH), 0.02),
        "odd_sg_w": nrm(ks[13], (N_ODD, SG_GROUPS, CHUNK, CHUNK), CHUNK ** -0.5),
        "odd_sg_b": 1.0 + nrm(ks[14], (N_ODD, SG_GROUPS, CHUNK), 0.02),
        "odd_w_out": nrm(ks[15], (N_ODD, ODD_OUT, D_MODEL), ODD_OUT ** -0.5),
        "ffn_w1": nrm(ks[16], (DEPTH, D_MODEL, D_FF), D_MODEL ** -0.5),
        "ffn_w2": nrm(ks[17], (DEPTH, D_FF, D_MODEL), D_FF ** -0.5),
        "final_g": 1.0 + nrm(ks[18], (D_MODEL,), 0.02),
    }


def _fwd_reference(x, norm_mix_g, norm_ffn_g, even_w_in, even_conv_k, even_conv_b, even_ln_g,
              even_ln_b, even_w_out, odd_w_in, odd_conv_k, odd_ln_g, odd_ln_b, odd_sg_w,
              odd_sg_b, odd_w_out, ffn_w1, ffn_w2, final_g):
    cos, sin = rope_tables(x.shape[1])
    h = x
    for i in range(DEPTH):
        hn = rms_norm(h, norm_mix_g[i])
        if i % 2 == 0:
            j = i // 2
            mix = even_mixer(hn, even_w_in[j], even_conv_k[j], even_conv_b[j], even_ln_g[j],
                             even_ln_b[j], even_w_out[j], cos, sin)
        else:
            j = i // 2
            mix = odd_mixer(hn, odd_w_in[j], odd_conv_k[j], odd_ln_g[j], odd_ln_b[j],
                            odd_sg_w[j], odd_sg_b[j], odd_w_out[j])
        h = h + mix.astype(h.dtype)
        h = h + channel_mixer(rms_norm(h, norm_ffn_g[i]), ffn_w1[i], ffn_w2[i]).astype(h.dtype)
    return rms_norm(h, final_g)


import jax as _jax
import jax.numpy as _jnp

TWIN_FORMAT = 'train_step'
FWD_PARAMS = ['x', 'norm_mix_g', 'norm_ffn_g', 'even_w_in', 'even_conv_k', 'even_conv_b', 'even_ln_g', 'even_ln_b', 'even_w_out', 'odd_w_in', 'odd_conv_k', 'odd_ln_g', 'odd_ln_b', 'odd_sg_w', 'odd_sg_b', 'odd_w_out', 'ffn_w1', 'ffn_w2', 'final_g']
TWIN_WEIGHTS = ['norm_mix_g', 'norm_ffn_g', 'even_w_in', 'even_conv_k', 'even_conv_b', 'even_ln_g', 'even_ln_b', 'even_w_out', 'odd_w_in', 'odd_conv_k', 'odd_ln_g', 'odd_ln_b', 'odd_sg_w', 'odd_sg_b', 'odd_w_out', 'ffn_w1', 'ffn_w2', 'final_g']
TWIN_DIFF_INPUT = 'x'
TWIN_INPUTS = ['x', 'norm_mix_g', 'norm_ffn_g', 'even_w_in', 'even_conv_k', 'even_conv_b', 'even_ln_g', 'even_ln_b', 'even_w_out', 'odd_w_in', 'odd_conv_k', 'odd_ln_g', 'odd_ln_b', 'odd_sg_w', 'odd_sg_b', 'odd_w_out', 'ffn_w1', 'ffn_w2', 'final_g', 'loss_target', 'm_norm_mix_g', 'm_norm_ffn_g', 'm_even_w_in', 'm_even_conv_k', 'm_even_conv_b', 'm_even_ln_g', 'm_even_ln_b', 'm_even_w_out', 'm_odd_w_in', 'm_odd_conv_k', 'm_odd_ln_g', 'm_odd_ln_b', 'm_odd_sg_w', 'm_odd_sg_b', 'm_odd_w_out', 'm_ffn_w1', 'm_ffn_w2', 'm_final_g', 'v_norm_mix_g', 'v_norm_ffn_g', 'v_even_w_in', 'v_even_conv_k', 'v_even_conv_b', 'v_even_ln_g', 'v_even_ln_b', 'v_even_w_out', 'v_odd_w_in', 'v_odd_conv_k', 'v_odd_ln_g', 'v_odd_ln_b', 'v_odd_sg_w', 'v_odd_sg_b', 'v_odd_w_out', 'v_ffn_w1', 'v_ffn_w2', 'v_final_g']
TWIN_OUTPUTS = ['loss', 'grad_x', 'grad_norm_mix_g', 'grad_norm_ffn_g', 'grad_even_w_in', 'grad_even_conv_k', 'grad_even_conv_b', 'grad_even_ln_g', 'grad_even_ln_b', 'grad_even_w_out', 'grad_odd_w_in', 'grad_odd_conv_k', 'grad_odd_ln_g', 'grad_odd_ln_b', 'grad_odd_sg_w', 'grad_odd_sg_b', 'grad_odd_w_out', 'grad_ffn_w1', 'grad_ffn_w2', 'grad_final_g', 'delta_norm_mix_g', 'delta_norm_ffn_g', 'delta_even_w_in', 'delta_even_conv_k', 'delta_even_conv_b', 'delta_even_ln_g', 'delta_even_ln_b', 'delta_even_w_out', 'delta_odd_w_in', 'delta_odd_conv_k', 'delta_odd_ln_g', 'delta_odd_ln_b', 'delta_odd_sg_w', 'delta_odd_sg_b', 'delta_odd_w_out', 'delta_ffn_w1', 'delta_ffn_w2', 'delta_final_g', 'new_m_norm_mix_g', 'new_m_norm_ffn_g', 'new_m_even_w_in', 'new_m_even_conv_k', 'new_m_even_conv_b', 'new_m_even_ln_g', 'new_m_even_ln_b', 'new_m_even_w_out', 'new_m_odd_w_in', 'new_m_odd_conv_k', 'new_m_odd_ln_g', 'new_m_odd_ln_b', 'new_m_odd_sg_w', 'new_m_odd_sg_b', 'new_m_odd_w_out', 'new_m_ffn_w1', 'new_m_ffn_w2', 'new_m_final_g', 'new_v_norm_mix_g', 'new_v_norm_ffn_g', 'new_v_even_w_in', 'new_v_even_conv_k', 'new_v_even_conv_b', 'new_v_even_ln_g', 'new_v_even_ln_b', 'new_v_even_w_out', 'new_v_odd_w_in', 'new_v_odd_conv_k', 'new_v_odd_ln_g', 'new_v_odd_ln_b', 'new_v_odd_sg_w', 'new_v_odd_sg_b', 'new_v_odd_w_out', 'new_v_ffn_w1', 'new_v_ffn_w2', 'new_v_final_g']
TWIN_LEAF_KINDS = {'loss': 'loss', 'grad_x': 'grad_x', 'grad_norm_mix_g': 'grad_w', 'grad_norm_ffn_g': 'grad_w', 'grad_even_w_in': 'grad_w', 'grad_even_conv_k': 'grad_w', 'grad_even_conv_b': 'grad_w', 'grad_even_ln_g': 'grad_w', 'grad_even_ln_b': 'grad_w', 'grad_even_w_out': 'grad_w', 'grad_odd_w_in': 'grad_w', 'grad_odd_conv_k': 'grad_w', 'grad_odd_ln_g': 'grad_w', 'grad_odd_ln_b': 'grad_w', 'grad_odd_sg_w': 'grad_w', 'grad_odd_sg_b': 'grad_w', 'grad_odd_w_out': 'grad_w', 'grad_ffn_w1': 'grad_w', 'grad_ffn_w2': 'grad_w', 'grad_final_g': 'grad_w', 'delta_norm_mix_g': 'delta_w', 'delta_norm_ffn_g': 'delta_w', 'delta_even_w_in': 'delta_w', 'delta_even_conv_k': 'delta_w', 'delta_even_conv_b': 'delta_w', 'delta_even_ln_g': 'delta_w', 'delta_even_ln_b': 'delta_w', 'delta_even_w_out': 'delta_w', 'delta_odd_w_in': 'delta_w', 'delta_odd_conv_k': 'delta_w', 'delta_odd_ln_g': 'delta_w', 'delta_odd_ln_b': 'delta_w', 'delta_odd_sg_w': 'delta_w', 'delta_odd_sg_b': 'delta_w', 'delta_odd_w_out': 'delta_w', 'delta_ffn_w1': 'delta_w', 'delta_ffn_w2': 'delta_w', 'delta_final_g': 'delta_w', 'new_m_norm_mix_g': 'new_m', 'new_m_norm_ffn_g': 'new_m', 'new_m_even_w_in': 'new_m', 'new_m_even_conv_k': 'new_m', 'new_m_even_conv_b': 'new_m', 'new_m_even_ln_g': 'new_m', 'new_m_even_ln_b': 'new_m', 'new_m_even_w_out': 'new_m', 'new_m_odd_w_in': 'new_m', 'new_m_odd_conv_k': 'new_m', 'new_m_odd_ln_g': 'new_m', 'new_m_odd_ln_b': 'new_m', 'new_m_odd_sg_w': 'new_m', 'new_m_odd_sg_b': 'new_m', 'new_m_odd_w_out': 'new_m', 'new_m_ffn_w1': 'new_m', 'new_m_ffn_w2': 'new_m', 'new_m_final_g': 'new_m', 'new_v_norm_mix_g': 'new_v', 'new_v_norm_ffn_g': 'new_v', 'new_v_even_w_in': 'new_v', 'new_v_even_conv_k': 'new_v', 'new_v_even_conv_b': 'new_v', 'new_v_even_ln_g': 'new_v', 'new_v_even_ln_b': 'new_v', 'new_v_even_w_out': 'new_v', 'new_v_odd_w_in': 'new_v', 'new_v_odd_conv_k': 'new_v', 'new_v_odd_ln_g': 'new_v', 'new_v_odd_ln_b': 'new_v', 'new_v_odd_sg_w': 'new_v', 'new_v_odd_sg_b': 'new_v', 'new_v_odd_w_out': 'new_v', 'new_v_ffn_w1': 'new_v', 'new_v_ffn_w2': 'new_v', 'new_v_final_g': 'new_v'}


def _forward(args):
    return _fwd_reference(*[args[k] for k in FWD_PARAMS])


def _output_shape():
    out = _jax.eval_shape(lambda: _forward(_fwd_setup_inputs(0)))
    return out.shape, out.dtype

N_MICROBATCH = 1
ADAM_LR = 0.001
ADAM_B1 = 0.9
ADAM_B2 = 0.999
ADAM_EPS = 1e-08
ADAM_WD = 0.01
ADAM_STEP = 10
PER_EXAMPLE_BATCH_AXIS = {'x': 0, 'loss_target': 0}
SHARED_INPUTS = []
_WEIGHT_DTYPES = {'norm_mix_g': _jnp.float32, 'norm_ffn_g': _jnp.float32, 'even_w_in': _jnp.float32, 'even_conv_k': _jnp.float32, 'even_conv_b': _jnp.float32, 'even_ln_g': _jnp.float32, 'even_ln_b': _jnp.float32, 'even_w_out': _jnp.float32, 'odd_w_in': _jnp.float32, 'odd_conv_k': _jnp.float32, 'odd_ln_g': _jnp.float32, 'odd_ln_b': _jnp.float32, 'odd_sg_w': _jnp.float32, 'odd_sg_b': _jnp.float32, 'odd_w_out': _jnp.float32, 'ffn_w1': _jnp.float32, 'ffn_w2': _jnp.float32, 'final_g': _jnp.float32}
MOMENT_SCALE = {'norm_mix_g': 8.931181e-02, 'norm_ffn_g': 1.072673e-01, 'even_w_in': 3.317041e-02, 'even_conv_k': 9.534630e-02, 'even_conv_b': 2.109657e-01, 'even_ln_g': 1.233066e-01, 'even_ln_b': 1.128427e-01, 'even_w_out': 6.706452e-02, 'odd_w_in': 6.294296e-02, 'odd_conv_k': 6.973021e-02, 'odd_ln_g': 3.331770e-02, 'odd_ln_b': 3.400426e-02, 'odd_sg_w': 3.354865e-02, 'odd_sg_b': 4.583526e-02, 'odd_w_out': 6.557165e-02, 'ffn_w1': 5.459775e-02, 'ffn_w2': 1.006694e-01, 'final_g': 1.628945e+01}


def _to_microbatches(a, axis):
    t = _jnp.moveaxis(a, axis, 0)
    t = t.reshape((N_MICROBATCH, t.shape[0] // N_MICROBATCH) + t.shape[1:])
    return _jnp.moveaxis(t, 1, axis + 1)


def setup_inputs(seed: int = 0) -> dict:
    inp = _fwd_setup_inputs(seed)
    key = _jax.random.fold_in(_jax.random.key(seed), 7919)
    shape, _ = _output_shape()
    out = dict(inp)
    out["loss_target"] = _jax.random.normal(_jax.random.fold_in(key, 0), shape, _jnp.float32)
    for i, name in enumerate(TWIN_WEIGHTS):
        w = inp[name].astype(_jnp.float32)
        if MOMENT_SCALE is None:
            s = _jnp.sqrt(_jnp.mean(_jnp.square(w)) + 1e-30)
        else:
            s = MOMENT_SCALE[name]
        km, kv = _jax.random.split(_jax.random.fold_in(key, i + 1))
        out[name] = w
        out["m_" + name] = s * _jax.random.normal(km, w.shape, _jnp.float32)
        out["v_" + name] = (s * s) * _jax.random.uniform(kv, w.shape, _jnp.float32, 0.5, 1.5)
    if N_MICROBATCH > 1:
        for name, axis in PER_EXAMPLE_BATCH_AXIS.items():
            out[name] = _to_microbatches(out[name], axis)
    return {'x': out['x'], 'norm_mix_g': out['norm_mix_g'], 'norm_ffn_g': out['norm_ffn_g'], 'even_w_in': out['even_w_in'], 'even_conv_k': out['even_conv_k'], 'even_conv_b': out['even_conv_b'], 'even_ln_g': out['even_ln_g'], 'even_ln_b': out['even_ln_b'], 'even_w_out': out['even_w_out'], 'odd_w_in': out['odd_w_in'], 'odd_conv_k': out['odd_conv_k'], 'odd_ln_g': out['odd_ln_g'], 'odd_ln_b': out['odd_ln_b'], 'odd_sg_w': out['odd_sg_w'], 'odd_sg_b': out['odd_sg_b'], 'odd_w_out': out['odd_w_out'], 'ffn_w1': out['ffn_w1'], 'ffn_w2': out['ffn_w2'], 'final_g': out['final_g'], 'loss_target': out['loss_target'], 'm_norm_mix_g': out['m_norm_mix_g'], 'm_norm_ffn_g': out['m_norm_ffn_g'], 'm_even_w_in': out['m_even_w_in'], 'm_even_conv_k': out['m_even_conv_k'], 'm_even_conv_b': out['m_even_conv_b'], 'm_even_ln_g': out['m_even_ln_g'], 'm_even_ln_b': out['m_even_ln_b'], 'm_even_w_out': out['m_even_w_out'], 'm_odd_w_in': out['m_odd_w_in'], 'm_odd_conv_k': out['m_odd_conv_k'], 'm_odd_ln_g': out['m_odd_ln_g'], 'm_odd_ln_b': out['m_odd_ln_b'], 'm_odd_sg_w': out['m_odd_sg_w'], 'm_odd_sg_b': out['m_odd_sg_b'], 'm_odd_w_out': out['m_odd_w_out'], 'm_ffn_w1': out['m_ffn_w1'], 'm_ffn_w2': out['m_ffn_w2'], 'm_final_g': out['m_final_g'], 'v_norm_mix_g': out['v_norm_mix_g'], 'v_norm_ffn_g': out['v_norm_ffn_g'], 'v_even_w_in': out['v_even_w_in'], 'v_even_conv_k': out['v_even_conv_k'], 'v_even_conv_b': out['v_even_conv_b'], 'v_even_ln_g': out['v_even_ln_g'], 'v_even_ln_b': out['v_even_ln_b'], 'v_even_w_out': out['v_even_w_out'], 'v_odd_w_in': out['v_odd_w_in'], 'v_odd_conv_k': out['v_odd_conv_k'], 'v_odd_ln_g': out['v_odd_ln_g'], 'v_odd_ln_b': out['v_odd_ln_b'], 'v_odd_sg_w': out['v_odd_sg_w'], 'v_odd_sg_b': out['v_odd_sg_b'], 'v_odd_w_out': out['v_odd_w_out'], 'v_ffn_w1': out['v_ffn_w1'], 'v_ffn_w2': out['v_ffn_w2'], 'v_final_g': out['v_final_g']}


def _loss(weights, diff, rest, loss_target):
    with _jax.named_scope("forward"):
        args = {**rest, TWIN_DIFF_INPUT: diff, **{k: w.astype(_WEIGHT_DTYPES[k]) for k, w in weights.items()}}
        y = _forward(args)
    with _jax.named_scope("loss_head"):
        err = _jnp.square(y.astype(_jnp.float32) - loss_target)
        return 0.5 * _jnp.sum(_jnp.mean(err, axis=-1)) if err.ndim else 0.5 * err


def _adamw(w, g, m, v):
    m = ADAM_B1 * m + (1.0 - ADAM_B1) * g
    v = ADAM_B2 * v + (1.0 - ADAM_B2) * _jnp.square(g)
    m_hat = m / (1.0 - ADAM_B1 ** ADAM_STEP)
    v_hat = v / (1.0 - ADAM_B2 ** ADAM_STEP)
    delta = -ADAM_LR * (m_hat / (_jnp.sqrt(v_hat) + ADAM_EPS) + ADAM_WD * w)
    return delta, m, v


def reference(x, norm_mix_g, norm_ffn_g, even_w_in, even_conv_k, even_conv_b, even_ln_g, even_ln_b, even_w_out, odd_w_in, odd_conv_k, odd_ln_g, odd_ln_b, odd_sg_w, odd_sg_b, odd_w_out, ffn_w1, ffn_w2, final_g, loss_target, m_norm_mix_g, m_norm_ffn_g, m_even_w_in, m_even_conv_k, m_even_conv_b, m_even_ln_g, m_even_ln_b, m_even_w_out, m_odd_w_in, m_odd_conv_k, m_odd_ln_g, m_odd_ln_b, m_odd_sg_w, m_odd_sg_b, m_odd_w_out, m_ffn_w1, m_ffn_w2, m_final_g, v_norm_mix_g, v_norm_ffn_g, v_even_w_in, v_even_conv_k, v_even_conv_b, v_even_ln_g, v_even_ln_b, v_even_w_out, v_odd_w_in, v_odd_conv_k, v_odd_ln_g, v_odd_ln_b, v_odd_sg_w, v_odd_sg_b, v_odd_w_out, v_ffn_w1, v_ffn_w2, v_final_g):
    given = dict(x=x, norm_mix_g=norm_mix_g, norm_ffn_g=norm_ffn_g, even_w_in=even_w_in, even_conv_k=even_conv_k, even_conv_b=even_conv_b, even_ln_g=even_ln_g, even_ln_b=even_ln_b, even_w_out=even_w_out, odd_w_in=odd_w_in, odd_conv_k=odd_conv_k, odd_ln_g=odd_ln_g, odd_ln_b=odd_ln_b, odd_sg_w=odd_sg_w, odd_sg_b=odd_sg_b, odd_w_out=odd_w_out, ffn_w1=ffn_w1, ffn_w2=ffn_w2, final_g=final_g, loss_target=loss_target, m_norm_mix_g=m_norm_mix_g, m_norm_ffn_g=m_norm_ffn_g, m_even_w_in=m_even_w_in, m_even_conv_k=m_even_conv_k, m_even_conv_b=m_even_conv_b, m_even_ln_g=m_even_ln_g, m_even_ln_b=m_even_ln_b, m_even_w_out=m_even_w_out, m_odd_w_in=m_odd_w_in, m_odd_conv_k=m_odd_conv_k, m_odd_ln_g=m_odd_ln_g, m_odd_ln_b=m_odd_ln_b, m_odd_sg_w=m_odd_sg_w, m_odd_sg_b=m_odd_sg_b, m_odd_w_out=m_odd_w_out, m_ffn_w1=m_ffn_w1, m_ffn_w2=m_ffn_w2, m_final_g=m_final_g, v_norm_mix_g=v_norm_mix_g, v_norm_ffn_g=v_norm_ffn_g, v_even_w_in=v_even_w_in, v_even_conv_k=v_even_conv_k, v_even_conv_b=v_even_conv_b, v_even_ln_g=v_even_ln_g, v_even_ln_b=v_even_ln_b, v_even_w_out=v_even_w_out, v_odd_w_in=v_odd_w_in, v_odd_conv_k=v_odd_conv_k, v_odd_ln_g=v_odd_ln_g, v_odd_ln_b=v_odd_ln_b, v_odd_sg_w=v_odd_sg_w, v_odd_sg_b=v_odd_sg_b, v_odd_w_out=v_odd_w_out, v_ffn_w1=v_ffn_w1, v_ffn_w2=v_ffn_w2, v_final_g=v_final_g)
    weights = {n: given[n] for n in TWIN_WEIGHTS}
    shared = {n: given[n] for n in SHARED_INPUTS}
    per_example = {n: given[n] for n in ['x']}
    grad_fn = _jax.value_and_grad(_loss, argnums=(0, 1))

    def one_microbatch(ex, loss_target):
        ex = dict(ex)
        diff = ex.pop(TWIN_DIFF_INPUT)
        return grad_fn(weights, diff, {**shared, **ex}, loss_target)

    if N_MICROBATCH == 1:
        loss, (grad_w, grad_x) = one_microbatch(per_example, given["loss_target"])
    else:
        def body(carry, xs):
            loss_sum, grad_sum = carry
            l_k, (gw_k, gx_k) = one_microbatch(xs[0], xs[1])
            with _jax.named_scope("update"):
                return (loss_sum + l_k, _jax.tree.map(_jnp.add, grad_sum, gw_k)), gx_k

        init = (_jnp.zeros((), _jnp.float32), _jax.tree.map(_jnp.zeros_like, weights))
        (loss, grad_w), grad_x = _jax.lax.scan(body, init, (per_example, given["loss_target"]))
    with _jax.named_scope("update"):
        delta_w, new_m, new_v = {}, {}, {}
        for n in TWIN_WEIGHTS:
            delta_w[n], new_m[n], new_v[n] = _adamw(weights[n], grad_w[n], given["m_" + n], given["v_" + n])
    return (loss, grad_x, *[grad_w[n] for n in TWIN_WEIGHTS], *[delta_w[n] for n in TWIN_WEIGHTS],
            *[new_m[n] for n in TWIN_WEIGHTS], *[new_v[n] for n in TWIN_WEIGHTS])
```

```python
import functools

import jax
import jax.numpy as jnp
from jax import lax
from jax.experimental import pallas as pl
from jax.experimental.pallas import tpu as pltpu

F32 = jnp.float32
BF16 = jnp.bfloat16

T = 2048
D = 1024
CONV_CH = 512
CONV_W = 31
HEAD_DIM = 64
ATT_W = 1536
EVEN_IN = 5632
ODD_IN = 2560
SCONV_W = 3
SG_GROUPS = 4
CHUNK = 128
D_FF = 4096
EPS = 1e-6
DILATIONS = (1, 4, 16)
BAND = 128
SCALE = HEAD_DIM ** -0.5
NEG = -1e30

ADAM_LR = 0.001
ADAM_B1 = 0.9
ADAM_B2 = 0.999
ADAM_EPS = 1e-08
ADAM_WD = 0.01
ADAM_STEP = 10

V7X_VMEM_BYTES = 64 * 2 ** 20
VMEM_LIMIT = V7X_VMEM_BYTES - 8 * 2 ** 20
LANES = 128


def _pcall(body, **kw):
    return pl.pallas_call(body, **kw)


def _params(*sem):
    return pltpu.CompilerParams(dimension_semantics=sem, vmem_limit_bytes=VMEM_LIMIT)


def _dot(a, b, dims):
    return lax.dot_general(a, b, (dims, ((), ())), preferred_element_type=F32)


def _nn(a, b):
    return _dot(a, b, ((1,), (0,)))


def _nt(a, b):
    return _dot(a, b, ((1,), (1,)))


def _tn(a, b):
    return _dot(a, b, ((0,), (0,)))


def _sigmoid(x):
    return 1.0 / (1.0 + jnp.exp(-x))


def _mm(name, mode, a, b, m, n, k, out_dtypes, *, tm=512, tn=512, b_off=0, extras=(), epi=None):
    if mode == "nn":
        a_spec = pl.BlockSpec((tm, k), lambda i, j: (i, 0))
        b_spec = pl.BlockSpec((k, tn), lambda i, j: (0, j + b_off))
        dims = ((1,), (0,))
    elif mode == "nt":
        a_spec = pl.BlockSpec((tm, k), lambda i, j: (i, 0))
        b_spec = pl.BlockSpec((tn, k), lambda i, j: (j, 0))
        dims = ((1,), (1,))
    else:
        a_spec = pl.BlockSpec((k, tm), lambda i, j: (0, i))
        b_spec = pl.BlockSpec((k, tn), lambda i, j: (0, j))
        dims = ((0,), (0,))
    o_spec = pl.BlockSpec((tm, tn), lambda i, j: (i, j))
    n_extra = len(extras)

    def body(a_ref, b_ref, *rest):
        acc = _dot(a_ref[...].astype(BF16), b_ref[...].astype(BF16), dims)
        vals = epi(acc, *[e[...] for e in rest[:n_extra]]) if epi is not None else (acc,)
        for o_ref, v in zip(rest[n_extra:], vals):
            o_ref[...] = v.astype(o_ref.dtype)

    outs = _pcall(
        body, name=name, grid=(m // tm, n // tn),
        in_specs=[a_spec, b_spec] + [o_spec] * n_extra,
        out_specs=[o_spec] * len(out_dtypes),
        out_shape=[jax.ShapeDtypeStruct((m, n), dt) for dt in out_dtypes],
        compiler_params=_params("parallel", "parallel"),
    )(a, b, *extras)
    return outs[0] if len(out_dtypes) == 1 else outs


def _rms_fwd(name, h, g, tm=512):
    def body(h_ref, g_ref, o_ref):
        x = h_ref[...]
        r = lax.rsqrt(jnp.mean(x * x, axis=-1, keepdims=True) + EPS)
        o_ref[...] = ((x * r) * g_ref[...]).astype(BF16)

    return _pcall(
        body, name=name, grid=(T // tm,),
        in_specs=[pl.BlockSpec((tm, D), lambda i: (i, 0)), pl.BlockSpec((1, D), lambda i: (0, 0))],
        out_specs=pl.BlockSpec((tm, D), lambda i: (i, 0)),
        out_shape=jax.ShapeDtypeStruct((T, D), BF16),
        compiler_params=_params("parallel"),
    )(h, g)


def _rms_bwd(name, h, dhn, g, dres, tm=512):
    def body(h_ref, d_ref, g_ref, r_ref, dh_ref, dg_ref):
        x = h_ref[...]
        r = lax.rsqrt(jnp.mean(x * x, axis=-1, keepdims=True) + EPS)
        nrm = x * r
        dy = d_ref[...]
        dn = dy * g_ref[...]
        dh_ref[...] = r_ref[...] + r * (dn - nrm * jnp.mean(dn * nrm, axis=-1, keepdims=True))

        @pl.when(pl.program_id(0) == 0)
        def _():
            dg_ref[...] = jnp.zeros_like(dg_ref)

        dg_ref[...] += jnp.sum(dy * nrm, axis=0, keepdims=True)

    row = pl.BlockSpec((tm, D), lambda i: (i, 0))
    vec = pl.BlockSpec((1, D), lambda i: (0, 0))
    return _pcall(
        body, name=name, grid=(T // tm,),
        in_specs=[row, row, vec, row], out_specs=[row, vec],
        out_shape=[jax.ShapeDtypeStruct((T, D), F32), jax.ShapeDtypeStruct((1, D), F32)],
        compiler_params=_params("arbitrary"),
    )(h, dhn, g, dres)


def _loss_head(h, g, target, tm=512):
    def body(h_ref, g_ref, t_ref, dh_ref, dg_ref, loss_ref):
        x = h_ref[...]
        r = lax.rsqrt(jnp.mean(x * x, axis=-1, keepdims=True) + EPS)
        nrm = x * r
        gain = g_ref[...]
        err = nrm * gain - t_ref[...]
        dy = err * (1.0 / D)
        dn = dy * gain
        dh_ref[...] = r * (dn - nrm * jnp.mean(dn * nrm, axis=-1, keepdims=True))

        @pl.when(pl.program_id(0) == 0)
        def _():
            dg_ref[...] = jnp.zeros_like(dg_ref)
            loss_ref[...] = jnp.zeros_like(loss_ref)

        dg_ref[...] += jnp.sum(dy * nrm, axis=0, keepdims=True)
        part = jnp.sum(jnp.sum(err * err, axis=1, keepdims=True), axis=0, keepdims=True) * (0.5 / D)
        loss_ref[...] += jnp.broadcast_to(part, (1, LANES))

    row = pl.BlockSpec((tm, D), lambda i: (i, 0))
    vec = pl.BlockSpec((1, D), lambda i: (0, 0))
    return _pcall(
        body, name="loss_head", grid=(T // tm,),
        in_specs=[row, vec, row], out_specs=[row, vec, pl.BlockSpec((1, LANES), lambda i: (0, 0))],
        out_shape=[jax.ShapeDtypeStruct((T, D), F32), jax.ShapeDtypeStruct((1, D), F32),
                   jax.ShapeDtypeStruct((1, LANES), F32)],
        compiler_params=_params("arbitrary"),
    )(h, g, target)


CONV_TILE = 256
CONV_HALO = 32


def _glu(z):
    return z[:, :CONV_CH] * _sigmoid(z[:, CONV_CH:])


def _econv_fwd(zc, conv_k, conv_b, ln_g, ln_b):
    R, H = CONV_TILE, CONV_HALO

    def body(z_ref, zh_ref, k_ref, b_ref, g_ref, be_ref, cv_ref, cat_ref):
        i = pl.program_id(0)
        glu = _glu(z_ref[...])
        halo = _glu(zh_ref[...]) * (i > 0).astype(F32)
        win = jnp.concatenate([halo, glu], axis=0)
        acc = jnp.zeros((R, CONV_CH), F32) + b_ref[...]
        for j in range(CONV_W):
            off = H - (CONV_W - 1) + j
            acc = acc + k_ref[j:j + 1, :] * win[off:off + R, :]
        cv_ref[...] = acc
        mu = jnp.mean(acc, axis=-1, keepdims=True)
        xc = acc - mu
        rstd = lax.rsqrt(jnp.mean(xc * xc, axis=-1, keepdims=True) + EPS)
        ln = xc * rstd * g_ref[...] + be_ref[...]
        cat_ref[...] = (ln * _sigmoid(ln)).astype(BF16)

    vec = pl.BlockSpec((1, CONV_CH), lambda i: (0, 0))
    return _pcall(
        body, name="econv_fwd", grid=(T // R,),
        in_specs=[pl.BlockSpec((R, 2 * CONV_CH), lambda i: (i, 0)),
                  pl.BlockSpec((H, 2 * CONV_CH), lambda i: (jnp.maximum(i * (R // H) - 1, 0), 0)),
                  pl.BlockSpec((CONV_W, CONV_CH), lambda i: (0, 0)), vec, vec, vec],
        out_specs=[pl.BlockSpec((R, CONV_CH), lambda i: (i, 0)), pl.BlockSpec((R, CONV_CH), lambda i: (i, 0))],
        out_shape=[jax.ShapeDtypeStruct((T, CONV_CH), F32), jax.ShapeDtypeStruct((T, D), BF16)],
        compiler_params=_params("parallel"),
    )(zc, zc, conv_k, conv_b, ln_g, ln_b)


def _econv_bwd_ln(cv, dcat, ln_g, ln_b):
    R = CONV_TILE

    def body(cv_ref, d_ref, g_ref, be_ref, dcv_ref, dg_ref, dbe_ref, dcb_ref):
        cv_t = cv_ref[...]
        mu = jnp.mean(cv_t, axis=-1, keepdims=True)
        xc = cv_t - mu
        rstd = lax.rsqrt(jnp.mean(xc * xc, axis=-1, keepdims=True) + EPS)
        xh = xc * rstd
        ln = xh * g_ref[...] + be_ref[...]
        sg = _sigmoid(ln)
        dln = d_ref[...] * (sg * (1.0 + ln * (1.0 - sg)))
        dxh = dln * g_ref[...]
        dcv = rstd * (dxh - jnp.mean(dxh, axis=-1, keepdims=True) - xh * jnp.mean(dxh * xh, axis=-1, keepdims=True))
        dcv_ref[...] = dcv

        @pl.when(pl.program_id(0) == 0)
        def _():
            dg_ref[...] = jnp.zeros_like(dg_ref)
            dbe_ref[...] = jnp.zeros_like(dbe_ref)
            dcb_ref[...] = jnp.zeros_like(dcb_ref)

        dg_ref[...] += jnp.sum(dln * xh, axis=0, keepdims=True)
        dbe_ref[...] += jnp.sum(dln, axis=0, keepdims=True)
        dcb_ref[...] += jnp.sum(dcv, axis=0, keepdims=True)

    vec = pl.BlockSpec((1, CONV_CH), lambda i: (0, 0))
    row = pl.BlockSpec((R, CONV_CH), lambda i: (i, 0))
    vshape = jax.ShapeDtypeStruct((1, CONV_CH), F32)
    return _pcall(
        body, name="econv_bwd_ln", grid=(T // R,),
        in_specs=[row, row, vec, vec], out_specs=[row, vec, vec, vec],
        out_shape=[jax.ShapeDtypeStruct((T, CONV_CH), F32), vshape, vshape, vshape],
        compiler_params=_params("arbitrary"),
    )(cv, dcat, ln_g, ln_b)


def _econv_bwd_conv(dcv, zc, conv_k):
    R, H = CONV_TILE, CONV_HALO
    last = T // R - 1

    def body(d_ref, dn_ref, z_ref, zh_ref, k_ref, dz_ref, dk_ref):
        i = pl.program_id(0)
        z = z_ref[...]
        a_lin = z[:, :CONV_CH]
        sg = _sigmoid(z[:, CONV_CH:])
        glu = a_lin * sg
        halo = _glu(zh_ref[...]) * (i > 0).astype(F32)
        win = jnp.concatenate([halo, glu], axis=0)
        dcv_t = d_ref[...]
        nxt = dn_ref[...] * (i < last).astype(F32)
        winb = jnp.concatenate([dcv_t, nxt], axis=0)

        @pl.when(i == 0)
        def _():
            dk_ref[...] = jnp.zeros_like(dk_ref)

        dglu = jnp.zeros((R, CONV_CH), F32)
        for j in range(CONV_W):
            off = H - (CONV_W - 1) + j
            dk_ref[j:j + 1, :] += jnp.sum(dcv_t * win[off:off + R, :], axis=0, keepdims=True)
            ob = CONV_W - 1 - j
            dglu = dglu + k_ref[j:j + 1, :] * winb[ob:ob + R, :]
        dz_ref[...] = jnp.concatenate([dglu * sg, dglu * a_lin * sg * (1.0 - sg)], axis=1).astype(BF16)

    return _pcall(
        body, name="econv_bwd_conv", grid=(T // R,),
        in_specs=[pl.BlockSpec((R, CONV_CH), lambda i: (i, 0)),
                  pl.BlockSpec((H, CONV_CH), lambda i: (jnp.minimum((i + 1) * (R // H), T // H - 1), 0)),
                  pl.BlockSpec((R, 2 * CONV_CH), lambda i: (i, 0)),
                  pl.BlockSpec((H, 2 * CONV_CH), lambda i: (jnp.maximum(i * (R // H) - 1, 0), 0)),
                  pl.BlockSpec((CONV_W, CONV_CH), lambda i: (0, 0))],
        out_specs=[pl.BlockSpec((R, 2 * CONV_CH), lambda i: (i, 0)), pl.BlockSpec((CONV_W, CONV_CH), lambda i: (0, 0))],
        out_shape=[jax.ShapeDtypeStruct((T, EVEN_IN), BF16), jax.ShapeDtypeStruct((CONV_W, CONV_CH), F32)],
        compiler_params=_params("arbitrary"),
    )(dcv, dcv, zc, zc, conv_k)


def _swap_halves(v):
    lane = lax.broadcasted_iota(jnp.int32, v.shape, 1)
    return jnp.where((lane % HEAD_DIM) < HEAD_DIM // 2, pltpu.roll(v, LANES - HEAD_DIM // 2, 1),
                     pltpu.roll(v, HEAD_DIM // 2, 1))


def _qkv_proj(hn, w_in, rope_c, rope_s, tm=512):
    tn = 4 * LANES

    def body(a_ref, b_ref, c_ref, s_ref, o_ref):
        j = pl.program_id(1)
        acc = _nn(a_ref[...], b_ref[...])
        for p in range(4):
            v = acc[:, p * LANES:(p + 1) * LANES]
            rot = v * c_ref[...] + _swap_halves(v) * s_ref[...]
            o_ref[p] = jnp.where(j < 6, rot, v)

    tab = pl.BlockSpec((tm, LANES), lambda i, j: (i, 0))
    return _pcall(
        body, name="qkv_proj", grid=(T // tm, 9),
        in_specs=[pl.BlockSpec((tm, D), lambda i, j: (i, 0)),
                  pl.BlockSpec((D, tn), lambda i, j: (0, j + (2 * CONV_CH) // tn)), tab, tab],
        out_specs=pl.BlockSpec((None, 4, tm, LANES), lambda i, j: (j, 0, i, 0)),
        out_shape=jax.ShapeDtypeStruct((9, 4, T, LANES), F32),
        compiler_params=_params("parallel", "parallel"),
    )(hn, w_in, rope_c, rope_s)


def _band_rows(start, d):
    if d == 1:
        return pl.ds(pl.multiple_of(start, BAND), BAND)
    return pl.ds(start, BAND, stride=d)


def _band_masks(n):
    row = lax.broadcasted_iota(jnp.int32, (BAND, BAND), 0)
    col = lax.broadcasted_iota(jnp.int32, (BAND, BAND), 1)
    no_prev = (n == 0).astype(jnp.int32) * (2 * BAND)
    return col <= row, col >= row + no_prev


def _attn_fwd(qkv, g):
    d = DILATIONS[g]
    nb = T // d // BAND

    def body(q_ref, k_ref, v_ref, o_ref, l_ref):
        lane_lo = lax.broadcasted_iota(jnp.int32, (BAND, LANES), 1) < HEAD_DIM

        def step(idx, carry):
            r = idx // nb
            n = idx % nb
            cur = _band_rows(n * (BAND * d) + r, d)
            prev = _band_rows(jnp.maximum(n - 1, 0) * (BAND * d) + r, d)
            q = q_ref[cur, :]
            kc = k_ref[cur, :].astype(BF16)
            vc = v_ref[cur, :].astype(BF16)
            kp = k_ref[prev, :].astype(BF16)
            vp = v_ref[prev, :].astype(BF16)
            mc, mp = _band_masks(n)
            outs, lses = [], []
            for h in range(2):
                hm = lane_lo if h == 0 else jnp.logical_not(lane_lo)
                qm = jnp.where(hm, q, 0.0).astype(BF16)
                sc = jnp.where(mc, _nt(qm, kc) * SCALE, NEG)
                sp = jnp.where(mp, _nt(qm, kp) * SCALE, NEG)
                mx = jnp.maximum(jnp.max(sc, axis=1, keepdims=True), jnp.max(sp, axis=1, keepdims=True))
                pc = jnp.exp(sc - mx)
                pp = jnp.exp(sp - mx)
                den = jnp.sum(pc, axis=1, keepdims=True) + jnp.sum(pp, axis=1, keepdims=True)
                outs.append((_nn(pc.astype(BF16), vc) + _nn(pp.astype(BF16), vp)) / den)
                lses.append(jnp.broadcast_to(mx + jnp.log(den), (BAND, LANES)))
            o_ref[cur, :] = jnp.where(lane_lo, outs[0], outs[1])
            l_ref[cur, :] = jnp.where(lane_lo, lses[0], lses[1])
            return carry

        lax.fori_loop(0, d * nb, step, 0)

    def slab(which):
        return pl.BlockSpec((None, None, T, LANES), lambda p: (which * 3 + g, p, 0, 0))

    out = pl.BlockSpec((None, T, LANES), lambda p: (p, 0, 0))
    shape = jax.ShapeDtypeStruct((4, T, LANES), F32)
    return _pcall(
        body, name=f"attn_fwd{g}", grid=(4,),
        in_specs=[slab(0), slab(1), slab(2)], out_specs=[out, out], out_shape=[shape, shape],
        compiler_params=_params("parallel"),
    )(qkv, qkv, qkv)


def _attn_merge(outs, lses, cat, tm=1024):
    def body(o0, o1, o2, l0, l1, l2, cat_in, cat_ref, att_ref, w0, w1, w2):
        del cat_in
        la, lb, lc = l0[...], l1[...], l2[...]
        mx = jnp.maximum(jnp.maximum(la, lb), lc)
        ea, eb, ec = jnp.exp(la - mx), jnp.exp(lb - mx), jnp.exp(lc - mx)
        inv = 1.0 / (ea + eb + ec)
        wa, wb, wc = ea * inv, eb * inv, ec * inv
        att = wa * o0[...] + wb * o1[...] + wc * o2[...]
        att_ref[...] = att
        cat_ref[...] = att.astype(BF16)
        w0[...] = wa
        w1[...] = wb
        w2[...] = wc

    slab = pl.BlockSpec((None, tm, LANES), lambda p, i: (p, i, 0))
    shape = jax.ShapeDtypeStruct((4, T, LANES), F32)
    return _pcall(
        body, name="attn_merge", grid=(4, T // tm),
        in_specs=[slab] * 6 + [pl.BlockSpec(memory_space=pl.ANY)],
        out_specs=[pl.BlockSpec((tm, LANES), lambda p, i: (i, CONV_CH // LANES + p)), slab, slab, slab, slab],
        out_shape=[jax.ShapeDtypeStruct((T, D), BF16), shape, shape, shape, shape],
        input_output_aliases={6: 0},
        compiler_params=_params("parallel", "parallel"),
    )(*outs, *lses, cat)


def _attn_bwd(qkv, lse, wgt, att, dcat, dqkv, g):
    d = DILATIONS[g]
    nb = T // d // BAND

    def body(q_ref, k_ref, v_ref, l_ref, w_ref, a_ref, da_ref, dq_in, o_ref):
        del dq_in
        lane = lax.broadcasted_iota(jnp.int32, (BAND, LANES), 1)
        lane_lo = lane < HEAD_DIM
        row = lax.broadcasted_iota(jnp.int32, (LANES, LANES), 0)
        same_head = ((row // HEAD_DIM) == (lane // HEAD_DIM)).astype(BF16)
        dq_ref, dk_ref, dv_ref = o_ref.at[0], o_ref.at[1], o_ref.at[2]
        dk_ref[...] = jnp.zeros((T, LANES), F32)
        dv_ref[...] = jnp.zeros((T, LANES), F32)

        def step(idx, carry):
            r = idx // nb
            n = idx % nb
            cur = _band_rows(n * (BAND * d) + r, d)
            prev = _band_rows(jnp.maximum(n - 1, 0) * (BAND * d) + r, d)
            q = q_ref[cur, :]
            kc = k_ref[cur, :].astype(BF16)
            vc = v_ref[cur, :].astype(BF16)
            kp = k_ref[prev, :].astype(BF16)
            vp = v_ref[prev, :].astype(BF16)
            lse_t = l_ref[cur, :]
            w_t = w_ref[cur, :]
            da = da_ref[cur, :]
            prod = da * a_ref[cur, :]
            hi = prod.astype(BF16)
            lo = (prod - hi.astype(F32)).astype(BF16)
            csum = _nn(hi, same_head) + _nn(lo, same_head)
            mc, mp = _band_masks(n)
            dqs = []
            dkc = jnp.zeros((BAND, LANES), F32)
            dkp = jnp.zeros((BAND, LANES), F32)
            dvc = jnp.zeros((BAND, LANES), F32)
            dvp = jnp.zeros((BAND, LANES), F32)
            for h in range(2):
                hm = lane_lo if h == 0 else jnp.logical_not(lane_lo)
                col0 = h * HEAD_DIM
                lse_h = lse_t[:, col0:col0 + 1]
                w_h = w_t[:, col0:col0 + 1]
                c_h = csum[:, col0:col0 + 1]
                qm = jnp.where(hm, q, 0.0).astype(BF16)
                dam = jnp.where(hm, da, 0.0).astype(BF16)
                pwc = w_h * jnp.exp(jnp.where(mc, _nt(qm, kc) * SCALE, NEG) - lse_h)
                pwp = w_h * jnp.exp(jnp.where(mp, _nt(qm, kp) * SCALE, NEG) - lse_h)
                dsc = (pwc * (_nt(dam, vc) - c_h) * SCALE).astype(BF16)
                dsp = (pwp * (_nt(dam, vp) - c_h) * SCALE).astype(BF16)
                dqs.append(_nn(dsc, kc) + _nn(dsp, kp))
                dkc = dkc + _tn(dsc, qm)
                dkp = dkp + _tn(dsp, qm)
                dvc = dvc + _tn(pwc.astype(BF16), dam)
                dvp = dvp + _tn(pwp.astype(BF16), dam)
            dq_ref[cur, :] = jnp.where(lane_lo, dqs[0], dqs[1])
            dk_ref[cur, :] += dkc
            dk_ref[prev, :] += dkp
            dv_ref[cur, :] += dvc
            dv_ref[prev, :] += dvp
            return carry

        lax.fori_loop(0, d * nb, step, 0)

    def slab(which):
        return pl.BlockSpec((None, None, T, LANES), lambda p: (which * 3 + g, p, 0, 0))

    per_pair = pl.BlockSpec((None, T, LANES), lambda p: (p, 0, 0))
    return _pcall(
        body, name=f"attn_bwd{g}", grid=(4,),
        in_specs=[slab(0), slab(1), slab(2), per_pair, per_pair, per_pair,
                  pl.BlockSpec((T, LANES), lambda p: (0, CONV_CH // LANES + p)),
                  pl.BlockSpec(memory_space=pl.ANY)],
        out_specs=pl.BlockSpec((None, 3, None, T, LANES), lambda p: (g, 0, p, 0, 0)),
        out_shape=jax.ShapeDtypeStruct((3, 3, 4, T, LANES), F32),
        input_output_aliases={7: 0},
        compiler_params=_params("parallel"),
    )(qkv, qkv, qkv, lse, wgt, att, dcat, dqkv)


def _rope_bwd(dqkv, rope_c, rope_s, dz):
    def body(d_ref, c_ref, s_ref, dz_in, o_ref):
        del dz_in
        w = pl.program_id(1)
        v = d_ref[...]
        rot = v * c_ref[...] + _swap_halves(v * s_ref[...])
        o_ref[...] = jnp.where(w < 2, rot, v).astype(BF16)

    tab = pl.BlockSpec((T, LANES), lambda g, w, p: (0, 0))
    return _pcall(
        body, name="rope_bwd", grid=(3, 3, 4),
        in_specs=[pl.BlockSpec((None, None, None, T, LANES), lambda g, w, p: (g, w, p, 0, 0)), tab, tab,
                  pl.BlockSpec(memory_space=pl.ANY)],
        out_specs=pl.BlockSpec((T, LANES), lambda g, w, p: (0, (2 * CONV_CH) // LANES + (w * 3 + g) * 4 + p)),
        out_shape=jax.ShapeDtypeStruct((T, EVEN_IN), BF16),
        input_output_aliases={3: 0},
        compiler_params=_params("parallel", "parallel", "parallel"),
    )(dqkv, rope_c, rope_s, dz)


ODD_TILE = 256
ODD_HALO = 8
GELU_C = 0.7978845608028654
GELU_A = 0.044715


def _gelu(x):
    return 0.5 * x * (1.0 + jnp.tanh(GELU_C * (x + GELU_A * x * x * x)))


def _gelu_grad(x):
    th = jnp.tanh(GELU_C * (x + GELU_A * x * x * x))
    return 0.5 * (1.0 + th) + 0.5 * x * (1.0 - th * th) * GELU_C * (1.0 + 3.0 * GELU_A * x * x)


def _tril():
    row = lax.broadcasted_iota(jnp.int32, (CHUNK, CHUNK), 0)
    col = lax.broadcasted_iota(jnp.int32, (CHUNK, CHUNK), 1)
    return (col <= row).astype(F32)


def _odd_parts(z, zh, i, k_ref, g_ref, be_ref, w_ref, bt_ref):
    R, H = ODD_TILE, ODD_HALO
    gb, gc, xs, uv = z[:, :512], z[:, 512:1024], z[:, 1024:1536], z[:, 1536:]
    halo = zh[:, 512:1024] * zh[:, 1024:1536] * (i > 0).astype(F32)
    win = jnp.concatenate([halo, gc * xs], axis=0)
    cv = jnp.zeros((R, 512), F32)
    for j in range(SCONV_W):
        off = H - (SCONV_W - 1) + j
        cv = cv + k_ref[j:j + 1, :] * win[off:off + R, :]
    ge = _gelu(uv)
    u, v = ge[:, :512], ge[:, 512:]
    mu = jnp.mean(v, axis=-1, keepdims=True)
    xc = v - mu
    rstd = lax.rsqrt(jnp.mean(xc * xc, axis=-1, keepdims=True) + EPS)
    xh = xc * rstd
    vn = xh * g_ref[...] + be_ref[...]
    tril = _tril()
    wms = [(w_ref[g] * tril).astype(BF16) for g in range(SG_GROUPS)]
    rows = []
    for ci in range(R // CHUNK):
        blocks = []
        for g in range(SG_GROUPS):
            blk = vn[ci * CHUNK:(ci + 1) * CHUNK, g * LANES:(g + 1) * LANES].astype(BF16)
            blocks.append(_nn(wms[g], blk) + bt_ref[:, g:g + 1])
        rows.append(jnp.concatenate(blocks, axis=1))
    vmix = jnp.concatenate(rows, axis=0)
    return gb, gc, xs, uv, win, cv, u, rstd, xh, vn, vmix, wms


def _odd_mid_fwd(z, conv_k, ln_g, ln_b, sg_w, sg_bt):
    R, H = ODD_TILE, ODD_HALO

    def body(z_ref, zh_ref, k_ref, g_ref, be_ref, w_ref, bt_ref, o_ref):
        i = pl.program_id(0)
        gb, _, _, _, _, cv, u, _, _, _, vmix, _ = _odd_parts(z_ref[...], zh_ref[...], i, k_ref, g_ref, be_ref, w_ref, bt_ref)
        o_ref[...] = jnp.concatenate([gb * cv, u * vmix], axis=1).astype(BF16)

    vec = pl.BlockSpec((1, 512), lambda i: (0, 0))
    return _pcall(
        body, name="odd_mid_fwd", grid=(T // R,),
        in_specs=[pl.BlockSpec((R, ODD_IN), lambda i: (i, 0)),
                  pl.BlockSpec((H, ODD_IN), lambda i: (jnp.maximum(i * (R // H) - 1, 0), 0)),
                  pl.BlockSpec((SCONV_W, 512), lambda i: (0, 0)), vec, vec,
                  pl.BlockSpec((SG_GROUPS, CHUNK, CHUNK), lambda i: (0, 0, 0)),
                  pl.BlockSpec((CHUNK, SG_GROUPS), lambda i: (0, 0))],
        out_specs=pl.BlockSpec((R, D), lambda i: (i, 0)),
        out_shape=jax.ShapeDtypeStruct((T, D), BF16),
        compiler_params=_params("parallel"),
    )(z, z, conv_k, ln_g, ln_b, sg_w, sg_bt)


def _odd_mid_bwd(z, dcat, conv_k, ln_g, ln_b, sg_w, sg_bt):
    R, H = ODD_TILE, ODD_HALO
    last = T // R - 1

    def body(z_ref, zh_ref, zn_ref, d_ref, dn_ref, k_ref, g_ref, be_ref, w_ref, bt_ref,
             dz_ref, dk_ref, dg_ref, dbe_ref, dw_ref, dbt_ref):
        i = pl.program_id(0)
        z = z_ref[...]
        gb, gc, xs, uv, win, cv, u, rstd, xh, vn, vmix, wms = _odd_parts(z, zh_ref[...], i, k_ref, g_ref, be_ref, w_ref, bt_ref)
        dcat_t = d_ref[...]
        dc, dd = dcat_t[:, :512], dcat_t[:, 512:]

        @pl.when(i == 0)
        def _():
            dk_ref[...] = jnp.zeros_like(dk_ref)
            dg_ref[...] = jnp.zeros_like(dg_ref)
            dbe_ref[...] = jnp.zeros_like(dbe_ref)
            dw_ref[...] = jnp.zeros_like(dw_ref)
            dbt_ref[...] = jnp.zeros_like(dbt_ref)

        dgb = dc * cv
        dcv = dc * gb
        nxt = dn_ref[:, :512] * zn_ref[:, :512] * (i < last).astype(F32)
        winb = jnp.concatenate([dcv, nxt], axis=0)
        dp = jnp.zeros((R, 512), F32)
        for j in range(SCONV_W):
            off = H - (SCONV_W - 1) + j
            dk_ref[j:j + 1, :] += jnp.sum(dcv * win[off:off + R, :], axis=0, keepdims=True)
            ob = SCONV_W - 1 - j
            dp = dp + k_ref[j:j + 1, :] * winb[ob:ob + R, :]
        dgc = dp * xs
        dxs = dp * gc
        du = dd * vmix
        dvmix = dd * u
        tril = _tril()
        rows = []
        for ci in range(R // CHUNK):
            blocks = []
            for g in range(SG_GROUPS):
                sl = (slice(ci * CHUNK, (ci + 1) * CHUNK), slice(g * LANES, (g + 1) * LANES))
                dblk = dvmix[sl]
                dblk16 = dblk.astype(BF16)
                blocks.append(_tn(wms[g], dblk16))
                dw_ref[g] += _nt(dblk16, vn[sl].astype(BF16)) * tril
                dbt_ref[:, g:g + 1] += jnp.sum(dblk, axis=1, keepdims=True)
            rows.append(jnp.concatenate(blocks, axis=1))
        dvn = jnp.concatenate(rows, axis=0)
        dg_ref[...] += jnp.sum(dvn * xh, axis=0, keepdims=True)
        dbe_ref[...] += jnp.sum(dvn, axis=0, keepdims=True)
        dxh = dvn * g_ref[...]
        dv = rstd * (dxh - jnp.mean(dxh, axis=-1, keepdims=True) - xh * jnp.mean(dxh * xh, axis=-1, keepdims=True))
        duv = jnp.concatenate([du, dv], axis=1) * _gelu_grad(uv)
        dz_ref[...] = jnp.concatenate([dgb, dgc, dxs, duv], axis=1).astype(BF16)

    vec = pl.BlockSpec((1, 512), lambda i: (0, 0))
    kspec = pl.BlockSpec((SCONV_W, 512), lambda i: (0, 0))
    wspec = pl.BlockSpec((SG_GROUPS, CHUNK, CHUNK), lambda i: (0, 0, 0))
    bspec = pl.BlockSpec((CHUNK, SG_GROUPS), lambda i: (0, 0))
    nxt_blk = lambda i: (jnp.minimum((i + 1) * (R // H), T // H - 1), 0)
    return _pcall(
        body, name="odd_mid_bwd", grid=(T // R,),
        in_specs=[pl.BlockSpec((R, ODD_IN), lambda i: (i, 0)),
                  pl.BlockSpec((H, ODD_IN), lambda i: (jnp.maximum(i * (R // H) - 1, 0), 0)),
                  pl.BlockSpec((H, ODD_IN), nxt_blk),
                  pl.BlockSpec((R, D), lambda i: (i, 0)),
                  pl.BlockSpec((H, D), nxt_blk),
                  kspec, vec, vec, wspec, bspec],
        out_specs=[pl.BlockSpec((R, ODD_IN), lambda i: (i, 0)), kspec, vec, vec, wspec, bspec],
        out_shape=[jax.ShapeDtypeStruct((T, ODD_IN), BF16), jax.ShapeDtypeStruct((SCONV_W, 512), F32),
                   jax.ShapeDtypeStruct((1, 512), F32), jax.ShapeDtypeStruct((1, 512), F32),
                   jax.ShapeDtypeStruct((SG_GROUPS, CHUNK, CHUNK), F32), jax.ShapeDtypeStruct((CHUNK, SG_GROUPS), F32)],
        compiler_params=_params("arbitrary"),
    )(z, z, z, dcat, dcat, conv_k, ln_g, ln_b, sg_w, sg_bt)


def _ffn_fwd(tag, h, g, w1, w2):
    hn = _rms_fwd(f"ffn{tag}_norm", h, g)

    def act(acc):
        r = jnp.maximum(acc, 0.0)
        return acc, r * r

    u, f = _mm(f"ffn{tag}_up", "nn", hn, w1, T, D_FF, D, (F32, BF16), epi=act)
    out = _mm(f"ffn{tag}_down", "nn", f, w2, T, D, D_FF, (F32,), epi=lambda acc, res: (acc + res,), extras=(h,))
    return out, (hn, u, f)


def _ffn_bwd(tag, h, g, w1, w2, saved, dout):
    hn, u, f = saved
    du = _mm(f"ffn{tag}_dact", "nt", dout, w2, T, D_FF, D, (BF16,),
             epi=lambda acc, uu: (acc * (2.0 * jnp.maximum(uu, 0.0)),), extras=(u,))
    dw2 = _mm(f"ffn{tag}_dw2", "tn", f, dout, D_FF, D, T, (BF16,))
    dw1 = _mm(f"ffn{tag}_dw1", "tn", hn, du, D, D_FF, T, (BF16,))
    dhn = _mm(f"ffn{tag}_dhn", "nt", du, w1, T, D, D_FF, (F32,))
    dh, dg = _rms_bwd(f"ffn{tag}_dnorm", h, dhn, g, dout)
    return dh, dg, dw1, dw2


def _rope_tables():
    half = HEAD_DIM // 2
    inv = 10000.0 ** (-jnp.arange(half, dtype=F32) / half)
    ang = jnp.arange(T, dtype=F32)[:, None] * inv[None, :]
    cos, sin = jnp.cos(ang), jnp.sin(ang)
    c = jnp.tile(jnp.concatenate([cos, cos], axis=1), (1, LANES // HEAD_DIM))
    s = jnp.tile(jnp.concatenate([-sin, sin], axis=1), (1, LANES // HEAD_DIM))
    return c, s


def _local_step(x, target, p):
    rope_c, rope_s = _rope_tables()
    grads = {}

    hn0 = _rms_fwd("mix0_norm", x, p["norm_mix_g0"])
    zc = _mm("even_in_conv", "nn", hn0, p["even_w_in"], T, 2 * CONV_CH, D, (F32,))
    qkv = _qkv_proj(hn0, p["even_w_in"], rope_c, rope_s)
    cv, cat0 = _econv_fwd(zc, p["even_conv_k"], p["even_conv_b"], p["even_ln_g"], p["even_ln_b"])
    att_parts = [_attn_fwd(qkv, g) for g in range(3)]
    outs = [a[0] for a in att_parts]
    lses = [a[1] for a in att_parts]
    cat0, att, w0, w1, w2 = _attn_merge(outs, lses, cat0)
    wgts = (w0, w1, w2)
    h1 = _mm("even_out", "nn", cat0, p["even_w_out"], T, D, D, (F32,), epi=lambda acc, res: (acc + res,), extras=(x,))
    h2, ffn0_saved = _ffn_fwd(0, h1, p["norm_ffn_g0"], p["ffn_w1_0"], p["ffn_w2_0"])

    hn1 = _rms_fwd("mix1_norm", h2, p["norm_mix_g1"])
    z1 = _mm("odd_in", "nn", hn1, p["odd_w_in"], T, ODD_IN, D, (F32,))
    cat1 = _odd_mid_fwd(z1, p["odd_conv_k"], p["odd_ln_g"], p["odd_ln_b"], p["odd_sg_w"], p["odd_sg_bt"])
    h3 = _mm("odd_out", "nn", cat1, p["odd_w_out"], T, D, D, (F32,), epi=lambda acc, res: (acc + res,), extras=(h2,))
    h4, ffn1_saved = _ffn_fwd(1, h3, p["norm_ffn_g1"], p["ffn_w1_1"], p["ffn_w2_1"])

    dh4, grads["final_g"], loss = _loss_head(h4, p["final_g"], target)

    dh3, grads["norm_ffn_g1"], grads["ffn_w1_1"], grads["ffn_w2_1"] = _ffn_bwd(
        1, h3, p["norm_ffn_g1"], p["ffn_w1_1"], p["ffn_w2_1"], ffn1_saved, dh4)
    dcat1 = _mm("odd_out_dx", "nt", dh3, p["odd_w_out"], T, D, D, (F32,))
    grads["odd_w_out"] = _mm("odd_out_dw", "tn", cat1, dh3, D, D, T, (BF16,))
    dz1, grads["odd_conv_k"], grads["odd_ln_g"], grads["odd_ln_b"], grads["odd_sg_w"], grads["odd_sg_bt"] = _odd_mid_bwd(
        z1, dcat1, p["odd_conv_k"], p["odd_ln_g"], p["odd_ln_b"], p["odd_sg_w"], p["odd_sg_bt"])
    grads["odd_w_in"] = _mm("odd_in_dw", "tn", hn1, dz1, D, ODD_IN, T, (BF16,))
    dhn1 = _mm("odd_in_dx", "nt", dz1, p["odd_w_in"], T, D, ODD_IN, (F32,))
    dh2, grads["norm_mix_g1"] = _rms_bwd("mix1_dnorm", h2, dhn1, p["norm_mix_g1"], dh3)

    dh1, grads["norm_ffn_g0"], grads["ffn_w1_0"], grads["ffn_w2_0"] = _ffn_bwd(
        0, h1, p["norm_ffn_g0"], p["ffn_w1_0"], p["ffn_w2_0"], ffn0_saved, dh2)
    dcat0 = _mm("even_out_dx", "nt", dh1, p["even_w_out"], T, D, D, (F32,))
    grads["even_w_out"] = _mm("even_out_dw", "tn", cat0, dh1, D, D, T, (BF16,))
    dcv, grads["even_ln_g"], grads["even_ln_b"], grads["even_conv_b"] = _econv_bwd_ln(
        cv, dcat0, p["even_ln_g"], p["even_ln_b"])
    dz0, grads["even_conv_k"] = _econv_bwd_conv(dcv, zc, p["even_conv_k"])
    dqkv = None
    for g in range(3):
        if dqkv is None:
            dqkv = lax.empty((3, 3, 4, T, LANES), F32)
        dqkv = _attn_bwd(qkv, lses[g], wgts[g], att, dcat0, dqkv, g)
    dz0 = _rope_bwd(dqkv, rope_c, rope_s, dz0)
    grads["even_w_in"] = _mm("even_in_dw", "tn", hn0, dz0, D, EVEN_IN, T, (BF16,))
    dhn0 = _mm("even_in_dx", "nt", dz0, p["even_w_in"], T, D, EVEN_IN, (F32,), tm=256)
    dx, grads["norm_mix_g0"] = _rms_bwd("mix0_dnorm", x, dhn0, p["norm_mix_g0"], dh1)
    return loss[0, 0], dx, grads


def _rowwise(name, fn, ins, out_dtypes, tm=256):
    rows, cols = ins[0].shape
    tm = tm if rows % tm == 0 else rows
    n_in = len(ins)

    def body(*refs):
        vals = fn(*[r[...] for r in refs[:n_in]])
        for o_ref, v in zip(refs[n_in:], vals):
            o_ref[...] = v.astype(o_ref.dtype)

    spec = pl.BlockSpec((tm, cols), lambda i: (i, 0))
    outs = _pcall(
        body, name=name, grid=(rows // tm,),
        in_specs=[spec] * n_in, out_specs=[spec] * len(out_dtypes),
        out_shape=[jax.ShapeDtypeStruct((rows, cols), dt) for dt in out_dtypes],
        compiler_params=_params("parallel"),
    )(*ins)
    return outs[0] if len(out_dtypes) == 1 else outs


def _sum_slots(name, stacked, out_dtype, tm=256):
    n, rows, cols = stacked.shape
    tm = min(tm, rows)

    def body(s_ref, o_ref):
        acc = s_ref[0].astype(F32)
        for k in range(1, n):
            acc = acc + s_ref[k].astype(F32)
        o_ref[...] = acc.astype(o_ref.dtype)

    return _pcall(
        body, name=name, grid=(rows // tm,),
        in_specs=[pl.BlockSpec((n, tm, cols), lambda i: (0, i, 0))],
        out_specs=pl.BlockSpec((tm, cols), lambda i: (i, 0)),
        out_shape=jax.ShapeDtypeStruct((rows, cols), out_dtype),
        compiler_params=_params("parallel"),
    )(stacked)


def _adamw(name, w, g, m, v):
    c1 = 1.0 - ADAM_B1 ** ADAM_STEP
    c2 = 1.0 - ADAM_B2 ** ADAM_STEP

    def fn(w_t, g_t, m_t, v_t):
        m_new = ADAM_B1 * m_t + (1.0 - ADAM_B1) * g_t
        v_new = ADAM_B2 * v_t + (1.0 - ADAM_B2) * (g_t * g_t)
        delta = -ADAM_LR * ((m_new / c1) / (jnp.sqrt(v_new / c2) + ADAM_EPS) + ADAM_WD * w_t)
        return delta, m_new, v_new

    return _rowwise(name, fn, (w, g, m, v), (F32, F32, F32))


class _Piece:
    def __init__(self, name, rows, cols, axis, src, src_row0):
        self.name, self.rows, self.cols, self.axis = name, rows, cols, axis
        self.width = (cols if axis == 1 else rows) // 4
        self.src, self.src_row0 = src, src_row0

    @property
    def full_shape(self):
        return (self.rows, self.cols)

    @property
    def half_shape(self):
        return (self.rows // 2, self.cols) if self.axis == 1 else (self.rows, self.cols // 2)

    @property
    def shard_half_shape(self):
        return (self.rows // 2, self.width) if self.axis == 1 else (self.width, self.cols // 2)

    def shard_whole(self, ref):
        n = self.rows if self.axis == 1 else self.width
        return ref.at[pl.ds(self.src_row0, n), :]

    def shard_half(self, ref, h):
        if self.axis == 1:
            return ref.at[pl.ds(self.src_row0 + h * (self.rows // 2), self.rows // 2), :]
        return ref.at[pl.ds(self.src_row0, self.width), pl.ds(h * (self.cols // 2), self.cols // 2)]

    def full_shard(self, ref, s):
        if self.axis == 1:
            return ref.at[:, pl.ds(s * self.width, self.width)]
        return ref.at[pl.ds(s * self.width, self.width), :]

    def full_shard_half(self, ref, s, h):
        if self.axis == 1:
            return ref.at[pl.ds(h * (self.rows // 2), self.rows // 2), pl.ds(s * self.width, self.width)]
        return ref.at[pl.ds(s * self.width, self.width), pl.ds(h * (self.cols // 2), self.cols // 2)]

    def full_half(self, ref, h):
        if self.axis == 1:
            return ref.at[pl.ds(h * (self.rows // 2), self.rows // 2), :]
        return ref.at[:, pl.ds(h * (self.cols // 2), self.cols // 2)]

    def half_shard(self, ref, s):
        return self.full_shard(ref, s)


PIECES = (
    _Piece("even_w_in", D, EVEN_IN, 1, 0, 0),
    _Piece("even_w_out", D, D, 0, 1, 0),
    _Piece("odd_w_in", D, ODD_IN, 1, 2, 0),
    _Piece("odd_w_out", D, D, 0, 3, 0),
    _Piece("ffn_w1_0", D, D_FF, 1, 4, 0),
    _Piece("ffn_w1_1", D, D_FF, 1, 4, D),
    _Piece("ffn_w2_0", D_FF, D, 0, 5, 0),
    _Piece("ffn_w2_1", D_FF, D, 0, 5, D_FF // 4),
)
N_PIECES = len(PIECES)
N_SHARD_OPERANDS = 6
ANY = pl.BlockSpec(memory_space=pl.ANY)
MESH = pl.DeviceIdType.MESH


def _mesh_place():
    x, y, c = lax.axis_index("x"), lax.axis_index("y"), lax.axis_index("c")
    chips = [(1 - x, y), (x, 1 - y), (1 - x, 1 - y)]
    return x, y, c, chips


def _remote(src, dst, send_sem, recv_sem, dev):
    return pltpu.make_async_remote_copy(src_ref=src, dst_ref=dst, send_sem=send_sem, recv_sem=recv_sem,
                                        device_id=dev, device_id_type=MESH)


def _gather_weights(shards):
    def body(*refs):
        srcs = refs[:N_SHARD_OPERANDS]
        fulls = refs[N_SHARD_OPERANDS:N_SHARD_OPERANDS + N_PIECES]
        send_sems, recv_sems, local_sems = refs[N_SHARD_OPERANDS + N_PIECES:]
        x, y, c, chips = _mesh_place()
        s = 2 * x + y
        sibling = (x, y, 1 - c)
        local = []
        for i, pc in enumerate(PIECES):
            cp = pltpu.make_async_copy(pc.shard_whole(srcs[pc.src]), pc.full_shard(fulls[i], s), local_sems.at[i])
            cp.start()
            local.append(cp)
        sends = []
        for k, (cx, cy) in enumerate(chips):
            for i, pc in enumerate(PIECES):
                cp = _remote(pc.shard_half(srcs[pc.src], c), pc.full_shard_half(fulls[i], s, c),
                             send_sems.at[k * N_PIECES + i], recv_sems.at[k * N_PIECES + i], (cx, cy, c))
                cp.start()
                sends.append(cp)
        for k, (cx, cy) in enumerate(chips):
            sk = 2 * cx + cy
            for i, pc in enumerate(PIECES):
                win = pc.full_shard_half(fulls[i], sk, c)
                _remote(win, win, send_sems.at[k * N_PIECES + i], recv_sems.at[k * N_PIECES + i], (cx, cy, c)).wait_recv()
                j = (3 + k) * N_PIECES + i
                cp = _remote(win, win, send_sems.at[j], recv_sems.at[j], sibling)
                cp.start()
                sends.append(cp)
        for k, (cx, cy) in enumerate(chips):
            sk = 2 * cx + cy
            for i, pc in enumerate(PIECES):
                win = pc.full_shard_half(fulls[i], sk, 1 - c)
                j = (3 + k) * N_PIECES + i
                _remote(win, win, send_sems.at[j], recv_sems.at[j], sibling).wait_recv()
        for cp in sends:
            cp.wait_send()
        for cp in local:
            cp.wait()

    return _pcall(
        body, name="gather_weights",
        in_specs=[ANY] * N_SHARD_OPERANDS, out_specs=[ANY] * N_PIECES,
        out_shape=[jax.ShapeDtypeStruct(pc.full_shape, BF16) for pc in PIECES],
        scratch_shapes=[pltpu.SemaphoreType.DMA((6 * N_PIECES,)), pltpu.SemaphoreType.DMA((6 * N_PIECES,)),
                        pltpu.SemaphoreType.DMA((N_PIECES,))],
    )(*shards)


def _reduce_cores(partials):
    def body(*refs):
        parts = refs[:N_PIECES]
        mine = refs[N_PIECES:2 * N_PIECES]
        theirs = refs[2 * N_PIECES:3 * N_PIECES]
        send_sems, recv_sems, local_sems = refs[3 * N_PIECES:]
        x, y, c, _ = _mesh_place()
        copies = []
        for i, pc in enumerate(PIECES):
            cp = pltpu.make_async_copy(pc.full_half(parts[i], c), mine[i], local_sems.at[i])
            cp.start()
            copies.append(cp)
            cp = _remote(pc.full_half(parts[i], 1 - c), theirs[i], send_sems.at[i], recv_sems.at[i], (x, y, 1 - c))
            cp.start()
            copies.append(cp)
        for cp in copies:
            cp.wait()

    shapes = [jax.ShapeDtypeStruct(pc.half_shape, BF16) for pc in PIECES]
    outs = _pcall(
        body, name="reduce_cores",
        in_specs=[ANY] * N_PIECES, out_specs=[ANY] * (2 * N_PIECES), out_shape=shapes + shapes,
        scratch_shapes=[pltpu.SemaphoreType.DMA((N_PIECES,)), pltpu.SemaphoreType.DMA((N_PIECES,)),
                        pltpu.SemaphoreType.DMA((N_PIECES,))],
    )(*partials)
    return outs[:N_PIECES], outs[N_PIECES:]


def _scatter_chips(chip_sums):
    def body(*refs):
        sums = refs[:N_PIECES]
        stacks = refs[N_PIECES:2 * N_PIECES]
        send_sems, recv_sems, local_sems = refs[2 * N_PIECES:]
        x, y, c, chips = _mesh_place()
        s = 2 * x + y
        copies = []
        for i, pc in enumerate(PIECES):
            cp = pltpu.make_async_copy(pc.half_shard(sums[i], s), stacks[i].at[3], local_sems.at[i])
            cp.start()
            copies.append(cp)
        for k, (cx, cy) in enumerate(chips):
            for i, pc in enumerate(PIECES):
                cp = _remote(pc.half_shard(sums[i], 2 * cx + cy), stacks[i].at[k],
                             send_sems.at[k * N_PIECES + i], recv_sems.at[k * N_PIECES + i], (cx, cy, c))
                cp.start()
                copies.append(cp)
        for cp in copies:
            cp.wait()

    return _pcall(
        body, name="scatter_chips",
        in_specs=[ANY] * N_PIECES, out_specs=[ANY] * N_PIECES,
        out_shape=[jax.ShapeDtypeStruct((4,) + pc.shard_half_shape, BF16) for pc in PIECES],
        scratch_shapes=[pltpu.SemaphoreType.DMA((3 * N_PIECES,)), pltpu.SemaphoreType.DMA((3 * N_PIECES,)),
                        pltpu.SemaphoreType.DMA((N_PIECES,))],
    )(*chip_sums)


SHARD_OPERAND_SHAPES = ((D, EVEN_IN // 4), (D // 4, D), (D, ODD_IN // 4), (D // 4, D), (2 * D, D_FF // 4), (2 * D_FF // 4, D))


def _join_cores(final_halves):
    def body(*refs):
        halves = refs[:N_PIECES]
        outs = refs[N_PIECES:N_PIECES + N_SHARD_OPERANDS]
        send_sems, recv_sems, local_sems = refs[N_PIECES + N_SHARD_OPERANDS:]
        x, y, c, _ = _mesh_place()
        copies = []
        for i, pc in enumerate(PIECES):
            dst = pc.shard_half(outs[pc.src], c)
            cp = pltpu.make_async_copy(halves[i], dst, local_sems.at[i])
            cp.start()
            copies.append(cp)
            cp = _remote(halves[i], dst, send_sems.at[i], recv_sems.at[i], (x, y, 1 - c))
            cp.start()
            copies.append(cp)
        for cp in copies:
            cp.wait()

    return _pcall(
        body, name="join_cores",
        in_specs=[ANY] * N_PIECES, out_specs=[ANY] * N_SHARD_OPERANDS,
        out_shape=[jax.ShapeDtypeStruct(sh, F32) for sh in SHARD_OPERAND_SHAPES],
        scratch_shapes=[pltpu.SemaphoreType.DMA((N_PIECES,)), pltpu.SemaphoreType.DMA((N_PIECES,)),
                        pltpu.SemaphoreType.DMA((N_PIECES,))],
    )(*final_halves)


def _allgather8(name, blk, with_sum):
    m = blk.shape[0]

    def body(x_ref, out_ref, *rest):
        if with_sum:
            sum_ref, send_sems, recv_sems, local_sem = rest
        else:
            send_sems, recv_sems, local_sem = rest
        x, y, c, chips = _mesh_place()
        me, sibling = (x, y, c), (x, y, 1 - c)

        def rows(px, py, pc):
            return out_ref.at[pl.ds((4 * px + 2 * py + pc) * m, m), :]

        def copy(k, block, to, src=None):
            return _remote(rows(*block) if src is None else src, rows(*block), send_sems.at[k], recv_sems.at[k], to)

        mine = pltpu.make_async_copy(x_ref, rows(*me), local_sem)
        mine.start()
        first = [copy(0, me, sibling, src=x_ref)]
        first += [copy(1 + j, me, (*chip, c), src=x_ref) for j, chip in enumerate(chips)]
        for cp in first:
            cp.start()
        passed = [copy(4 + j, (*chip, c), sibling) for j, chip in enumerate(chips)]
        for j, chip in enumerate(chips):
            copy(1 + j, (*chip, c), me).wait_recv()
            passed[j].start()
        copy(0, sibling, me).wait_recv()
        for j, chip in enumerate(chips):
            copy(4 + j, (*chip, 1 - c), me).wait_recv()
        for cp in first + passed:
            cp.wait_send()
        mine.wait()
        if with_sum:
            acc = out_ref[0:m, :]
            for dev in range(1, 8):
                acc = acc + out_ref[dev * m:(dev + 1) * m, :]
            sum_ref[...] = acc

    vm = pl.BlockSpec(memory_space=pltpu.VMEM)
    out_shape = [jax.ShapeDtypeStruct((8 * m, LANES), F32)]
    if with_sum:
        out_shape.append(jax.ShapeDtypeStruct((m, LANES), F32))
    return _pcall(
        body, name=name, in_specs=[vm], out_specs=[vm] * len(out_shape), out_shape=out_shape,
        scratch_shapes=[pltpu.SemaphoreType.DMA((7,)), pltpu.SemaphoreType.DMA((7,)), pltpu.SemaphoreType.DMA],
    )(blk)


def _pack(arrays, row_counts):
    rows = []
    for a, n in zip(arrays, row_counts):
        flat = a.reshape(-1, LANES)
        if flat.shape[0] < n:
            flat = jnp.concatenate([flat, jnp.zeros((n - flat.shape[0], LANES), F32)], axis=0)
        rows.append(flat)
    return jnp.concatenate(rows, axis=0)


def _unpack(buf, shapes, row_counts):
    out, r0 = [], 0
    for sh, n in zip(shapes, row_counts):
        size = 1
        for dim in sh:
            size *= dim
        out.append(buf[r0:r0 + size // LANES].reshape(sh))
        r0 += n
    return out


REPL_NAMES = ("norm_mix_g", "norm_ffn_g", "even_conv_b", "even_ln_g", "even_ln_b", "odd_sg_w", "odd_sg_b", "final_g")
REPL_SHAPES = ((2, D), (2, D), (1, 512), (1, 512), (1, 512), (1, SG_GROUPS, CHUNK, CHUNK), (1, SG_GROUPS, CHUNK), (D,))
REPL_ROWS = (16, 16, 4, 4, 4, 512, 4, 8)
SHARDED_NAMES = ("even_conv_k", "odd_conv_k", "odd_ln_g", "odd_ln_b")
SHARDED_SHARD_SHAPES = ((1, CONV_W, LANES), (1, SCONV_W, LANES), (1, LANES), (1, LANES))
SHARDED_SHARD_ROWS = (32, 8, 4, 4)
SHARDED_FULL_SHAPES = ((CONV_W, 512), (SCONV_W, 512), (1, 512), (1, 512))
SHARDED_FULL_ROWS = (124, 12, 4, 4)


def kernel(x, norm_mix_g, norm_ffn_g, even_w_in, even_conv_k, even_conv_b, even_ln_g, even_ln_b, even_w_out, odd_w_in, odd_conv_k, odd_ln_g, odd_ln_b, odd_sg_w, odd_sg_b, odd_w_out, ffn_w1, ffn_w2, final_g, loss_target, m_norm_mix_g, m_norm_ffn_g, m_even_w_in, m_even_conv_k, m_even_conv_b, m_even_ln_g, m_even_ln_b, m_even_w_out, m_odd_w_in, m_odd_conv_k, m_odd_ln_g, m_odd_ln_b, m_odd_sg_w, m_odd_sg_b, m_odd_w_out, m_ffn_w1, m_ffn_w2, m_final_g, v_norm_mix_g, v_norm_ffn_g, v_even_w_in, v_even_conv_k, v_even_conv_b, v_even_ln_g, v_even_ln_b, v_even_w_out, v_odd_w_in, v_odd_conv_k, v_odd_ln_g, v_odd_ln_b, v_odd_sg_w, v_odd_sg_b, v_odd_w_out, v_ffn_w1, v_ffn_w2, v_final_g):
    names = ("norm_mix_g", "norm_ffn_g", "even_w_in", "even_conv_k", "even_conv_b", "even_ln_g", "even_ln_b", "even_w_out",
             "odd_w_in", "odd_conv_k", "odd_ln_g", "odd_ln_b", "odd_sg_w", "odd_sg_b", "odd_w_out", "ffn_w1", "ffn_w2", "final_g")
    w = dict(zip(names, (norm_mix_g, norm_ffn_g, even_w_in, even_conv_k, even_conv_b, even_ln_g, even_ln_b, even_w_out,
                         odd_w_in, odd_conv_k, odd_ln_g, odd_ln_b, odd_sg_w, odd_sg_b, odd_w_out, ffn_w1, ffn_w2, final_g)))
    mom = dict(zip(names, (m_norm_mix_g, m_norm_ffn_g, m_even_w_in, m_even_conv_k, m_even_conv_b, m_even_ln_g, m_even_ln_b,
                           m_even_w_out, m_odd_w_in, m_odd_conv_k, m_odd_ln_g, m_odd_ln_b, m_odd_sg_w, m_odd_sg_b, m_odd_w_out,
                           m_ffn_w1, m_ffn_w2, m_final_g)))
    vel = dict(zip(names, (v_norm_mix_g, v_norm_ffn_g, v_even_w_in, v_even_conv_k, v_even_conv_b, v_even_ln_g, v_even_ln_b,
                           v_even_w_out, v_odd_w_in, v_odd_conv_k, v_odd_ln_g, v_odd_ln_b, v_odd_sg_w, v_odd_sg_b, v_odd_w_out,
                           v_ffn_w1, v_ffn_w2, v_final_g)))
    big_names = ("even_w_in", "even_w_out", "odd_w_in", "odd_w_out", "ffn_w1", "ffn_w2")
    chip = 2 * lax.axis_index("x") + lax.axis_index("y")

    def shard2d(t, name):
        return t[name].reshape(SHARD_OPERAND_SHAPES[big_names.index(name)])

    shards16 = [_rowwise(f"cast_{n}", lambda a: (a,), (shard2d(w, n),), (BF16,)) for n in big_names]
    full = dict(zip((pc.name for pc in PIECES), _gather_weights(shards16)))
    small_pack = _pack([w[n] for n in SHARDED_NAMES], SHARDED_SHARD_ROWS)
    gathered = _allgather8("gather_small", small_pack, False)[0].reshape(4, 2, sum(SHARDED_SHARD_ROWS), LANES)[:, 0]
    r0 = 0
    for n, sh, rows, full_sh in zip(SHARDED_NAMES, SHARDED_SHARD_SHAPES, SHARDED_SHARD_ROWS, SHARDED_FULL_SHAPES):
        per_chip = gathered[:, r0:r0 + rows].reshape(4, -1)[:, :full_sh[0] * LANES].reshape(4, full_sh[0], LANES)
        full[n] = jnp.transpose(per_chip, (1, 0, 2)).reshape(full_sh)
        r0 += rows
    p = dict(full)
    p.update(norm_mix_g0=norm_mix_g[0:1], norm_mix_g1=norm_mix_g[1:2], norm_ffn_g0=norm_ffn_g[0:1], norm_ffn_g1=norm_ffn_g[1:2],
             even_conv_b=even_conv_b, even_ln_g=even_ln_g, even_ln_b=even_ln_b,
             odd_sg_w=odd_sg_w[0], odd_sg_bt=odd_sg_b[0].T, final_g=final_g[None, :])

    loss, dx, g = _local_step(x[0], loss_target[0], p)
    loss = lax.psum(loss, ("x", "y", "c"))

    mine, theirs = _reduce_cores([g[pc.name] for pc in PIECES])
    chip_sums = [_rowwise(f"chipsum_{pc.name}", lambda a, b: (a.astype(F32) + b.astype(F32),), (mine[i], theirs[i]), (BF16,))
                 for i, pc in enumerate(PIECES)]
    stacks = _scatter_chips(chip_sums)
    finals = [_sum_slots(f"allsum_{pc.name}", stacks[i], F32) for i, pc in enumerate(PIECES)]
    big_grads = dict(zip(big_names, _join_cores(finals)))

    g_small = {"norm_mix_g": jnp.concatenate([g["norm_mix_g0"], g["norm_mix_g1"]], axis=0),
               "norm_ffn_g": jnp.concatenate([g["norm_ffn_g0"], g["norm_ffn_g1"]], axis=0),
               "even_conv_b": g["even_conv_b"], "even_ln_g": g["even_ln_g"], "even_ln_b": g["even_ln_b"],
               "odd_sg_w": g["odd_sg_w"], "odd_sg_b": g["odd_sg_bt"].T, "final_g": g["final_g"],
               "even_conv_k": g["even_conv_k"], "odd_conv_k": g["odd_conv_k"], "odd_ln_g": g["odd_ln_g"], "odd_ln_b": g["odd_ln_b"]}
    grad_pack = _pack([g_small[n] for n in REPL_NAMES + SHARDED_NAMES], REPL_ROWS + SHARDED_FULL_ROWS)
    grad_sum = _allgather8("allreduce_small", grad_pack, True)[1]
    parts = _unpack(grad_sum, REPL_SHAPES + SHARDED_FULL_SHAPES, REPL_ROWS + SHARDED_FULL_ROWS)
    grads = dict(zip(REPL_NAMES, parts[:len(REPL_NAMES)]))
    for n, full_g, sh in zip(SHARDED_NAMES, parts[len(REPL_NAMES):], SHARDED_SHARD_SHAPES):
        grads[n] = lax.dynamic_slice_in_dim(full_g, chip * LANES, LANES, axis=1).reshape(sh)
    for n in big_names:
        grads[n] = big_grads[n].reshape(w[n].shape)

    delta, new_m, new_v = {}, {}, {}
    for n in big_names:
        d2, m2, v2 = _adamw(f"adamw_{n}", shard2d(w, n), big_grads[n], shard2d(mom, n), shard2d(vel, n))
        delta[n], new_m[n], new_v[n] = (t.reshape(w[n].shape) for t in (d2, m2, v2))
    for tag, group, rows, shapes in (("repl", REPL_NAMES, REPL_ROWS, [w[n].shape for n in REPL_NAMES]),
                                     ("sharded", SHARDED_NAMES, SHARDED_SHARD_ROWS, SHARDED_SHARD_SHAPES)):
        packs = [_pack([t[n] for n in group], rows) for t in (w, grads, mom, vel)]
        outs = _adamw(f"adamw_{tag}", *packs)
        for res, o in zip((delta, new_m, new_v), outs):
            res.update(zip(group, _unpack(o, shapes, rows)))

    out = [loss, dx[None]]
    for res in (grads, delta, new_m, new_v):
        out.extend(res[n] for n in names)
    return tuple(out)
```

```python
import functools

import jax
import jax.numpy as jnp
from jax import lax
from jax.experimental import pallas as pl
from jax.experimental.pallas import tpu as pltpu

F32 = jnp.float32
BF16 = jnp.bfloat16

T = 2048
D = 1024
CONV_CH = 512
CONV_W = 31
HEAD_DIM = 64
ATT_W = 1536
EVEN_IN = 5632
ODD_IN = 2560
SCONV_W = 3
SG_GROUPS = 4
CHUNK = 128
D_FF = 4096
EPS = 1e-6
DILATIONS = (1, 4, 16)
BAND = 128
SCALE = HEAD_DIM ** -0.5
NEG = -1e30

ADAM_LR = 0.001
ADAM_B1 = 0.9
ADAM_B2 = 0.999
ADAM_EPS = 1e-08
ADAM_WD = 0.01
ADAM_STEP = 10

V7X_VMEM_BYTES = 64 * 2 ** 20
VMEM_LIMIT = V7X_VMEM_BYTES - 8 * 2 ** 20
LANES = 128


def _pcall(body, **kw):
    return pl.pallas_call(body, **kw)


def _params(*sem):
    return pltpu.CompilerParams(dimension_semantics=sem, vmem_limit_bytes=VMEM_LIMIT)


def _dot(a, b, dims):
    return lax.dot_general(a, b, (dims, ((), ())), preferred_element_type=F32)


def _nn(a, b):
    return _dot(a, b, ((1,), (0,)))


def _nt(a, b):
    return _dot(a, b, ((1,), (1,)))


def _tn(a, b):
    return _dot(a, b, ((0,), (0,)))


def _sigmoid(x):
    return 1.0 / (1.0 + jnp.exp(-x))


def _mm(name, mode, a, b, m, n, k, out_dtypes, *, tm=512, tn=512, b_off=0, extras=(), epi=None):
    if mode == "nn":
        a_spec = pl.BlockSpec((tm, k), lambda i, j: (i, 0))
        b_spec = pl.BlockSpec((k, tn), lambda i, j: (0, j + b_off))
        dims = ((1,), (0,))
    elif mode == "nt":
        a_spec = pl.BlockSpec((tm, k), lambda i, j: (i, 0))
        b_spec = pl.BlockSpec((tn, k), lambda i, j: (j, 0))
        dims = ((1,), (1,))
    else:
        a_spec = pl.BlockSpec((k, tm), lambda i, j: (0, i))
        b_spec = pl.BlockSpec((k, tn), lambda i, j: (0, j))
        dims = ((0,), (0,))
    o_spec = pl.BlockSpec((tm, tn), lambda i, j: (i, j))
    n_extra = len(extras)

    def body(a_ref, b_ref, *rest):
        acc = _dot(a_ref[...].astype(BF16), b_ref[...].astype(BF16), dims)
        vals = epi(acc, *[e[...] for e in rest[:n_extra]]) if epi is not None else (acc,)
        for o_ref, v in zip(rest[n_extra:], vals):
            o_ref[...] = v.astype(o_ref.dtype)

    outs = _pcall(
        body, name=name, grid=(m // tm, n // tn),
        in_specs=[a_spec, b_spec] + [o_spec] * n_extra,
        out_specs=[o_spec] * len(out_dtypes),
        out_shape=[jax.ShapeDtypeStruct((m, n), dt) for dt in out_dtypes],
        compiler_params=_params("parallel", "parallel"),
    )(a, b, *extras)
    return outs[0] if len(out_dtypes) == 1 else outs


def _rms_fwd(name, h, g, tm=512):
    def body(h_ref, g_ref, o_ref):
        x = h_ref[...]
        r = lax.rsqrt(jnp.mean(x * x, axis=-1, keepdims=True) + EPS)
        o_ref[...] = ((x * r) * g_ref[...]).astype(BF16)

    return _pcall(
        body, name=name, grid=(T // tm,),
        in_specs=[pl.BlockSpec((tm, D), lambda i: (i, 0)), pl.BlockSpec((1, D), lambda i: (0, 0))],
        out_specs=pl.BlockSpec((tm, D), lambda i: (i, 0)),
        out_shape=jax.ShapeDtypeStruct((T, D), BF16),
        compiler_params=_params("parallel"),
    )(h, g)


def _rms_bwd(name, h, dhn, g, dres, tm=512):
    def body(h_ref, d_ref, g_ref, r_ref, dh_ref, dg_ref):
        x = h_ref[...]
        r = lax.rsqrt(jnp.mean(x * x, axis=-1, keepdims=True) + EPS)
        nrm = x * r
        dy = d_ref[...]
        dn = dy * g_ref[...]
        dh_ref[...] = r_ref[...] + r * (dn - nrm * jnp.mean(dn * nrm, axis=-1, keepdims=True))

        @pl.when(pl.program_id(0) == 0)
        def _():
            dg_ref[...] = jnp.zeros_like(dg_ref)

        dg_ref[...] += jnp.sum(dy * nrm, axis=0, keepdims=True)

    row = pl.BlockSpec((tm, D), lambda i: (i, 0))
    vec = pl.BlockSpec((1, D), lambda i: (0, 0))
    return _pcall(
        body, name=name, grid=(T // tm,),
        in_specs=[row, row, vec, row], out_specs=[row, vec],
        out_shape=[jax.ShapeDtypeStruct((T, D), F32), jax.ShapeDtypeStruct((1, D), F32)],
        compiler_params=_params("arbitrary"),
    )(h, dhn, g, dres)


def _loss_head(h, g, target, tm=512):
    def body(h_ref, g_ref, t_ref, dh_ref, dg_ref, loss_ref):
        x = h_ref[...]
        r = lax.rsqrt(jnp.mean(x * x, axis=-1, keepdims=True) + EPS)
        nrm = x * r
        gain = g_ref[...]
        err = nrm * gain - t_ref[...]
        dy = err * (1.0 / D)
        dn = dy * gain
        dh_ref[...] = r * (dn - nrm * jnp.mean(dn * nrm, axis=-1, keepdims=True))

        @pl.when(pl.program_id(0) == 0)
        def _():
            dg_ref[...] = jnp.zeros_like(dg_ref)
            loss_ref[...] = jnp.zeros_like(loss_ref)

        dg_ref[...] += jnp.sum(dy * nrm, axis=0, keepdims=True)
        part = jnp.sum(jnp.sum(err * err, axis=1, keepdims=True), axis=0, keepdims=True) * (0.5 / D)
        loss_ref[...] += jnp.broadcast_to(part, (1, LANES))

    row = pl.BlockSpec((tm, D), lambda i: (i, 0))
    vec = pl.BlockSpec((1, D), lambda i: (0, 0))
    return _pcall(
        body, name="loss_head", grid=(T // tm,),
        in_specs=[row, vec, row], out_specs=[row, vec, pl.BlockSpec((1, LANES), lambda i: (0, 0))],
        out_shape=[jax.ShapeDtypeStruct((T, D), F32), jax.ShapeDtypeStruct((1, D), F32),
                   jax.ShapeDtypeStruct((1, LANES), F32)],
        compiler_params=_params("arbitrary"),
    )(h, g, target)


CONV_TILE = 256
CONV_HALO = 32


def _glu(z):
    return z[:, :CONV_CH] * _sigmoid(z[:, CONV_CH:])


def _econv_fwd(zc, conv_k, conv_b, ln_g, ln_b):
    R, H = CONV_TILE, CONV_HALO

    def body(z_ref, zh_ref, k_ref, b_ref, g_ref, be_ref, cv_ref, cat_ref):
        i = pl.program_id(0)
        glu = _glu(z_ref[...])
        halo = _glu(zh_ref[...]) * (i > 0).astype(F32)
        win = jnp.concatenate([halo, glu], axis=0)
        acc = jnp.zeros((R, CONV_CH), F32) + b_ref[...]
        for j in range(CONV_W):
            off = H - (CONV_W - 1) + j
            acc = acc + k_ref[j:j + 1, :] * win[off:off + R, :]
        cv_ref[...] = acc
        mu = jnp.mean(acc, axis=-1, keepdims=True)
        xc = acc - mu
        rstd = lax.rsqrt(jnp.mean(xc * xc, axis=-1, keepdims=True) + EPS)
        ln = xc * rstd * g_ref[...] + be_ref[...]
        cat_ref[...] = (ln * _sigmoid(ln)).astype(BF16)

    vec = pl.BlockSpec((1, CONV_CH), lambda i: (0, 0))
    return _pcall(
        body, name="econv_fwd", grid=(T // R,),
        in_specs=[pl.BlockSpec((R, 2 * CONV_CH), lambda i: (i, 0)),
                  pl.BlockSpec((H, 2 * CONV_CH), lambda i: (jnp.maximum(i * (R // H) - 1, 0), 0)),
                  pl.BlockSpec((CONV_W, CONV_CH), lambda i: (0, 0)), vec, vec, vec],
        out_specs=[pl.BlockSpec((R, CONV_CH), lambda i: (i, 0)), pl.BlockSpec((R, CONV_CH), lambda i: (i, 0))],
        out_shape=[jax.ShapeDtypeStruct((T, CONV_CH), F32), jax.ShapeDtypeStruct((T, D), BF16)],
        compiler_params=_params("parallel"),
    )(zc, zc, conv_k, conv_b, ln_g, ln_b)


def _econv_bwd_ln(cv, dcat, ln_g, ln_b):
    R = CONV_TILE

    def body(cv_ref, d_ref, g_ref, be_ref, dcv_ref, dg_ref, dbe_ref, dcb_ref):
        cv_t = cv_ref[...]
        mu = jnp.mean(cv_t, axis=-1, keepdims=True)
        xc = cv_t - mu
        rstd = lax.rsqrt(jnp.mean(xc * xc, axis=-1, keepdims=True) + EPS)
        xh = xc * rstd
        ln = xh * g_ref[...] + be_ref[...]
        sg = _sigmoid(ln)
        dln = d_ref[...] * (sg * (1.0 + ln * (1.0 - sg)))
        dxh = dln * g_ref[...]
        dcv = rstd * (dxh - jnp.mean(dxh, axis=-1, keepdims=True) - xh * jnp.mean(dxh * xh, axis=-1, keepdims=True))
        dcv_ref[...] = dcv

        @pl.when(pl.program_id(0) == 0)
        def _():
            dg_ref[...] = jnp.zeros_like(dg_ref)
            dbe_ref[...] = jnp.zeros_like(dbe_ref)
            dcb_ref[...] = jnp.zeros_like(dcb_ref)

        dg_ref[...] += jnp.sum(dln * xh, axis=0, keepdims=True)
        dbe_ref[...] += jnp.sum(dln, axis=0, keepdims=True)
        dcb_ref[...] += jnp.sum(dcv, axis=0, keepdims=True)

    vec = pl.BlockSpec((1, CONV_CH), lambda i: (0, 0))
    row = pl.BlockSpec((R, CONV_CH), lambda i: (i, 0))
    vshape = jax.ShapeDtypeStruct((1, CONV_CH), F32)
    return _pcall(
        body, name="econv_bwd_ln", grid=(T // R,),
        in_specs=[row, row, vec, vec], out_specs=[row, vec, vec, vec],
        out_shape=[jax.ShapeDtypeStruct((T, CONV_CH), F32), vshape, vshape, vshape],
        compiler_params=_params("arbitrary"),
    )(cv, dcat, ln_g, ln_b)


def _econv_bwd_conv(dcv, zc, conv_k):
    R, H = CONV_TILE, CONV_HALO
    last = T // R - 1

    def body(d_ref, dn_ref, z_ref, zh_ref, k_ref, dz_ref, dk_ref):
        i = pl.program_id(0)
        z = z_ref[...]
        a_lin = z[:, :CONV_CH]
        sg = _sigmoid(z[:, CONV_CH:])
        glu = a_lin * sg
        halo = _glu(zh_ref[...]) * (i > 0).astype(F32)
        win = jnp.concatenate([halo, glu], axis=0)
        dcv_t = d_ref[...]
        nxt = dn_ref[...] * (i < last).astype(F32)
        winb = jnp.concatenate([dcv_t, nxt], axis=0)

        @pl.when(i == 0)
        def _():
            dk_ref[...] = jnp.zeros_like(dk_ref)

        dglu = jnp.zeros((R, CONV_CH), F32)
        for j in range(CONV_W):
            off = H - (CONV_W - 1) + j
            dk_ref[j:j + 1, :] += jnp.sum(dcv_t * win[off:off + R, :], axis=0, keepdims=True)
            ob = CONV_W - 1 - j
            dglu = dglu + k_ref[j:j + 1, :] * winb[ob:ob + R, :]
        dz_ref[...] = jnp.concatenate([dglu * sg, dglu * a_lin * sg * (1.0 - sg)], axis=1).astype(BF16)

    return _pcall(
        body, name="econv_bwd_conv", grid=(T // R,),
        in_specs=[pl.BlockSpec((R, CONV_CH), lambda i: (i, 0)),
                  pl.BlockSpec((H, CONV_CH), lambda i: (jnp.minimum((i + 1) * (R // H), T // H - 1), 0)),
                  pl.BlockSpec((R, 2 * CONV_CH), lambda i: (i, 0)),
                  pl.BlockSpec((H, 2 * CONV_CH), lambda i: (jnp.maximum(i * (R // H) - 1, 0), 0)),
                  pl.BlockSpec((CONV_W, CONV_CH), lambda i: (0, 0))],
        out_specs=[pl.BlockSpec((R, 2 * CONV_CH), lambda i: (i, 0)), pl.BlockSpec((CONV_W, CONV_CH), lambda i: (0, 0))],
        out_shape=[jax.ShapeDtypeStruct((T, EVEN_IN), BF16), jax.ShapeDtypeStruct((CONV_W, CONV_CH), F32)],
        compiler_params=_params("arbitrary"),
    )(dcv, dcv, zc, zc, conv_k)


def _swap_halves(v):
    lane = lax.broadcasted_iota(jnp.int32, v.shape, 1)
    return jnp.where((lane % HEAD_DIM) < HEAD_DIM // 2, pltpu.roll(v, LANES - HEAD_DIM // 2, 1),
                     pltpu.roll(v, HEAD_DIM // 2, 1))


def _qkv_proj(hn, w_in, rope_c, rope_s, tm=512):
    tn = 4 * LANES

    def body(a_ref, b_ref, c_ref, s_ref, o_ref):
        j = pl.program_id(1)
        acc = _nn(a_ref[...], b_ref[...])
        for p in range(4):
            v = acc[:, p * LANES:(p + 1) * LANES]
            rot = v * c_ref[...] + _swap_halves(v) * s_ref[...]
            o_ref[p] = jnp.where(j < 6, rot, v)

    tab = pl.BlockSpec((tm, LANES), lambda i, j: (i, 0))
    return _pcall(
        body, name="qkv_proj", grid=(T // tm, 9),
        in_specs=[pl.BlockSpec((tm, D), lambda i, j: (i, 0)),
                  pl.BlockSpec((D, tn), lambda i, j: (0, j + (2 * CONV_CH) // tn)), tab, tab],
        out_specs=pl.BlockSpec((None, 4, tm, LANES), lambda i, j: (j, 0, i, 0)),
        out_shape=jax.ShapeDtypeStruct((9, 4, T, LANES), F32),
        compiler_params=_params("parallel", "parallel"),
    )(hn, w_in, rope_c, rope_s)


def _band_rows(start, d):
    if d == 1:
        return pl.ds(pl.multiple_of(start, BAND), BAND)
    return pl.ds(start, BAND, stride=d)


def _band_masks(n):
    row = lax.broadcasted_iota(jnp.int32, (BAND, BAND), 0)
    col = lax.broadcasted_iota(jnp.int32, (BAND, BAND), 1)
    no_prev = (n == 0).astype(jnp.int32) * (2 * BAND)
    return col <= row, col >= row + no_prev


def _attn_fwd(qkv, g):
    d = DILATIONS[g]
    nb = T // d // BAND

    def body(q_ref, k_ref, v_ref, o_ref, l_ref):
        lane_lo = lax.broadcasted_iota(jnp.int32, (BAND, LANES), 1) < HEAD_DIM

        def step(idx, carry):
            r = idx // nb
            n = idx % nb
            cur = _band_rows(n * (BAND * d) + r, d)
            prev = _band_rows(jnp.maximum(n - 1, 0) * (BAND * d) + r, d)
            q = q_ref[cur, :]
            kc = k_ref[cur, :].astype(BF16)
            vc = v_ref[cur, :].astype(BF16)
            kp = k_ref[prev, :].astype(BF16)
            vp = v_ref[prev, :].astype(BF16)
            mc, mp = _band_masks(n)
            outs, lses = [], []
            for h in range(2):
                hm = lane_lo if h == 0 else jnp.logical_not(lane_lo)
                qm = jnp.where(hm, q, 0.0).astype(BF16)
                sc = jnp.where(mc, _nt(qm, kc) * SCALE, NEG)
                sp = jnp.where(mp, _nt(qm, kp) * SCALE, NEG)
                mx = jnp.maximum(jnp.max(sc, axis=1, keepdims=True), jnp.max(sp, axis=1, keepdims=True))
                pc = jnp.exp(sc - mx)
                pp = jnp.exp(sp - mx)
                den = jnp.sum(pc, axis=1, keepdims=True) + jnp.sum(pp, axis=1, keepdims=True)
                outs.append((_nn(pc.astype(BF16), vc) + _nn(pp.astype(BF16), vp)) / den)
                lses.append(jnp.broadcast_to(mx + jnp.log(den), (BAND, LANES)))
            o_ref[cur, :] = jnp.where(lane_lo, outs[0], outs[1])
            l_ref[cur, :] = jnp.where(lane_lo, lses[0], lses[1])
            return carry

        lax.fori_loop(0, d * nb, step, 0)

    def slab(which):
        return pl.BlockSpec((None, None, T, LANES), lambda p: (which * 3 + g, p, 0, 0))

    out = pl.BlockSpec((None, T, LANES), lambda p: (p, 0, 0))
    shape = jax.ShapeDtypeStruct((4, T, LANES), F32)
    return _pcall(
        body, name=f"attn_fwd{g}", grid=(4,),
        in_specs=[slab(0), slab(1), slab(2)], out_specs=[out, out], out_shape=[shape, shape],
        compiler_params=_params("parallel"),
    )(qkv, qkv, qkv)


def _attn_merge(outs, lses, cat, tm=1024):
    def body(o0, o1, o2, l0, l1, l2, cat_in, cat_ref, att_ref, w0, w1, w2):
        del cat_in
        la, lb, lc = l0[...], l1[...], l2[...]
        mx = jnp.maximum(jnp.maximum(la, lb), lc)
        ea, eb, ec = jnp.exp(la - mx), jnp.exp(lb - mx), jnp.exp(lc - mx)
        inv = 1.0 / (ea + eb + ec)
        wa, wb, wc = ea * inv, eb * inv, ec * inv
        att = wa * o0[...] + wb * o1[...] + wc * o2[...]
        att_ref[...] = att
        cat_ref[...] = att.astype(BF16)
        w0[...] = wa
        w1[...] = wb
        w2[...] = wc

    slab = pl.BlockSpec((None, tm, LANES), lambda p, i: (p, i, 0))
    shape = jax.ShapeDtypeStruct((4, T, LANES), F32)
    return _pcall(
        body, name="attn_merge", grid=(4, T // tm),
        in_specs=[slab] * 6 + [pl.BlockSpec(memory_space=pl.ANY)],
        out_specs=[pl.BlockSpec((tm, LANES), lambda p, i: (i, CONV_CH // LANES + p)), slab, slab, slab, slab],
        out_shape=[jax.ShapeDtypeStruct((T, D), BF16), shape, shape, shape, shape],
        input_output_aliases={6: 0},
        compiler_params=_params("parallel", "parallel"),
    )(*outs, *lses, cat)


def _attn_bwd(qkv, lse, wgt, att, dcat, dqkv, g):
    d = DILATIONS[g]
    nb = T // d // BAND

    def body(q_ref, k_ref, v_ref, l_ref, w_ref, a_ref, da_ref, dq_in, o_ref):
        del dq_in
        lane = lax.broadcasted_iota(jnp.int32, (BAND, LANES), 1)
        lane_lo = lane < HEAD_DIM
        row = lax.broadcasted_iota(jnp.int32, (LANES, LANES), 0)
        same_head = ((row // HEAD_DIM) == (lane // HEAD_DIM)).astype(BF16)
        dq_ref, dk_ref, dv_ref = o_ref.at[0], o_ref.at[1], o_ref.at[2]
        dk_ref[...] = jnp.zeros((T, LANES), F32)
        dv_ref[...] = jnp.zeros((T, LANES), F32)

        def step(idx, carry):
            r = idx // nb
            n = idx % nb
            cur = _band_rows(n * (BAND * d) + r, d)
            prev = _band_rows(jnp.maximum(n - 1, 0) * (BAND * d) + r, d)
            q = q_ref[cur, :]
            kc = k_ref[cur, :].astype(BF16)
            vc = v_ref[cur, :].astype(BF16)
            kp = k_ref[prev, :].astype(BF16)
            vp = v_ref[prev, :].astype(BF16)
            lse_t = l_ref[cur, :]
            w_t = w_ref[cur, :]
            da = da_ref[cur, :]
            prod = da * a_ref[cur, :]
            hi = prod.astype(BF16)
            lo = (prod - hi.astype(F32)).astype(BF16)
            csum = _nn(hi, same_head) + _nn(lo, same_head)
            mc, mp = _band_masks(n)
            dqs = []
            dkc = jnp.zeros((BAND, LANES), F32)
            dkp = jnp.zeros((BAND, LANES), F32)
            dvc = jnp.zeros((BAND, LANES), F32)
            dvp = jnp.zeros((BAND, LANES), F32)
            for h in range(2):
                hm = lane_lo if h == 0 else jnp.logical_not(lane_lo)
                col0 = h * HEAD_DIM
                lse_h = lse_t[:, col0:col0 + 1]
                w_h = w_t[:, col0:col0 + 1]
                c_h = csum[:, col0:col0 + 1]
                qm = jnp.where(hm, q, 0.0).astype(BF16)
                dam = jnp.where(hm, da, 0.0).astype(BF16)
                pwc = w_h * jnp.exp(jnp.where(mc, _nt(qm, kc) * SCALE, NEG) - lse_h)
                pwp = w_h * jnp.exp(jnp.where(mp, _nt(qm, kp) * SCALE, NEG) - lse_h)
                dsc = (pwc * (_nt(dam, vc) - c_h) * SCALE).astype(BF16)
                dsp = (pwp * (_nt(dam, vp) - c_h) * SCALE).astype(BF16)
                dqs.append(_nn(dsc, kc) + _nn(dsp, kp))
                dkc = dkc + _tn(dsc, qm)
                dkp = dkp + _tn(dsp, qm)
                dvc = dvc + _tn(pwc.astype(BF16), dam)
                dvp = dvp + _tn(pwp.astype(BF16), dam)
            dq_ref[cur, :] = jnp.where(lane_lo, dqs[0], dqs[1])
            dk_ref[cur, :] += dkc
            dk_ref[prev, :] += dkp
            dv_ref[cur, :] += dvc
            dv_ref[prev, :] += dvp
            return carry

        lax.fori_loop(0, d * nb, step, 0)

    def slab(which):
        return pl.BlockSpec((None, None, T, LANES), lambda p: (which * 3 + g, p, 0, 0))

    per_pair = pl.BlockSpec((None, T, LANES), lambda p: (p, 0, 0))
    return _pcall(
        body, name=f"attn_bwd{g}", grid=(4,),
        in_specs=[slab(0), slab(1), slab(2), per_pair, per_pair, per_pair,
                  pl.BlockSpec((T, LANES), lambda p: (0, CONV_CH // LANES + p)),
                  pl.BlockSpec(memory_space=pl.ANY)],
        out_specs=pl.BlockSpec((None, 3, None, T, LANES), lambda p: (g, 0, p, 0, 0)),
        out_shape=jax.ShapeDtypeStruct((3, 3, 4, T, LANES), F32),
        input_output_aliases={7: 0},
        compiler_params=_params("parallel"),
    )(qkv, qkv, qkv, lse, wgt, att, dcat, dqkv)


def _rope_bwd(dqkv, rope_c, rope_s, dz):
    def body(d_ref, c_ref, s_ref, dz_in, o_ref):
        del dz_in
        w = pl.program_id(1)
        v = d_ref[...]
        rot = v * c_ref[...] + _swap_halves(v * s_ref[...])
        o_ref[...] = jnp.where(w < 2, rot, v).astype(BF16)

    tab = pl.BlockSpec((T, LANES), lambda g, w, p: (0, 0))
    return _pcall(
        body, name="rope_bwd", grid=(3, 3, 4),
        in_specs=[pl.BlockSpec((None, None, None, T, LANES), lambda g, w, p: (g, w, p, 0, 0)), tab, tab,
                  pl.BlockSpec(memory_space=pl.ANY)],
        out_specs=pl.BlockSpec((T, LANES), lambda g, w, p: (0, (2 * CONV_CH) // LANES + (w * 3 + g) * 4 + p)),
        out_shape=jax.ShapeDtypeStruct((T, EVEN_IN), BF16),
        input_output_aliases={3: 0},
        compiler_params=_params("parallel", "parallel", "parallel"),
    )(dqkv, rope_c, rope_s, dz)


ODD_TILE = 256
ODD_HALO = 8
GELU_C = 0.7978845608028654
GELU_A = 0.044715


def _gelu(x):
    return 0.5 * x * (1.0 + jnp.tanh(GELU_C * (x + GELU_A * x * x * x)))


def _gelu_grad(x):
    th = jnp.tanh(GELU_C * (x + GELU_A * x * x * x))
    return 0.5 * (1.0 + th) + 0.5 * x * (1.0 - th * th) * GELU_C * (1.0 + 3.0 * GELU_A * x * x)


def _tril():
    row = lax.broadcasted_iota(jnp.int32, (CHUNK, CHUNK), 0)
    col = lax.broadcasted_iota(jnp.int32, (CHUNK, CHUNK), 1)
    return (col <= row).astype(F32)


def _odd_parts(z, zh, i, k_ref, g_ref, be_ref, w_ref, bt_ref):
    R, H = ODD_TILE, ODD_HALO
    gb, gc, xs, uv = z[:, :512], z[:, 512:1024], z[:, 1024:1536], z[:, 1536:]
    halo = zh[:, 512:1024] * zh[:, 1024:1536] * (i > 0).astype(F32)
    win = jnp.concatenate([halo, gc * xs], axis=0)
    cv = jnp.zeros((R, 512), F32)
    for j in range(SCONV_W):
        off = H - (SCONV_W - 1) + j
        cv = cv + k_ref[j:j + 1, :] * win[off:off + R, :]
    ge = _gelu(uv)
    u, v = ge[:, :512], ge[:, 512:]
    mu = jnp.mean(v, axis=-1, keepdims=True)
    xc = v - mu
    rstd = lax.rsqrt(jnp.mean(xc * xc, axis=-1, keepdims=True) + EPS)
    xh = xc * rstd
    vn = xh * g_ref[...] + be_ref[...]
    tril = _tril()
    wms = [(w_ref[g] * tril).astype(BF16) for g in range(SG_GROUPS)]
    rows = []
    for ci in range(R // CHUNK):
        blocks = []
        for g in range(SG_GROUPS):
            blk = vn[ci * CHUNK:(ci + 1) * CHUNK, g * LANES:(g + 1) * LANES].astype(BF16)
            blocks.append(_nn(wms[g], blk) + bt_ref[:, g:g + 1])
        rows.append(jnp.concatenate(blocks, axis=1))
    vmix = jnp.concatenate(rows, axis=0)
    return gb, gc, xs, uv, win, cv, u, rstd, xh, vn, vmix, wms


def _odd_mid_fwd(z, conv_k, ln_g, ln_b, sg_w, sg_bt):
    R, H = ODD_TILE, ODD_HALO

    def body(z_ref, zh_ref, k_ref, g_ref, be_ref, w_ref, bt_ref, o_ref):
        i = pl.program_id(0)
        gb, _, _, _, _, cv, u, _, _, _, vmix, _ = _odd_parts(z_ref[...], zh_ref[...], i, k_ref, g_ref, be_ref, w_ref, bt_ref)
        o_ref[...] = jnp.concatenate([gb * cv, u * vmix], axis=1).astype(BF16)

    vec = pl.BlockSpec((1, 512), lambda i: (0, 0))
    return _pcall(
        body, name="odd_mid_fwd", grid=(T // R,),
        in_specs=[pl.BlockSpec((R, ODD_IN), lambda i: (i, 0)),
                  pl.BlockSpec((H, ODD_IN), lambda i: (jnp.maximum(i * (R // H) - 1, 0), 0)),
                  pl.BlockSpec((SCONV_W, 512), lambda i: (0, 0)), vec, vec,
                  pl.BlockSpec((SG_GROUPS, CHUNK, CHUNK), lambda i: (0, 0, 0)),
                  pl.BlockSpec((CHUNK, SG_GROUPS), lambda i: (0, 0))],
        out_specs=pl.BlockSpec((R, D), lambda i: (i, 0)),
        out_shape=jax.ShapeDtypeStruct((T, D), BF16),
        compiler_params=_params("parallel"),
    )(z, z, conv_k, ln_g, ln_b, sg_w, sg_bt)


def _odd_mid_bwd(z, dcat, conv_k, ln_g, ln_b, sg_w, sg_bt):
    R, H = ODD_TILE, ODD_HALO
    last = T // R - 1

    def body(z_ref, zh_ref, zn_ref, d_ref, dn_ref, k_ref, g_ref, be_ref, w_ref, bt_ref,
             dz_ref, dk_ref, dg_ref, dbe_ref, dw_ref, dbt_ref):
        i = pl.program_id(0)
        z = z_ref[...]
        gb, gc, xs, uv, win, cv, u, rstd, xh, vn, vmix, wms = _odd_parts(z, zh_ref[...], i, k_ref, g_ref, be_ref, w_ref, bt_ref)
        dcat_t = d_ref[...]
        dc, dd = dcat_t[:, :512], dcat_t[:, 512:]

        @pl.when(i == 0)
        def _():
            dk_ref[...] = jnp.zeros_like(dk_ref)
            dg_ref[...] = jnp.zeros_like(dg_ref)
            dbe_ref[...] = jnp.zeros_like(dbe_ref)
            dw_ref[...] = jnp.zeros_like(dw_ref)
            dbt_ref[...] = jnp.zeros_like(dbt_ref)

        dgb = dc * cv
        dcv = dc * gb
        nxt = dn_ref[:, :512] * zn_ref[:, :512] * (i < last).astype(F32)
        winb = jnp.concatenate([dcv, nxt], axis=0)
        dp = jnp.zeros((R, 512), F32)
        for j in range(SCONV_W):
            off = H - (SCONV_W - 1) + j
            dk_ref[j:j + 1, :] += jnp.sum(dcv * win[off:off + R, :], axis=0, keepdims=True)
            ob = SCONV_W - 1 - j
            dp = dp + k_ref[j:j + 1, :] * winb[ob:ob + R, :]
        dgc = dp * xs
        dxs = dp * gc
        du = dd * vmix
        dvmix = dd * u
        tril = _tril()
        rows = []
        for ci in range(R // CHUNK):
            blocks = []
            for g in range(SG_GROUPS):
                sl = (slice(ci * CHUNK, (ci + 1) * CHUNK), slice(g * LANES, (g + 1) * LANES))
                dblk = dvmix[sl]
                dblk16 = dblk.astype(BF16)
                blocks.append(_tn(wms[g], dblk16))
                dw_ref[g] += _nt(dblk16, vn[sl].astype(BF16)) * tril
                dbt_ref[:, g:g + 1] += jnp.sum(dblk, axis=1, keepdims=True)
            rows.append(jnp.concatenate(blocks, axis=1))
        dvn = jnp.concatenate(rows, axis=0)
        dg_ref[...] += jnp.sum(dvn * xh, axis=0, keepdims=True)
        dbe_ref[...] += jnp.sum(dvn, axis=0, keepdims=True)
        dxh = dvn * g_ref[...]
        dv = rstd * (dxh - jnp.mean(dxh, axis=-1, keepdims=True) - xh * jnp.mean(dxh * xh, axis=-1, keepdims=True))
        duv = jnp.concatenate([du, dv], axis=1) * _gelu_grad(uv)
        dz_ref[...] = jnp.concatenate([dgb, dgc, dxs, duv], axis=1).astype(BF16)

    vec = pl.BlockSpec((1, 512), lambda i: (0, 0))
    kspec = pl.BlockSpec((SCONV_W, 512), lambda i: (0, 0))
    wspec = pl.BlockSpec((SG_GROUPS, CHUNK, CHUNK), lambda i: (0, 0, 0))
    bspec = pl.BlockSpec((CHUNK, SG_GROUPS), lambda i: (0, 0))
    nxt_blk = lambda i: (jnp.minimum((i + 1) * (R // H), T // H - 1), 0)
    return _pcall(
        body, name="odd_mid_bwd", grid=(T // R,),
        in_specs=[pl.BlockSpec((R, ODD_IN), lambda i: (i, 0)),
                  pl.BlockSpec((H, ODD_IN), lambda i: (jnp.maximum(i * (R // H) - 1, 0), 0)),
                  pl.BlockSpec((H, ODD_IN), nxt_blk),
                  pl.BlockSpec((R, D), lambda i: (i, 0)),
                  pl.BlockSpec((H, D), nxt_blk),
                  kspec, vec, vec, wspec, bspec],
        out_specs=[pl.BlockSpec((R, ODD_IN), lambda i: (i, 0)), kspec, vec, vec, wspec, bspec],
        out_shape=[jax.ShapeDtypeStruct((T, ODD_IN), BF16), jax.ShapeDtypeStruct((SCONV_W, 512), F32),
                   jax.ShapeDtypeStruct((1, 512), F32), jax.ShapeDtypeStruct((1, 512), F32),
                   jax.ShapeDtypeStruct((SG_GROUPS, CHUNK, CHUNK), F32), jax.ShapeDtypeStruct((CHUNK, SG_GROUPS), F32)],
        compiler_params=_params("arbitrary"),
    )(z, z, z, dcat, dcat, conv_k, ln_g, ln_b, sg_w, sg_bt)


def _ffn_fwd(tag, h, g, w1, w2):
    hn = _rms_fwd(f"ffn{tag}_norm", h, g)

    def act(acc):
        r = jnp.maximum(acc, 0.0)
        return acc, r * r

    u, f = _mm(f"ffn{tag}_up", "nn", hn, w1, T, D_FF, D, (F32, BF16), epi=act)
    out = _mm(f"ffn{tag}_down", "nn", f, w2, T, D, D_FF, (F32,), epi=lambda acc, res: (acc + res,), extras=(h,))
    return out, (hn, u, f)


def _ffn_bwd(tag, h, g, w1, w2, saved, dout):
    hn, u, f = saved
    du = _mm(f"ffn{tag}_dact", "nt", dout, w2, T, D_FF, D, (BF16,),
             epi=lambda acc, uu: (acc * (2.0 * jnp.maximum(uu, 0.0)),), extras=(u,))
    dw2 = _mm(f"ffn{tag}_dw2", "tn", f, dout, D_FF, D, T, (BF16,))
    dw1 = _mm(f"ffn{tag}_dw1", "tn", hn, du, D, D_FF, T, (BF16,))
    dhn = _mm(f"ffn{tag}_dhn", "nt", du, w1, T, D, D_FF, (F32,))
    dh, dg = _rms_bwd(f"ffn{tag}_dnorm", h, dhn, g, dout)
    return dh, dg, dw1, dw2


def _rope_tables():
    half = HEAD_DIM // 2
    inv = 10000.0 ** (-jnp.arange(half, dtype=F32) / half)
    ang = jnp.arange(T, dtype=F32)[:, None] * inv[None, :]
    cos, sin = jnp.cos(ang), jnp.sin(ang)
    c = jnp.tile(jnp.concatenate([cos, cos], axis=1), (1, LANES // HEAD_DIM))
    s = jnp.tile(jnp.concatenate([-sin, sin], axis=1), (1, LANES // HEAD_DIM))
    return c, s


def _local_step(x, target, p):
    rope_c, rope_s = _rope_tables()
    grads = {}

    hn0 = _rms_fwd("mix0_norm", x, p["norm_mix_g0"])
    zc = _mm("even_in_conv", "nn", hn0, p["even_w_in"], T, 2 * CONV_CH, D, (F32,))
    qkv = _qkv_proj(hn0, p["even_w_in"], rope_c, rope_s)
    cv, cat0 = _econv_fwd(zc, p["even_conv_k"], p["even_conv_b"], p["even_ln_g"], p["even_ln_b"])
    att_parts = [_attn_fwd(qkv, g) for g in range(3)]
    outs = [a[0] for a in att_parts]
    lses = [a[1] for a in att_parts]
    cat0, att, w0, w1, w2 = _attn_merge(outs, lses, cat0)
    wgts = (w0, w1, w2)
    h1 = _mm("even_out", "nn", cat0, p["even_w_out"], T, D, D, (F32,), epi=lambda acc, res: (acc + res,), extras=(x,))
    h2, ffn0_saved = _ffn_fwd(0, h1, p["norm_ffn_g0"], p["ffn_w1_0"], p["ffn_w2_0"])

    hn1 = _rms_fwd("mix1_norm", h2, p["norm_mix_g1"])
    z1 = _mm("odd_in", "nn", hn1, p["odd_w_in"], T, ODD_IN, D, (F32,))
    cat1 = _odd_mid_fwd(z1, p["odd_conv_k"], p["odd_ln_g"], p["odd_ln_b"], p["odd_sg_w"], p["odd_sg_bt"])
    h3 = _mm("odd_out", "nn", cat1, p["odd_w_out"], T, D, D, (F32,), epi=lambda acc, res: (acc + res,), extras=(h2,))
    h4, ffn1_saved = _ffn_fwd(1, h3, p["norm_ffn_g1"], p["ffn_w1_1"], p["ffn_w2_1"])

    dh4, grads["final_g"], loss = _loss_head(h4, p["final_g"], target)

    dh3, grads["norm_ffn_g1"], grads["ffn_w1_1"], grads["ffn_w2_1"] = _ffn_bwd(
        1, h3, p["norm_ffn_g1"], p["ffn_w1_1"], p["ffn_w2_1"], ffn1_saved, dh4)
    dcat1 = _mm("odd_out_dx", "nt", dh3, p["odd_w_out"], T, D, D, (F32,))
    grads["odd_w_out"] = _mm("odd_out_dw", "tn", cat1, dh3, D, D, T, (BF16,))
    dz1, grads["odd_conv_k"], grads["odd_ln_g"], grads["odd_ln_b"], grads["odd_sg_w"], grads["odd_sg_bt"] = _odd_mid_bwd(
        z1, dcat1, p["odd_conv_k"], p["odd_ln_g"], p["odd_ln_b"], p["odd_sg_w"], p["odd_sg_bt"])
    grads["odd_w_in"] = _mm("odd_in_dw", "tn", hn1, dz1, D, ODD_IN, T, (BF16,))
    dhn1 = _mm("odd_in_dx", "nt", dz1, p["odd_w_in"], T, D, ODD_IN, (F32,))
    dh2, grads["norm_mix_g1"] = _rms_bwd("mix1_dnorm", h2, dhn1, p["norm_mix_g1"], dh3)

    dh1, grads["norm_ffn_g0"], grads["ffn_w1_0"], grads["ffn_w2_0"] = _ffn_bwd(
        0, h1, p["norm_ffn_g0"], p["ffn_w1_0"], p["ffn_w2_0"], ffn0_saved, dh2)
    dcat0 = _mm("even_out_dx", "nt", dh1, p["even_w_out"], T, D, D, (F32,))
    grads["even_w_out"] = _mm("even_out_dw", "tn", cat0, dh1, D, D, T, (BF16,))
    dcv, grads["even_ln_g"], grads["even_ln_b"], grads["even_conv_b"] = _econv_bwd_ln(
        cv, dcat0, p["even_ln_g"], p["even_ln_b"])
    dz0, grads["even_conv_k"] = _econv_bwd_conv(dcv, zc, p["even_conv_k"])
    dqkv = None
    for g in range(3):
        if dqkv is None:
            dqkv = lax.empty((3, 3, 4, T, LANES), F32)
        dqkv = _attn_bwd(qkv, lses[g], wgts[g], att, dcat0, dqkv, g)
    dz0 = _rope_bwd(dqkv, rope_c, rope_s, dz0)
    grads["even_w_in"] = _mm("even_in_dw", "tn", hn0, dz0, D, EVEN_IN, T, (BF16,))
    dhn0 = _mm("even_in_dx", "nt", dz0, p["even_w_in"], T, D, EVEN_IN, (F32,), tm=256)
    dx, grads["norm_mix_g0"] = _rms_bwd("mix0_dnorm", x, dhn0, p["norm_mix_g0"], dh1)
    return loss[0, 0], dx, grads


def _rowwise(name, fn, ins, out_dtypes, tm=256):
    rows, cols = ins[0].shape
    tm = tm if rows % tm == 0 else rows
    n_in = len(ins)

    def body(*refs):
        vals = fn(*[r[...] for r in refs[:n_in]])
        for o_ref, v in zip(refs[n_in:], vals):
            o_ref[...] = v.astype(o_ref.dtype)

    spec = pl.BlockSpec((tm, cols), lambda i: (i, 0))
    outs = _pcall(
        body, name=name, grid=(rows // tm,),
        in_specs=[spec] * n_in, out_specs=[spec] * len(out_dtypes),
        out_shape=[jax.ShapeDtypeStruct((rows, cols), dt) for dt in out_dtypes],
        compiler_params=_params("parallel"),
    )(*ins)
    return outs[0] if len(out_dtypes) == 1 else outs


def _adamw(name, w, g, m, v):
    c1 = 1.0 - ADAM_B1 ** ADAM_STEP
    c2 = 1.0 - ADAM_B2 ** ADAM_STEP

    def fn(w_t, g_t, m_t, v_t):
        m_new = ADAM_B1 * m_t + (1.0 - ADAM_B1) * g_t
        v_new = ADAM_B2 * v_t + (1.0 - ADAM_B2) * (g_t * g_t)
        delta = -ADAM_LR * ((m_new / c1) / (jnp.sqrt(v_new / c2) + ADAM_EPS) + ADAM_WD * w_t)
        return delta, m_new, v_new

    return _rowwise(name, fn, (w, g, m, v), (F32, F32, F32))


class _Piece:
    def __init__(self, name, rows, cols, axis, src, src_row0):
        self.name, self.rows, self.cols, self.axis = name, rows, cols, axis
        self.width = (cols if axis == 1 else rows) // 4
        self.src, self.src_row0 = src, src_row0

    @property
    def full_shape(self):
        return (self.rows, self.cols)

    @property
    def half_shape(self):
        return (self.rows // 2, self.cols) if self.axis == 1 else (self.rows, self.cols // 2)

    @property
    def shard_half_shape(self):
        return (self.rows // 2, self.width) if self.axis == 1 else (self.width, self.cols // 2)

    def shard_whole(self, ref):
        n = self.rows if self.axis == 1 else self.width
        return ref.at[pl.ds(self.src_row0, n), :]

    def shard_half(self, ref, h):
        if self.axis == 1:
            return ref.at[pl.ds(self.src_row0 + h * (self.rows // 2), self.rows // 2), :]
        return ref.at[pl.ds(self.src_row0, self.width), pl.ds(h * (self.cols // 2), self.cols // 2)]

    def full_shard(self, ref, s):
        if self.axis == 1:
            return ref.at[:, pl.ds(s * self.width, self.width)]
        return ref.at[pl.ds(s * self.width, self.width), :]

    def full_shard_half(self, ref, s, h):
        if self.axis == 1:
            return ref.at[pl.ds(h * (self.rows // 2), self.rows // 2), pl.ds(s * self.width, self.width)]
        return ref.at[pl.ds(s * self.width, self.width), pl.ds(h * (self.cols // 2), self.cols // 2)]

    def full_half(self, ref, h):
        if self.axis == 1:
            return ref.at[pl.ds(h * (self.rows // 2), self.rows // 2), :]
        return ref.at[:, pl.ds(h * (self.cols // 2), self.cols // 2)]

    def full_half_rows(self, ref, h, r0, n):
        if self.axis == 1:
            return ref.at[pl.ds(h * (self.rows // 2) + r0, n), :]
        return ref.at[pl.ds(r0, n), pl.ds(h * (self.cols // 2), self.cols // 2)]

    def half_shard(self, ref, s):
        return self.full_shard(ref, s)


PIECES = (
    _Piece("even_w_in", D, EVEN_IN, 1, 0, 0),
    _Piece("even_w_out", D, D, 0, 1, 0),
    _Piece("odd_w_in", D, ODD_IN, 1, 2, 0),
    _Piece("odd_w_out", D, D, 0, 3, 0),
    _Piece("ffn_w1_0", D, D_FF, 1, 4, 0),
    _Piece("ffn_w1_1", D, D_FF, 1, 4, D),
    _Piece("ffn_w2_0", D_FF, D, 0, 5, 0),
    _Piece("ffn_w2_1", D_FF, D, 0, 5, D_FF // 4),
)
N_PIECES = len(PIECES)
N_SHARD_OPERANDS = 6
ANY = pl.BlockSpec(memory_space=pl.ANY)
MESH = pl.DeviceIdType.MESH


def _mesh_place():
    x, y, c = lax.axis_index("x"), lax.axis_index("y"), lax.axis_index("c")
    chips = [(1 - x, y), (x, 1 - y), (1 - x, 1 - y)]
    return x, y, c, chips


def _remote(src, dst, send_sem, recv_sem, dev):
    return pltpu.make_async_remote_copy(src_ref=src, dst_ref=dst, send_sem=send_sem, recv_sem=recv_sem,
                                        device_id=dev, device_id_type=MESH)


def _gather_weights(shards):
    def body(*refs):
        srcs = refs[:N_SHARD_OPERANDS]
        fulls = refs[N_SHARD_OPERANDS:N_SHARD_OPERANDS + N_PIECES]
        send_sems, recv_sems, local_sems = refs[N_SHARD_OPERANDS + N_PIECES:]
        x, y, c, chips = _mesh_place()
        s = 2 * x + y
        sibling = (x, y, 1 - c)
        local = []
        for i, pc in enumerate(PIECES):
            cp = pltpu.make_async_copy(pc.shard_whole(srcs[pc.src]), pc.full_shard(fulls[i], s), local_sems.at[i])
            cp.start()
            local.append(cp)
        sends = []
        for k, (cx, cy) in enumerate(chips):
            for i, pc in enumerate(PIECES):
                cp = _remote(pc.shard_half(srcs[pc.src], c), pc.full_shard_half(fulls[i], s, c),
                             send_sems.at[k * N_PIECES + i], recv_sems.at[k * N_PIECES + i], (cx, cy, c))
                cp.start()
                sends.append(cp)
        for k, (cx, cy) in enumerate(chips):
            sk = 2 * cx + cy
            for i, pc in enumerate(PIECES):
                win = pc.full_shard_half(fulls[i], sk, c)
                _remote(win, win, send_sems.at[k * N_PIECES + i], recv_sems.at[k * N_PIECES + i], (cx, cy, c)).wait_recv()
                j = (3 + k) * N_PIECES + i
                cp = _remote(win, win, send_sems.at[j], recv_sems.at[j], sibling)
                cp.start()
                sends.append(cp)
        for k, (cx, cy) in enumerate(chips):
            sk = 2 * cx + cy
            for i, pc in enumerate(PIECES):
                win = pc.full_shard_half(fulls[i], sk, 1 - c)
                j = (3 + k) * N_PIECES + i
                _remote(win, win, send_sems.at[j], recv_sems.at[j], sibling).wait_recv()
        for cp in sends:
            cp.wait_send()
        for cp in local:
            cp.wait()

    return _pcall(
        body, name="gather_weights",
        in_specs=[ANY] * N_SHARD_OPERANDS, out_specs=[ANY] * N_PIECES,
        out_shape=[jax.ShapeDtypeStruct(pc.full_shape, BF16) for pc in PIECES],
        scratch_shapes=[pltpu.SemaphoreType.DMA((6 * N_PIECES,)), pltpu.SemaphoreType.DMA((6 * N_PIECES,)),
                        pltpu.SemaphoreType.DMA((N_PIECES,))],
    )(*shards)


CHIPSUM_CHUNKS = 4


def _chipsum(pc, partial):
    hr, hc = pc.half_shape
    ch = hr // CHIPSUM_CHUNKS

    def body(g_ref, out_ref, send_buf, recv_buf, own_buf, sum_buf, load_sems, own_sems, send_sems, recv_sems, out_sems):
        x, y, c, _ = _mesh_place()
        chunks = [pl.ds(k * ch, ch) for k in range(CHIPSUM_CHUNKS)]
        loads, owns, sends, stores = [], [], [], []
        for k, rows in enumerate(chunks):
            cp = pltpu.make_async_copy(pc.full_half_rows(g_ref, 1 - c, k * ch, ch), send_buf.at[rows, :], load_sems.at[k])
            cp.start()
            loads.append(cp)
            cp = pltpu.make_async_copy(pc.full_half_rows(g_ref, c, k * ch, ch), own_buf.at[rows, :], own_sems.at[k])
            cp.start()
            owns.append(cp)
        for k, rows in enumerate(chunks):
            loads[k].wait()
            cp = _remote(send_buf.at[rows, :], recv_buf.at[rows, :], send_sems.at[k], recv_sems.at[k], (x, y, 1 - c))
            cp.start()
            sends.append(cp)
        for k, rows in enumerate(chunks):
            owns[k].wait()
            sends[k].wait_recv()
            sum_buf[rows, :] = (own_buf[rows, :].astype(F32) + recv_buf[rows, :].astype(F32)).astype(BF16)
            cp = pltpu.make_async_copy(sum_buf.at[rows, :], out_ref.at[rows, :], out_sems.at[k])
            cp.start()
            stores.append(cp)
        for k in range(CHIPSUM_CHUNKS):
            sends[k].wait_send()
            stores[k].wait()

    buf = pltpu.VMEM((hr, hc), BF16)
    sems = pltpu.SemaphoreType.DMA((CHIPSUM_CHUNKS,))
    return _pcall(
        body, name=f"chipsum_{pc.name}", in_specs=[ANY], out_specs=ANY,
        out_shape=jax.ShapeDtypeStruct((hr, hc), BF16),
        scratch_shapes=[buf, buf, buf, buf, sems, sems, sems, sems, sems],
        compiler_params=pltpu.CompilerParams(vmem_limit_bytes=VMEM_LIMIT),
    )(partial)


def _scatter_chips(chip_sums):
    def body(*refs):
        sums = refs[:N_PIECES]
        stacks = refs[N_PIECES:2 * N_PIECES]
        send_sems, recv_sems, local_sems = refs[2 * N_PIECES:]
        x, y, c, chips = _mesh_place()
        s = 2 * x + y
        copies = []
        for i, pc in enumerate(PIECES):
            cp = pltpu.make_async_copy(pc.half_shard(sums[i], s), stacks[i].at[3], local_sems.at[i])
            cp.start()
            copies.append(cp)
        for k, (cx, cy) in enumerate(chips):
            for i, pc in enumerate(PIECES):
                cp = _remote(pc.half_shard(sums[i], 2 * cx + cy), stacks[i].at[k],
                             send_sems.at[k * N_PIECES + i], recv_sems.at[k * N_PIECES + i], (cx, cy, c))
                cp.start()
                copies.append(cp)
        for cp in copies:
            cp.wait()

    return _pcall(
        body, name="scatter_chips",
        in_specs=[ANY] * N_PIECES, out_specs=[ANY] * N_PIECES,
        out_shape=[jax.ShapeDtypeStruct((4,) + pc.shard_half_shape, BF16) for pc in PIECES],
        scratch_shapes=[pltpu.SemaphoreType.DMA((3 * N_PIECES,)), pltpu.SemaphoreType.DMA((3 * N_PIECES,)),
                        pltpu.SemaphoreType.DMA((N_PIECES,))],
    )(*chip_sums)


SHARD_OPERAND_SHAPES = ((D, EVEN_IN // 4), (D // 4, D), (D, ODD_IN // 4), (D // 4, D), (2 * D, D_FF // 4), (2 * D_FF // 4, D))


def _allsum_join(operand, stacks):
    pieces = [pc for pc in PIECES if pc.src == operand]
    n = len(pieces)

    def body(*refs):
        stack_refs = refs[:n]
        out_ref = refs[n]
        in_bufs = refs[n + 1:2 * n + 1]
        fin_bufs = refs[2 * n + 1:3 * n + 1]
        recv_bufs = refs[3 * n + 1:4 * n + 1]
        load_sems, send_sems, recv_sems, out_sems = refs[4 * n + 1:]
        x, y, c, _ = _mesh_place()
        loads, sends, stores = [], [], []
        for j in range(n):
            cp = pltpu.make_async_copy(stack_refs[j], in_bufs[j], load_sems.at[j])
            cp.start()
            loads.append(cp)
        for j, pc in enumerate(pieces):
            loads[j].wait()
            acc = in_bufs[j][0].astype(F32)
            for k in range(1, 4):
                acc = acc + in_bufs[j][k].astype(F32)
            fin_bufs[j][...] = acc
            cp = pltpu.make_async_copy(fin_bufs[j], pc.shard_half(out_ref, c), out_sems.at[2 * j])
            cp.start()
            stores.append(cp)
            cp = _remote(fin_bufs[j], recv_bufs[j], send_sems.at[j], recv_sems.at[j], (x, y, 1 - c))
            cp.start()
            sends.append(cp)
        for j, pc in enumerate(pieces):
            sends[j].wait_recv()
            cp = pltpu.make_async_copy(recv_bufs[j], pc.shard_half(out_ref, 1 - c), out_sems.at[2 * j + 1])
            cp.start()
            stores.append(cp)
        for cp in sends:
            cp.wait_send()
        for cp in stores:
            cp.wait()

    sh = pieces[0].shard_half_shape
    return _pcall(
        body, name=f"allsum_join_{operand}", in_specs=[ANY] * n, out_specs=ANY,
        out_shape=jax.ShapeDtypeStruct(SHARD_OPERAND_SHAPES[operand], F32),
        scratch_shapes=[pltpu.VMEM((4,) + sh, BF16)] * n + [pltpu.VMEM(sh, F32)] * (2 * n)
        + [pltpu.SemaphoreType.DMA((n,)), pltpu.SemaphoreType.DMA((n,)), pltpu.SemaphoreType.DMA((n,)),
           pltpu.SemaphoreType.DMA((2 * n,))],
        compiler_params=pltpu.CompilerParams(vmem_limit_bytes=VMEM_LIMIT),
    )(*stacks)


def _allgather8(name, blk, with_sum):
    m = blk.shape[0]

    def body(x_ref, out_ref, *rest):
        if with_sum:
            sum_ref, send_sems, recv_sems, local_sem = rest
        else:
            send_sems, recv_sems, local_sem = rest
        x, y, c, chips = _mesh_place()
        me, sibling = (x, y, c), (x, y, 1 - c)

        def rows(px, py, pc):
            return out_ref.at[pl.ds((4 * px + 2 * py + pc) * m, m), :]

        def copy(k, block, to, src=None):
            return _remote(rows(*block) if src is None else src, rows(*block), send_sems.at[k], recv_sems.at[k], to)

        mine = pltpu.make_async_copy(x_ref, rows(*me), local_sem)
        mine.start()
        first = [copy(0, me, sibling, src=x_ref)]
        first += [copy(1 + j, me, (*chip, c), src=x_ref) for j, chip in enumerate(chips)]
        for cp in first:
            cp.start()
        passed = [copy(4 + j, (*chip, c), sibling) for j, chip in enumerate(chips)]
        for j, chip in enumerate(chips):
            copy(1 + j, (*chip, c), me).wait_recv()
            passed[j].start()
        copy(0, sibling, me).wait_recv()
        for j, chip in enumerate(chips):
            copy(4 + j, (*chip, 1 - c), me).wait_recv()
        for cp in first + passed:
            cp.wait_send()
        mine.wait()
        if with_sum:
            acc = out_ref[0:m, :]
            for dev in range(1, 8):
                acc = acc + out_ref[dev * m:(dev + 1) * m, :]
            sum_ref[...] = acc

    vm = pl.BlockSpec(memory_space=pltpu.VMEM)
    out_shape = [jax.ShapeDtypeStruct((8 * m, LANES), F32)]
    if with_sum:
        out_shape.append(jax.ShapeDtypeStruct((m, LANES), F32))
    return _pcall(
        body, name=name, in_specs=[vm], out_specs=[vm] * len(out_shape), out_shape=out_shape,
        scratch_shapes=[pltpu.SemaphoreType.DMA((7,)), pltpu.SemaphoreType.DMA((7,)), pltpu.SemaphoreType.DMA],
    )(blk)


def _pack(arrays, row_counts):
    rows = []
    for a, n in zip(arrays, row_counts):
        flat = a.reshape(-1, LANES)
        if flat.shape[0] < n:
            flat = jnp.concatenate([flat, jnp.zeros((n - flat.shape[0], LANES), F32)], axis=0)
        rows.append(flat)
    return jnp.concatenate(rows, axis=0)


def _unpack(buf, shapes, row_counts):
    out, r0 = [], 0
    for sh, n in zip(shapes, row_counts):
        size = 1
        for dim in sh:
            size *= dim
        out.append(buf[r0:r0 + size // LANES].reshape(sh))
        r0 += n
    return out


REPL_NAMES = ("norm_mix_g", "norm_ffn_g", "even_conv_b", "even_ln_g", "even_ln_b", "odd_sg_w", "odd_sg_b", "final_g")
REPL_SHAPES = ((2, D), (2, D), (1, 512), (1, 512), (1, 512), (1, SG_GROUPS, CHUNK, CHUNK), (1, SG_GROUPS, CHUNK), (D,))
REPL_ROWS = (16, 16, 4, 4, 4, 512, 4, 8)
SHARDED_NAMES = ("even_conv_k", "odd_conv_k", "odd_ln_g", "odd_ln_b")
SHARDED_SHARD_SHAPES = ((1, CONV_W, LANES), (1, SCONV_W, LANES), (1, LANES), (1, LANES))
SHARDED_SHARD_ROWS = (32, 8, 4, 4)
SHARDED_FULL_SHAPES = ((CONV_W, 512), (SCONV_W, 512), (1, 512), (1, 512))
SHARDED_FULL_ROWS = (124, 12, 4, 4)


def kernel(x, norm_mix_g, norm_ffn_g, even_w_in, even_conv_k, even_conv_b, even_ln_g, even_ln_b, even_w_out, odd_w_in, odd_conv_k, odd_ln_g, odd_ln_b, odd_sg_w, odd_sg_b, odd_w_out, ffn_w1, ffn_w2, final_g, loss_target, m_norm_mix_g, m_norm_ffn_g, m_even_w_in, m_even_conv_k, m_even_conv_b, m_even_ln_g, m_even_ln_b, m_even_w_out, m_odd_w_in, m_odd_conv_k, m_odd_ln_g, m_odd_ln_b, m_odd_sg_w, m_odd_sg_b, m_odd_w_out, m_ffn_w1, m_ffn_w2, m_final_g, v_norm_mix_g, v_norm_ffn_g, v_even_w_in, v_even_conv_k, v_even_conv_b, v_even_ln_g, v_even_ln_b, v_even_w_out, v_odd_w_in, v_odd_conv_k, v_odd_ln_g, v_odd_ln_b, v_odd_sg_w, v_odd_sg_b, v_odd_w_out, v_ffn_w1, v_ffn_w2, v_final_g):
    names = ("norm_mix_g", "norm_ffn_g", "even_w_in", "even_conv_k", "even_conv_b", "even_ln_g", "even_ln_b", "even_w_out",
             "odd_w_in", "odd_conv_k", "odd_ln_g", "odd_ln_b", "odd_sg_w", "odd_sg_b", "odd_w_out", "ffn_w1", "ffn_w2", "final_g")
    w = dict(zip(names, (norm_mix_g, norm_ffn_g, even_w_in, even_conv_k, even_conv_b, even_ln_g, even_ln_b, even_w_out,
                         odd_w_in, odd_conv_k, odd_ln_g, odd_ln_b, odd_sg_w, odd_sg_b, odd_w_out, ffn_w1, ffn_w2, final_g)))
    mom = dict(zip(names, (m_norm_mix_g, m_norm_ffn_g, m_even_w_in, m_even_conv_k, m_even_conv_b, m_even_ln_g, m_even_ln_b,
                           m_even_w_out, m_odd_w_in, m_odd_conv_k, m_odd_ln_g, m_odd_ln_b, m_odd_sg_w, m_odd_sg_b, m_odd_w_out,
                           m_ffn_w1, m_ffn_w2, m_final_g)))
    vel = dict(zip(names, (v_norm_mix_g, v_norm_ffn_g, v_even_w_in, v_even_conv_k, v_even_conv_b, v_even_ln_g, v_even_ln_b,
                           v_even_w_out, v_odd_w_in, v_odd_conv_k, v_odd_ln_g, v_odd_ln_b, v_odd_sg_w, v_odd_sg_b, v_odd_w_out,
                           v_ffn_w1, v_ffn_w2, v_final_g)))
    big_names = ("even_w_in", "even_w_out", "odd_w_in", "odd_w_out", "ffn_w1", "ffn_w2")
    chip = 2 * lax.axis_index("x") + lax.axis_index("y")

    def shard2d(t, name):
        return t[name].reshape(SHARD_OPERAND_SHAPES[big_names.index(name)])

    shards16 = [_rowwise(f"cast_{n}", lambda a: (a,), (shard2d(w, n),), (BF16,)) for n in big_names]
    full = dict(zip((pc.name for pc in PIECES), _gather_weights(shards16)))
    small_pack = _pack([w[n] for n in SHARDED_NAMES], SHARDED_SHARD_ROWS)
    gathered = _allgather8("gather_small", small_pack, False)[0].reshape(4, 2, sum(SHARDED_SHARD_ROWS), LANES)[:, 0]
    r0 = 0
    for n, sh, rows, full_sh in zip(SHARDED_NAMES, SHARDED_SHARD_SHAPES, SHARDED_SHARD_ROWS, SHARDED_FULL_SHAPES):
        per_chip = gathered[:, r0:r0 + rows].reshape(4, -1)[:, :full_sh[0] * LANES].reshape(4, full_sh[0], LANES)
        full[n] = jnp.transpose(per_chip, (1, 0, 2)).reshape(full_sh)
        r0 += rows
    p = dict(full)
    p.update(norm_mix_g0=norm_mix_g[0:1], norm_mix_g1=norm_mix_g[1:2], norm_ffn_g0=norm_ffn_g[0:1], norm_ffn_g1=norm_ffn_g[1:2],
             even_conv_b=even_conv_b, even_ln_g=even_ln_g, even_ln_b=even_ln_b,
             odd_sg_w=odd_sg_w[0], odd_sg_bt=odd_sg_b[0].T, final_g=final_g[None, :])

    loss, dx, g = _local_step(x[0], loss_target[0], p)
    loss = lax.psum(loss, ("x", "y", "c"))

    chip_sums = [_chipsum(pc, g[pc.name]) for pc in PIECES]
    stacks = _scatter_chips(chip_sums)
    big_grads = {n: _allsum_join(o, [stacks[i] for i, pc in enumerate(PIECES) if pc.src == o])
                 for o, n in enumerate(big_names)}

    g_small = {"norm_mix_g": jnp.concatenate([g["norm_mix_g0"], g["norm_mix_g1"]], axis=0),
               "norm_ffn_g": jnp.concatenate([g["norm_ffn_g0"], g["norm_ffn_g1"]], axis=0),
               "even_conv_b": g["even_conv_b"], "even_ln_g": g["even_ln_g"], "even_ln_b": g["even_ln_b"],
               "odd_sg_w": g["odd_sg_w"], "odd_sg_b": g["odd_sg_bt"].T, "final_g": g["final_g"],
               "even_conv_k": g["even_conv_k"], "odd_conv_k": g["odd_conv_k"], "odd_ln_g": g["odd_ln_g"], "odd_ln_b": g["odd_ln_b"]}
    grad_pack = _pack([g_small[n] for n in REPL_NAMES + SHARDED_NAMES], REPL_ROWS + SHARDED_FULL_ROWS)
    grad_sum = _allgather8("allreduce_small", grad_pack, True)[1]
    parts = _unpack(grad_sum, REPL_SHAPES + SHARDED_FULL_SHAPES, REPL_ROWS + SHARDED_FULL_ROWS)
    grads = dict(zip(REPL_NAMES, parts[:len(REPL_NAMES)]))
    for n, full_g, sh in zip(SHARDED_NAMES, parts[len(REPL_NAMES):], SHARDED_SHARD_SHAPES):
        grads[n] = lax.dynamic_slice_in_dim(full_g, chip * LANES, LANES, axis=1).reshape(sh)
    for n in big_names:
        grads[n] = big_grads[n].reshape(w[n].shape)

    delta, new_m, new_v = {}, {}, {}
    for n in big_names:
        d2, m2, v2 = _adamw(f"adamw_{n}", shard2d(w, n), big_grads[n], shard2d(mom, n), shard2d(vel, n))
        delta[n], new_m[n], new_v[n] = (t.reshape(w[n].shape) for t in (d2, m2, v2))
    for tag, group, rows, shapes in (("repl", REPL_NAMES, REPL_ROWS, [w[n].shape for n in REPL_NAMES]),
                                     ("sharded", SHARDED_NAMES, SHARDED_SHARD_ROWS, SHARDED_SHARD_SHAPES)):
        packs = [_pack([t[n] for n in group], rows) for t in (w, grads, mom, vel)]
        outs = _adamw(f"adamw_{tag}", *packs)
        for res, o in zip((delta, new_m, new_v), outs):
            res.update(zip(group, _unpack(o, shapes, rows)))

    out = [loss, dx[None]]
    for res in (grads, delta, new_m, new_v):
        out.extend(res[n] for n in names)
    return tuple(out)
```

```python
import functools

import jax
import jax.numpy as jnp
from jax import lax
from jax.experimental import pallas as pl
from jax.experimental.pallas import tpu as pltpu

F32 = jnp.float32
BF16 = jnp.bfloat16

T = 2048
D = 1024
CONV_CH = 512
CONV_W = 31
HEAD_DIM = 64
ATT_W = 1536
EVEN_IN = 5632
ODD_IN = 2560
SCONV_W = 3
SG_GROUPS = 4
CHUNK = 128
D_FF = 4096
EPS = 1e-6
DILATIONS = (1, 4, 16)
BAND = 128
SCALE = HEAD_DIM ** -0.5
NEG = -1e30

ADAM_LR = 0.001
ADAM_B1 = 0.9
ADAM_B2 = 0.999
ADAM_EPS = 1e-08
ADAM_WD = 0.01
ADAM_STEP = 10

V7X_VMEM_BYTES = 64 * 2 ** 20
VMEM_LIMIT = V7X_VMEM_BYTES - 8 * 2 ** 20
LANES = 128
TOKEN_SHAPE = (8, LANES)


def _pcall(body, **kw):
    return pl.pallas_call(body, **kw)


def _params(*sem):
    return pltpu.CompilerParams(dimension_semantics=sem, vmem_limit_bytes=VMEM_LIMIT)


def _dot(a, b, dims):
    return lax.dot_general(a, b, (dims, ((), ())), preferred_element_type=F32)


def _nn(a, b):
    return _dot(a, b, ((1,), (0,)))


def _nt(a, b):
    return _dot(a, b, ((1,), (1,)))


def _tn(a, b):
    return _dot(a, b, ((0,), (0,)))


def _sigmoid(x):
    return 1.0 / (1.0 + jnp.exp(-x))


def _mm(name, mode, a, b, m, n, k, out_dtypes, *, tm=512, tn=512, b_off=0, extras=(), epi=None, tie=None):
    if mode == "nn":
        a_spec = pl.BlockSpec((tm, k), lambda i, j: (i, 0))
        b_spec = pl.BlockSpec((k, tn), lambda i, j: (0, j + b_off))
        dims = ((1,), (0,))
    elif mode == "nt":
        a_spec = pl.BlockSpec((tm, k), lambda i, j: (i, 0))
        b_spec = pl.BlockSpec((tn, k), lambda i, j: (j, 0))
        dims = ((1,), (1,))
    else:
        a_spec = pl.BlockSpec((k, tm), lambda i, j: (0, i))
        b_spec = pl.BlockSpec((k, tn), lambda i, j: (0, j))
        dims = ((0,), (0,))
    o_spec = pl.BlockSpec((tm, tn), lambda i, j: (i, j))
    n_extra = len(extras)
    ties = () if tie is None else (tie,)

    def body(a_ref, b_ref, *rest):
        rest = rest[len(ties):]
        acc = _dot(a_ref[...].astype(BF16), b_ref[...].astype(BF16), dims)
        vals = epi(acc, *[e[...] for e in rest[:n_extra]]) if epi is not None else (acc,)
        for o_ref, v in zip(rest[n_extra:], vals):
            o_ref[...] = v.astype(o_ref.dtype)

    outs = _pcall(
        body, name=name, grid=(m // tm, n // tn),
        in_specs=[a_spec, b_spec] + [pl.BlockSpec(TOKEN_SHAPE, lambda i, j: (0, 0))] * len(ties) + [o_spec] * n_extra,
        out_specs=[o_spec] * len(out_dtypes),
        out_shape=[jax.ShapeDtypeStruct((m, n), dt) for dt in out_dtypes],
        compiler_params=_params("parallel", "parallel"),
    )(a, b, *ties, *extras)
    return outs[0] if len(out_dtypes) == 1 else outs


def _rms_fwd(name, h, g, tm=512):
    def body(h_ref, g_ref, o_ref):
        x = h_ref[...]
        r = lax.rsqrt(jnp.mean(x * x, axis=-1, keepdims=True) + EPS)
        o_ref[...] = ((x * r) * g_ref[...]).astype(BF16)

    return _pcall(
        body, name=name, grid=(T // tm,),
        in_specs=[pl.BlockSpec((tm, D), lambda i: (i, 0)), pl.BlockSpec((1, D), lambda i: (0, 0))],
        out_specs=pl.BlockSpec((tm, D), lambda i: (i, 0)),
        out_shape=jax.ShapeDtypeStruct((T, D), BF16),
        compiler_params=_params("parallel"),
    )(h, g)


def _rms_bwd(name, h, dhn, g, dres, tm=512):
    def body(h_ref, d_ref, g_ref, r_ref, dh_ref, dg_ref):
        x = h_ref[...]
        r = lax.rsqrt(jnp.mean(x * x, axis=-1, keepdims=True) + EPS)
        nrm = x * r
        dy = d_ref[...]
        dn = dy * g_ref[...]
        dh_ref[...] = r_ref[...] + r * (dn - nrm * jnp.mean(dn * nrm, axis=-1, keepdims=True))

        @pl.when(pl.program_id(0) == 0)
        def _():
            dg_ref[...] = jnp.zeros_like(dg_ref)

        dg_ref[...] += jnp.sum(dy * nrm, axis=0, keepdims=True)

    row = pl.BlockSpec((tm, D), lambda i: (i, 0))
    vec = pl.BlockSpec((1, D), lambda i: (0, 0))
    return _pcall(
        body, name=name, grid=(T // tm,),
        in_specs=[row, row, vec, row], out_specs=[row, vec],
        out_shape=[jax.ShapeDtypeStruct((T, D), F32), jax.ShapeDtypeStruct((1, D), F32)],
        compiler_params=_params("arbitrary"),
    )(h, dhn, g, dres)


def _loss_head(h, g, target, tm=512):
    def body(h_ref, g_ref, t_ref, dh_ref, dg_ref, loss_ref):
        x = h_ref[...]
        r = lax.rsqrt(jnp.mean(x * x, axis=-1, keepdims=True) + EPS)
        nrm = x * r
        gain = g_ref[...]
        err = nrm * gain - t_ref[...]
        dy = err * (1.0 / D)
        dn = dy * gain
        dh_ref[...] = r * (dn - nrm * jnp.mean(dn * nrm, axis=-1, keepdims=True))

        @pl.when(pl.program_id(0) == 0)
        def _():
            dg_ref[...] = jnp.zeros_like(dg_ref)
            loss_ref[...] = jnp.zeros_like(loss_ref)

        dg_ref[...] += jnp.sum(dy * nrm, axis=0, keepdims=True)
        part = jnp.sum(jnp.sum(err * err, axis=1, keepdims=True), axis=0, keepdims=True) * (0.5 / D)
        loss_ref[...] += jnp.broadcast_to(part, (1, LANES))

    row = pl.BlockSpec((tm, D), lambda i: (i, 0))
    vec = pl.BlockSpec((1, D), lambda i: (0, 0))
    return _pcall(
        body, name="loss_head", grid=(T // tm,),
        in_specs=[row, vec, row], out_specs=[row, vec, pl.BlockSpec((1, LANES), lambda i: (0, 0))],
        out_shape=[jax.ShapeDtypeStruct((T, D), F32), jax.ShapeDtypeStruct((1, D), F32),
                   jax.ShapeDtypeStruct((1, LANES), F32)],
        compiler_params=_params("arbitrary"),
    )(h, g, target)


CONV_TILE = 256
CONV_HALO = 32


def _glu(z):
    return z[:, :CONV_CH] * _sigmoid(z[:, CONV_CH:])


def _econv_fwd(zc, conv_k, conv_b, ln_g, ln_b):
    R, H = CONV_TILE, CONV_HALO

    def body(z_ref, zh_ref, k_ref, b_ref, g_ref, be_ref, cv_ref, cat_ref):
        i = pl.program_id(0)
        glu = _glu(z_ref[...])
        halo = _glu(zh_ref[...]) * (i > 0).astype(F32)
        win = jnp.concatenate([halo, glu], axis=0)
        acc = jnp.zeros((R, CONV_CH), F32) + b_ref[...]
        for j in range(CONV_W):
            off = H - (CONV_W - 1) + j
            acc = acc + k_ref[j:j + 1, :] * win[off:off + R, :]
        cv_ref[...] = acc
        mu = jnp.mean(acc, axis=-1, keepdims=True)
        xc = acc - mu
        rstd = lax.rsqrt(jnp.mean(xc * xc, axis=-1, keepdims=True) + EPS)
        ln = xc * rstd * g_ref[...] + be_ref[...]
        cat_ref[...] = (ln * _sigmoid(ln)).astype(BF16)

    vec = pl.BlockSpec((1, CONV_CH), lambda i: (0, 0))
    return _pcall(
        body, name="econv_fwd", grid=(T // R,),
        in_specs=[pl.BlockSpec((R, 2 * CONV_CH), lambda i: (i, 0)),
                  pl.BlockSpec((H, 2 * CONV_CH), lambda i: (jnp.maximum(i * (R // H) - 1, 0), 0)),
                  pl.BlockSpec((CONV_W, CONV_CH), lambda i: (0, 0)), vec, vec, vec],
        out_specs=[pl.BlockSpec((R, CONV_CH), lambda i: (i, 0)), pl.BlockSpec((R, CONV_CH), lambda i: (i, 0))],
        out_shape=[jax.ShapeDtypeStruct((T, CONV_CH), F32), jax.ShapeDtypeStruct((T, D), BF16)],
        compiler_params=_params("parallel"),
    )(zc, zc, conv_k, conv_b, ln_g, ln_b)


def _econv_bwd_ln(cv, dcat, ln_g, ln_b):
    R = CONV_TILE

    def body(cv_ref, d_ref, g_ref, be_ref, dcv_ref, dg_ref, dbe_ref, dcb_ref):
        cv_t = cv_ref[...]
        mu = jnp.mean(cv_t, axis=-1, keepdims=True)
        xc = cv_t - mu
        rstd = lax.rsqrt(jnp.mean(xc * xc, axis=-1, keepdims=True) + EPS)
        xh = xc * rstd
        ln = xh * g_ref[...] + be_ref[...]
        sg = _sigmoid(ln)
        dln = d_ref[...] * (sg * (1.0 + ln * (1.0 - sg)))
        dxh = dln * g_ref[...]
        dcv = rstd * (dxh - jnp.mean(dxh, axis=-1, keepdims=True) - xh * jnp.mean(dxh * xh, axis=-1, keepdims=True))
        dcv_ref[...] = dcv

        @pl.when(pl.program_id(0) == 0)
        def _():
            dg_ref[...] = jnp.zeros_like(dg_ref)
            dbe_ref[...] = jnp.zeros_like(dbe_ref)
            dcb_ref[...] = jnp.zeros_like(dcb_ref)

        dg_ref[...] += jnp.sum(dln * xh, axis=0, keepdims=True)
        dbe_ref[...] += jnp.sum(dln, axis=0, keepdims=True)
        dcb_ref[...] += jnp.sum(dcv, axis=0, keepdims=True)

    vec = pl.BlockSpec((1, CONV_CH), lambda i: (0, 0))
    row = pl.BlockSpec((R, CONV_CH), lambda i: (i, 0))
    vshape = jax.ShapeDtypeStruct((1, CONV_CH), F32)
    return _pcall(
        body, name="econv_bwd_ln", grid=(T // R,),
        in_specs=[row, row, vec, vec], out_specs=[row, vec, vec, vec],
        out_shape=[jax.ShapeDtypeStruct((T, CONV_CH), F32), vshape, vshape, vshape],
        compiler_params=_params("arbitrary"),
    )(cv, dcat, ln_g, ln_b)


def _econv_bwd_conv(dcv, zc, conv_k):
    R, H = CONV_TILE, CONV_HALO
    last = T // R - 1

    def body(d_ref, dn_ref, z_ref, zh_ref, k_ref, dz_ref, dk_ref):
        i = pl.program_id(0)
        z = z_ref[...]
        a_lin = z[:, :CONV_CH]
        sg = _sigmoid(z[:, CONV_CH:])
        glu = a_lin * sg
        halo = _glu(zh_ref[...]) * (i > 0).astype(F32)
        win = jnp.concatenate([halo, glu], axis=0)
        dcv_t = d_ref[...]
        nxt = dn_ref[...] * (i < last).astype(F32)
        winb = jnp.concatenate([dcv_t, nxt], axis=0)

        @pl.when(i == 0)
        def _():
            dk_ref[...] = jnp.zeros_like(dk_ref)

        dglu = jnp.zeros((R, CONV_CH), F32)
        for j in range(CONV_W):
            off = H - (CONV_W - 1) + j
            dk_ref[j:j + 1, :] += jnp.sum(dcv_t * win[off:off + R, :], axis=0, keepdims=True)
            ob = CONV_W - 1 - j
            dglu = dglu + k_ref[j:j + 1, :] * winb[ob:ob + R, :]
        dz_ref[...] = jnp.concatenate([dglu * sg, dglu * a_lin * sg * (1.0 - sg)], axis=1).astype(BF16)

    return _pcall(
        body, name="econv_bwd_conv", grid=(T // R,),
        in_specs=[pl.BlockSpec((R, CONV_CH), lambda i: (i, 0)),
                  pl.BlockSpec((H, CONV_CH), lambda i: (jnp.minimum((i + 1) * (R // H), T // H - 1), 0)),
                  pl.BlockSpec((R, 2 * CONV_CH), lambda i: (i, 0)),
                  pl.BlockSpec((H, 2 * CONV_CH), lambda i: (jnp.maximum(i * (R // H) - 1, 0), 0)),
                  pl.BlockSpec((CONV_W, CONV_CH), lambda i: (0, 0))],
        out_specs=[pl.BlockSpec((R, 2 * CONV_CH), lambda i: (i, 0)), pl.BlockSpec((CONV_W, CONV_CH), lambda i: (0, 0))],
        out_shape=[jax.ShapeDtypeStruct((T, EVEN_IN), BF16), jax.ShapeDtypeStruct((CONV_W, CONV_CH), F32)],
        compiler_params=_params("arbitrary"),
    )(dcv, dcv, zc, zc, conv_k)


def _swap_halves(v):
    lane = lax.broadcasted_iota(jnp.int32, v.shape, 1)
    return jnp.where((lane % HEAD_DIM) < HEAD_DIM // 2, pltpu.roll(v, LANES - HEAD_DIM // 2, 1),
                     pltpu.roll(v, HEAD_DIM // 2, 1))


def _qkv_proj(hn, w_in, rope_c, rope_s, tm=512):
    tn = 4 * LANES

    def body(a_ref, b_ref, c_ref, s_ref, o_ref):
        j = pl.program_id(1)
        acc = _nn(a_ref[...], b_ref[...])
        for p in range(4):
            v = acc[:, p * LANES:(p + 1) * LANES]
            rot = v * c_ref[...] + _swap_halves(v) * s_ref[...]
            o_ref[p] = jnp.where(j < 6, rot, v)

    tab = pl.BlockSpec((tm, LANES), lambda i, j: (i, 0))
    return _pcall(
        body, name="qkv_proj", grid=(T // tm, 9),
        in_specs=[pl.BlockSpec((tm, D), lambda i, j: (i, 0)),
                  pl.BlockSpec((D, tn), lambda i, j: (0, j + (2 * CONV_CH) // tn)), tab, tab],
        out_specs=pl.BlockSpec((None, 4, tm, LANES), lambda i, j: (j, 0, i, 0)),
        out_shape=jax.ShapeDtypeStruct((9, 4, T, LANES), F32),
        compiler_params=_params("parallel", "parallel"),
    )(hn, w_in, rope_c, rope_s)


def _band_rows(start, d):
    if d == 1:
        return pl.ds(pl.multiple_of(start, BAND), BAND)
    return pl.ds(start, BAND, stride=d)


def _band_masks(n):
    row = lax.broadcasted_iota(jnp.int32, (BAND, BAND), 0)
    col = lax.broadcasted_iota(jnp.int32, (BAND, BAND), 1)
    no_prev = (n == 0).astype(jnp.int32) * (2 * BAND)
    return col <= row, col >= row + no_prev


def _attn_fwd(qkv, g):
    d = DILATIONS[g]
    nb = T // d // BAND

    def body(q_ref, k_ref, v_ref, o_ref, l_ref):
        lane_lo = lax.broadcasted_iota(jnp.int32, (BAND, LANES), 1) < HEAD_DIM

        def step(idx, carry):
            r = idx // nb
            n = idx % nb
            cur = _band_rows(n * (BAND * d) + r, d)
            prev = _band_rows(jnp.maximum(n - 1, 0) * (BAND * d) + r, d)
            q = q_ref[cur, :]
            kc = k_ref[cur, :].astype(BF16)
            vc = v_ref[cur, :].astype(BF16)
            kp = k_ref[prev, :].astype(BF16)
            vp = v_ref[prev, :].astype(BF16)
            mc, mp = _band_masks(n)
            outs, lses = [], []
            for h in range(2):
                hm = lane_lo if h == 0 else jnp.logical_not(lane_lo)
                qm = jnp.where(hm, q, 0.0).astype(BF16)
                sc = jnp.where(mc, _nt(qm, kc) * SCALE, NEG)
                sp = jnp.where(mp, _nt(qm, kp) * SCALE, NEG)
                mx = jnp.maximum(jnp.max(sc, axis=1, keepdims=True), jnp.max(sp, axis=1, keepdims=True))
                pc = jnp.exp(sc - mx)
                pp = jnp.exp(sp - mx)
                den = jnp.sum(pc, axis=1, keepdims=True) + jnp.sum(pp, axis=1, keepdims=True)
                outs.append((_nn(pc.astype(BF16), vc) + _nn(pp.astype(BF16), vp)) / den)
                lses.append(jnp.broadcast_to(mx + jnp.log(den), (BAND, LANES)))
            o_ref[cur, :] = jnp.where(lane_lo, outs[0], outs[1])
            l_ref[cur, :] = jnp.where(lane_lo, lses[0], lses[1])
            return carry

        lax.fori_loop(0, d * nb, step, 0)

    def slab(which):
        return pl.BlockSpec((None, None, T, LANES), lambda p: (which * 3 + g, p, 0, 0))

    out = pl.BlockSpec((None, T, LANES), lambda p: (p, 0, 0))
    shape = jax.ShapeDtypeStruct((4, T, LANES), F32)
    return _pcall(
        body, name=f"attn_fwd{g}", grid=(4,),
        in_specs=[slab(0), slab(1), slab(2)], out_specs=[out, out], out_shape=[shape, shape],
        compiler_params=_params("parallel"),
    )(qkv, qkv, qkv)


def _attn_merge(outs, lses, cat, tm=1024):
    def body(o0, o1, o2, l0, l1, l2, cat_in, cat_ref, att_ref, w0, w1, w2):
        del cat_in
        la, lb, lc = l0[...], l1[...], l2[...]
        mx = jnp.maximum(jnp.maximum(la, lb), lc)
        ea, eb, ec = jnp.exp(la - mx), jnp.exp(lb - mx), jnp.exp(lc - mx)
        inv = 1.0 / (ea + eb + ec)
        wa, wb, wc = ea * inv, eb * inv, ec * inv
        att = wa * o0[...] + wb * o1[...] + wc * o2[...]
        att_ref[...] = att
        cat_ref[...] = att.astype(BF16)
        w0[...] = wa
        w1[...] = wb
        w2[...] = wc

    slab = pl.BlockSpec((None, tm, LANES), lambda p, i: (p, i, 0))
    shape = jax.ShapeDtypeStruct((4, T, LANES), F32)
    return _pcall(
        body, name="attn_merge", grid=(4, T // tm),
        in_specs=[slab] * 6 + [pl.BlockSpec(memory_space=pl.ANY)],
        out_specs=[pl.BlockSpec((tm, LANES), lambda p, i: (i, CONV_CH // LANES + p)), slab, slab, slab, slab],
        out_shape=[jax.ShapeDtypeStruct((T, D), BF16), shape, shape, shape, shape],
        input_output_aliases={6: 0},
        compiler_params=_params("parallel", "parallel"),
    )(*outs, *lses, cat)


def _attn_bwd(qkv, lse, wgt, att, dcat, dqkv, g):
    d = DILATIONS[g]
    nb = T // d // BAND

    def body(q_ref, k_ref, v_ref, l_ref, w_ref, a_ref, da_ref, dq_in, o_ref):
        del dq_in
        lane = lax.broadcasted_iota(jnp.int32, (BAND, LANES), 1)
        lane_lo = lane < HEAD_DIM
        row = lax.broadcasted_iota(jnp.int32, (LANES, LANES), 0)
        same_head = ((row // HEAD_DIM) == (lane // HEAD_DIM)).astype(BF16)
        dq_ref, dk_ref, dv_ref = o_ref.at[0], o_ref.at[1], o_ref.at[2]
        dk_ref[...] = jnp.zeros((T, LANES), F32)
        dv_ref[...] = jnp.zeros((T, LANES), F32)

        def step(idx, carry):
            r = idx // nb
            n = idx % nb
            cur = _band_rows(n * (BAND * d) + r, d)
            prev = _band_rows(jnp.maximum(n - 1, 0) * (BAND * d) + r, d)
            q = q_ref[cur, :]
            kc = k_ref[cur, :].astype(BF16)
            vc = v_ref[cur, :].astype(BF16)
            kp = k_ref[prev, :].astype(BF16)
            vp = v_ref[prev, :].astype(BF16)
            lse_t = l_ref[cur, :]
            w_t = w_ref[cur, :]
            da = da_ref[cur, :]
            prod = da * a_ref[cur, :]
            hi = prod.astype(BF16)
            lo = (prod - hi.astype(F32)).astype(BF16)
            csum = _nn(hi, same_head) + _nn(lo, same_head)
            mc, mp = _band_masks(n)
            dqs = []
            dkc = jnp.zeros((BAND, LANES), F32)
            dkp = jnp.zeros((BAND, LANES), F32)
            dvc = jnp.zeros((BAND, LANES), F32)
            dvp = jnp.zeros((BAND, LANES), F32)
            for h in range(2):
                hm = lane_lo if h == 0 else jnp.logical_not(lane_lo)
                col0 = h * HEAD_DIM
                lse_h = lse_t[:, col0:col0 + 1]
                w_h = w_t[:, col0:col0 + 1]
                c_h = csum[:, col0:col0 + 1]
                qm = jnp.where(hm, q, 0.0).astype(BF16)
                dam = jnp.where(hm, da, 0.0).astype(BF16)
                pwc = w_h * jnp.exp(jnp.where(mc, _nt(qm, kc) * SCALE, NEG) - lse_h)
                pwp = w_h * jnp.exp(jnp.where(mp, _nt(qm, kp) * SCALE, NEG) - lse_h)
                dsc = (pwc * (_nt(dam, vc) - c_h) * SCALE).astype(BF16)
                dsp = (pwp * (_nt(dam, vp) - c_h) * SCALE).astype(BF16)
                dqs.append(_nn(dsc, kc) + _nn(dsp, kp))
                dkc = dkc + _tn(dsc, qm)
                dkp = dkp + _tn(dsp, qm)
                dvc = dvc + _tn(pwc.astype(BF16), dam)
                dvp = dvp + _tn(pwp.astype(BF16), dam)
            dq_ref[cur, :] = jnp.where(lane_lo, dqs[0], dqs[1])
            dk_ref[cur, :] += dkc
            dk_ref[prev, :] += dkp
            dv_ref[cur, :] += dvc
            dv_ref[prev, :] += dvp
            return carry

        lax.fori_loop(0, d * nb, step, 0)

    def slab(which):
        return pl.BlockSpec((None, None, T, LANES), lambda p: (which * 3 + g, p, 0, 0))

    per_pair = pl.BlockSpec((None, T, LANES), lambda p: (p, 0, 0))
    return _pcall(
        body, name=f"attn_bwd{g}", grid=(4,),
        in_specs=[slab(0), slab(1), slab(2), per_pair, per_pair, per_pair,
                  pl.BlockSpec((T, LANES), lambda p: (0, CONV_CH // LANES + p)),
                  pl.BlockSpec(memory_space=pl.ANY)],
        out_specs=pl.BlockSpec((None, 3, None, T, LANES), lambda p: (g, 0, p, 0, 0)),
        out_shape=jax.ShapeDtypeStruct((3, 3, 4, T, LANES), F32),
        input_output_aliases={7: 0},
        compiler_params=_params("parallel"),
    )(qkv, qkv, qkv, lse, wgt, att, dcat, dqkv)


def _rope_bwd(dqkv, rope_c, rope_s, dz):
    def body(d_ref, c_ref, s_ref, dz_in, o_ref):
        del dz_in
        w = pl.program_id(1)
        v = d_ref[...]
        rot = v * c_ref[...] + _swap_halves(v * s_ref[...])
        o_ref[...] = jnp.where(w < 2, rot, v).astype(BF16)

    tab = pl.BlockSpec((T, LANES), lambda g, w, p: (0, 0))
    return _pcall(
        body, name="rope_bwd", grid=(3, 3, 4),
        in_specs=[pl.BlockSpec((None, None, None, T, LANES), lambda g, w, p: (g, w, p, 0, 0)), tab, tab,
                  pl.BlockSpec(memory_space=pl.ANY)],
        out_specs=pl.BlockSpec((T, LANES), lambda g, w, p: (0, (2 * CONV_CH) // LANES + (w * 3 + g) * 4 + p)),
        out_shape=jax.ShapeDtypeStruct((T, EVEN_IN), BF16),
        input_output_aliases={3: 0},
        compiler_params=_params("parallel", "parallel", "parallel"),
    )(dqkv, rope_c, rope_s, dz)


ODD_TILE = 256
ODD_HALO = 8
GELU_C = 0.7978845608028654
GELU_A = 0.044715


def _gelu(x):
    return 0.5 * x * (1.0 + jnp.tanh(GELU_C * (x + GELU_A * x * x * x)))


def _gelu_grad(x):
    th = jnp.tanh(GELU_C * (x + GELU_A * x * x * x))
    return 0.5 * (1.0 + th) + 0.5 * x * (1.0 - th * th) * GELU_C * (1.0 + 3.0 * GELU_A * x * x)


def _tril():
    row = lax.broadcasted_iota(jnp.int32, (CHUNK, CHUNK), 0)
    col = lax.broadcasted_iota(jnp.int32, (CHUNK, CHUNK), 1)
    return (col <= row).astype(F32)


def _odd_parts(z, zh, i, k_ref, g_ref, be_ref, w_ref, bt_ref):
    R, H = ODD_TILE, ODD_HALO
    gb, gc, xs, uv = z[:, :512], z[:, 512:1024], z[:, 1024:1536], z[:, 1536:]
    halo = zh[:, 512:1024] * zh[:, 1024:1536] * (i > 0).astype(F32)
    win = jnp.concatenate([halo, gc * xs], axis=0)
    cv = jnp.zeros((R, 512), F32)
    for j in range(SCONV_W):
        off = H - (SCONV_W - 1) + j
        cv = cv + k_ref[j:j + 1, :] * win[off:off + R, :]
    ge = _gelu(uv)
    u, v = ge[:, :512], ge[:, 512:]
    mu = jnp.mean(v, axis=-1, keepdims=True)
    xc = v - mu
    rstd = lax.rsqrt(jnp.mean(xc * xc, axis=-1, keepdims=True) + EPS)
    xh = xc * rstd
    vn = xh * g_ref[...] + be_ref[...]
    tril = _tril()
    wms = [(w_ref[g] * tril).astype(BF16) for g in range(SG_GROUPS)]
    rows = []
    for ci in range(R // CHUNK):
        blocks = []
        for g in range(SG_GROUPS):
            blk = vn[ci * CHUNK:(ci + 1) * CHUNK, g * LANES:(g + 1) * LANES].astype(BF16)
            blocks.append(_nn(wms[g], blk) + bt_ref[:, g:g + 1])
        rows.append(jnp.concatenate(blocks, axis=1))
    vmix = jnp.concatenate(rows, axis=0)
    return gb, gc, xs, uv, win, cv, u, rstd, xh, vn, vmix, wms


def _odd_mid_fwd(z, conv_k, ln_g, ln_b, sg_w, sg_bt):
    R, H = ODD_TILE, ODD_HALO

    def body(z_ref, zh_ref, k_ref, g_ref, be_ref, w_ref, bt_ref, o_ref):
        i = pl.program_id(0)
        gb, _, _, _, _, cv, u, _, _, _, vmix, _ = _odd_parts(z_ref[...], zh_ref[...], i, k_ref, g_ref, be_ref, w_ref, bt_ref)
        o_ref[...] = jnp.concatenate([gb * cv, u * vmix], axis=1).astype(BF16)

    vec = pl.BlockSpec((1, 512), lambda i: (0, 0))
    return _pcall(
        body, name="odd_mid_fwd", grid=(T // R,),
        in_specs=[pl.BlockSpec((R, ODD_IN), lambda i: (i, 0)),
                  pl.BlockSpec((H, ODD_IN), lambda i: (jnp.maximum(i * (R // H) - 1, 0), 0)),
                  pl.BlockSpec((SCONV_W, 512), lambda i: (0, 0)), vec, vec,
                  pl.BlockSpec((SG_GROUPS, CHUNK, CHUNK), lambda i: (0, 0, 0)),
                  pl.BlockSpec((CHUNK, SG_GROUPS), lambda i: (0, 0))],
        out_specs=pl.BlockSpec((R, D), lambda i: (i, 0)),
        out_shape=jax.ShapeDtypeStruct((T, D), BF16),
        compiler_params=_params("parallel"),
    )(z, z, conv_k, ln_g, ln_b, sg_w, sg_bt)


def _odd_mid_bwd(z, dcat, conv_k, ln_g, ln_b, sg_w, sg_bt):
    R, H = ODD_TILE, ODD_HALO
    last = T // R - 1

    def body(z_ref, zh_ref, zn_ref, d_ref, dn_ref, k_ref, g_ref, be_ref, w_ref, bt_ref,
             dz_ref, dk_ref, dg_ref, dbe_ref, dw_ref, dbt_ref):
        i = pl.program_id(0)
        z = z_ref[...]
        gb, gc, xs, uv, win, cv, u, rstd, xh, vn, vmix, wms = _odd_parts(z, zh_ref[...], i, k_ref, g_ref, be_ref, w_ref, bt_ref)
        dcat_t = d_ref[...]
        dc, dd = dcat_t[:, :512], dcat_t[:, 512:]

        @pl.when(i == 0)
        def _():
            dk_ref[...] = jnp.zeros_like(dk_ref)
            dg_ref[...] = jnp.zeros_like(dg_ref)
            dbe_ref[...] = jnp.zeros_like(dbe_ref)
            dw_ref[...] = jnp.zeros_like(dw_ref)
            dbt_ref[...] = jnp.zeros_like(dbt_ref)

        dgb = dc * cv
        dcv = dc * gb
        nxt = dn_ref[:, :512] * zn_ref[:, :512] * (i < last).astype(F32)
        winb = jnp.concatenate([dcv, nxt], axis=0)
        dp = jnp.zeros((R, 512), F32)
        for j in range(SCONV_W):
            off = H - (SCONV_W - 1) + j
            dk_ref[j:j + 1, :] += jnp.sum(dcv * win[off:off + R, :], axis=0, keepdims=True)
            ob = SCONV_W - 1 - j
            dp = dp + k_ref[j:j + 1, :] * winb[ob:ob + R, :]
        dgc = dp * xs
        dxs = dp * gc
        du = dd * vmix
        dvmix = dd * u
        tril = _tril()
        rows = []
        for ci in range(R // CHUNK):
            blocks = []
            for g in range(SG_GROUPS):
                sl = (slice(ci * CHUNK, (ci + 1) * CHUNK), slice(g * LANES, (g + 1) * LANES))
                dblk = dvmix[sl]
                dblk16 = dblk.astype(BF16)
                blocks.append(_tn(wms[g], dblk16))
                dw_ref[g] += _nt(dblk16, vn[sl].astype(BF16)) * tril
                dbt_ref[:, g:g + 1] += jnp.sum(dblk, axis=1, keepdims=True)
            rows.append(jnp.concatenate(blocks, axis=1))
        dvn = jnp.concatenate(rows, axis=0)
        dg_ref[...] += jnp.sum(dvn * xh, axis=0, keepdims=True)
        dbe_ref[...] += jnp.sum(dvn, axis=0, keepdims=True)
        dxh = dvn * g_ref[...]
        dv = rstd * (dxh - jnp.mean(dxh, axis=-1, keepdims=True) - xh * jnp.mean(dxh * xh, axis=-1, keepdims=True))
        duv = jnp.concatenate([du, dv], axis=1) * _gelu_grad(uv)
        dz_ref[...] = jnp.concatenate([dgb, dgc, dxs, duv], axis=1).astype(BF16)

    vec = pl.BlockSpec((1, 512), lambda i: (0, 0))
    kspec = pl.BlockSpec((SCONV_W, 512), lambda i: (0, 0))
    wspec = pl.BlockSpec((SG_GROUPS, CHUNK, CHUNK), lambda i: (0, 0, 0))
    bspec = pl.BlockSpec((CHUNK, SG_GROUPS), lambda i: (0, 0))
    nxt_blk = lambda i: (jnp.minimum((i + 1) * (R // H), T // H - 1), 0)
    return _pcall(
        body, name="odd_mid_bwd", grid=(T // R,),
        in_specs=[pl.BlockSpec((R, ODD_IN), lambda i: (i, 0)),
                  pl.BlockSpec((H, ODD_IN), lambda i: (jnp.maximum(i * (R // H) - 1, 0), 0)),
                  pl.BlockSpec((H, ODD_IN), nxt_blk),
                  pl.BlockSpec((R, D), lambda i: (i, 0)),
                  pl.BlockSpec((H, D), nxt_blk),
                  kspec, vec, vec, wspec, bspec],
        out_specs=[pl.BlockSpec((R, ODD_IN), lambda i: (i, 0)), kspec, vec, vec, wspec, bspec],
        out_shape=[jax.ShapeDtypeStruct((T, ODD_IN), BF16), jax.ShapeDtypeStruct((SCONV_W, 512), F32),
                   jax.ShapeDtypeStruct((1, 512), F32), jax.ShapeDtypeStruct((1, 512), F32),
                   jax.ShapeDtypeStruct((SG_GROUPS, CHUNK, CHUNK), F32), jax.ShapeDtypeStruct((CHUNK, SG_GROUPS), F32)],
        compiler_params=_params("arbitrary"),
    )(z, z, z, dcat, dcat, conv_k, ln_g, ln_b, sg_w, sg_bt)


def _ffn_fwd(tag, h, g, weight):
    hn = _rms_fwd(f"ffn{tag}_norm", h, g)

    def act(acc):
        r = jnp.maximum(acc, 0.0)
        return acc, r * r

    u, f = _mm(f"ffn{tag}_up", "nn", hn, weight(f"ffn_w1_{tag}", hn), T, D_FF, D, (F32, BF16), epi=act)
    out = _mm(f"ffn{tag}_down", "nn", f, weight(f"ffn_w2_{tag}", f), T, D, D_FF, (F32,),
              epi=lambda acc, res: (acc + res,), extras=(h,))
    return out, (hn, u, f)


def _ffn_bwd(tag, h, g, weight, emit, saved, dout):
    hn, u, f = saved
    du = _mm(f"ffn{tag}_dact", "nt", dout, weight(f"ffn_w2_{tag}", dout), T, D_FF, D, (BF16,),
             epi=lambda acc, uu: (acc * (2.0 * jnp.maximum(uu, 0.0)),), extras=(u,))
    tok = emit(f"ffn_w2_{tag}", _mm(f"ffn{tag}_dw2", "tn", f, dout, D_FF, D, T, (BF16,)))
    tok = emit(f"ffn_w1_{tag}", _mm(f"ffn{tag}_dw1", "tn", hn, du, D, D_FF, T, (BF16,), tie=tok))
    dhn = _mm(f"ffn{tag}_dhn", "nt", du, weight(f"ffn_w1_{tag}", du), T, D, D_FF, (F32,), tie=tok)
    return _rms_bwd(f"ffn{tag}_dnorm", h, dhn, g, dout)


def _rope_tables():
    half = HEAD_DIM // 2
    inv = 10000.0 ** (-jnp.arange(half, dtype=F32) / half)
    ang = jnp.arange(T, dtype=F32)[:, None] * inv[None, :]
    cos, sin = jnp.cos(ang), jnp.sin(ang)
    c = jnp.tile(jnp.concatenate([cos, cos], axis=1), (1, LANES // HEAD_DIM))
    s = jnp.tile(jnp.concatenate([-sin, sin], axis=1), (1, LANES // HEAD_DIM))
    return c, s


def _local_step(x, target, p, weight, emit):
    rope_c, rope_s = _rope_tables()
    grads = {}
    residual = lambda acc, res: (acc + res,)

    hn0 = _rms_fwd("mix0_norm", x, p["norm_mix_g0"])
    zc = _mm("even_in_conv", "nn", hn0, weight("even_w_in", hn0), T, 2 * CONV_CH, D, (F32,))
    qkv = _qkv_proj(hn0, weight("even_w_in", hn0), rope_c, rope_s)
    cv, cat0 = _econv_fwd(zc, p["even_conv_k"], p["even_conv_b"], p["even_ln_g"], p["even_ln_b"])
    att_parts = [_attn_fwd(qkv, g) for g in range(3)]
    outs = [a[0] for a in att_parts]
    lses = [a[1] for a in att_parts]
    cat0, att, w0, w1, w2 = _attn_merge(outs, lses, cat0)
    wgts = (w0, w1, w2)
    h1 = _mm("even_out", "nn", cat0, weight("even_w_out", cat0), T, D, D, (F32,), epi=residual, extras=(x,))
    h2, ffn0_saved = _ffn_fwd(0, h1, p["norm_ffn_g0"], weight)

    hn1 = _rms_fwd("mix1_norm", h2, p["norm_mix_g1"])
    z1 = _mm("odd_in", "nn", hn1, weight("odd_w_in", hn1), T, ODD_IN, D, (F32,))
    cat1 = _odd_mid_fwd(z1, p["odd_conv_k"], p["odd_ln_g"], p["odd_ln_b"], p["odd_sg_w"], p["odd_sg_bt"])
    h3 = _mm("odd_out", "nn", cat1, weight("odd_w_out", cat1), T, D, D, (F32,), epi=residual, extras=(h2,))
    h4, ffn1_saved = _ffn_fwd(1, h3, p["norm_ffn_g1"], weight)

    dh4, grads["final_g"], loss = _loss_head(h4, p["final_g"], target)

    dh3, grads["norm_ffn_g1"] = _ffn_bwd(1, h3, p["norm_ffn_g1"], weight, emit, ffn1_saved, dh4)
    tok = emit("odd_w_out", _mm("odd_out_dw", "tn", cat1, dh3, D, D, T, (BF16,)))
    dcat1 = _mm("odd_out_dx", "nt", dh3, weight("odd_w_out", dh3), T, D, D, (F32,), tie=tok)
    dz1, grads["odd_conv_k"], grads["odd_ln_g"], grads["odd_ln_b"], grads["odd_sg_w"], grads["odd_sg_bt"] = _odd_mid_bwd(
        z1, dcat1, p["odd_conv_k"], p["odd_ln_g"], p["odd_ln_b"], p["odd_sg_w"], p["odd_sg_bt"])
    tok = emit("odd_w_in", _mm("odd_in_dw", "tn", hn1, dz1, D, ODD_IN, T, (BF16,)))
    dhn1 = _mm("odd_in_dx", "nt", dz1, weight("odd_w_in", dz1), T, D, ODD_IN, (F32,), tie=tok)
    dh2, grads["norm_mix_g1"] = _rms_bwd("mix1_dnorm", h2, dhn1, p["norm_mix_g1"], dh3)

    dh1, grads["norm_ffn_g0"] = _ffn_bwd(0, h1, p["norm_ffn_g0"], weight, emit, ffn0_saved, dh2)
    tok = emit("even_w_out", _mm("even_out_dw", "tn", cat0, dh1, D, D, T, (BF16,)))
    dcat0 = _mm("even_out_dx", "nt", dh1, weight("even_w_out", dh1), T, D, D, (F32,), tie=tok)
    dcv, grads["even_ln_g"], grads["even_ln_b"], grads["even_conv_b"] = _econv_bwd_ln(
        cv, dcat0, p["even_ln_g"], p["even_ln_b"])
    dz0, grads["even_conv_k"] = _econv_bwd_conv(dcv, zc, p["even_conv_k"])
    dqkv = lax.empty((3, 3, 4, T, LANES), F32)
    for g in range(3):
        dqkv = _attn_bwd(qkv, lses[g], wgts[g], att, dcat0, dqkv, g)
    dz0 = _rope_bwd(dqkv, rope_c, rope_s, dz0)
    tok = emit("even_w_in", _mm("even_in_dw", "tn", hn0, dz0, D, EVEN_IN, T, (BF16,)))
    dhn0 = _mm("even_in_dx", "nt", dz0, weight("even_w_in", dz0), T, D, EVEN_IN, (F32,), tm=256, tie=tok)
    dx, grads["norm_mix_g0"] = _rms_bwd("mix0_dnorm", x, dhn0, p["norm_mix_g0"], dh1)
    return loss[0, 0], dx, grads


def _rowwise(name, fn, ins, out_dtypes, tm=256):
    rows, cols = ins[0].shape
    tm = tm if rows % tm == 0 else rows
    n_in = len(ins)

    def body(*refs):
        vals = fn(*[r[...] for r in refs[:n_in]])
        for o_ref, v in zip(refs[n_in:], vals):
            o_ref[...] = v.astype(o_ref.dtype)

    spec = pl.BlockSpec((tm, cols), lambda i: (i, 0))
    outs = _pcall(
        body, name=name, grid=(rows // tm,),
        in_specs=[spec] * n_in, out_specs=[spec] * len(out_dtypes),
        out_shape=[jax.ShapeDtypeStruct((rows, cols), dt) for dt in out_dtypes],
        compiler_params=_params("parallel"),
    )(*ins)
    return outs[0] if len(out_dtypes) == 1 else outs


def _adamw(name, w, g, m, v):
    c1 = 1.0 - ADAM_B1 ** ADAM_STEP
    c2 = 1.0 - ADAM_B2 ** ADAM_STEP

    def fn(w_t, g_t, m_t, v_t):
        m_new = ADAM_B1 * m_t + (1.0 - ADAM_B1) * g_t
        v_new = ADAM_B2 * v_t + (1.0 - ADAM_B2) * (g_t * g_t)
        delta = -ADAM_LR * ((m_new / c1) / (jnp.sqrt(v_new / c2) + ADAM_EPS) + ADAM_WD * w_t)
        return delta, m_new, v_new

    return _rowwise(name, fn, (w, g, m, v), (F32, F32, F32))


class _Piece:
    def __init__(self, name, rows, cols, axis, src, src_row0):
        self.name, self.rows, self.cols, self.axis = name, rows, cols, axis
        self.width = (cols if axis == 1 else rows) // 4
        self.src, self.src_row0 = src, src_row0

    @property
    def full_shape(self):
        return (self.rows, self.cols)

    @property
    def half_shape(self):
        return (self.rows // 2, self.cols) if self.axis == 1 else (self.rows, self.cols // 2)

    @property
    def shard_half_shape(self):
        return (self.rows // 2, self.width) if self.axis == 1 else (self.width, self.cols // 2)

    def shard_whole(self, ref):
        n = self.rows if self.axis == 1 else self.width
        return ref.at[pl.ds(self.src_row0, n), :]

    def shard_half(self, ref, h):
        if self.axis == 1:
            return ref.at[pl.ds(self.src_row0 + h * (self.rows // 2), self.rows // 2), :]
        return ref.at[pl.ds(self.src_row0, self.width), pl.ds(h * (self.cols // 2), self.cols // 2)]

    def full_shard(self, ref, s):
        if self.axis == 1:
            return ref.at[:, pl.ds(s * self.width, self.width)]
        return ref.at[pl.ds(s * self.width, self.width), :]

    def full_shard_half(self, ref, s, h):
        if self.axis == 1:
            return ref.at[pl.ds(h * (self.rows // 2), self.rows // 2), pl.ds(s * self.width, self.width)]
        return ref.at[pl.ds(s * self.width, self.width), pl.ds(h * (self.cols // 2), self.cols // 2)]

    def full_half(self, ref, h):
        if self.axis == 1:
            return ref.at[pl.ds(h * (self.rows // 2), self.rows // 2), :]
        return ref.at[:, pl.ds(h * (self.cols // 2), self.cols // 2)]

    def full_half_rows(self, ref, h, r0, n):
        if self.axis == 1:
            return ref.at[pl.ds(h * (self.rows // 2) + r0, n), :]
        return ref.at[pl.ds(r0, n), pl.ds(h * (self.cols // 2), self.cols // 2)]

    def half_shard(self, ref, s):
        return self.full_shard(ref, s)


PIECES = (
    _Piece("even_w_in", D, EVEN_IN, 1, 0, 0),
    _Piece("even_w_out", D, D, 0, 1, 0),
    _Piece("odd_w_in", D, ODD_IN, 1, 2, 0),
    _Piece("odd_w_out", D, D, 0, 3, 0),
    _Piece("ffn_w1_0", D, D_FF, 1, 4, 0),
    _Piece("ffn_w1_1", D, D_FF, 1, 4, D),
    _Piece("ffn_w2_0", D_FF, D, 0, 5, 0),
    _Piece("ffn_w2_1", D_FF, D, 0, 5, D_FF // 4),
)
N_PIECES = len(PIECES)
N_SHARD_OPERANDS = 6
ANY = pl.BlockSpec(memory_space=pl.ANY)
MESH = pl.DeviceIdType.MESH


def _mesh_place():
    x, y, c = lax.axis_index("x"), lax.axis_index("y"), lax.axis_index("c")
    chips = [(1 - x, y), (x, 1 - y), (1 - x, 1 - y)]
    return x, y, c, chips


def _remote(src, dst, send_sem, recv_sem, dev):
    return pltpu.make_async_remote_copy(src_ref=src, dst_ref=dst, send_sem=send_sem, recv_sem=recv_sem,
                                        device_id=dev, device_id_type=MESH)


HBM = pl.BlockSpec(memory_space=pltpu.HBM)
SEM = pl.BlockSpec(memory_space=pltpu.SEMAPHORE)
SPLIT_PARAMS = pltpu.CompilerParams(has_side_effects=pltpu.SideEffectType.DATAFLOW_SIDE_EFFECTING)
CAST_TILE = 256


def _in_hbm(a):
    return pltpu.with_memory_space_constraint(a, pltpu.HBM)


def _cast_place(pc, shard_operand, chip):
    rows, cols = (pc.rows, pc.width) if pc.axis == 1 else (pc.width, pc.cols)
    nblk = rows // CAST_TILE
    blk0 = pc.src_row0 // CAST_TILE

    def body(chip_ref, x_ref, o_ref):
        del chip_ref
        o_ref[...] = x_ref[...].astype(BF16)

    if pc.axis == 1:
        out_map = lambda i, chip_ref: (i, chip_ref[0])
    else:
        out_map = lambda i, chip_ref: (chip_ref[0] * nblk + i, 0)
    return _pcall(
        body, name=f"cast_{pc.name}",
        grid_spec=pltpu.PrefetchScalarGridSpec(
            num_scalar_prefetch=1, grid=(nblk,),
            in_specs=[pl.BlockSpec((CAST_TILE, cols), lambda i, chip_ref: (blk0 + i, 0))],
            out_specs=pl.BlockSpec((CAST_TILE, cols), out_map)),
        out_shape=jax.ShapeDtypeStruct(pc.full_shape, BF16),
        compiler_params=_params("parallel"),
    )(chip, shard_operand)


def _gather_start(fulls):
    def body(*refs):
        ins = refs[:N_PIECES]
        sends = refs[2 * N_PIECES:3 * N_PIECES]
        recvs = refs[3 * N_PIECES:4 * N_PIECES]
        token = refs[4 * N_PIECES]
        x, y, c, chips = _mesh_place()
        s = 2 * x + y
        for i, pc in enumerate(PIECES):
            win = pc.full_shard_half(ins[i], s, c)
            for k, (cx, cy) in enumerate(chips):
                _remote(win, win, sends[i].at[k], recvs[i].at[k], (cx, cy, c)).start()
        token[...] = jnp.zeros(TOKEN_SHAPE, F32)

    sems = [pltpu.SemaphoreType.DMA((3,))] * (2 * N_PIECES)
    outs = _pcall(
        body, name="gather_start",
        in_specs=[HBM] * N_PIECES,
        out_specs=[HBM] * N_PIECES + [SEM] * (2 * N_PIECES) + [pl.BlockSpec(memory_space=pltpu.VMEM)],
        out_shape=[pltpu.HBM(pc.full_shape, BF16) for pc in PIECES] + sems + [jax.ShapeDtypeStruct(TOKEN_SHAPE, F32)],
        input_output_aliases={i: i for i in range(N_PIECES)},
        compiler_params=SPLIT_PARAMS,
    )(*[_in_hbm(f) for f in fulls])
    return outs[:N_PIECES], outs[N_PIECES:2 * N_PIECES], outs[2 * N_PIECES:3 * N_PIECES], outs[3 * N_PIECES]


def _gather_wait(pc, full, send_sems, recv_sems, after):
    def body(full_ref, send_ref, recv_ref, after_ref, out_ref):
        del after_ref, out_ref
        x, y, c, chips = _mesh_place()
        for k, (cx, cy) in enumerate(chips):
            win = pc.full_shard_half(full_ref, 2 * cx + cy, c)
            cp = _remote(win, win, send_ref.at[k], recv_ref.at[k], (cx, cy, c))
            cp.wait_send()
            cp.wait_recv()

    return _pcall(
        body, name=f"gather_wait_{pc.name}",
        in_specs=[HBM, SEM, SEM, ANY], out_specs=HBM, out_shape=pltpu.HBM(pc.full_shape, BF16),
        input_output_aliases={0: 0}, compiler_params=SPLIT_PARAMS,
    )(full, send_sems, recv_sems, after)


def _core_forward(pc, full):
    sh = pc.shard_half_shape

    def body(full_in, full_ref, send_buf, recv_buf, load_sems, send_sems, recv_sems, store_sems):
        x, y, c, chips = _mesh_place()
        loads, sends, stores = [], [], []
        for k, (cx, cy) in enumerate(chips):
            cp = pltpu.make_async_copy(pc.full_shard_half(full_in, 2 * cx + cy, c), send_buf.at[k], load_sems.at[k])
            cp.start()
            loads.append(cp)
        for k in range(3):
            loads[k].wait()
            cp = _remote(send_buf.at[k], recv_buf.at[k], send_sems.at[k], recv_sems.at[k], (x, y, 1 - c))
            cp.start()
            sends.append(cp)
        for k, (cx, cy) in enumerate(chips):
            sends[k].wait_recv()
            cp = pltpu.make_async_copy(recv_buf.at[k], pc.full_shard_half(full_ref, 2 * cx + cy, 1 - c), store_sems.at[k])
            cp.start()
            stores.append(cp)
        for k in range(3):
            sends[k].wait_send()
            stores[k].wait()

    sems = pltpu.SemaphoreType.DMA((3,))
    return _pcall(
        body, name=f"core_forward_{pc.name}", in_specs=[ANY], out_specs=ANY,
        out_shape=jax.ShapeDtypeStruct(pc.full_shape, BF16),
        scratch_shapes=[pltpu.VMEM((3,) + sh, BF16), pltpu.VMEM((3,) + sh, BF16), sems, sems, sems, sems],
        input_output_aliases={0: 0},
        compiler_params=pltpu.CompilerParams(vmem_limit_bytes=VMEM_LIMIT),
    )(full)


CHIPSUM_CHUNKS = 4


def _chipsum(pc, partial):
    hr, hc = pc.half_shape
    ch = hr // CHIPSUM_CHUNKS

    def body(g_ref, out_ref, send_buf, recv_buf, own_buf, sum_buf, load_sems, own_sems, send_sems, recv_sems, out_sems):
        x, y, c, _ = _mesh_place()
        chunks = [pl.ds(k * ch, ch) for k in range(CHIPSUM_CHUNKS)]
        loads, owns, sends, stores = [], [], [], []
        for k, rows in enumerate(chunks):
            cp = pltpu.make_async_copy(pc.full_half_rows(g_ref, 1 - c, k * ch, ch), send_buf.at[rows, :], load_sems.at[k])
            cp.start()
            loads.append(cp)
            cp = pltpu.make_async_copy(pc.full_half_rows(g_ref, c, k * ch, ch), own_buf.at[rows, :], own_sems.at[k])
            cp.start()
            owns.append(cp)
        for k, rows in enumerate(chunks):
            loads[k].wait()
            cp = _remote(send_buf.at[rows, :], recv_buf.at[rows, :], send_sems.at[k], recv_sems.at[k], (x, y, 1 - c))
            cp.start()
            sends.append(cp)
        for k, rows in enumerate(chunks):
            owns[k].wait()
            sends[k].wait_recv()
            sum_buf[rows, :] = (own_buf[rows, :].astype(F32) + recv_buf[rows, :].astype(F32)).astype(BF16)
            cp = pltpu.make_async_copy(sum_buf.at[rows, :], out_ref.at[rows, :], out_sems.at[k])
            cp.start()
            stores.append(cp)
        for k in range(CHIPSUM_CHUNKS):
            sends[k].wait_send()
            stores[k].wait()

    buf = pltpu.VMEM((hr, hc), BF16)
    sems = pltpu.SemaphoreType.DMA((CHIPSUM_CHUNKS,))
    return _pcall(
        body, name=f"chipsum_{pc.name}", in_specs=[ANY], out_specs=ANY,
        out_shape=jax.ShapeDtypeStruct((hr, hc), BF16),
        scratch_shapes=[buf, buf, buf, buf, sems, sems, sems, sems, sems],
        compiler_params=pltpu.CompilerParams(vmem_limit_bytes=VMEM_LIMIT),
    )(partial)


def _scatter_start(pc, chip_sum):
    def body(sum_ref, land_ref, sum_out, land_out, sends, recvs, token):
        del sum_out, land_out
        x, y, c, chips = _mesh_place()
        for k, (cx, cy) in enumerate(chips):
            _remote(pc.half_shard(sum_ref, 2 * cx + cy), land_ref.at[k], sends.at[k], recvs.at[k], (cx, cy, c)).start()
        token[...] = jnp.zeros(TOKEN_SHAPE, F32)

    land_shape = (3,) + pc.shard_half_shape
    sems = pltpu.SemaphoreType.DMA((3,))
    return _pcall(
        body, name=f"scatter_start_{pc.name}",
        in_specs=[HBM, HBM], out_specs=[HBM, HBM, SEM, SEM, pl.BlockSpec(memory_space=pltpu.VMEM)],
        out_shape=[pltpu.HBM(pc.half_shape, BF16), pltpu.HBM(land_shape, BF16), sems, sems,
                   jax.ShapeDtypeStruct(TOKEN_SHAPE, F32)],
        input_output_aliases={0: 0, 1: 1}, compiler_params=SPLIT_PARAMS,
    )(_in_hbm(chip_sum), _in_hbm(lax.empty(land_shape, BF16)))


def _scatter_wait(pc, chip_sum, land, send_sems, recv_sems, after):
    def body(sum_ref, land_ref, send_ref, recv_ref, after_ref, sum_out, land_out):
        del after_ref, sum_out, land_out
        x, y, c, chips = _mesh_place()
        for k, (cx, cy) in enumerate(chips):
            cp = _remote(pc.half_shard(sum_ref, 2 * cx + cy), land_ref.at[k], send_ref.at[k], recv_ref.at[k], (cx, cy, c))
            cp.wait_send()
            cp.wait_recv()

    return _pcall(
        body, name=f"scatter_wait_{pc.name}",
        in_specs=[HBM, HBM, SEM, SEM, ANY], out_specs=[HBM, HBM],
        out_shape=[pltpu.HBM(pc.half_shape, BF16), pltpu.HBM((3,) + pc.shard_half_shape, BF16)],
        input_output_aliases={0: 0, 1: 1}, compiler_params=SPLIT_PARAMS,
    )(chip_sum, land, send_sems, recv_sems, after)


SHARD_OPERAND_SHAPES = ((D, EVEN_IN // 4), (D // 4, D), (D, ODD_IN // 4), (D // 4, D), (2 * D, D_FF // 4), (2 * D_FF // 4, D))


def _allsum_join(operand, chip_sums, lands):
    pieces = [pc for pc in PIECES if pc.src == operand]
    n = len(pieces)

    def body(*refs):
        sum_refs = refs[:n]
        land_refs = refs[n:2 * n]
        refs = refs[n:]
        out_ref = refs[n]
        in_bufs = refs[n + 1:2 * n + 1]
        fin_bufs = refs[2 * n + 1:3 * n + 1]
        recv_bufs = refs[3 * n + 1:4 * n + 1]
        load_sems, send_sems, recv_sems, out_sems = refs[4 * n + 1:]
        x, y, c, _ = _mesh_place()
        s = 2 * x + y
        loads, sends, stores = [], [], []
        for j, pc in enumerate(pieces):
            cp = pltpu.make_async_copy(land_refs[j], in_bufs[j].at[pl.ds(0, 3)], load_sems.at[2 * j])
            cp.start()
            loads.append(cp)
            cp = pltpu.make_async_copy(pc.half_shard(sum_refs[j], s), in_bufs[j].at[3], load_sems.at[2 * j + 1])
            cp.start()
            loads.append(cp)
        for j, pc in enumerate(pieces):
            loads[2 * j].wait()
            loads[2 * j + 1].wait()
            acc = in_bufs[j][0].astype(F32)
            for k in range(1, 4):
                acc = acc + in_bufs[j][k].astype(F32)
            fin_bufs[j][...] = acc
            cp = pltpu.make_async_copy(fin_bufs[j], pc.shard_half(out_ref, c), out_sems.at[2 * j])
            cp.start()
            stores.append(cp)
            cp = _remote(fin_bufs[j], recv_bufs[j], send_sems.at[j], recv_sems.at[j], (x, y, 1 - c))
            cp.start()
            sends.append(cp)
        for j, pc in enumerate(pieces):
            sends[j].wait_recv()
            cp = pltpu.make_async_copy(recv_bufs[j], pc.shard_half(out_ref, 1 - c), out_sems.at[2 * j + 1])
            cp.start()
            stores.append(cp)
        for cp in sends:
            cp.wait_send()
        for cp in stores:
            cp.wait()

    sh = pieces[0].shard_half_shape
    return _pcall(
        body, name=f"allsum_join_{operand}", in_specs=[ANY] * (2 * n), out_specs=ANY,
        out_shape=jax.ShapeDtypeStruct(SHARD_OPERAND_SHAPES[operand], F32),
        scratch_shapes=[pltpu.VMEM((4,) + sh, BF16)] * n + [pltpu.VMEM(sh, F32)] * (2 * n)
        + [pltpu.SemaphoreType.DMA((2 * n,)), pltpu.SemaphoreType.DMA((n,)), pltpu.SemaphoreType.DMA((n,)),
           pltpu.SemaphoreType.DMA((2 * n,))],
        compiler_params=pltpu.CompilerParams(vmem_limit_bytes=VMEM_LIMIT),
    )(*chip_sums, *lands)


def _allgather8(name, blk, with_sum):
    m = blk.shape[0]

    def body(x_ref, out_ref, *rest):
        if with_sum:
            sum_ref, send_sems, recv_sems, local_sem = rest
        else:
            send_sems, recv_sems, local_sem = rest
        x, y, c, chips = _mesh_place()
        me, sibling = (x, y, c), (x, y, 1 - c)

        def rows(px, py, pc):
            return out_ref.at[pl.ds((4 * px + 2 * py + pc) * m, m), :]

        def copy(k, block, to, src=None):
            return _remote(rows(*block) if src is None else src, rows(*block), send_sems.at[k], recv_sems.at[k], to)

        mine = pltpu.make_async_copy(x_ref, rows(*me), local_sem)
        mine.start()
        first = [copy(0, me, sibling, src=x_ref)]
        first += [copy(1 + j, me, (*chip, c), src=x_ref) for j, chip in enumerate(chips)]
        for cp in first:
            cp.start()
        passed = [copy(4 + j, (*chip, c), sibling) for j, chip in enumerate(chips)]
        for j, chip in enumerate(chips):
            copy(1 + j, (*chip, c), me).wait_recv()
            passed[j].start()
        copy(0, sibling, me).wait_recv()
        for j, chip in enumerate(chips):
            copy(4 + j, (*chip, 1 - c), me).wait_recv()
        for cp in first + passed:
            cp.wait_send()
        mine.wait()
        if with_sum:
            acc = out_ref[0:m, :]
            for dev in range(1, 8):
                acc = acc + out_ref[dev * m:(dev + 1) * m, :]
            sum_ref[...] = acc

    vm = pl.BlockSpec(memory_space=pltpu.VMEM)
    out_shape = [jax.ShapeDtypeStruct((8 * m, LANES), F32)]
    if with_sum:
        out_shape.append(jax.ShapeDtypeStruct((m, LANES), F32))
    return _pcall(
        body, name=name, in_specs=[vm], out_specs=[vm] * len(out_shape), out_shape=out_shape,
        scratch_shapes=[pltpu.SemaphoreType.DMA((7,)), pltpu.SemaphoreType.DMA((7,)), pltpu.SemaphoreType.DMA],
    )(blk)


def _pack(arrays, row_counts):
    rows = []
    for a, n in zip(arrays, row_counts):
        flat = a.reshape(-1, LANES)
        rows.append(jnp.pad(flat, ((0, n - flat.shape[0]), (0, 0))))
    return jnp.concatenate(rows, axis=0)


def _unpack(buf, shapes, row_counts):
    out, r0 = [], 0
    for sh, n in zip(shapes, row_counts):
        size = 1
        for dim in sh:
            size *= dim
        out.append(buf[r0:r0 + size // LANES].reshape(sh))
        r0 += n
    return out


REPL_NAMES = ("norm_mix_g", "norm_ffn_g", "even_conv_b", "even_ln_g", "even_ln_b", "odd_sg_w", "odd_sg_b", "final_g")
REPL_SHAPES = ((2, D), (2, D), (1, 512), (1, 512), (1, 512), (1, SG_GROUPS, CHUNK, CHUNK), (1, SG_GROUPS, CHUNK), (D,))
REPL_ROWS = (16, 16, 8, 8, 8, 512, 8, 8)
SHARDED_NAMES = ("even_conv_k", "odd_conv_k", "odd_ln_g", "odd_ln_b")
SHARDED_SHARD_SHAPES = ((1, CONV_W, LANES), (1, SCONV_W, LANES), (1, LANES), (1, LANES))
SHARDED_SHARD_ROWS = (32, 8, 8, 8)
SHARDED_FULL_SHAPES = ((CONV_W, 512), (SCONV_W, 512), (1, 512), (1, 512))
SHARDED_FULL_ROWS = (128, 16, 8, 8)


def kernel(x, norm_mix_g, norm_ffn_g, even_w_in, even_conv_k, even_conv_b, even_ln_g, even_ln_b, even_w_out, odd_w_in, odd_conv_k, odd_ln_g, odd_ln_b, odd_sg_w, odd_sg_b, odd_w_out, ffn_w1, ffn_w2, final_g, loss_target, m_norm_mix_g, m_norm_ffn_g, m_even_w_in, m_even_conv_k, m_even_conv_b, m_even_ln_g, m_even_ln_b, m_even_w_out, m_odd_w_in, m_odd_conv_k, m_odd_ln_g, m_odd_ln_b, m_odd_sg_w, m_odd_sg_b, m_odd_w_out, m_ffn_w1, m_ffn_w2, m_final_g, v_norm_mix_g, v_norm_ffn_g, v_even_w_in, v_even_conv_k, v_even_conv_b, v_even_ln_g, v_even_ln_b, v_even_w_out, v_odd_w_in, v_odd_conv_k, v_odd_ln_g, v_odd_ln_b, v_odd_sg_w, v_odd_sg_b, v_odd_w_out, v_ffn_w1, v_ffn_w2, v_final_g):
    names = ("norm_mix_g", "norm_ffn_g", "even_w_in", "even_conv_k", "even_conv_b", "even_ln_g", "even_ln_b", "even_w_out",
             "odd_w_in", "odd_conv_k", "odd_ln_g", "odd_ln_b", "odd_sg_w", "odd_sg_b", "odd_w_out", "ffn_w1", "ffn_w2", "final_g")
    w = dict(zip(names, (norm_mix_g, norm_ffn_g, even_w_in, even_conv_k, even_conv_b, even_ln_g, even_ln_b, even_w_out,
                         odd_w_in, odd_conv_k, odd_ln_g, odd_ln_b, odd_sg_w, odd_sg_b, odd_w_out, ffn_w1, ffn_w2, final_g)))
    mom = dict(zip(names, (m_norm_mix_g, m_norm_ffn_g, m_even_w_in, m_even_conv_k, m_even_conv_b, m_even_ln_g, m_even_ln_b,
                           m_even_w_out, m_odd_w_in, m_odd_conv_k, m_odd_ln_g, m_odd_ln_b, m_odd_sg_w, m_odd_sg_b, m_odd_w_out,
                           m_ffn_w1, m_ffn_w2, m_final_g)))
    vel = dict(zip(names, (v_norm_mix_g, v_norm_ffn_g, v_even_w_in, v_even_conv_k, v_even_conv_b, v_even_ln_g, v_even_ln_b,
                           v_even_w_out, v_odd_w_in, v_odd_conv_k, v_odd_ln_g, v_odd_ln_b, v_odd_sg_w, v_odd_sg_b, v_odd_w_out,
                           v_ffn_w1, v_ffn_w2, v_final_g)))
    big_names = ("even_w_in", "even_w_out", "odd_w_in", "odd_w_out", "ffn_w1", "ffn_w2")
    chip = 2 * lax.axis_index("x") + lax.axis_index("y")

    def shard2d(t, name):
        return t[name].reshape(SHARD_OPERAND_SHAPES[big_names.index(name)])

    chip_op = jnp.reshape(chip, (1,)).astype(jnp.int32)
    placed = [_cast_place(pc, shard2d(w, big_names[pc.src]), chip_op) for pc in PIECES]
    flying, gather_send, gather_recv, _ = _gather_start(placed)
    ready = {}

    def weight(name, after):
        if name not in ready:
            i = [pc.name for pc in PIECES].index(name)
            landed = _gather_wait(PIECES[i], flying[i], gather_send[i], gather_recv[i], after)
            ready[name] = _core_forward(PIECES[i], landed)
        return ready[name]

    scattering = []

    def emit(name, partial):
        pc = PIECES[[q.name for q in PIECES].index(name)]
        chip_sum, land, send_sems, recv_sems, token = _scatter_start(pc, _chipsum(pc, partial))
        scattering.append((pc, chip_sum, land, send_sems, recv_sems))
        return token

    full = {}
    small_pack = _pack([w[n] for n in SHARDED_NAMES], SHARDED_SHARD_ROWS)
    gathered = _allgather8("gather_small", small_pack, False)[0].reshape(4, 2, sum(SHARDED_SHARD_ROWS), LANES)[:, 0]
    r0 = 0
    for n, sh, rows, full_sh in zip(SHARDED_NAMES, SHARDED_SHARD_SHAPES, SHARDED_SHARD_ROWS, SHARDED_FULL_SHAPES):
        per_chip = gathered[:, r0:r0 + rows].reshape(4, -1)[:, :full_sh[0] * LANES].reshape(4, full_sh[0], LANES)
        full[n] = jnp.transpose(per_chip, (1, 0, 2)).reshape(full_sh)
        r0 += rows
    p = dict(full)
    p.update(norm_mix_g0=norm_mix_g[0:1], norm_mix_g1=norm_mix_g[1:2], norm_ffn_g0=norm_ffn_g[0:1], norm_ffn_g1=norm_ffn_g[1:2],
             even_conv_b=even_conv_b, even_ln_g=even_ln_g, even_ln_b=even_ln_b,
             odd_sg_w=odd_sg_w[0], odd_sg_bt=odd_sg_b[0].T, final_g=final_g[None, :])

    loss, dx, g = _local_step(x[0], loss_target[0], p, weight, emit)
    loss = lax.psum(loss, ("x", "y", "c"))

    landed = {pc.name: _scatter_wait(pc, chip_sum, land, send_sems, recv_sems, dx)
              for pc, chip_sum, land, send_sems, recv_sems in scattering}
    big_grads = {n: _allsum_join(o, [landed[pc.name][0] for pc in PIECES if pc.src == o],
                                 [landed[pc.name][1] for pc in PIECES if pc.src == o])
                 for o, n in enumerate(big_names)}

    grad_parts = [g["norm_mix_g0"], g["norm_mix_g1"], g["norm_ffn_g0"], g["norm_ffn_g1"], g["even_conv_b"], g["even_ln_g"],
                  g["even_ln_b"], g["odd_sg_w"], g["odd_sg_bt"].T, g["final_g"],
                  g["even_conv_k"], g["odd_conv_k"], g["odd_ln_g"], g["odd_ln_b"]]
    grad_pack = _pack(grad_parts, (8, 8, 8, 8) + REPL_ROWS[2:] + SHARDED_FULL_ROWS)
    grad_sum = _allgather8("allreduce_small", grad_pack, True)[1]
    parts = _unpack(grad_sum, REPL_SHAPES + SHARDED_FULL_SHAPES, REPL_ROWS + SHARDED_FULL_ROWS)
    grads = dict(zip(REPL_NAMES, parts[:len(REPL_NAMES)]))
    for n, full_g, sh in zip(SHARDED_NAMES, parts[len(REPL_NAMES):], SHARDED_SHARD_SHAPES):
        grads[n] = lax.dynamic_slice_in_dim(full_g, chip * LANES, LANES, axis=1).reshape(sh)
    for n in big_names:
        grads[n] = big_grads[n].reshape(w[n].shape)

    delta, new_m, new_v = {}, {}, {}
    for n in big_names:
        d2, m2, v2 = _adamw(f"adamw_{n}", shard2d(w, n), big_grads[n], shard2d(mom, n), shard2d(vel, n))
        delta[n], new_m[n], new_v[n] = (t.reshape(w[n].shape) for t in (d2, m2, v2))
    for tag, group, rows, shapes in (("repl", REPL_NAMES, REPL_ROWS, [w[n].shape for n in REPL_NAMES]),
                                     ("sharded", SHARDED_NAMES, SHARDED_SHARD_ROWS, SHARDED_SHARD_SHAPES)):
        packs = [_pack([t[n] for n in group], rows) for t in (w, grads, mom, vel)]
        outs = _adamw(f"adamw_{tag}", *packs)
        for res, o in zip((delta, new_m, new_v), outs):
            res.update(zip(group, _unpack(o, shapes, rows)))

    out = [loss, dx[None]]
    for res in (grads, delta, new_m, new_v):
        out.extend(res[n] for n in names)
    return tuple(out)
```

```python
import functools

import jax
import jax.numpy as jnp
from jax import lax
from jax.experimental import pallas as pl
from jax.experimental.pallas import tpu as pltpu

F32 = jnp.float32
BF16 = jnp.bfloat16

T = 2048
D = 1024
CONV_CH = 512
CONV_W = 31
HEAD_DIM = 64
ATT_W = 1536
EVEN_IN = 5632
ODD_IN = 2560
SCONV_W = 3
SG_GROUPS = 4
CHUNK = 128
D_FF = 4096
EPS = 1e-6
DILATIONS = (1, 4, 16)
BAND = 128
SCALE = HEAD_DIM ** -0.5
NEG = -1e30

ADAM_LR = 0.001
ADAM_B1 = 0.9
ADAM_B2 = 0.999
ADAM_EPS = 1e-08
ADAM_WD = 0.01
ADAM_STEP = 10

V7X_VMEM_BYTES = 64 * 2 ** 20
VMEM_LIMIT = V7X_VMEM_BYTES - 8 * 2 ** 20
LANES = 128
TOKEN_SHAPE = (8, LANES)


def _pcall(body, **kw):
    return pl.pallas_call(body, **kw)


def _params(*sem):
    return pltpu.CompilerParams(dimension_semantics=sem, vmem_limit_bytes=VMEM_LIMIT)


def _dot(a, b, dims):
    return lax.dot_general(a, b, (dims, ((), ())), preferred_element_type=F32)


def _nn(a, b):
    return _dot(a, b, ((1,), (0,)))


def _nt(a, b):
    return _dot(a, b, ((1,), (1,)))


def _tn(a, b):
    return _dot(a, b, ((0,), (0,)))


def _sigmoid(x):
    return 1.0 / (1.0 + jnp.exp(-x))


MM_VMEM_BUDGET = 40 * 2 ** 20


def _mm_tiles(mode, m, n, k, a_bytes, b_bytes, extra_bytes, out_bytes):
    def divisors(total, unit):
        return [t for t in range(unit, total + 1, unit) if total % t == 0]

    best = None
    for tm in divisors(m, LANES if mode == "tn" else 8):
        for tn in divisors(n, LANES):
            blocks = tm * k * a_bytes + tn * k * b_bytes + tm * tn * (extra_bytes + out_bytes)
            casts = (tm * k * 2 if a_bytes == 4 else 0) + (tn * k * 2 if b_bytes == 4 else 0)
            if 2 * blocks + casts + tm * tn * 4 > MM_VMEM_BUDGET:
                continue
            key = ((m // tm) * (n // tn), (m // tm) * n * k * b_bytes, abs(tm - tn))
            if best is None or key < best[0]:
                best = (key, tm, tn)
    return best[1], best[2]


def _mm(name, mode, a, b, m, n, k, out_dtypes, *, b_off=0, extras=(), epi=None, tie=None):
    tm, tn = _mm_tiles(mode, m, n, k, a.dtype.itemsize, b.dtype.itemsize, sum(e.dtype.itemsize for e in extras),
                       sum(jnp.dtype(dt).itemsize for dt in out_dtypes))
    assert b_off % tn == 0
    b_off //= tn
    if mode == "nn":
        a_spec = pl.BlockSpec((tm, k), lambda i, j: (i, 0))
        b_spec = pl.BlockSpec((k, tn), lambda i, j: (0, j + b_off))
        dims = ((1,), (0,))
    elif mode == "nt":
        a_spec = pl.BlockSpec((tm, k), lambda i, j: (i, 0))
        b_spec = pl.BlockSpec((tn, k), lambda i, j: (j, 0))
        dims = ((1,), (1,))
    else:
        a_spec = pl.BlockSpec((k, tm), lambda i, j: (0, i))
        b_spec = pl.BlockSpec((k, tn), lambda i, j: (0, j))
        dims = ((0,), (0,))
    o_spec = pl.BlockSpec((tm, tn), lambda i, j: (i, j))
    n_extra = len(extras)
    ties = () if tie is None else (tie,)

    def body(a_ref, b_ref, *rest):
        rest = rest[len(ties):]
        acc = _dot(a_ref[...].astype(BF16), b_ref[...].astype(BF16), dims)
        vals = epi(acc, *[e[...] for e in rest[:n_extra]]) if epi is not None else (acc,)
        for o_ref, v in zip(rest[n_extra:], vals):
            o_ref[...] = v.astype(o_ref.dtype)

    outs = _pcall(
        body, name=name, grid=(m // tm, n // tn),
        in_specs=[a_spec, b_spec] + [pl.BlockSpec(TOKEN_SHAPE, lambda i, j: (0, 0))] * len(ties) + [o_spec] * n_extra,
        out_specs=[o_spec] * len(out_dtypes),
        out_shape=[jax.ShapeDtypeStruct((m, n), dt) for dt in out_dtypes],
        compiler_params=_params("parallel", "parallel"),
    )(a, b, *ties, *extras)
    return outs[0] if len(out_dtypes) == 1 else outs


def _rms_fwd(name, h, g, tm=512):
    def body(h_ref, g_ref, o_ref):
        x = h_ref[...]
        r = lax.rsqrt(jnp.mean(x * x, axis=-1, keepdims=True) + EPS)
        o_ref[...] = ((x * r) * g_ref[...]).astype(BF16)

    return _pcall(
        body, name=name, grid=(T // tm,),
        in_specs=[pl.BlockSpec((tm, D), lambda i: (i, 0)), pl.BlockSpec((1, D), lambda i: (0, 0))],
        out_specs=pl.BlockSpec((tm, D), lambda i: (i, 0)),
        out_shape=jax.ShapeDtypeStruct((T, D), BF16),
        compiler_params=_params("parallel"),
    )(h, g)


def _rms_bwd(name, h, dhn, g, dres, tm=512):
    def body(h_ref, d_ref, g_ref, r_ref, dh_ref, dg_ref):
        x = h_ref[...]
        r = lax.rsqrt(jnp.mean(x * x, axis=-1, keepdims=True) + EPS)
        nrm = x * r
        dy = d_ref[...]
        dn = dy * g_ref[...]
        dh_ref[...] = r_ref[...] + r * (dn - nrm * jnp.mean(dn * nrm, axis=-1, keepdims=True))

        @pl.when(pl.program_id(0) == 0)
        def _():
            dg_ref[...] = jnp.zeros_like(dg_ref)

        dg_ref[...] += jnp.sum(dy * nrm, axis=0, keepdims=True)

    row = pl.BlockSpec((tm, D), lambda i: (i, 0))
    vec = pl.BlockSpec((1, D), lambda i: (0, 0))
    return _pcall(
        body, name=name, grid=(T // tm,),
        in_specs=[row, row, vec, row], out_specs=[row, vec],
        out_shape=[jax.ShapeDtypeStruct((T, D), F32), jax.ShapeDtypeStruct((1, D), F32)],
        compiler_params=_params("arbitrary"),
    )(h, dhn, g, dres)


def _loss_head(h, g, target, tm=512):
    def body(h_ref, g_ref, t_ref, dh_ref, dg_ref, loss_ref):
        x = h_ref[...]
        r = lax.rsqrt(jnp.mean(x * x, axis=-1, keepdims=True) + EPS)
        nrm = x * r
        gain = g_ref[...]
        err = nrm * gain - t_ref[...]
        dy = err * (1.0 / D)
        dn = dy * gain
        dh_ref[...] = r * (dn - nrm * jnp.mean(dn * nrm, axis=-1, keepdims=True))

        @pl.when(pl.program_id(0) == 0)
        def _():
            dg_ref[...] = jnp.zeros_like(dg_ref)
            loss_ref[...] = jnp.zeros_like(loss_ref)

        dg_ref[...] += jnp.sum(dy * nrm, axis=0, keepdims=True)
        part = jnp.sum(jnp.sum(err * err, axis=1, keepdims=True), axis=0, keepdims=True) * (0.5 / D)
        loss_ref[...] += jnp.broadcast_to(part, (1, LANES))

    row = pl.BlockSpec((tm, D), lambda i: (i, 0))
    vec = pl.BlockSpec((1, D), lambda i: (0, 0))
    return _pcall(
        body, name="loss_head", grid=(T // tm,),
        in_specs=[row, vec, row], out_specs=[row, vec, pl.BlockSpec((1, LANES), lambda i: (0, 0))],
        out_shape=[jax.ShapeDtypeStruct((T, D), F32), jax.ShapeDtypeStruct((1, D), F32),
                   jax.ShapeDtypeStruct((1, LANES), F32)],
        compiler_params=_params("arbitrary"),
    )(h, g, target)


CONV_TILE = 256
CONV_HALO = 32


def _glu(z):
    return z[:, :CONV_CH] * _sigmoid(z[:, CONV_CH:])


def _econv_fwd(zc, conv_k, conv_b, ln_g, ln_b):
    R, H = CONV_TILE, CONV_HALO

    def body(z_ref, zh_ref, k_ref, b_ref, g_ref, be_ref, cv_ref, cat_ref):
        i = pl.program_id(0)
        glu = _glu(z_ref[...])
        halo = _glu(zh_ref[...]) * (i > 0).astype(F32)
        win = jnp.concatenate([halo, glu], axis=0)
        acc = jnp.zeros((R, CONV_CH), F32) + b_ref[...]
        for j in range(CONV_W):
            off = H - (CONV_W - 1) + j
            acc = acc + k_ref[j:j + 1, :] * win[off:off + R, :]
        cv_ref[...] = acc
        mu = jnp.mean(acc, axis=-1, keepdims=True)
        xc = acc - mu
        rstd = lax.rsqrt(jnp.mean(xc * xc, axis=-1, keepdims=True) + EPS)
        ln = xc * rstd * g_ref[...] + be_ref[...]
        cat_ref[...] = (ln * _sigmoid(ln)).astype(BF16)

    vec = pl.BlockSpec((1, CONV_CH), lambda i: (0, 0))
    return _pcall(
        body, name="econv_fwd", grid=(T // R,),
        in_specs=[pl.BlockSpec((R, 2 * CONV_CH), lambda i: (i, 0)),
                  pl.BlockSpec((H, 2 * CONV_CH), lambda i: (jnp.maximum(i * (R // H) - 1, 0), 0)),
                  pl.BlockSpec((CONV_W, CONV_CH), lambda i: (0, 0)), vec, vec, vec],
        out_specs=[pl.BlockSpec((R, CONV_CH), lambda i: (i, 0)), pl.BlockSpec((R, CONV_CH), lambda i: (i, 0))],
        out_shape=[jax.ShapeDtypeStruct((T, CONV_CH), F32), jax.ShapeDtypeStruct((T, D), BF16)],
        compiler_params=_params("parallel"),
    )(zc, zc, conv_k, conv_b, ln_g, ln_b)


def _econv_bwd_ln(cv, dcat, ln_g, ln_b):
    R = CONV_TILE

    def body(cv_ref, d_ref, g_ref, be_ref, dcv_ref, dg_ref, dbe_ref, dcb_ref):
        cv_t = cv_ref[...]
        mu = jnp.mean(cv_t, axis=-1, keepdims=True)
        xc = cv_t - mu
        rstd = lax.rsqrt(jnp.mean(xc * xc, axis=-1, keepdims=True) + EPS)
        xh = xc * rstd
        ln = xh * g_ref[...] + be_ref[...]
        sg = _sigmoid(ln)
        dln = d_ref[...] * (sg * (1.0 + ln * (1.0 - sg)))
        dxh = dln * g_ref[...]
        dcv = rstd * (dxh - jnp.mean(dxh, axis=-1, keepdims=True) - xh * jnp.mean(dxh * xh, axis=-1, keepdims=True))
        dcv_ref[...] = dcv

        @pl.when(pl.program_id(0) == 0)
        def _():
            dg_ref[...] = jnp.zeros_like(dg_ref)
            dbe_ref[...] = jnp.zeros_like(dbe_ref)
            dcb_ref[...] = jnp.zeros_like(dcb_ref)

        dg_ref[...] += jnp.sum(dln * xh, axis=0, keepdims=True)
        dbe_ref[...] += jnp.sum(dln, axis=0, keepdims=True)
        dcb_ref[...] += jnp.sum(dcv, axis=0, keepdims=True)

    vec = pl.BlockSpec((1, CONV_CH), lambda i: (0, 0))
    row = pl.BlockSpec((R, CONV_CH), lambda i: (i, 0))
    vshape = jax.ShapeDtypeStruct((1, CONV_CH), F32)
    return _pcall(
        body, name="econv_bwd_ln", grid=(T // R,),
        in_specs=[row, row, vec, vec], out_specs=[row, vec, vec, vec],
        out_shape=[jax.ShapeDtypeStruct((T, CONV_CH), F32), vshape, vshape, vshape],
        compiler_params=_params("arbitrary"),
    )(cv, dcat, ln_g, ln_b)


def _econv_bwd_conv(dcv, zc, conv_k):
    R, H = CONV_TILE, CONV_HALO
    last = T // R - 1

    def body(d_ref, dn_ref, z_ref, zh_ref, k_ref, dz_ref, dk_ref):
        i = pl.program_id(0)
        z = z_ref[...]
        a_lin = z[:, :CONV_CH]
        sg = _sigmoid(z[:, CONV_CH:])
        glu = a_lin * sg
        halo = _glu(zh_ref[...]) * (i > 0).astype(F32)
        win = jnp.concatenate([halo, glu], axis=0)
        dcv_t = d_ref[...]
        nxt = dn_ref[...] * (i < last).astype(F32)
        winb = jnp.concatenate([dcv_t, nxt], axis=0)

        @pl.when(i == 0)
        def _():
            dk_ref[...] = jnp.zeros_like(dk_ref)

        dglu = jnp.zeros((R, CONV_CH), F32)
        for j in range(CONV_W):
            off = H - (CONV_W - 1) + j
            dk_ref[j:j + 1, :] += jnp.sum(dcv_t * win[off:off + R, :], axis=0, keepdims=True)
            ob = CONV_W - 1 - j
            dglu = dglu + k_ref[j:j + 1, :] * winb[ob:ob + R, :]
        dz_ref[...] = jnp.concatenate([dglu * sg, dglu * a_lin * sg * (1.0 - sg)], axis=1).astype(BF16)

    return _pcall(
        body, name="econv_bwd_conv", grid=(T // R,),
        in_specs=[pl.BlockSpec((R, CONV_CH), lambda i: (i, 0)),
                  pl.BlockSpec((H, CONV_CH), lambda i: (jnp.minimum((i + 1) * (R // H), T // H - 1), 0)),
                  pl.BlockSpec((R, 2 * CONV_CH), lambda i: (i, 0)),
                  pl.BlockSpec((H, 2 * CONV_CH), lambda i: (jnp.maximum(i * (R // H) - 1, 0), 0)),
                  pl.BlockSpec((CONV_W, CONV_CH), lambda i: (0, 0))],
        out_specs=[pl.BlockSpec((R, 2 * CONV_CH), lambda i: (i, 0)), pl.BlockSpec((CONV_W, CONV_CH), lambda i: (0, 0))],
        out_shape=[jax.ShapeDtypeStruct((T, EVEN_IN), BF16), jax.ShapeDtypeStruct((CONV_W, CONV_CH), F32)],
        compiler_params=_params("arbitrary"),
    )(dcv, dcv, zc, zc, conv_k)


def _swap_halves(v):
    lane = lax.broadcasted_iota(jnp.int32, v.shape, 1)
    return jnp.where((lane % HEAD_DIM) < HEAD_DIM // 2, pltpu.roll(v, LANES - HEAD_DIM // 2, 1),
                     pltpu.roll(v, HEAD_DIM // 2, 1))


def _qkv_proj(hn, w_in, rope_c, rope_s, tm=1024):
    tn = 4 * LANES

    def body(a_ref, b_ref, c_ref, s_ref, o_ref):
        j = pl.program_id(1)
        acc = _nn(a_ref[...], b_ref[...])
        for p in range(4):
            v = acc[:, p * LANES:(p + 1) * LANES]
            rot = v * c_ref[...] + _swap_halves(v) * s_ref[...]
            o_ref[p] = jnp.where(j < 6, rot, v)

    tab = pl.BlockSpec((tm, LANES), lambda i, j: (i, 0))
    return _pcall(
        body, name="qkv_proj", grid=(T // tm, 9),
        in_specs=[pl.BlockSpec((tm, D), lambda i, j: (i, 0)),
                  pl.BlockSpec((D, tn), lambda i, j: (0, j + (2 * CONV_CH) // tn)), tab, tab],
        out_specs=pl.BlockSpec((None, 4, tm, LANES), lambda i, j: (j, 0, i, 0)),
        out_shape=jax.ShapeDtypeStruct((9, 4, T, LANES), F32),
        compiler_params=_params("parallel", "parallel"),
    )(hn, w_in, rope_c, rope_s)


ATTN_FWD_UNROLL = 4
ATTN_BWD_UNROLL = 2


def _band_rows(start, d):
    if d == 1:
        return pl.ds(pl.multiple_of(start, BAND), BAND)
    return pl.ds(start, BAND, stride=d)


def _band_masks(n):
    row = lax.broadcasted_iota(jnp.int32, (BAND, BAND), 0)
    col = lax.broadcasted_iota(jnp.int32, (BAND, BAND), 1)
    no_prev = (n == 0).astype(jnp.int32) * (2 * BAND)
    return col <= row, col >= row + no_prev


def _attn_fwd(qkv, g):
    d = DILATIONS[g]
    nb = T // d // BAND

    def body(q_ref, k_ref, v_ref, o_ref, l_ref):
        lane_lo = lax.broadcasted_iota(jnp.int32, (BAND, LANES), 1) < HEAD_DIM

        def step(idx, carry):
            r = idx // nb
            n = idx % nb
            cur = _band_rows(n * (BAND * d) + r, d)
            prev = _band_rows(jnp.maximum(n - 1, 0) * (BAND * d) + r, d)
            q = q_ref[cur, :]
            kc = k_ref[cur, :].astype(BF16)
            vc = v_ref[cur, :].astype(BF16)
            kp = k_ref[prev, :].astype(BF16)
            vp = v_ref[prev, :].astype(BF16)
            mc, mp = _band_masks(n)
            outs, lses = [], []
            for h in range(2):
                hm = lane_lo if h == 0 else jnp.logical_not(lane_lo)
                qm = jnp.where(hm, q, 0.0).astype(BF16)
                sc = jnp.where(mc, _nt(qm, kc) * SCALE, NEG)
                sp = jnp.where(mp, _nt(qm, kp) * SCALE, NEG)
                mx = jnp.maximum(jnp.max(sc, axis=1, keepdims=True), jnp.max(sp, axis=1, keepdims=True))
                pc = jnp.exp(sc - mx)
                pp = jnp.exp(sp - mx)
                den = jnp.sum(pc, axis=1, keepdims=True) + jnp.sum(pp, axis=1, keepdims=True)
                outs.append((_nn(pc.astype(BF16), vc) + _nn(pp.astype(BF16), vp)) / den)
                lses.append(jnp.broadcast_to(mx + jnp.log(den), (BAND, LANES)))
            o_ref[cur, :] = jnp.where(lane_lo, outs[0], outs[1])
            l_ref[cur, :] = jnp.where(lane_lo, lses[0], lses[1])
            return carry

        lax.fori_loop(0, d * nb, step, 0, unroll=ATTN_FWD_UNROLL)

    def slab(which):
        return pl.BlockSpec((None, None, T, LANES), lambda p: (which * 3 + g, p, 0, 0))

    out = pl.BlockSpec((None, T, LANES), lambda p: (p, 0, 0))
    shape = jax.ShapeDtypeStruct((4, T, LANES), F32)
    return _pcall(
        body, name=f"attn_fwd{g}", grid=(4,),
        in_specs=[slab(0), slab(1), slab(2)], out_specs=[out, out], out_shape=[shape, shape],
        compiler_params=_params("parallel"),
    )(qkv, qkv, qkv)


def _attn_merge(outs, lses, cat, tm=1024):
    def body(o0, o1, o2, l0, l1, l2, cat_in, cat_ref, att_ref, w0, w1, w2):
        del cat_in
        la, lb, lc = l0[...], l1[...], l2[...]
        mx = jnp.maximum(jnp.maximum(la, lb), lc)
        ea, eb, ec = jnp.exp(la - mx), jnp.exp(lb - mx), jnp.exp(lc - mx)
        inv = 1.0 / (ea + eb + ec)
        wa, wb, wc = ea * inv, eb * inv, ec * inv
        att = wa * o0[...] + wb * o1[...] + wc * o2[...]
        att_ref[...] = att
        cat_ref[...] = att.astype(BF16)
        w0[...] = wa
        w1[...] = wb
        w2[...] = wc

    slab = pl.BlockSpec((None, tm, LANES), lambda p, i: (p, i, 0))
    shape = jax.ShapeDtypeStruct((4, T, LANES), F32)
    return _pcall(
        body, name="attn_merge", grid=(4, T // tm),
        in_specs=[slab] * 6 + [pl.BlockSpec(memory_space=pl.ANY)],
        out_specs=[pl.BlockSpec((tm, LANES), lambda p, i: (i, CONV_CH // LANES + p)), slab, slab, slab, slab],
        out_shape=[jax.ShapeDtypeStruct((T, D), BF16), shape, shape, shape, shape],
        input_output_aliases={6: 0},
        compiler_params=_params("parallel", "parallel"),
    )(*outs, *lses, cat)


def _attn_bwd(qkv, lse, wgt, att, dcat, dqkv, g):
    d = DILATIONS[g]
    nb = T // d // BAND

    def body(q_ref, k_ref, v_ref, l_ref, w_ref, a_ref, da_ref, dq_in, o_ref):
        del dq_in
        lane = lax.broadcasted_iota(jnp.int32, (BAND, LANES), 1)
        lane_lo = lane < HEAD_DIM
        row = lax.broadcasted_iota(jnp.int32, (LANES, LANES), 0)
        same_head = ((row // HEAD_DIM) == (lane // HEAD_DIM)).astype(BF16)
        dq_ref, dk_ref, dv_ref = o_ref.at[0], o_ref.at[1], o_ref.at[2]
        dk_ref[...] = jnp.zeros((T, LANES), F32)
        dv_ref[...] = jnp.zeros((T, LANES), F32)

        def step(idx, carry):
            r = idx // nb
            n = idx % nb
            cur = _band_rows(n * (BAND * d) + r, d)
            prev = _band_rows(jnp.maximum(n - 1, 0) * (BAND * d) + r, d)
            q = q_ref[cur, :]
            kc = k_ref[cur, :].astype(BF16)
            vc = v_ref[cur, :].astype(BF16)
            kp = k_ref[prev, :].astype(BF16)
            vp = v_ref[prev, :].astype(BF16)
            lse_t = l_ref[cur, :]
            w_t = w_ref[cur, :]
            da = da_ref[cur, :]
            prod = da * a_ref[cur, :]
            hi = prod.astype(BF16)
            lo = (prod - hi.astype(F32)).astype(BF16)
            csum = _nn(hi, same_head) + _nn(lo, same_head)
            mc, mp = _band_masks(n)
            dqs = []
            dkc = jnp.zeros((BAND, LANES), F32)
            dkp = jnp.zeros((BAND, LANES), F32)
            dvc = jnp.zeros((BAND, LANES), F32)
            dvp = jnp.zeros((BAND, LANES), F32)
            for h in range(2):
                hm = lane_lo if h == 0 else jnp.logical_not(lane_lo)
                col0 = h * HEAD_DIM
                lse_h = lse_t[:, col0:col0 + 1]
                w_h = w_t[:, col0:col0 + 1]
                c_h = csum[:, col0:col0 + 1]
                qm = jnp.where(hm, q, 0.0).astype(BF16)
                dam = jnp.where(hm, da, 0.0).astype(BF16)
                pwc = w_h * jnp.exp(jnp.where(mc, _nt(qm, kc) * SCALE, NEG) - lse_h)
                pwp = w_h * jnp.exp(jnp.where(mp, _nt(qm, kp) * SCALE, NEG) - lse_h)
                dsc = (pwc * (_nt(dam, vc) - c_h) * SCALE).astype(BF16)
                dsp = (pwp * (_nt(dam, vp) - c_h) * SCALE).astype(BF16)
                dqs.append(_nn(dsc, kc) + _nn(dsp, kp))
                dkc = dkc + _tn(dsc, qm)
                dkp = dkp + _tn(dsp, qm)
                dvc = dvc + _tn(pwc.astype(BF16), dam)
                dvp = dvp + _tn(pwp.astype(BF16), dam)
            dq_ref[cur, :] = jnp.where(lane_lo, dqs[0], dqs[1])
            dk_ref[cur, :] += dkc
            dk_ref[prev, :] += dkp
            dv_ref[cur, :] += dvc
            dv_ref[prev, :] += dvp
            return carry

        lax.fori_loop(0, d * nb, step, 0, unroll=ATTN_BWD_UNROLL)

    def slab(which):
        return pl.BlockSpec((None, None, T, LANES), lambda p: (which * 3 + g, p, 0, 0))

    per_pair = pl.BlockSpec((None, T, LANES), lambda p: (p, 0, 0))
    return _pcall(
        body, name=f"attn_bwd{g}", grid=(4,),
        in_specs=[slab(0), slab(1), slab(2), per_pair, per_pair, per_pair,
                  pl.BlockSpec((T, LANES), lambda p: (0, CONV_CH // LANES + p)),
                  pl.BlockSpec(memory_space=pl.ANY)],
        out_specs=pl.BlockSpec((None, 3, None, T, LANES), lambda p: (g, 0, p, 0, 0)),
        out_shape=jax.ShapeDtypeStruct((3, 3, 4, T, LANES), F32),
        input_output_aliases={7: 0},
        compiler_params=_params("parallel"),
    )(qkv, qkv, qkv, lse, wgt, att, dcat, dqkv)


def _rope_bwd(dqkv, rope_c, rope_s, dz):
    wide = 4 * LANES

    def body(d_ref, c_ref, s_ref, dz_in, o_ref):
        del dz_in
        w = pl.program_id(1)
        for p in range(4):
            v = d_ref[p]
            rot = v * c_ref[...] + _swap_halves(v * s_ref[...])
            o_ref[:, p * LANES:(p + 1) * LANES] = jnp.where(w < 2, rot, v).astype(BF16)

    tab = pl.BlockSpec((T, LANES), lambda g, w: (0, 0))
    return _pcall(
        body, name="rope_bwd", grid=(3, 3),
        in_specs=[pl.BlockSpec((None, None, 4, T, LANES), lambda g, w: (g, w, 0, 0, 0)), tab, tab,
                  pl.BlockSpec(memory_space=pl.ANY)],
        out_specs=pl.BlockSpec((T, wide), lambda g, w: (0, (2 * CONV_CH) // wide + w * 3 + g)),
        out_shape=jax.ShapeDtypeStruct((T, EVEN_IN), BF16),
        input_output_aliases={3: 0},
        compiler_params=_params("parallel", "parallel"),
    )(dqkv, rope_c, rope_s, dz)


ODD_TILE = 256
ODD_HALO = 8
GELU_C = 0.7978845608028654
GELU_A = 0.044715


def _gelu(x):
    return 0.5 * x * (1.0 + jnp.tanh(GELU_C * (x + GELU_A * x * x * x)))


def _gelu_grad(x):
    th = jnp.tanh(GELU_C * (x + GELU_A * x * x * x))
    return 0.5 * (1.0 + th) + 0.5 * x * (1.0 - th * th) * GELU_C * (1.0 + 3.0 * GELU_A * x * x)


def _tril():
    row = lax.broadcasted_iota(jnp.int32, (CHUNK, CHUNK), 0)
    col = lax.broadcasted_iota(jnp.int32, (CHUNK, CHUNK), 1)
    return (col <= row).astype(F32)


def _odd_parts(z, zh, i, k_ref, g_ref, be_ref, w_ref, bt_ref):
    R, H = ODD_TILE, ODD_HALO
    gb, gc, xs, uv = z[:, :512], z[:, 512:1024], z[:, 1024:1536], z[:, 1536:]
    halo = zh[:, 512:1024] * zh[:, 1024:1536] * (i > 0).astype(F32)
    win = jnp.concatenate([halo, gc * xs], axis=0)
    cv = jnp.zeros((R, 512), F32)
    for j in range(SCONV_W):
        off = H - (SCONV_W - 1) + j
        cv = cv + k_ref[j:j + 1, :] * win[off:off + R, :]
    ge = _gelu(uv)
    u, v = ge[:, :512], ge[:, 512:]
    mu = jnp.mean(v, axis=-1, keepdims=True)
    xc = v - mu
    rstd = lax.rsqrt(jnp.mean(xc * xc, axis=-1, keepdims=True) + EPS)
    xh = xc * rstd
    vn = xh * g_ref[...] + be_ref[...]
    tril = _tril()
    wms = [(w_ref[g] * tril).astype(BF16) for g in range(SG_GROUPS)]
    rows = []
    for ci in range(R // CHUNK):
        blocks = []
        for g in range(SG_GROUPS):
            blk = vn[ci * CHUNK:(ci + 1) * CHUNK, g * LANES:(g + 1) * LANES].astype(BF16)
            blocks.append(_nn(wms[g], blk) + bt_ref[:, g:g + 1])
        rows.append(jnp.concatenate(blocks, axis=1))
    vmix = jnp.concatenate(rows, axis=0)
    return gb, gc, xs, uv, win, cv, u, rstd, xh, vn, vmix, wms


def _odd_mid_fwd(z, conv_k, ln_g, ln_b, sg_w, sg_bt):
    R, H = ODD_TILE, ODD_HALO

    def body(z_ref, zh_ref, k_ref, g_ref, be_ref, w_ref, bt_ref, o_ref):
        i = pl.program_id(0)
        gb, _, _, _, _, cv, u, _, _, _, vmix, _ = _odd_parts(z_ref[...], zh_ref[...], i, k_ref, g_ref, be_ref, w_ref, bt_ref)
        o_ref[...] = jnp.concatenate([gb * cv, u * vmix], axis=1).astype(BF16)

    vec = pl.BlockSpec((1, 512), lambda i: (0, 0))
    return _pcall(
        body, name="odd_mid_fwd", grid=(T // R,),
        in_specs=[pl.BlockSpec((R, ODD_IN), lambda i: (i, 0)),
                  pl.BlockSpec((H, ODD_IN), lambda i: (jnp.maximum(i * (R // H) - 1, 0), 0)),
                  pl.BlockSpec((SCONV_W, 512), lambda i: (0, 0)), vec, vec,
                  pl.BlockSpec((SG_GROUPS, CHUNK, CHUNK), lambda i: (0, 0, 0)),
                  pl.BlockSpec((CHUNK, SG_GROUPS), lambda i: (0, 0))],
        out_specs=pl.BlockSpec((R, D), lambda i: (i, 0)),
        out_shape=jax.ShapeDtypeStruct((T, D), BF16),
        compiler_params=_params("parallel"),
    )(z, z, conv_k, ln_g, ln_b, sg_w, sg_bt)


def _odd_mid_bwd(z, dcat, conv_k, ln_g, ln_b, sg_w, sg_bt):
    R, H = ODD_TILE, ODD_HALO
    last = T // R - 1

    def body(z_ref, zh_ref, zn_ref, d_ref, dn_ref, k_ref, g_ref, be_ref, w_ref, bt_ref,
             dz_ref, dk_ref, dg_ref, dbe_ref, dw_ref, dbt_ref):
        i = pl.program_id(0)
        z = z_ref[...]
        gb, gc, xs, uv, win, cv, u, rstd, xh, vn, vmix, wms = _odd_parts(z, zh_ref[...], i, k_ref, g_ref, be_ref, w_ref, bt_ref)
        dcat_t = d_ref[...]
        dc, dd = dcat_t[:, :512], dcat_t[:, 512:]

        @pl.when(i == 0)
        def _():
            dk_ref[...] = jnp.zeros_like(dk_ref)
            dg_ref[...] = jnp.zeros_like(dg_ref)
            dbe_ref[...] = jnp.zeros_like(dbe_ref)
            dw_ref[...] = jnp.zeros_like(dw_ref)
            dbt_ref[...] = jnp.zeros_like(dbt_ref)

        dgb = dc * cv
        dcv = dc * gb
        nxt = dn_ref[:, :512] * zn_ref[:, :512] * (i < last).astype(F32)
        winb = jnp.concatenate([dcv, nxt], axis=0)
        dp = jnp.zeros((R, 512), F32)
        for j in range(SCONV_W):
            off = H - (SCONV_W - 1) + j
            dk_ref[j:j + 1, :] += jnp.sum(dcv * win[off:off + R, :], axis=0, keepdims=True)
            ob = SCONV_W - 1 - j
            dp = dp + k_ref[j:j + 1, :] * winb[ob:ob + R, :]
        dgc = dp * xs
        dxs = dp * gc
        du = dd * vmix
        dvmix = dd * u
        tril = _tril()
        rows = []
        for ci in range(R // CHUNK):
            blocks = []
            for g in range(SG_GROUPS):
                sl = (slice(ci * CHUNK, (ci + 1) * CHUNK), slice(g * LANES, (g + 1) * LANES))
                dblk = dvmix[sl]
                dblk16 = dblk.astype(BF16)
                blocks.append(_tn(wms[g], dblk16))
                dw_ref[g] += _nt(dblk16, vn[sl].astype(BF16)) * tril
                dbt_ref[:, g:g + 1] += jnp.sum(dblk, axis=1, keepdims=True)
            rows.append(jnp.concatenate(blocks, axis=1))
        dvn = jnp.concatenate(rows, axis=0)
        dg_ref[...] += jnp.sum(dvn * xh, axis=0, keepdims=True)
        dbe_ref[...] += jnp.sum(dvn, axis=0, keepdims=True)
        dxh = dvn * g_ref[...]
        dv = rstd * (dxh - jnp.mean(dxh, axis=-1, keepdims=True) - xh * jnp.mean(dxh * xh, axis=-1, keepdims=True))
        duv = jnp.concatenate([du, dv], axis=1) * _gelu_grad(uv)
        dz_ref[...] = jnp.concatenate([dgb, dgc, dxs, duv], axis=1).astype(BF16)

    vec = pl.BlockSpec((1, 512), lambda i: (0, 0))
    kspec = pl.BlockSpec((SCONV_W, 512), lambda i: (0, 0))
    wspec = pl.BlockSpec((SG_GROUPS, CHUNK, CHUNK), lambda i: (0, 0, 0))
    bspec = pl.BlockSpec((CHUNK, SG_GROUPS), lambda i: (0, 0))
    nxt_blk = lambda i: (jnp.minimum((i + 1) * (R // H), T // H - 1), 0)
    return _pcall(
        body, name="odd_mid_bwd", grid=(T // R,),
        in_specs=[pl.BlockSpec((R, ODD_IN), lambda i: (i, 0)),
                  pl.BlockSpec((H, ODD_IN), lambda i: (jnp.maximum(i * (R // H) - 1, 0), 0)),
                  pl.BlockSpec((H, ODD_IN), nxt_blk),
                  pl.BlockSpec((R, D), lambda i: (i, 0)),
                  pl.BlockSpec((H, D), nxt_blk),
                  kspec, vec, vec, wspec, bspec],
        out_specs=[pl.BlockSpec((R, ODD_IN), lambda i: (i, 0)), kspec, vec, vec, wspec, bspec],
        out_shape=[jax.ShapeDtypeStruct((T, ODD_IN), BF16), jax.ShapeDtypeStruct((SCONV_W, 512), F32),
                   jax.ShapeDtypeStruct((1, 512), F32), jax.ShapeDtypeStruct((1, 512), F32),
                   jax.ShapeDtypeStruct((SG_GROUPS, CHUNK, CHUNK), F32), jax.ShapeDtypeStruct((CHUNK, SG_GROUPS), F32)],
        compiler_params=_params("arbitrary"),
    )(z, z, z, dcat, dcat, conv_k, ln_g, ln_b, sg_w, sg_bt)


def _ffn_fwd(tag, h, g, weight):
    hn = _rms_fwd(f"ffn{tag}_norm", h, g)

    def act(acc):
        r = jnp.maximum(acc, 0.0)
        return (r * r,)

    f = _mm(f"ffn{tag}_up", "nn", hn, weight(f"ffn_w1_{tag}", hn), T, D_FF, D, (BF16,), epi=act)
    out = _mm(f"ffn{tag}_down", "nn", f, weight(f"ffn_w2_{tag}", f), T, D, D_FF, (F32,),
              epi=lambda acc, res: (acc + res,), extras=(h,))
    return out, (hn, f)


def _ffn_bwd(tag, h, g, weight, emit, saved, dout):
    hn, f = saved
    du = _mm(f"ffn{tag}_dact", "nt", dout, weight(f"ffn_w2_{tag}", dout), T, D_FF, D, (BF16,),
             epi=lambda acc, ff: (acc * (2.0 * jnp.sqrt(ff.astype(F32))),), extras=(f,))
    tok = emit(f"ffn_w2_{tag}", _mm(f"ffn{tag}_dw2", "tn", f, dout, D_FF, D, T, (BF16,)))
    tok = emit(f"ffn_w1_{tag}", _mm(f"ffn{tag}_dw1", "tn", hn, du, D, D_FF, T, (BF16,), tie=tok))
    dhn = _mm(f"ffn{tag}_dhn", "nt", du, weight(f"ffn_w1_{tag}", du), T, D, D_FF, (F32,), tie=tok)
    return _rms_bwd(f"ffn{tag}_dnorm", h, dhn, g, dout)


def _rope_tables():
    half = HEAD_DIM // 2
    inv = 10000.0 ** (-jnp.arange(half, dtype=F32) / half)
    ang = jnp.arange(T, dtype=F32)[:, None] * inv[None, :]
    cos, sin = jnp.cos(ang), jnp.sin(ang)
    c = jnp.tile(jnp.concatenate([cos, cos], axis=1), (1, LANES // HEAD_DIM))
    s = jnp.tile(jnp.concatenate([-sin, sin], axis=1), (1, LANES // HEAD_DIM))
    return c, s


def _local_step(x, target, p, weight, emit):
    rope_c, rope_s = _rope_tables()
    grads = {}
    residual = lambda acc, res: (acc + res,)

    hn0 = _rms_fwd("mix0_norm", x, p["norm_mix_g0"])
    zc = _mm("even_in_conv", "nn", hn0, weight("even_w_in", hn0), T, 2 * CONV_CH, D, (F32,))
    qkv = _qkv_proj(hn0, weight("even_w_in", hn0), rope_c, rope_s)
    cv, cat0 = _econv_fwd(zc, p["even_conv_k"], p["even_conv_b"], p["even_ln_g"], p["even_ln_b"])
    att_parts = [_attn_fwd(qkv, g) for g in range(3)]
    outs = [a[0] for a in att_parts]
    lses = [a[1] for a in att_parts]
    cat0, att, w0, w1, w2 = _attn_merge(outs, lses, cat0)
    wgts = (w0, w1, w2)
    h1 = _mm("even_out", "nn", cat0, weight("even_w_out", cat0), T, D, D, (F32,), epi=residual, extras=(x,))
    h2, ffn0_saved = _ffn_fwd(0, h1, p["norm_ffn_g0"], weight)

    hn1 = _rms_fwd("mix1_norm", h2, p["norm_mix_g1"])
    z1 = _mm("odd_in", "nn", hn1, weight("odd_w_in", hn1), T, ODD_IN, D, (F32,))
    cat1 = _odd_mid_fwd(z1, p["odd_conv_k"], p["odd_ln_g"], p["odd_ln_b"], p["odd_sg_w"], p["odd_sg_bt"])
    h3 = _mm("odd_out", "nn", cat1, weight("odd_w_out", cat1), T, D, D, (F32,), epi=residual, extras=(h2,))
    h4, ffn1_saved = _ffn_fwd(1, h3, p["norm_ffn_g1"], weight)

    dh4, grads["final_g"], loss = _loss_head(h4, p["final_g"], target)

    dh3, grads["norm_ffn_g1"] = _ffn_bwd(1, h3, p["norm_ffn_g1"], weight, emit, ffn1_saved, dh4)
    tok = emit("odd_w_out", _mm("odd_out_dw", "tn", cat1, dh3, D, D, T, (BF16,)))
    dcat1 = _mm("odd_out_dx", "nt", dh3, weight("odd_w_out", dh3), T, D, D, (F32,), tie=tok)
    dz1, grads["odd_conv_k"], grads["odd_ln_g"], grads["odd_ln_b"], grads["odd_sg_w"], grads["odd_sg_bt"] = _odd_mid_bwd(
        z1, dcat1, p["odd_conv_k"], p["odd_ln_g"], p["odd_ln_b"], p["odd_sg_w"], p["odd_sg_bt"])
    tok = emit("odd_w_in", _mm("odd_in_dw", "tn", hn1, dz1, D, ODD_IN, T, (BF16,)))
    dhn1 = _mm("odd_in_dx", "nt", dz1, weight("odd_w_in", dz1), T, D, ODD_IN, (F32,), tie=tok)
    dh2, grads["norm_mix_g1"] = _rms_bwd("mix1_dnorm", h2, dhn1, p["norm_mix_g1"], dh3)

    dh1, grads["norm_ffn_g0"] = _ffn_bwd(0, h1, p["norm_ffn_g0"], weight, emit, ffn0_saved, dh2)
    tok = emit("even_w_out", _mm("even_out_dw", "tn", cat0, dh1, D, D, T, (BF16,)))
    dcat0 = _mm("even_out_dx", "nt", dh1, weight("even_w_out", dh1), T, D, D, (F32,), tie=tok)
    dcv, grads["even_ln_g"], grads["even_ln_b"], grads["even_conv_b"] = _econv_bwd_ln(
        cv, dcat0, p["even_ln_g"], p["even_ln_b"])
    dz0, grads["even_conv_k"] = _econv_bwd_conv(dcv, zc, p["even_conv_k"])
    dqkv = lax.empty((3, 3, 4, T, LANES), F32)
    for g in range(3):
        dqkv = _attn_bwd(qkv, lses[g], wgts[g], att, dcat0, dqkv, g)
    dz0 = _rope_bwd(dqkv, rope_c, rope_s, dz0)
    tok = emit("even_w_in", _mm("even_in_dw", "tn", hn0, dz0, D, EVEN_IN, T, (BF16,)))
    dhn0 = _mm("even_in_dx", "nt", dz0, weight("even_w_in", dz0), T, D, EVEN_IN, (F32,), tie=tok)
    dx, grads["norm_mix_g0"] = _rms_bwd("mix0_dnorm", x, dhn0, p["norm_mix_g0"], dh1)
    return loss[0, 0], dx, grads


def _rowwise(name, fn, ins, out_dtypes, tm=256):
    rows, cols = ins[0].shape
    tm = tm if rows % tm == 0 else rows
    n_in = len(ins)

    def body(*refs):
        vals = fn(*[r[...] for r in refs[:n_in]])
        for o_ref, v in zip(refs[n_in:], vals):
            o_ref[...] = v.astype(o_ref.dtype)

    spec = pl.BlockSpec((tm, cols), lambda i: (i, 0))
    outs = _pcall(
        body, name=name, grid=(rows // tm,),
        in_specs=[spec] * n_in, out_specs=[spec] * len(out_dtypes),
        out_shape=[jax.ShapeDtypeStruct((rows, cols), dt) for dt in out_dtypes],
        compiler_params=_params("parallel"),
    )(*ins)
    return outs[0] if len(out_dtypes) == 1 else outs


def _adamw(name, w, g, m, v):
    c1 = 1.0 - ADAM_B1 ** ADAM_STEP
    c2 = 1.0 - ADAM_B2 ** ADAM_STEP

    def fn(w_t, g_t, m_t, v_t):
        m_new = ADAM_B1 * m_t + (1.0 - ADAM_B1) * g_t
        v_new = ADAM_B2 * v_t + (1.0 - ADAM_B2) * (g_t * g_t)
        delta = -ADAM_LR * ((m_new / c1) / (jnp.sqrt(v_new / c2) + ADAM_EPS) + ADAM_WD * w_t)
        return delta, m_new, v_new

    return _rowwise(name, fn, (w, g, m, v), (F32, F32, F32))


class _Piece:
    def __init__(self, name, rows, cols, axis, src, src_row0):
        self.name, self.rows, self.cols, self.axis = name, rows, cols, axis
        self.width = (cols if axis == 1 else rows) // 4
        self.src, self.src_row0 = src, src_row0

    @property
    def full_shape(self):
        return (self.rows, self.cols)

    @property
    def half_shape(self):
        return (self.rows // 2, self.cols) if self.axis == 1 else (self.rows, self.cols // 2)

    @property
    def shard_half_shape(self):
        return (self.rows // 2, self.width) if self.axis == 1 else (self.width, self.cols // 2)

    def shard_whole(self, ref):
        n = self.rows if self.axis == 1 else self.width
        return ref.at[pl.ds(self.src_row0, n), :]

    def shard_half(self, ref, h):
        if self.axis == 1:
            return ref.at[pl.ds(self.src_row0 + h * (self.rows // 2), self.rows // 2), :]
        return ref.at[pl.ds(self.src_row0, self.width), pl.ds(h * (self.cols // 2), self.cols // 2)]

    def full_shard(self, ref, s):
        if self.axis == 1:
            return ref.at[:, pl.ds(s * self.width, self.width)]
        return ref.at[pl.ds(s * self.width, self.width), :]

    def full_shard_half(self, ref, s, h):
        if self.axis == 1:
            return ref.at[pl.ds(h * (self.rows // 2), self.rows // 2), pl.ds(s * self.width, self.width)]
        return ref.at[pl.ds(s * self.width, self.width), pl.ds(h * (self.cols // 2), self.cols // 2)]

    def full_half(self, ref, h):
        if self.axis == 1:
            return ref.at[pl.ds(h * (self.rows // 2), self.rows // 2), :]
        return ref.at[:, pl.ds(h * (self.cols // 2), self.cols // 2)]

    def full_half_rows(self, ref, h, r0, n):
        if self.axis == 1:
            return ref.at[pl.ds(h * (self.rows // 2) + r0, n), :]
        return ref.at[pl.ds(r0, n), pl.ds(h * (self.cols // 2), self.cols // 2)]

    def half_shard(self, ref, s):
        return self.full_shard(ref, s)


PIECES = (
    _Piece("even_w_in", D, EVEN_IN, 1, 0, 0),
    _Piece("even_w_out", D, D, 0, 1, 0),
    _Piece("odd_w_in", D, ODD_IN, 1, 2, 0),
    _Piece("odd_w_out", D, D, 0, 3, 0),
    _Piece("ffn_w1_0", D, D_FF, 1, 4, 0),
    _Piece("ffn_w1_1", D, D_FF, 1, 4, D),
    _Piece("ffn_w2_0", D_FF, D, 0, 5, 0),
    _Piece("ffn_w2_1", D_FF, D, 0, 5, D_FF // 4),
)
N_PIECES = len(PIECES)
N_SHARD_OPERANDS = 6
ANY = pl.BlockSpec(memory_space=pl.ANY)
MESH = pl.DeviceIdType.MESH


def _mesh_place():
    x, y, c = lax.axis_index("x"), lax.axis_index("y"), lax.axis_index("c")
    chips = [(1 - x, y), (x, 1 - y), (1 - x, 1 - y)]
    return x, y, c, chips


def _remote(src, dst, send_sem, recv_sem, dev):
    return pltpu.make_async_remote_copy(src_ref=src, dst_ref=dst, send_sem=send_sem, recv_sem=recv_sem,
                                        device_id=dev, device_id_type=MESH)


HBM = pl.BlockSpec(memory_space=pltpu.HBM)
SEM = pl.BlockSpec(memory_space=pltpu.SEMAPHORE)
SPLIT_PARAMS = pltpu.CompilerParams(has_side_effects=pltpu.SideEffectType.DATAFLOW_SIDE_EFFECTING)
CAST_TILE = 256


def _in_hbm(a):
    return pltpu.with_memory_space_constraint(a, pltpu.HBM)


def _cast_place(pc, shard_operand, chip, tie=None):
    rows, cols = (pc.rows, pc.width) if pc.axis == 1 else (pc.width, pc.cols)
    nblk = rows // CAST_TILE
    blk0 = pc.src_row0 // CAST_TILE
    ties = () if tie is None else (tie,)

    def body(chip_ref, x_ref, *rest):
        del chip_ref
        rest[-1][...] = x_ref[...].astype(BF16)

    if pc.axis == 1:
        out_map = lambda i, chip_ref: (i, chip_ref[0])
    else:
        out_map = lambda i, chip_ref: (chip_ref[0] * nblk + i, 0)
    return _pcall(
        body, name=f"cast_{pc.name}",
        grid_spec=pltpu.PrefetchScalarGridSpec(
            num_scalar_prefetch=1, grid=(nblk,),
            in_specs=[pl.BlockSpec((CAST_TILE, cols), lambda i, chip_ref: (blk0 + i, 0))]
            + [pl.BlockSpec(TOKEN_SHAPE, lambda i, chip_ref: (0, 0))] * len(ties),
            out_specs=pl.BlockSpec((CAST_TILE, cols), out_map)),
        out_shape=jax.ShapeDtypeStruct(pc.full_shape, BF16),
        compiler_params=_params("parallel"),
    )(chip, shard_operand, *ties)


def _gather_start(name, pieces, fulls):
    n = len(pieces)

    def body(*refs):
        ins = refs[:n]
        sends = refs[2 * n:3 * n]
        recvs = refs[3 * n:4 * n]
        token = refs[4 * n]
        x, y, c, chips = _mesh_place()
        s = 2 * x + y
        for i, pc in enumerate(pieces):
            win = pc.full_shard_half(ins[i], s, c)
            for k, (cx, cy) in enumerate(chips):
                _remote(win, win, sends[i].at[k], recvs[i].at[k], (cx, cy, c)).start()
        token[...] = jnp.zeros(TOKEN_SHAPE, F32)

    sems = [pltpu.SemaphoreType.DMA((3,))] * (2 * n)
    outs = _pcall(
        body, name=name,
        in_specs=[HBM] * n,
        out_specs=[HBM] * n + [SEM] * (2 * n) + [pl.BlockSpec(memory_space=pltpu.VMEM)],
        out_shape=[pltpu.HBM(pc.full_shape, BF16) for pc in pieces] + sems + [jax.ShapeDtypeStruct(TOKEN_SHAPE, F32)],
        input_output_aliases={i: i for i in range(n)},
        compiler_params=SPLIT_PARAMS,
    )(*[_in_hbm(f) for f in fulls])
    return outs[:n], outs[n:2 * n], outs[2 * n:3 * n], outs[3 * n]


def _gather_wait(pc, full, send_sems, recv_sems, after):
    def body(full_ref, send_ref, recv_ref, after_ref, out_ref):
        del after_ref, out_ref
        x, y, c, chips = _mesh_place()
        for k, (cx, cy) in enumerate(chips):
            win = pc.full_shard_half(full_ref, 2 * cx + cy, c)
            cp = _remote(win, win, send_ref.at[k], recv_ref.at[k], (cx, cy, c))
            cp.wait_send()
            cp.wait_recv()

    return _pcall(
        body, name=f"gather_wait_{pc.name}",
        in_specs=[HBM, SEM, SEM, ANY], out_specs=HBM, out_shape=pltpu.HBM(pc.full_shape, BF16),
        input_output_aliases={0: 0}, compiler_params=SPLIT_PARAMS,
    )(full, send_sems, recv_sems, after)


def _core_forward(pc, full):
    sh = pc.shard_half_shape

    def body(full_in, full_ref, send_buf, recv_buf, load_sems, send_sems, recv_sems, store_sems):
        x, y, c, chips = _mesh_place()
        loads, sends, stores = [], [], []
        for k, (cx, cy) in enumerate(chips):
            cp = pltpu.make_async_copy(pc.full_shard_half(full_in, 2 * cx + cy, c), send_buf.at[k], load_sems.at[k])
            cp.start()
            loads.append(cp)
        for k in range(3):
            loads[k].wait()
            cp = _remote(send_buf.at[k], recv_buf.at[k], send_sems.at[k], recv_sems.at[k], (x, y, 1 - c))
            cp.start()
            sends.append(cp)
        for k, (cx, cy) in enumerate(chips):
            sends[k].wait_recv()
            cp = pltpu.make_async_copy(recv_buf.at[k], pc.full_shard_half(full_ref, 2 * cx + cy, 1 - c), store_sems.at[k])
            cp.start()
            stores.append(cp)
        for k in range(3):
            sends[k].wait_send()
            stores[k].wait()

    sems = pltpu.SemaphoreType.DMA((3,))
    return _pcall(
        body, name=f"core_forward_{pc.name}", in_specs=[ANY], out_specs=ANY,
        out_shape=jax.ShapeDtypeStruct(pc.full_shape, BF16),
        scratch_shapes=[pltpu.VMEM((3,) + sh, BF16), pltpu.VMEM((3,) + sh, BF16), sems, sems, sems, sems],
        input_output_aliases={0: 0},
        compiler_params=pltpu.CompilerParams(vmem_limit_bytes=VMEM_LIMIT),
    )(full)


CHIPSUM_CHUNKS = 4


def _chipsum(pc, partial):
    hr, hc = pc.half_shape
    ch = hr // CHIPSUM_CHUNKS

    def body(g_ref, out_ref, send_buf, recv_buf, own_buf, sum_buf, load_sems, own_sems, send_sems, recv_sems, out_sems):
        x, y, c, _ = _mesh_place()
        chunks = [pl.ds(k * ch, ch) for k in range(CHIPSUM_CHUNKS)]
        loads, owns, sends, stores = [], [], [], []
        for k, rows in enumerate(chunks):
            cp = pltpu.make_async_copy(pc.full_half_rows(g_ref, 1 - c, k * ch, ch), send_buf.at[rows, :], load_sems.at[k])
            cp.start()
            loads.append(cp)
            cp = pltpu.make_async_copy(pc.full_half_rows(g_ref, c, k * ch, ch), own_buf.at[rows, :], own_sems.at[k])
            cp.start()
            owns.append(cp)
        for k, rows in enumerate(chunks):
            loads[k].wait()
            cp = _remote(send_buf.at[rows, :], recv_buf.at[rows, :], send_sems.at[k], recv_sems.at[k], (x, y, 1 - c))
            cp.start()
            sends.append(cp)
        for k, rows in enumerate(chunks):
            owns[k].wait()
            sends[k].wait_recv()
            sum_buf[rows, :] = (own_buf[rows, :].astype(F32) + recv_buf[rows, :].astype(F32)).astype(BF16)
            cp = pltpu.make_async_copy(sum_buf.at[rows, :], out_ref.at[rows, :], out_sems.at[k])
            cp.start()
            stores.append(cp)
        for k in range(CHIPSUM_CHUNKS):
            sends[k].wait_send()
            stores[k].wait()

    buf = pltpu.VMEM((hr, hc), BF16)
    sems = pltpu.SemaphoreType.DMA((CHIPSUM_CHUNKS,))
    return _pcall(
        body, name=f"chipsum_{pc.name}", in_specs=[ANY], out_specs=ANY,
        out_shape=jax.ShapeDtypeStruct((hr, hc), BF16),
        scratch_shapes=[buf, buf, buf, buf, sems, sems, sems, sems, sems],
        compiler_params=pltpu.CompilerParams(vmem_limit_bytes=VMEM_LIMIT),
    )(partial)


def _scatter_start(pc, chip_sum):
    def body(sum_ref, land_ref, sum_out, land_out, sends, recvs, token):
        del sum_out, land_out
        x, y, c, chips = _mesh_place()
        for k, (cx, cy) in enumerate(chips):
            _remote(pc.half_shard(sum_ref, 2 * cx + cy), land_ref.at[k], sends.at[k], recvs.at[k], (cx, cy, c)).start()
        token[...] = jnp.zeros(TOKEN_SHAPE, F32)

    land_shape = (3,) + pc.shard_half_shape
    sems = pltpu.SemaphoreType.DMA((3,))
    return _pcall(
        body, name=f"scatter_start_{pc.name}",
        in_specs=[HBM, HBM], out_specs=[HBM, HBM, SEM, SEM, pl.BlockSpec(memory_space=pltpu.VMEM)],
        out_shape=[pltpu.HBM(pc.half_shape, BF16), pltpu.HBM(land_shape, BF16), sems, sems,
                   jax.ShapeDtypeStruct(TOKEN_SHAPE, F32)],
        input_output_aliases={0: 0, 1: 1}, compiler_params=SPLIT_PARAMS,
    )(_in_hbm(chip_sum), _in_hbm(lax.empty(land_shape, BF16)))


def _scatter_wait(pc, chip_sum, land, send_sems, recv_sems, after):
    def body(sum_ref, land_ref, send_ref, recv_ref, after_ref, sum_out, land_out):
        del after_ref, sum_out, land_out
        x, y, c, chips = _mesh_place()
        for k, (cx, cy) in enumerate(chips):
            cp = _remote(pc.half_shard(sum_ref, 2 * cx + cy), land_ref.at[k], send_ref.at[k], recv_ref.at[k], (cx, cy, c))
            cp.wait_send()
            cp.wait_recv()

    return _pcall(
        body, name=f"scatter_wait_{pc.name}",
        in_specs=[HBM, HBM, SEM, SEM, ANY], out_specs=[HBM, HBM],
        out_shape=[pltpu.HBM(pc.half_shape, BF16), pltpu.HBM((3,) + pc.shard_half_shape, BF16)],
        input_output_aliases={0: 0, 1: 1}, compiler_params=SPLIT_PARAMS,
    )(chip_sum, land, send_sems, recv_sems, after)


SHARD_OPERAND_SHAPES = ((D, EVEN_IN // 4), (D // 4, D), (D, ODD_IN // 4), (D // 4, D), (2 * D, D_FF // 4), (2 * D_FF // 4, D))


def _allsum_join(operand, chip_sums, lands):
    pieces = [pc for pc in PIECES if pc.src == operand]
    n = len(pieces)

    def body(*refs):
        sum_refs = refs[:n]
        land_refs = refs[n:2 * n]
        refs = refs[n:]
        out_ref = refs[n]
        in_bufs = refs[n + 1:2 * n + 1]
        fin_bufs = refs[2 * n + 1:3 * n + 1]
        recv_bufs = refs[3 * n + 1:4 * n + 1]
        load_sems, send_sems, recv_sems, out_sems = refs[4 * n + 1:]
        x, y, c, _ = _mesh_place()
        s = 2 * x + y
        loads, sends, stores = [], [], []
        for j, pc in enumerate(pieces):
            cp = pltpu.make_async_copy(land_refs[j], in_bufs[j].at[pl.ds(0, 3)], load_sems.at[2 * j])
            cp.start()
            loads.append(cp)
            cp = pltpu.make_async_copy(pc.half_shard(sum_refs[j], s), in_bufs[j].at[3], load_sems.at[2 * j + 1])
            cp.start()
            loads.append(cp)
        for j, pc in enumerate(pieces):
            loads[2 * j].wait()
            loads[2 * j + 1].wait()
            acc = in_bufs[j][0].astype(F32)
            for k in range(1, 4):
                acc = acc + in_bufs[j][k].astype(F32)
            fin_bufs[j][...] = acc
            cp = pltpu.make_async_copy(fin_bufs[j], pc.shard_half(out_ref, c), out_sems.at[2 * j])
            cp.start()
            stores.append(cp)
            cp = _remote(fin_bufs[j], recv_bufs[j], send_sems.at[j], recv_sems.at[j], (x, y, 1 - c))
            cp.start()
            sends.append(cp)
        for j, pc in enumerate(pieces):
            sends[j].wait_recv()
            cp = pltpu.make_async_copy(recv_bufs[j], pc.shard_half(out_ref, 1 - c), out_sems.at[2 * j + 1])
            cp.start()
            stores.append(cp)
        for cp in sends:
            cp.wait_send()
        for cp in stores:
            cp.wait()

    sh = pieces[0].shard_half_shape
    return _pcall(
        body, name=f"allsum_join_{operand}", in_specs=[ANY] * (2 * n), out_specs=ANY,
        out_shape=jax.ShapeDtypeStruct(SHARD_OPERAND_SHAPES[operand], F32),
        scratch_shapes=[pltpu.VMEM((4,) + sh, BF16)] * n + [pltpu.VMEM(sh, F32)] * (2 * n)
        + [pltpu.SemaphoreType.DMA((2 * n,)), pltpu.SemaphoreType.DMA((n,)), pltpu.SemaphoreType.DMA((n,)),
           pltpu.SemaphoreType.DMA((2 * n,))],
        compiler_params=pltpu.CompilerParams(vmem_limit_bytes=VMEM_LIMIT),
    )(*chip_sums, *lands)


def _allgather8(name, blk, with_sum):
    m = blk.shape[0]

    def body(x_ref, out_ref, *rest):
        if with_sum:
            sum_ref, send_sems, recv_sems, local_sem = rest
        else:
            send_sems, recv_sems, local_sem = rest
        x, y, c, chips = _mesh_place()
        me, sibling = (x, y, c), (x, y, 1 - c)

        def rows(px, py, pc):
            return out_ref.at[pl.ds((4 * px + 2 * py + pc) * m, m), :]

        def copy(k, block, to, src=None):
            return _remote(rows(*block) if src is None else src, rows(*block), send_sems.at[k], recv_sems.at[k], to)

        mine = pltpu.make_async_copy(x_ref, rows(*me), local_sem)
        mine.start()
        first = [copy(0, me, sibling, src=x_ref)]
        first += [copy(1 + j, me, (*chip, c), src=x_ref) for j, chip in enumerate(chips)]
        for cp in first:
            cp.start()
        passed = [copy(4 + j, (*chip, c), sibling) for j, chip in enumerate(chips)]
        for j, chip in enumerate(chips):
            copy(1 + j, (*chip, c), me).wait_recv()
            passed[j].start()
        copy(0, sibling, me).wait_recv()
        for j, chip in enumerate(chips):
            copy(4 + j, (*chip, 1 - c), me).wait_recv()
        for cp in first + passed:
            cp.wait_send()
        mine.wait()
        if with_sum:
            acc = out_ref[0:m, :]
            for dev in range(1, 8):
                acc = acc + out_ref[dev * m:(dev + 1) * m, :]
            sum_ref[...] = acc

    vm = pl.BlockSpec(memory_space=pltpu.VMEM)
    out_shape = [jax.ShapeDtypeStruct((8 * m, LANES), F32)]
    if with_sum:
        out_shape.append(jax.ShapeDtypeStruct((m, LANES), F32))
    return _pcall(
        body, name=name, in_specs=[vm], out_specs=[vm] * len(out_shape), out_shape=out_shape,
        scratch_shapes=[pltpu.SemaphoreType.DMA((7,)), pltpu.SemaphoreType.DMA((7,)), pltpu.SemaphoreType.DMA],
    )(blk)


def _pack(arrays, row_counts):
    rows = []
    for a, n in zip(arrays, row_counts):
        flat = a.reshape(-1, LANES)
        rows.append(jnp.pad(flat, ((0, n - flat.shape[0]), (0, 0))))
    return jnp.concatenate(rows, axis=0)


def _unpack(buf, shapes, row_counts):
    out, r0 = [], 0
    for sh, n in zip(shapes, row_counts):
        size = 1
        for dim in sh:
            size *= dim
        out.append(buf[r0:r0 + size // LANES].reshape(sh))
        r0 += n
    return out


REPL_NAMES = ("norm_mix_g", "norm_ffn_g", "even_conv_b", "even_ln_g", "even_ln_b", "odd_sg_w", "odd_sg_b", "final_g")
REPL_SHAPES = ((2, D), (2, D), (1, 512), (1, 512), (1, 512), (1, SG_GROUPS, CHUNK, CHUNK), (1, SG_GROUPS, CHUNK), (D,))
REPL_ROWS = (16, 16, 8, 8, 8, 512, 8, 8)
SHARDED_NAMES = ("even_conv_k", "odd_conv_k", "odd_ln_g", "odd_ln_b")
SHARDED_SHARD_SHAPES = ((1, CONV_W, LANES), (1, SCONV_W, LANES), (1, LANES), (1, LANES))
SHARDED_SHARD_ROWS = (32, 8, 8, 8)
SHARDED_FULL_SHAPES = ((CONV_W, 512), (SCONV_W, 512), (1, 512), (1, 512))
SHARDED_FULL_ROWS = (128, 16, 8, 8)


def kernel(x, norm_mix_g, norm_ffn_g, even_w_in, even_conv_k, even_conv_b, even_ln_g, even_ln_b, even_w_out, odd_w_in, odd_conv_k, odd_ln_g, odd_ln_b, odd_sg_w, odd_sg_b, odd_w_out, ffn_w1, ffn_w2, final_g, loss_target, m_norm_mix_g, m_norm_ffn_g, m_even_w_in, m_even_conv_k, m_even_conv_b, m_even_ln_g, m_even_ln_b, m_even_w_out, m_odd_w_in, m_odd_conv_k, m_odd_ln_g, m_odd_ln_b, m_odd_sg_w, m_odd_sg_b, m_odd_w_out, m_ffn_w1, m_ffn_w2, m_final_g, v_norm_mix_g, v_norm_ffn_g, v_even_w_in, v_even_conv_k, v_even_conv_b, v_even_ln_g, v_even_ln_b, v_even_w_out, v_odd_w_in, v_odd_conv_k, v_odd_ln_g, v_odd_ln_b, v_odd_sg_w, v_odd_sg_b, v_odd_w_out, v_ffn_w1, v_ffn_w2, v_final_g):
    names = ("norm_mix_g", "norm_ffn_g", "even_w_in", "even_conv_k", "even_conv_b", "even_ln_g", "even_ln_b", "even_w_out",
             "odd_w_in", "odd_conv_k", "odd_ln_g", "odd_ln_b", "odd_sg_w", "odd_sg_b", "odd_w_out", "ffn_w1", "ffn_w2", "final_g")
    w = dict(zip(names, (norm_mix_g, norm_ffn_g, even_w_in, even_conv_k, even_conv_b, even_ln_g, even_ln_b, even_w_out,
                         odd_w_in, odd_conv_k, odd_ln_g, odd_ln_b, odd_sg_w, odd_sg_b, odd_w_out, ffn_w1, ffn_w2, final_g)))
    mom = dict(zip(names, (m_norm_mix_g, m_norm_ffn_g, m_even_w_in, m_even_conv_k, m_even_conv_b, m_even_ln_g, m_even_ln_b,
                           m_even_w_out, m_odd_w_in, m_odd_conv_k, m_odd_ln_g, m_odd_ln_b, m_odd_sg_w, m_odd_sg_b, m_odd_w_out,
                           m_ffn_w1, m_ffn_w2, m_final_g)))
    vel = dict(zip(names, (v_norm_mix_g, v_norm_ffn_g, v_even_w_in, v_even_conv_k, v_even_conv_b, v_even_ln_g, v_even_ln_b,
                           v_even_w_out, v_odd_w_in, v_odd_conv_k, v_odd_ln_g, v_odd_ln_b, v_odd_sg_w, v_odd_sg_b, v_odd_w_out,
                           v_ffn_w1, v_ffn_w2, v_final_g)))
    big_names = ("even_w_in", "even_w_out", "odd_w_in", "odd_w_out", "ffn_w1", "ffn_w2")
    chip = 2 * lax.axis_index("x") + lax.axis_index("y")

    def shard2d(t, name):
        return t[name].reshape(SHARD_OPERAND_SHAPES[big_names.index(name)])

    chip_op = jnp.reshape(chip, (1,)).astype(jnp.int32)
    first = _cast_place(PIECES[0], shard2d(w, big_names[PIECES[0].src]), chip_op)
    fly0, send0, recv0, token = _gather_start("gather_start_first", PIECES[:1], [first])
    placed = [_cast_place(pc, shard2d(w, big_names[pc.src]), chip_op, tie=token) for pc in PIECES[1:]]
    fly1, send1, recv1, _ = _gather_start("gather_start_rest", PIECES[1:], placed)
    flying, gather_send, gather_recv = fly0 + fly1, send0 + send1, recv0 + recv1
    ready = {}

    def weight(name, after):
        if name not in ready:
            i = [pc.name for pc in PIECES].index(name)
            landed = _gather_wait(PIECES[i], flying[i], gather_send[i], gather_recv[i], after)
            ready[name] = _core_forward(PIECES[i], landed)
        return ready[name]

    scattering = []

    def emit(name, partial):
        pc = PIECES[[q.name for q in PIECES].index(name)]
        chip_sum, land, send_sems, recv_sems, token = _scatter_start(pc, _chipsum(pc, partial))
        scattering.append((pc, chip_sum, land, send_sems, recv_sems))
        return token

    full = {}
    small_pack = _pack([w[n] for n in SHARDED_NAMES], SHARDED_SHARD_ROWS)
    gathered = _allgather8("gather_small", small_pack, False)[0].reshape(4, 2, sum(SHARDED_SHARD_ROWS), LANES)[:, 0]
    r0 = 0
    for n, sh, rows, full_sh in zip(SHARDED_NAMES, SHARDED_SHARD_SHAPES, SHARDED_SHARD_ROWS, SHARDED_FULL_SHAPES):
        per_chip = gathered[:, r0:r0 + rows].reshape(4, -1)[:, :full_sh[0] * LANES].reshape(4, full_sh[0], LANES)
        full[n] = jnp.transpose(per_chip, (1, 0, 2)).reshape(full_sh)
        r0 += rows
    p = dict(full)
    p.update(norm_mix_g0=norm_mix_g[0:1], norm_mix_g1=norm_mix_g[1:2], norm_ffn_g0=norm_ffn_g[0:1], norm_ffn_g1=norm_ffn_g[1:2],
             even_conv_b=even_conv_b, even_ln_g=even_ln_g, even_ln_b=even_ln_b,
             odd_sg_w=odd_sg_w[0], odd_sg_bt=odd_sg_b[0].T, final_g=final_g[None, :])

    loss, dx, g = _local_step(x[0], loss_target[0], p, weight, emit)
    loss = lax.psum(loss, ("x", "y", "c"))

    landed = {pc.name: _scatter_wait(pc, chip_sum, land, send_sems, recv_sems, dx)
              for pc, chip_sum, land, send_sems, recv_sems in scattering}
    big_grads = {n: _allsum_join(o, [landed[pc.name][0] for pc in PIECES if pc.src == o],
                                 [landed[pc.name][1] for pc in PIECES if pc.src == o])
                 for o, n in enumerate(big_names)}

    grad_parts = [g["norm_mix_g0"], g["norm_mix_g1"], g["norm_ffn_g0"], g["norm_ffn_g1"], g["even_conv_b"], g["even_ln_g"],
                  g["even_ln_b"], g["odd_sg_w"], g["odd_sg_bt"].T, g["final_g"],
                  g["even_conv_k"], g["odd_conv_k"], g["odd_ln_g"], g["odd_ln_b"]]
    grad_pack = _pack(grad_parts, (8, 8, 8, 8) + REPL_ROWS[2:] + SHARDED_FULL_ROWS)
    grad_sum = _allgather8("allreduce_small", grad_pack, True)[1]
    parts = _unpack(grad_sum, REPL_SHAPES + SHARDED_FULL_SHAPES, REPL_ROWS + SHARDED_FULL_ROWS)
    grads = dict(zip(REPL_NAMES, parts[:len(REPL_NAMES)]))
    for n, full_g, sh in zip(SHARDED_NAMES, parts[len(REPL_NAMES):], SHARDED_SHARD_SHAPES):
        grads[n] = lax.dynamic_slice_in_dim(full_g, chip * LANES, LANES, axis=1).reshape(sh)
    for n in big_names:
        grads[n] = big_grads[n].reshape(w[n].shape)

    delta, new_m, new_v = {}, {}, {}
    for n in big_names:
        d2, m2, v2 = _adamw(f"adamw_{n}", shard2d(w, n), big_grads[n], shard2d(mom, n), shard2d(vel, n))
        delta[n], new_m[n], new_v[n] = (t.reshape(w[n].shape) for t in (d2, m2, v2))
    for tag, group, rows, shapes in (("repl", REPL_NAMES, REPL_ROWS, [w[n].shape for n in REPL_NAMES]),
                                     ("sharded", SHARDED_NAMES, SHARDED_SHARD_ROWS, SHARDED_SHARD_SHAPES)):
        packs = [_pack([t[n] for n in group], rows) for t in (w, grads, mom, vel)]
        outs = _adamw(f"adamw_{tag}", *packs)
        for res, o in zip((delta, new_m, new_v), outs):
            res.update(zip(group, _unpack(o, shapes, rows)))

    out = [loss, dx[None]]
    for res in (grads, delta, new_m, new_v):
        out.extend(res[n] for n in names)
    return tuple(out)
```

```python
import functools

import jax
import jax.numpy as jnp
from jax import lax
from jax.experimental import pallas as pl
from jax.experimental.pallas import tpu as pltpu

F32 = jnp.float32
BF16 = jnp.bfloat16

T = 2048
D = 1024
CONV_CH = 512
CONV_W = 31
HEAD_DIM = 64
ATT_W = 1536
EVEN_IN = 5632
ODD_IN = 2560
SCONV_W = 3
SG_GROUPS = 4
CHUNK = 128
D_FF = 4096
EPS = 1e-6
DILATIONS = (1, 4, 16)
BAND = 128
SCALE = HEAD_DIM ** -0.5
NEG = -1e30

ADAM_LR = 0.001
ADAM_B1 = 0.9
ADAM_B2 = 0.999
ADAM_EPS = 1e-08
ADAM_WD = 0.01
ADAM_STEP = 10

V7X_VMEM_BYTES = 64 * 2 ** 20
VMEM_LIMIT = V7X_VMEM_BYTES - 8 * 2 ** 20
LANES = 128
TOKEN_SHAPE = (8, LANES)


def _pcall(body, **kw):
    return pl.pallas_call(body, **kw)


def _params(*sem):
    return pltpu.CompilerParams(dimension_semantics=sem, vmem_limit_bytes=VMEM_LIMIT)


def _dot(a, b, dims):
    return lax.dot_general(a, b, (dims, ((), ())), preferred_element_type=F32)


def _nn(a, b):
    return _dot(a, b, ((1,), (0,)))


def _nt(a, b):
    return _dot(a, b, ((1,), (1,)))


def _tn(a, b):
    return _dot(a, b, ((0,), (0,)))


def _sigmoid(x):
    return 1.0 / (1.0 + jnp.exp(-x))


MM_VMEM_BUDGET = 40 * 2 ** 20


def _mm_tiles(mode, m, n, k, a_bytes, b_bytes, extra_bytes, out_bytes):
    def divisors(total, unit):
        return [t for t in range(unit, total + 1, unit) if total % t == 0]

    best = None
    for tm in divisors(m, LANES if mode == "tn" else 8):
        for tn in divisors(n, LANES):
            blocks = tm * k * a_bytes + tn * k * b_bytes + tm * tn * (extra_bytes + out_bytes)
            casts = (tm * k * 2 if a_bytes == 4 else 0) + (tn * k * 2 if b_bytes == 4 else 0)
            if 2 * blocks + casts + tm * tn * 4 > MM_VMEM_BUDGET:
                continue
            key = ((m // tm) * (n // tn), (m // tm) * n * k * b_bytes, abs(tm - tn))
            if best is None or key < best[0]:
                best = (key, tm, tn)
    return best[1], best[2]


def _mm(name, mode, a, b, m, n, k, out_dtypes, *, b_off=0, extras=(), epi=None, tie=None):
    tm, tn = _mm_tiles(mode, m, n, k, a.dtype.itemsize, b.dtype.itemsize, sum(e.dtype.itemsize for e in extras),
                       sum(jnp.dtype(dt).itemsize for dt in out_dtypes))
    assert b_off % tn == 0
    b_off //= tn
    if mode == "nn":
        a_spec = pl.BlockSpec((tm, k), lambda i, j: (i, 0))
        b_spec = pl.BlockSpec((k, tn), lambda i, j: (0, j + b_off))
        dims = ((1,), (0,))
    elif mode == "nt":
        a_spec = pl.BlockSpec((tm, k), lambda i, j: (i, 0))
        b_spec = pl.BlockSpec((tn, k), lambda i, j: (j, 0))
        dims = ((1,), (1,))
    else:
        a_spec = pl.BlockSpec((k, tm), lambda i, j: (0, i))
        b_spec = pl.BlockSpec((k, tn), lambda i, j: (0, j))
        dims = ((0,), (0,))
    o_spec = pl.BlockSpec((tm, tn), lambda i, j: (i, j))
    n_extra = len(extras)
    ties = () if tie is None else (tie,)

    def body(a_ref, b_ref, *rest):
        rest = rest[len(ties):]
        acc = _dot(a_ref[...].astype(BF16), b_ref[...].astype(BF16), dims)
        vals = epi(acc, *[e[...] for e in rest[:n_extra]]) if epi is not None else (acc,)
        for o_ref, v in zip(rest[n_extra:], vals):
            o_ref[...] = v.astype(o_ref.dtype)

    outs = _pcall(
        body, name=name, grid=(m // tm, n // tn),
        in_specs=[a_spec, b_spec] + [pl.BlockSpec(TOKEN_SHAPE, lambda i, j: (0, 0))] * len(ties) + [o_spec] * n_extra,
        out_specs=[o_spec] * len(out_dtypes),
        out_shape=[jax.ShapeDtypeStruct((m, n), dt) for dt in out_dtypes],
        compiler_params=_params("parallel", "parallel"),
    )(a, b, *ties, *extras)
    return outs[0] if len(out_dtypes) == 1 else outs


def _rms_fwd(name, h, g, tm=512):
    def body(h_ref, g_ref, o_ref):
        x = h_ref[...]
        r = lax.rsqrt(jnp.mean(x * x, axis=-1, keepdims=True) + EPS)
        o_ref[...] = ((x * r) * g_ref[...]).astype(BF16)

    return _pcall(
        body, name=name, grid=(T // tm,),
        in_specs=[pl.BlockSpec((tm, D), lambda i: (i, 0)), pl.BlockSpec((1, D), lambda i: (0, 0))],
        out_specs=pl.BlockSpec((tm, D), lambda i: (i, 0)),
        out_shape=jax.ShapeDtypeStruct((T, D), BF16),
        compiler_params=_params("parallel"),
    )(h, g)


def _rms_bwd(name, h, dhn, g, dres, tm=512):
    def body(h_ref, d_ref, g_ref, r_ref, dh_ref, dg_ref):
        x = h_ref[...]
        r = lax.rsqrt(jnp.mean(x * x, axis=-1, keepdims=True) + EPS)
        nrm = x * r
        dy = d_ref[...]
        dn = dy * g_ref[...]
        dh_ref[...] = r_ref[...] + r * (dn - nrm * jnp.mean(dn * nrm, axis=-1, keepdims=True))

        @pl.when(pl.program_id(0) == 0)
        def _():
            dg_ref[...] = jnp.zeros_like(dg_ref)

        dg_ref[...] += jnp.sum(dy * nrm, axis=0, keepdims=True)

    row = pl.BlockSpec((tm, D), lambda i: (i, 0))
    vec = pl.BlockSpec((1, D), lambda i: (0, 0))
    return _pcall(
        body, name=name, grid=(T // tm,),
        in_specs=[row, row, vec, row], out_specs=[row, vec],
        out_shape=[jax.ShapeDtypeStruct((T, D), F32), jax.ShapeDtypeStruct((1, D), F32)],
        compiler_params=_params("arbitrary"),
    )(h, dhn, g, dres)


def _loss_head(h, g, target, tm=512):
    def body(h_ref, g_ref, t_ref, dh_ref, dg_ref, loss_ref):
        x = h_ref[...]
        r = lax.rsqrt(jnp.mean(x * x, axis=-1, keepdims=True) + EPS)
        nrm = x * r
        gain = g_ref[...]
        err = nrm * gain - t_ref[...]
        dy = err * (1.0 / D)
        dn = dy * gain
        dh_ref[...] = r * (dn - nrm * jnp.mean(dn * nrm, axis=-1, keepdims=True))

        @pl.when(pl.program_id(0) == 0)
        def _():
            dg_ref[...] = jnp.zeros_like(dg_ref)
            loss_ref[...] = jnp.zeros_like(loss_ref)

        dg_ref[...] += jnp.sum(dy * nrm, axis=0, keepdims=True)
        part = jnp.sum(jnp.sum(err * err, axis=1, keepdims=True), axis=0, keepdims=True) * (0.5 / D)
        loss_ref[...] += jnp.broadcast_to(part, (1, LANES))

    row = pl.BlockSpec((tm, D), lambda i: (i, 0))
    vec = pl.BlockSpec((1, D), lambda i: (0, 0))
    return _pcall(
        body, name="loss_head", grid=(T // tm,),
        in_specs=[row, vec, row], out_specs=[row, vec, pl.BlockSpec((1, LANES), lambda i: (0, 0))],
        out_shape=[jax.ShapeDtypeStruct((T, D), F32), jax.ShapeDtypeStruct((1, D), F32),
                   jax.ShapeDtypeStruct((1, LANES), F32)],
        compiler_params=_params("arbitrary"),
    )(h, g, target)


CONV_TILE = 256
CONV_HALO = 32


def _glu(z):
    return z[:, :CONV_CH] * _sigmoid(z[:, CONV_CH:])


def _econv_fwd(zc, conv_k, conv_b, ln_g, ln_b):
    R, H = CONV_TILE, CONV_HALO

    def body(z_ref, zh_ref, k_ref, b_ref, g_ref, be_ref, cv_ref, cat_ref):
        i = pl.program_id(0)
        glu = _glu(z_ref[...])
        halo = _glu(zh_ref[...]) * (i > 0).astype(F32)
        win = jnp.concatenate([halo, glu], axis=0)
        acc = jnp.zeros((R, CONV_CH), F32) + b_ref[...]
        for j in range(CONV_W):
            off = H - (CONV_W - 1) + j
            acc = acc + k_ref[j:j + 1, :] * win[off:off + R, :]
        cv_ref[...] = acc
        mu = jnp.mean(acc, axis=-1, keepdims=True)
        xc = acc - mu
        rstd = lax.rsqrt(jnp.mean(xc * xc, axis=-1, keepdims=True) + EPS)
        ln = xc * rstd * g_ref[...] + be_ref[...]
        cat_ref[...] = (ln * _sigmoid(ln)).astype(BF16)

    vec = pl.BlockSpec((1, CONV_CH), lambda i: (0, 0))
    return _pcall(
        body, name="econv_fwd", grid=(T // R,),
        in_specs=[pl.BlockSpec((R, 2 * CONV_CH), lambda i: (i, 0)),
                  pl.BlockSpec((H, 2 * CONV_CH), lambda i: (jnp.maximum(i * (R // H) - 1, 0), 0)),
                  pl.BlockSpec((CONV_W, CONV_CH), lambda i: (0, 0)), vec, vec, vec],
        out_specs=[pl.BlockSpec((R, CONV_CH), lambda i: (i, 0)), pl.BlockSpec((R, CONV_CH), lambda i: (i, 0))],
        out_shape=[jax.ShapeDtypeStruct((T, CONV_CH), F32), jax.ShapeDtypeStruct((T, D), BF16)],
        compiler_params=_params("parallel"),
    )(zc, zc, conv_k, conv_b, ln_g, ln_b)


def _econv_bwd_ln(cv, dcat, ln_g, ln_b):
    R = CONV_TILE

    def body(cv_ref, d_ref, g_ref, be_ref, dcv_ref, dg_ref, dbe_ref, dcb_ref):
        cv_t = cv_ref[...]
        mu = jnp.mean(cv_t, axis=-1, keepdims=True)
        xc = cv_t - mu
        rstd = lax.rsqrt(jnp.mean(xc * xc, axis=-1, keepdims=True) + EPS)
        xh = xc * rstd
        ln = xh * g_ref[...] + be_ref[...]
        sg = _sigmoid(ln)
        dln = d_ref[...] * (sg * (1.0 + ln * (1.0 - sg)))
        dxh = dln * g_ref[...]
        dcv = rstd * (dxh - jnp.mean(dxh, axis=-1, keepdims=True) - xh * jnp.mean(dxh * xh, axis=-1, keepdims=True))
        dcv_ref[...] = dcv

        @pl.when(pl.program_id(0) == 0)
        def _():
            dg_ref[...] = jnp.zeros_like(dg_ref)
            dbe_ref[...] = jnp.zeros_like(dbe_ref)
            dcb_ref[...] = jnp.zeros_like(dcb_ref)

        dg_ref[...] += jnp.sum(dln * xh, axis=0, keepdims=True)
        dbe_ref[...] += jnp.sum(dln, axis=0, keepdims=True)
        dcb_ref[...] += jnp.sum(dcv, axis=0, keepdims=True)

    vec = pl.BlockSpec((1, CONV_CH), lambda i: (0, 0))
    row = pl.BlockSpec((R, CONV_CH), lambda i: (i, 0))
    vshape = jax.ShapeDtypeStruct((1, CONV_CH), F32)
    return _pcall(
        body, name="econv_bwd_ln", grid=(T // R,),
        in_specs=[row, row, vec, vec], out_specs=[row, vec, vec, vec],
        out_shape=[jax.ShapeDtypeStruct((T, CONV_CH), F32), vshape, vshape, vshape],
        compiler_params=_params("arbitrary"),
    )(cv, dcat, ln_g, ln_b)


def _econv_bwd_conv(dcv, zc, conv_k):
    R, H = CONV_TILE, CONV_HALO
    last = T // R - 1

    def body(d_ref, dn_ref, z_ref, zh_ref, k_ref, dz_ref, dk_ref):
        i = pl.program_id(0)
        z = z_ref[...]
        a_lin = z[:, :CONV_CH]
        sg = _sigmoid(z[:, CONV_CH:])
        glu = a_lin * sg
        halo = _glu(zh_ref[...]) * (i > 0).astype(F32)
        win = jnp.concatenate([halo, glu], axis=0)
        dcv_t = d_ref[...]
        nxt = dn_ref[...] * (i < last).astype(F32)
        winb = jnp.concatenate([dcv_t, nxt], axis=0)

        @pl.when(i == 0)
        def _():
            dk_ref[...] = jnp.zeros_like(dk_ref)

        dglu = jnp.zeros((R, CONV_CH), F32)
        for j in range(CONV_W):
            off = H - (CONV_W - 1) + j
            dk_ref[j:j + 1, :] += jnp.sum(dcv_t * win[off:off + R, :], axis=0, keepdims=True)
            ob = CONV_W - 1 - j
            dglu = dglu + k_ref[j:j + 1, :] * winb[ob:ob + R, :]
        dz_ref[...] = jnp.concatenate([dglu * sg, dglu * a_lin * sg * (1.0 - sg)], axis=1).astype(BF16)

    return _pcall(
        body, name="econv_bwd_conv", grid=(T // R,),
        in_specs=[pl.BlockSpec((R, CONV_CH), lambda i: (i, 0)),
                  pl.BlockSpec((H, CONV_CH), lambda i: (jnp.minimum((i + 1) * (R // H), T // H - 1), 0)),
                  pl.BlockSpec((R, 2 * CONV_CH), lambda i: (i, 0)),
                  pl.BlockSpec((H, 2 * CONV_CH), lambda i: (jnp.maximum(i * (R // H) - 1, 0), 0)),
                  pl.BlockSpec((CONV_W, CONV_CH), lambda i: (0, 0))],
        out_specs=[pl.BlockSpec((R, 2 * CONV_CH), lambda i: (i, 0)), pl.BlockSpec((CONV_W, CONV_CH), lambda i: (0, 0))],
        out_shape=[jax.ShapeDtypeStruct((T, EVEN_IN), BF16), jax.ShapeDtypeStruct((CONV_W, CONV_CH), F32)],
        compiler_params=_params("arbitrary"),
    )(dcv, dcv, zc, zc, conv_k)


def _swap_halves(v):
    lane = lax.broadcasted_iota(jnp.int32, v.shape, 1)
    return jnp.where((lane % HEAD_DIM) < HEAD_DIM // 2, pltpu.roll(v, LANES - HEAD_DIM // 2, 1),
                     pltpu.roll(v, HEAD_DIM // 2, 1))


def _qkv_proj(hn, w_in, rope_c, rope_s, tm=1024):
    tn = 4 * LANES

    def body(a_ref, b_ref, c_ref, s_ref, o_ref):
        j = pl.program_id(1)
        acc = _nn(a_ref[...], b_ref[...])
        for p in range(4):
            v = acc[:, p * LANES:(p + 1) * LANES]
            rot = v * c_ref[...] + _swap_halves(v) * s_ref[...]
            o_ref[p] = jnp.where(j < 6, rot, v)

    tab = pl.BlockSpec((tm, LANES), lambda i, j: (i, 0))
    return _pcall(
        body, name="qkv_proj", grid=(T // tm, 9),
        in_specs=[pl.BlockSpec((tm, D), lambda i, j: (i, 0)),
                  pl.BlockSpec((D, tn), lambda i, j: (0, j + (2 * CONV_CH) // tn)), tab, tab],
        out_specs=pl.BlockSpec((None, 4, tm, LANES), lambda i, j: (j, 0, i, 0)),
        out_shape=jax.ShapeDtypeStruct((9, 4, T, LANES), F32),
        compiler_params=_params("parallel", "parallel"),
    )(hn, w_in, rope_c, rope_s)


ATTN_FWD_UNROLL = 4
ATTN_BWD_UNROLL = 2


def _band_rows(start, d):
    if d == 1:
        return pl.ds(pl.multiple_of(start, BAND), BAND)
    return pl.ds(start, BAND, stride=d)


def _band_masks(n):
    row = lax.broadcasted_iota(jnp.int32, (BAND, BAND), 0)
    col = lax.broadcasted_iota(jnp.int32, (BAND, BAND), 1)
    no_prev = (n == 0).astype(jnp.int32) * (2 * BAND)
    return col <= row, col >= row + no_prev


def _attn_fwd(qkv, g):
    d = DILATIONS[g]
    nb = T // d // BAND

    def body(q_ref, k_ref, v_ref, o_ref, l_ref):
        lane_lo = lax.broadcasted_iota(jnp.int32, (BAND, LANES), 1) < HEAD_DIM

        def step(idx, carry):
            r = idx // nb
            n = idx % nb
            cur = _band_rows(n * (BAND * d) + r, d)
            prev = _band_rows(jnp.maximum(n - 1, 0) * (BAND * d) + r, d)
            q = q_ref[cur, :]
            kc = k_ref[cur, :].astype(BF16)
            vc = v_ref[cur, :].astype(BF16)
            kp = k_ref[prev, :].astype(BF16)
            vp = v_ref[prev, :].astype(BF16)
            mc, mp = _band_masks(n)
            outs, lses = [], []
            for h in range(2):
                hm = lane_lo if h == 0 else jnp.logical_not(lane_lo)
                qm = jnp.where(hm, q, 0.0).astype(BF16)
                sc = jnp.where(mc, _nt(qm, kc) * SCALE, NEG)
                sp = jnp.where(mp, _nt(qm, kp) * SCALE, NEG)
                mx = jnp.maximum(jnp.max(sc, axis=1, keepdims=True), jnp.max(sp, axis=1, keepdims=True))
                pc = jnp.exp(sc - mx)
                pp = jnp.exp(sp - mx)
                den = jnp.sum(pc, axis=1, keepdims=True) + jnp.sum(pp, axis=1, keepdims=True)
                outs.append((_nn(pc.astype(BF16), vc) + _nn(pp.astype(BF16), vp)) / den)
                lses.append(jnp.broadcast_to(mx + jnp.log(den), (BAND, LANES)))
            o_ref[cur, :] = jnp.where(lane_lo, outs[0], outs[1])
            l_ref[cur, :] = jnp.where(lane_lo, lses[0], lses[1])
            return carry

        lax.fori_loop(0, d * nb, step, 0, unroll=ATTN_FWD_UNROLL)

    def slab(which):
        return pl.BlockSpec((None, None, T, LANES), lambda p: (which * 3 + g, p, 0, 0))

    out = pl.BlockSpec((None, T, LANES), lambda p: (p, 0, 0))
    shape = jax.ShapeDtypeStruct((4, T, LANES), F32)
    return _pcall(
        body, name=f"attn_fwd{g}", grid=(4,),
        in_specs=[slab(0), slab(1), slab(2)], out_specs=[out, out], out_shape=[shape, shape],
        compiler_params=_params("parallel"),
    )(qkv, qkv, qkv)


def _attn_merge(outs, lses, cat, tm=1024):
    def body(o0, o1, o2, l0, l1, l2, cat_in, cat_ref, att_ref, w0, w1, w2):
        del cat_in
        la, lb, lc = l0[...], l1[...], l2[...]
        mx = jnp.maximum(jnp.maximum(la, lb), lc)
        ea, eb, ec = jnp.exp(la - mx), jnp.exp(lb - mx), jnp.exp(lc - mx)
        inv = 1.0 / (ea + eb + ec)
        wa, wb, wc = ea * inv, eb * inv, ec * inv
        att = wa * o0[...] + wb * o1[...] + wc * o2[...]
        att_ref[...] = att
        cat_ref[...] = att.astype(BF16)
        w0[...] = wa
        w1[...] = wb
        w2[...] = wc

    slab = pl.BlockSpec((None, tm, LANES), lambda p, i: (p, i, 0))
    shape = jax.ShapeDtypeStruct((4, T, LANES), F32)
    return _pcall(
        body, name="attn_merge", grid=(4, T // tm),
        in_specs=[slab] * 6 + [pl.BlockSpec(memory_space=pl.ANY)],
        out_specs=[pl.BlockSpec((tm, LANES), lambda p, i: (i, CONV_CH // LANES + p)), slab, slab, slab, slab],
        out_shape=[jax.ShapeDtypeStruct((T, D), BF16), shape, shape, shape, shape],
        input_output_aliases={6: 0},
        compiler_params=_params("parallel", "parallel"),
    )(*outs, *lses, cat)


def _attn_bwd(qkv, lse, wgt, att, dcat, dqkv, g):
    d = DILATIONS[g]
    nb = T // d // BAND

    def body(q_ref, k_ref, v_ref, l_ref, w_ref, a_ref, da_ref, dq_in, o_ref):
        del dq_in
        lane = lax.broadcasted_iota(jnp.int32, (BAND, LANES), 1)
        lane_lo = lane < HEAD_DIM
        row = lax.broadcasted_iota(jnp.int32, (LANES, LANES), 0)
        same_head = ((row // HEAD_DIM) == (lane // HEAD_DIM)).astype(BF16)
        dq_ref, dk_ref, dv_ref = o_ref.at[0], o_ref.at[1], o_ref.at[2]
        dk_ref[...] = jnp.zeros((T, LANES), F32)
        dv_ref[...] = jnp.zeros((T, LANES), F32)

        def step(idx, carry):
            r = idx // nb
            n = idx % nb
            cur = _band_rows(n * (BAND * d) + r, d)
            prev = _band_rows(jnp.maximum(n - 1, 0) * (BAND * d) + r, d)
            q = q_ref[cur, :]
            kc = k_ref[cur, :].astype(BF16)
            vc = v_ref[cur, :].astype(BF16)
            kp = k_ref[prev, :].astype(BF16)
            vp = v_ref[prev, :].astype(BF16)
            lse_t = l_ref[cur, :]
            w_t = w_ref[cur, :]
            da = da_ref[cur, :]
            prod = da * a_ref[cur, :]
            hi = prod.astype(BF16)
            lo = (prod - hi.astype(F32)).astype(BF16)
            csum = _nn(hi, same_head) + _nn(lo, same_head)
            mc, mp = _band_masks(n)
            dqs = []
            dkc = jnp.zeros((BAND, LANES), F32)
            dkp = jnp.zeros((BAND, LANES), F32)
            dvc = jnp.zeros((BAND, LANES), F32)
            dvp = jnp.zeros((BAND, LANES), F32)
            for h in range(2):
                hm = lane_lo if h == 0 else jnp.logical_not(lane_lo)
                col0 = h * HEAD_DIM
                lse_h = lse_t[:, col0:col0 + 1]
                w_h = w_t[:, col0:col0 + 1]
                c_h = csum[:, col0:col0 + 1]
                qm = jnp.where(hm, q, 0.0).astype(BF16)
                dam = jnp.where(hm, da, 0.0).astype(BF16)
                pwc = w_h * jnp.exp(jnp.where(mc, _nt(qm, kc) * SCALE, NEG) - lse_h)
                pwp = w_h * jnp.exp(jnp.where(mp, _nt(qm, kp) * SCALE, NEG) - lse_h)
                dsc = (pwc * (_nt(dam, vc) - c_h) * SCALE).astype(BF16)
                dsp = (pwp * (_nt(dam, vp) - c_h) * SCALE).astype(BF16)
                dqs.append(_nn(dsc, kc) + _nn(dsp, kp))
                dkc = dkc + _tn(dsc, qm)
                dkp = dkp + _tn(dsp, qm)
                dvc = dvc + _tn(pwc.astype(BF16), dam)
                dvp = dvp + _tn(pwp.astype(BF16), dam)
            dq_ref[cur, :] = jnp.where(lane_lo, dqs[0], dqs[1])
            dk_ref[cur, :] += dkc
            dk_ref[prev, :] += dkp
            dv_ref[cur, :] += dvc
            dv_ref[prev, :] += dvp
            return carry

        lax.fori_loop(0, d * nb, step, 0, unroll=ATTN_BWD_UNROLL)

    def slab(which):
        return pl.BlockSpec((None, None, T, LANES), lambda p: (which * 3 + g, p, 0, 0))

    per_pair = pl.BlockSpec((None, T, LANES), lambda p: (p, 0, 0))
    return _pcall(
        body, name=f"attn_bwd{g}", grid=(4,),
        in_specs=[slab(0), slab(1), slab(2), per_pair, per_pair, per_pair,
                  pl.BlockSpec((T, LANES), lambda p: (0, CONV_CH // LANES + p)),
                  pl.BlockSpec(memory_space=pl.ANY)],
        out_specs=pl.BlockSpec((None, 3, None, T, LANES), lambda p: (g, 0, p, 0, 0)),
        out_shape=jax.ShapeDtypeStruct((3, 3, 4, T, LANES), F32),
        input_output_aliases={7: 0},
        compiler_params=_params("parallel"),
    )(qkv, qkv, qkv, lse, wgt, att, dcat, dqkv)


def _rope_bwd(dqkv, rope_c, rope_s, dz):
    wide = 4 * LANES

    def body(d_ref, c_ref, s_ref, dz_in, o_ref):
        del dz_in
        w = pl.program_id(1)
        for p in range(4):
            v = d_ref[p]
            rot = v * c_ref[...] + _swap_halves(v * s_ref[...])
            o_ref[:, p * LANES:(p + 1) * LANES] = jnp.where(w < 2, rot, v).astype(BF16)

    tab = pl.BlockSpec((T, LANES), lambda g, w: (0, 0))
    return _pcall(
        body, name="rope_bwd", grid=(3, 3),
        in_specs=[pl.BlockSpec((None, None, 4, T, LANES), lambda g, w: (g, w, 0, 0, 0)), tab, tab,
                  pl.BlockSpec(memory_space=pl.ANY)],
        out_specs=pl.BlockSpec((T, wide), lambda g, w: (0, (2 * CONV_CH) // wide + w * 3 + g)),
        out_shape=jax.ShapeDtypeStruct((T, EVEN_IN), BF16),
        input_output_aliases={3: 0},
        compiler_params=_params("parallel", "parallel"),
    )(dqkv, rope_c, rope_s, dz)


ODD_TILE = 256
ODD_HALO = 8
GELU_C = 0.7978845608028654
GELU_A = 0.044715


def _gelu(x):
    return 0.5 * x * (1.0 + jnp.tanh(GELU_C * (x + GELU_A * x * x * x)))


def _gelu_grad(x):
    th = jnp.tanh(GELU_C * (x + GELU_A * x * x * x))
    return 0.5 * (1.0 + th) + 0.5 * x * (1.0 - th * th) * GELU_C * (1.0 + 3.0 * GELU_A * x * x)


def _tril():
    row = lax.broadcasted_iota(jnp.int32, (CHUNK, CHUNK), 0)
    col = lax.broadcasted_iota(jnp.int32, (CHUNK, CHUNK), 1)
    return (col <= row).astype(F32)


def _odd_parts(z, zh, i, k_ref, g_ref, be_ref, w_ref, bt_ref):
    R, H = ODD_TILE, ODD_HALO
    gb, gc, xs, uv = z[:, :512], z[:, 512:1024], z[:, 1024:1536], z[:, 1536:]
    halo = zh[:, 512:1024] * zh[:, 1024:1536] * (i > 0).astype(F32)
    win = jnp.concatenate([halo, gc * xs], axis=0)
    cv = jnp.zeros((R, 512), F32)
    for j in range(SCONV_W):
        off = H - (SCONV_W - 1) + j
        cv = cv + k_ref[j:j + 1, :] * win[off:off + R, :]
    ge = _gelu(uv)
    u, v = ge[:, :512], ge[:, 512:]
    mu = jnp.mean(v, axis=-1, keepdims=True)
    xc = v - mu
    rstd = lax.rsqrt(jnp.mean(xc * xc, axis=-1, keepdims=True) + EPS)
    xh = xc * rstd
    vn = xh * g_ref[...] + be_ref[...]
    tril = _tril()
    wms = [(w_ref[g] * tril).astype(BF16) for g in range(SG_GROUPS)]
    rows = []
    for ci in range(R // CHUNK):
        blocks = []
        for g in range(SG_GROUPS):
            blk = vn[ci * CHUNK:(ci + 1) * CHUNK, g * LANES:(g + 1) * LANES].astype(BF16)
            blocks.append(_nn(wms[g], blk) + bt_ref[:, g:g + 1])
        rows.append(jnp.concatenate(blocks, axis=1))
    vmix = jnp.concatenate(rows, axis=0)
    return gb, gc, xs, uv, win, cv, u, rstd, xh, vn, vmix, wms


def _odd_mid_fwd(z, conv_k, ln_g, ln_b, sg_w, sg_bt):
    R, H = ODD_TILE, ODD_HALO

    def body(z_ref, zh_ref, k_ref, g_ref, be_ref, w_ref, bt_ref, o_ref):
        i = pl.program_id(0)
        gb, _, _, _, _, cv, u, _, _, _, vmix, _ = _odd_parts(z_ref[...], zh_ref[...], i, k_ref, g_ref, be_ref, w_ref, bt_ref)
        o_ref[...] = jnp.concatenate([gb * cv, u * vmix], axis=1).astype(BF16)

    vec = pl.BlockSpec((1, 512), lambda i: (0, 0))
    return _pcall(
        body, name="odd_mid_fwd", grid=(T // R,),
        in_specs=[pl.BlockSpec((R, ODD_IN), lambda i: (i, 0)),
                  pl.BlockSpec((H, ODD_IN), lambda i: (jnp.maximum(i * (R // H) - 1, 0), 0)),
                  pl.BlockSpec((SCONV_W, 512), lambda i: (0, 0)), vec, vec,
                  pl.BlockSpec((SG_GROUPS, CHUNK, CHUNK), lambda i: (0, 0, 0)),
                  pl.BlockSpec((CHUNK, SG_GROUPS), lambda i: (0, 0))],
        out_specs=pl.BlockSpec((R, D), lambda i: (i, 0)),
        out_shape=jax.ShapeDtypeStruct((T, D), BF16),
        compiler_params=_params("parallel"),
    )(z, z, conv_k, ln_g, ln_b, sg_w, sg_bt)


def _odd_mid_bwd(z, dcat, conv_k, ln_g, ln_b, sg_w, sg_bt):
    R, H = ODD_TILE, ODD_HALO
    last = T // R - 1

    def body(z_ref, zh_ref, zn_ref, d_ref, dn_ref, k_ref, g_ref, be_ref, w_ref, bt_ref,
             dz_ref, dk_ref, dg_ref, dbe_ref, dw_ref, dbt_ref):
        i = pl.program_id(0)
        z = z_ref[...]
        gb, gc, xs, uv, win, cv, u, rstd, xh, vn, vmix, wms = _odd_parts(z, zh_ref[...], i, k_ref, g_ref, be_ref, w_ref, bt_ref)
        dcat_t = d_ref[...]
        dc, dd = dcat_t[:, :512], dcat_t[:, 512:]

        @pl.when(i == 0)
        def _():
            dk_ref[...] = jnp.zeros_like(dk_ref)
            dg_ref[...] = jnp.zeros_like(dg_ref)
            dbe_ref[...] = jnp.zeros_like(dbe_ref)
            dw_ref[...] = jnp.zeros_like(dw_ref)
            dbt_ref[...] = jnp.zeros_like(dbt_ref)

        dgb = dc * cv
        dcv = dc * gb
        nxt = dn_ref[:, :512] * zn_ref[:, :512] * (i < last).astype(F32)
        winb = jnp.concatenate([dcv, nxt], axis=0)
        dp = jnp.zeros((R, 512), F32)
        for j in range(SCONV_W):
            off = H - (SCONV_W - 1) + j
            dk_ref[j:j + 1, :] += jnp.sum(dcv * win[off:off + R, :], axis=0, keepdims=True)
            ob = SCONV_W - 1 - j
            dp = dp + k_ref[j:j + 1, :] * winb[ob:ob + R, :]
        dgc = dp * xs
        dxs = dp * gc
        du = dd * vmix
        dvmix = dd * u
        tril = _tril()
        rows = []
        for ci in range(R // CHUNK):
            blocks = []
            for g in range(SG_GROUPS):
                sl = (slice(ci * CHUNK, (ci + 1) * CHUNK), slice(g * LANES, (g + 1) * LANES))
                dblk = dvmix[sl]
                dblk16 = dblk.astype(BF16)
                blocks.append(_tn(wms[g], dblk16))
                dw_ref[g] += _nt(dblk16, vn[sl].astype(BF16)) * tril
                dbt_ref[:, g:g + 1] += jnp.sum(dblk, axis=1, keepdims=True)
            rows.append(jnp.concatenate(blocks, axis=1))
        dvn = jnp.concatenate(rows, axis=0)
        dg_ref[...] += jnp.sum(dvn * xh, axis=0, keepdims=True)
        dbe_ref[...] += jnp.sum(dvn, axis=0, keepdims=True)
        dxh = dvn * g_ref[...]
        dv = rstd * (dxh - jnp.mean(dxh, axis=-1, keepdims=True) - xh * jnp.mean(dxh * xh, axis=-1, keepdims=True))
        duv = jnp.concatenate([du, dv], axis=1) * _gelu_grad(uv)
        dz_ref[...] = jnp.concatenate([dgb, dgc, dxs, duv], axis=1).astype(BF16)

    vec = pl.BlockSpec((1, 512), lambda i: (0, 0))
    kspec = pl.BlockSpec((SCONV_W, 512), lambda i: (0, 0))
    wspec = pl.BlockSpec((SG_GROUPS, CHUNK, CHUNK), lambda i: (0, 0, 0))
    bspec = pl.BlockSpec((CHUNK, SG_GROUPS), lambda i: (0, 0))
    nxt_blk = lambda i: (jnp.minimum((i + 1) * (R // H), T // H - 1), 0)
    return _pcall(
        body, name="odd_mid_bwd", grid=(T // R,),
        in_specs=[pl.BlockSpec((R, ODD_IN), lambda i: (i, 0)),
                  pl.BlockSpec((H, ODD_IN), lambda i: (jnp.maximum(i * (R // H) - 1, 0), 0)),
                  pl.BlockSpec((H, ODD_IN), nxt_blk),
                  pl.BlockSpec((R, D), lambda i: (i, 0)),
                  pl.BlockSpec((H, D), nxt_blk),
                  kspec, vec, vec, wspec, bspec],
        out_specs=[pl.BlockSpec((R, ODD_IN), lambda i: (i, 0)), kspec, vec, vec, wspec, bspec],
        out_shape=[jax.ShapeDtypeStruct((T, ODD_IN), BF16), jax.ShapeDtypeStruct((SCONV_W, 512), F32),
                   jax.ShapeDtypeStruct((1, 512), F32), jax.ShapeDtypeStruct((1, 512), F32),
                   jax.ShapeDtypeStruct((SG_GROUPS, CHUNK, CHUNK), F32), jax.ShapeDtypeStruct((CHUNK, SG_GROUPS), F32)],
        compiler_params=_params("arbitrary"),
    )(z, z, z, dcat, dcat, conv_k, ln_g, ln_b, sg_w, sg_bt)


def _ffn_fwd(tag, h, g, weight):
    hn = _rms_fwd(f"ffn{tag}_norm", h, g)

    def act(acc):
        r = jnp.maximum(acc, 0.0)
        return (r * r,)

    f = _mm(f"ffn{tag}_up", "nn", hn, weight(f"ffn_w1_{tag}", hn), T, D_FF, D, (BF16,), epi=act)
    out = _mm(f"ffn{tag}_down", "nn", f, weight(f"ffn_w2_{tag}", f), T, D, D_FF, (F32,),
              epi=lambda acc, res: (acc + res,), extras=(h,))
    return out, (hn, f)


def _ffn_bwd(tag, h, g, weight, emit, saved, dout):
    hn, f = saved
    du = _mm(f"ffn{tag}_dact", "nt", dout, weight(f"ffn_w2_{tag}", dout), T, D_FF, D, (BF16,),
             epi=lambda acc, ff: (acc * (2.0 * jnp.sqrt(ff.astype(F32))),), extras=(f,))
    tok = emit(f"ffn_w2_{tag}", _mm(f"ffn{tag}_dw2", "tn", f, dout, D_FF, D, T, (BF16,)))
    tok = emit(f"ffn_w1_{tag}", _mm(f"ffn{tag}_dw1", "tn", hn, du, D, D_FF, T, (BF16,), tie=tok))
    dhn = _mm(f"ffn{tag}_dhn", "nt", du, weight(f"ffn_w1_{tag}", du), T, D, D_FF, (F32,), tie=tok)
    return _rms_bwd(f"ffn{tag}_dnorm", h, dhn, g, dout)


def _rope_tables():
    half = HEAD_DIM // 2
    inv = 10000.0 ** (-jnp.arange(half, dtype=F32) / half)
    ang = jnp.arange(T, dtype=F32)[:, None] * inv[None, :]
    cos, sin = jnp.cos(ang), jnp.sin(ang)
    c = jnp.tile(jnp.concatenate([cos, cos], axis=1), (1, LANES // HEAD_DIM))
    s = jnp.tile(jnp.concatenate([-sin, sin], axis=1), (1, LANES // HEAD_DIM))
    return c, s


def _local_step(x, target, p, weight, emit):
    rope_c, rope_s = _rope_tables()
    grads = {}
    residual = lambda acc, res: (acc + res,)

    hn0 = _rms_fwd("mix0_norm", x, p["norm_mix_g0"])
    zc = _mm("even_in_conv", "nn", hn0, weight("even_w_in", hn0), T, 2 * CONV_CH, D, (F32,))
    qkv = _qkv_proj(hn0, weight("even_w_in", hn0), rope_c, rope_s)
    cv, cat0 = _econv_fwd(zc, p["even_conv_k"], p["even_conv_b"], p["even_ln_g"], p["even_ln_b"])
    att_parts = [_attn_fwd(qkv, g) for g in range(3)]
    outs = [a[0] for a in att_parts]
    lses = [a[1] for a in att_parts]
    cat0, att, w0, w1, w2 = _attn_merge(outs, lses, cat0)
    wgts = (w0, w1, w2)
    h1 = _mm("even_out", "nn", cat0, weight("even_w_out", cat0), T, D, D, (F32,), epi=residual, extras=(x,))
    h2, ffn0_saved = _ffn_fwd(0, h1, p["norm_ffn_g0"], weight)

    hn1 = _rms_fwd("mix1_norm", h2, p["norm_mix_g1"])
    z1 = _mm("odd_in", "nn", hn1, weight("odd_w_in", hn1), T, ODD_IN, D, (F32,))
    cat1 = _odd_mid_fwd(z1, p["odd_conv_k"], p["odd_ln_g"], p["odd_ln_b"], p["odd_sg_w"], p["odd_sg_bt"])
    h3 = _mm("odd_out", "nn", cat1, weight("odd_w_out", cat1), T, D, D, (F32,), epi=residual, extras=(h2,))
    h4, ffn1_saved = _ffn_fwd(1, h3, p["norm_ffn_g1"], weight)

    dh4, grads["final_g"], loss = _loss_head(h4, p["final_g"], target)

    dh3, grads["norm_ffn_g1"] = _ffn_bwd(1, h3, p["norm_ffn_g1"], weight, emit, ffn1_saved, dh4)
    tok = emit("odd_w_out", _mm("odd_out_dw", "tn", cat1, dh3, D, D, T, (BF16,)))
    dcat1 = _mm("odd_out_dx", "nt", dh3, weight("odd_w_out", dh3), T, D, D, (F32,), tie=tok)
    dz1, grads["odd_conv_k"], grads["odd_ln_g"], grads["odd_ln_b"], grads["odd_sg_w"], grads["odd_sg_bt"] = _odd_mid_bwd(
        z1, dcat1, p["odd_conv_k"], p["odd_ln_g"], p["odd_ln_b"], p["odd_sg_w"], p["odd_sg_bt"])
    tok = emit("odd_w_in", _mm("odd_in_dw", "tn", hn1, dz1, D, ODD_IN, T, (BF16,)))
    dhn1 = _mm("odd_in_dx", "nt", dz1, weight("odd_w_in", dz1), T, D, ODD_IN, (F32,), tie=tok)
    dh2, grads["norm_mix_g1"] = _rms_bwd("mix1_dnorm", h2, dhn1, p["norm_mix_g1"], dh3)

    dh1, grads["norm_ffn_g0"] = _ffn_bwd(0, h1, p["norm_ffn_g0"], weight, emit, ffn0_saved, dh2)
    tok = emit("even_w_out", _mm("even_out_dw", "tn", cat0, dh1, D, D, T, (BF16,)))
    dcat0 = _mm("even_out_dx", "nt", dh1, weight("even_w_out", dh1), T, D, D, (F32,), tie=tok)
    dcv, grads["even_ln_g"], grads["even_ln_b"], grads["even_conv_b"] = _econv_bwd_ln(
        cv, dcat0, p["even_ln_g"], p["even_ln_b"])
    dz0, grads["even_conv_k"] = _econv_bwd_conv(dcv, zc, p["even_conv_k"])
    dqkv = lax.empty((3, 3, 4, T, LANES), F32)
    for g in range(3):
        dqkv = _attn_bwd(qkv, lses[g], wgts[g], att, dcat0, dqkv, g)
    dz0 = _rope_bwd(dqkv, rope_c, rope_s, dz0)
    tok = emit("even_w_in", _mm("even_in_dw", "tn", hn0, dz0, D, EVEN_IN, T, (BF16,)))
    dhn0 = _mm("even_in_dx", "nt", dz0, weight("even_w_in", dz0), T, D, EVEN_IN, (F32,), tie=tok)
    dx, grads["norm_mix_g0"] = _rms_bwd("mix0_dnorm", x, dhn0, p["norm_mix_g0"], dh1)
    return loss, dx, grads


def _rowwise(name, fn, ins, out_dtypes, tm=256):
    rows, cols = ins[0].shape
    tm = tm if rows % tm == 0 else rows
    n_in = len(ins)

    def body(*refs):
        vals = fn(*[r[...] for r in refs[:n_in]])
        for o_ref, v in zip(refs[n_in:], vals):
            o_ref[...] = v.astype(o_ref.dtype)

    spec = pl.BlockSpec((tm, cols), lambda i: (i, 0))
    outs = _pcall(
        body, name=name, grid=(rows // tm,),
        in_specs=[spec] * n_in, out_specs=[spec] * len(out_dtypes),
        out_shape=[jax.ShapeDtypeStruct((rows, cols), dt) for dt in out_dtypes],
        compiler_params=_params("parallel"),
    )(*ins)
    return outs[0] if len(out_dtypes) == 1 else outs


def _adamw(name, w, g, m, v):
    c1 = 1.0 - ADAM_B1 ** ADAM_STEP
    c2 = 1.0 - ADAM_B2 ** ADAM_STEP

    def fn(w_t, g_t, m_t, v_t):
        m_new = ADAM_B1 * m_t + (1.0 - ADAM_B1) * g_t
        v_new = ADAM_B2 * v_t + (1.0 - ADAM_B2) * (g_t * g_t)
        delta = -ADAM_LR * ((m_new / c1) / (jnp.sqrt(v_new / c2) + ADAM_EPS) + ADAM_WD * w_t)
        return delta, m_new, v_new

    return _rowwise(name, fn, (w, g, m, v), (F32, F32, F32))


class _Piece:
    def __init__(self, name, rows, cols, axis, src, src_row0):
        self.name, self.rows, self.cols, self.axis = name, rows, cols, axis
        self.width = (cols if axis == 1 else rows) // 4
        self.src, self.src_row0 = src, src_row0

    @property
    def full_shape(self):
        return (self.rows, self.cols)

    @property
    def half_shape(self):
        return (self.rows // 2, self.cols) if self.axis == 1 else (self.rows, self.cols // 2)

    @property
    def shard_half_shape(self):
        return (self.rows // 2, self.width) if self.axis == 1 else (self.width, self.cols // 2)

    def shard_whole(self, ref):
        n = self.rows if self.axis == 1 else self.width
        return ref.at[pl.ds(self.src_row0, n), :]

    def shard_half(self, ref, h):
        if self.axis == 1:
            return ref.at[pl.ds(self.src_row0 + h * (self.rows // 2), self.rows // 2), :]
        return ref.at[pl.ds(self.src_row0, self.width), pl.ds(h * (self.cols // 2), self.cols // 2)]

    def full_shard(self, ref, s):
        if self.axis == 1:
            return ref.at[:, pl.ds(s * self.width, self.width)]
        return ref.at[pl.ds(s * self.width, self.width), :]

    def full_shard_half(self, ref, s, h):
        if self.axis == 1:
            return ref.at[pl.ds(h * (self.rows // 2), self.rows // 2), pl.ds(s * self.width, self.width)]
        return ref.at[pl.ds(s * self.width, self.width), pl.ds(h * (self.cols // 2), self.cols // 2)]

    def full_half(self, ref, h):
        if self.axis == 1:
            return ref.at[pl.ds(h * (self.rows // 2), self.rows // 2), :]
        return ref.at[:, pl.ds(h * (self.cols // 2), self.cols // 2)]

    def full_half_rows(self, ref, h, r0, n):
        if self.axis == 1:
            return ref.at[pl.ds(h * (self.rows // 2) + r0, n), :]
        return ref.at[pl.ds(r0, n), pl.ds(h * (self.cols // 2), self.cols // 2)]

    def half_shard(self, ref, s):
        return self.full_shard(ref, s)


PIECES = (
    _Piece("even_w_in", D, EVEN_IN, 1, 0, 0),
    _Piece("even_w_out", D, D, 0, 1, 0),
    _Piece("ffn_w1_0", D, D_FF, 1, 4, 0),
    _Piece("ffn_w2_0", D_FF, D, 0, 5, 0),
    _Piece("odd_w_in", D, ODD_IN, 1, 2, 0),
    _Piece("odd_w_out", D, D, 0, 3, 0),
    _Piece("ffn_w1_1", D, D_FF, 1, 4, D),
    _Piece("ffn_w2_1", D_FF, D, 0, 5, D_FF // 4),
)
N_PIECES = len(PIECES)
N_SHARD_OPERANDS = 6
ANY = pl.BlockSpec(memory_space=pl.ANY)
MESH = pl.DeviceIdType.MESH


def _mesh_place():
    x, y, c = lax.axis_index("x"), lax.axis_index("y"), lax.axis_index("c")
    chips = [(1 - x, y), (x, 1 - y), (1 - x, 1 - y)]
    return x, y, c, chips


def _remote(src, dst, send_sem, recv_sem, dev):
    return pltpu.make_async_remote_copy(src_ref=src, dst_ref=dst, send_sem=send_sem, recv_sem=recv_sem,
                                        device_id=dev, device_id_type=MESH)


HBM = pl.BlockSpec(memory_space=pltpu.HBM)
SEM = pl.BlockSpec(memory_space=pltpu.SEMAPHORE)
SPLIT_PARAMS = pltpu.CompilerParams(has_side_effects=pltpu.SideEffectType.DATAFLOW_SIDE_EFFECTING)
CAST_TILE = 256


def _in_hbm(a):
    return pltpu.with_memory_space_constraint(a, pltpu.HBM)


def _cast_place(pc, shard_operand, chip, tie=None):
    rows, cols = (pc.rows, pc.width) if pc.axis == 1 else (pc.width, pc.cols)
    nblk = rows // CAST_TILE
    blk0 = pc.src_row0 // CAST_TILE
    ties = () if tie is None else (tie,)

    def body(chip_ref, x_ref, *rest):
        del chip_ref
        rest[-1][...] = x_ref[...].astype(BF16)

    if pc.axis == 1:
        out_map = lambda i, chip_ref: (i, chip_ref[0])
    else:
        out_map = lambda i, chip_ref: (chip_ref[0] * nblk + i, 0)
    return _pcall(
        body, name=f"cast_{pc.name}",
        grid_spec=pltpu.PrefetchScalarGridSpec(
            num_scalar_prefetch=1, grid=(nblk,),
            in_specs=[pl.BlockSpec((CAST_TILE, cols), lambda i, chip_ref: (blk0 + i, 0))]
            + [pl.BlockSpec(TOKEN_SHAPE, lambda i, chip_ref: (0, 0))] * len(ties),
            out_specs=pl.BlockSpec((CAST_TILE, cols), out_map)),
        out_shape=jax.ShapeDtypeStruct(pc.full_shape, BF16),
        compiler_params=_params("parallel"),
    )(chip, shard_operand, *ties)


def _gather_start(name, pieces, fulls):
    n = len(pieces)

    def body(*refs):
        ins = refs[:n]
        sends = refs[2 * n:3 * n]
        recvs = refs[3 * n:4 * n]
        token = refs[4 * n]
        x, y, c, chips = _mesh_place()
        s = 2 * x + y
        for i, pc in enumerate(pieces):
            win = pc.full_shard_half(ins[i], s, c)
            for k, (cx, cy) in enumerate(chips):
                _remote(win, win, sends[i].at[k], recvs[i].at[k], (cx, cy, c)).start()
        token[...] = jnp.zeros(TOKEN_SHAPE, F32)

    sems = [pltpu.SemaphoreType.DMA((3,))] * (2 * n)
    outs = _pcall(
        body, name=name,
        in_specs=[HBM] * n,
        out_specs=[HBM] * n + [SEM] * (2 * n) + [pl.BlockSpec(memory_space=pltpu.VMEM)],
        out_shape=[pltpu.HBM(pc.full_shape, BF16) for pc in pieces] + sems + [jax.ShapeDtypeStruct(TOKEN_SHAPE, F32)],
        input_output_aliases={i: i for i in range(n)},
        compiler_params=SPLIT_PARAMS,
    )(*[_in_hbm(f) for f in fulls])
    return outs[:n], outs[n:2 * n], outs[2 * n:3 * n], outs[3 * n]


def _gather_wait(pc, full, send_sems, recv_sems, after):
    def body(full_ref, send_ref, recv_ref, after_ref, out_ref):
        del after_ref, out_ref
        x, y, c, chips = _mesh_place()
        for k, (cx, cy) in enumerate(chips):
            win = pc.full_shard_half(full_ref, 2 * cx + cy, c)
            cp = _remote(win, win, send_ref.at[k], recv_ref.at[k], (cx, cy, c))
            cp.wait_send()
            cp.wait_recv()

    return _pcall(
        body, name=f"gather_wait_{pc.name}",
        in_specs=[HBM, SEM, SEM, ANY], out_specs=HBM, out_shape=pltpu.HBM(pc.full_shape, BF16),
        input_output_aliases={0: 0}, compiler_params=SPLIT_PARAMS,
    )(full, send_sems, recv_sems, after)


def _core_forward(pc, full):
    sh = pc.shard_half_shape

    def body(full_in, full_ref, send_buf, recv_buf, load_sems, send_sems, recv_sems, store_sems):
        x, y, c, chips = _mesh_place()
        loads, sends, stores = [], [], []
        for k, (cx, cy) in enumerate(chips):
            cp = pltpu.make_async_copy(pc.full_shard_half(full_in, 2 * cx + cy, c), send_buf.at[k], load_sems.at[k])
            cp.start()
            loads.append(cp)
        for k in range(3):
            loads[k].wait()
            cp = _remote(send_buf.at[k], recv_buf.at[k], send_sems.at[k], recv_sems.at[k], (x, y, 1 - c))
            cp.start()
            sends.append(cp)
        for k, (cx, cy) in enumerate(chips):
            sends[k].wait_recv()
            cp = pltpu.make_async_copy(recv_buf.at[k], pc.full_shard_half(full_ref, 2 * cx + cy, 1 - c), store_sems.at[k])
            cp.start()
            stores.append(cp)
        for k in range(3):
            sends[k].wait_send()
            stores[k].wait()

    sems = pltpu.SemaphoreType.DMA((3,))
    return _pcall(
        body, name=f"core_forward_{pc.name}", in_specs=[ANY], out_specs=ANY,
        out_shape=jax.ShapeDtypeStruct(pc.full_shape, BF16),
        scratch_shapes=[pltpu.VMEM((3,) + sh, BF16), pltpu.VMEM((3,) + sh, BF16), sems, sems, sems, sems],
        input_output_aliases={0: 0},
        compiler_params=pltpu.CompilerParams(vmem_limit_bytes=VMEM_LIMIT),
    )(full)


CHIPSUM_CHUNKS = 4


def _chipsum(pc, partial):
    hr, hc = pc.half_shape
    ch = hr // CHIPSUM_CHUNKS

    def body(g_ref, out_ref, send_buf, recv_buf, own_buf, sum_buf, load_sems, own_sems, send_sems, recv_sems, out_sems):
        x, y, c, _ = _mesh_place()
        chunks = [pl.ds(k * ch, ch) for k in range(CHIPSUM_CHUNKS)]
        loads, owns, sends, stores = [], [], [], []
        for k, rows in enumerate(chunks):
            cp = pltpu.make_async_copy(pc.full_half_rows(g_ref, 1 - c, k * ch, ch), send_buf.at[rows, :], load_sems.at[k])
            cp.start()
            loads.append(cp)
            cp = pltpu.make_async_copy(pc.full_half_rows(g_ref, c, k * ch, ch), own_buf.at[rows, :], own_sems.at[k])
            cp.start()
            owns.append(cp)
        for k, rows in enumerate(chunks):
            loads[k].wait()
            cp = _remote(send_buf.at[rows, :], recv_buf.at[rows, :], send_sems.at[k], recv_sems.at[k], (x, y, 1 - c))
            cp.start()
            sends.append(cp)
        for k, rows in enumerate(chunks):
            owns[k].wait()
            sends[k].wait_recv()
            sum_buf[rows, :] = (own_buf[rows, :].astype(F32) + recv_buf[rows, :].astype(F32)).astype(BF16)
            cp = pltpu.make_async_copy(sum_buf.at[rows, :], out_ref.at[rows, :], out_sems.at[k])
            cp.start()
            stores.append(cp)
        for k in range(CHIPSUM_CHUNKS):
            sends[k].wait_send()
            stores[k].wait()

    buf = pltpu.VMEM((hr, hc), BF16)
    sems = pltpu.SemaphoreType.DMA((CHIPSUM_CHUNKS,))
    return _pcall(
        body, name=f"chipsum_{pc.name}", in_specs=[ANY], out_specs=ANY,
        out_shape=jax.ShapeDtypeStruct((hr, hc), BF16),
        scratch_shapes=[buf, buf, buf, buf, sems, sems, sems, sems, sems],
        compiler_params=pltpu.CompilerParams(vmem_limit_bytes=VMEM_LIMIT),
    )(partial)


def _scatter_start(pc, chip_sum):
    def body(sum_ref, land_ref, sum_out, land_out, sends, recvs, token):
        del sum_out, land_out
        x, y, c, chips = _mesh_place()
        for k, (cx, cy) in enumerate(chips):
            _remote(pc.half_shard(sum_ref, 2 * cx + cy), land_ref.at[k], sends.at[k], recvs.at[k], (cx, cy, c)).start()
        token[...] = jnp.zeros(TOKEN_SHAPE, F32)

    land_shape = (3,) + pc.shard_half_shape
    sems = pltpu.SemaphoreType.DMA((3,))
    return _pcall(
        body, name=f"scatter_start_{pc.name}",
        in_specs=[HBM, HBM], out_specs=[HBM, HBM, SEM, SEM, pl.BlockSpec(memory_space=pltpu.VMEM)],
        out_shape=[pltpu.HBM(pc.half_shape, BF16), pltpu.HBM(land_shape, BF16), sems, sems,
                   jax.ShapeDtypeStruct(TOKEN_SHAPE, F32)],
        input_output_aliases={0: 0, 1: 1}, compiler_params=SPLIT_PARAMS,
    )(_in_hbm(chip_sum), _in_hbm(lax.empty(land_shape, BF16)))


def _scatter_wait(pc, chip_sum, land, send_sems, recv_sems, after):
    def body(sum_ref, land_ref, send_ref, recv_ref, after_ref, sum_out, land_out):
        del after_ref, sum_out, land_out
        x, y, c, chips = _mesh_place()
        for k, (cx, cy) in enumerate(chips):
            cp = _remote(pc.half_shard(sum_ref, 2 * cx + cy), land_ref.at[k], send_ref.at[k], recv_ref.at[k], (cx, cy, c))
            cp.wait_send()
            cp.wait_recv()

    return _pcall(
        body, name=f"scatter_wait_{pc.name}",
        in_specs=[HBM, HBM, SEM, SEM, ANY], out_specs=[HBM, HBM],
        out_shape=[pltpu.HBM(pc.half_shape, BF16), pltpu.HBM((3,) + pc.shard_half_shape, BF16)],
        input_output_aliases={0: 0, 1: 1}, compiler_params=SPLIT_PARAMS,
    )(chip_sum, land, send_sems, recv_sems, after)


SHARD_OPERAND_SHAPES = ((D, EVEN_IN // 4), (D // 4, D), (D, ODD_IN // 4), (D // 4, D), (2 * D, D_FF // 4), (2 * D_FF // 4, D))


def _allsum_join(operand, chip_sums, lands):
    pieces = [pc for pc in PIECES if pc.src == operand]
    n = len(pieces)

    def body(*refs):
        sum_refs = refs[:n]
        land_refs = refs[n:2 * n]
        refs = refs[n:]
        out_ref = refs[n]
        in_bufs = refs[n + 1:2 * n + 1]
        fin_bufs = refs[2 * n + 1:3 * n + 1]
        recv_bufs = refs[3 * n + 1:4 * n + 1]
        load_sems, send_sems, recv_sems, out_sems = refs[4 * n + 1:]
        x, y, c, _ = _mesh_place()
        s = 2 * x + y
        loads, sends, stores = [], [], []
        for j, pc in enumerate(pieces):
            cp = pltpu.make_async_copy(land_refs[j], in_bufs[j].at[pl.ds(0, 3)], load_sems.at[2 * j])
            cp.start()
            loads.append(cp)
            cp = pltpu.make_async_copy(pc.half_shard(sum_refs[j], s), in_bufs[j].at[3], load_sems.at[2 * j + 1])
            cp.start()
            loads.append(cp)
        for j, pc in enumerate(pieces):
            loads[2 * j].wait()
            loads[2 * j + 1].wait()
            acc = in_bufs[j][0].astype(F32)
            for k in range(1, 4):
                acc = acc + in_bufs[j][k].astype(F32)
            fin_bufs[j][...] = acc
            cp = pltpu.make_async_copy(fin_bufs[j], pc.shard_half(out_ref, c), out_sems.at[2 * j])
            cp.start()
            stores.append(cp)
            cp = _remote(fin_bufs[j], recv_bufs[j], send_sems.at[j], recv_sems.at[j], (x, y, 1 - c))
            cp.start()
            sends.append(cp)
        for j, pc in enumerate(pieces):
            sends[j].wait_recv()
            cp = pltpu.make_async_copy(recv_bufs[j], pc.shard_half(out_ref, 1 - c), out_sems.at[2 * j + 1])
            cp.start()
            stores.append(cp)
        for cp in sends:
            cp.wait_send()
        for cp in stores:
            cp.wait()

    sh = pieces[0].shard_half_shape
    return _pcall(
        body, name=f"allsum_join_{operand}", in_specs=[ANY] * (2 * n), out_specs=ANY,
        out_shape=jax.ShapeDtypeStruct(SHARD_OPERAND_SHAPES[operand], F32),
        scratch_shapes=[pltpu.VMEM((4,) + sh, BF16)] * n + [pltpu.VMEM(sh, F32)] * (2 * n)
        + [pltpu.SemaphoreType.DMA((2 * n,)), pltpu.SemaphoreType.DMA((n,)), pltpu.SemaphoreType.DMA((n,)),
           pltpu.SemaphoreType.DMA((2 * n,))],
        compiler_params=pltpu.CompilerParams(vmem_limit_bytes=VMEM_LIMIT),
    )(*chip_sums, *lands)


def _allgather8(name, blk, with_sum):
    m = blk.shape[0]

    def body(x_ref, out_ref, *rest):
        if with_sum:
            sum_ref, send_sems, recv_sems, local_sem = rest
        else:
            send_sems, recv_sems, local_sem = rest
        x, y, c, chips = _mesh_place()
        me, sibling = (x, y, c), (x, y, 1 - c)

        def rows(px, py, pc):
            return out_ref.at[pl.ds((4 * px + 2 * py + pc) * m, m), :]

        def copy(k, block, to, src=None):
            return _remote(rows(*block) if src is None else src, rows(*block), send_sems.at[k], recv_sems.at[k], to)

        mine = pltpu.make_async_copy(x_ref, rows(*me), local_sem)
        mine.start()
        first = [copy(0, me, sibling, src=x_ref)]
        first += [copy(1 + j, me, (*chip, c), src=x_ref) for j, chip in enumerate(chips)]
        for cp in first:
            cp.start()
        passed = [copy(4 + j, (*chip, c), sibling) for j, chip in enumerate(chips)]
        for j, chip in enumerate(chips):
            copy(1 + j, (*chip, c), me).wait_recv()
            passed[j].start()
        copy(0, sibling, me).wait_recv()
        for j, chip in enumerate(chips):
            copy(4 + j, (*chip, 1 - c), me).wait_recv()
        for cp in first + passed:
            cp.wait_send()
        mine.wait()
        if with_sum:
            acc = out_ref[0:m, :]
            for dev in range(1, 8):
                acc = acc + out_ref[dev * m:(dev + 1) * m, :]
            sum_ref[...] = acc

    vm = pl.BlockSpec(memory_space=pltpu.VMEM)
    out_shape = [jax.ShapeDtypeStruct((8 * m, LANES), F32)]
    if with_sum:
        out_shape.append(jax.ShapeDtypeStruct((m, LANES), F32))
    return _pcall(
        body, name=name, in_specs=[vm], out_specs=[vm] * len(out_shape), out_shape=out_shape,
        scratch_shapes=[pltpu.SemaphoreType.DMA((7,)), pltpu.SemaphoreType.DMA((7,)), pltpu.SemaphoreType.DMA],
    )(blk)


def _pack(arrays, row_counts):
    rows = []
    for a, n in zip(arrays, row_counts):
        flat = a.reshape(-1, LANES)
        rows.append(jnp.pad(flat, ((0, n - flat.shape[0]), (0, 0))))
    return jnp.concatenate(rows, axis=0)


def _unpack(buf, shapes, row_counts):
    out, r0 = [], 0
    for sh, n in zip(shapes, row_counts):
        size = 1
        for dim in sh:
            size *= dim
        out.append(buf[r0:r0 + size // LANES].reshape(sh))
        r0 += n
    return out


REPL_NAMES = ("norm_mix_g", "norm_ffn_g", "even_conv_b", "even_ln_g", "even_ln_b", "odd_sg_w", "odd_sg_b", "final_g")
REPL_SHAPES = ((2, D), (2, D), (1, 512), (1, 512), (1, 512), (1, SG_GROUPS, CHUNK, CHUNK), (1, SG_GROUPS, CHUNK), (D,))
REPL_ROWS = (16, 16, 8, 8, 8, 512, 8, 8)
SHARDED_NAMES = ("even_conv_k", "odd_conv_k", "odd_ln_g", "odd_ln_b")
SHARDED_SHARD_SHAPES = ((1, CONV_W, LANES), (1, SCONV_W, LANES), (1, LANES), (1, LANES))
SHARDED_SHARD_ROWS = (32, 8, 8, 8)
SHARDED_FULL_SHAPES = ((CONV_W, 512), (SCONV_W, 512), (1, 512), (1, 512))
SHARDED_FULL_ROWS = (128, 16, 8, 8)


def kernel(x, norm_mix_g, norm_ffn_g, even_w_in, even_conv_k, even_conv_b, even_ln_g, even_ln_b, even_w_out, odd_w_in, odd_conv_k, odd_ln_g, odd_ln_b, odd_sg_w, odd_sg_b, odd_w_out, ffn_w1, ffn_w2, final_g, loss_target, m_norm_mix_g, m_norm_ffn_g, m_even_w_in, m_even_conv_k, m_even_conv_b, m_even_ln_g, m_even_ln_b, m_even_w_out, m_odd_w_in, m_odd_conv_k, m_odd_ln_g, m_odd_ln_b, m_odd_sg_w, m_odd_sg_b, m_odd_w_out, m_ffn_w1, m_ffn_w2, m_final_g, v_norm_mix_g, v_norm_ffn_g, v_even_w_in, v_even_conv_k, v_even_conv_b, v_even_ln_g, v_even_ln_b, v_even_w_out, v_odd_w_in, v_odd_conv_k, v_odd_ln_g, v_odd_ln_b, v_odd_sg_w, v_odd_sg_b, v_odd_w_out, v_ffn_w1, v_ffn_w2, v_final_g):
    names = ("norm_mix_g", "norm_ffn_g", "even_w_in", "even_conv_k", "even_conv_b", "even_ln_g", "even_ln_b", "even_w_out",
             "odd_w_in", "odd_conv_k", "odd_ln_g", "odd_ln_b", "odd_sg_w", "odd_sg_b", "odd_w_out", "ffn_w1", "ffn_w2", "final_g")
    w = dict(zip(names, (norm_mix_g, norm_ffn_g, even_w_in, even_conv_k, even_conv_b, even_ln_g, even_ln_b, even_w_out,
                         odd_w_in, odd_conv_k, odd_ln_g, odd_ln_b, odd_sg_w, odd_sg_b, odd_w_out, ffn_w1, ffn_w2, final_g)))
    mom = dict(zip(names, (m_norm_mix_g, m_norm_ffn_g, m_even_w_in, m_even_conv_k, m_even_conv_b, m_even_ln_g, m_even_ln_b,
                           m_even_w_out, m_odd_w_in, m_odd_conv_k, m_odd_ln_g, m_odd_ln_b, m_odd_sg_w, m_odd_sg_b, m_odd_w_out,
                           m_ffn_w1, m_ffn_w2, m_final_g)))
    vel = dict(zip(names, (v_norm_mix_g, v_norm_ffn_g, v_even_w_in, v_even_conv_k, v_even_conv_b, v_even_ln_g, v_even_ln_b,
                           v_even_w_out, v_odd_w_in, v_odd_conv_k, v_odd_ln_g, v_odd_ln_b, v_odd_sg_w, v_odd_sg_b, v_odd_w_out,
                           v_ffn_w1, v_ffn_w2, v_final_g)))
    big_names = ("even_w_in", "even_w_out", "odd_w_in", "odd_w_out", "ffn_w1", "ffn_w2")
    chip = 2 * lax.axis_index("x") + lax.axis_index("y")

    def shard2d(t, name):
        return t[name].reshape(SHARD_OPERAND_SHAPES[big_names.index(name)])

    chip_op = jnp.reshape(chip, (1,)).astype(jnp.int32)
    first = _cast_place(PIECES[0], shard2d(w, big_names[PIECES[0].src]), chip_op)
    fly0, send0, recv0, token = _gather_start("gather_start_first", PIECES[:1], [first])
    placed = [_cast_place(pc, shard2d(w, big_names[pc.src]), chip_op, tie=token) for pc in PIECES[1:]]
    fly1, send1, recv1, all_started = _gather_start("gather_start_rest", PIECES[1:], placed)
    flying, gather_send, gather_recv = fly0 + fly1, send0 + send1, recv0 + recv1
    ready = {}

    def weight(name, after):
        if name not in ready:
            i = [pc.name for pc in PIECES].index(name)
            if i == 0:
                after = all_started
            landed = _gather_wait(PIECES[i], flying[i], gather_send[i], gather_recv[i], after)
            ready[name] = _core_forward(PIECES[i], landed)
        return ready[name]

    scattering = []

    def emit(name, partial):
        pc = PIECES[[q.name for q in PIECES].index(name)]
        chip_sum, land, send_sems, recv_sems, token = _scatter_start(pc, _chipsum(pc, partial))
        scattering.append((pc, chip_sum, land, send_sems, recv_sems))
        return token

    full = {}
    small_pack = _pack([w[n] for n in SHARDED_NAMES], SHARDED_SHARD_ROWS)
    gathered = _allgather8("gather_small", small_pack, False)[0].reshape(4, 2, sum(SHARDED_SHARD_ROWS), LANES)[:, 0]
    r0 = 0
    for n, sh, rows, full_sh in zip(SHARDED_NAMES, SHARDED_SHARD_SHAPES, SHARDED_SHARD_ROWS, SHARDED_FULL_SHAPES):
        per_chip = gathered[:, r0:r0 + rows].reshape(4, -1)[:, :full_sh[0] * LANES].reshape(4, full_sh[0], LANES)
        full[n] = jnp.transpose(per_chip, (1, 0, 2)).reshape(full_sh)
        r0 += rows
    p = dict(full)
    p.update(norm_mix_g0=norm_mix_g[0:1], norm_mix_g1=norm_mix_g[1:2], norm_ffn_g0=norm_ffn_g[0:1], norm_ffn_g1=norm_ffn_g[1:2],
             even_conv_b=even_conv_b, even_ln_g=even_ln_g, even_ln_b=even_ln_b,
             odd_sg_w=odd_sg_w[0], odd_sg_bt=odd_sg_b[0].T, final_g=final_g[None, :])

    loss_row, dx, g = _local_step(x[0], loss_target[0], p, weight, emit)

    landed = {pc.name: _scatter_wait(pc, chip_sum, land, send_sems, recv_sems, dx)
              for pc, chip_sum, land, send_sems, recv_sems in scattering}
    big_grads = {n: _allsum_join(o, [landed[pc.name][0] for pc in PIECES if pc.src == o],
                                 [landed[pc.name][1] for pc in PIECES if pc.src == o])
                 for o, n in enumerate(big_names)}

    grad_parts = [loss_row, g["norm_mix_g0"], g["norm_mix_g1"], g["norm_ffn_g0"], g["norm_ffn_g1"], g["even_conv_b"],
                  g["even_ln_g"], g["even_ln_b"], g["odd_sg_w"], g["odd_sg_bt"].T, g["final_g"],
                  g["even_conv_k"], g["odd_conv_k"], g["odd_ln_g"], g["odd_ln_b"]]
    grad_pack = _pack(grad_parts, (8, 8, 8, 8, 8) + REPL_ROWS[2:] + SHARDED_FULL_ROWS)
    grad_sum = _allgather8("allreduce_small", grad_pack, True)[1]
    loss = grad_sum[0, 0]
    parts = _unpack(grad_sum[8:], REPL_SHAPES + SHARDED_FULL_SHAPES, REPL_ROWS + SHARDED_FULL_ROWS)
    grads = dict(zip(REPL_NAMES, parts[:len(REPL_NAMES)]))
    for n, full_g, sh in zip(SHARDED_NAMES, parts[len(REPL_NAMES):], SHARDED_SHARD_SHAPES):
        grads[n] = lax.dynamic_slice_in_dim(full_g, chip * LANES, LANES, axis=1).reshape(sh)
    for n in big_names:
        grads[n] = big_grads[n].reshape(w[n].shape)

    delta, new_m, new_v = {}, {}, {}
    for n in big_names:
        d2, m2, v2 = _adamw(f"adamw_{n}", shard2d(w, n), big_grads[n], shard2d(mom, n), shard2d(vel, n))
        delta[n], new_m[n], new_v[n] = (t.reshape(w[n].shape) for t in (d2, m2, v2))
    for tag, group, rows, shapes in (("repl", REPL_NAMES, REPL_ROWS, [w[n].shape for n in REPL_NAMES]),
                                     ("sharded", SHARDED_NAMES, SHARDED_SHARD_ROWS, SHARDED_SHARD_SHAPES)):
        packs = [_pack([t[n] for n in group], rows) for t in (w, grads, mom, vel)]
        outs = _adamw(f"adamw_{tag}", *packs)
        for res, o in zip((delta, new_m, new_v), outs):
            res.update(zip(group, _unpack(o, shapes, rows)))

    out = [loss, dx[None]]
    for res in (grads, delta, new_m, new_v):
        out.extend(res[n] for n in names)
    return tuple(out)
```

```python
import functools

import jax
import jax.numpy as jnp
from jax import lax
from jax.experimental import pallas as pl
from jax.experimental.pallas import tpu as pltpu

F32 = jnp.float32
BF16 = jnp.bfloat16

T = 2048
D = 1024
CONV_CH = 512
CONV_W = 31
HEAD_DIM = 64
ATT_W = 1536
EVEN_IN = 5632
ODD_IN = 2560
SCONV_W = 3
SG_GROUPS = 4
CHUNK = 128
D_FF = 4096
EPS = 1e-6
DILATIONS = (1, 4, 16)
BAND = 128
SCALE = HEAD_DIM ** -0.5
NEG = -1e30

ADAM_LR = 0.001
ADAM_B1 = 0.9
ADAM_B2 = 0.999
ADAM_EPS = 1e-08
ADAM_WD = 0.01
ADAM_STEP = 10

V7X_VMEM_BYTES = 64 * 2 ** 20
VMEM_LIMIT = V7X_VMEM_BYTES - 8 * 2 ** 20
LANES = 128
TOKEN_SHAPE = (8, LANES)


def _pcall(body, **kw):
    return pl.pallas_call(body, **kw)


def _params(*sem):
    return pltpu.CompilerParams(dimension_semantics=sem, vmem_limit_bytes=VMEM_LIMIT)


def _dot(a, b, dims):
    return lax.dot_general(a, b, (dims, ((), ())), preferred_element_type=F32)


def _nn(a, b):
    return _dot(a, b, ((1,), (0,)))


def _nt(a, b):
    return _dot(a, b, ((1,), (1,)))


def _tn(a, b):
    return _dot(a, b, ((0,), (0,)))


def _sigmoid(x):
    return 1.0 / (1.0 + jnp.exp(-x))


MM_VMEM_BUDGET = 40 * 2 ** 20


def _mm_tiles(mode, m, n, k, a_bytes, b_bytes, extra_bytes, out_bytes):
    def divisors(total, unit):
        return [t for t in range(unit, total + 1, unit) if total % t == 0]

    best = None
    for tm in divisors(m, LANES if mode == "tn" else 8):
        for tn in divisors(n, LANES):
            blocks = tm * k * a_bytes + tn * k * b_bytes + tm * tn * (extra_bytes + out_bytes)
            casts = (tm * k * 2 if a_bytes == 4 else 0) + (tn * k * 2 if b_bytes == 4 else 0)
            if 2 * blocks + casts + tm * tn * 4 > MM_VMEM_BUDGET:
                continue
            key = ((m // tm) * (n // tn), (m // tm) * n * k * b_bytes, abs(tm - tn))
            if best is None or key < best[0]:
                best = (key, tm, tn)
    return best[1], best[2]


def _mm(name, mode, a, b, m, n, k, out_dtypes, *, b_off=0, extras=(), epi=None, tie=None):
    tm, tn = _mm_tiles(mode, m, n, k, a.dtype.itemsize, b.dtype.itemsize, sum(e.dtype.itemsize for e in extras),
                       sum(jnp.dtype(dt).itemsize for dt in out_dtypes))
    assert b_off % tn == 0
    b_off //= tn
    if mode == "nn":
        a_spec = pl.BlockSpec((tm, k), lambda i, j: (i, 0))
        b_spec = pl.BlockSpec((k, tn), lambda i, j: (0, j + b_off))
        dims = ((1,), (0,))
    elif mode == "nt":
        a_spec = pl.BlockSpec((tm, k), lambda i, j: (i, 0))
        b_spec = pl.BlockSpec((tn, k), lambda i, j: (j, 0))
        dims = ((1,), (1,))
    else:
        a_spec = pl.BlockSpec((k, tm), lambda i, j: (0, i))
        b_spec = pl.BlockSpec((k, tn), lambda i, j: (0, j))
        dims = ((0,), (0,))
    o_spec = pl.BlockSpec((tm, tn), lambda i, j: (i, j))
    n_extra = len(extras)
    ties = () if tie is None else (tie,)

    def body(a_ref, b_ref, *rest):
        rest = rest[len(ties):]
        acc = _dot(a_ref[...].astype(BF16), b_ref[...].astype(BF16), dims)
        vals = epi(acc, *[e[...] for e in rest[:n_extra]]) if epi is not None else (acc,)
        for o_ref, v in zip(rest[n_extra:], vals):
            o_ref[...] = v.astype(o_ref.dtype)

    outs = _pcall(
        body, name=name, grid=(m // tm, n // tn),
        in_specs=[a_spec, b_spec] + [pl.BlockSpec(TOKEN_SHAPE, lambda i, j: (0, 0))] * len(ties) + [o_spec] * n_extra,
        out_specs=[o_spec] * len(out_dtypes),
        out_shape=[jax.ShapeDtypeStruct((m, n), dt) for dt in out_dtypes],
        compiler_params=_params("parallel", "parallel"),
    )(a, b, *ties, *extras)
    return outs[0] if len(out_dtypes) == 1 else outs


def _rms_fwd(name, h, g, tm=512):
    def body(h_ref, g_ref, o_ref):
        x = h_ref[...]
        r = lax.rsqrt(jnp.mean(x * x, axis=-1, keepdims=True) + EPS)
        o_ref[...] = ((x * r) * g_ref[...]).astype(BF16)

    return _pcall(
        body, name=name, grid=(T // tm,),
        in_specs=[pl.BlockSpec((tm, D), lambda i: (i, 0)), pl.BlockSpec((1, D), lambda i: (0, 0))],
        out_specs=pl.BlockSpec((tm, D), lambda i: (i, 0)),
        out_shape=jax.ShapeDtypeStruct((T, D), BF16),
        compiler_params=_params("parallel"),
    )(h, g)


def _rms_bwd(name, h, dhn, g, dres, tm=512):
    def body(h_ref, d_ref, g_ref, r_ref, dh_ref, dg_ref):
        x = h_ref[...]
        r = lax.rsqrt(jnp.mean(x * x, axis=-1, keepdims=True) + EPS)
        nrm = x * r
        dy = d_ref[...]
        dn = dy * g_ref[...]
        dh_ref[...] = r_ref[...] + r * (dn - nrm * jnp.mean(dn * nrm, axis=-1, keepdims=True))

        @pl.when(pl.program_id(0) == 0)
        def _():
            dg_ref[...] = jnp.zeros_like(dg_ref)

        dg_ref[...] += jnp.sum(dy * nrm, axis=0, keepdims=True)

    row = pl.BlockSpec((tm, D), lambda i: (i, 0))
    vec = pl.BlockSpec((1, D), lambda i: (0, 0))
    return _pcall(
        body, name=name, grid=(T // tm,),
        in_specs=[row, row, vec, row], out_specs=[row, vec],
        out_shape=[jax.ShapeDtypeStruct((T, D), F32), jax.ShapeDtypeStruct((1, D), F32)],
        compiler_params=_params("arbitrary"),
    )(h, dhn, g, dres)


def _loss_head(h, g, target, tm=512):
    def body(h_ref, g_ref, t_ref, dh_ref, dg_ref, loss_ref):
        x = h_ref[...]
        r = lax.rsqrt(jnp.mean(x * x, axis=-1, keepdims=True) + EPS)
        nrm = x * r
        gain = g_ref[...]
        err = nrm * gain - t_ref[...]
        dy = err * (1.0 / D)
        dn = dy * gain
        dh_ref[...] = r * (dn - nrm * jnp.mean(dn * nrm, axis=-1, keepdims=True))

        @pl.when(pl.program_id(0) == 0)
        def _():
            dg_ref[...] = jnp.zeros_like(dg_ref)
            loss_ref[...] = jnp.zeros_like(loss_ref)

        dg_ref[...] += jnp.sum(dy * nrm, axis=0, keepdims=True)
        part = jnp.sum(jnp.sum(err * err, axis=1, keepdims=True), axis=0, keepdims=True) * (0.5 / D)
        loss_ref[...] += jnp.broadcast_to(part, (1, LANES))

    row = pl.BlockSpec((tm, D), lambda i: (i, 0))
    vec = pl.BlockSpec((1, D), lambda i: (0, 0))
    return _pcall(
        body, name="loss_head", grid=(T // tm,),
        in_specs=[row, vec, row], out_specs=[row, vec, pl.BlockSpec((1, LANES), lambda i: (0, 0))],
        out_shape=[jax.ShapeDtypeStruct((T, D), F32), jax.ShapeDtypeStruct((1, D), F32),
                   jax.ShapeDtypeStruct((1, LANES), F32)],
        compiler_params=_params("arbitrary"),
    )(h, g, target)


CONV_TILE = 256
CONV_HALO = 32


def _glu(z):
    return z[:, :CONV_CH] * _sigmoid(z[:, CONV_CH:])


def _econv_fwd(zc, conv_k, conv_b, ln_g, ln_b):
    R, H = CONV_TILE, CONV_HALO

    def body(z_ref, zh_ref, k_ref, b_ref, g_ref, be_ref, cv_ref, cat_ref):
        i = pl.program_id(0)
        glu = _glu(z_ref[...])
        halo = _glu(zh_ref[...]) * (i > 0).astype(F32)
        win = jnp.concatenate([halo, glu], axis=0)
        acc = jnp.zeros((R, CONV_CH), F32) + b_ref[...]
        for j in range(CONV_W):
            off = H - (CONV_W - 1) + j
            acc = acc + k_ref[j:j + 1, :] * win[off:off + R, :]
        cv_ref[...] = acc
        mu = jnp.mean(acc, axis=-1, keepdims=True)
        xc = acc - mu
        rstd = lax.rsqrt(jnp.mean(xc * xc, axis=-1, keepdims=True) + EPS)
        ln = xc * rstd * g_ref[...] + be_ref[...]
        cat_ref[...] = (ln * _sigmoid(ln)).astype(BF16)

    vec = pl.BlockSpec((1, CONV_CH), lambda i: (0, 0))
    return _pcall(
        body, name="econv_fwd", grid=(T // R,),
        in_specs=[pl.BlockSpec((R, 2 * CONV_CH), lambda i: (i, 0)),
                  pl.BlockSpec((H, 2 * CONV_CH), lambda i: (jnp.maximum(i * (R // H) - 1, 0), 0)),
                  pl.BlockSpec((CONV_W, CONV_CH), lambda i: (0, 0)), vec, vec, vec],
        out_specs=[pl.BlockSpec((R, CONV_CH), lambda i: (i, 0)), pl.BlockSpec((R, CONV_CH), lambda i: (i, 0))],
        out_shape=[jax.ShapeDtypeStruct((T, CONV_CH), F32), jax.ShapeDtypeStruct((T, D), BF16)],
        compiler_params=_params("parallel"),
    )(zc, zc, conv_k, conv_b, ln_g, ln_b)


def _econv_bwd_ln(cv, dcat, ln_g, ln_b):
    R = CONV_TILE

    def body(cv_ref, d_ref, g_ref, be_ref, dcv_ref, dg_ref, dbe_ref, dcb_ref):
        cv_t = cv_ref[...]
        mu = jnp.mean(cv_t, axis=-1, keepdims=True)
        xc = cv_t - mu
        rstd = lax.rsqrt(jnp.mean(xc * xc, axis=-1, keepdims=True) + EPS)
        xh = xc * rstd
        ln = xh * g_ref[...] + be_ref[...]
        sg = _sigmoid(ln)
        dln = d_ref[...] * (sg * (1.0 + ln * (1.0 - sg)))
        dxh = dln * g_ref[...]
        dcv = rstd * (dxh - jnp.mean(dxh, axis=-1, keepdims=True) - xh * jnp.mean(dxh * xh, axis=-1, keepdims=True))
        dcv_ref[...] = dcv

        @pl.when(pl.program_id(0) == 0)
        def _():
            dg_ref[...] = jnp.zeros_like(dg_ref)
            dbe_ref[...] = jnp.zeros_like(dbe_ref)
            dcb_ref[...] = jnp.zeros_like(dcb_ref)

        dg_ref[...] += jnp.sum(dln * xh, axis=0, keepdims=True)
        dbe_ref[...] += jnp.sum(dln, axis=0, keepdims=True)
        dcb_ref[...] += jnp.sum(dcv, axis=0, keepdims=True)

    vec = pl.BlockSpec((1, CONV_CH), lambda i: (0, 0))
    row = pl.BlockSpec((R, CONV_CH), lambda i: (i, 0))
    vshape = jax.ShapeDtypeStruct((1, CONV_CH), F32)
    return _pcall(
        body, name="econv_bwd_ln", grid=(T // R,),
        in_specs=[row, row, vec, vec], out_specs=[row, vec, vec, vec],
        out_shape=[jax.ShapeDtypeStruct((T, CONV_CH), F32), vshape, vshape, vshape],
        compiler_params=_params("arbitrary"),
    )(cv, dcat, ln_g, ln_b)


def _econv_bwd_conv(dcv, zc, conv_k):
    R, H = CONV_TILE, CONV_HALO
    last = T // R - 1

    def body(d_ref, dn_ref, z_ref, zh_ref, k_ref, dz_ref, dk_ref):
        i = pl.program_id(0)
        z = z_ref[...]
        a_lin = z[:, :CONV_CH]
        sg = _sigmoid(z[:, CONV_CH:])
        glu = a_lin * sg
        halo = _glu(zh_ref[...]) * (i > 0).astype(F32)
        win = jnp.concatenate([halo, glu], axis=0)
        dcv_t = d_ref[...]
        nxt = dn_ref[...] * (i < last).astype(F32)
        winb = jnp.concatenate([dcv_t, nxt], axis=0)

        @pl.when(i == 0)
        def _():
            dk_ref[...] = jnp.zeros_like(dk_ref)

        dglu = jnp.zeros((R, CONV_CH), F32)
        for j in range(CONV_W):
            off = H - (CONV_W - 1) + j
            dk_ref[j:j + 1, :] += jnp.sum(dcv_t * win[off:off + R, :], axis=0, keepdims=True)
            ob = CONV_W - 1 - j
            dglu = dglu + k_ref[j:j + 1, :] * winb[ob:ob + R, :]
        dz_ref[...] = jnp.concatenate([dglu * sg, dglu * a_lin * sg * (1.0 - sg)], axis=1).astype(BF16)

    return _pcall(
        body, name="econv_bwd_conv", grid=(T // R,),
        in_specs=[pl.BlockSpec((R, CONV_CH), lambda i: (i, 0)),
                  pl.BlockSpec((H, CONV_CH), lambda i: (jnp.minimum((i + 1) * (R // H), T // H - 1), 0)),
                  pl.BlockSpec((R, 2 * CONV_CH), lambda i: (i, 0)),
                  pl.BlockSpec((H, 2 * CONV_CH), lambda i: (jnp.maximum(i * (R // H) - 1, 0), 0)),
                  pl.BlockSpec((CONV_W, CONV_CH), lambda i: (0, 0))],
        out_specs=[pl.BlockSpec((R, 2 * CONV_CH), lambda i: (i, 0)), pl.BlockSpec((CONV_W, CONV_CH), lambda i: (0, 0))],
        out_shape=[jax.ShapeDtypeStruct((T, EVEN_IN), BF16), jax.ShapeDtypeStruct((CONV_W, CONV_CH), F32)],
        compiler_params=_params("arbitrary"),
    )(dcv, dcv, zc, zc, conv_k)


def _swap_halves(v):
    lane = lax.broadcasted_iota(jnp.int32, v.shape, 1)
    return jnp.where((lane % HEAD_DIM) < HEAD_DIM // 2, pltpu.roll(v, LANES - HEAD_DIM // 2, 1),
                     pltpu.roll(v, HEAD_DIM // 2, 1))


def _qkv_proj(hn, w_in, rope_c, rope_s, tm=1024):
    tn = 4 * LANES

    def body(a_ref, b_ref, c_ref, s_ref, o_ref):
        j = pl.program_id(1)
        acc = _nn(a_ref[...], b_ref[...])
        for p in range(4):
            v = acc[:, p * LANES:(p + 1) * LANES]
            rot = v * c_ref[...] + _swap_halves(v) * s_ref[...]
            o_ref[p] = jnp.where(j < 6, rot, v)

    tab = pl.BlockSpec((tm, LANES), lambda i, j: (i, 0))
    return _pcall(
        body, name="qkv_proj", grid=(T // tm, 9),
        in_specs=[pl.BlockSpec((tm, D), lambda i, j: (i, 0)),
                  pl.BlockSpec((D, tn), lambda i, j: (0, j + (2 * CONV_CH) // tn)), tab, tab],
        out_specs=pl.BlockSpec((None, 4, tm, LANES), lambda i, j: (j, 0, i, 0)),
        out_shape=jax.ShapeDtypeStruct((9, 4, T, LANES), F32),
        compiler_params=_params("parallel", "parallel"),
    )(hn, w_in, rope_c, rope_s)


ATTN_FWD_UNROLL = 4
ATTN_BWD_UNROLL = 2


def _band_rows(start, d):
    if d == 1:
        return pl.ds(pl.multiple_of(start, BAND), BAND)
    return pl.ds(start, BAND, stride=d)


def _band_masks(n):
    row = lax.broadcasted_iota(jnp.int32, (BAND, BAND), 0)
    col = lax.broadcasted_iota(jnp.int32, (BAND, BAND), 1)
    no_prev = (n == 0).astype(jnp.int32) * (2 * BAND)
    return col <= row, col >= row + no_prev


def _attn_fwd(qkv, g):
    d = DILATIONS[g]
    nb = T // d // BAND

    def body(q_ref, k_ref, v_ref, o_ref, l_ref):
        lane_lo = lax.broadcasted_iota(jnp.int32, (BAND, LANES), 1) < HEAD_DIM

        heads = (lane_lo, jnp.logical_not(lane_lo))
        ones = jnp.ones((BAND, LANES), BF16)

        def step(it, carry):
            tiles = []
            for u in range(ATTN_FWD_UNROLL):
                idx = it * ATTN_FWD_UNROLL + u
                r = idx // nb
                n = idx % nb
                cur = _band_rows(n * (BAND * d) + r, d)
                prev = _band_rows(jnp.maximum(n - 1, 0) * (BAND * d) + r, d)
                mc, mp = _band_masks(n)
                tiles.append((cur, mc, mp, q_ref[cur, :], k_ref[cur, :].astype(BF16), v_ref[cur, :].astype(BF16),
                              k_ref[prev, :].astype(BF16), v_ref[prev, :].astype(BF16)))
            scores = []
            for cur, mc, mp, q, kc, vc, kp, vp in tiles:
                for hm in heads:
                    qm = jnp.where(hm, q, 0.0).astype(BF16)
                    scores.append((jnp.where(mc, _nt(qm, kc) * SCALE, NEG), jnp.where(mp, _nt(qm, kp) * SCALE, NEG)))
            maxes = [jnp.maximum(jnp.max(sc, axis=1, keepdims=True), jnp.max(sp, axis=1, keepdims=True))
                     for sc, sp in scores]
            probs = [(jnp.exp(sc - mx).astype(BF16), jnp.exp(sp - mx).astype(BF16))
                     for (sc, sp), mx in zip(scores, maxes)]
            dens = [_nn(pc, ones) + _nn(pp, ones) for pc, pp in probs]
            for t, (cur, mc, mp, q, kc, vc, kp, vp) in enumerate(tiles):
                outs, lses = [], []
                for h in range(2):
                    pc, pp = probs[2 * t + h]
                    outs.append((_nn(pc, vc) + _nn(pp, vp)) / dens[2 * t + h])
                    lses.append(maxes[2 * t + h] + jnp.log(dens[2 * t + h]))
                o_ref[cur, :] = jnp.where(lane_lo, outs[0], outs[1])
                l_ref[cur, :] = jnp.where(lane_lo, lses[0], lses[1])
            return carry

        lax.fori_loop(0, d * nb // ATTN_FWD_UNROLL, step, 0)

    def slab(which):
        return pl.BlockSpec((None, None, T, LANES), lambda p: (which * 3 + g, p, 0, 0))

    out = pl.BlockSpec((None, T, LANES), lambda p: (p, 0, 0))
    shape = jax.ShapeDtypeStruct((4, T, LANES), F32)
    return _pcall(
        body, name=f"attn_fwd{g}", grid=(4,),
        in_specs=[slab(0), slab(1), slab(2)], out_specs=[out, out], out_shape=[shape, shape],
        compiler_params=_params("parallel"),
    )(qkv, qkv, qkv)


def _attn_merge(outs, lses, cat, tm=1024):
    def body(o0, o1, o2, l0, l1, l2, cat_in, cat_ref, att_ref, w0, w1, w2):
        del cat_in
        la, lb, lc = l0[...], l1[...], l2[...]
        mx = jnp.maximum(jnp.maximum(la, lb), lc)
        ea, eb, ec = jnp.exp(la - mx), jnp.exp(lb - mx), jnp.exp(lc - mx)
        inv = 1.0 / (ea + eb + ec)
        wa, wb, wc = ea * inv, eb * inv, ec * inv
        att = wa * o0[...] + wb * o1[...] + wc * o2[...]
        att_ref[...] = att
        cat_ref[...] = att.astype(BF16)
        w0[...] = wa
        w1[...] = wb
        w2[...] = wc

    slab = pl.BlockSpec((None, tm, LANES), lambda p, i: (p, i, 0))
    shape = jax.ShapeDtypeStruct((4, T, LANES), F32)
    return _pcall(
        body, name="attn_merge", grid=(4, T // tm),
        in_specs=[slab] * 6 + [pl.BlockSpec(memory_space=pl.ANY)],
        out_specs=[pl.BlockSpec((tm, LANES), lambda p, i: (i, CONV_CH // LANES + p)), slab, slab, slab, slab],
        out_shape=[jax.ShapeDtypeStruct((T, D), BF16), shape, shape, shape, shape],
        input_output_aliases={6: 0},
        compiler_params=_params("parallel", "parallel"),
    )(*outs, *lses, cat)


def _attn_bwd(qkv, lse, wgt, att, dcat, dqkv, g):
    d = DILATIONS[g]
    nb = T // d // BAND

    def body(q_ref, k_ref, v_ref, l_ref, w_ref, a_ref, da_ref, dq_in, o_ref):
        del dq_in
        lane = lax.broadcasted_iota(jnp.int32, (BAND, LANES), 1)
        lane_lo = lane < HEAD_DIM
        row = lax.broadcasted_iota(jnp.int32, (LANES, LANES), 0)
        same_head = ((row // HEAD_DIM) == (lane // HEAD_DIM)).astype(BF16)
        dq_ref, dk_ref, dv_ref = o_ref.at[0], o_ref.at[1], o_ref.at[2]
        dk_ref[...] = jnp.zeros((T, LANES), F32)
        dv_ref[...] = jnp.zeros((T, LANES), F32)

        heads = (lane_lo, jnp.logical_not(lane_lo))

        def step(it, carry):
            tiles = []
            for u in range(ATTN_BWD_UNROLL):
                idx = it * ATTN_BWD_UNROLL + u
                r = idx // nb
                n = idx % nb
                cur = _band_rows(n * (BAND * d) + r, d)
                prev = _band_rows(jnp.maximum(n - 1, 0) * (BAND * d) + r, d)
                mc, mp = _band_masks(n)
                da = da_ref[cur, :]
                prod = da * a_ref[cur, :]
                hi = prod.astype(BF16)
                lo = (prod - hi.astype(F32)).astype(BF16)
                tiles.append(dict(cur=cur, prev=prev, mc=mc, mp=mp, da=da, hi=hi, lo=lo, q=q_ref[cur, :],
                                  kc=k_ref[cur, :].astype(BF16), vc=v_ref[cur, :].astype(BF16),
                                  kp=k_ref[prev, :].astype(BF16), vp=v_ref[prev, :].astype(BF16),
                                  lse=l_ref[cur, :], w=w_ref[cur, :]))
            for t in tiles:
                t["csum"] = _nn(t["hi"], same_head) + _nn(t["lo"], same_head)
            chains = []
            for t in tiles:
                for h, hm in enumerate(heads):
                    qm = jnp.where(hm, t["q"], 0.0).astype(BF16)
                    dam = jnp.where(hm, t["da"], 0.0).astype(BF16)
                    chains.append(dict(t=t, h=h, qm=qm, dam=dam,
                                       sc=jnp.where(t["mc"], _nt(qm, t["kc"]) * SCALE, NEG),
                                       sp=jnp.where(t["mp"], _nt(qm, t["kp"]) * SCALE, NEG),
                                       dpc=_nt(dam, t["vc"]), dpp=_nt(dam, t["vp"])))
            for ch in chains:
                t, col0 = ch["t"], ch["h"] * HEAD_DIM
                lse_h = t["lse"][:, col0:col0 + 1]
                w_h = t["w"][:, col0:col0 + 1]
                c_h = t["csum"][:, col0:col0 + 1]
                pwc = w_h * jnp.exp(ch["sc"] - lse_h)
                pwp = w_h * jnp.exp(ch["sp"] - lse_h)
                ch["dsc"] = (pwc * (ch["dpc"] - c_h) * SCALE).astype(BF16)
                ch["dsp"] = (pwp * (ch["dpp"] - c_h) * SCALE).astype(BF16)
                ch["pwc"] = pwc.astype(BF16)
                ch["pwp"] = pwp.astype(BF16)
            for ch in chains:
                t = ch["t"]
                ch["dq"] = _nn(ch["dsc"], t["kc"]) + _nn(ch["dsp"], t["kp"])
                ch["dkc"] = _tn(ch["dsc"], ch["qm"])
                ch["dkp"] = _tn(ch["dsp"], ch["qm"])
                ch["dvc"] = _tn(ch["pwc"], ch["dam"])
                ch["dvp"] = _tn(ch["pwp"], ch["dam"])
            for i, t in enumerate(tiles):
                c0, c1 = chains[2 * i], chains[2 * i + 1]
                dq_ref[t["cur"], :] = jnp.where(lane_lo, c0["dq"], c1["dq"])
                dk_ref[t["cur"], :] += c0["dkc"] + c1["dkc"]
                dk_ref[t["prev"], :] += c0["dkp"] + c1["dkp"]
                dv_ref[t["cur"], :] += c0["dvc"] + c1["dvc"]
                dv_ref[t["prev"], :] += c0["dvp"] + c1["dvp"]
            return carry

        lax.fori_loop(0, d * nb // ATTN_BWD_UNROLL, step, 0)

    def slab(which):
        return pl.BlockSpec((None, None, T, LANES), lambda p: (which * 3 + g, p, 0, 0))

    per_pair = pl.BlockSpec((None, T, LANES), lambda p: (p, 0, 0))
    return _pcall(
        body, name=f"attn_bwd{g}", grid=(4,),
        in_specs=[slab(0), slab(1), slab(2), per_pair, per_pair, per_pair,
                  pl.BlockSpec((T, LANES), lambda p: (0, CONV_CH // LANES + p)),
                  pl.BlockSpec(memory_space=pl.ANY)],
        out_specs=pl.BlockSpec((None, 3, None, T, LANES), lambda p: (g, 0, p, 0, 0)),
        out_shape=jax.ShapeDtypeStruct((3, 3, 4, T, LANES), F32),
        input_output_aliases={7: 0},
        compiler_params=_params("parallel"),
    )(qkv, qkv, qkv, lse, wgt, att, dcat, dqkv)


def _rope_bwd(dqkv, rope_c, rope_s, dz):
    wide = 4 * LANES

    def body(d_ref, c_ref, s_ref, dz_in, o_ref):
        del dz_in
        w = pl.program_id(1)
        for p in range(4):
            v = d_ref[p]
            rot = v * c_ref[...] + _swap_halves(v * s_ref[...])
            o_ref[:, p * LANES:(p + 1) * LANES] = jnp.where(w < 2, rot, v).astype(BF16)

    tab = pl.BlockSpec((T, LANES), lambda g, w: (0, 0))
    return _pcall(
        body, name="rope_bwd", grid=(3, 3),
        in_specs=[pl.BlockSpec((None, None, 4, T, LANES), lambda g, w: (g, w, 0, 0, 0)), tab, tab,
                  pl.BlockSpec(memory_space=pl.ANY)],
        out_specs=pl.BlockSpec((T, wide), lambda g, w: (0, (2 * CONV_CH) // wide + w * 3 + g)),
        out_shape=jax.ShapeDtypeStruct((T, EVEN_IN), BF16),
        input_output_aliases={3: 0},
        compiler_params=_params("parallel", "parallel"),
    )(dqkv, rope_c, rope_s, dz)


ODD_TILE = 256
ODD_HALO = 8
GELU_C = 0.7978845608028654
GELU_A = 0.044715


def _gelu(x):
    return 0.5 * x * (1.0 + jnp.tanh(GELU_C * (x + GELU_A * x * x * x)))


def _gelu_grad(x):
    th = jnp.tanh(GELU_C * (x + GELU_A * x * x * x))
    return 0.5 * (1.0 + th) + 0.5 * x * (1.0 - th * th) * GELU_C * (1.0 + 3.0 * GELU_A * x * x)


def _tril():
    row = lax.broadcasted_iota(jnp.int32, (CHUNK, CHUNK), 0)
    col = lax.broadcasted_iota(jnp.int32, (CHUNK, CHUNK), 1)
    return (col <= row).astype(F32)


def _odd_parts(z, zh, i, k_ref, g_ref, be_ref, w_ref, bt_ref):
    R, H = ODD_TILE, ODD_HALO
    gb, gc, xs, uv = z[:, :512], z[:, 512:1024], z[:, 1024:1536], z[:, 1536:]
    halo = zh[:, 512:1024] * zh[:, 1024:1536] * (i > 0).astype(F32)
    win = jnp.concatenate([halo, gc * xs], axis=0)
    cv = jnp.zeros((R, 512), F32)
    for j in range(SCONV_W):
        off = H - (SCONV_W - 1) + j
        cv = cv + k_ref[j:j + 1, :] * win[off:off + R, :]
    ge = _gelu(uv)
    u, v = ge[:, :512], ge[:, 512:]
    mu = jnp.mean(v, axis=-1, keepdims=True)
    xc = v - mu
    rstd = lax.rsqrt(jnp.mean(xc * xc, axis=-1, keepdims=True) + EPS)
    xh = xc * rstd
    vn = xh * g_ref[...] + be_ref[...]
    tril = _tril()
    wms = [(w_ref[g] * tril).astype(BF16) for g in range(SG_GROUPS)]
    rows = []
    for ci in range(R // CHUNK):
        blocks = []
        for g in range(SG_GROUPS):
            blk = vn[ci * CHUNK:(ci + 1) * CHUNK, g * LANES:(g + 1) * LANES].astype(BF16)
            blocks.append(_nn(wms[g], blk) + bt_ref[:, g:g + 1])
        rows.append(jnp.concatenate(blocks, axis=1))
    vmix = jnp.concatenate(rows, axis=0)
    return gb, gc, xs, uv, win, cv, u, rstd, xh, vn, vmix, wms


def _odd_mid_fwd(z, conv_k, ln_g, ln_b, sg_w, sg_bt):
    R, H = ODD_TILE, ODD_HALO

    def body(z_ref, zh_ref, k_ref, g_ref, be_ref, w_ref, bt_ref, o_ref):
        i = pl.program_id(0)
        gb, _, _, _, _, cv, u, _, _, _, vmix, _ = _odd_parts(z_ref[...], zh_ref[...], i, k_ref, g_ref, be_ref, w_ref, bt_ref)
        o_ref[...] = jnp.concatenate([gb * cv, u * vmix], axis=1).astype(BF16)

    vec = pl.BlockSpec((1, 512), lambda i: (0, 0))
    return _pcall(
        body, name="odd_mid_fwd", grid=(T // R,),
        in_specs=[pl.BlockSpec((R, ODD_IN), lambda i: (i, 0)),
                  pl.BlockSpec((H, ODD_IN), lambda i: (jnp.maximum(i * (R // H) - 1, 0), 0)),
                  pl.BlockSpec((SCONV_W, 512), lambda i: (0, 0)), vec, vec,
                  pl.BlockSpec((SG_GROUPS, CHUNK, CHUNK), lambda i: (0, 0, 0)),
                  pl.BlockSpec((CHUNK, SG_GROUPS), lambda i: (0, 0))],
        out_specs=pl.BlockSpec((R, D), lambda i: (i, 0)),
        out_shape=jax.ShapeDtypeStruct((T, D), BF16),
        compiler_params=_params("parallel"),
    )(z, z, conv_k, ln_g, ln_b, sg_w, sg_bt)


def _odd_mid_bwd(z, dcat, conv_k, ln_g, ln_b, sg_w, sg_bt):
    R, H = ODD_TILE, ODD_HALO
    last = T // R - 1

    def body(z_ref, zh_ref, zn_ref, d_ref, dn_ref, k_ref, g_ref, be_ref, w_ref, bt_ref,
             dz_ref, dk_ref, dg_ref, dbe_ref, dw_ref, dbt_ref):
        i = pl.program_id(0)
        z = z_ref[...]
        gb, gc, xs, uv, win, cv, u, rstd, xh, vn, vmix, wms = _odd_parts(z, zh_ref[...], i, k_ref, g_ref, be_ref, w_ref, bt_ref)
        dcat_t = d_ref[...]
        dc, dd = dcat_t[:, :512], dcat_t[:, 512:]

        @pl.when(i == 0)
        def _():
            dk_ref[...] = jnp.zeros_like(dk_ref)
            dg_ref[...] = jnp.zeros_like(dg_ref)
            dbe_ref[...] = jnp.zeros_like(dbe_ref)
            dw_ref[...] = jnp.zeros_like(dw_ref)
            dbt_ref[...] = jnp.zeros_like(dbt_ref)

        dgb = dc * cv
        dcv = dc * gb
        nxt = dn_ref[:, :512] * zn_ref[:, :512] * (i < last).astype(F32)
        winb = jnp.concatenate([dcv, nxt], axis=0)
        dp = jnp.zeros((R, 512), F32)
        for j in range(SCONV_W):
            off = H - (SCONV_W - 1) + j
            dk_ref[j:j + 1, :] += jnp.sum(dcv * win[off:off + R, :], axis=0, keepdims=True)
            ob = SCONV_W - 1 - j
            dp = dp + k_ref[j:j + 1, :] * winb[ob:ob + R, :]
        dgc = dp * xs
        dxs = dp * gc
        du = dd * vmix
        dvmix = dd * u
        tril = _tril()
        rows = []
        for ci in range(R // CHUNK):
            blocks = []
            for g in range(SG_GROUPS):
                sl = (slice(ci * CHUNK, (ci + 1) * CHUNK), slice(g * LANES, (g + 1) * LANES))
                dblk = dvmix[sl]
                dblk16 = dblk.astype(BF16)
                blocks.append(_tn(wms[g], dblk16))
                dw_ref[g] += _nt(dblk16, vn[sl].astype(BF16)) * tril
                dbt_ref[:, g:g + 1] += jnp.sum(dblk, axis=1, keepdims=True)
            rows.append(jnp.concatenate(blocks, axis=1))
        dvn = jnp.concatenate(rows, axis=0)
        dg_ref[...] += jnp.sum(dvn * xh, axis=0, keepdims=True)
        dbe_ref[...] += jnp.sum(dvn, axis=0, keepdims=True)
        dxh = dvn * g_ref[...]
        dv = rstd * (dxh - jnp.mean(dxh, axis=-1, keepdims=True) - xh * jnp.mean(dxh * xh, axis=-1, keepdims=True))
        duv = jnp.concatenate([du, dv], axis=1) * _gelu_grad(uv)
        dz_ref[...] = jnp.concatenate([dgb, dgc, dxs, duv], axis=1).astype(BF16)

    vec = pl.BlockSpec((1, 512), lambda i: (0, 0))
    kspec = pl.BlockSpec((SCONV_W, 512), lambda i: (0, 0))
    wspec = pl.BlockSpec((SG_GROUPS, CHUNK, CHUNK), lambda i: (0, 0, 0))
    bspec = pl.BlockSpec((CHUNK, SG_GROUPS), lambda i: (0, 0))
    nxt_blk = lambda i: (jnp.minimum((i + 1) * (R // H), T // H - 1), 0)
    return _pcall(
        body, name="odd_mid_bwd", grid=(T // R,),
        in_specs=[pl.BlockSpec((R, ODD_IN), lambda i: (i, 0)),
                  pl.BlockSpec((H, ODD_IN), lambda i: (jnp.maximum(i * (R // H) - 1, 0), 0)),
                  pl.BlockSpec((H, ODD_IN), nxt_blk),
                  pl.BlockSpec((R, D), lambda i: (i, 0)),
                  pl.BlockSpec((H, D), nxt_blk),
                  kspec, vec, vec, wspec, bspec],
        out_specs=[pl.BlockSpec((R, ODD_IN), lambda i: (i, 0)), kspec, vec, vec, wspec, bspec],
        out_shape=[jax.ShapeDtypeStruct((T, ODD_IN), BF16), jax.ShapeDtypeStruct((SCONV_W, 512), F32),
                   jax.ShapeDtypeStruct((1, 512), F32), jax.ShapeDtypeStruct((1, 512), F32),
                   jax.ShapeDtypeStruct((SG_GROUPS, CHUNK, CHUNK), F32), jax.ShapeDtypeStruct((CHUNK, SG_GROUPS), F32)],
        compiler_params=_params("arbitrary"),
    )(z, z, z, dcat, dcat, conv_k, ln_g, ln_b, sg_w, sg_bt)


def _ffn_fwd(tag, h, g, weight):
    hn = _rms_fwd(f"ffn{tag}_norm", h, g)

    def act(acc):
        r = jnp.maximum(acc, 0.0)
        return (r * r,)

    f = _mm(f"ffn{tag}_up", "nn", hn, weight(f"ffn_w1_{tag}", hn), T, D_FF, D, (BF16,), epi=act)
    out = _mm(f"ffn{tag}_down", "nn", f, weight(f"ffn_w2_{tag}", f), T, D, D_FF, (F32,),
              epi=lambda acc, res: (acc + res,), extras=(h,))
    return out, (hn, f)


def _ffn_bwd(tag, h, g, weight, emit, saved, dout):
    hn, f = saved
    du = _mm(f"ffn{tag}_dact", "nt", dout, weight(f"ffn_w2_{tag}", dout), T, D_FF, D, (BF16,),
             epi=lambda acc, ff: (acc * (2.0 * jnp.sqrt(ff.astype(F32))),), extras=(f,))
    tok = emit(f"ffn_w2_{tag}", _mm(f"ffn{tag}_dw2", "tn", f, dout, D_FF, D, T, (BF16,)))
    tok = emit(f"ffn_w1_{tag}", _mm(f"ffn{tag}_dw1", "tn", hn, du, D, D_FF, T, (BF16,), tie=tok))
    dhn = _mm(f"ffn{tag}_dhn", "nt", du, weight(f"ffn_w1_{tag}", du), T, D, D_FF, (F32,), tie=tok)
    return _rms_bwd(f"ffn{tag}_dnorm", h, dhn, g, dout)


def _rope_tables():
    half = HEAD_DIM // 2
    inv = 10000.0 ** (-jnp.arange(half, dtype=F32) / half)
    ang = jnp.arange(T, dtype=F32)[:, None] * inv[None, :]
    cos, sin = jnp.cos(ang), jnp.sin(ang)
    c = jnp.tile(jnp.concatenate([cos, cos], axis=1), (1, LANES // HEAD_DIM))
    s = jnp.tile(jnp.concatenate([-sin, sin], axis=1), (1, LANES // HEAD_DIM))
    return c, s


def _local_step(x, target, p, weight, emit):
    rope_c, rope_s = _rope_tables()
    grads = {}
    residual = lambda acc, res: (acc + res,)

    hn0 = _rms_fwd("mix0_norm", x, p["norm_mix_g0"])
    zc = _mm("even_in_conv", "nn", hn0, weight("even_w_in", hn0), T, 2 * CONV_CH, D, (F32,))
    qkv = _qkv_proj(hn0, weight("even_w_in", hn0), rope_c, rope_s)
    cv, cat0 = _econv_fwd(zc, p["even_conv_k"], p["even_conv_b"], p["even_ln_g"], p["even_ln_b"])
    att_parts = [_attn_fwd(qkv, g) for g in range(3)]
    outs = [a[0] for a in att_parts]
    lses = [a[1] for a in att_parts]
    cat0, att, w0, w1, w2 = _attn_merge(outs, lses, cat0)
    wgts = (w0, w1, w2)
    h1 = _mm("even_out", "nn", cat0, weight("even_w_out", cat0), T, D, D, (F32,), epi=residual, extras=(x,))
    h2, ffn0_saved = _ffn_fwd(0, h1, p["norm_ffn_g0"], weight)

    hn1 = _rms_fwd("mix1_norm", h2, p["norm_mix_g1"])
    z1 = _mm("odd_in", "nn", hn1, weight("odd_w_in", hn1), T, ODD_IN, D, (F32,))
    cat1 = _odd_mid_fwd(z1, p["odd_conv_k"], p["odd_ln_g"], p["odd_ln_b"], p["odd_sg_w"], p["odd_sg_bt"])
    h3 = _mm("odd_out", "nn", cat1, weight("odd_w_out", cat1), T, D, D, (F32,), epi=residual, extras=(h2,))
    h4, ffn1_saved = _ffn_fwd(1, h3, p["norm_ffn_g1"], weight)

    dh4, grads["final_g"], loss = _loss_head(h4, p["final_g"], target)

    dh3, grads["norm_ffn_g1"] = _ffn_bwd(1, h3, p["norm_ffn_g1"], weight, emit, ffn1_saved, dh4)
    tok = emit("odd_w_out", _mm("odd_out_dw", "tn", cat1, dh3, D, D, T, (BF16,)))
    dcat1 = _mm("odd_out_dx", "nt", dh3, weight("odd_w_out", dh3), T, D, D, (F32,), tie=tok)
    dz1, grads["odd_conv_k"], grads["odd_ln_g"], grads["odd_ln_b"], grads["odd_sg_w"], grads["odd_sg_bt"] = _odd_mid_bwd(
        z1, dcat1, p["odd_conv_k"], p["odd_ln_g"], p["odd_ln_b"], p["odd_sg_w"], p["odd_sg_bt"])
    tok = emit("odd_w_in", _mm("odd_in_dw", "tn", hn1, dz1, D, ODD_IN, T, (BF16,)))
    dhn1 = _mm("odd_in_dx", "nt", dz1, weight("odd_w_in", dz1), T, D, ODD_IN, (F32,), tie=tok)
    dh2, grads["norm_mix_g1"] = _rms_bwd("mix1_dnorm", h2, dhn1, p["norm_mix_g1"], dh3)

    dh1, grads["norm_ffn_g0"] = _ffn_bwd(0, h1, p["norm_ffn_g0"], weight, emit, ffn0_saved, dh2)
    tok = emit("even_w_out", _mm("even_out_dw", "tn", cat0, dh1, D, D, T, (BF16,)))
    dcat0 = _mm("even_out_dx", "nt", dh1, weight("even_w_out", dh1), T, D, D, (F32,), tie=tok)
    dcv, grads["even_ln_g"], grads["even_ln_b"], grads["even_conv_b"] = _econv_bwd_ln(
        cv, dcat0, p["even_ln_g"], p["even_ln_b"])
    dz0, grads["even_conv_k"] = _econv_bwd_conv(dcv, zc, p["even_conv_k"])
    dqkv = lax.empty((3, 3, 4, T, LANES), F32)
    for g in range(3):
        dqkv = _attn_bwd(qkv, lses[g], wgts[g], att, dcat0, dqkv, g)
    dz0 = _rope_bwd(dqkv, rope_c, rope_s, dz0)
    tok = emit("even_w_in", _mm("even_in_dw", "tn", hn0, dz0, D, EVEN_IN, T, (BF16,)))
    dhn0 = _mm("even_in_dx", "nt", dz0, weight("even_w_in", dz0), T, D, EVEN_IN, (F32,), tie=tok)
    dx, grads["norm_mix_g0"] = _rms_bwd("mix0_dnorm", x, dhn0, p["norm_mix_g0"], dh1)
    return loss, dx, grads


def _rowwise(name, fn, ins, out_dtypes, tm=256):
    rows, cols = ins[0].shape
    tm = tm if rows % tm == 0 else rows
    n_in = len(ins)

    def body(*refs):
        vals = fn(*[r[...] for r in refs[:n_in]])
        for o_ref, v in zip(refs[n_in:], vals):
            o_ref[...] = v.astype(o_ref.dtype)

    spec = pl.BlockSpec((tm, cols), lambda i: (i, 0))
    outs = _pcall(
        body, name=name, grid=(rows // tm,),
        in_specs=[spec] * n_in, out_specs=[spec] * len(out_dtypes),
        out_shape=[jax.ShapeDtypeStruct((rows, cols), dt) for dt in out_dtypes],
        compiler_params=_params("parallel"),
    )(*ins)
    return outs[0] if len(out_dtypes) == 1 else outs


def _adamw(name, w, g, m, v):
    c1 = 1.0 - ADAM_B1 ** ADAM_STEP
    c2 = 1.0 - ADAM_B2 ** ADAM_STEP

    def fn(w_t, g_t, m_t, v_t):
        m_new = ADAM_B1 * m_t + (1.0 - ADAM_B1) * g_t
        v_new = ADAM_B2 * v_t + (1.0 - ADAM_B2) * (g_t * g_t)
        delta = -ADAM_LR * ((m_new / c1) / (jnp.sqrt(v_new / c2) + ADAM_EPS) + ADAM_WD * w_t)
        return delta, m_new, v_new

    return _rowwise(name, fn, (w, g, m, v), (F32, F32, F32))


class _Piece:
    def __init__(self, name, rows, cols, axis, src, src_row0):
        self.name, self.rows, self.cols, self.axis = name, rows, cols, axis
        self.width = (cols if axis == 1 else rows) // 4
        self.src, self.src_row0 = src, src_row0

    @property
    def full_shape(self):
        return (self.rows, self.cols)

    @property
    def half_shape(self):
        return (self.rows // 2, self.cols) if self.axis == 1 else (self.rows, self.cols // 2)

    @property
    def shard_half_shape(self):
        return (self.rows // 2, self.width) if self.axis == 1 else (self.width, self.cols // 2)

    def shard_whole(self, ref):
        n = self.rows if self.axis == 1 else self.width
        return ref.at[pl.ds(self.src_row0, n), :]

    def shard_half(self, ref, h):
        if self.axis == 1:
            return ref.at[pl.ds(self.src_row0 + h * (self.rows // 2), self.rows // 2), :]
        return ref.at[pl.ds(self.src_row0, self.width), pl.ds(h * (self.cols // 2), self.cols // 2)]

    def full_shard(self, ref, s):
        if self.axis == 1:
            return ref.at[:, pl.ds(s * self.width, self.width)]
        return ref.at[pl.ds(s * self.width, self.width), :]

    def full_shard_half(self, ref, s, h):
        if self.axis == 1:
            return ref.at[pl.ds(h * (self.rows // 2), self.rows // 2), pl.ds(s * self.width, self.width)]
        return ref.at[pl.ds(s * self.width, self.width), pl.ds(h * (self.cols // 2), self.cols // 2)]

    def full_half(self, ref, h):
        if self.axis == 1:
            return ref.at[pl.ds(h * (self.rows // 2), self.rows // 2), :]
        return ref.at[:, pl.ds(h * (self.cols // 2), self.cols // 2)]

    def full_half_rows(self, ref, h, r0, n):
        if self.axis == 1:
            return ref.at[pl.ds(h * (self.rows // 2) + r0, n), :]
        return ref.at[pl.ds(r0, n), pl.ds(h * (self.cols // 2), self.cols // 2)]

    def half_shard(self, ref, s):
        return self.full_shard(ref, s)


PIECES = (
    _Piece("even_w_in", D, EVEN_IN, 1, 0, 0),
    _Piece("even_w_out", D, D, 0, 1, 0),
    _Piece("ffn_w1_0", D, D_FF, 1, 4, 0),
    _Piece("ffn_w2_0", D_FF, D, 0, 5, 0),
    _Piece("odd_w_in", D, ODD_IN, 1, 2, 0),
    _Piece("odd_w_out", D, D, 0, 3, 0),
    _Piece("ffn_w1_1", D, D_FF, 1, 4, D),
    _Piece("ffn_w2_1", D_FF, D, 0, 5, D_FF // 4),
)
N_PIECES = len(PIECES)
N_SHARD_OPERANDS = 6
ANY = pl.BlockSpec(memory_space=pl.ANY)
MESH = pl.DeviceIdType.MESH


def _mesh_place():
    x, y, c = lax.axis_index("x"), lax.axis_index("y"), lax.axis_index("c")
    chips = [(1 - x, y), (x, 1 - y), (1 - x, 1 - y)]
    return x, y, c, chips


def _remote(src, dst, send_sem, recv_sem, dev):
    return pltpu.make_async_remote_copy(src_ref=src, dst_ref=dst, send_sem=send_sem, recv_sem=recv_sem,
                                        device_id=dev, device_id_type=MESH)


HBM = pl.BlockSpec(memory_space=pltpu.HBM)
SEM = pl.BlockSpec(memory_space=pltpu.SEMAPHORE)
SPLIT_PARAMS = pltpu.CompilerParams(has_side_effects=pltpu.SideEffectType.DATAFLOW_SIDE_EFFECTING)
CAST_TILE = 256


def _in_hbm(a):
    return pltpu.with_memory_space_constraint(a, pltpu.HBM)


def _cast_place(pc, shard_operand, chip, tie=None):
    rows, cols = (pc.rows, pc.width) if pc.axis == 1 else (pc.width, pc.cols)
    nblk = rows // CAST_TILE
    blk0 = pc.src_row0 // CAST_TILE
    ties = () if tie is None else (tie,)

    def body(chip_ref, x_ref, *rest):
        del chip_ref
        rest[-1][...] = x_ref[...].astype(BF16)

    if pc.axis == 1:
        out_map = lambda i, chip_ref: (i, chip_ref[0])
    else:
        out_map = lambda i, chip_ref: (chip_ref[0] * nblk + i, 0)
    return _pcall(
        body, name=f"cast_{pc.name}",
        grid_spec=pltpu.PrefetchScalarGridSpec(
            num_scalar_prefetch=1, grid=(nblk,),
            in_specs=[pl.BlockSpec((CAST_TILE, cols), lambda i, chip_ref: (blk0 + i, 0))]
            + [pl.BlockSpec(TOKEN_SHAPE, lambda i, chip_ref: (0, 0))] * len(ties),
            out_specs=pl.BlockSpec((CAST_TILE, cols), out_map)),
        out_shape=jax.ShapeDtypeStruct(pc.full_shape, BF16),
        compiler_params=_params("parallel"),
    )(chip, shard_operand, *ties)


def _gather_start(name, pieces, fulls):
    n = len(pieces)

    def body(*refs):
        ins = refs[:n]
        sends = refs[2 * n:3 * n]
        recvs = refs[3 * n:4 * n]
        token = refs[4 * n]
        x, y, c, chips = _mesh_place()
        s = 2 * x + y
        for i, pc in enumerate(pieces):
            win = pc.full_shard_half(ins[i], s, c)
            for k, (cx, cy) in enumerate(chips):
                _remote(win, win, sends[i].at[k], recvs[i].at[k], (cx, cy, c)).start()
        token[...] = jnp.zeros(TOKEN_SHAPE, F32)

    sems = [pltpu.SemaphoreType.DMA((3,))] * (2 * n)
    outs = _pcall(
        body, name=name,
        in_specs=[HBM] * n,
        out_specs=[HBM] * n + [SEM] * (2 * n) + [pl.BlockSpec(memory_space=pltpu.VMEM)],
        out_shape=[pltpu.HBM(pc.full_shape, BF16) for pc in pieces] + sems + [jax.ShapeDtypeStruct(TOKEN_SHAPE, F32)],
        input_output_aliases={i: i for i in range(n)},
        compiler_params=SPLIT_PARAMS,
    )(*[_in_hbm(f) for f in fulls])
    return outs[:n], outs[n:2 * n], outs[2 * n:3 * n], outs[3 * n]


def _gather_wait(pc, full, send_sems, recv_sems, after):
    def body(full_ref, send_ref, recv_ref, after_ref, out_ref):
        del after_ref, out_ref
        x, y, c, chips = _mesh_place()
        for k, (cx, cy) in enumerate(chips):
            win = pc.full_shard_half(full_ref, 2 * cx + cy, c)
            cp = _remote(win, win, send_ref.at[k], recv_ref.at[k], (cx, cy, c))
            cp.wait_send()
            cp.wait_recv()

    return _pcall(
        body, name=f"gather_wait_{pc.name}",
        in_specs=[HBM, SEM, SEM, ANY], out_specs=HBM, out_shape=pltpu.HBM(pc.full_shape, BF16),
        input_output_aliases={0: 0}, compiler_params=SPLIT_PARAMS,
    )(full, send_sems, recv_sems, after)


def _core_forward(pc, full):
    sh = pc.shard_half_shape

    def body(full_in, full_ref, send_buf, recv_buf, load_sems, send_sems, recv_sems, store_sems):
        x, y, c, chips = _mesh_place()
        loads, sends, stores = [], [], []
        for k, (cx, cy) in enumerate(chips):
            cp = pltpu.make_async_copy(pc.full_shard_half(full_in, 2 * cx + cy, c), send_buf.at[k], load_sems.at[k])
            cp.start()
            loads.append(cp)
        for k in range(3):
            loads[k].wait()
            cp = _remote(send_buf.at[k], recv_buf.at[k], send_sems.at[k], recv_sems.at[k], (x, y, 1 - c))
            cp.start()
            sends.append(cp)
        for k, (cx, cy) in enumerate(chips):
            sends[k].wait_recv()
            cp = pltpu.make_async_copy(recv_buf.at[k], pc.full_shard_half(full_ref, 2 * cx + cy, 1 - c), store_sems.at[k])
            cp.start()
            stores.append(cp)
        for k in range(3):
            sends[k].wait_send()
            stores[k].wait()

    sems = pltpu.SemaphoreType.DMA((3,))
    return _pcall(
        body, name=f"core_forward_{pc.name}", in_specs=[ANY], out_specs=ANY,
        out_shape=jax.ShapeDtypeStruct(pc.full_shape, BF16),
        scratch_shapes=[pltpu.VMEM((3,) + sh, BF16), pltpu.VMEM((3,) + sh, BF16), sems, sems, sems, sems],
        input_output_aliases={0: 0},
        compiler_params=pltpu.CompilerParams(vmem_limit_bytes=VMEM_LIMIT),
    )(full)


CHIPSUM_CHUNKS = 4


def _chipsum(pc, partial):
    hr, hc = pc.half_shape
    ch = hr // CHIPSUM_CHUNKS

    def body(g_ref, out_ref, send_buf, recv_buf, own_buf, sum_buf, load_sems, own_sems, send_sems, recv_sems, out_sems):
        x, y, c, _ = _mesh_place()
        chunks = [pl.ds(k * ch, ch) for k in range(CHIPSUM_CHUNKS)]
        loads, owns, sends, stores = [], [], [], []
        for k, rows in enumerate(chunks):
            cp = pltpu.make_async_copy(pc.full_half_rows(g_ref, 1 - c, k * ch, ch), send_buf.at[rows, :], load_sems.at[k])
            cp.start()
            loads.append(cp)
            cp = pltpu.make_async_copy(pc.full_half_rows(g_ref, c, k * ch, ch), own_buf.at[rows, :], own_sems.at[k])
            cp.start()
            owns.append(cp)
        for k, rows in enumerate(chunks):
            loads[k].wait()
            cp = _remote(send_buf.at[rows, :], recv_buf.at[rows, :], send_sems.at[k], recv_sems.at[k], (x, y, 1 - c))
            cp.start()
            sends.append(cp)
        for k, rows in enumerate(chunks):
            owns[k].wait()
            sends[k].wait_recv()
            sum_buf[rows, :] = (own_buf[rows, :].astype(F32) + recv_buf[rows, :].astype(F32)).astype(BF16)
            cp = pltpu.make_async_copy(sum_buf.at[rows, :], out_ref.at[rows, :], out_sems.at[k])
            cp.start()
            stores.append(cp)
        for k in range(CHIPSUM_CHUNKS):
            sends[k].wait_send()
            stores[k].wait()

    buf = pltpu.VMEM((hr, hc), BF16)
    sems = pltpu.SemaphoreType.DMA((CHIPSUM_CHUNKS,))
    return _pcall(
        body, name=f"chipsum_{pc.name}", in_specs=[ANY], out_specs=ANY,
        out_shape=jax.ShapeDtypeStruct((hr, hc), BF16),
        scratch_shapes=[buf, buf, buf, buf, sems, sems, sems, sems, sems],
        compiler_params=pltpu.CompilerParams(vmem_limit_bytes=VMEM_LIMIT),
    )(partial)


def _scatter_start(pc, chip_sum):
    def body(sum_ref, land_ref, sum_out, land_out, sends, recvs, token):
        del sum_out, land_out
        x, y, c, chips = _mesh_place()
        for k, (cx, cy) in enumerate(chips):
            _remote(pc.half_shard(sum_ref, 2 * cx + cy), land_ref.at[k], sends.at[k], recvs.at[k], (cx, cy, c)).start()
        token[...] = jnp.zeros(TOKEN_SHAPE, F32)

    land_shape = (3,) + pc.shard_half_shape
    sems = pltpu.SemaphoreType.DMA((3,))
    return _pcall(
        body, name=f"scatter_start_{pc.name}",
        in_specs=[HBM, HBM], out_specs=[HBM, HBM, SEM, SEM, pl.BlockSpec(memory_space=pltpu.VMEM)],
        out_shape=[pltpu.HBM(pc.half_shape, BF16), pltpu.HBM(land_shape, BF16), sems, sems,
                   jax.ShapeDtypeStruct(TOKEN_SHAPE, F32)],
        input_output_aliases={0: 0, 1: 1}, compiler_params=SPLIT_PARAMS,
    )(_in_hbm(chip_sum), _in_hbm(lax.empty(land_shape, BF16)))


def _scatter_wait(pc, chip_sum, land, send_sems, recv_sems, after):
    def body(sum_ref, land_ref, send_ref, recv_ref, after_ref, sum_out, land_out):
        del after_ref, sum_out, land_out
        x, y, c, chips = _mesh_place()
        for k, (cx, cy) in enumerate(chips):
            cp = _remote(pc.half_shard(sum_ref, 2 * cx + cy), land_ref.at[k], send_ref.at[k], recv_ref.at[k], (cx, cy, c))
            cp.wait_send()
            cp.wait_recv()

    return _pcall(
        body, name=f"scatter_wait_{pc.name}",
        in_specs=[HBM, HBM, SEM, SEM, ANY], out_specs=[HBM, HBM],
        out_shape=[pltpu.HBM(pc.half_shape, BF16), pltpu.HBM((3,) + pc.shard_half_shape, BF16)],
        input_output_aliases={0: 0, 1: 1}, compiler_params=SPLIT_PARAMS,
    )(chip_sum, land, send_sems, recv_sems, after)


SHARD_OPERAND_SHAPES = ((D, EVEN_IN // 4), (D // 4, D), (D, ODD_IN // 4), (D // 4, D), (2 * D, D_FF // 4), (2 * D_FF // 4, D))


def _allsum_join(operand, chip_sums, lands):
    pieces = [pc for pc in PIECES if pc.src == operand]
    n = len(pieces)

    def body(*refs):
        sum_refs = refs[:n]
        land_refs = refs[n:2 * n]
        refs = refs[n:]
        out_ref = refs[n]
        in_bufs = refs[n + 1:2 * n + 1]
        fin_bufs = refs[2 * n + 1:3 * n + 1]
        recv_bufs = refs[3 * n + 1:4 * n + 1]
        load_sems, send_sems, recv_sems, out_sems = refs[4 * n + 1:]
        x, y, c, _ = _mesh_place()
        s = 2 * x + y
        loads, sends, stores = [], [], []
        for j, pc in enumerate(pieces):
            cp = pltpu.make_async_copy(land_refs[j], in_bufs[j].at[pl.ds(0, 3)], load_sems.at[2 * j])
            cp.start()
            loads.append(cp)
            cp = pltpu.make_async_copy(pc.half_shard(sum_refs[j], s), in_bufs[j].at[3], load_sems.at[2 * j + 1])
            cp.start()
            loads.append(cp)
        for j, pc in enumerate(pieces):
            loads[2 * j].wait()
            loads[2 * j + 1].wait()
            acc = in_bufs[j][0].astype(F32)
            for k in range(1, 4):
                acc = acc + in_bufs[j][k].astype(F32)
            fin_bufs[j][...] = acc
            cp = pltpu.make_async_copy(fin_bufs[j], pc.shard_half(out_ref, c), out_sems.at[2 * j])
            cp.start()
            stores.append(cp)
            cp = _remote(fin_bufs[j], recv_bufs[j], send_sems.at[j], recv_sems.at[j], (x, y, 1 - c))
            cp.start()
            sends.append(cp)
        for j, pc in enumerate(pieces):
            sends[j].wait_recv()
            cp = pltpu.make_async_copy(recv_bufs[j], pc.shard_half(out_ref, 1 - c), out_sems.at[2 * j + 1])
            cp.start()
            stores.append(cp)
        for cp in sends:
            cp.wait_send()
        for cp in stores:
            cp.wait()

    sh = pieces[0].shard_half_shape
    return _pcall(
        body, name=f"allsum_join_{operand}", in_specs=[ANY] * (2 * n), out_specs=ANY,
        out_shape=jax.ShapeDtypeStruct(SHARD_OPERAND_SHAPES[operand], F32),
        scratch_shapes=[pltpu.VMEM((4,) + sh, BF16)] * n + [pltpu.VMEM(sh, F32)] * (2 * n)
        + [pltpu.SemaphoreType.DMA((2 * n,)), pltpu.SemaphoreType.DMA((n,)), pltpu.SemaphoreType.DMA((n,)),
           pltpu.SemaphoreType.DMA((2 * n,))],
        compiler_params=pltpu.CompilerParams(vmem_limit_bytes=VMEM_LIMIT),
    )(*chip_sums, *lands)


PEER_FLIPS = tuple((a, b, e) for a in (0, 1) for b in (0, 1) for e in (0, 1) if (a, b, e) != (0, 0, 0))


def _peers():
    x, y, c = lax.axis_index("x"), lax.axis_index("y"), lax.axis_index("c")
    me = 4 * x + 2 * y + c
    out = []
    for a, b, e in PEER_FLIPS:
        px, py, pc = (1 - x if a else x), (1 - y if b else y), (1 - c if e else c)
        out.append(((px, py, pc), 4 * px + 2 * py + pc))
    return me, out


def _exchange8_start(name, blk):
    m = blk.shape[0]

    def body(blk_ref, land_ref, blk_out, land_out, sends, recvs, token, local_sem):
        del blk_out, land_out
        me, peers = _peers()
        mine = pltpu.make_async_copy(blk_ref, land_ref.at[me], local_sem)
        mine.start()
        for k, (dev, _) in enumerate(peers):
            _remote(blk_ref, land_ref.at[me], sends.at[k], recvs.at[k], dev).start()
        token[...] = jnp.zeros(TOKEN_SHAPE, F32)
        mine.wait()

    sems = pltpu.SemaphoreType.DMA((7,))
    return _pcall(
        body, name=name,
        in_specs=[HBM, HBM], out_specs=[HBM, HBM, SEM, SEM, pl.BlockSpec(memory_space=pltpu.VMEM)],
        out_shape=[pltpu.HBM((m, LANES), F32), pltpu.HBM((8, m, LANES), F32), sems, sems,
                   jax.ShapeDtypeStruct(TOKEN_SHAPE, F32)],
        scratch_shapes=[pltpu.SemaphoreType.DMA],
        input_output_aliases={0: 0, 1: 1}, compiler_params=SPLIT_PARAMS,
    )(_in_hbm(blk), _in_hbm(lax.empty((8, m, LANES), F32)))


def _exchange8_wait(name, blk, land, send_sems, recv_sems, after):
    def body(blk_ref, land_ref, send_ref, recv_ref, after_ref, blk_out, land_out):
        del after_ref, blk_out, land_out
        _, peers = _peers()
        for k, (dev, slot) in enumerate(peers):
            cp = _remote(blk_ref, land_ref.at[slot], send_ref.at[k], recv_ref.at[k], dev)
            cp.wait_send()
            cp.wait_recv()

    m = blk.shape[0]
    return _pcall(
        body, name=name,
        in_specs=[HBM, HBM, SEM, SEM, ANY], out_specs=[HBM, HBM],
        out_shape=[pltpu.HBM((m, LANES), F32), pltpu.HBM((8, m, LANES), F32)],
        input_output_aliases={0: 0, 1: 1}, compiler_params=SPLIT_PARAMS,
    )(blk, land, send_sems, recv_sems, after)[1]


def _sum8(name, stacked):
    def body(s_ref, o_ref):
        acc = s_ref[0]
        for dev in range(1, 8):
            acc = acc + s_ref[dev]
        o_ref[...] = acc

    vm = pl.BlockSpec(memory_space=pltpu.VMEM)
    return _pcall(body, name=name, in_specs=[vm], out_specs=vm,
                  out_shape=jax.ShapeDtypeStruct(stacked.shape[1:], F32))(stacked)


def _pack(arrays, row_counts):
    rows = []
    for a, n in zip(arrays, row_counts):
        flat = a.reshape(-1, LANES)
        rows.append(jnp.pad(flat, ((0, n - flat.shape[0]), (0, 0))))
    return jnp.concatenate(rows, axis=0)


def _unpack(buf, shapes, row_counts):
    out, r0 = [], 0
    for sh, n in zip(shapes, row_counts):
        size = 1
        for dim in sh:
            size *= dim
        out.append(buf[r0:r0 + size // LANES].reshape(sh))
        r0 += n
    return out


REPL_NAMES = ("norm_mix_g", "norm_ffn_g", "even_conv_b", "even_ln_g", "even_ln_b", "odd_sg_w", "odd_sg_b", "final_g")
REPL_SHAPES = ((2, D), (2, D), (1, 512), (1, 512), (1, 512), (1, SG_GROUPS, CHUNK, CHUNK), (1, SG_GROUPS, CHUNK), (D,))
REPL_ROWS = (16, 16, 8, 8, 8, 512, 8, 8)
SHARDED_NAMES = ("even_conv_k", "odd_conv_k", "odd_ln_g", "odd_ln_b")
SHARDED_SHARD_SHAPES = ((1, CONV_W, LANES), (1, SCONV_W, LANES), (1, LANES), (1, LANES))
SHARDED_SHARD_ROWS = (32, 8, 8, 8)
SHARDED_FULL_SHAPES = ((CONV_W, 512), (SCONV_W, 512), (1, 512), (1, 512))
SHARDED_FULL_ROWS = (128, 16, 8, 8)


def kernel(x, norm_mix_g, norm_ffn_g, even_w_in, even_conv_k, even_conv_b, even_ln_g, even_ln_b, even_w_out, odd_w_in, odd_conv_k, odd_ln_g, odd_ln_b, odd_sg_w, odd_sg_b, odd_w_out, ffn_w1, ffn_w2, final_g, loss_target, m_norm_mix_g, m_norm_ffn_g, m_even_w_in, m_even_conv_k, m_even_conv_b, m_even_ln_g, m_even_ln_b, m_even_w_out, m_odd_w_in, m_odd_conv_k, m_odd_ln_g, m_odd_ln_b, m_odd_sg_w, m_odd_sg_b, m_odd_w_out, m_ffn_w1, m_ffn_w2, m_final_g, v_norm_mix_g, v_norm_ffn_g, v_even_w_in, v_even_conv_k, v_even_conv_b, v_even_ln_g, v_even_ln_b, v_even_w_out, v_odd_w_in, v_odd_conv_k, v_odd_ln_g, v_odd_ln_b, v_odd_sg_w, v_odd_sg_b, v_odd_w_out, v_ffn_w1, v_ffn_w2, v_final_g):
    names = ("norm_mix_g", "norm_ffn_g", "even_w_in", "even_conv_k", "even_conv_b", "even_ln_g", "even_ln_b", "even_w_out",
             "odd_w_in", "odd_conv_k", "odd_ln_g", "odd_ln_b", "odd_sg_w", "odd_sg_b", "odd_w_out", "ffn_w1", "ffn_w2", "final_g")
    w = dict(zip(names, (norm_mix_g, norm_ffn_g, even_w_in, even_conv_k, even_conv_b, even_ln_g, even_ln_b, even_w_out,
                         odd_w_in, odd_conv_k, odd_ln_g, odd_ln_b, odd_sg_w, odd_sg_b, odd_w_out, ffn_w1, ffn_w2, final_g)))
    mom = dict(zip(names, (m_norm_mix_g, m_norm_ffn_g, m_even_w_in, m_even_conv_k, m_even_conv_b, m_even_ln_g, m_even_ln_b,
                           m_even_w_out, m_odd_w_in, m_odd_conv_k, m_odd_ln_g, m_odd_ln_b, m_odd_sg_w, m_odd_sg_b, m_odd_w_out,
                           m_ffn_w1, m_ffn_w2, m_final_g)))
    vel = dict(zip(names, (v_norm_mix_g, v_norm_ffn_g, v_even_w_in, v_even_conv_k, v_even_conv_b, v_even_ln_g, v_even_ln_b,
                           v_even_w_out, v_odd_w_in, v_odd_conv_k, v_odd_ln_g, v_odd_ln_b, v_odd_sg_w, v_odd_sg_b, v_odd_w_out,
                           v_ffn_w1, v_ffn_w2, v_final_g)))
    big_names = ("even_w_in", "even_w_out", "odd_w_in", "odd_w_out", "ffn_w1", "ffn_w2")
    chip = 2 * lax.axis_index("x") + lax.axis_index("y")

    def shard2d(t, name):
        return t[name].reshape(SHARD_OPERAND_SHAPES[big_names.index(name)])

    chip_op = jnp.reshape(chip, (1,)).astype(jnp.int32)
    small_pack = _pack([w[n] for n in SHARDED_NAMES], SHARDED_SHARD_ROWS)
    small_blk, small_land, small_send, small_recv, small_token = _exchange8_start("gather_small_start", small_pack)
    first = _cast_place(PIECES[0], shard2d(w, big_names[PIECES[0].src]), chip_op, tie=small_token)
    fly0, send0, recv0, token = _gather_start("gather_start_first", PIECES[:1], [first])
    placed = [_cast_place(pc, shard2d(w, big_names[pc.src]), chip_op, tie=token) for pc in PIECES[1:]]
    fly1, send1, recv1, all_started = _gather_start("gather_start_rest", PIECES[1:], placed)
    flying, gather_send, gather_recv = fly0 + fly1, send0 + send1, recv0 + recv1
    ready = {}

    def weight(name, after):
        if name not in ready:
            i = [pc.name for pc in PIECES].index(name)
            if i == 0:
                after = all_started
            landed = _gather_wait(PIECES[i], flying[i], gather_send[i], gather_recv[i], after)
            ready[name] = _core_forward(PIECES[i], landed)
        return ready[name]

    scattering = []

    def emit(name, partial):
        pc = PIECES[[q.name for q in PIECES].index(name)]
        chip_sum, land, send_sems, recv_sems, token = _scatter_start(pc, _chipsum(pc, partial))
        scattering.append((pc, chip_sum, land, send_sems, recv_sems))
        return token

    full = {}
    gathered = _exchange8_wait("gather_small_wait", small_blk, small_land, small_send, small_recv, all_started)
    gathered = gathered.reshape(4, 2, sum(SHARDED_SHARD_ROWS), LANES)[:, 0]
    r0 = 0
    for n, sh, rows, full_sh in zip(SHARDED_NAMES, SHARDED_SHARD_SHAPES, SHARDED_SHARD_ROWS, SHARDED_FULL_SHAPES):
        per_chip = gathered[:, r0:r0 + rows].reshape(4, -1)[:, :full_sh[0] * LANES].reshape(4, full_sh[0], LANES)
        full[n] = jnp.transpose(per_chip, (1, 0, 2)).reshape(full_sh)
        r0 += rows
    p = dict(full)
    p.update(norm_mix_g0=norm_mix_g[0:1], norm_mix_g1=norm_mix_g[1:2], norm_ffn_g0=norm_ffn_g[0:1], norm_ffn_g1=norm_ffn_g[1:2],
             even_conv_b=even_conv_b, even_ln_g=even_ln_g, even_ln_b=even_ln_b,
             odd_sg_w=odd_sg_w[0], odd_sg_bt=odd_sg_b[0].T, final_g=final_g[None, :])

    loss_row, dx, g = _local_step(x[0], loss_target[0], p, weight, emit)

    grad_parts = [loss_row, g["norm_mix_g0"], g["norm_mix_g1"], g["norm_ffn_g0"], g["norm_ffn_g1"], g["even_conv_b"],
                  g["even_ln_g"], g["even_ln_b"], g["odd_sg_w"], g["odd_sg_bt"].T, g["final_g"],
                  g["even_conv_k"], g["odd_conv_k"], g["odd_ln_g"], g["odd_ln_b"]]
    grad_pack = _pack(grad_parts, (8, 8, 8, 8, 8) + REPL_ROWS[2:] + SHARDED_FULL_ROWS)
    grad_blk, grad_land, grad_send, grad_recv, grad_token = _exchange8_start("allreduce_small_start", grad_pack)

    landed = {pc.name: _scatter_wait(pc, chip_sum, land, send_sems, recv_sems, grad_token)
              for pc, chip_sum, land, send_sems, recv_sems in scattering}
    big_grads = {n: _allsum_join(o, [landed[pc.name][0] for pc in PIECES if pc.src == o],
                                 [landed[pc.name][1] for pc in PIECES if pc.src == o])
                 for o, n in enumerate(big_names)}

    grad_all = _exchange8_wait("allreduce_small_wait", grad_blk, grad_land, grad_send, grad_recv,
                               big_grads[big_names[-1]])
    grad_sum = _sum8("allreduce_small_sum", grad_all)
    loss = grad_sum[0, 0]
    parts = _unpack(grad_sum[8:], REPL_SHAPES + SHARDED_FULL_SHAPES, REPL_ROWS + SHARDED_FULL_ROWS)
    grads = dict(zip(REPL_NAMES, parts[:len(REPL_NAMES)]))
    for n, full_g, sh in zip(SHARDED_NAMES, parts[len(REPL_NAMES):], SHARDED_SHARD_SHAPES):
        grads[n] = lax.dynamic_slice_in_dim(full_g, chip * LANES, LANES, axis=1).reshape(sh)
    for n in big_names:
        grads[n] = big_grads[n].reshape(w[n].shape)

    delta, new_m, new_v = {}, {}, {}
    for n in big_names:
        d2, m2, v2 = _adamw(f"adamw_{n}", shard2d(w, n), big_grads[n], shard2d(mom, n), shard2d(vel, n))
        delta[n], new_m[n], new_v[n] = (t.reshape(w[n].shape) for t in (d2, m2, v2))
    for tag, group, rows, shapes in (("repl", REPL_NAMES, REPL_ROWS, [w[n].shape for n in REPL_NAMES]),
                                     ("sharded", SHARDED_NAMES, SHARDED_SHARD_ROWS, SHARDED_SHARD_SHAPES)):
        packs = [_pack([t[n] for n in group], rows) for t in (w, grads, mom, vel)]
        outs = _adamw(f"adamw_{tag}", *packs)
        for res, o in zip((delta, new_m, new_v), outs):
            res.update(zip(group, _unpack(o, shapes, rows)))

    out = [loss, dx[None]]
    for res in (grads, delta, new_m, new_v):
        out.extend(res[n] for n in names)
    return tuple(out)
```

```python
import functools

import jax
import jax.numpy as jnp
from jax import lax
from jax.experimental import pallas as pl
from jax.experimental.pallas import tpu as pltpu

F32 = jnp.float32
BF16 = jnp.bfloat16

T = 2048
D = 1024
CONV_CH = 512
CONV_W = 31
HEAD_DIM = 64
ATT_W = 1536
EVEN_IN = 5632
ODD_IN = 2560
SCONV_W = 3
SG_GROUPS = 4
CHUNK = 128
D_FF = 4096
EPS = 1e-6
DILATIONS = (1, 4, 16)
BAND = 128
SCALE = HEAD_DIM ** -0.5
NEG = -1e30

ADAM_LR = 0.001
ADAM_B1 = 0.9
ADAM_B2 = 0.999
ADAM_EPS = 1e-08
ADAM_WD = 0.01
ADAM_STEP = 10

V7X_VMEM_BYTES = 64 * 2 ** 20
VMEM_LIMIT = V7X_VMEM_BYTES - 8 * 2 ** 20
LANES = 128
TOKEN_SHAPE = (8, LANES)


def _pcall(body, **kw):
    return pl.pallas_call(body, **kw)


def _params(*sem):
    return pltpu.CompilerParams(dimension_semantics=sem, vmem_limit_bytes=VMEM_LIMIT)


def _dot(a, b, dims):
    return lax.dot_general(a, b, (dims, ((), ())), preferred_element_type=F32)


def _nn(a, b):
    return _dot(a, b, ((1,), (0,)))


def _nt(a, b):
    return _dot(a, b, ((1,), (1,)))


def _tn(a, b):
    return _dot(a, b, ((0,), (0,)))


def _sigmoid(x):
    return 1.0 / (1.0 + jnp.exp(-x))


MM_VMEM_BUDGET = 40 * 2 ** 20


def _mm_tiles(mode, m, n, k, a_bytes, b_bytes, extra_bytes, out_bytes):
    def divisors(total, unit):
        return [t for t in range(unit, total + 1, unit) if total % t == 0]

    best = None
    for tm in divisors(m, LANES if mode == "tn" else 8):
        for tn in divisors(n, LANES):
            blocks = tm * k * a_bytes + tn * k * b_bytes + tm * tn * (extra_bytes + out_bytes)
            casts = (tm * k * 2 if a_bytes == 4 else 0) + (tn * k * 2 if b_bytes == 4 else 0)
            if 2 * blocks + casts + tm * tn * 4 > MM_VMEM_BUDGET:
                continue
            key = ((m // tm) * (n // tn), (m // tm) * n * k * b_bytes, abs(tm - tn))
            if best is None or key < best[0]:
                best = (key, tm, tn)
    return best[1], best[2]


def _mm(name, mode, a, b, m, n, k, out_dtypes, *, b_off=0, extras=(), epi=None, tie=None):
    tm, tn = _mm_tiles(mode, m, n, k, a.dtype.itemsize, b.dtype.itemsize, sum(e.dtype.itemsize for e in extras),
                       sum(jnp.dtype(dt).itemsize for dt in out_dtypes))
    assert b_off % tn == 0
    b_off //= tn
    if mode == "nn":
        a_spec = pl.BlockSpec((tm, k), lambda i, j: (i, 0))
        b_spec = pl.BlockSpec((k, tn), lambda i, j: (0, j + b_off))
        dims = ((1,), (0,))
    elif mode == "nt":
        a_spec = pl.BlockSpec((tm, k), lambda i, j: (i, 0))
        b_spec = pl.BlockSpec((tn, k), lambda i, j: (j, 0))
        dims = ((1,), (1,))
    else:
        a_spec = pl.BlockSpec((k, tm), lambda i, j: (0, i))
        b_spec = pl.BlockSpec((k, tn), lambda i, j: (0, j))
        dims = ((0,), (0,))
    o_spec = pl.BlockSpec((tm, tn), lambda i, j: (i, j))
    n_extra = len(extras)
    ties = () if tie is None else (tie,)

    def body(a_ref, b_ref, *rest):
        rest = rest[len(ties):]
        acc = _dot(a_ref[...].astype(BF16), b_ref[...].astype(BF16), dims)
        vals = epi(acc, *[e[...] for e in rest[:n_extra]]) if epi is not None else (acc,)
        for o_ref, v in zip(rest[n_extra:], vals):
            o_ref[...] = v.astype(o_ref.dtype)

    outs = _pcall(
        body, name=name, grid=(m // tm, n // tn),
        in_specs=[a_spec, b_spec] + [pl.BlockSpec(TOKEN_SHAPE, lambda i, j: (0, 0))] * len(ties) + [o_spec] * n_extra,
        out_specs=[o_spec] * len(out_dtypes),
        out_shape=[jax.ShapeDtypeStruct((m, n), dt) for dt in out_dtypes],
        compiler_params=_params("parallel", "parallel"),
    )(a, b, *ties, *extras)
    return outs[0] if len(out_dtypes) == 1 else outs


def _rms_fwd(name, h, g, tm=512):
    def body(h_ref, g_ref, o_ref):
        x = h_ref[...]
        r = lax.rsqrt(jnp.mean(x * x, axis=-1, keepdims=True) + EPS)
        o_ref[...] = ((x * r) * g_ref[...]).astype(BF16)

    return _pcall(
        body, name=name, grid=(T // tm,),
        in_specs=[pl.BlockSpec((tm, D), lambda i: (i, 0)), pl.BlockSpec((1, D), lambda i: (0, 0))],
        out_specs=pl.BlockSpec((tm, D), lambda i: (i, 0)),
        out_shape=jax.ShapeDtypeStruct((T, D), BF16),
        compiler_params=_params("parallel"),
    )(h, g)


def _rms_bwd(name, h, dhn, g, dres, tm=512):
    def body(h_ref, d_ref, g_ref, r_ref, dh_ref, dg_ref):
        x = h_ref[...]
        r = lax.rsqrt(jnp.mean(x * x, axis=-1, keepdims=True) + EPS)
        nrm = x * r
        dy = d_ref[...]
        dn = dy * g_ref[...]
        dh_ref[...] = r_ref[...] + r * (dn - nrm * jnp.mean(dn * nrm, axis=-1, keepdims=True))

        @pl.when(pl.program_id(0) == 0)
        def _():
            dg_ref[...] = jnp.zeros_like(dg_ref)

        dg_ref[...] += jnp.sum(dy * nrm, axis=0, keepdims=True)

    row = pl.BlockSpec((tm, D), lambda i: (i, 0))
    vec = pl.BlockSpec((1, D), lambda i: (0, 0))
    return _pcall(
        body, name=name, grid=(T // tm,),
        in_specs=[row, row, vec, row], out_specs=[row, vec],
        out_shape=[jax.ShapeDtypeStruct((T, D), F32), jax.ShapeDtypeStruct((1, D), F32)],
        compiler_params=_params("arbitrary"),
    )(h, dhn, g, dres)


def _loss_head(h, g, target, tm=512):
    def body(h_ref, g_ref, t_ref, dh_ref, dg_ref, loss_ref):
        x = h_ref[...]
        r = lax.rsqrt(jnp.mean(x * x, axis=-1, keepdims=True) + EPS)
        nrm = x * r
        gain = g_ref[...]
        err = nrm * gain - t_ref[...]
        dy = err * (1.0 / D)
        dn = dy * gain
        dh_ref[...] = r * (dn - nrm * jnp.mean(dn * nrm, axis=-1, keepdims=True))

        @pl.when(pl.program_id(0) == 0)
        def _():
            dg_ref[...] = jnp.zeros_like(dg_ref)
            loss_ref[...] = jnp.zeros_like(loss_ref)

        dg_ref[...] += jnp.sum(dy * nrm, axis=0, keepdims=True)
        part = jnp.sum(jnp.sum(err * err, axis=1, keepdims=True), axis=0, keepdims=True) * (0.5 / D)
        loss_ref[...] += jnp.broadcast_to(part, (1, LANES))

    row = pl.BlockSpec((tm, D), lambda i: (i, 0))
    vec = pl.BlockSpec((1, D), lambda i: (0, 0))
    return _pcall(
        body, name="loss_head", grid=(T // tm,),
        in_specs=[row, vec, row], out_specs=[row, vec, pl.BlockSpec((1, LANES), lambda i: (0, 0))],
        out_shape=[jax.ShapeDtypeStruct((T, D), F32), jax.ShapeDtypeStruct((1, D), F32),
                   jax.ShapeDtypeStruct((1, LANES), F32)],
        compiler_params=_params("arbitrary"),
    )(h, g, target)


CONV_TILE = 256
CONV_HALO = 32


def _glu(z):
    return z[:, :CONV_CH] * _sigmoid(z[:, CONV_CH:])


def _econv_fwd(zc, conv_k, conv_b, ln_g, ln_b):
    R, H = CONV_TILE, CONV_HALO

    def body(z_ref, zh_ref, k_ref, b_ref, g_ref, be_ref, cv_ref, cat_ref):
        i = pl.program_id(0)
        glu = _glu(z_ref[...])
        halo = _glu(zh_ref[...]) * (i > 0).astype(F32)
        win = jnp.concatenate([halo, glu], axis=0)
        acc = jnp.zeros((R, CONV_CH), F32) + b_ref[...]
        for j in range(CONV_W):
            off = H - (CONV_W - 1) + j
            acc = acc + k_ref[j:j + 1, :] * win[off:off + R, :]
        cv_ref[...] = acc
        mu = jnp.mean(acc, axis=-1, keepdims=True)
        xc = acc - mu
        rstd = lax.rsqrt(jnp.mean(xc * xc, axis=-1, keepdims=True) + EPS)
        ln = xc * rstd * g_ref[...] + be_ref[...]
        cat_ref[...] = (ln * _sigmoid(ln)).astype(BF16)

    vec = pl.BlockSpec((1, CONV_CH), lambda i: (0, 0))
    return _pcall(
        body, name="econv_fwd", grid=(T // R,),
        in_specs=[pl.BlockSpec((R, 2 * CONV_CH), lambda i: (i, 0)),
                  pl.BlockSpec((H, 2 * CONV_CH), lambda i: (jnp.maximum(i * (R // H) - 1, 0), 0)),
                  pl.BlockSpec((CONV_W, CONV_CH), lambda i: (0, 0)), vec, vec, vec],
        out_specs=[pl.BlockSpec((R, CONV_CH), lambda i: (i, 0)), pl.BlockSpec((R, CONV_CH), lambda i: (i, 0))],
        out_shape=[jax.ShapeDtypeStruct((T, CONV_CH), F32), jax.ShapeDtypeStruct((T, D), BF16)],
        compiler_params=_params("parallel"),
    )(zc, zc, conv_k, conv_b, ln_g, ln_b)


def _econv_bwd_ln(cv, dcat, ln_g, ln_b):
    R = CONV_TILE

    def body(cv_ref, d_ref, g_ref, be_ref, dcv_ref, dg_ref, dbe_ref, dcb_ref):
        cv_t = cv_ref[...]
        mu = jnp.mean(cv_t, axis=-1, keepdims=True)
        xc = cv_t - mu
        rstd = lax.rsqrt(jnp.mean(xc * xc, axis=-1, keepdims=True) + EPS)
        xh = xc * rstd
        ln = xh * g_ref[...] + be_ref[...]
        sg = _sigmoid(ln)
        dln = d_ref[...] * (sg * (1.0 + ln * (1.0 - sg)))
        dxh = dln * g_ref[...]
        dcv = rstd * (dxh - jnp.mean(dxh, axis=-1, keepdims=True) - xh * jnp.mean(dxh * xh, axis=-1, keepdims=True))
        dcv_ref[...] = dcv

        @pl.when(pl.program_id(0) == 0)
        def _():
            dg_ref[...] = jnp.zeros_like(dg_ref)
            dbe_ref[...] = jnp.zeros_like(dbe_ref)
            dcb_ref[...] = jnp.zeros_like(dcb_ref)

        dg_ref[...] += jnp.sum(dln * xh, axis=0, keepdims=True)
        dbe_ref[...] += jnp.sum(dln, axis=0, keepdims=True)
        dcb_ref[...] += jnp.sum(dcv, axis=0, keepdims=True)

    vec = pl.BlockSpec((1, CONV_CH), lambda i: (0, 0))
    row = pl.BlockSpec((R, CONV_CH), lambda i: (i, 0))
    vshape = jax.ShapeDtypeStruct((1, CONV_CH), F32)
    return _pcall(
        body, name="econv_bwd_ln", grid=(T // R,),
        in_specs=[row, row, vec, vec], out_specs=[row, vec, vec, vec],
        out_shape=[jax.ShapeDtypeStruct((T, CONV_CH), F32), vshape, vshape, vshape],
        compiler_params=_params("arbitrary"),
    )(cv, dcat, ln_g, ln_b)


def _econv_bwd_conv(dcv, zc, conv_k):
    R, H = CONV_TILE, CONV_HALO
    last = T // R - 1

    def body(d_ref, dn_ref, z_ref, zh_ref, k_ref, dz_ref, dk_ref):
        i = pl.program_id(0)
        z = z_ref[...]
        a_lin = z[:, :CONV_CH]
        sg = _sigmoid(z[:, CONV_CH:])
        glu = a_lin * sg
        halo = _glu(zh_ref[...]) * (i > 0).astype(F32)
        win = jnp.concatenate([halo, glu], axis=0)
        dcv_t = d_ref[...]
        nxt = dn_ref[...] * (i < last).astype(F32)
        winb = jnp.concatenate([dcv_t, nxt], axis=0)

        @pl.when(i == 0)
        def _():
            dk_ref[...] = jnp.zeros_like(dk_ref)

        dglu = jnp.zeros((R, CONV_CH), F32)
        for j in range(CONV_W):
            off = H - (CONV_W - 1) + j
            dk_ref[j:j + 1, :] += jnp.sum(dcv_t * win[off:off + R, :], axis=0, keepdims=True)
            ob = CONV_W - 1 - j
            dglu = dglu + k_ref[j:j + 1, :] * winb[ob:ob + R, :]
        dz_ref[...] = jnp.concatenate([dglu * sg, dglu * a_lin * sg * (1.0 - sg)], axis=1).astype(BF16)

    return _pcall(
        body, name="econv_bwd_conv", grid=(T // R,),
        in_specs=[pl.BlockSpec((R, CONV_CH), lambda i: (i, 0)),
                  pl.BlockSpec((H, CONV_CH), lambda i: (jnp.minimum((i + 1) * (R // H), T // H - 1), 0)),
                  pl.BlockSpec((R, 2 * CONV_CH), lambda i: (i, 0)),
                  pl.BlockSpec((H, 2 * CONV_CH), lambda i: (jnp.maximum(i * (R // H) - 1, 0), 0)),
                  pl.BlockSpec((CONV_W, CONV_CH), lambda i: (0, 0))],
        out_specs=[pl.BlockSpec((R, 2 * CONV_CH), lambda i: (i, 0)), pl.BlockSpec((CONV_W, CONV_CH), lambda i: (0, 0))],
        out_shape=[jax.ShapeDtypeStruct((T, EVEN_IN), BF16), jax.ShapeDtypeStruct((CONV_W, CONV_CH), F32)],
        compiler_params=_params("arbitrary"),
    )(dcv, dcv, zc, zc, conv_k)


def _swap_halves(v):
    lane = lax.broadcasted_iota(jnp.int32, v.shape, 1)
    return jnp.where((lane % HEAD_DIM) < HEAD_DIM // 2, pltpu.roll(v, LANES - HEAD_DIM // 2, 1),
                     pltpu.roll(v, HEAD_DIM // 2, 1))


def _qkv_proj(hn, w_in, rope_c, rope_s, tm=T):
    tn = 4 * LANES

    def body(a_ref, b_ref, c_ref, s_ref, o_ref):
        j = pl.program_id(1)
        acc = _nn(a_ref[...], b_ref[...])
        for p in range(4):
            v = acc[:, p * LANES:(p + 1) * LANES]
            rot = v * c_ref[...] + _swap_halves(v) * s_ref[...]
            o_ref[p] = jnp.where(j < 6, rot, v)

    tab = pl.BlockSpec((tm, LANES), lambda i, j: (i, 0))
    return _pcall(
        body, name="qkv_proj", grid=(T // tm, 9),
        in_specs=[pl.BlockSpec((tm, D), lambda i, j: (i, 0)),
                  pl.BlockSpec((D, tn), lambda i, j: (0, j + (2 * CONV_CH) // tn)), tab, tab],
        out_specs=pl.BlockSpec((None, 4, tm, LANES), lambda i, j: (j, 0, i, 0)),
        out_shape=jax.ShapeDtypeStruct((9, 4, T, LANES), F32),
        compiler_params=_params("parallel", "parallel"),
    )(hn, w_in, rope_c, rope_s)


ATTN_FWD_UNROLL = 4
ATTN_BWD_UNROLL = 4


def _band_rows(start, d):
    if d == 1:
        return pl.ds(pl.multiple_of(start, BAND), BAND)
    return pl.ds(start, BAND, stride=d)


def _band_masks(n):
    row = lax.broadcasted_iota(jnp.int32, (BAND, BAND), 0)
    col = lax.broadcasted_iota(jnp.int32, (BAND, BAND), 1)
    no_prev = (n == 0).astype(jnp.int32) * (2 * BAND)
    return col <= row, col >= row + no_prev


def _attn_fwd(qkv, g):
    d = DILATIONS[g]
    nb = T // d // BAND

    def body(q_ref, k_ref, v_ref, o_ref, l_ref):
        lane_lo = lax.broadcasted_iota(jnp.int32, (BAND, LANES), 1) < HEAD_DIM

        heads = (lane_lo, jnp.logical_not(lane_lo))
        ones = jnp.ones((BAND, LANES), BF16)

        def step(it, carry):
            tiles = []
            for u in range(ATTN_FWD_UNROLL):
                idx = it * ATTN_FWD_UNROLL + u
                r = idx // nb
                n = idx % nb
                cur = _band_rows(n * (BAND * d) + r, d)
                prev = _band_rows(jnp.maximum(n - 1, 0) * (BAND * d) + r, d)
                mc, mp = _band_masks(n)
                tiles.append((cur, mc, mp, q_ref[cur, :], k_ref[cur, :].astype(BF16), v_ref[cur, :].astype(BF16),
                              k_ref[prev, :].astype(BF16), v_ref[prev, :].astype(BF16)))
            scores = []
            for cur, mc, mp, q, kc, vc, kp, vp in tiles:
                for hm in heads:
                    qm = jnp.where(hm, q, 0.0).astype(BF16)
                    scores.append((jnp.where(mc, _nt(qm, kc) * SCALE, NEG), jnp.where(mp, _nt(qm, kp) * SCALE, NEG)))
            maxes = [jnp.maximum(jnp.max(sc, axis=1, keepdims=True), jnp.max(sp, axis=1, keepdims=True))
                     for sc, sp in scores]
            probs = [(jnp.exp(sc - mx).astype(BF16), jnp.exp(sp - mx).astype(BF16))
                     for (sc, sp), mx in zip(scores, maxes)]
            dens = [_nn(pc, ones) + _nn(pp, ones) for pc, pp in probs]
            for t, (cur, mc, mp, q, kc, vc, kp, vp) in enumerate(tiles):
                outs, lses = [], []
                for h in range(2):
                    pc, pp = probs[2 * t + h]
                    outs.append((_nn(pc, vc) + _nn(pp, vp)) / dens[2 * t + h])
                    lses.append(maxes[2 * t + h] + jnp.log(dens[2 * t + h]))
                o_ref[cur, :] = jnp.where(lane_lo, outs[0], outs[1])
                l_ref[cur, :] = jnp.where(lane_lo, lses[0], lses[1])
            return carry

        lax.fori_loop(0, d * nb // ATTN_FWD_UNROLL, step, 0)

    def slab(which):
        return pl.BlockSpec((None, None, T, LANES), lambda p: (which * 3 + g, p, 0, 0))

    out = pl.BlockSpec((None, T, LANES), lambda p: (p, 0, 0))
    shape = jax.ShapeDtypeStruct((4, T, LANES), F32)
    return _pcall(
        body, name=f"attn_fwd{g}", grid=(4,),
        in_specs=[slab(0), slab(1), slab(2)], out_specs=[out, out], out_shape=[shape, shape],
        compiler_params=_params("parallel"),
    )(qkv, qkv, qkv)


def _attn_merge(outs, lses, cat, tm=1024):
    def body(o0, o1, o2, l0, l1, l2, cat_in, cat_ref, att_ref, w0, w1, w2):
        del cat_in
        la, lb, lc = l0[...], l1[...], l2[...]
        mx = jnp.maximum(jnp.maximum(la, lb), lc)
        ea, eb, ec = jnp.exp(la - mx), jnp.exp(lb - mx), jnp.exp(lc - mx)
        inv = 1.0 / (ea + eb + ec)
        wa, wb, wc = ea * inv, eb * inv, ec * inv
        att = wa * o0[...] + wb * o1[...] + wc * o2[...]
        att_ref[...] = att
        cat_ref[...] = att.astype(BF16)
        w0[...] = wa
        w1[...] = wb
        w2[...] = wc

    slab = pl.BlockSpec((None, tm, LANES), lambda p, i: (p, i, 0))
    shape = jax.ShapeDtypeStruct((4, T, LANES), F32)
    return _pcall(
        body, name="attn_merge", grid=(4, T // tm),
        in_specs=[slab] * 6 + [pl.BlockSpec(memory_space=pl.ANY)],
        out_specs=[pl.BlockSpec((tm, LANES), lambda p, i: (i, CONV_CH // LANES + p)), slab, slab, slab, slab],
        out_shape=[jax.ShapeDtypeStruct((T, D), BF16), shape, shape, shape, shape],
        input_output_aliases={6: 0},
        compiler_params=_params("parallel", "parallel"),
    )(*outs, *lses, cat)


def _attn_bwd(qkv, lse, wgt, att, dcat, dqkv, g):
    d = DILATIONS[g]
    nb = T // d // BAND

    def body(q_ref, k_ref, v_ref, l_ref, w_ref, a_ref, da_ref, dq_in, o_ref):
        del dq_in
        lane = lax.broadcasted_iota(jnp.int32, (BAND, LANES), 1)
        lane_lo = lane < HEAD_DIM
        row = lax.broadcasted_iota(jnp.int32, (LANES, LANES), 0)
        same_head = ((row // HEAD_DIM) == (lane // HEAD_DIM)).astype(BF16)
        dq_ref, dk_ref, dv_ref = o_ref.at[0], o_ref.at[1], o_ref.at[2]
        dk_ref[...] = jnp.zeros((T, LANES), F32)
        dv_ref[...] = jnp.zeros((T, LANES), F32)

        heads = (lane_lo, jnp.logical_not(lane_lo))

        def step(it, carry):
            tiles = []
            for u in range(ATTN_BWD_UNROLL):
                idx = it * ATTN_BWD_UNROLL + u
                r = idx // nb
                n = idx % nb
                cur = _band_rows(n * (BAND * d) + r, d)
                prev = _band_rows(jnp.maximum(n - 1, 0) * (BAND * d) + r, d)
                mc, mp = _band_masks(n)
                da = da_ref[cur, :]
                prod = da * a_ref[cur, :]
                hi = prod.astype(BF16)
                lo = (prod - hi.astype(F32)).astype(BF16)
                tiles.append(dict(cur=cur, prev=prev, mc=mc, mp=mp, da=da, hi=hi, lo=lo, q=q_ref[cur, :],
                                  kc=k_ref[cur, :].astype(BF16), vc=v_ref[cur, :].astype(BF16),
                                  kp=k_ref[prev, :].astype(BF16), vp=v_ref[prev, :].astype(BF16),
                                  lse=l_ref[cur, :], w=w_ref[cur, :]))
            for t in tiles:
                t["csum"] = _nn(t["hi"], same_head) + _nn(t["lo"], same_head)
            chains = []
            for t in tiles:
                for h, hm in enumerate(heads):
                    qm = jnp.where(hm, t["q"], 0.0).astype(BF16)
                    dam = jnp.where(hm, t["da"], 0.0).astype(BF16)
                    chains.append(dict(t=t, h=h, qm=qm, dam=dam,
                                       sc=jnp.where(t["mc"], _nt(qm, t["kc"]) * SCALE, NEG),
                                       sp=jnp.where(t["mp"], _nt(qm, t["kp"]) * SCALE, NEG),
                                       dpc=_nt(dam, t["vc"]), dpp=_nt(dam, t["vp"])))
            for ch in chains:
                t, col0 = ch["t"], ch["h"] * HEAD_DIM
                lse_h = t["lse"][:, col0:col0 + 1]
                w_h = t["w"][:, col0:col0 + 1]
                c_h = t["csum"][:, col0:col0 + 1]
                pwc = w_h * jnp.exp(ch["sc"] - lse_h)
                pwp = w_h * jnp.exp(ch["sp"] - lse_h)
                ch["dsc"] = (pwc * (ch["dpc"] - c_h) * SCALE).astype(BF16)
                ch["dsp"] = (pwp * (ch["dpp"] - c_h) * SCALE).astype(BF16)
                ch["pwc"] = pwc.astype(BF16)
                ch["pwp"] = pwp.astype(BF16)
            for ch in chains:
                t = ch["t"]
                ch["dq"] = _nn(ch["dsc"], t["kc"]) + _nn(ch["dsp"], t["kp"])
                ch["dkc"] = _tn(ch["dsc"], ch["qm"])
                ch["dkp"] = _tn(ch["dsp"], ch["qm"])
                ch["dvc"] = _tn(ch["pwc"], ch["dam"])
                ch["dvp"] = _tn(ch["pwp"], ch["dam"])
            for i, t in enumerate(tiles):
                c0, c1 = chains[2 * i], chains[2 * i + 1]
                dq_ref[t["cur"], :] = jnp.where(lane_lo, c0["dq"], c1["dq"])
                dk_ref[t["cur"], :] += c0["dkc"] + c1["dkc"]
                dk_ref[t["prev"], :] += c0["dkp"] + c1["dkp"]
                dv_ref[t["cur"], :] += c0["dvc"] + c1["dvc"]
                dv_ref[t["prev"], :] += c0["dvp"] + c1["dvp"]
            return carry

        lax.fori_loop(0, d * nb // ATTN_BWD_UNROLL, step, 0)

    def slab(which):
        return pl.BlockSpec((None, None, T, LANES), lambda p: (which * 3 + g, p, 0, 0))

    per_pair = pl.BlockSpec((None, T, LANES), lambda p: (p, 0, 0))
    return _pcall(
        body, name=f"attn_bwd{g}", grid=(4,),
        in_specs=[slab(0), slab(1), slab(2), per_pair, per_pair, per_pair,
                  pl.BlockSpec((T, LANES), lambda p: (0, CONV_CH // LANES + p)),
                  pl.BlockSpec(memory_space=pl.ANY)],
        out_specs=pl.BlockSpec((None, 3, None, T, LANES), lambda p: (g, 0, p, 0, 0)),
        out_shape=jax.ShapeDtypeStruct((3, 3, 4, T, LANES), F32),
        input_output_aliases={7: 0},
        compiler_params=_params("parallel"),
    )(qkv, qkv, qkv, lse, wgt, att, dcat, dqkv)


def _rope_bwd(dqkv, rope_c, rope_s, dz):
    wide = 4 * LANES

    def body(d_ref, c_ref, s_ref, dz_in, o_ref):
        del dz_in
        w = pl.program_id(1)
        for p in range(4):
            v = d_ref[p]
            rot = v * c_ref[...] + _swap_halves(v * s_ref[...])
            o_ref[:, p * LANES:(p + 1) * LANES] = jnp.where(w < 2, rot, v).astype(BF16)

    tab = pl.BlockSpec((T, LANES), lambda g, w: (0, 0))
    return _pcall(
        body, name="rope_bwd", grid=(3, 3),
        in_specs=[pl.BlockSpec((None, None, 4, T, LANES), lambda g, w: (g, w, 0, 0, 0)), tab, tab,
                  pl.BlockSpec(memory_space=pl.ANY)],
        out_specs=pl.BlockSpec((T, wide), lambda g, w: (0, (2 * CONV_CH) // wide + w * 3 + g)),
        out_shape=jax.ShapeDtypeStruct((T, EVEN_IN), BF16),
        input_output_aliases={3: 0},
        compiler_params=_params("parallel", "parallel"),
    )(dqkv, rope_c, rope_s, dz)


ODD_TILE = 256
ODD_HALO = 8
GELU_C = 0.7978845608028654
GELU_A = 0.044715


def _gelu(x):
    return 0.5 * x * (1.0 + jnp.tanh(GELU_C * (x + GELU_A * x * x * x)))


def _gelu_grad(x):
    th = jnp.tanh(GELU_C * (x + GELU_A * x * x * x))
    return 0.5 * (1.0 + th) + 0.5 * x * (1.0 - th * th) * GELU_C * (1.0 + 3.0 * GELU_A * x * x)


def _tril():
    row = lax.broadcasted_iota(jnp.int32, (CHUNK, CHUNK), 0)
    col = lax.broadcasted_iota(jnp.int32, (CHUNK, CHUNK), 1)
    return (col <= row).astype(F32)


def _odd_parts(z, zh, i, k_ref, g_ref, be_ref, w_ref, bt_ref):
    R, H = ODD_TILE, ODD_HALO
    gb, gc, xs, uv = z[:, :512], z[:, 512:1024], z[:, 1024:1536], z[:, 1536:]
    halo = zh[:, 512:1024] * zh[:, 1024:1536] * (i > 0).astype(F32)
    win = jnp.concatenate([halo, gc * xs], axis=0)
    cv = jnp.zeros((R, 512), F32)
    for j in range(SCONV_W):
        off = H - (SCONV_W - 1) + j
        cv = cv + k_ref[j:j + 1, :] * win[off:off + R, :]
    ge = _gelu(uv)
    u, v = ge[:, :512], ge[:, 512:]
    mu = jnp.mean(v, axis=-1, keepdims=True)
    xc = v - mu
    rstd = lax.rsqrt(jnp.mean(xc * xc, axis=-1, keepdims=True) + EPS)
    xh = xc * rstd
    vn = xh * g_ref[...] + be_ref[...]
    tril = _tril()
    wms = [(w_ref[g] * tril).astype(BF16) for g in range(SG_GROUPS)]
    rows = []
    for ci in range(R // CHUNK):
        blocks = []
        for g in range(SG_GROUPS):
            blk = vn[ci * CHUNK:(ci + 1) * CHUNK, g * LANES:(g + 1) * LANES].astype(BF16)
            blocks.append(_nn(wms[g], blk) + bt_ref[:, g:g + 1])
        rows.append(jnp.concatenate(blocks, axis=1))
    vmix = jnp.concatenate(rows, axis=0)
    return gb, gc, xs, uv, win, cv, u, rstd, xh, vn, vmix, wms


def _odd_mid_fwd(z, conv_k, ln_g, ln_b, sg_w, sg_bt):
    R, H = ODD_TILE, ODD_HALO

    def body(z_ref, zh_ref, k_ref, g_ref, be_ref, w_ref, bt_ref, o_ref):
        i = pl.program_id(0)
        gb, _, _, _, _, cv, u, _, _, _, vmix, _ = _odd_parts(z_ref[...], zh_ref[...], i, k_ref, g_ref, be_ref, w_ref, bt_ref)
        o_ref[...] = jnp.concatenate([gb * cv, u * vmix], axis=1).astype(BF16)

    vec = pl.BlockSpec((1, 512), lambda i: (0, 0))
    return _pcall(
        body, name="odd_mid_fwd", grid=(T // R,),
        in_specs=[pl.BlockSpec((R, ODD_IN), lambda i: (i, 0)),
                  pl.BlockSpec((H, ODD_IN), lambda i: (jnp.maximum(i * (R // H) - 1, 0), 0)),
                  pl.BlockSpec((SCONV_W, 512), lambda i: (0, 0)), vec, vec,
                  pl.BlockSpec((SG_GROUPS, CHUNK, CHUNK), lambda i: (0, 0, 0)),
                  pl.BlockSpec((CHUNK, SG_GROUPS), lambda i: (0, 0))],
        out_specs=pl.BlockSpec((R, D), lambda i: (i, 0)),
        out_shape=jax.ShapeDtypeStruct((T, D), BF16),
        compiler_params=_params("parallel"),
    )(z, z, conv_k, ln_g, ln_b, sg_w, sg_bt)


def _odd_mid_bwd(z, dcat, conv_k, ln_g, ln_b, sg_w, sg_bt):
    R, H = ODD_TILE, ODD_HALO
    last = T // R - 1

    def body(z_ref, zh_ref, zn_ref, d_ref, dn_ref, k_ref, g_ref, be_ref, w_ref, bt_ref,
             dz_ref, dk_ref, dg_ref, dbe_ref, dw_ref, dbt_ref):
        i = pl.program_id(0)
        z = z_ref[...]
        gb, gc, xs, uv, win, cv, u, rstd, xh, vn, vmix, wms = _odd_parts(z, zh_ref[...], i, k_ref, g_ref, be_ref, w_ref, bt_ref)
        dcat_t = d_ref[...]
        dc, dd = dcat_t[:, :512], dcat_t[:, 512:]

        @pl.when(i == 0)
        def _():
            dk_ref[...] = jnp.zeros_like(dk_ref)
            dg_ref[...] = jnp.zeros_like(dg_ref)
            dbe_ref[...] = jnp.zeros_like(dbe_ref)
            dw_ref[...] = jnp.zeros_like(dw_ref)
            dbt_ref[...] = jnp.zeros_like(dbt_ref)

        dgb = dc * cv
        dcv = dc * gb
        nxt = dn_ref[:, :512] * zn_ref[:, :512] * (i < last).astype(F32)
        winb = jnp.concatenate([dcv, nxt], axis=0)
        dp = jnp.zeros((R, 512), F32)
        for j in range(SCONV_W):
            off = H - (SCONV_W - 1) + j
            dk_ref[j:j + 1, :] += jnp.sum(dcv * win[off:off + R, :], axis=0, keepdims=True)
            ob = SCONV_W - 1 - j
            dp = dp + k_ref[j:j + 1, :] * winb[ob:ob + R, :]
        dgc = dp * xs
        dxs = dp * gc
        du = dd * vmix
        dvmix = dd * u
        tril = _tril()
        rows = []
        for ci in range(R // CHUNK):
            blocks = []
            for g in range(SG_GROUPS):
                sl = (slice(ci * CHUNK, (ci + 1) * CHUNK), slice(g * LANES, (g + 1) * LANES))
                dblk = dvmix[sl]
                dblk16 = dblk.astype(BF16)
                blocks.append(_tn(wms[g], dblk16))
                dw_ref[g] += _nt(dblk16, vn[sl].astype(BF16)) * tril
                dbt_ref[:, g:g + 1] += jnp.sum(dblk, axis=1, keepdims=True)
            rows.append(jnp.concatenate(blocks, axis=1))
        dvn = jnp.concatenate(rows, axis=0)
        dg_ref[...] += jnp.sum(dvn * xh, axis=0, keepdims=True)
        dbe_ref[...] += jnp.sum(dvn, axis=0, keepdims=True)
        dxh = dvn * g_ref[...]
        dv = rstd * (dxh - jnp.mean(dxh, axis=-1, keepdims=True) - xh * jnp.mean(dxh * xh, axis=-1, keepdims=True))
        duv = jnp.concatenate([du, dv], axis=1) * _gelu_grad(uv)
        dz_ref[...] = jnp.concatenate([dgb, dgc, dxs, duv], axis=1).astype(BF16)

    vec = pl.BlockSpec((1, 512), lambda i: (0, 0))
    kspec = pl.BlockSpec((SCONV_W, 512), lambda i: (0, 0))
    wspec = pl.BlockSpec((SG_GROUPS, CHUNK, CHUNK), lambda i: (0, 0, 0))
    bspec = pl.BlockSpec((CHUNK, SG_GROUPS), lambda i: (0, 0))
    nxt_blk = lambda i: (jnp.minimum((i + 1) * (R // H), T // H - 1), 0)
    return _pcall(
        body, name="odd_mid_bwd", grid=(T // R,),
        in_specs=[pl.BlockSpec((R, ODD_IN), lambda i: (i, 0)),
                  pl.BlockSpec((H, ODD_IN), lambda i: (jnp.maximum(i * (R // H) - 1, 0), 0)),
                  pl.BlockSpec((H, ODD_IN), nxt_blk),
                  pl.BlockSpec((R, D), lambda i: (i, 0)),
                  pl.BlockSpec((H, D), nxt_blk),
                  kspec, vec, vec, wspec, bspec],
        out_specs=[pl.BlockSpec((R, ODD_IN), lambda i: (i, 0)), kspec, vec, vec, wspec, bspec],
        out_shape=[jax.ShapeDtypeStruct((T, ODD_IN), BF16), jax.ShapeDtypeStruct((SCONV_W, 512), F32),
                   jax.ShapeDtypeStruct((1, 512), F32), jax.ShapeDtypeStruct((1, 512), F32),
                   jax.ShapeDtypeStruct((SG_GROUPS, CHUNK, CHUNK), F32), jax.ShapeDtypeStruct((CHUNK, SG_GROUPS), F32)],
        compiler_params=_params("arbitrary"),
    )(z, z, z, dcat, dcat, conv_k, ln_g, ln_b, sg_w, sg_bt)


def _ffn_fwd(tag, h, g, weight):
    hn = _rms_fwd(f"ffn{tag}_norm", h, g)

    def act(acc):
        r = jnp.maximum(acc, 0.0)
        return (r * r,)

    f = _mm(f"ffn{tag}_up", "nn", hn, weight(f"ffn_w1_{tag}", hn), T, D_FF, D, (BF16,), epi=act)
    out = _mm(f"ffn{tag}_down", "nn", f, weight(f"ffn_w2_{tag}", f), T, D, D_FF, (F32,),
              epi=lambda acc, res: (acc + res,), extras=(h,))
    return out, (hn, f)


def _ffn_bwd(tag, h, g, weight, emit, saved, dout):
    hn, f = saved
    du = _mm(f"ffn{tag}_dact", "nt", dout, weight(f"ffn_w2_{tag}", dout), T, D_FF, D, (BF16,),
             epi=lambda acc, ff: (acc * (2.0 * jnp.sqrt(ff.astype(F32))),), extras=(f,))
    tok = emit(f"ffn_w2_{tag}", _mm(f"ffn{tag}_dw2", "tn", f, dout, D_FF, D, T, (BF16,)))
    tok = emit(f"ffn_w1_{tag}", _mm(f"ffn{tag}_dw1", "tn", hn, du, D, D_FF, T, (BF16,), tie=tok))
    dhn = _mm(f"ffn{tag}_dhn", "nt", du, weight(f"ffn_w1_{tag}", du), T, D, D_FF, (F32,), tie=tok)
    return _rms_bwd(f"ffn{tag}_dnorm", h, dhn, g, dout)


def _rope_tables():
    half = HEAD_DIM // 2
    inv = 10000.0 ** (-jnp.arange(half, dtype=F32) / half)
    ang = jnp.arange(T, dtype=F32)[:, None] * inv[None, :]
    cos, sin = jnp.cos(ang), jnp.sin(ang)
    c = jnp.tile(jnp.concatenate([cos, cos], axis=1), (1, LANES // HEAD_DIM))
    s = jnp.tile(jnp.concatenate([-sin, sin], axis=1), (1, LANES // HEAD_DIM))
    return c, s


def _local_step(x, target, p, weight, emit):
    rope_c, rope_s = _rope_tables()
    grads = {}
    residual = lambda acc, res: (acc + res,)

    hn0 = _rms_fwd("mix0_norm", x, p["norm_mix_g0"])
    zc = _mm("even_in_conv", "nn", hn0, weight("even_w_in", hn0), T, 2 * CONV_CH, D, (F32,))
    qkv = _qkv_proj(hn0, weight("even_w_in", hn0), rope_c, rope_s)
    cv, cat0 = _econv_fwd(zc, p["even_conv_k"], p["even_conv_b"], p["even_ln_g"], p["even_ln_b"])
    att_parts = [_attn_fwd(qkv, g) for g in range(3)]
    outs = [a[0] for a in att_parts]
    lses = [a[1] for a in att_parts]
    cat0, att, w0, w1, w2 = _attn_merge(outs, lses, cat0)
    wgts = (w0, w1, w2)
    h1 = _mm("even_out", "nn", cat0, weight("even_w_out", cat0), T, D, D, (F32,), epi=residual, extras=(x,))
    h2, ffn0_saved = _ffn_fwd(0, h1, p["norm_ffn_g0"], weight)

    hn1 = _rms_fwd("mix1_norm", h2, p["norm_mix_g1"])
    z1 = _mm("odd_in", "nn", hn1, weight("odd_w_in", hn1), T, ODD_IN, D, (F32,))
    cat1 = _odd_mid_fwd(z1, p["odd_conv_k"], p["odd_ln_g"], p["odd_ln_b"], p["odd_sg_w"], p["odd_sg_bt"])
    h3 = _mm("odd_out", "nn", cat1, weight("odd_w_out", cat1), T, D, D, (F32,), epi=residual, extras=(h2,))
    h4, ffn1_saved = _ffn_fwd(1, h3, p["norm_ffn_g1"], weight)

    dh4, grads["final_g"], loss = _loss_head(h4, p["final_g"], target)

    dh3, grads["norm_ffn_g1"] = _ffn_bwd(1, h3, p["norm_ffn_g1"], weight, emit, ffn1_saved, dh4)
    tok = emit("odd_w_out", _mm("odd_out_dw", "tn", cat1, dh3, D, D, T, (BF16,)))
    dcat1 = _mm("odd_out_dx", "nt", dh3, weight("odd_w_out", dh3), T, D, D, (F32,), tie=tok)
    dz1, grads["odd_conv_k"], grads["odd_ln_g"], grads["odd_ln_b"], grads["odd_sg_w"], grads["odd_sg_bt"] = _odd_mid_bwd(
        z1, dcat1, p["odd_conv_k"], p["odd_ln_g"], p["odd_ln_b"], p["odd_sg_w"], p["odd_sg_bt"])
    tok = emit("odd_w_in", _mm("odd_in_dw", "tn", hn1, dz1, D, ODD_IN, T, (BF16,)))
    dhn1 = _mm("odd_in_dx", "nt", dz1, weight("odd_w_in", dz1), T, D, ODD_IN, (F32,), tie=tok)
    dh2, grads["norm_mix_g1"] = _rms_bwd("mix1_dnorm", h2, dhn1, p["norm_mix_g1"], dh3)

    dh1, grads["norm_ffn_g0"] = _ffn_bwd(0, h1, p["norm_ffn_g0"], weight, emit, ffn0_saved, dh2)
    tok = emit("even_w_out", _mm("even_out_dw", "tn", cat0, dh1, D, D, T, (BF16,)))
    dcat0 = _mm("even_out_dx", "nt", dh1, weight("even_w_out", dh1), T, D, D, (F32,), tie=tok)
    dcv, grads["even_ln_g"], grads["even_ln_b"], grads["even_conv_b"] = _econv_bwd_ln(
        cv, dcat0, p["even_ln_g"], p["even_ln_b"])
    dz0, grads["even_conv_k"] = _econv_bwd_conv(dcv, zc, p["even_conv_k"])
    dqkv = lax.empty((3, 3, 4, T, LANES), F32)
    for g in range(3):
        dqkv = _attn_bwd(qkv, lses[g], wgts[g], att, dcat0, dqkv, g)
    dz0 = _rope_bwd(dqkv, rope_c, rope_s, dz0)
    tok = emit("even_w_in", _mm("even_in_dw", "tn", hn0, dz0, D, EVEN_IN, T, (BF16,)))
    dhn0 = _mm("even_in_dx", "nt", dz0, weight("even_w_in", dz0), T, D, EVEN_IN, (F32,), tie=tok)
    dx, grads["norm_mix_g0"] = _rms_bwd("mix0_dnorm", x, dhn0, p["norm_mix_g0"], dh1)
    return loss, dx, grads


def _rowwise(name, fn, ins, out_dtypes, tm=256):
    rows, cols = ins[0].shape
    tm = tm if rows % tm == 0 else rows
    n_in = len(ins)

    def body(*refs):
        vals = fn(*[r[...] for r in refs[:n_in]])
        for o_ref, v in zip(refs[n_in:], vals):
            o_ref[...] = v.astype(o_ref.dtype)

    spec = pl.BlockSpec((tm, cols), lambda i: (i, 0))
    outs = _pcall(
        body, name=name, grid=(rows // tm,),
        in_specs=[spec] * n_in, out_specs=[spec] * len(out_dtypes),
        out_shape=[jax.ShapeDtypeStruct((rows, cols), dt) for dt in out_dtypes],
        compiler_params=_params("parallel"),
    )(*ins)
    return outs[0] if len(out_dtypes) == 1 else outs


def _adamw(name, w, g, m, v):
    c1 = 1.0 - ADAM_B1 ** ADAM_STEP
    c2 = 1.0 - ADAM_B2 ** ADAM_STEP

    def fn(w_t, g_t, m_t, v_t):
        m_new = ADAM_B1 * m_t + (1.0 - ADAM_B1) * g_t
        v_new = ADAM_B2 * v_t + (1.0 - ADAM_B2) * (g_t * g_t)
        delta = -ADAM_LR * ((m_new / c1) / (jnp.sqrt(v_new / c2) + ADAM_EPS) + ADAM_WD * w_t)
        return delta, m_new, v_new

    return _rowwise(name, fn, (w, g, m, v), (F32, F32, F32))


class _Piece:
    def __init__(self, name, rows, cols, axis, src, src_row0):
        self.name, self.rows, self.cols, self.axis = name, rows, cols, axis
        self.width = (cols if axis == 1 else rows) // 4
        self.src, self.src_row0 = src, src_row0

    @property
    def full_shape(self):
        return (self.rows, self.cols)

    @property
    def half_shape(self):
        return (self.rows // 2, self.cols) if self.axis == 1 else (self.rows, self.cols // 2)

    @property
    def shard_half_shape(self):
        return (self.rows // 2, self.width) if self.axis == 1 else (self.width, self.cols // 2)

    def shard_whole(self, ref):
        n = self.rows if self.axis == 1 else self.width
        return ref.at[pl.ds(self.src_row0, n), :]

    def shard_half(self, ref, h):
        if self.axis == 1:
            return ref.at[pl.ds(self.src_row0 + h * (self.rows // 2), self.rows // 2), :]
        return ref.at[pl.ds(self.src_row0, self.width), pl.ds(h * (self.cols // 2), self.cols // 2)]

    def full_shard(self, ref, s):
        if self.axis == 1:
            return ref.at[:, pl.ds(s * self.width, self.width)]
        return ref.at[pl.ds(s * self.width, self.width), :]

    def full_shard_half(self, ref, s, h):
        if self.axis == 1:
            return ref.at[pl.ds(h * (self.rows // 2), self.rows // 2), pl.ds(s * self.width, self.width)]
        return ref.at[pl.ds(s * self.width, self.width), pl.ds(h * (self.cols // 2), self.cols // 2)]

    def full_half(self, ref, h):
        if self.axis == 1:
            return ref.at[pl.ds(h * (self.rows // 2), self.rows // 2), :]
        return ref.at[:, pl.ds(h * (self.cols // 2), self.cols // 2)]

    def full_half_rows(self, ref, h, r0, n):
        if self.axis == 1:
            return ref.at[pl.ds(h * (self.rows // 2) + r0, n), :]
        return ref.at[pl.ds(r0, n), pl.ds(h * (self.cols // 2), self.cols // 2)]

    def half_shard(self, ref, s):
        return self.full_shard(ref, s)


PIECES = (
    _Piece("even_w_in", D, EVEN_IN, 1, 0, 0),
    _Piece("even_w_out", D, D, 0, 1, 0),
    _Piece("ffn_w1_0", D, D_FF, 1, 4, 0),
    _Piece("ffn_w2_0", D_FF, D, 0, 5, 0),
    _Piece("odd_w_in", D, ODD_IN, 1, 2, 0),
    _Piece("odd_w_out", D, D, 0, 3, 0),
    _Piece("ffn_w1_1", D, D_FF, 1, 4, D),
    _Piece("ffn_w2_1", D_FF, D, 0, 5, D_FF // 4),
)
N_PIECES = len(PIECES)
FORWARD_GROUPS = ((0,), (1, 2, 3), (4, 5, 6, 7))
JOIN_GROUPS = ((0, 1, 2, 3), (4, 5))
HOLD_BACK = ("ffn_w2_0", "ffn_w2_1")
N_SHARD_OPERANDS = 6
ANY = pl.BlockSpec(memory_space=pl.ANY)
MESH = pl.DeviceIdType.MESH


def _mesh_place():
    x, y, c = lax.axis_index("x"), lax.axis_index("y"), lax.axis_index("c")
    chips = [(1 - x, y), (x, 1 - y), (1 - x, 1 - y)]
    return x, y, c, chips


def _remote(src, dst, send_sem, recv_sem, dev):
    return pltpu.make_async_remote_copy(src_ref=src, dst_ref=dst, send_sem=send_sem, recv_sem=recv_sem,
                                        device_id=dev, device_id_type=MESH)


HBM = pl.BlockSpec(memory_space=pltpu.HBM)
SEM = pl.BlockSpec(memory_space=pltpu.SEMAPHORE)
SPLIT_PARAMS = pltpu.CompilerParams(has_side_effects=pltpu.SideEffectType.DATAFLOW_SIDE_EFFECTING)
CAST_TILE = 256


def _in_hbm(a):
    return pltpu.with_memory_space_constraint(a, pltpu.HBM)


def _cast_place(pc, shard_operand, chip, tie=None):
    rows, cols = (pc.rows, pc.width) if pc.axis == 1 else (pc.width, pc.cols)
    nblk = rows // CAST_TILE
    blk0 = pc.src_row0 // CAST_TILE
    ties = () if tie is None else (tie,)

    def body(chip_ref, x_ref, *rest):
        del chip_ref
        rest[-1][...] = x_ref[...].astype(BF16)

    if pc.axis == 1:
        out_map = lambda i, chip_ref: (i, chip_ref[0])
    else:
        out_map = lambda i, chip_ref: (chip_ref[0] * nblk + i, 0)
    return _pcall(
        body, name=f"cast_{pc.name}",
        grid_spec=pltpu.PrefetchScalarGridSpec(
            num_scalar_prefetch=1, grid=(nblk,),
            in_specs=[pl.BlockSpec((CAST_TILE, cols), lambda i, chip_ref: (blk0 + i, 0))]
            + [pl.BlockSpec(TOKEN_SHAPE, lambda i, chip_ref: (0, 0))] * len(ties),
            out_specs=pl.BlockSpec((CAST_TILE, cols), out_map)),
        out_shape=jax.ShapeDtypeStruct(pc.full_shape, BF16),
        compiler_params=_params("parallel"),
    )(chip, shard_operand, *ties)


def _gather_start(name, pieces, fulls):
    n = len(pieces)

    def body(*refs):
        ins = refs[:n]
        sends = refs[2 * n:3 * n]
        recvs = refs[3 * n:4 * n]
        token = refs[4 * n]
        x, y, c, chips = _mesh_place()
        s = 2 * x + y
        for i, pc in enumerate(pieces):
            win = pc.full_shard_half(ins[i], s, c)
            for k, (cx, cy) in enumerate(chips):
                _remote(win, win, sends[i].at[k], recvs[i].at[k], (cx, cy, c)).start()
        token[...] = jnp.zeros(TOKEN_SHAPE, F32)

    sems = [pltpu.SemaphoreType.DMA((3,))] * (2 * n)
    outs = _pcall(
        body, name=name,
        in_specs=[HBM] * n,
        out_specs=[HBM] * n + [SEM] * (2 * n) + [pl.BlockSpec(memory_space=pltpu.VMEM)],
        out_shape=[pltpu.HBM(pc.full_shape, BF16) for pc in pieces] + sems + [jax.ShapeDtypeStruct(TOKEN_SHAPE, F32)],
        input_output_aliases={i: i for i in range(n)},
        compiler_params=SPLIT_PARAMS,
    )(*[_in_hbm(f) for f in fulls])
    return outs[:n], outs[n:2 * n], outs[2 * n:3 * n], outs[3 * n]


def _gather_wait(pc, full, send_sems, recv_sems, after):
    def body(full_ref, send_ref, recv_ref, after_ref, out_ref):
        del after_ref, out_ref
        x, y, c, chips = _mesh_place()
        for k, (cx, cy) in enumerate(chips):
            win = pc.full_shard_half(full_ref, 2 * cx + cy, c)
            cp = _remote(win, win, send_ref.at[k], recv_ref.at[k], (cx, cy, c))
            cp.wait_send()
            cp.wait_recv()

    return _pcall(
        body, name=f"gather_wait_{pc.name}",
        in_specs=[HBM, SEM, SEM, ANY], out_specs=HBM, out_shape=pltpu.HBM(pc.full_shape, BF16),
        input_output_aliases={0: 0}, compiler_params=SPLIT_PARAMS,
    )(full, send_sems, recv_sems, after)


def _core_forward(pieces, fulls):
    n = len(pieces)

    def body(*refs):
        ins, outs = refs[:n], refs[n:2 * n]
        send_bufs, recv_bufs = refs[2 * n:3 * n], refs[3 * n:4 * n]
        load_sems, send_sems, recv_sems, store_sems = refs[4 * n:]
        x, y, c, chips = _mesh_place()
        loads, sends, stores = [], [], []
        for i, pc in enumerate(pieces):
            for k, (cx, cy) in enumerate(chips):
                cp = pltpu.make_async_copy(pc.full_shard_half(ins[i], 2 * cx + cy, c), send_bufs[i].at[k],
                                           load_sems.at[3 * i + k])
                cp.start()
                loads.append(cp)
        for i in range(n):
            for k in range(3):
                j = 3 * i + k
                loads[j].wait()
                cp = _remote(send_bufs[i].at[k], recv_bufs[i].at[k], send_sems.at[j], recv_sems.at[j], (x, y, 1 - c))
                cp.start()
                sends.append(cp)
        for i, pc in enumerate(pieces):
            for k, (cx, cy) in enumerate(chips):
                j = 3 * i + k
                sends[j].wait_recv()
                cp = pltpu.make_async_copy(recv_bufs[i].at[k], pc.full_shard_half(outs[i], 2 * cx + cy, 1 - c),
                                           store_sems.at[j])
                cp.start()
                stores.append(cp)
        for j in range(3 * n):
            sends[j].wait_send()
            stores[j].wait()

    sems = pltpu.SemaphoreType.DMA((3 * n,))
    bufs = [pltpu.VMEM((3,) + pc.shard_half_shape, BF16) for pc in pieces]
    return _pcall(
        body, name="core_forward_" + pieces[0].name, in_specs=[ANY] * n, out_specs=[ANY] * n,
        out_shape=[jax.ShapeDtypeStruct(pc.full_shape, BF16) for pc in pieces],
        scratch_shapes=bufs + bufs + [sems, sems, sems, sems],
        input_output_aliases={i: i for i in range(n)},
        compiler_params=pltpu.CompilerParams(vmem_limit_bytes=VMEM_LIMIT),
    )(*fulls)


CHIPSUM_CHUNKS = 4


def _chipsum(pieces, partials):
    n = len(pieces)
    nch = CHIPSUM_CHUNKS

    def body(*refs):
        g_refs, out_refs = refs[:n], refs[n:2 * n]
        bufs = refs[2 * n:6 * n]
        load_sems, own_sems, send_sems, recv_sems, out_sems = refs[6 * n:]
        x, y, c, _ = _mesh_place()
        loads, owns, sends, stores, rows_of = [], [], [], [], []
        for i, pc in enumerate(pieces):
            send_buf, own_buf = bufs[4 * i], bufs[4 * i + 2]
            ch = pc.half_shape[0] // nch
            for k in range(nch):
                rows = pl.ds(k * ch, ch)
                rows_of.append(rows)
                cp = pltpu.make_async_copy(pc.full_half_rows(g_refs[i], 1 - c, k * ch, ch), send_buf.at[rows, :],
                                           load_sems.at[nch * i + k])
                cp.start()
                loads.append(cp)
                cp = pltpu.make_async_copy(pc.full_half_rows(g_refs[i], c, k * ch, ch), own_buf.at[rows, :],
                                           own_sems.at[nch * i + k])
                cp.start()
                owns.append(cp)
        for i in range(n):
            send_buf, recv_buf = bufs[4 * i], bufs[4 * i + 1]
            for k in range(nch):
                j = nch * i + k
                loads[j].wait()
                cp = _remote(send_buf.at[rows_of[j], :], recv_buf.at[rows_of[j], :], send_sems.at[j], recv_sems.at[j],
                             (x, y, 1 - c))
                cp.start()
                sends.append(cp)
        for i in range(n):
            recv_buf, own_buf, sum_buf = bufs[4 * i + 1], bufs[4 * i + 2], bufs[4 * i + 3]
            for k in range(nch):
                j = nch * i + k
                rows = rows_of[j]
                owns[j].wait()
                sends[j].wait_recv()
                sum_buf[rows, :] = (own_buf[rows, :].astype(F32) + recv_buf[rows, :].astype(F32)).astype(BF16)
                cp = pltpu.make_async_copy(sum_buf.at[rows, :], out_refs[i].at[rows, :], out_sems.at[j])
                cp.start()
                stores.append(cp)
        for j in range(nch * n):
            sends[j].wait_send()
            stores[j].wait()

    sems = pltpu.SemaphoreType.DMA((nch * n,))
    return _pcall(
        body, name="chipsum_" + pieces[0].name, in_specs=[ANY] * n, out_specs=[ANY] * n,
        out_shape=[jax.ShapeDtypeStruct(pc.half_shape, BF16) for pc in pieces],
        scratch_shapes=[pltpu.VMEM(pc.half_shape, BF16) for pc in pieces for _ in range(4)] + [sems] * 5,
        compiler_params=pltpu.CompilerParams(vmem_limit_bytes=VMEM_LIMIT),
    )(*partials)


def _scatter_start(pieces, chip_sums):
    n = len(pieces)

    def body(*refs):
        sums, lands = refs[:n], refs[n:2 * n]
        sends, recvs = refs[4 * n:5 * n], refs[5 * n:6 * n]
        token = refs[6 * n]
        x, y, c, chips = _mesh_place()
        for i, pc in enumerate(pieces):
            for k, (cx, cy) in enumerate(chips):
                _remote(pc.half_shard(sums[i], 2 * cx + cy), lands[i].at[k], sends[i].at[k], recvs[i].at[k],
                        (cx, cy, c)).start()
        token[...] = jnp.zeros(TOKEN_SHAPE, F32)

    land_shapes = [(3,) + pc.shard_half_shape for pc in pieces]
    sems = [pltpu.SemaphoreType.DMA((3,))] * (2 * n)
    outs = _pcall(
        body, name="scatter_start_" + pieces[0].name,
        in_specs=[HBM] * (2 * n), out_specs=[HBM] * (2 * n) + [SEM] * (2 * n) + [pl.BlockSpec(memory_space=pltpu.VMEM)],
        out_shape=[pltpu.HBM(pc.half_shape, BF16) for pc in pieces] + [pltpu.HBM(sh, BF16) for sh in land_shapes]
        + sems + [jax.ShapeDtypeStruct(TOKEN_SHAPE, F32)],
        input_output_aliases={i: i for i in range(2 * n)}, compiler_params=SPLIT_PARAMS,
    )(*[_in_hbm(cs) for cs in chip_sums], *[_in_hbm(lax.empty(sh, BF16)) for sh in land_shapes])
    return [(outs[i], outs[n + i], outs[2 * n + i], outs[3 * n + i]) for i in range(n)], outs[4 * n]


def _scatter_wait(pc, chip_sum, land, send_sems, recv_sems, after):
    def body(sum_ref, land_ref, send_ref, recv_ref, after_ref, sum_out, land_out):
        del after_ref, sum_out, land_out
        x, y, c, chips = _mesh_place()
        for k, (cx, cy) in enumerate(chips):
            cp = _remote(pc.half_shard(sum_ref, 2 * cx + cy), land_ref.at[k], send_ref.at[k], recv_ref.at[k], (cx, cy, c))
            cp.wait_send()
            cp.wait_recv()

    return _pcall(
        body, name=f"scatter_wait_{pc.name}",
        in_specs=[HBM, HBM, SEM, SEM, ANY], out_specs=[HBM, HBM],
        out_shape=[pltpu.HBM(pc.half_shape, BF16), pltpu.HBM((3,) + pc.shard_half_shape, BF16)],
        input_output_aliases={0: 0, 1: 1}, compiler_params=SPLIT_PARAMS,
    )(chip_sum, land, send_sems, recv_sems, after)


SHARD_OPERAND_SHAPES = ((D, EVEN_IN // 4), (D // 4, D), (D, ODD_IN // 4), (D // 4, D), (2 * D, D_FF // 4), (2 * D_FF // 4, D))


def _allsum_join(operands, chip_sums, lands):
    pieces = [pc for pc in PIECES if pc.src in operands]
    n = len(pieces)
    n_out = len(operands)

    def body(*refs):
        sum_refs = refs[:n]
        land_refs = refs[n:2 * n]
        out_refs = dict(zip(operands, refs[2 * n:2 * n + n_out]))
        refs = refs[2 * n + n_out:]
        in_bufs, fin_bufs, recv_bufs = refs[:n], refs[n:2 * n], refs[2 * n:3 * n]
        load_sems, send_sems, recv_sems, out_sems = refs[3 * n:]
        x, y, c, _ = _mesh_place()
        s = 2 * x + y
        loads, sends, stores = [], [], []
        for j, pc in enumerate(pieces):
            cp = pltpu.make_async_copy(land_refs[j], in_bufs[j].at[pl.ds(0, 3)], load_sems.at[2 * j])
            cp.start()
            loads.append(cp)
            cp = pltpu.make_async_copy(pc.half_shard(sum_refs[j], s), in_bufs[j].at[3], load_sems.at[2 * j + 1])
            cp.start()
            loads.append(cp)
        for j, pc in enumerate(pieces):
            loads[2 * j].wait()
            loads[2 * j + 1].wait()
            acc = in_bufs[j][0].astype(F32)
            for k in range(1, 4):
                acc = acc + in_bufs[j][k].astype(F32)
            fin_bufs[j][...] = acc
            cp = pltpu.make_async_copy(fin_bufs[j], pc.shard_half(out_refs[pc.src], c), out_sems.at[2 * j])
            cp.start()
            stores.append(cp)
            cp = _remote(fin_bufs[j], recv_bufs[j], send_sems.at[j], recv_sems.at[j], (x, y, 1 - c))
            cp.start()
            sends.append(cp)
        for j, pc in enumerate(pieces):
            sends[j].wait_recv()
            cp = pltpu.make_async_copy(recv_bufs[j], pc.shard_half(out_refs[pc.src], 1 - c), out_sems.at[2 * j + 1])
            cp.start()
            stores.append(cp)
        for cp in sends:
            cp.wait_send()
        for cp in stores:
            cp.wait()

    halves = [pc.shard_half_shape for pc in pieces]
    return _pcall(
        body, name=f"allsum_join_{operands[0]}", in_specs=[ANY] * (2 * n), out_specs=[ANY] * n_out,
        out_shape=[jax.ShapeDtypeStruct(SHARD_OPERAND_SHAPES[o], F32) for o in operands],
        scratch_shapes=[pltpu.VMEM((4,) + sh, BF16) for sh in halves] + [pltpu.VMEM(sh, F32) for sh in halves] * 2
        + [pltpu.SemaphoreType.DMA((2 * n,)), pltpu.SemaphoreType.DMA((n,)), pltpu.SemaphoreType.DMA((n,)),
           pltpu.SemaphoreType.DMA((2 * n,))],
        compiler_params=pltpu.CompilerParams(vmem_limit_bytes=VMEM_LIMIT),
    )(*chip_sums, *lands)


PEER_FLIPS = tuple((a, b, e) for a in (0, 1) for b in (0, 1) for e in (0, 1) if (a, b, e) != (0, 0, 0))


def _peers():
    x, y, c = lax.axis_index("x"), lax.axis_index("y"), lax.axis_index("c")
    me = 4 * x + 2 * y + c
    out = []
    for a, b, e in PEER_FLIPS:
        px, py, pc = (1 - x if a else x), (1 - y if b else y), (1 - c if e else c)
        out.append(((px, py, pc), 4 * px + 2 * py + pc))
    return me, out


def _exchange8_start(name, blk):
    m = blk.shape[0]

    def body(blk_ref, land_ref, blk_out, land_out, sends, recvs, token):
        del blk_out, land_out
        me, peers = _peers()
        for k, (dev, _) in enumerate(peers):
            _remote(blk_ref, land_ref.at[me], sends.at[k], recvs.at[k], dev).start()
        token[...] = jnp.zeros(TOKEN_SHAPE, F32)

    sems = pltpu.SemaphoreType.DMA((7,))
    return _pcall(
        body, name=name,
        in_specs=[HBM, HBM], out_specs=[HBM, HBM, SEM, SEM, pl.BlockSpec(memory_space=pltpu.VMEM)],
        out_shape=[pltpu.HBM((m, LANES), F32), pltpu.HBM((8, m, LANES), F32), sems, sems,
                   jax.ShapeDtypeStruct(TOKEN_SHAPE, F32)],
        input_output_aliases={0: 0, 1: 1}, compiler_params=SPLIT_PARAMS,
    )(_in_hbm(blk), _in_hbm(lax.empty((8, m, LANES), F32)))


def _exchange8_wait(name, blk, land, send_sems, recv_sems, after):
    def body(blk_ref, land_ref, send_ref, recv_ref, after_ref, blk_out, land_out):
        del after_ref, blk_out, land_out
        _, peers = _peers()
        for k, (dev, slot) in enumerate(peers):
            cp = _remote(blk_ref, land_ref.at[slot], send_ref.at[k], recv_ref.at[k], dev)
            cp.wait_send()
            cp.wait_recv()

    m = blk.shape[0]
    return _pcall(
        body, name=name,
        in_specs=[HBM, HBM, SEM, SEM, ANY], out_specs=[HBM, HBM],
        out_shape=[pltpu.HBM((m, LANES), F32), pltpu.HBM((8, m, LANES), F32)],
        input_output_aliases={0: 0, 1: 1}, compiler_params=SPLIT_PARAMS,
    )(blk, land, send_sems, recv_sems, after)


def _collect8(name, blk, land, with_sum):
    m = blk.shape[0]

    def body(blk_ref, land_ref, out_ref, *scratch):
        sems = scratch[-1]
        dst = scratch[0] if with_sum else out_ref
        me, peers = _peers()
        copies = [pltpu.make_async_copy(blk_ref, dst.at[me], sems.at[7])]
        for k, (_, slot) in enumerate(peers):
            copies.append(pltpu.make_async_copy(land_ref.at[slot], dst.at[slot], sems.at[k]))
        for cp in copies:
            cp.start()
        for cp in copies:
            cp.wait()
        if with_sum:
            acc = dst[0]
            for dev in range(1, 8):
                acc = acc + dst[dev]
            out_ref[...] = acc

    all_shape = (8, m, LANES)
    return _pcall(
        body, name=name, in_specs=[ANY, ANY], out_specs=pl.BlockSpec(memory_space=pltpu.VMEM),
        out_shape=jax.ShapeDtypeStruct((m, LANES) if with_sum else all_shape, F32),
        scratch_shapes=([pltpu.VMEM(all_shape, F32)] if with_sum else []) + [pltpu.SemaphoreType.DMA((8,))],
    )(blk, land)


def _pack(arrays, row_counts):
    rows = []
    for a, n in zip(arrays, row_counts):
        flat = a.reshape(-1, LANES)
        rows.append(jnp.pad(flat, ((0, n - flat.shape[0]), (0, 0))))
    return jnp.concatenate(rows, axis=0)


def _unpack(buf, shapes, row_counts):
    out, r0 = [], 0
    for sh, n in zip(shapes, row_counts):
        size = 1
        for dim in sh:
            size *= dim
        out.append(buf[r0:r0 + size // LANES].reshape(sh))
        r0 += n
    return out


REPL_NAMES = ("norm_mix_g", "norm_ffn_g", "even_conv_b", "even_ln_g", "even_ln_b", "odd_sg_w", "odd_sg_b", "final_g")
REPL_SHAPES = ((2, D), (2, D), (1, 512), (1, 512), (1, 512), (1, SG_GROUPS, CHUNK, CHUNK), (1, SG_GROUPS, CHUNK), (D,))
REPL_ROWS = (16, 16, 8, 8, 8, 512, 8, 8)
SHARDED_NAMES = ("even_conv_k", "odd_conv_k", "odd_ln_g", "odd_ln_b")
SHARDED_SHARD_SHAPES = ((1, CONV_W, LANES), (1, SCONV_W, LANES), (1, LANES), (1, LANES))
SHARDED_SHARD_ROWS = (32, 8, 8, 8)
SHARDED_FULL_SHAPES = ((CONV_W, 512), (SCONV_W, 512), (1, 512), (1, 512))
SHARDED_FULL_ROWS = (128, 16, 8, 8)


def kernel(x, norm_mix_g, norm_ffn_g, even_w_in, even_conv_k, even_conv_b, even_ln_g, even_ln_b, even_w_out, odd_w_in, odd_conv_k, odd_ln_g, odd_ln_b, odd_sg_w, odd_sg_b, odd_w_out, ffn_w1, ffn_w2, final_g, loss_target, m_norm_mix_g, m_norm_ffn_g, m_even_w_in, m_even_conv_k, m_even_conv_b, m_even_ln_g, m_even_ln_b, m_even_w_out, m_odd_w_in, m_odd_conv_k, m_odd_ln_g, m_odd_ln_b, m_odd_sg_w, m_odd_sg_b, m_odd_w_out, m_ffn_w1, m_ffn_w2, m_final_g, v_norm_mix_g, v_norm_ffn_g, v_even_w_in, v_even_conv_k, v_even_conv_b, v_even_ln_g, v_even_ln_b, v_even_w_out, v_odd_w_in, v_odd_conv_k, v_odd_ln_g, v_odd_ln_b, v_odd_sg_w, v_odd_sg_b, v_odd_w_out, v_ffn_w1, v_ffn_w2, v_final_g):
    names = ("norm_mix_g", "norm_ffn_g", "even_w_in", "even_conv_k", "even_conv_b", "even_ln_g", "even_ln_b", "even_w_out",
             "odd_w_in", "odd_conv_k", "odd_ln_g", "odd_ln_b", "odd_sg_w", "odd_sg_b", "odd_w_out", "ffn_w1", "ffn_w2", "final_g")
    w = dict(zip(names, (norm_mix_g, norm_ffn_g, even_w_in, even_conv_k, even_conv_b, even_ln_g, even_ln_b, even_w_out,
                         odd_w_in, odd_conv_k, odd_ln_g, odd_ln_b, odd_sg_w, odd_sg_b, odd_w_out, ffn_w1, ffn_w2, final_g)))
    mom = dict(zip(names, (m_norm_mix_g, m_norm_ffn_g, m_even_w_in, m_even_conv_k, m_even_conv_b, m_even_ln_g, m_even_ln_b,
                           m_even_w_out, m_odd_w_in, m_odd_conv_k, m_odd_ln_g, m_odd_ln_b, m_odd_sg_w, m_odd_sg_b, m_odd_w_out,
                           m_ffn_w1, m_ffn_w2, m_final_g)))
    vel = dict(zip(names, (v_norm_mix_g, v_norm_ffn_g, v_even_w_in, v_even_conv_k, v_even_conv_b, v_even_ln_g, v_even_ln_b,
                           v_even_w_out, v_odd_w_in, v_odd_conv_k, v_odd_ln_g, v_odd_ln_b, v_odd_sg_w, v_odd_sg_b, v_odd_w_out,
                           v_ffn_w1, v_ffn_w2, v_final_g)))
    big_names = ("even_w_in", "even_w_out", "odd_w_in", "odd_w_out", "ffn_w1", "ffn_w2")
    chip = 2 * lax.axis_index("x") + lax.axis_index("y")

    def shard2d(t, name):
        return t[name].reshape(SHARD_OPERAND_SHAPES[big_names.index(name)])

    chip_op = jnp.reshape(chip, (1,)).astype(jnp.int32)
    small_pack = _pack([w[n] for n in SHARDED_NAMES], SHARDED_SHARD_ROWS)
    small_blk, small_land, small_send, small_recv, small_token = _exchange8_start("gather_small_start", small_pack)
    first = _cast_place(PIECES[0], shard2d(w, big_names[PIECES[0].src]), chip_op, tie=small_token)
    fly0, send0, recv0, token = _gather_start("gather_start_first", PIECES[:1], [first])
    placed = [_cast_place(pc, shard2d(w, big_names[pc.src]), chip_op, tie=token) for pc in PIECES[1:]]
    fly1, send1, recv1, all_started = _gather_start("gather_start_rest", PIECES[1:], placed)
    flying, gather_send, gather_recv = fly0 + fly1, send0 + send1, recv0 + recv1
    ready = {}

    names_in_order = [pc.name for pc in PIECES]

    def weight(name, after):
        if name not in ready:
            i = names_in_order.index(name)
            group = next(grp for grp in FORWARD_GROUPS if i in grp)
            if i == 0:
                after = all_started
            landed = [_gather_wait(PIECES[j], flying[j], gather_send[j], gather_recv[j], after) for j in group]
            ready.update(zip((PIECES[j].name for j in group), _core_forward([PIECES[j] for j in group], landed)))
        return ready[name]

    scattering = []
    held = []

    def emit(name, partial):
        held.append((PIECES[names_in_order.index(name)], partial))
        if name in HOLD_BACK:
            return None
        pieces = [pc for pc, _ in held]
        started, token = _scatter_start(pieces, _chipsum(pieces, [part for _, part in held]))
        scattering.extend((pc,) + tuple(st) for pc, st in zip(pieces, started))
        held.clear()
        return token

    full = {}
    small_blk, small_land = _exchange8_wait("gather_small_wait", small_blk, small_land, small_send, small_recv, all_started)
    gathered = _collect8("gather_small_collect", small_blk, small_land, False)
    gathered = gathered.reshape(4, 2, sum(SHARDED_SHARD_ROWS), LANES)[:, 0]
    r0 = 0
    for n, sh, rows, full_sh in zip(SHARDED_NAMES, SHARDED_SHARD_SHAPES, SHARDED_SHARD_ROWS, SHARDED_FULL_SHAPES):
        per_chip = gathered[:, r0:r0 + rows].reshape(4, -1)[:, :full_sh[0] * LANES].reshape(4, full_sh[0], LANES)
        full[n] = jnp.transpose(per_chip, (1, 0, 2)).reshape(full_sh)
        r0 += rows
    p = dict(full)
    p.update(norm_mix_g0=norm_mix_g[0:1], norm_mix_g1=norm_mix_g[1:2], norm_ffn_g0=norm_ffn_g[0:1], norm_ffn_g1=norm_ffn_g[1:2],
             even_conv_b=even_conv_b, even_ln_g=even_ln_g, even_ln_b=even_ln_b,
             odd_sg_w=odd_sg_w[0], odd_sg_bt=odd_sg_b[0].T, final_g=final_g[None, :])

    loss_row, dx, g = _local_step(x[0], loss_target[0], p, weight, emit)

    grad_parts = [loss_row, g["norm_mix_g0"], g["norm_mix_g1"], g["norm_ffn_g0"], g["norm_ffn_g1"], g["even_conv_b"],
                  g["even_ln_g"], g["even_ln_b"], g["odd_sg_w"], g["odd_sg_bt"].T, g["final_g"],
                  g["even_conv_k"], g["odd_conv_k"], g["odd_ln_g"], g["odd_ln_b"]]
    grad_pack = _pack(grad_parts, (8, 8, 8, 8, 8) + REPL_ROWS[2:] + SHARDED_FULL_ROWS)
    grad_blk, grad_land, grad_send, grad_recv, grad_token = _exchange8_start("allreduce_small_start", grad_pack)

    landed = {pc.name: _scatter_wait(pc, chip_sum, land, send_sems, recv_sems, grad_token)
              for pc, chip_sum, land, send_sems, recv_sems in scattering}
    big_grads = {}
    for operands in JOIN_GROUPS:
        pieces = [pc for pc in PIECES if pc.src in operands]
        joined = _allsum_join(operands, [landed[pc.name][0] for pc in pieces], [landed[pc.name][1] for pc in pieces])
        big_grads.update(zip((big_names[o] for o in operands), joined))

    grad_blk, grad_land = _exchange8_wait("allreduce_small_wait", grad_blk, grad_land, grad_send, grad_recv,
                                          big_grads[big_names[JOIN_GROUPS[-1][-1]]])
    grad_sum = _collect8("allreduce_small_sum", grad_blk, grad_land, True)
    loss = grad_sum[0, 0]
    parts = _unpack(grad_sum[8:], REPL_SHAPES + SHARDED_FULL_SHAPES, REPL_ROWS + SHARDED_FULL_ROWS)
    grads = dict(zip(REPL_NAMES, parts[:len(REPL_NAMES)]))
    for n, full_g, sh in zip(SHARDED_NAMES, parts[len(REPL_NAMES):], SHARDED_SHARD_SHAPES):
        grads[n] = lax.dynamic_slice_in_dim(full_g, chip * LANES, LANES, axis=1).reshape(sh)
    for n in big_names:
        grads[n] = big_grads[n].reshape(w[n].shape)

    delta, new_m, new_v = {}, {}, {}
    for n in big_names:
        d2, m2, v2 = _adamw(f"adamw_{n}", shard2d(w, n), big_grads[n], shard2d(mom, n), shard2d(vel, n))
        delta[n], new_m[n], new_v[n] = (t.reshape(w[n].shape) for t in (d2, m2, v2))
    for tag, group, rows, shapes in (("repl", REPL_NAMES, REPL_ROWS, [w[n].shape for n in REPL_NAMES]),
                                     ("sharded", SHARDED_NAMES, SHARDED_SHARD_ROWS, SHARDED_SHARD_SHAPES)):
        packs = [_pack([t[n] for n in group], rows) for t in (w, grads, mom, vel)]
        outs = _adamw(f"adamw_{tag}", *packs)
        for res, o in zip((delta, new_m, new_v), outs):
            res.update(zip(group, _unpack(o, shapes, rows)))

    out = [loss, dx[None]]
    for res in (grads, delta, new_m, new_v):
        out.extend(res[n] for n in names)
    return tuple(out)
```

```python
import functools

import jax
import jax.numpy as jnp
from jax import lax
from jax.experimental import pallas as pl
from jax.experimental.pallas import tpu as pltpu

F32 = jnp.float32
BF16 = jnp.bfloat16

T = 2048
D = 1024
CONV_CH = 512
CONV_W = 31
HEAD_DIM = 64
ATT_W = 1536
EVEN_IN = 5632
ODD_IN = 2560
SCONV_W = 3
SG_GROUPS = 4
CHUNK = 128
D_FF = 4096
EPS = 1e-6
DILATIONS = (1, 4, 16)
BAND = 128
SCALE = HEAD_DIM ** -0.5
NEG = -1e30

ADAM_LR = 0.001
ADAM_B1 = 0.9
ADAM_B2 = 0.999
ADAM_EPS = 1e-08
ADAM_WD = 0.01
ADAM_STEP = 10

V7X_VMEM_BYTES = 64 * 2 ** 20
VMEM_LIMIT = V7X_VMEM_BYTES - 8 * 2 ** 20
LANES = 128
TOKEN_SHAPE = (8, LANES)


def _pcall(body, **kw):
    return pl.pallas_call(body, **kw)


def _params(*sem):
    return pltpu.CompilerParams(dimension_semantics=sem, vmem_limit_bytes=VMEM_LIMIT)


def _dot(a, b, dims):
    return lax.dot_general(a, b, (dims, ((), ())), preferred_element_type=F32)


def _nn(a, b):
    return _dot(a, b, ((1,), (0,)))


def _nt(a, b):
    return _dot(a, b, ((1,), (1,)))


def _tn(a, b):
    return _dot(a, b, ((0,), (0,)))


def _sigmoid(x):
    return 1.0 / (1.0 + jnp.exp(-x))


MM_VMEM_BUDGET = 40 * 2 ** 20


def _mm_tiles(mode, m, n, k, a_bytes, b_bytes, extra_bytes, out_bytes):
    def divisors(total, unit):
        return [t for t in range(unit, total + 1, unit) if total % t == 0]

    best = None
    for tm in divisors(m, LANES if mode == "tn" else 8):
        for tn in divisors(n, LANES):
            blocks = tm * k * a_bytes + tn * k * b_bytes + tm * tn * (extra_bytes + out_bytes)
            casts = (tm * k * 2 if a_bytes == 4 else 0) + (tn * k * 2 if b_bytes == 4 else 0)
            if 2 * blocks + casts + tm * tn * 4 > MM_VMEM_BUDGET:
                continue
            key = ((m // tm) * (n // tn), (m // tm) * n * k * b_bytes, abs(tm - tn))
            if best is None or key < best[0]:
                best = (key, tm, tn)
    return best[1], best[2]


def _mm(name, mode, a, b, m, n, k, out_dtypes, *, b_off=0, extras=(), epi=None, tie=None):
    tm, tn = _mm_tiles(mode, m, n, k, a.dtype.itemsize, b.dtype.itemsize, sum(e.dtype.itemsize for e in extras),
                       sum(jnp.dtype(dt).itemsize for dt in out_dtypes))
    assert b_off % tn == 0
    b_off //= tn
    if mode == "nn":
        a_spec = pl.BlockSpec((tm, k), lambda i, j: (i, 0))
        b_spec = pl.BlockSpec((k, tn), lambda i, j: (0, j + b_off))
        dims = ((1,), (0,))
    elif mode == "nt":
        a_spec = pl.BlockSpec((tm, k), lambda i, j: (i, 0))
        b_spec = pl.BlockSpec((tn, k), lambda i, j: (j, 0))
        dims = ((1,), (1,))
    else:
        a_spec = pl.BlockSpec((k, tm), lambda i, j: (0, i))
        b_spec = pl.BlockSpec((k, tn), lambda i, j: (0, j))
        dims = ((0,), (0,))
    o_spec = pl.BlockSpec((tm, tn), lambda i, j: (i, j))
    n_extra = len(extras)
    ties = () if tie is None else (tie,)

    def body(a_ref, b_ref, *rest):
        rest = rest[len(ties):]
        acc = _dot(a_ref[...].astype(BF16), b_ref[...].astype(BF16), dims)
        vals = epi(acc, *[e[...] for e in rest[:n_extra]]) if epi is not None else (acc,)
        for o_ref, v in zip(rest[n_extra:], vals):
            o_ref[...] = v.astype(o_ref.dtype)

    outs = _pcall(
        body, name=name, grid=(m // tm, n // tn),
        in_specs=[a_spec, b_spec] + [pl.BlockSpec(TOKEN_SHAPE, lambda i, j: (0, 0))] * len(ties) + [o_spec] * n_extra,
        out_specs=[o_spec] * len(out_dtypes),
        out_shape=[jax.ShapeDtypeStruct((m, n), dt) for dt in out_dtypes],
        compiler_params=_params("parallel", "parallel"),
    )(a, b, *ties, *extras)
    return outs[0] if len(out_dtypes) == 1 else outs


def _row_tile(k, a_bytes, n_row_blocks):
    for tm in (1024, 512, 256, 128):
        if 2 * (tm * k * a_bytes + D * k * 2 + n_row_blocks * tm * D * 4) + tm * D * 4 <= MM_VMEM_BUDGET + 4 * 2 ** 20:
            return tm
    raise ValueError("no row tile fits")


def _mm_out_norm(name, a, b, k, res, g_next):
    tm = _row_tile(k, a.dtype.itemsize, 3)

    def body(a_ref, b_ref, r_ref, g_ref, h_ref, hn_ref):
        h = _nn(a_ref[...].astype(BF16), b_ref[...]) + r_ref[...]
        h_ref[...] = h
        r = lax.rsqrt(jnp.mean(h * h, axis=-1, keepdims=True) + EPS)
        hn_ref[...] = ((h * r) * g_ref[...]).astype(BF16)

    row = pl.BlockSpec((tm, D), lambda i: (i, 0))
    return _pcall(
        body, name=name, grid=(T // tm,),
        in_specs=[pl.BlockSpec((tm, k), lambda i: (i, 0)), pl.BlockSpec((k, D), lambda i: (0, 0)), row,
                  pl.BlockSpec((1, D), lambda i: (0, 0))],
        out_specs=[row, row],
        out_shape=[jax.ShapeDtypeStruct((T, D), F32), jax.ShapeDtypeStruct((T, D), BF16)],
        compiler_params=_params("parallel"),
    )(a, b, res, g_next)


def _mm_dx_norm(name, dz, w, k, h, g, dres, tie=None):
    tm = _row_tile(k, dz.dtype.itemsize, 3)
    ties = () if tie is None else (tie,)

    def body(a_ref, b_ref, *rest):
        h_ref, g_ref, r_ref, dh_ref, dg_ref = rest[len(ties):]
        dy = _nt(a_ref[...].astype(BF16), b_ref[...])
        x = h_ref[...]
        r = lax.rsqrt(jnp.mean(x * x, axis=-1, keepdims=True) + EPS)
        nrm = x * r
        dn = dy * g_ref[...]
        dh_ref[...] = r_ref[...] + r * (dn - nrm * jnp.mean(dn * nrm, axis=-1, keepdims=True))

        @pl.when(pl.program_id(0) == 0)
        def _():
            dg_ref[...] = jnp.zeros_like(dg_ref)

        dg_ref[...] += jnp.sum(dy * nrm, axis=0, keepdims=True)

    row = pl.BlockSpec((tm, D), lambda i: (i, 0))
    vec = pl.BlockSpec((1, D), lambda i: (0, 0))
    return _pcall(
        body, name=name, grid=(T // tm,),
        in_specs=[pl.BlockSpec((tm, k), lambda i: (i, 0)), pl.BlockSpec((D, k), lambda i: (0, 0))]
        + [pl.BlockSpec(TOKEN_SHAPE, lambda i: (0, 0))] * len(ties) + [row, vec, row],
        out_specs=[row, vec],
        out_shape=[jax.ShapeDtypeStruct((T, D), F32), jax.ShapeDtypeStruct((1, D), F32)],
        compiler_params=_params("arbitrary"),
    )(dz, w, *ties, h, g, dres)


def _rms_fwd(name, h, g, tm=512):
    def body(h_ref, g_ref, o_ref):
        x = h_ref[...]
        r = lax.rsqrt(jnp.mean(x * x, axis=-1, keepdims=True) + EPS)
        o_ref[...] = ((x * r) * g_ref[...]).astype(BF16)

    return _pcall(
        body, name=name, grid=(T // tm,),
        in_specs=[pl.BlockSpec((tm, D), lambda i: (i, 0)), pl.BlockSpec((1, D), lambda i: (0, 0))],
        out_specs=pl.BlockSpec((tm, D), lambda i: (i, 0)),
        out_shape=jax.ShapeDtypeStruct((T, D), BF16),
        compiler_params=_params("parallel"),
    )(h, g)


def _loss_head(h, g, target, tm=512):
    def body(h_ref, g_ref, t_ref, dh_ref, dg_ref, loss_ref):
        x = h_ref[...]
        r = lax.rsqrt(jnp.mean(x * x, axis=-1, keepdims=True) + EPS)
        nrm = x * r
        gain = g_ref[...]
        err = nrm * gain - t_ref[...]
        dy = err * (1.0 / D)
        dn = dy * gain
        dh_ref[...] = r * (dn - nrm * jnp.mean(dn * nrm, axis=-1, keepdims=True))

        @pl.when(pl.program_id(0) == 0)
        def _():
            dg_ref[...] = jnp.zeros_like(dg_ref)
            loss_ref[...] = jnp.zeros_like(loss_ref)

        dg_ref[...] += jnp.sum(dy * nrm, axis=0, keepdims=True)
        part = jnp.sum(jnp.sum(err * err, axis=1, keepdims=True), axis=0, keepdims=True) * (0.5 / D)
        loss_ref[...] += jnp.broadcast_to(part, (1, LANES))

    row = pl.BlockSpec((tm, D), lambda i: (i, 0))
    vec = pl.BlockSpec((1, D), lambda i: (0, 0))
    return _pcall(
        body, name="loss_head", grid=(T // tm,),
        in_specs=[row, vec, row], out_specs=[row, vec, pl.BlockSpec((1, LANES), lambda i: (0, 0))],
        out_shape=[jax.ShapeDtypeStruct((T, D), F32), jax.ShapeDtypeStruct((1, D), F32),
                   jax.ShapeDtypeStruct((1, LANES), F32)],
        compiler_params=_params("arbitrary"),
    )(h, g, target)


CONV_TILE = 256
CONV_HALO = 32


def _glu(z):
    return z[:, :CONV_CH] * _sigmoid(z[:, CONV_CH:])


SUBLANES = 8


def _sublane_shifts(win):
    n = win.shape[0]
    return [win] + [win[r:r + n - SUBLANES, :] for r in range(1, SUBLANES)]


def _rows_from(shifts, off, n):
    q, r = divmod(off, SUBLANES)
    return shifts[r][q * SUBLANES:q * SUBLANES + n, :]


def _econv_fwd(zc, conv_k, conv_b, ln_g, ln_b):
    R, H = CONV_TILE, CONV_HALO

    def body(z_ref, zh_ref, k_ref, b_ref, g_ref, be_ref, cv_ref, cat_ref):
        i = pl.program_id(0)
        glu = _glu(z_ref[...])
        halo = _glu(zh_ref[...]) * (i > 0).astype(F32)
        win = _sublane_shifts(jnp.concatenate([halo, glu], axis=0))
        acc = jnp.zeros((R, CONV_CH), F32) + b_ref[...]
        for j in range(CONV_W):
            acc = acc + k_ref[j:j + 1, :] * _rows_from(win, H - (CONV_W - 1) + j, R)
        cv_ref[...] = acc
        mu = jnp.mean(acc, axis=-1, keepdims=True)
        xc = acc - mu
        rstd = lax.rsqrt(jnp.mean(xc * xc, axis=-1, keepdims=True) + EPS)
        ln = xc * rstd * g_ref[...] + be_ref[...]
        cat_ref[...] = (ln * _sigmoid(ln)).astype(BF16)

    vec = pl.BlockSpec((1, CONV_CH), lambda i: (0, 0))
    return _pcall(
        body, name="econv_fwd", grid=(T // R,),
        in_specs=[pl.BlockSpec((R, 2 * CONV_CH), lambda i: (i, 0)),
                  pl.BlockSpec((H, 2 * CONV_CH), lambda i: (jnp.maximum(i * (R // H) - 1, 0), 0)),
                  pl.BlockSpec((CONV_W, CONV_CH), lambda i: (0, 0)), vec, vec, vec],
        out_specs=[pl.BlockSpec((R, CONV_CH), lambda i: (i, 0)), pl.BlockSpec((R, CONV_CH), lambda i: (i, 0))],
        out_shape=[jax.ShapeDtypeStruct((T, CONV_CH), F32), jax.ShapeDtypeStruct((T, D), BF16)],
        compiler_params=_params("parallel"),
    )(zc, zc, conv_k, conv_b, ln_g, ln_b)


def _econv_bwd_ln(cv, dcat, ln_g, ln_b):
    R = CONV_TILE

    def body(cv_ref, d_ref, g_ref, be_ref, dcv_ref, dg_ref, dbe_ref, dcb_ref):
        cv_t = cv_ref[...]
        mu = jnp.mean(cv_t, axis=-1, keepdims=True)
        xc = cv_t - mu
        rstd = lax.rsqrt(jnp.mean(xc * xc, axis=-1, keepdims=True) + EPS)
        xh = xc * rstd
        ln = xh * g_ref[...] + be_ref[...]
        sg = _sigmoid(ln)
        dln = d_ref[...] * (sg * (1.0 + ln * (1.0 - sg)))
        dxh = dln * g_ref[...]
        dcv = rstd * (dxh - jnp.mean(dxh, axis=-1, keepdims=True) - xh * jnp.mean(dxh * xh, axis=-1, keepdims=True))
        dcv_ref[...] = dcv

        @pl.when(pl.program_id(0) == 0)
        def _():
            dg_ref[...] = jnp.zeros_like(dg_ref)
            dbe_ref[...] = jnp.zeros_like(dbe_ref)
            dcb_ref[...] = jnp.zeros_like(dcb_ref)

        dg_ref[...] += jnp.sum(dln * xh, axis=0, keepdims=True)
        dbe_ref[...] += jnp.sum(dln, axis=0, keepdims=True)
        dcb_ref[...] += jnp.sum(dcv, axis=0, keepdims=True)

    vec = pl.BlockSpec((1, CONV_CH), lambda i: (0, 0))
    row = pl.BlockSpec((R, CONV_CH), lambda i: (i, 0))
    vshape = jax.ShapeDtypeStruct((1, CONV_CH), F32)
    return _pcall(
        body, name="econv_bwd_ln", grid=(T // R,),
        in_specs=[row, row, vec, vec], out_specs=[row, vec, vec, vec],
        out_shape=[jax.ShapeDtypeStruct((T, CONV_CH), F32), vshape, vshape, vshape],
        compiler_params=_params("arbitrary"),
    )(cv, dcat, ln_g, ln_b)


def _econv_bwd_conv(dcv, zc, conv_k):
    R, H = CONV_TILE, CONV_HALO
    last = T // R - 1

    def body(d_ref, dn_ref, z_ref, zh_ref, k_ref, dz_ref, dk_ref):
        i = pl.program_id(0)
        z = z_ref[...]
        a_lin = z[:, :CONV_CH]
        sg = _sigmoid(z[:, CONV_CH:])
        glu = a_lin * sg
        halo = _glu(zh_ref[...]) * (i > 0).astype(F32)
        win = _sublane_shifts(jnp.concatenate([halo, glu], axis=0))
        dcv_t = d_ref[...]
        nxt = dn_ref[...] * (i < last).astype(F32)
        winb = _sublane_shifts(jnp.concatenate([dcv_t, nxt], axis=0))

        @pl.when(i == 0)
        def _():
            dk_ref[...] = jnp.zeros_like(dk_ref)

        dglu = jnp.zeros((R, CONV_CH), F32)
        for j in range(CONV_W):
            dk_ref[j:j + 1, :] += jnp.sum(dcv_t * _rows_from(win, H - (CONV_W - 1) + j, R), axis=0, keepdims=True)
            dglu = dglu + k_ref[j:j + 1, :] * _rows_from(winb, CONV_W - 1 - j, R)
        dz_ref[...] = jnp.concatenate([dglu * sg, dglu * a_lin * sg * (1.0 - sg)], axis=1).astype(BF16)

    return _pcall(
        body, name="econv_bwd_conv", grid=(T // R,),
        in_specs=[pl.BlockSpec((R, CONV_CH), lambda i: (i, 0)),
                  pl.BlockSpec((H, CONV_CH), lambda i: (jnp.minimum((i + 1) * (R // H), T // H - 1), 0)),
                  pl.BlockSpec((R, 2 * CONV_CH), lambda i: (i, 0)),
                  pl.BlockSpec((H, 2 * CONV_CH), lambda i: (jnp.maximum(i * (R // H) - 1, 0), 0)),
                  pl.BlockSpec((CONV_W, CONV_CH), lambda i: (0, 0))],
        out_specs=[pl.BlockSpec((R, 2 * CONV_CH), lambda i: (i, 0)), pl.BlockSpec((CONV_W, CONV_CH), lambda i: (0, 0))],
        out_shape=[jax.ShapeDtypeStruct((T, EVEN_IN), BF16), jax.ShapeDtypeStruct((CONV_W, CONV_CH), F32)],
        compiler_params=_params("arbitrary"),
    )(dcv, dcv, zc, zc, conv_k)


def _swap_halves(v):
    lane = lax.broadcasted_iota(jnp.int32, v.shape, 1)
    return jnp.where((lane % HEAD_DIM) < HEAD_DIM // 2, pltpu.roll(v, LANES - HEAD_DIM // 2, 1),
                     pltpu.roll(v, HEAD_DIM // 2, 1))


def _qkv_proj(hn, w_in, rope_c, rope_s, tm=T):
    tn = 4 * LANES

    def body(a_ref, b_ref, c_ref, s_ref, o_ref):
        j = pl.program_id(1)
        acc = _nn(a_ref[...], b_ref[...])
        for p in range(4):
            v = acc[:, p * LANES:(p + 1) * LANES]
            rot = v * c_ref[...] + _swap_halves(v) * s_ref[...]
            o_ref[p] = jnp.where(j < 6, rot, v)

    tab = pl.BlockSpec((tm, LANES), lambda i, j: (i, 0))
    return _pcall(
        body, name="qkv_proj", grid=(T // tm, 9),
        in_specs=[pl.BlockSpec((tm, D), lambda i, j: (i, 0)),
                  pl.BlockSpec((D, tn), lambda i, j: (0, j + (2 * CONV_CH) // tn)), tab, tab],
        out_specs=pl.BlockSpec((None, 4, tm, LANES), lambda i, j: (j, 0, i, 0)),
        out_shape=jax.ShapeDtypeStruct((9, 4, T, LANES), F32),
        compiler_params=_params("parallel", "parallel"),
    )(hn, w_in, rope_c, rope_s)


ATTN_FWD_UNROLL = 4
ATTN_BWD_UNROLL = 4


def _band_rows(start, d):
    if d == 1:
        return pl.ds(pl.multiple_of(start, BAND), BAND)
    return pl.ds(start, BAND, stride=d)


def _band_masks(n):
    row = lax.broadcasted_iota(jnp.int32, (BAND, BAND), 0)
    col = lax.broadcasted_iota(jnp.int32, (BAND, BAND), 1)
    no_prev = (n == 0).astype(jnp.int32) * (2 * BAND)
    return col <= row, col >= row + no_prev


def _attn_fwd(qkv, g):
    d = DILATIONS[g]
    nb = T // d // BAND

    def body(q_ref, k_ref, v_ref, o_ref, l_ref):
        lane_lo = lax.broadcasted_iota(jnp.int32, (BAND, LANES), 1) < HEAD_DIM

        heads = (lane_lo, jnp.logical_not(lane_lo))
        ones = jnp.ones((BAND, LANES), BF16)

        def step(it, carry):
            tiles = []
            for u in range(ATTN_FWD_UNROLL):
                idx = it * ATTN_FWD_UNROLL + u
                r = idx // nb
                n = idx % nb
                cur = _band_rows(n * (BAND * d) + r, d)
                prev = _band_rows(jnp.maximum(n - 1, 0) * (BAND * d) + r, d)
                mc, mp = _band_masks(n)
                tiles.append((cur, mc, mp, q_ref[cur, :], k_ref[cur, :].astype(BF16), v_ref[cur, :].astype(BF16),
                              k_ref[prev, :].astype(BF16), v_ref[prev, :].astype(BF16)))
            scores = []
            for cur, mc, mp, q, kc, vc, kp, vp in tiles:
                for hm in heads:
                    qm = jnp.where(hm, q, 0.0).astype(BF16)
                    scores.append((jnp.where(mc, _nt(qm, kc) * SCALE, NEG), jnp.where(mp, _nt(qm, kp) * SCALE, NEG)))
            maxes = [jnp.maximum(jnp.max(sc, axis=1, keepdims=True), jnp.max(sp, axis=1, keepdims=True))
                     for sc, sp in scores]
            probs = [(jnp.exp(sc - mx).astype(BF16), jnp.exp(sp - mx).astype(BF16))
                     for (sc, sp), mx in zip(scores, maxes)]
            dens = [_nn(pc, ones) + _nn(pp, ones) for pc, pp in probs]
            for t, (cur, mc, mp, q, kc, vc, kp, vp) in enumerate(tiles):
                outs, lses = [], []
                for h in range(2):
                    pc, pp = probs[2 * t + h]
                    outs.append((_nn(pc, vc) + _nn(pp, vp)) / dens[2 * t + h])
                    lses.append(maxes[2 * t + h] + jnp.log(dens[2 * t + h]))
                o_ref[cur, :] = jnp.where(lane_lo, outs[0], outs[1])
                l_ref[cur, :] = jnp.where(lane_lo, lses[0], lses[1])
            return carry

        lax.fori_loop(0, d * nb // ATTN_FWD_UNROLL, step, 0)

    def slab(which):
        return pl.BlockSpec((None, None, T, LANES), lambda p: (which * 3 + g, p, 0, 0))

    out = pl.BlockSpec((None, T, LANES), lambda p: (p, 0, 0))
    shape = jax.ShapeDtypeStruct((4, T, LANES), F32)
    return _pcall(
        body, name=f"attn_fwd{g}", grid=(4,),
        in_specs=[slab(0), slab(1), slab(2)], out_specs=[out, out], out_shape=[shape, shape],
        compiler_params=_params("parallel"),
    )(qkv, qkv, qkv)


def _attn_merge(outs, lses, cat, tm=1024):
    def body(o0, o1, o2, l0, l1, l2, cat_in, cat_ref, att_ref, w0, w1, w2):
        del cat_in
        la, lb, lc = l0[...], l1[...], l2[...]
        mx = jnp.maximum(jnp.maximum(la, lb), lc)
        ea, eb, ec = jnp.exp(la - mx), jnp.exp(lb - mx), jnp.exp(lc - mx)
        inv = 1.0 / (ea + eb + ec)
        wa, wb, wc = ea * inv, eb * inv, ec * inv
        att = wa * o0[...] + wb * o1[...] + wc * o2[...]
        att_ref[...] = att
        cat_ref[...] = att.astype(BF16)
        w0[...] = wa
        w1[...] = wb
        w2[...] = wc

    slab = pl.BlockSpec((None, tm, LANES), lambda p, i: (p, i, 0))
    shape = jax.ShapeDtypeStruct((4, T, LANES), F32)
    return _pcall(
        body, name="attn_merge", grid=(4, T // tm),
        in_specs=[slab] * 6 + [pl.BlockSpec(memory_space=pl.ANY)],
        out_specs=[pl.BlockSpec((tm, LANES), lambda p, i: (i, CONV_CH // LANES + p)), slab, slab, slab, slab],
        out_shape=[jax.ShapeDtypeStruct((T, D), BF16), shape, shape, shape, shape],
        input_output_aliases={6: 0},
        compiler_params=_params("parallel", "parallel"),
    )(*outs, *lses, cat)


def _attn_bwd(qkv, lse, wgt, att, dcat, dqkv, g):
    d = DILATIONS[g]
    nb = T // d // BAND

    def body(q_ref, k_ref, v_ref, l_ref, w_ref, a_ref, da_ref, dq_in, o_ref):
        del dq_in
        lane = lax.broadcasted_iota(jnp.int32, (BAND, LANES), 1)
        lane_lo = lane < HEAD_DIM
        row = lax.broadcasted_iota(jnp.int32, (LANES, LANES), 0)
        same_head = ((row // HEAD_DIM) == (lane // HEAD_DIM)).astype(BF16)
        dq_ref, dk_ref, dv_ref = o_ref.at[0], o_ref.at[1], o_ref.at[2]
        dk_ref[...] = jnp.zeros((T, LANES), F32)
        dv_ref[...] = jnp.zeros((T, LANES), F32)

        heads = (lane_lo, jnp.logical_not(lane_lo))

        def step(it, carry):
            tiles = []
            for u in range(ATTN_BWD_UNROLL):
                idx = it * ATTN_BWD_UNROLL + u
                r = idx // nb
                n = idx % nb
                cur = _band_rows(n * (BAND * d) + r, d)
                prev = _band_rows(jnp.maximum(n - 1, 0) * (BAND * d) + r, d)
                mc, mp = _band_masks(n)
                da = da_ref[cur, :]
                prod = da * a_ref[cur, :]
                hi = prod.astype(BF16)
                lo = (prod - hi.astype(F32)).astype(BF16)
                tiles.append(dict(cur=cur, prev=prev, mc=mc, mp=mp, da=da, hi=hi, lo=lo, q=q_ref[cur, :],
                                  kc=k_ref[cur, :].astype(BF16), vc=v_ref[cur, :].astype(BF16),
                                  kp=k_ref[prev, :].astype(BF16), vp=v_ref[prev, :].astype(BF16),
                                  lse=l_ref[cur, :], w=w_ref[cur, :]))
            for t in tiles:
                t["csum"] = _nn(t["hi"], same_head) + _nn(t["lo"], same_head)
            chains = []
            for t in tiles:
                for h, hm in enumerate(heads):
                    qm = jnp.where(hm, t["q"], 0.0).astype(BF16)
                    dam = jnp.where(hm, t["da"], 0.0).astype(BF16)
                    chains.append(dict(t=t, h=h, qm=qm, dam=dam,
                                       sc=jnp.where(t["mc"], _nt(qm, t["kc"]) * SCALE, NEG),
                                       sp=jnp.where(t["mp"], _nt(qm, t["kp"]) * SCALE, NEG),
                                       dpc=_nt(dam, t["vc"]), dpp=_nt(dam, t["vp"])))
            for ch in chains:
                t, col0 = ch["t"], ch["h"] * HEAD_DIM
                lse_h = t["lse"][:, col0:col0 + 1]
                w_h = t["w"][:, col0:col0 + 1]
                c_h = t["csum"][:, col0:col0 + 1]
                pwc = w_h * jnp.exp(ch["sc"] - lse_h)
                pwp = w_h * jnp.exp(ch["sp"] - lse_h)
                ch["dsc"] = (pwc * (ch["dpc"] - c_h) * SCALE).astype(BF16)
                ch["dsp"] = (pwp * (ch["dpp"] - c_h) * SCALE).astype(BF16)
                ch["pwc"] = pwc.astype(BF16)
                ch["pwp"] = pwp.astype(BF16)
            for ch in chains:
                t = ch["t"]
                ch["dq"] = _nn(ch["dsc"], t["kc"]) + _nn(ch["dsp"], t["kp"])
                ch["dkc"] = _tn(ch["dsc"], ch["qm"])
                ch["dkp"] = _tn(ch["dsp"], ch["qm"])
                ch["dvc"] = _tn(ch["pwc"], ch["dam"])
                ch["dvp"] = _tn(ch["pwp"], ch["dam"])
            for i, t in enumerate(tiles):
                c0, c1 = chains[2 * i], chains[2 * i + 1]
                dq_ref[t["cur"], :] = jnp.where(lane_lo, c0["dq"], c1["dq"])
                dk_ref[t["cur"], :] += c0["dkc"] + c1["dkc"]
                dk_ref[t["prev"], :] += c0["dkp"] + c1["dkp"]
                dv_ref[t["cur"], :] += c0["dvc"] + c1["dvc"]
                dv_ref[t["prev"], :] += c0["dvp"] + c1["dvp"]
            return carry

        lax.fori_loop(0, d * nb // ATTN_BWD_UNROLL, step, 0)

    def slab(which):
        return pl.BlockSpec((None, None, T, LANES), lambda p: (which * 3 + g, p, 0, 0))

    per_pair = pl.BlockSpec((None, T, LANES), lambda p: (p, 0, 0))
    return _pcall(
        body, name=f"attn_bwd{g}", grid=(4,),
        in_specs=[slab(0), slab(1), slab(2), per_pair, per_pair, per_pair,
                  pl.BlockSpec((T, LANES), lambda p: (0, CONV_CH // LANES + p)),
                  pl.BlockSpec(memory_space=pl.ANY)],
        out_specs=pl.BlockSpec((None, 3, None, T, LANES), lambda p: (g, 0, p, 0, 0)),
        out_shape=jax.ShapeDtypeStruct((3, 3, 4, T, LANES), F32),
        input_output_aliases={7: 0},
        compiler_params=_params("parallel"),
    )(qkv, qkv, qkv, lse, wgt, att, dcat, dqkv)


def _rope_bwd(dqkv, rope_c, rope_s, dz):
    wide = 4 * LANES

    def body(d_ref, c_ref, s_ref, dz_in, o_ref):
        del dz_in
        w = pl.program_id(1)
        for p in range(4):
            v = d_ref[p]
            rot = v * c_ref[...] + _swap_halves(v * s_ref[...])
            o_ref[:, p * LANES:(p + 1) * LANES] = jnp.where(w < 2, rot, v).astype(BF16)

    tab = pl.BlockSpec((T, LANES), lambda g, w: (0, 0))
    return _pcall(
        body, name="rope_bwd", grid=(3, 3),
        in_specs=[pl.BlockSpec((None, None, 4, T, LANES), lambda g, w: (g, w, 0, 0, 0)), tab, tab,
                  pl.BlockSpec(memory_space=pl.ANY)],
        out_specs=pl.BlockSpec((T, wide), lambda g, w: (0, (2 * CONV_CH) // wide + w * 3 + g)),
        out_shape=jax.ShapeDtypeStruct((T, EVEN_IN), BF16),
        input_output_aliases={3: 0},
        compiler_params=_params("parallel", "parallel"),
    )(dqkv, rope_c, rope_s, dz)


ODD_TILE = 256
ODD_HALO = 8
GELU_C = 0.7978845608028654
GELU_A = 0.044715


def _gelu(x):
    return 0.5 * x * (1.0 + jnp.tanh(GELU_C * (x + GELU_A * x * x * x)))


def _gelu_grad(x):
    th = jnp.tanh(GELU_C * (x + GELU_A * x * x * x))
    return 0.5 * (1.0 + th) + 0.5 * x * (1.0 - th * th) * GELU_C * (1.0 + 3.0 * GELU_A * x * x)


def _tril():
    row = lax.broadcasted_iota(jnp.int32, (CHUNK, CHUNK), 0)
    col = lax.broadcasted_iota(jnp.int32, (CHUNK, CHUNK), 1)
    return (col <= row).astype(F32)


def _odd_parts(z, zh, i, k_ref, g_ref, be_ref, w_ref, bt_ref):
    R, H = ODD_TILE, ODD_HALO
    gb, gc, xs, uv = z[:, :512], z[:, 512:1024], z[:, 1024:1536], z[:, 1536:]
    halo = zh[:, 512:1024] * zh[:, 1024:1536] * (i > 0).astype(F32)
    win = jnp.concatenate([halo, gc * xs], axis=0)
    cv = jnp.zeros((R, 512), F32)
    for j in range(SCONV_W):
        off = H - (SCONV_W - 1) + j
        cv = cv + k_ref[j:j + 1, :] * win[off:off + R, :]
    ge = _gelu(uv)
    u, v = ge[:, :512], ge[:, 512:]
    mu = jnp.mean(v, axis=-1, keepdims=True)
    xc = v - mu
    rstd = lax.rsqrt(jnp.mean(xc * xc, axis=-1, keepdims=True) + EPS)
    xh = xc * rstd
    vn = xh * g_ref[...] + be_ref[...]
    tril = _tril()
    wms = [(w_ref[g] * tril).astype(BF16) for g in range(SG_GROUPS)]
    rows = []
    for ci in range(R // CHUNK):
        blocks = []
        for g in range(SG_GROUPS):
            blk = vn[ci * CHUNK:(ci + 1) * CHUNK, g * LANES:(g + 1) * LANES].astype(BF16)
            blocks.append(_nn(wms[g], blk) + bt_ref[:, g:g + 1])
        rows.append(jnp.concatenate(blocks, axis=1))
    vmix = jnp.concatenate(rows, axis=0)
    return gb, gc, xs, uv, win, cv, u, rstd, xh, vn, vmix, wms


def _odd_mid_fwd(z, conv_k, ln_g, ln_b, sg_w, sg_bt):
    R, H = ODD_TILE, ODD_HALO

    def body(z_ref, zh_ref, k_ref, g_ref, be_ref, w_ref, bt_ref, o_ref):
        i = pl.program_id(0)
        gb, _, _, _, _, cv, u, _, _, _, vmix, _ = _odd_parts(z_ref[...], zh_ref[...], i, k_ref, g_ref, be_ref, w_ref, bt_ref)
        o_ref[...] = jnp.concatenate([gb * cv, u * vmix], axis=1).astype(BF16)

    vec = pl.BlockSpec((1, 512), lambda i: (0, 0))
    return _pcall(
        body, name="odd_mid_fwd", grid=(T // R,),
        in_specs=[pl.BlockSpec((R, ODD_IN), lambda i: (i, 0)),
                  pl.BlockSpec((H, ODD_IN), lambda i: (jnp.maximum(i * (R // H) - 1, 0), 0)),
                  pl.BlockSpec((SCONV_W, 512), lambda i: (0, 0)), vec, vec,
                  pl.BlockSpec((SG_GROUPS, CHUNK, CHUNK), lambda i: (0, 0, 0)),
                  pl.BlockSpec((CHUNK, SG_GROUPS), lambda i: (0, 0))],
        out_specs=pl.BlockSpec((R, D), lambda i: (i, 0)),
        out_shape=jax.ShapeDtypeStruct((T, D), BF16),
        compiler_params=_params("parallel"),
    )(z, z, conv_k, ln_g, ln_b, sg_w, sg_bt)


def _odd_mid_bwd(z, dcat, conv_k, ln_g, ln_b, sg_w, sg_bt):
    R, H = ODD_TILE, ODD_HALO
    last = T // R - 1

    def body(z_ref, zh_ref, zn_ref, d_ref, dn_ref, k_ref, g_ref, be_ref, w_ref, bt_ref,
             dz_ref, dk_ref, dg_ref, dbe_ref, dw_ref, dbt_ref):
        i = pl.program_id(0)
        z = z_ref[...]
        gb, gc, xs, uv, win, cv, u, rstd, xh, vn, vmix, wms = _odd_parts(z, zh_ref[...], i, k_ref, g_ref, be_ref, w_ref, bt_ref)
        dcat_t = d_ref[...]
        dc, dd = dcat_t[:, :512], dcat_t[:, 512:]

        @pl.when(i == 0)
        def _():
            dk_ref[...] = jnp.zeros_like(dk_ref)
            dg_ref[...] = jnp.zeros_like(dg_ref)
            dbe_ref[...] = jnp.zeros_like(dbe_ref)
            dw_ref[...] = jnp.zeros_like(dw_ref)
            dbt_ref[...] = jnp.zeros_like(dbt_ref)

        dgb = dc * cv
        dcv = dc * gb
        nxt = dn_ref[:, :512] * zn_ref[:, :512] * (i < last).astype(F32)
        winb = jnp.concatenate([dcv, nxt], axis=0)
        dp = jnp.zeros((R, 512), F32)
        for j in range(SCONV_W):
            off = H - (SCONV_W - 1) + j
            dk_ref[j:j + 1, :] += jnp.sum(dcv * win[off:off + R, :], axis=0, keepdims=True)
            ob = SCONV_W - 1 - j
            dp = dp + k_ref[j:j + 1, :] * winb[ob:ob + R, :]
        dgc = dp * xs
        dxs = dp * gc
        du = dd * vmix
        dvmix = dd * u
        tril = _tril()
        rows = []
        for ci in range(R // CHUNK):
            blocks = []
            for g in range(SG_GROUPS):
                sl = (slice(ci * CHUNK, (ci + 1) * CHUNK), slice(g * LANES, (g + 1) * LANES))
                dblk = dvmix[sl]
                dblk16 = dblk.astype(BF16)
                blocks.append(_tn(wms[g], dblk16))
                dw_ref[g] += _nt(dblk16, vn[sl].astype(BF16)) * tril
                dbt_ref[:, g:g + 1] += jnp.sum(dblk, axis=1, keepdims=True)
            rows.append(jnp.concatenate(blocks, axis=1))
        dvn = jnp.concatenate(rows, axis=0)
        dg_ref[...] += jnp.sum(dvn * xh, axis=0, keepdims=True)
        dbe_ref[...] += jnp.sum(dvn, axis=0, keepdims=True)
        dxh = dvn * g_ref[...]
        dv = rstd * (dxh - jnp.mean(dxh, axis=-1, keepdims=True) - xh * jnp.mean(dxh * xh, axis=-1, keepdims=True))
        duv = jnp.concatenate([du, dv], axis=1) * _gelu_grad(uv)
        dz_ref[...] = jnp.concatenate([dgb, dgc, dxs, duv], axis=1).astype(BF16)

    vec = pl.BlockSpec((1, 512), lambda i: (0, 0))
    kspec = pl.BlockSpec((SCONV_W, 512), lambda i: (0, 0))
    wspec = pl.BlockSpec((SG_GROUPS, CHUNK, CHUNK), lambda i: (0, 0, 0))
    bspec = pl.BlockSpec((CHUNK, SG_GROUPS), lambda i: (0, 0))
    nxt_blk = lambda i: (jnp.minimum((i + 1) * (R // H), T // H - 1), 0)
    return _pcall(
        body, name="odd_mid_bwd", grid=(T // R,),
        in_specs=[pl.BlockSpec((R, ODD_IN), lambda i: (i, 0)),
                  pl.BlockSpec((H, ODD_IN), lambda i: (jnp.maximum(i * (R // H) - 1, 0), 0)),
                  pl.BlockSpec((H, ODD_IN), nxt_blk),
                  pl.BlockSpec((R, D), lambda i: (i, 0)),
                  pl.BlockSpec((H, D), nxt_blk),
                  kspec, vec, vec, wspec, bspec],
        out_specs=[pl.BlockSpec((R, ODD_IN), lambda i: (i, 0)), kspec, vec, vec, wspec, bspec],
        out_shape=[jax.ShapeDtypeStruct((T, ODD_IN), BF16), jax.ShapeDtypeStruct((SCONV_W, 512), F32),
                   jax.ShapeDtypeStruct((1, 512), F32), jax.ShapeDtypeStruct((1, 512), F32),
                   jax.ShapeDtypeStruct((SG_GROUPS, CHUNK, CHUNK), F32), jax.ShapeDtypeStruct((CHUNK, SG_GROUPS), F32)],
        compiler_params=_params("arbitrary"),
    )(z, z, z, dcat, dcat, conv_k, ln_g, ln_b, sg_w, sg_bt)


def _ffn_up(tag, hn, weight):
    def act(acc):
        r = jnp.maximum(acc, 0.0)
        return (r * r,)

    return _mm(f"ffn{tag}_up", "nn", hn, weight(f"ffn_w1_{tag}", hn), T, D_FF, D, (BF16,), epi=act)


def _ffn_bwd(tag, h, g, weight, emit, saved, dout):
    hn, f = saved
    du = _mm(f"ffn{tag}_dact", "nt", dout, weight(f"ffn_w2_{tag}", dout), T, D_FF, D, (BF16,),
             epi=lambda acc, ff: (acc * (2.0 * jnp.sqrt(ff.astype(F32))),), extras=(f,))
    tok = emit(f"ffn_w2_{tag}", _mm(f"ffn{tag}_dw2", "tn", f, dout, D_FF, D, T, (BF16,)))
    tok = emit(f"ffn_w1_{tag}", _mm(f"ffn{tag}_dw1", "tn", hn, du, D, D_FF, T, (BF16,), tie=tok))
    return _mm_dx_norm(f"ffn{tag}_dhn", du, weight(f"ffn_w1_{tag}", du), D_FF, h, g, dout, tie=tok)


def _rope_tables():
    half = HEAD_DIM // 2
    inv = 10000.0 ** (-jnp.arange(half, dtype=F32) / half)
    ang = jnp.arange(T, dtype=F32)[:, None] * inv[None, :]
    cos, sin = jnp.cos(ang), jnp.sin(ang)
    c = jnp.tile(jnp.concatenate([cos, cos], axis=1), (1, LANES // HEAD_DIM))
    s = jnp.tile(jnp.concatenate([-sin, sin], axis=1), (1, LANES // HEAD_DIM))
    return c, s


def _local_step(x, target, p, weight, emit):
    rope_c, rope_s = _rope_tables()
    grads = {}
    residual = lambda acc, res: (acc + res,)

    hn0 = _rms_fwd("mix0_norm", x, p["norm_mix_g0"])
    zc = _mm("even_in_conv", "nn", hn0, weight("even_w_in", hn0), T, 2 * CONV_CH, D, (F32,))
    qkv = _qkv_proj(hn0, weight("even_w_in", hn0), rope_c, rope_s)
    cv, cat0 = _econv_fwd(zc, p["even_conv_k"], p["even_conv_b"], p["even_ln_g"], p["even_ln_b"])
    att_parts = [_attn_fwd(qkv, g) for g in range(3)]
    outs = [a[0] for a in att_parts]
    lses = [a[1] for a in att_parts]
    cat0, att, w0, w1, w2 = _attn_merge(outs, lses, cat0)
    wgts = (w0, w1, w2)
    h1, hnf0 = _mm_out_norm("even_out", cat0, weight("even_w_out", cat0), D, x, p["norm_ffn_g0"])
    f0 = _ffn_up(0, hnf0, weight)
    h2, hn1 = _mm_out_norm("ffn0_down", f0, weight("ffn_w2_0", f0), D_FF, h1, p["norm_mix_g1"])

    z1 = _mm("odd_in", "nn", hn1, weight("odd_w_in", hn1), T, ODD_IN, D, (F32,))
    cat1 = _odd_mid_fwd(z1, p["odd_conv_k"], p["odd_ln_g"], p["odd_ln_b"], p["odd_sg_w"], p["odd_sg_bt"])
    h3, hnf1 = _mm_out_norm("odd_out", cat1, weight("odd_w_out", cat1), D, h2, p["norm_ffn_g1"])
    f1 = _ffn_up(1, hnf1, weight)
    h4 = _mm("ffn1_down", "nn", f1, weight("ffn_w2_1", f1), T, D, D_FF, (F32,), epi=residual, extras=(h3,))

    dh4, grads["final_g"], loss = _loss_head(h4, p["final_g"], target)

    dh3, grads["norm_ffn_g1"] = _ffn_bwd(1, h3, p["norm_ffn_g1"], weight, emit, (hnf1, f1), dh4)
    tok = emit("odd_w_out", _mm("odd_out_dw", "tn", cat1, dh3, D, D, T, (BF16,)))
    dcat1 = _mm("odd_out_dx", "nt", dh3, weight("odd_w_out", dh3), T, D, D, (F32,), tie=tok)
    dz1, grads["odd_conv_k"], grads["odd_ln_g"], grads["odd_ln_b"], grads["odd_sg_w"], grads["odd_sg_bt"] = _odd_mid_bwd(
        z1, dcat1, p["odd_conv_k"], p["odd_ln_g"], p["odd_ln_b"], p["odd_sg_w"], p["odd_sg_bt"])
    tok = emit("odd_w_in", _mm("odd_in_dw", "tn", hn1, dz1, D, ODD_IN, T, (BF16,)))
    dh2, grads["norm_mix_g1"] = _mm_dx_norm("odd_in_dx", dz1, weight("odd_w_in", dz1), ODD_IN, h2, p["norm_mix_g1"],
                                            dh3, tie=tok)

    dh1, grads["norm_ffn_g0"] = _ffn_bwd(0, h1, p["norm_ffn_g0"], weight, emit, (hnf0, f0), dh2)
    tok = emit("even_w_out", _mm("even_out_dw", "tn", cat0, dh1, D, D, T, (BF16,)))
    dcat0 = _mm("even_out_dx", "nt", dh1, weight("even_w_out", dh1), T, D, D, (F32,), tie=tok)
    dcv, grads["even_ln_g"], grads["even_ln_b"], grads["even_conv_b"] = _econv_bwd_ln(
        cv, dcat0, p["even_ln_g"], p["even_ln_b"])
    dz0, grads["even_conv_k"] = _econv_bwd_conv(dcv, zc, p["even_conv_k"])
    dqkv = lax.empty((3, 3, 4, T, LANES), F32)
    for g in range(3):
        dqkv = _attn_bwd(qkv, lses[g], wgts[g], att, dcat0, dqkv, g)
    dz0 = _rope_bwd(dqkv, rope_c, rope_s, dz0)
    tok = emit("even_w_in", _mm("even_in_dw", "tn", hn0, dz0, D, EVEN_IN, T, (BF16,)))
    dx, grads["norm_mix_g0"] = _mm_dx_norm("even_in_dx", dz0, weight("even_w_in", dz0), EVEN_IN, x, p["norm_mix_g0"],
                                           dh1, tie=tok)
    return loss, dx, grads


def _rowwise(name, fn, ins, out_dtypes, tm=256):
    rows, cols = ins[0].shape
    tm = tm if rows % tm == 0 else rows
    n_in = len(ins)

    def body(*refs):
        vals = fn(*[r[...] for r in refs[:n_in]])
        for o_ref, v in zip(refs[n_in:], vals):
            o_ref[...] = v.astype(o_ref.dtype)

    spec = pl.BlockSpec((tm, cols), lambda i: (i, 0))
    outs = _pcall(
        body, name=name, grid=(rows // tm,),
        in_specs=[spec] * n_in, out_specs=[spec] * len(out_dtypes),
        out_shape=[jax.ShapeDtypeStruct((rows, cols), dt) for dt in out_dtypes],
        compiler_params=_params("parallel"),
    )(*ins)
    return outs[0] if len(out_dtypes) == 1 else outs


def _adamw(name, w, g, m, v):
    c1 = 1.0 - ADAM_B1 ** ADAM_STEP
    c2 = 1.0 - ADAM_B2 ** ADAM_STEP

    def fn(w_t, g_t, m_t, v_t):
        m_new = ADAM_B1 * m_t + (1.0 - ADAM_B1) * g_t
        v_new = ADAM_B2 * v_t + (1.0 - ADAM_B2) * (g_t * g_t)
        delta = -ADAM_LR * ((m_new / c1) / (jnp.sqrt(v_new / c2) + ADAM_EPS) + ADAM_WD * w_t)
        return delta, m_new, v_new

    return _rowwise(name, fn, (w, g, m, v), (F32, F32, F32))


class _Piece:
    def __init__(self, name, rows, cols, axis, src, src_row0):
        self.name, self.rows, self.cols, self.axis = name, rows, cols, axis
        self.width = (cols if axis == 1 else rows) // 4
        self.src, self.src_row0 = src, src_row0

    @property
    def full_shape(self):
        return (self.rows, self.cols)

    @property
    def half_shape(self):
        return (self.rows // 2, self.cols) if self.axis == 1 else (self.rows, self.cols // 2)

    @property
    def shard_half_shape(self):
        return (self.rows // 2, self.width) if self.axis == 1 else (self.width, self.cols // 2)

    def shard_whole(self, ref):
        n = self.rows if self.axis == 1 else self.width
        return ref.at[pl.ds(self.src_row0, n), :]

    def shard_half(self, ref, h):
        if self.axis == 1:
            return ref.at[pl.ds(self.src_row0 + h * (self.rows // 2), self.rows // 2), :]
        return ref.at[pl.ds(self.src_row0, self.width), pl.ds(h * (self.cols // 2), self.cols // 2)]

    def full_shard(self, ref, s):
        if self.axis == 1:
            return ref.at[:, pl.ds(s * self.width, self.width)]
        return ref.at[pl.ds(s * self.width, self.width), :]

    def full_shard_half(self, ref, s, h):
        if self.axis == 1:
            return ref.at[pl.ds(h * (self.rows // 2), self.rows // 2), pl.ds(s * self.width, self.width)]
        return ref.at[pl.ds(s * self.width, self.width), pl.ds(h * (self.cols // 2), self.cols // 2)]

    def full_half(self, ref, h):
        if self.axis == 1:
            return ref.at[pl.ds(h * (self.rows // 2), self.rows // 2), :]
        return ref.at[:, pl.ds(h * (self.cols // 2), self.cols // 2)]

    def full_half_rows(self, ref, h, r0, n):
        if self.axis == 1:
            return ref.at[pl.ds(h * (self.rows // 2) + r0, n), :]
        return ref.at[pl.ds(r0, n), pl.ds(h * (self.cols // 2), self.cols // 2)]

    def half_shard(self, ref, s):
        return self.full_shard(ref, s)


PIECES = (
    _Piece("even_w_in", D, EVEN_IN, 1, 0, 0),
    _Piece("even_w_out", D, D, 0, 1, 0),
    _Piece("ffn_w1_0", D, D_FF, 1, 4, 0),
    _Piece("ffn_w2_0", D_FF, D, 0, 5, 0),
    _Piece("odd_w_in", D, ODD_IN, 1, 2, 0),
    _Piece("odd_w_out", D, D, 0, 3, 0),
    _Piece("ffn_w1_1", D, D_FF, 1, 4, D),
    _Piece("ffn_w2_1", D_FF, D, 0, 5, D_FF // 4),
)
N_PIECES = len(PIECES)
FORWARD_GROUPS = ((0,), (1, 2, 3), (4, 5, 6, 7))
JOIN_GROUPS = ((0, 1, 2, 3), (4, 5))
HOLD_BACK = ("ffn_w2_0", "ffn_w2_1")
N_SHARD_OPERANDS = 6
ANY = pl.BlockSpec(memory_space=pl.ANY)
MESH = pl.DeviceIdType.MESH


def _mesh_place():
    x, y, c = lax.axis_index("x"), lax.axis_index("y"), lax.axis_index("c")
    chips = [(1 - x, y), (x, 1 - y), (1 - x, 1 - y)]
    return x, y, c, chips


def _remote(src, dst, send_sem, recv_sem, dev):
    return pltpu.make_async_remote_copy(src_ref=src, dst_ref=dst, send_sem=send_sem, recv_sem=recv_sem,
                                        device_id=dev, device_id_type=MESH)


HBM = pl.BlockSpec(memory_space=pltpu.HBM)
SEM = pl.BlockSpec(memory_space=pltpu.SEMAPHORE)
SPLIT_PARAMS = pltpu.CompilerParams(has_side_effects=pltpu.SideEffectType.DATAFLOW_SIDE_EFFECTING)
CAST_TILE = 256


def _in_hbm(a):
    return pltpu.with_memory_space_constraint(a, pltpu.HBM)


def _cast_place(pc, shard_operand, chip, tie=None):
    rows, cols = (pc.rows, pc.width) if pc.axis == 1 else (pc.width, pc.cols)
    nblk = rows // CAST_TILE
    blk0 = pc.src_row0 // CAST_TILE
    ties = () if tie is None else (tie,)

    def body(chip_ref, x_ref, *rest):
        del chip_ref
        rest[-1][...] = x_ref[...].astype(BF16)

    if pc.axis == 1:
        out_map = lambda i, chip_ref: (i, chip_ref[0])
    else:
        out_map = lambda i, chip_ref: (chip_ref[0] * nblk + i, 0)
    return _pcall(
        body, name=f"cast_{pc.name}",
        grid_spec=pltpu.PrefetchScalarGridSpec(
            num_scalar_prefetch=1, grid=(nblk,),
            in_specs=[pl.BlockSpec((CAST_TILE, cols), lambda i, chip_ref: (blk0 + i, 0))]
            + [pl.BlockSpec(TOKEN_SHAPE, lambda i, chip_ref: (0, 0))] * len(ties),
            out_specs=pl.BlockSpec((CAST_TILE, cols), out_map)),
        out_shape=jax.ShapeDtypeStruct(pc.full_shape, BF16),
        compiler_params=_params("parallel"),
    )(chip, shard_operand, *ties)


def _gather_start(name, pieces, fulls):
    n = len(pieces)

    def body(*refs):
        ins = refs[:n]
        sends = refs[2 * n:3 * n]
        recvs = refs[3 * n:4 * n]
        token = refs[4 * n]
        x, y, c, chips = _mesh_place()
        s = 2 * x + y
        for i, pc in enumerate(pieces):
            win = pc.full_shard_half(ins[i], s, c)
            for k, (cx, cy) in enumerate(chips):
                _remote(win, win, sends[i].at[k], recvs[i].at[k], (cx, cy, c)).start()
        token[...] = jnp.zeros(TOKEN_SHAPE, F32)

    sems = [pltpu.SemaphoreType.DMA((3,))] * (2 * n)
    outs = _pcall(
        body, name=name,
        in_specs=[HBM] * n,
        out_specs=[HBM] * n + [SEM] * (2 * n) + [pl.BlockSpec(memory_space=pltpu.VMEM)],
        out_shape=[pltpu.HBM(pc.full_shape, BF16) for pc in pieces] + sems + [jax.ShapeDtypeStruct(TOKEN_SHAPE, F32)],
        input_output_aliases={i: i for i in range(n)},
        compiler_params=SPLIT_PARAMS,
    )(*[_in_hbm(f) for f in fulls])
    return outs[:n], outs[n:2 * n], outs[2 * n:3 * n], outs[3 * n]


def _gather_wait(pc, full, send_sems, recv_sems, after):
    def body(full_ref, send_ref, recv_ref, after_ref, out_ref):
        del after_ref, out_ref
        x, y, c, chips = _mesh_place()
        for k, (cx, cy) in enumerate(chips):
            win = pc.full_shard_half(full_ref, 2 * cx + cy, c)
            cp = _remote(win, win, send_ref.at[k], recv_ref.at[k], (cx, cy, c))
            cp.wait_send()
            cp.wait_recv()

    return _pcall(
        body, name=f"gather_wait_{pc.name}",
        in_specs=[HBM, SEM, SEM, ANY], out_specs=HBM, out_shape=pltpu.HBM(pc.full_shape, BF16),
        input_output_aliases={0: 0}, compiler_params=SPLIT_PARAMS,
    )(full, send_sems, recv_sems, after)


def _core_forward(pieces, fulls):
    n = len(pieces)

    def body(*refs):
        ins, outs = refs[:n], refs[n:2 * n]
        send_bufs, recv_bufs = refs[2 * n:3 * n], refs[3 * n:4 * n]
        load_sems, send_sems, recv_sems, store_sems = refs[4 * n:]
        x, y, c, chips = _mesh_place()
        loads, sends, stores = [], [], []
        for i, pc in enumerate(pieces):
            for k, (cx, cy) in enumerate(chips):
                cp = pltpu.make_async_copy(pc.full_shard_half(ins[i], 2 * cx + cy, c), send_bufs[i].at[k],
                                           load_sems.at[3 * i + k])
                cp.start()
                loads.append(cp)
        for i in range(n):
            for k in range(3):
                j = 3 * i + k
                loads[j].wait()
                cp = _remote(send_bufs[i].at[k], recv_bufs[i].at[k], send_sems.at[j], recv_sems.at[j], (x, y, 1 - c))
                cp.start()
                sends.append(cp)
        for i, pc in enumerate(pieces):
            for k, (cx, cy) in enumerate(chips):
                j = 3 * i + k
                sends[j].wait_recv()
                cp = pltpu.make_async_copy(recv_bufs[i].at[k], pc.full_shard_half(outs[i], 2 * cx + cy, 1 - c),
                                           store_sems.at[j])
                cp.start()
                stores.append(cp)
        for j in range(3 * n):
            sends[j].wait_send()
            stores[j].wait()

    sems = pltpu.SemaphoreType.DMA((3 * n,))
    bufs = [pltpu.VMEM((3,) + pc.shard_half_shape, BF16) for pc in pieces]
    return _pcall(
        body, name="core_forward_" + pieces[0].name, in_specs=[ANY] * n, out_specs=[ANY] * n,
        out_shape=[jax.ShapeDtypeStruct(pc.full_shape, BF16) for pc in pieces],
        scratch_shapes=bufs + bufs + [sems, sems, sems, sems],
        input_output_aliases={i: i for i in range(n)},
        compiler_params=pltpu.CompilerParams(vmem_limit_bytes=VMEM_LIMIT),
    )(*fulls)


CHIPSUM_CHUNKS = 4


def _chipsum(pieces, partials):
    n = len(pieces)
    nch = CHIPSUM_CHUNKS

    def body(*refs):
        g_refs, out_refs = refs[:n], refs[n:2 * n]
        bufs = refs[2 * n:6 * n]
        load_sems, own_sems, send_sems, recv_sems, out_sems = refs[6 * n:]
        x, y, c, _ = _mesh_place()
        loads, owns, sends, stores, rows_of = [], [], [], [], []
        for i, pc in enumerate(pieces):
            send_buf, own_buf = bufs[4 * i], bufs[4 * i + 2]
            ch = pc.half_shape[0] // nch
            for k in range(nch):
                rows = pl.ds(k * ch, ch)
                rows_of.append(rows)
                cp = pltpu.make_async_copy(pc.full_half_rows(g_refs[i], 1 - c, k * ch, ch), send_buf.at[rows, :],
                                           load_sems.at[nch * i + k])
                cp.start()
                loads.append(cp)
                cp = pltpu.make_async_copy(pc.full_half_rows(g_refs[i], c, k * ch, ch), own_buf.at[rows, :],
                                           own_sems.at[nch * i + k])
                cp.start()
                owns.append(cp)
        for i in range(n):
            send_buf, recv_buf = bufs[4 * i], bufs[4 * i + 1]
            for k in range(nch):
                j = nch * i + k
                loads[j].wait()
                cp = _remote(send_buf.at[rows_of[j], :], recv_buf.at[rows_of[j], :], send_sems.at[j], recv_sems.at[j],
                             (x, y, 1 - c))
                cp.start()
                sends.append(cp)
        for i in range(n):
            recv_buf, own_buf, sum_buf = bufs[4 * i + 1], bufs[4 * i + 2], bufs[4 * i + 3]
            for k in range(nch):
                j = nch * i + k
                rows = rows_of[j]
                owns[j].wait()
                sends[j].wait_recv()
                sum_buf[rows, :] = (own_buf[rows, :].astype(F32) + recv_buf[rows, :].astype(F32)).astype(BF16)
                cp = pltpu.make_async_copy(sum_buf.at[rows, :], out_refs[i].at[rows, :], out_sems.at[j])
                cp.start()
                stores.append(cp)
        for j in range(nch * n):
            sends[j].wait_send()
            stores[j].wait()

    sems = pltpu.SemaphoreType.DMA((nch * n,))
    return _pcall(
        body, name="chipsum_" + pieces[0].name, in_specs=[ANY] * n, out_specs=[ANY] * n,
        out_shape=[jax.ShapeDtypeStruct(pc.half_shape, BF16) for pc in pieces],
        scratch_shapes=[pltpu.VMEM(pc.half_shape, BF16) for pc in pieces for _ in range(4)] + [sems] * 5,
        compiler_params=pltpu.CompilerParams(vmem_limit_bytes=VMEM_LIMIT),
    )(*partials)


def _scatter_start(pieces, chip_sums):
    n = len(pieces)

    def body(*refs):
        sums, lands = refs[:n], refs[n:2 * n]
        sends, recvs = refs[4 * n:5 * n], refs[5 * n:6 * n]
        token = refs[6 * n]
        x, y, c, chips = _mesh_place()
        for i, pc in enumerate(pieces):
            for k, (cx, cy) in enumerate(chips):
                _remote(pc.half_shard(sums[i], 2 * cx + cy), lands[i].at[k], sends[i].at[k], recvs[i].at[k],
                        (cx, cy, c)).start()
        token[...] = jnp.zeros(TOKEN_SHAPE, F32)

    land_shapes = [(3,) + pc.shard_half_shape for pc in pieces]
    sems = [pltpu.SemaphoreType.DMA((3,))] * (2 * n)
    outs = _pcall(
        body, name="scatter_start_" + pieces[0].name,
        in_specs=[HBM] * (2 * n), out_specs=[HBM] * (2 * n) + [SEM] * (2 * n) + [pl.BlockSpec(memory_space=pltpu.VMEM)],
        out_shape=[pltpu.HBM(pc.half_shape, BF16) for pc in pieces] + [pltpu.HBM(sh, BF16) for sh in land_shapes]
        + sems + [jax.ShapeDtypeStruct(TOKEN_SHAPE, F32)],
        input_output_aliases={i: i for i in range(2 * n)}, compiler_params=SPLIT_PARAMS,
    )(*[_in_hbm(cs) for cs in chip_sums], *[_in_hbm(lax.empty(sh, BF16)) for sh in land_shapes])
    return [(outs[i], outs[n + i], outs[2 * n + i], outs[3 * n + i]) for i in range(n)], outs[4 * n]


def _scatter_wait(pc, chip_sum, land, send_sems, recv_sems, after):
    def body(sum_ref, land_ref, send_ref, recv_ref, after_ref, sum_out, land_out):
        del after_ref, sum_out, land_out
        x, y, c, chips = _mesh_place()
        for k, (cx, cy) in enumerate(chips):
            cp = _remote(pc.half_shard(sum_ref, 2 * cx + cy), land_ref.at[k], send_ref.at[k], recv_ref.at[k], (cx, cy, c))
            cp.wait_send()
            cp.wait_recv()

    return _pcall(
        body, name=f"scatter_wait_{pc.name}",
        in_specs=[HBM, HBM, SEM, SEM, ANY], out_specs=[HBM, HBM],
        out_shape=[pltpu.HBM(pc.half_shape, BF16), pltpu.HBM((3,) + pc.shard_half_shape, BF16)],
        input_output_aliases={0: 0, 1: 1}, compiler_params=SPLIT_PARAMS,
    )(chip_sum, land, send_sems, recv_sems, after)


SHARD_OPERAND_SHAPES = ((D, EVEN_IN // 4), (D // 4, D), (D, ODD_IN // 4), (D // 4, D), (2 * D, D_FF // 4), (2 * D_FF // 4, D))


def _allsum_join(operands, chip_sums, lands):
    pieces = [pc for pc in PIECES if pc.src in operands]
    n = len(pieces)
    n_out = len(operands)

    def body(*refs):
        sum_refs = refs[:n]
        land_refs = refs[n:2 * n]
        out_refs = dict(zip(operands, refs[2 * n:2 * n + n_out]))
        refs = refs[2 * n + n_out:]
        in_bufs, fin_bufs, recv_bufs = refs[:n], refs[n:2 * n], refs[2 * n:3 * n]
        load_sems, send_sems, recv_sems, out_sems = refs[3 * n:]
        x, y, c, _ = _mesh_place()
        s = 2 * x + y
        loads, sends, stores = [], [], []
        for j, pc in enumerate(pieces):
            cp = pltpu.make_async_copy(land_refs[j], in_bufs[j].at[pl.ds(0, 3)], load_sems.at[2 * j])
            cp.start()
            loads.append(cp)
            cp = pltpu.make_async_copy(pc.half_shard(sum_refs[j], s), in_bufs[j].at[3], load_sems.at[2 * j + 1])
            cp.start()
            loads.append(cp)
        for j, pc in enumerate(pieces):
            loads[2 * j].wait()
            loads[2 * j + 1].wait()
            acc = in_bufs[j][0].astype(F32)
            for k in range(1, 4):
                acc = acc + in_bufs[j][k].astype(F32)
            fin_bufs[j][...] = acc
            cp = pltpu.make_async_copy(fin_bufs[j], pc.shard_half(out_refs[pc.src], c), out_sems.at[2 * j])
            cp.start()
            stores.append(cp)
            cp = _remote(fin_bufs[j], recv_bufs[j], send_sems.at[j], recv_sems.at[j], (x, y, 1 - c))
            cp.start()
            sends.append(cp)
        for j, pc in enumerate(pieces):
            sends[j].wait_recv()
            cp = pltpu.make_async_copy(recv_bufs[j], pc.shard_half(out_refs[pc.src], 1 - c), out_sems.at[2 * j + 1])
            cp.start()
            stores.append(cp)
        for cp in sends:
            cp.wait_send()
        for cp in stores:
            cp.wait()

    halves = [pc.shard_half_shape for pc in pieces]
    return _pcall(
        body, name=f"allsum_join_{operands[0]}", in_specs=[ANY] * (2 * n), out_specs=[ANY] * n_out,
        out_shape=[jax.ShapeDtypeStruct(SHARD_OPERAND_SHAPES[o], F32) for o in operands],
        scratch_shapes=[pltpu.VMEM((4,) + sh, BF16) for sh in halves] + [pltpu.VMEM(sh, F32) for sh in halves] * 2
        + [pltpu.SemaphoreType.DMA((2 * n,)), pltpu.SemaphoreType.DMA((n,)), pltpu.SemaphoreType.DMA((n,)),
           pltpu.SemaphoreType.DMA((2 * n,))],
        compiler_params=pltpu.CompilerParams(vmem_limit_bytes=VMEM_LIMIT),
    )(*chip_sums, *lands)


PEER_FLIPS = tuple((a, b, e) for a in (0, 1) for b in (0, 1) for e in (0, 1) if (a, b, e) != (0, 0, 0))


def _peers():
    x, y, c = lax.axis_index("x"), lax.axis_index("y"), lax.axis_index("c")
    me = 4 * x + 2 * y + c
    out = []
    for a, b, e in PEER_FLIPS:
        px, py, pc = (1 - x if a else x), (1 - y if b else y), (1 - c if e else c)
        out.append(((px, py, pc), 4 * px + 2 * py + pc))
    return me, out


def _exchange8_start(name, blk):
    m = blk.shape[0]

    def body(blk_ref, land_ref, blk_out, land_out, sends, recvs, token):
        del blk_out, land_out
        me, peers = _peers()
        for k, (dev, _) in enumerate(peers):
            _remote(blk_ref, land_ref.at[me], sends.at[k], recvs.at[k], dev).start()
        token[...] = jnp.zeros(TOKEN_SHAPE, F32)

    sems = pltpu.SemaphoreType.DMA((7,))
    return _pcall(
        body, name=name,
        in_specs=[HBM, HBM], out_specs=[HBM, HBM, SEM, SEM, pl.BlockSpec(memory_space=pltpu.VMEM)],
        out_shape=[pltpu.HBM((m, LANES), F32), pltpu.HBM((8, m, LANES), F32), sems, sems,
                   jax.ShapeDtypeStruct(TOKEN_SHAPE, F32)],
        input_output_aliases={0: 0, 1: 1}, compiler_params=SPLIT_PARAMS,
    )(_in_hbm(blk), _in_hbm(lax.empty((8, m, LANES), F32)))


def _exchange8_wait(name, blk, land, send_sems, recv_sems, after):
    def body(blk_ref, land_ref, send_ref, recv_ref, after_ref, blk_out, land_out):
        del after_ref, blk_out, land_out
        _, peers = _peers()
        for k, (dev, slot) in enumerate(peers):
            cp = _remote(blk_ref, land_ref.at[slot], send_ref.at[k], recv_ref.at[k], dev)
            cp.wait_send()
            cp.wait_recv()

    m = blk.shape[0]
    return _pcall(
        body, name=name,
        in_specs=[HBM, HBM, SEM, SEM, ANY], out_specs=[HBM, HBM],
        out_shape=[pltpu.HBM((m, LANES), F32), pltpu.HBM((8, m, LANES), F32)],
        input_output_aliases={0: 0, 1: 1}, compiler_params=SPLIT_PARAMS,
    )(blk, land, send_sems, recv_sems, after)


def _collect8(name, blk, land, with_sum):
    m = blk.shape[0]

    def body(blk_ref, land_ref, out_ref, *scratch):
        sems = scratch[-1]
        dst = scratch[0] if with_sum else out_ref
        me, peers = _peers()
        copies = [pltpu.make_async_copy(blk_ref, dst.at[me], sems.at[7])]
        for k, (_, slot) in enumerate(peers):
            copies.append(pltpu.make_async_copy(land_ref.at[slot], dst.at[slot], sems.at[k]))
        for cp in copies:
            cp.start()
        for cp in copies:
            cp.wait()
        if with_sum:
            acc = dst[0]
            for dev in range(1, 8):
                acc = acc + dst[dev]
            out_ref[...] = acc

    all_shape = (8, m, LANES)
    return _pcall(
        body, name=name, in_specs=[ANY, ANY], out_specs=pl.BlockSpec(memory_space=pltpu.VMEM),
        out_shape=jax.ShapeDtypeStruct((m, LANES) if with_sum else all_shape, F32),
        scratch_shapes=([pltpu.VMEM(all_shape, F32)] if with_sum else []) + [pltpu.SemaphoreType.DMA((8,))],
    )(blk, land)


def _pack(arrays, row_counts):
    rows = []
    for a, n in zip(arrays, row_counts):
        flat = a.reshape(-1, LANES)
        rows.append(jnp.pad(flat, ((0, n - flat.shape[0]), (0, 0))))
    return jnp.concatenate(rows, axis=0)


def _unpack(buf, shapes, row_counts):
    out, r0 = [], 0
    for sh, n in zip(shapes, row_counts):
        size = 1
        for dim in sh:
            size *= dim
        out.append(buf[r0:r0 + size // LANES].reshape(sh))
        r0 += n
    return out


REPL_NAMES = ("norm_mix_g", "norm_ffn_g", "even_conv_b", "even_ln_g", "even_ln_b", "odd_sg_w", "odd_sg_b", "final_g")
REPL_SHAPES = ((2, D), (2, D), (1, 512), (1, 512), (1, 512), (1, SG_GROUPS, CHUNK, CHUNK), (1, SG_GROUPS, CHUNK), (D,))
REPL_ROWS = (16, 16, 8, 8, 8, 512, 8, 8)
SHARDED_NAMES = ("even_conv_k", "odd_conv_k", "odd_ln_g", "odd_ln_b")
SHARDED_SHARD_SHAPES = ((1, CONV_W, LANES), (1, SCONV_W, LANES), (1, LANES), (1, LANES))
SHARDED_SHARD_ROWS = (32, 8, 8, 8)
SHARDED_FULL_SHAPES = ((CONV_W, 512), (SCONV_W, 512), (1, 512), (1, 512))
SHARDED_FULL_ROWS = (128, 16, 8, 8)


def kernel(x, norm_mix_g, norm_ffn_g, even_w_in, even_conv_k, even_conv_b, even_ln_g, even_ln_b, even_w_out, odd_w_in, odd_conv_k, odd_ln_g, odd_ln_b, odd_sg_w, odd_sg_b, odd_w_out, ffn_w1, ffn_w2, final_g, loss_target, m_norm_mix_g, m_norm_ffn_g, m_even_w_in, m_even_conv_k, m_even_conv_b, m_even_ln_g, m_even_ln_b, m_even_w_out, m_odd_w_in, m_odd_conv_k, m_odd_ln_g, m_odd_ln_b, m_odd_sg_w, m_odd_sg_b, m_odd_w_out, m_ffn_w1, m_ffn_w2, m_final_g, v_norm_mix_g, v_norm_ffn_g, v_even_w_in, v_even_conv_k, v_even_conv_b, v_even_ln_g, v_even_ln_b, v_even_w_out, v_odd_w_in, v_odd_conv_k, v_odd_ln_g, v_odd_ln_b, v_odd_sg_w, v_odd_sg_b, v_odd_w_out, v_ffn_w1, v_ffn_w2, v_final_g):
    names = ("norm_mix_g", "norm_ffn_g", "even_w_in", "even_conv_k", "even_conv_b", "even_ln_g", "even_ln_b", "even_w_out",
             "odd_w_in", "odd_conv_k", "odd_ln_g", "odd_ln_b", "odd_sg_w", "odd_sg_b", "odd_w_out", "ffn_w1", "ffn_w2", "final_g")
    w = dict(zip(names, (norm_mix_g, norm_ffn_g, even_w_in, even_conv_k, even_conv_b, even_ln_g, even_ln_b, even_w_out,
                         odd_w_in, odd_conv_k, odd_ln_g, odd_ln_b, odd_sg_w, odd_sg_b, odd_w_out, ffn_w1, ffn_w2, final_g)))
    mom = dict(zip(names, (m_norm_mix_g, m_norm_ffn_g, m_even_w_in, m_even_conv_k, m_even_conv_b, m_even_ln_g, m_even_ln_b,
                           m_even_w_out, m_odd_w_in, m_odd_conv_k, m_odd_ln_g, m_odd_ln_b, m_odd_sg_w, m_odd_sg_b, m_odd_w_out,
                           m_ffn_w1, m_ffn_w2, m_final_g)))
    vel = dict(zip(names, (v_norm_mix_g, v_norm_ffn_g, v_even_w_in, v_even_conv_k, v_even_conv_b, v_even_ln_g, v_even_ln_b,
                           v_even_w_out, v_odd_w_in, v_odd_conv_k, v_odd_ln_g, v_odd_ln_b, v_odd_sg_w, v_odd_sg_b, v_odd_w_out,
                           v_ffn_w1, v_ffn_w2, v_final_g)))
    big_names = ("even_w_in", "even_w_out", "odd_w_in", "odd_w_out", "ffn_w1", "ffn_w2")
    chip = 2 * lax.axis_index("x") + lax.axis_index("y")

    def shard2d(t, name):
        return t[name].reshape(SHARD_OPERAND_SHAPES[big_names.index(name)])

    chip_op = jnp.reshape(chip, (1,)).astype(jnp.int32)
    small_pack = _pack([w[n] for n in SHARDED_NAMES], SHARDED_SHARD_ROWS)
    small_blk, small_land, small_send, small_recv, small_token = _exchange8_start("gather_small_start", small_pack)
    first = _cast_place(PIECES[0], shard2d(w, big_names[PIECES[0].src]), chip_op, tie=small_token)
    fly0, send0, recv0, token = _gather_start("gather_start_first", PIECES[:1], [first])
    placed = [_cast_place(pc, shard2d(w, big_names[pc.src]), chip_op, tie=token) for pc in PIECES[1:]]
    fly1, send1, recv1, all_started = _gather_start("gather_start_rest", PIECES[1:], placed)
    flying, gather_send, gather_recv = fly0 + fly1, send0 + send1, recv0 + recv1
    ready = {}

    names_in_order = [pc.name for pc in PIECES]

    def weight(name, after):
        if name not in ready:
            i = names_in_order.index(name)
            group = next(grp for grp in FORWARD_GROUPS if i in grp)
            if i == 0:
                after = all_started
            landed = [_gather_wait(PIECES[j], flying[j], gather_send[j], gather_recv[j], after) for j in group]
            ready.update(zip((PIECES[j].name for j in group), _core_forward([PIECES[j] for j in group], landed)))
        return ready[name]

    scattering = []
    held = []

    def emit(name, partial):
        held.append((PIECES[names_in_order.index(name)], partial))
        if name in HOLD_BACK:
            return None
        pieces = [pc for pc, _ in held]
        started, token = _scatter_start(pieces, _chipsum(pieces, [part for _, part in held]))
        scattering.extend((pc,) + tuple(st) for pc, st in zip(pieces, started))
        held.clear()
        return token

    full = {}
    small_blk, small_land = _exchange8_wait("gather_small_wait", small_blk, small_land, small_send, small_recv, all_started)
    gathered = _collect8("gather_small_collect", small_blk, small_land, False)
    gathered = gathered.reshape(4, 2, sum(SHARDED_SHARD_ROWS), LANES)[:, 0]
    r0 = 0
    for n, sh, rows, full_sh in zip(SHARDED_NAMES, SHARDED_SHARD_SHAPES, SHARDED_SHARD_ROWS, SHARDED_FULL_SHAPES):
        per_chip = gathered[:, r0:r0 + rows].reshape(4, -1)[:, :full_sh[0] * LANES].reshape(4, full_sh[0], LANES)
        full[n] = jnp.transpose(per_chip, (1, 0, 2)).reshape(full_sh)
        r0 += rows
    p = dict(full)
    p.update(norm_mix_g0=norm_mix_g[0:1], norm_mix_g1=norm_mix_g[1:2], norm_ffn_g0=norm_ffn_g[0:1], norm_ffn_g1=norm_ffn_g[1:2],
             even_conv_b=even_conv_b, even_ln_g=even_ln_g, even_ln_b=even_ln_b,
             odd_sg_w=odd_sg_w[0], odd_sg_bt=odd_sg_b[0].T, final_g=final_g[None, :])

    loss_row, dx, g = _local_step(x[0], loss_target[0], p, weight, emit)

    grad_parts = [loss_row, g["norm_mix_g0"], g["norm_mix_g1"], g["norm_ffn_g0"], g["norm_ffn_g1"], g["even_conv_b"],
                  g["even_ln_g"], g["even_ln_b"], g["odd_sg_w"], g["odd_sg_bt"].T, g["final_g"],
                  g["even_conv_k"], g["odd_conv_k"], g["odd_ln_g"], g["odd_ln_b"]]
    grad_pack = _pack(grad_parts, (8, 8, 8, 8, 8) + REPL_ROWS[2:] + SHARDED_FULL_ROWS)
    grad_blk, grad_land, grad_send, grad_recv, grad_token = _exchange8_start("allreduce_small_start", grad_pack)

    landed = {pc.name: _scatter_wait(pc, chip_sum, land, send_sems, recv_sems, grad_token)
              for pc, chip_sum, land, send_sems, recv_sems in scattering}
    big_grads = {}
    for operands in JOIN_GROUPS:
        pieces = [pc for pc in PIECES if pc.src in operands]
        joined = _allsum_join(operands, [landed[pc.name][0] for pc in pieces], [landed[pc.name][1] for pc in pieces])
        big_grads.update(zip((big_names[o] for o in operands), joined))

    grad_blk, grad_land = _exchange8_wait("allreduce_small_wait", grad_blk, grad_land, grad_send, grad_recv,
                                          big_grads[big_names[JOIN_GROUPS[-1][-1]]])
    grad_sum = _collect8("allreduce_small_sum", grad_blk, grad_land, True)
    loss = grad_sum[0, 0]
    parts = _unpack(grad_sum[8:], REPL_SHAPES + SHARDED_FULL_SHAPES, REPL_ROWS + SHARDED_FULL_ROWS)
    grads = dict(zip(REPL_NAMES, parts[:len(REPL_NAMES)]))
    for n, full_g, sh in zip(SHARDED_NAMES, parts[len(REPL_NAMES):], SHARDED_SHARD_SHAPES):
        grads[n] = lax.dynamic_slice_in_dim(full_g, chip * LANES, LANES, axis=1).reshape(sh)
    for n in big_names:
        grads[n] = big_grads[n].reshape(w[n].shape)

    delta, new_m, new_v = {}, {}, {}
    for n in big_names:
        d2, m2, v2 = _adamw(f"adamw_{n}", shard2d(w, n), big_grads[n], shard2d(mom, n), shard2d(vel, n))
        delta[n], new_m[n], new_v[n] = (t.reshape(w[n].shape) for t in (d2, m2, v2))
    for tag, group, rows, shapes in (("repl", REPL_NAMES, REPL_ROWS, [w[n].shape for n in REPL_NAMES]),
                                     ("sharded", SHARDED_NAMES, SHARDED_SHARD_ROWS, SHARDED_SHARD_SHAPES)):
        packs = [_pack([t[n] for n in group], rows) for t in (w, grads, mom, vel)]
        outs = _adamw(f"adamw_{tag}", *packs)
        for res, o in zip((delta, new_m, new_v), outs):
            res.update(zip(group, _unpack(o, shapes, rows)))

    out = [loss, dx[None]]
    for res in (grads, delta, new_m, new_v):
        out.extend(res[n] for n in names)
    return tuple(out)
```

```python
import functools

import jax
import jax.numpy as jnp
from jax import lax
from jax.experimental import pallas as pl
from jax.experimental.pallas import tpu as pltpu

F32 = jnp.float32
BF16 = jnp.bfloat16

T = 2048
D = 1024
CONV_CH = 512
CONV_W = 31
HEAD_DIM = 64
ATT_W = 1536
EVEN_IN = 5632
ODD_IN = 2560
SCONV_W = 3
SG_GROUPS = 4
CHUNK = 128
D_FF = 4096
EPS = 1e-6
DILATIONS = (1, 4, 16)
BAND = 128
SCALE = HEAD_DIM ** -0.5
NEG = -1e30

ADAM_LR = 0.001
ADAM_B1 = 0.9
ADAM_B2 = 0.999
ADAM_EPS = 1e-08
ADAM_WD = 0.01
ADAM_STEP = 10

V7X_VMEM_BYTES = 64 * 2 ** 20
VMEM_LIMIT = V7X_VMEM_BYTES - 8 * 2 ** 20
LANES = 128
TOKEN_SHAPE = (8, LANES)


def _pcall(body, **kw):
    return pl.pallas_call(body, **kw)


def _params(*sem):
    return pltpu.CompilerParams(dimension_semantics=sem, vmem_limit_bytes=VMEM_LIMIT)


def _dot(a, b, dims):
    return lax.dot_general(a, b, (dims, ((), ())), preferred_element_type=F32)


def _nn(a, b):
    return _dot(a, b, ((1,), (0,)))


def _nt(a, b):
    return _dot(a, b, ((1,), (1,)))


def _tn(a, b):
    return _dot(a, b, ((0,), (0,)))


def _sigmoid(x):
    return 1.0 / (1.0 + jnp.exp(-x))


MM_VMEM_BUDGET = 40 * 2 ** 20


def _mm_tiles(mode, m, n, k, a_bytes, b_bytes, extra_bytes, out_bytes):
    def divisors(total, unit):
        return [t for t in range(unit, total + 1, unit) if total % t == 0]

    best = None
    for tm in divisors(m, LANES if mode == "tn" else 8):
        for tn in divisors(n, LANES):
            blocks = tm * k * a_bytes + tn * k * b_bytes + tm * tn * (extra_bytes + out_bytes)
            casts = (tm * k * 2 if a_bytes == 4 else 0) + (tn * k * 2 if b_bytes == 4 else 0)
            if 2 * blocks + casts + tm * tn * 4 > MM_VMEM_BUDGET:
                continue
            key = ((m // tm) * (n // tn), (m // tm) * n * k * b_bytes, abs(tm - tn))
            if best is None or key < best[0]:
                best = (key, tm, tn)
    return best[1], best[2]


def _mm(name, mode, a, b, m, n, k, out_dtypes, *, b_off=0, extras=(), epi=None, tie=None):
    tm, tn = _mm_tiles(mode, m, n, k, a.dtype.itemsize, b.dtype.itemsize, sum(e.dtype.itemsize for e in extras),
                       sum(jnp.dtype(dt).itemsize for dt in out_dtypes))
    assert b_off % tn == 0
    b_off //= tn
    if mode == "nn":
        a_spec = pl.BlockSpec((tm, k), lambda i, j: (i, 0))
        b_spec = pl.BlockSpec((k, tn), lambda i, j: (0, j + b_off))
        dims = ((1,), (0,))
    elif mode == "nt":
        a_spec = pl.BlockSpec((tm, k), lambda i, j: (i, 0))
        b_spec = pl.BlockSpec((tn, k), lambda i, j: (j, 0))
        dims = ((1,), (1,))
    else:
        a_spec = pl.BlockSpec((k, tm), lambda i, j: (0, i))
        b_spec = pl.BlockSpec((k, tn), lambda i, j: (0, j))
        dims = ((0,), (0,))
    o_spec = pl.BlockSpec((tm, tn), lambda i, j: (i, j))
    n_extra = len(extras)
    ties = () if tie is None else (tie,)

    def body(a_ref, b_ref, *rest):
        rest = rest[len(ties):]
        acc = _dot(a_ref[...].astype(BF16), b_ref[...].astype(BF16), dims)
        vals = epi(acc, *[e[...] for e in rest[:n_extra]]) if epi is not None else (acc,)
        for o_ref, v in zip(rest[n_extra:], vals):
            o_ref[...] = v.astype(o_ref.dtype)

    outs = _pcall(
        body, name=name, grid=(m // tm, n // tn),
        in_specs=[a_spec, b_spec] + [pl.BlockSpec(TOKEN_SHAPE, lambda i, j: (0, 0))] * len(ties) + [o_spec] * n_extra,
        out_specs=[o_spec] * len(out_dtypes),
        out_shape=[jax.ShapeDtypeStruct((m, n), dt) for dt in out_dtypes],
        compiler_params=_params("parallel", "parallel"),
    )(a, b, *ties, *extras)
    return outs[0] if len(out_dtypes) == 1 else outs


def _row_tile(k, a_bytes, n_row_blocks):
    for tm in (1024, 512, 256, 128):
        if 2 * (tm * k * a_bytes + D * k * 2 + n_row_blocks * tm * D * 4) + tm * D * 4 <= MM_VMEM_BUDGET + 4 * 2 ** 20:
            return tm
    raise ValueError("no row tile fits")


def _mm_out_norm(name, a, b, k, res, g_next):
    tm = _row_tile(k, a.dtype.itemsize, 3)

    def body(a_ref, b_ref, r_ref, g_ref, h_ref, hn_ref):
        h = _nn(a_ref[...].astype(BF16), b_ref[...]) + r_ref[...]
        h_ref[...] = h
        r = lax.rsqrt(jnp.mean(h * h, axis=-1, keepdims=True) + EPS)
        hn_ref[...] = ((h * r) * g_ref[...]).astype(BF16)

    row = pl.BlockSpec((tm, D), lambda i: (i, 0))
    return _pcall(
        body, name=name, grid=(T // tm,),
        in_specs=[pl.BlockSpec((tm, k), lambda i: (i, 0)), pl.BlockSpec((k, D), lambda i: (0, 0)), row,
                  pl.BlockSpec((1, D), lambda i: (0, 0))],
        out_specs=[row, row],
        out_shape=[jax.ShapeDtypeStruct((T, D), F32), jax.ShapeDtypeStruct((T, D), BF16)],
        compiler_params=_params("parallel"),
    )(a, b, res, g_next)


def _mm_dx_norm(name, dz, w, k, h, g, dres, tie=None):
    tm = _row_tile(k, dz.dtype.itemsize, 3)
    ties = () if tie is None else (tie,)

    def body(a_ref, b_ref, *rest):
        h_ref, g_ref, r_ref, dh_ref, dg_ref = rest[len(ties):]
        dy = _nt(a_ref[...].astype(BF16), b_ref[...])
        x = h_ref[...]
        r = lax.rsqrt(jnp.mean(x * x, axis=-1, keepdims=True) + EPS)
        nrm = x * r
        dn = dy * g_ref[...]
        dh_ref[...] = r_ref[...] + r * (dn - nrm * jnp.mean(dn * nrm, axis=-1, keepdims=True))

        @pl.when(pl.program_id(0) == 0)
        def _():
            dg_ref[...] = jnp.zeros_like(dg_ref)

        dg_ref[...] += jnp.sum(dy * nrm, axis=0, keepdims=True)

    row = pl.BlockSpec((tm, D), lambda i: (i, 0))
    vec = pl.BlockSpec((1, D), lambda i: (0, 0))
    return _pcall(
        body, name=name, grid=(T // tm,),
        in_specs=[pl.BlockSpec((tm, k), lambda i: (i, 0)), pl.BlockSpec((D, k), lambda i: (0, 0))]
        + [pl.BlockSpec(TOKEN_SHAPE, lambda i: (0, 0))] * len(ties) + [row, vec, row],
        out_specs=[row, vec],
        out_shape=[jax.ShapeDtypeStruct((T, D), F32), jax.ShapeDtypeStruct((1, D), F32)],
        compiler_params=_params("arbitrary"),
    )(dz, w, *ties, h, g, dres)


def _rms_fwd(name, h, g, tm=512):
    def body(h_ref, g_ref, o_ref):
        x = h_ref[...]
        r = lax.rsqrt(jnp.mean(x * x, axis=-1, keepdims=True) + EPS)
        o_ref[...] = ((x * r) * g_ref[...]).astype(BF16)

    return _pcall(
        body, name=name, grid=(T // tm,),
        in_specs=[pl.BlockSpec((tm, D), lambda i: (i, 0)), pl.BlockSpec((1, D), lambda i: (0, 0))],
        out_specs=pl.BlockSpec((tm, D), lambda i: (i, 0)),
        out_shape=jax.ShapeDtypeStruct((T, D), BF16),
        compiler_params=_params("parallel"),
    )(h, g)


def _loss_head(h, g, target, tm=512):
    def body(h_ref, g_ref, t_ref, dh_ref, dg_ref, loss_ref):
        x = h_ref[...]
        r = lax.rsqrt(jnp.mean(x * x, axis=-1, keepdims=True) + EPS)
        nrm = x * r
        gain = g_ref[...]
        err = nrm * gain - t_ref[...]
        dy = err * (1.0 / D)
        dn = dy * gain
        dh_ref[...] = r * (dn - nrm * jnp.mean(dn * nrm, axis=-1, keepdims=True))

        @pl.when(pl.program_id(0) == 0)
        def _():
            dg_ref[...] = jnp.zeros_like(dg_ref)
            loss_ref[...] = jnp.zeros_like(loss_ref)

        dg_ref[...] += jnp.sum(dy * nrm, axis=0, keepdims=True)
        part = jnp.sum(jnp.sum(err * err, axis=1, keepdims=True), axis=0, keepdims=True) * (0.5 / D)
        loss_ref[...] += jnp.broadcast_to(part, (1, LANES))

    row = pl.BlockSpec((tm, D), lambda i: (i, 0))
    vec = pl.BlockSpec((1, D), lambda i: (0, 0))
    return _pcall(
        body, name="loss_head", grid=(T // tm,),
        in_specs=[row, vec, row], out_specs=[row, vec, pl.BlockSpec((1, LANES), lambda i: (0, 0))],
        out_shape=[jax.ShapeDtypeStruct((T, D), F32), jax.ShapeDtypeStruct((1, D), F32),
                   jax.ShapeDtypeStruct((1, LANES), F32)],
        compiler_params=_params("arbitrary"),
    )(h, g, target)


CONV_TILE = 256
CONV_HALO = 32


def _glu(z):
    return z[:, :CONV_CH] * _sigmoid(z[:, CONV_CH:])


SUBLANES = 8


def _sublane_shifts(win):
    n = win.shape[0]
    return [win] + [win[r:r + n - SUBLANES, :] for r in range(1, SUBLANES)]


def _rows_from(shifts, off, n):
    q, r = divmod(off, SUBLANES)
    return shifts[r][q * SUBLANES:q * SUBLANES + n, :]


def _econv_fwd(zc, conv_k, conv_b, ln_g, ln_b):
    R, H = CONV_TILE, CONV_HALO

    def body(z_ref, zh_ref, k_ref, b_ref, g_ref, be_ref, cv_ref, cat_ref):
        i = pl.program_id(0)
        glu = _glu(z_ref[...])
        halo = _glu(zh_ref[...]) * (i > 0).astype(F32)
        win = _sublane_shifts(jnp.concatenate([halo, glu], axis=0))
        acc = jnp.zeros((R, CONV_CH), F32) + b_ref[...]
        for j in range(CONV_W):
            acc = acc + k_ref[j:j + 1, :] * _rows_from(win, H - (CONV_W - 1) + j, R)
        cv_ref[...] = acc
        mu = jnp.mean(acc, axis=-1, keepdims=True)
        xc = acc - mu
        rstd = lax.rsqrt(jnp.mean(xc * xc, axis=-1, keepdims=True) + EPS)
        ln = xc * rstd * g_ref[...] + be_ref[...]
        cat_ref[...] = (ln * _sigmoid(ln)).astype(BF16)

    vec = pl.BlockSpec((1, CONV_CH), lambda i: (0, 0))
    return _pcall(
        body, name="econv_fwd", grid=(T // R,),
        in_specs=[pl.BlockSpec((R, 2 * CONV_CH), lambda i: (i, 0)),
                  pl.BlockSpec((H, 2 * CONV_CH), lambda i: (jnp.maximum(i * (R // H) - 1, 0), 0)),
                  pl.BlockSpec((CONV_W, CONV_CH), lambda i: (0, 0)), vec, vec, vec],
        out_specs=[pl.BlockSpec((R, CONV_CH), lambda i: (i, 0)), pl.BlockSpec((R, CONV_CH), lambda i: (i, 0))],
        out_shape=[jax.ShapeDtypeStruct((T, CONV_CH), F32), jax.ShapeDtypeStruct((T, D), BF16)],
        compiler_params=_params("parallel"),
    )(zc, zc, conv_k, conv_b, ln_g, ln_b)


def _econv_bwd_ln(cv, dcat, ln_g, ln_b):
    R = CONV_TILE

    def body(cv_ref, d_ref, g_ref, be_ref, dcv_ref, dg_ref, dbe_ref, dcb_ref):
        cv_t = cv_ref[...]
        mu = jnp.mean(cv_t, axis=-1, keepdims=True)
        xc = cv_t - mu
        rstd = lax.rsqrt(jnp.mean(xc * xc, axis=-1, keepdims=True) + EPS)
        xh = xc * rstd
        ln = xh * g_ref[...] + be_ref[...]
        sg = _sigmoid(ln)
        dln = d_ref[...] * (sg * (1.0 + ln * (1.0 - sg)))
        dxh = dln * g_ref[...]
        dcv = rstd * (dxh - jnp.mean(dxh, axis=-1, keepdims=True) - xh * jnp.mean(dxh * xh, axis=-1, keepdims=True))
        dcv_ref[...] = dcv

        @pl.when(pl.program_id(0) == 0)
        def _():
            dg_ref[...] = jnp.zeros_like(dg_ref)
            dbe_ref[...] = jnp.zeros_like(dbe_ref)
            dcb_ref[...] = jnp.zeros_like(dcb_ref)

        dg_ref[...] += jnp.sum(dln * xh, axis=0, keepdims=True)
        dbe_ref[...] += jnp.sum(dln, axis=0, keepdims=True)
        dcb_ref[...] += jnp.sum(dcv, axis=0, keepdims=True)

    vec = pl.BlockSpec((1, CONV_CH), lambda i: (0, 0))
    row = pl.BlockSpec((R, CONV_CH), lambda i: (i, 0))
    vshape = jax.ShapeDtypeStruct((1, CONV_CH), F32)
    return _pcall(
        body, name="econv_bwd_ln", grid=(T // R,),
        in_specs=[row, row, vec, vec], out_specs=[row, vec, vec, vec],
        out_shape=[jax.ShapeDtypeStruct((T, CONV_CH), F32), vshape, vshape, vshape],
        compiler_params=_params("arbitrary"),
    )(cv, dcat, ln_g, ln_b)


def _econv_bwd_conv(dcv, zc, conv_k):
    R, H = CONV_TILE, CONV_HALO
    last = T // R - 1

    def body(d_ref, dn_ref, z_ref, zh_ref, k_ref, dz_ref, dk_ref):
        i = pl.program_id(0)
        z = z_ref[...]
        a_lin = z[:, :CONV_CH]
        sg = _sigmoid(z[:, CONV_CH:])
        glu = a_lin * sg
        halo = _glu(zh_ref[...]) * (i > 0).astype(F32)
        win = _sublane_shifts(jnp.concatenate([halo, glu], axis=0))
        dcv_t = d_ref[...]
        nxt = dn_ref[...] * (i < last).astype(F32)
        winb = _sublane_shifts(jnp.concatenate([dcv_t, nxt], axis=0))

        @pl.when(i == 0)
        def _():
            dk_ref[...] = jnp.zeros_like(dk_ref)

        dglu = jnp.zeros((R, CONV_CH), F32)
        for j in range(CONV_W):
            dk_ref[j:j + 1, :] += jnp.sum(dcv_t * _rows_from(win, H - (CONV_W - 1) + j, R), axis=0, keepdims=True)
            dglu = dglu + k_ref[j:j + 1, :] * _rows_from(winb, CONV_W - 1 - j, R)
        dz_ref[...] = jnp.concatenate([dglu * sg, dglu * a_lin * sg * (1.0 - sg)], axis=1).astype(BF16)

    return _pcall(
        body, name="econv_bwd_conv", grid=(T // R,),
        in_specs=[pl.BlockSpec((R, CONV_CH), lambda i: (i, 0)),
                  pl.BlockSpec((H, CONV_CH), lambda i: (jnp.minimum((i + 1) * (R // H), T // H - 1), 0)),
                  pl.BlockSpec((R, 2 * CONV_CH), lambda i: (i, 0)),
                  pl.BlockSpec((H, 2 * CONV_CH), lambda i: (jnp.maximum(i * (R // H) - 1, 0), 0)),
                  pl.BlockSpec((CONV_W, CONV_CH), lambda i: (0, 0))],
        out_specs=[pl.BlockSpec((R, 2 * CONV_CH), lambda i: (i, 0)), pl.BlockSpec((CONV_W, CONV_CH), lambda i: (0, 0))],
        out_shape=[jax.ShapeDtypeStruct((T, EVEN_IN), BF16), jax.ShapeDtypeStruct((CONV_W, CONV_CH), F32)],
        compiler_params=_params("arbitrary"),
    )(dcv, dcv, zc, zc, conv_k)


def _swap_halves(v):
    lane = lax.broadcasted_iota(jnp.int32, v.shape, 1)
    return jnp.where((lane % HEAD_DIM) < HEAD_DIM // 2, pltpu.roll(v, LANES - HEAD_DIM // 2, 1),
                     pltpu.roll(v, HEAD_DIM // 2, 1))


def _qkv_proj(hn, w_in, rope_c, rope_s, tm=T):
    tn = 4 * LANES

    def body(a_ref, b_ref, c_ref, s_ref, o_ref):
        j = pl.program_id(1)
        acc = _nn(a_ref[...], b_ref[...])
        for p in range(4):
            v = acc[:, p * LANES:(p + 1) * LANES]
            rot = v * c_ref[...] + _swap_halves(v) * s_ref[...]
            o_ref[p] = jnp.where(j < 6, rot, v)

    tab = pl.BlockSpec((tm, LANES), lambda i, j: (i, 0))
    return _pcall(
        body, name="qkv_proj", grid=(T // tm, 9),
        in_specs=[pl.BlockSpec((tm, D), lambda i, j: (i, 0)),
                  pl.BlockSpec((D, tn), lambda i, j: (0, j + (2 * CONV_CH) // tn)), tab, tab],
        out_specs=pl.BlockSpec((None, 4, tm, LANES), lambda i, j: (j, 0, i, 0)),
        out_shape=jax.ShapeDtypeStruct((9, 4, T, LANES), F32),
        compiler_params=_params("parallel", "parallel"),
    )(hn, w_in, rope_c, rope_s)


ATTN_FWD_UNROLL = 4
ATTN_BWD_UNROLL = 4


def _band_rows(start, d):
    if d == 1:
        return pl.ds(pl.multiple_of(start, BAND), BAND)
    return pl.ds(start, BAND, stride=d)


def _band_masks(n):
    row = lax.broadcasted_iota(jnp.int32, (BAND, BAND), 0)
    col = lax.broadcasted_iota(jnp.int32, (BAND, BAND), 1)
    no_prev = (n == 0).astype(jnp.int32) * (2 * BAND)
    return col <= row, col >= row + no_prev


def _attn_fwd(qkv, g):
    d = DILATIONS[g]
    nb = T // d // BAND

    def body(q_ref, k_ref, v_ref, o_ref, l_ref):
        lane_lo = lax.broadcasted_iota(jnp.int32, (BAND, LANES), 1) < HEAD_DIM

        heads = (lane_lo, jnp.logical_not(lane_lo))
        ones = jnp.ones((BAND, LANES), BF16)

        def step(it, carry):
            tiles = []
            for u in range(ATTN_FWD_UNROLL):
                idx = it * ATTN_FWD_UNROLL + u
                r = idx // nb
                n = idx % nb
                cur = _band_rows(n * (BAND * d) + r, d)
                prev = _band_rows(jnp.maximum(n - 1, 0) * (BAND * d) + r, d)
                mc, mp = _band_masks(n)
                tiles.append((cur, mc, mp, q_ref[cur, :], k_ref[cur, :].astype(BF16), v_ref[cur, :].astype(BF16),
                              k_ref[prev, :].astype(BF16), v_ref[prev, :].astype(BF16)))
            scores = []
            for cur, mc, mp, q, kc, vc, kp, vp in tiles:
                for hm in heads:
                    qm = jnp.where(hm, q, 0.0).astype(BF16)
                    scores.append((jnp.where(mc, _nt(qm, kc) * SCALE, NEG), jnp.where(mp, _nt(qm, kp) * SCALE, NEG)))
            maxes = [jnp.maximum(jnp.max(sc, axis=1, keepdims=True), jnp.max(sp, axis=1, keepdims=True))
                     for sc, sp in scores]
            probs = [(jnp.exp(sc - mx).astype(BF16), jnp.exp(sp - mx).astype(BF16))
                     for (sc, sp), mx in zip(scores, maxes)]
            dens = [_nn(pc, ones) + _nn(pp, ones) for pc, pp in probs]
            for t, (cur, mc, mp, q, kc, vc, kp, vp) in enumerate(tiles):
                outs, lses = [], []
                for h in range(2):
                    pc, pp = probs[2 * t + h]
                    outs.append((_nn(pc, vc) + _nn(pp, vp)) / dens[2 * t + h])
                    lses.append(maxes[2 * t + h] + jnp.log(dens[2 * t + h]))
                o_ref[cur, :] = jnp.where(lane_lo, outs[0], outs[1])
                l_ref[cur, :] = jnp.where(lane_lo, lses[0], lses[1])
            return carry

        lax.fori_loop(0, d * nb // ATTN_FWD_UNROLL, step, 0)

    def slab(which):
        return pl.BlockSpec((None, None, T, LANES), lambda p: (which * 3 + g, p, 0, 0))

    out = pl.BlockSpec((None, T, LANES), lambda p: (p, 0, 0))
    shape = jax.ShapeDtypeStruct((4, T, LANES), F32)
    return _pcall(
        body, name=f"attn_fwd{g}", grid=(4,),
        in_specs=[slab(0), slab(1), slab(2)], out_specs=[out, out], out_shape=[shape, shape],
        compiler_params=_params("parallel"),
    )(qkv, qkv, qkv)


def _attn_merge(outs, lses, cat, tm=1024):
    def body(o0, o1, o2, l0, l1, l2, cat_in, cat_ref, att_ref, w0, w1, w2):
        del cat_in
        la, lb, lc = l0[...], l1[...], l2[...]
        mx = jnp.maximum(jnp.maximum(la, lb), lc)
        ea, eb, ec = jnp.exp(la - mx), jnp.exp(lb - mx), jnp.exp(lc - mx)
        inv = 1.0 / (ea + eb + ec)
        wa, wb, wc = ea * inv, eb * inv, ec * inv
        att = wa * o0[...] + wb * o1[...] + wc * o2[...]
        att_ref[...] = att
        cat_ref[...] = att.astype(BF16)
        w0[...] = wa
        w1[...] = wb
        w2[...] = wc

    slab = pl.BlockSpec((None, tm, LANES), lambda p, i: (p, i, 0))
    shape = jax.ShapeDtypeStruct((4, T, LANES), F32)
    return _pcall(
        body, name="attn_merge", grid=(4, T // tm),
        in_specs=[slab] * 6 + [pl.BlockSpec(memory_space=pl.ANY)],
        out_specs=[pl.BlockSpec((tm, LANES), lambda p, i: (i, CONV_CH // LANES + p)), slab, slab, slab, slab],
        out_shape=[jax.ShapeDtypeStruct((T, D), BF16), shape, shape, shape, shape],
        input_output_aliases={6: 0},
        compiler_params=_params("parallel", "parallel"),
    )(*outs, *lses, cat)


def _attn_bwd(qkv, lse, wgt, att, dcat, dqkv, g):
    d = DILATIONS[g]
    nb = T // d // BAND

    def body(q_ref, k_ref, v_ref, l_ref, w_ref, a_ref, da_ref, dq_in, o_ref):
        del dq_in
        lane = lax.broadcasted_iota(jnp.int32, (BAND, LANES), 1)
        lane_lo = lane < HEAD_DIM
        row = lax.broadcasted_iota(jnp.int32, (LANES, LANES), 0)
        same_head = ((row // HEAD_DIM) == (lane // HEAD_DIM)).astype(BF16)
        dq_ref, dk_ref, dv_ref = o_ref.at[0], o_ref.at[1], o_ref.at[2]
        dk_ref[...] = jnp.zeros((T, LANES), F32)
        dv_ref[...] = jnp.zeros((T, LANES), F32)

        heads = (lane_lo, jnp.logical_not(lane_lo))

        def step(it, carry):
            tiles = []
            for u in range(ATTN_BWD_UNROLL):
                idx = it * ATTN_BWD_UNROLL + u
                r = idx // nb
                n = idx % nb
                cur = _band_rows(n * (BAND * d) + r, d)
                prev = _band_rows(jnp.maximum(n - 1, 0) * (BAND * d) + r, d)
                mc, mp = _band_masks(n)
                da = da_ref[cur, :]
                prod = da * a_ref[cur, :]
                hi = prod.astype(BF16)
                lo = (prod - hi.astype(F32)).astype(BF16)
                tiles.append(dict(cur=cur, prev=prev, mc=mc, mp=mp, da=da, hi=hi, lo=lo, q=q_ref[cur, :],
                                  kc=k_ref[cur, :].astype(BF16), vc=v_ref[cur, :].astype(BF16),
                                  kp=k_ref[prev, :].astype(BF16), vp=v_ref[prev, :].astype(BF16),
                                  lse=l_ref[cur, :], w=w_ref[cur, :]))
            for t in tiles:
                t["csum"] = _nn(t["hi"], same_head) + _nn(t["lo"], same_head)
            chains = []
            for t in tiles:
                for h, hm in enumerate(heads):
                    qm = jnp.where(hm, t["q"], 0.0).astype(BF16)
                    dam = jnp.where(hm, t["da"], 0.0).astype(BF16)
                    chains.append(dict(t=t, h=h, qm=qm, dam=dam,
                                       sc=jnp.where(t["mc"], _nt(qm, t["kc"]) * SCALE, NEG),
                                       sp=jnp.where(t["mp"], _nt(qm, t["kp"]) * SCALE, NEG),
                                       dpc=_nt(dam, t["vc"]), dpp=_nt(dam, t["vp"])))
            for ch in chains:
                t, col0 = ch["t"], ch["h"] * HEAD_DIM
                lse_h = t["lse"][:, col0:col0 + 1]
                w_h = t["w"][:, col0:col0 + 1]
                c_h = t["csum"][:, col0:col0 + 1]
                pwc = w_h * jnp.exp(ch["sc"] - lse_h)
                pwp = w_h * jnp.exp(ch["sp"] - lse_h)
                ch["dsc"] = (pwc * (ch["dpc"] - c_h) * SCALE).astype(BF16)
                ch["dsp"] = (pwp * (ch["dpp"] - c_h) * SCALE).astype(BF16)
                ch["pwc"] = pwc.astype(BF16)
                ch["pwp"] = pwp.astype(BF16)
            for ch in chains:
                t = ch["t"]
                ch["dq"] = _nn(ch["dsc"], t["kc"]) + _nn(ch["dsp"], t["kp"])
                ch["dkc"] = _tn(ch["dsc"], ch["qm"])
                ch["dkp"] = _tn(ch["dsp"], ch["qm"])
                ch["dvc"] = _tn(ch["pwc"], ch["dam"])
                ch["dvp"] = _tn(ch["pwp"], ch["dam"])
            for i, t in enumerate(tiles):
                c0, c1 = chains[2 * i], chains[2 * i + 1]
                dq_ref[t["cur"], :] = jnp.where(lane_lo, c0["dq"], c1["dq"])
                dk_ref[t["cur"], :] += c0["dkc"] + c1["dkc"]
                dk_ref[t["prev"], :] += c0["dkp"] + c1["dkp"]
                dv_ref[t["cur"], :] += c0["dvc"] + c1["dvc"]
                dv_ref[t["prev"], :] += c0["dvp"] + c1["dvp"]
            return carry

        lax.fori_loop(0, d * nb // ATTN_BWD_UNROLL, step, 0)

    def slab(which):
        return pl.BlockSpec((None, None, T, LANES), lambda p: (which * 3 + g, p, 0, 0))

    per_pair = pl.BlockSpec((None, T, LANES), lambda p: (p, 0, 0))
    return _pcall(
        body, name=f"attn_bwd{g}", grid=(4,),
        in_specs=[slab(0), slab(1), slab(2), per_pair, per_pair, per_pair,
                  pl.BlockSpec((T, LANES), lambda p: (0, CONV_CH // LANES + p)),
                  pl.BlockSpec(memory_space=pl.ANY)],
        out_specs=pl.BlockSpec((None, 3, None, T, LANES), lambda p: (g, 0, p, 0, 0)),
        out_shape=jax.ShapeDtypeStruct((3, 3, 4, T, LANES), F32),
        input_output_aliases={7: 0},
        compiler_params=_params("parallel"),
    )(qkv, qkv, qkv, lse, wgt, att, dcat, dqkv)


def _rope_bwd(dqkv, rope_c, rope_s, dz):
    wide = 4 * LANES

    def body(d_ref, c_ref, s_ref, dz_in, o_ref):
        del dz_in
        w = pl.program_id(1)
        for p in range(4):
            v = d_ref[p]
            rot = v * c_ref[...] + _swap_halves(v * s_ref[...])
            o_ref[:, p * LANES:(p + 1) * LANES] = jnp.where(w < 2, rot, v).astype(BF16)

    tab = pl.BlockSpec((T, LANES), lambda g, w: (0, 0))
    return _pcall(
        body, name="rope_bwd", grid=(3, 3),
        in_specs=[pl.BlockSpec((None, None, 4, T, LANES), lambda g, w: (g, w, 0, 0, 0)), tab, tab,
                  pl.BlockSpec(memory_space=pl.ANY)],
        out_specs=pl.BlockSpec((T, wide), lambda g, w: (0, (2 * CONV_CH) // wide + w * 3 + g)),
        out_shape=jax.ShapeDtypeStruct((T, EVEN_IN), BF16),
        input_output_aliases={3: 0},
        compiler_params=_params("parallel", "parallel"),
    )(dqkv, rope_c, rope_s, dz)


ODD_TILE = 256
ODD_HALO = 8
GELU_C = 0.7978845608028654
GELU_A = 0.044715


def _gelu(x):
    return 0.5 * x * (1.0 + jnp.tanh(GELU_C * (x + GELU_A * x * x * x)))


def _gelu_grad(x):
    th = jnp.tanh(GELU_C * (x + GELU_A * x * x * x))
    return 0.5 * (1.0 + th) + 0.5 * x * (1.0 - th * th) * GELU_C * (1.0 + 3.0 * GELU_A * x * x)


def _tril():
    row = lax.broadcasted_iota(jnp.int32, (CHUNK, CHUNK), 0)
    col = lax.broadcasted_iota(jnp.int32, (CHUNK, CHUNK), 1)
    return (col <= row).astype(F32)


def _odd_parts(z, zh, i, k_ref, g_ref, be_ref, w_ref, bt_ref):
    R, H = ODD_TILE, ODD_HALO
    gb, gc, xs, uv = z[:, :512], z[:, 512:1024], z[:, 1024:1536], z[:, 1536:]
    halo = zh[:, 512:1024] * zh[:, 1024:1536] * (i > 0).astype(F32)
    win = jnp.concatenate([halo, gc * xs], axis=0)
    cv = jnp.zeros((R, 512), F32)
    for j in range(SCONV_W):
        off = H - (SCONV_W - 1) + j
        cv = cv + k_ref[j:j + 1, :] * win[off:off + R, :]
    ge = _gelu(uv)
    u, v = ge[:, :512], ge[:, 512:]
    mu = jnp.mean(v, axis=-1, keepdims=True)
    xc = v - mu
    rstd = lax.rsqrt(jnp.mean(xc * xc, axis=-1, keepdims=True) + EPS)
    xh = xc * rstd
    vn = xh * g_ref[...] + be_ref[...]
    tril = _tril()
    wms = [(w_ref[g] * tril).astype(BF16) for g in range(SG_GROUPS)]
    rows = []
    for ci in range(R // CHUNK):
        blocks = []
        for g in range(SG_GROUPS):
            blk = vn[ci * CHUNK:(ci + 1) * CHUNK, g * LANES:(g + 1) * LANES].astype(BF16)
            blocks.append(_nn(wms[g], blk) + bt_ref[:, g:g + 1])
        rows.append(jnp.concatenate(blocks, axis=1))
    vmix = jnp.concatenate(rows, axis=0)
    return gb, gc, xs, uv, win, cv, u, rstd, xh, vn, vmix, wms


def _odd_mid_fwd(z, conv_k, ln_g, ln_b, sg_w, sg_bt):
    R, H = ODD_TILE, ODD_HALO

    def body(z_ref, zh_ref, k_ref, g_ref, be_ref, w_ref, bt_ref, o_ref):
        i = pl.program_id(0)
        gb, _, _, _, _, cv, u, _, _, _, vmix, _ = _odd_parts(z_ref[...], zh_ref[...], i, k_ref, g_ref, be_ref, w_ref, bt_ref)
        o_ref[...] = jnp.concatenate([gb * cv, u * vmix], axis=1).astype(BF16)

    vec = pl.BlockSpec((1, 512), lambda i: (0, 0))
    return _pcall(
        body, name="odd_mid_fwd", grid=(T // R,),
        in_specs=[pl.BlockSpec((R, ODD_IN), lambda i: (i, 0)),
                  pl.BlockSpec((H, ODD_IN), lambda i: (jnp.maximum(i * (R // H) - 1, 0), 0)),
                  pl.BlockSpec((SCONV_W, 512), lambda i: (0, 0)), vec, vec,
                  pl.BlockSpec((SG_GROUPS, CHUNK, CHUNK), lambda i: (0, 0, 0)),
                  pl.BlockSpec((CHUNK, SG_GROUPS), lambda i: (0, 0))],
        out_specs=pl.BlockSpec((R, D), lambda i: (i, 0)),
        out_shape=jax.ShapeDtypeStruct((T, D), BF16),
        compiler_params=_params("parallel"),
    )(z, z, conv_k, ln_g, ln_b, sg_w, sg_bt)


def _odd_mid_bwd(z, dcat, conv_k, ln_g, ln_b, sg_w, sg_bt):
    R, H = ODD_TILE, ODD_HALO
    last = T // R - 1

    def body(z_ref, zh_ref, zn_ref, d_ref, dn_ref, k_ref, g_ref, be_ref, w_ref, bt_ref,
             dz_ref, dk_ref, dg_ref, dbe_ref, dw_ref, dbt_ref):
        i = pl.program_id(0)
        z = z_ref[...]
        gb, gc, xs, uv, win, cv, u, rstd, xh, vn, vmix, wms = _odd_parts(z, zh_ref[...], i, k_ref, g_ref, be_ref, w_ref, bt_ref)
        dcat_t = d_ref[...]
        dc, dd = dcat_t[:, :512], dcat_t[:, 512:]

        @pl.when(i == 0)
        def _():
            dk_ref[...] = jnp.zeros_like(dk_ref)
            dg_ref[...] = jnp.zeros_like(dg_ref)
            dbe_ref[...] = jnp.zeros_like(dbe_ref)
            dw_ref[...] = jnp.zeros_like(dw_ref)
            dbt_ref[...] = jnp.zeros_like(dbt_ref)

        dgb = dc * cv
        dcv = dc * gb
        nxt = dn_ref[:, :512] * zn_ref[:, :512] * (i < last).astype(F32)
        winb = jnp.concatenate([dcv, nxt], axis=0)
        dp = jnp.zeros((R, 512), F32)
        for j in range(SCONV_W):
            off = H - (SCONV_W - 1) + j
            dk_ref[j:j + 1, :] += jnp.sum(dcv * win[off:off + R, :], axis=0, keepdims=True)
            ob = SCONV_W - 1 - j
            dp = dp + k_ref[j:j + 1, :] * winb[ob:ob + R, :]
        dgc = dp * xs
        dxs = dp * gc
        du = dd * vmix
        dvmix = dd * u
        tril = _tril()
        rows = []
        for ci in range(R // CHUNK):
            blocks = []
            for g in range(SG_GROUPS):
                sl = (slice(ci * CHUNK, (ci + 1) * CHUNK), slice(g * LANES, (g + 1) * LANES))
                dblk = dvmix[sl]
                dblk16 = dblk.astype(BF16)
                blocks.append(_tn(wms[g], dblk16))
                dw_ref[g] += _nt(dblk16, vn[sl].astype(BF16)) * tril
                dbt_ref[:, g:g + 1] += jnp.sum(dblk, axis=1, keepdims=True)
            rows.append(jnp.concatenate(blocks, axis=1))
        dvn = jnp.concatenate(rows, axis=0)
        dg_ref[...] += jnp.sum(dvn * xh, axis=0, keepdims=True)
        dbe_ref[...] += jnp.sum(dvn, axis=0, keepdims=True)
        dxh = dvn * g_ref[...]
        dv = rstd * (dxh - jnp.mean(dxh, axis=-1, keepdims=True) - xh * jnp.mean(dxh * xh, axis=-1, keepdims=True))
        duv = jnp.concatenate([du, dv], axis=1) * _gelu_grad(uv)
        dz_ref[...] = jnp.concatenate([dgb, dgc, dxs, duv], axis=1).astype(BF16)

    vec = pl.BlockSpec((1, 512), lambda i: (0, 0))
    kspec = pl.BlockSpec((SCONV_W, 512), lambda i: (0, 0))
    wspec = pl.BlockSpec((SG_GROUPS, CHUNK, CHUNK), lambda i: (0, 0, 0))
    bspec = pl.BlockSpec((CHUNK, SG_GROUPS), lambda i: (0, 0))
    nxt_blk = lambda i: (jnp.minimum((i + 1) * (R // H), T // H - 1), 0)
    return _pcall(
        body, name="odd_mid_bwd", grid=(T // R,),
        in_specs=[pl.BlockSpec((R, ODD_IN), lambda i: (i, 0)),
                  pl.BlockSpec((H, ODD_IN), lambda i: (jnp.maximum(i * (R // H) - 1, 0), 0)),
                  pl.BlockSpec((H, ODD_IN), nxt_blk),
                  pl.BlockSpec((R, D), lambda i: (i, 0)),
                  pl.BlockSpec((H, D), nxt_blk),
                  kspec, vec, vec, wspec, bspec],
        out_specs=[pl.BlockSpec((R, ODD_IN), lambda i: (i, 0)), kspec, vec, vec, wspec, bspec],
        out_shape=[jax.ShapeDtypeStruct((T, ODD_IN), BF16), jax.ShapeDtypeStruct((SCONV_W, 512), F32),
                   jax.ShapeDtypeStruct((1, 512), F32), jax.ShapeDtypeStruct((1, 512), F32),
                   jax.ShapeDtypeStruct((SG_GROUPS, CHUNK, CHUNK), F32), jax.ShapeDtypeStruct((CHUNK, SG_GROUPS), F32)],
        compiler_params=_params("arbitrary"),
    )(z, z, z, dcat, dcat, conv_k, ln_g, ln_b, sg_w, sg_bt)


def _ffn_up(tag, hn, weight):
    def act(acc):
        r = jnp.maximum(acc, 0.0)
        return (r * r,)

    return _mm(f"ffn{tag}_up", "nn", hn, weight(f"ffn_w1_{tag}", hn), T, D_FF, D, (BF16,), epi=act)


def _ffn_bwd(tag, h, g, weight, emit, saved, dout):
    hn, f = saved
    du = _mm(f"ffn{tag}_dact", "nt", dout, weight(f"ffn_w2_{tag}", dout), T, D_FF, D, (BF16,),
             epi=lambda acc, ff: (acc * (2.0 * jnp.sqrt(ff.astype(F32))),), extras=(f,))
    tok = emit(f"ffn_w2_{tag}", _mm(f"ffn{tag}_dw2", "tn", f, dout, D_FF, D, T, (BF16,)))
    tok = emit(f"ffn_w1_{tag}", _mm(f"ffn{tag}_dw1", "tn", hn, du, D, D_FF, T, (BF16,), tie=tok))
    return _mm_dx_norm(f"ffn{tag}_dhn", du, weight(f"ffn_w1_{tag}", du), D_FF, h, g, dout, tie=tok)


def _rope_tables():
    half = HEAD_DIM // 2
    inv = 10000.0 ** (-jnp.arange(half, dtype=F32) / half)
    ang = jnp.arange(T, dtype=F32)[:, None] * inv[None, :]
    cos, sin = jnp.cos(ang), jnp.sin(ang)
    c = jnp.tile(jnp.concatenate([cos, cos], axis=1), (1, LANES // HEAD_DIM))
    s = jnp.tile(jnp.concatenate([-sin, sin], axis=1), (1, LANES // HEAD_DIM))
    return c, s


def _local_step(x, target, p, weight, emit, emit_small):
    rope_c, rope_s = _rope_tables()
    grads = {}
    residual = lambda acc, res: (acc + res,)

    hn0 = _rms_fwd("mix0_norm", x, p["norm_mix_g0"])
    zc = _mm("even_in_conv", "nn", hn0, weight("even_w_in", hn0), T, 2 * CONV_CH, D, (F32,))
    qkv = _qkv_proj(hn0, weight("even_w_in", hn0), rope_c, rope_s)
    cv, cat0 = _econv_fwd(zc, p["even_conv_k"], p["even_conv_b"], p["even_ln_g"], p["even_ln_b"])
    att_parts = [_attn_fwd(qkv, g) for g in range(3)]
    outs = [a[0] for a in att_parts]
    lses = [a[1] for a in att_parts]
    cat0, att, w0, w1, w2 = _attn_merge(outs, lses, cat0)
    wgts = (w0, w1, w2)
    h1, hnf0 = _mm_out_norm("even_out", cat0, weight("even_w_out", cat0), D, x, p["norm_ffn_g0"])
    f0 = _ffn_up(0, hnf0, weight)
    h2, hn1 = _mm_out_norm("ffn0_down", f0, weight("ffn_w2_0", f0), D_FF, h1, p["norm_mix_g1"])

    z1 = _mm("odd_in", "nn", hn1, weight("odd_w_in", hn1), T, ODD_IN, D, (F32,))
    cat1 = _odd_mid_fwd(z1, p["odd_conv_k"], p["odd_ln_g"], p["odd_ln_b"], p["odd_sg_w"], p["odd_sg_bt"])
    h3, hnf1 = _mm_out_norm("odd_out", cat1, weight("odd_w_out", cat1), D, h2, p["norm_ffn_g1"])
    f1 = _ffn_up(1, hnf1, weight)
    h4 = _mm("ffn1_down", "nn", f1, weight("ffn_w2_1", f1), T, D, D_FF, (F32,), epi=residual, extras=(h3,))

    dh4, grads["final_g"], loss = _loss_head(h4, p["final_g"], target)

    dh3, grads["norm_ffn_g1"] = _ffn_bwd(1, h3, p["norm_ffn_g1"], weight, emit, (hnf1, f1), dh4)
    tok = emit("odd_w_out", _mm("odd_out_dw", "tn", cat1, dh3, D, D, T, (BF16,)))
    dcat1 = _mm("odd_out_dx", "nt", dh3, weight("odd_w_out", dh3), T, D, D, (F32,), tie=tok)
    dz1, grads["odd_conv_k"], grads["odd_ln_g"], grads["odd_ln_b"], grads["odd_sg_w"], grads["odd_sg_bt"] = _odd_mid_bwd(
        z1, dcat1, p["odd_conv_k"], p["odd_ln_g"], p["odd_ln_b"], p["odd_sg_w"], p["odd_sg_bt"])
    tok = emit("odd_w_in", _mm("odd_in_dw", "tn", hn1, dz1, D, ODD_IN, T, (BF16,)))
    dh2, grads["norm_mix_g1"] = _mm_dx_norm("odd_in_dx", dz1, weight("odd_w_in", dz1), ODD_IN, h2, p["norm_mix_g1"],
                                            dh3, tie=tok)

    dh1, grads["norm_ffn_g0"] = _ffn_bwd(0, h1, p["norm_ffn_g0"], weight, emit, (hnf0, f0), dh2)
    tok = emit("even_w_out", _mm("even_out_dw", "tn", cat0, dh1, D, D, T, (BF16,)))
    dcat0 = _mm("even_out_dx", "nt", dh1, weight("even_w_out", dh1), T, D, D, (F32,), tie=tok)
    dcv, grads["even_ln_g"], grads["even_ln_b"], grads["even_conv_b"] = _econv_bwd_ln(
        cv, dcat0, p["even_ln_g"], p["even_ln_b"])
    dz0, grads["even_conv_k"] = _econv_bwd_conv(dcv, zc, p["even_conv_k"])
    tok = emit_small(loss, grads)
    dqkv = lax.empty((3, 3, 4, T, LANES), F32)
    for g in range(3):
        dqkv = _attn_bwd(qkv, lses[g], wgts[g], att, dcat0, dqkv, g)
    dz0 = _rope_bwd(dqkv, rope_c, rope_s, dz0)
    tok = emit("even_w_in", _mm("even_in_dw", "tn", hn0, dz0, D, EVEN_IN, T, (BF16,), tie=tok))
    dx, dg0 = _mm_dx_norm("even_in_dx", dz0, weight("even_w_in", dz0), EVEN_IN, x, p["norm_mix_g0"], dh1, tie=tok)
    return dx, dg0


def _rowwise(name, fn, ins, out_dtypes, tm=256):
    rows, cols = ins[0].shape
    tm = tm if rows % tm == 0 else rows
    n_in = len(ins)

    def body(*refs):
        vals = fn(*[r[...] for r in refs[:n_in]])
        for o_ref, v in zip(refs[n_in:], vals):
            o_ref[...] = v.astype(o_ref.dtype)

    spec = pl.BlockSpec((tm, cols), lambda i: (i, 0))
    outs = _pcall(
        body, name=name, grid=(rows // tm,),
        in_specs=[spec] * n_in, out_specs=[spec] * len(out_dtypes),
        out_shape=[jax.ShapeDtypeStruct((rows, cols), dt) for dt in out_dtypes],
        compiler_params=_params("parallel"),
    )(*ins)
    return outs[0] if len(out_dtypes) == 1 else outs


def _adamw(name, w, g, m, v, with_grad=False):
    c1 = 1.0 - ADAM_B1 ** ADAM_STEP
    c2 = 1.0 - ADAM_B2 ** ADAM_STEP

    def fn(w_t, g_t, m_t, v_t):
        m_new = ADAM_B1 * m_t + (1.0 - ADAM_B1) * g_t
        v_new = ADAM_B2 * v_t + (1.0 - ADAM_B2) * (g_t * g_t)
        delta = -ADAM_LR * ((m_new / c1) / (jnp.sqrt(v_new / c2) + ADAM_EPS) + ADAM_WD * w_t)
        return (delta, m_new, v_new, g_t) if with_grad else (delta, m_new, v_new)

    return _rowwise(name, fn, (w, g, m, v), (F32,) * (4 if with_grad else 3))


class _Piece:
    def __init__(self, name, rows, cols, axis, src, src_row0):
        self.name, self.rows, self.cols, self.axis = name, rows, cols, axis
        self.width = (cols if axis == 1 else rows) // 4
        self.src, self.src_row0 = src, src_row0

    @property
    def full_shape(self):
        return (self.rows, self.cols)

    @property
    def half_shape(self):
        return (self.rows // 2, self.cols) if self.axis == 1 else (self.rows, self.cols // 2)

    @property
    def shard_half_shape(self):
        return (self.rows // 2, self.width) if self.axis == 1 else (self.width, self.cols // 2)

    def shard_whole(self, ref):
        n = self.rows if self.axis == 1 else self.width
        return ref.at[pl.ds(self.src_row0, n), :]

    def shard_half(self, ref, h):
        if self.axis == 1:
            return ref.at[pl.ds(self.src_row0 + h * (self.rows // 2), self.rows // 2), :]
        return ref.at[pl.ds(self.src_row0, self.width), pl.ds(h * (self.cols // 2), self.cols // 2)]

    def full_shard(self, ref, s):
        if self.axis == 1:
            return ref.at[:, pl.ds(s * self.width, self.width)]
        return ref.at[pl.ds(s * self.width, self.width), :]

    def full_shard_half(self, ref, s, h):
        if self.axis == 1:
            return ref.at[pl.ds(h * (self.rows // 2), self.rows // 2), pl.ds(s * self.width, self.width)]
        return ref.at[pl.ds(s * self.width, self.width), pl.ds(h * (self.cols // 2), self.cols // 2)]

    def full_half(self, ref, h):
        if self.axis == 1:
            return ref.at[pl.ds(h * (self.rows // 2), self.rows // 2), :]
        return ref.at[:, pl.ds(h * (self.cols // 2), self.cols // 2)]

    def full_half_rows(self, ref, h, r0, n):
        if self.axis == 1:
            return ref.at[pl.ds(h * (self.rows // 2) + r0, n), :]
        return ref.at[pl.ds(r0, n), pl.ds(h * (self.cols // 2), self.cols // 2)]

    def half_shard(self, ref, s):
        return self.full_shard(ref, s)


PIECES = (
    _Piece("even_w_in", D, EVEN_IN, 1, 0, 0),
    _Piece("even_w_out", D, D, 0, 1, 0),
    _Piece("ffn_w1_0", D, D_FF, 1, 4, 0),
    _Piece("ffn_w2_0", D_FF, D, 0, 5, 0),
    _Piece("odd_w_in", D, ODD_IN, 1, 2, 0),
    _Piece("odd_w_out", D, D, 0, 3, 0),
    _Piece("ffn_w1_1", D, D_FF, 1, 4, D),
    _Piece("ffn_w2_1", D_FF, D, 0, 5, D_FF // 4),
)
N_PIECES = len(PIECES)
FORWARD_GROUPS = ((0,), (1, 2, 3), (4, 5, 6, 7))
JOIN_GROUPS = ((0, 1, 2, 3), (4, 5))
HOLD_BACK = ("ffn_w2_0", "ffn_w2_1")
N_SHARD_OPERANDS = 6
ANY = pl.BlockSpec(memory_space=pl.ANY)
MESH = pl.DeviceIdType.MESH


def _mesh_place():
    x, y, c = lax.axis_index("x"), lax.axis_index("y"), lax.axis_index("c")
    chips = [(1 - x, y), (x, 1 - y), (1 - x, 1 - y)]
    return x, y, c, chips


def _remote(src, dst, send_sem, recv_sem, dev):
    return pltpu.make_async_remote_copy(src_ref=src, dst_ref=dst, send_sem=send_sem, recv_sem=recv_sem,
                                        device_id=dev, device_id_type=MESH)


HBM = pl.BlockSpec(memory_space=pltpu.HBM)
SEM = pl.BlockSpec(memory_space=pltpu.SEMAPHORE)
SPLIT_PARAMS = pltpu.CompilerParams(has_side_effects=pltpu.SideEffectType.DATAFLOW_SIDE_EFFECTING)
CAST_TILE = 256


def _in_hbm(a):
    return pltpu.with_memory_space_constraint(a, pltpu.HBM)


def _cast_place(pc, shard_operand, chip, tie=None):
    rows, cols = (pc.rows, pc.width) if pc.axis == 1 else (pc.width, pc.cols)
    nblk = rows // CAST_TILE
    blk0 = pc.src_row0 // CAST_TILE
    ties = () if tie is None else (tie,)

    def body(chip_ref, x_ref, *rest):
        del chip_ref
        rest[-1][...] = x_ref[...].astype(BF16)

    if pc.axis == 1:
        out_map = lambda i, chip_ref: (i, chip_ref[0])
    else:
        out_map = lambda i, chip_ref: (chip_ref[0] * nblk + i, 0)
    return _pcall(
        body, name=f"cast_{pc.name}",
        grid_spec=pltpu.PrefetchScalarGridSpec(
            num_scalar_prefetch=1, grid=(nblk,),
            in_specs=[pl.BlockSpec((CAST_TILE, cols), lambda i, chip_ref: (blk0 + i, 0))]
            + [pl.BlockSpec(TOKEN_SHAPE, lambda i, chip_ref: (0, 0))] * len(ties),
            out_specs=pl.BlockSpec((CAST_TILE, cols), out_map)),
        out_shape=jax.ShapeDtypeStruct(pc.full_shape, BF16),
        compiler_params=_params("parallel"),
    )(chip, shard_operand, *ties)


def _gather_start(name, pieces, fulls):
    n = len(pieces)

    def body(*refs):
        ins = refs[:n]
        sends = refs[2 * n:3 * n]
        recvs = refs[3 * n:4 * n]
        token = refs[4 * n]
        x, y, c, chips = _mesh_place()
        s = 2 * x + y
        for i, pc in enumerate(pieces):
            win = pc.full_shard_half(ins[i], s, c)
            for k, (cx, cy) in enumerate(chips):
                _remote(win, win, sends[i].at[k], recvs[i].at[k], (cx, cy, c)).start()
        token[...] = jnp.zeros(TOKEN_SHAPE, F32)

    sems = [pltpu.SemaphoreType.DMA((3,))] * (2 * n)
    outs = _pcall(
        body, name=name,
        in_specs=[HBM] * n,
        out_specs=[HBM] * n + [SEM] * (2 * n) + [pl.BlockSpec(memory_space=pltpu.VMEM)],
        out_shape=[pltpu.HBM(pc.full_shape, BF16) for pc in pieces] + sems + [jax.ShapeDtypeStruct(TOKEN_SHAPE, F32)],
        input_output_aliases={i: i for i in range(n)},
        compiler_params=SPLIT_PARAMS,
    )(*[_in_hbm(f) for f in fulls])
    return outs[:n], outs[n:2 * n], outs[2 * n:3 * n], outs[3 * n]


def _gather_wait(pc, full, send_sems, recv_sems, after):
    def body(full_ref, send_ref, recv_ref, after_ref, out_ref):
        del after_ref, out_ref
        x, y, c, chips = _mesh_place()
        for k, (cx, cy) in enumerate(chips):
            win = pc.full_shard_half(full_ref, 2 * cx + cy, c)
            cp = _remote(win, win, send_ref.at[k], recv_ref.at[k], (cx, cy, c))
            cp.wait_send()
            cp.wait_recv()

    return _pcall(
        body, name=f"gather_wait_{pc.name}",
        in_specs=[HBM, SEM, SEM, ANY], out_specs=HBM, out_shape=pltpu.HBM(pc.full_shape, BF16),
        input_output_aliases={0: 0}, compiler_params=SPLIT_PARAMS,
    )(full, send_sems, recv_sems, after)


def _core_forward(pieces, fulls):
    n = len(pieces)

    def body(*refs):
        ins, outs = refs[:n], refs[n:2 * n]
        send_bufs, recv_bufs = refs[2 * n:3 * n], refs[3 * n:4 * n]
        load_sems, send_sems, recv_sems, store_sems = refs[4 * n:]
        x, y, c, chips = _mesh_place()
        loads, sends, stores = [], [], []
        for i, pc in enumerate(pieces):
            for k, (cx, cy) in enumerate(chips):
                cp = pltpu.make_async_copy(pc.full_shard_half(ins[i], 2 * cx + cy, c), send_bufs[i].at[k],
                                           load_sems.at[3 * i + k])
                cp.start()
                loads.append(cp)
        for i in range(n):
            for k in range(3):
                j = 3 * i + k
                loads[j].wait()
                cp = _remote(send_bufs[i].at[k], recv_bufs[i].at[k], send_sems.at[j], recv_sems.at[j], (x, y, 1 - c))
                cp.start()
                sends.append(cp)
        for i, pc in enumerate(pieces):
            for k, (cx, cy) in enumerate(chips):
                j = 3 * i + k
                sends[j].wait_recv()
                cp = pltpu.make_async_copy(recv_bufs[i].at[k], pc.full_shard_half(outs[i], 2 * cx + cy, 1 - c),
                                           store_sems.at[j])
                cp.start()
                stores.append(cp)
        for j in range(3 * n):
            sends[j].wait_send()
            stores[j].wait()

    sems = pltpu.SemaphoreType.DMA((3 * n,))
    bufs = [pltpu.VMEM((3,) + pc.shard_half_shape, BF16) for pc in pieces]
    return _pcall(
        body, name="core_forward_" + pieces[0].name, in_specs=[ANY] * n, out_specs=[ANY] * n,
        out_shape=[jax.ShapeDtypeStruct(pc.full_shape, BF16) for pc in pieces],
        scratch_shapes=bufs + bufs + [sems, sems, sems, sems],
        input_output_aliases={i: i for i in range(n)},
        compiler_params=pltpu.CompilerParams(vmem_limit_bytes=VMEM_LIMIT),
    )(*fulls)


CHIPSUM_CHUNKS = 4


def _chipsum(pieces, partials):
    n = len(pieces)
    nch = CHIPSUM_CHUNKS

    def body(*refs):
        g_refs, out_refs = refs[:n], refs[n:2 * n]
        bufs = refs[2 * n:6 * n]
        load_sems, own_sems, send_sems, recv_sems, out_sems = refs[6 * n:]
        x, y, c, _ = _mesh_place()
        loads, owns, sends, stores, rows_of = [], [], [], [], []
        for i, pc in enumerate(pieces):
            send_buf, own_buf = bufs[4 * i], bufs[4 * i + 2]
            ch = pc.half_shape[0] // nch
            for k in range(nch):
                rows = pl.ds(k * ch, ch)
                rows_of.append(rows)
                cp = pltpu.make_async_copy(pc.full_half_rows(g_refs[i], 1 - c, k * ch, ch), send_buf.at[rows, :],
                                           load_sems.at[nch * i + k])
                cp.start()
                loads.append(cp)
                cp = pltpu.make_async_copy(pc.full_half_rows(g_refs[i], c, k * ch, ch), own_buf.at[rows, :],
                                           own_sems.at[nch * i + k])
                cp.start()
                owns.append(cp)
        for i in range(n):
            send_buf, recv_buf = bufs[4 * i], bufs[4 * i + 1]
            for k in range(nch):
                j = nch * i + k
                loads[j].wait()
                cp = _remote(send_buf.at[rows_of[j], :], recv_buf.at[rows_of[j], :], send_sems.at[j], recv_sems.at[j],
                             (x, y, 1 - c))
                cp.start()
                sends.append(cp)
        for i in range(n):
            recv_buf, own_buf, sum_buf = bufs[4 * i + 1], bufs[4 * i + 2], bufs[4 * i + 3]
            for k in range(nch):
                j = nch * i + k
                rows = rows_of[j]
                owns[j].wait()
                sends[j].wait_recv()
                sum_buf[rows, :] = (own_buf[rows, :].astype(F32) + recv_buf[rows, :].astype(F32)).astype(BF16)
                cp = pltpu.make_async_copy(sum_buf.at[rows, :], out_refs[i].at[rows, :], out_sems.at[j])
                cp.start()
                stores.append(cp)
        for j in range(nch * n):
            sends[j].wait_send()
            stores[j].wait()

    sems = pltpu.SemaphoreType.DMA((nch * n,))
    return _pcall(
        body, name="chipsum_" + pieces[0].name, in_specs=[ANY] * n, out_specs=[ANY] * n,
        out_shape=[jax.ShapeDtypeStruct(pc.half_shape, BF16) for pc in pieces],
        scratch_shapes=[pltpu.VMEM(pc.half_shape, BF16) for pc in pieces for _ in range(4)] + [sems] * 5,
        compiler_params=pltpu.CompilerParams(vmem_limit_bytes=VMEM_LIMIT),
    )(*partials)


def _scatter_start(pieces, chip_sums):
    n = len(pieces)

    def body(*refs):
        sums, lands = refs[:n], refs[n:2 * n]
        sends, recvs = refs[4 * n:5 * n], refs[5 * n:6 * n]
        token = refs[6 * n]
        x, y, c, chips = _mesh_place()
        for i, pc in enumerate(pieces):
            for k, (cx, cy) in enumerate(chips):
                _remote(pc.half_shard(sums[i], 2 * cx + cy), lands[i].at[k], sends[i].at[k], recvs[i].at[k],
                        (cx, cy, c)).start()
        token[...] = jnp.zeros(TOKEN_SHAPE, F32)

    land_shapes = [(3,) + pc.shard_half_shape for pc in pieces]
    sems = [pltpu.SemaphoreType.DMA((3,))] * (2 * n)
    outs = _pcall(
        body, name="scatter_start_" + pieces[0].name,
        in_specs=[HBM] * (2 * n), out_specs=[HBM] * (2 * n) + [SEM] * (2 * n) + [pl.BlockSpec(memory_space=pltpu.VMEM)],
        out_shape=[pltpu.HBM(pc.half_shape, BF16) for pc in pieces] + [pltpu.HBM(sh, BF16) for sh in land_shapes]
        + sems + [jax.ShapeDtypeStruct(TOKEN_SHAPE, F32)],
        input_output_aliases={i: i for i in range(2 * n)}, compiler_params=SPLIT_PARAMS,
    )(*[_in_hbm(cs) for cs in chip_sums], *[_in_hbm(lax.empty(sh, BF16)) for sh in land_shapes])
    return [(outs[i], outs[n + i], outs[2 * n + i], outs[3 * n + i]) for i in range(n)], outs[4 * n]


def _scatter_wait(pc, chip_sum, land, send_sems, recv_sems, after):
    def body(sum_ref, land_ref, send_ref, recv_ref, after_ref, sum_out, land_out):
        del after_ref, sum_out, land_out
        x, y, c, chips = _mesh_place()
        for k, (cx, cy) in enumerate(chips):
            cp = _remote(pc.half_shard(sum_ref, 2 * cx + cy), land_ref.at[k], send_ref.at[k], recv_ref.at[k], (cx, cy, c))
            cp.wait_send()
            cp.wait_recv()

    return _pcall(
        body, name=f"scatter_wait_{pc.name}",
        in_specs=[HBM, HBM, SEM, SEM, ANY], out_specs=[HBM, HBM],
        out_shape=[pltpu.HBM(pc.half_shape, BF16), pltpu.HBM((3,) + pc.shard_half_shape, BF16)],
        input_output_aliases={0: 0, 1: 1}, compiler_params=SPLIT_PARAMS,
    )(chip_sum, land, send_sems, recv_sems, after)


SHARD_OPERAND_SHAPES = ((D, EVEN_IN // 4), (D // 4, D), (D, ODD_IN // 4), (D // 4, D), (2 * D, D_FF // 4), (2 * D_FF // 4, D))


def _allsum_join(operands, chip_sums, lands):
    pieces = [pc for pc in PIECES if pc.src in operands]
    n = len(pieces)
    n_out = len(operands)

    def body(*refs):
        sum_refs = refs[:n]
        land_refs = refs[n:2 * n]
        out_refs = dict(zip(operands, refs[2 * n:2 * n + n_out]))
        refs = refs[2 * n + n_out:]
        in_bufs, fin_bufs, recv_bufs = refs[:n], refs[n:2 * n], refs[2 * n:3 * n]
        load_sems, send_sems, recv_sems, out_sems = refs[3 * n:]
        x, y, c, _ = _mesh_place()
        s = 2 * x + y
        loads, sends, stores = [], [], []
        for j, pc in enumerate(pieces):
            cp = pltpu.make_async_copy(land_refs[j], in_bufs[j].at[pl.ds(0, 3)], load_sems.at[2 * j])
            cp.start()
            loads.append(cp)
            cp = pltpu.make_async_copy(pc.half_shard(sum_refs[j], s), in_bufs[j].at[3], load_sems.at[2 * j + 1])
            cp.start()
            loads.append(cp)
        for j, pc in enumerate(pieces):
            loads[2 * j].wait()
            loads[2 * j + 1].wait()
            acc = in_bufs[j][0].astype(F32)
            for k in range(1, 4):
                acc = acc + in_bufs[j][k].astype(F32)
            fin_bufs[j][...] = acc
            cp = pltpu.make_async_copy(fin_bufs[j], pc.shard_half(out_refs[pc.src], c), out_sems.at[2 * j])
            cp.start()
            stores.append(cp)
            cp = _remote(fin_bufs[j], recv_bufs[j], send_sems.at[j], recv_sems.at[j], (x, y, 1 - c))
            cp.start()
            sends.append(cp)
        for j, pc in enumerate(pieces):
            sends[j].wait_recv()
            cp = pltpu.make_async_copy(recv_bufs[j], pc.shard_half(out_refs[pc.src], 1 - c), out_sems.at[2 * j + 1])
            cp.start()
            stores.append(cp)
        for cp in sends:
            cp.wait_send()
        for cp in stores:
            cp.wait()

    halves = [pc.shard_half_shape for pc in pieces]
    return _pcall(
        body, name=f"allsum_join_{operands[0]}", in_specs=[ANY] * (2 * n), out_specs=[ANY] * n_out,
        out_shape=[jax.ShapeDtypeStruct(SHARD_OPERAND_SHAPES[o], F32) for o in operands],
        scratch_shapes=[pltpu.VMEM((4,) + sh, BF16) for sh in halves] + [pltpu.VMEM(sh, F32) for sh in halves] * 2
        + [pltpu.SemaphoreType.DMA((2 * n,)), pltpu.SemaphoreType.DMA((n,)), pltpu.SemaphoreType.DMA((n,)),
           pltpu.SemaphoreType.DMA((2 * n,))],
        compiler_params=pltpu.CompilerParams(vmem_limit_bytes=VMEM_LIMIT),
    )(*chip_sums, *lands)


PEER_FLIPS = tuple((a, b, e) for a in (0, 1) for b in (0, 1) for e in (0, 1) if (a, b, e) != (0, 0, 0))


def _peers():
    x, y, c = lax.axis_index("x"), lax.axis_index("y"), lax.axis_index("c")
    me = 4 * x + 2 * y + c
    out = []
    for a, b, e in PEER_FLIPS:
        px, py, pc = (1 - x if a else x), (1 - y if b else y), (1 - c if e else c)
        out.append(((px, py, pc), 4 * px + 2 * py + pc))
    return me, out


def _exchange8_start(name, blk):
    m = blk.shape[0]

    def body(blk_ref, land_ref, blk_out, land_out, sends, recvs, token):
        del blk_out, land_out
        me, peers = _peers()
        for k, (dev, _) in enumerate(peers):
            _remote(blk_ref, land_ref.at[me], sends.at[k], recvs.at[k], dev).start()
        token[...] = jnp.zeros(TOKEN_SHAPE, F32)

    sems = pltpu.SemaphoreType.DMA((7,))
    return _pcall(
        body, name=name,
        in_specs=[HBM, HBM], out_specs=[HBM, HBM, SEM, SEM, pl.BlockSpec(memory_space=pltpu.VMEM)],
        out_shape=[pltpu.HBM((m, LANES), F32), pltpu.HBM((8, m, LANES), F32), sems, sems,
                   jax.ShapeDtypeStruct(TOKEN_SHAPE, F32)],
        input_output_aliases={0: 0, 1: 1}, compiler_params=SPLIT_PARAMS,
    )(_in_hbm(blk), _in_hbm(lax.empty((8, m, LANES), F32)))


def _exchange8_wait(name, blk, land, send_sems, recv_sems, after):
    def body(blk_ref, land_ref, send_ref, recv_ref, after_ref, blk_out, land_out):
        del after_ref, blk_out, land_out
        _, peers = _peers()
        for k, (dev, slot) in enumerate(peers):
            cp = _remote(blk_ref, land_ref.at[slot], send_ref.at[k], recv_ref.at[k], dev)
            cp.wait_send()
            cp.wait_recv()

    m = blk.shape[0]
    return _pcall(
        body, name=name,
        in_specs=[HBM, HBM, SEM, SEM, ANY], out_specs=[HBM, HBM],
        out_shape=[pltpu.HBM((m, LANES), F32), pltpu.HBM((8, m, LANES), F32)],
        input_output_aliases={0: 0, 1: 1}, compiler_params=SPLIT_PARAMS,
    )(blk, land, send_sems, recv_sems, after)


def _collect8(name, blk, land, with_sum):
    m = blk.shape[0]

    def body(blk_ref, land_ref, out_ref, *scratch):
        sems = scratch[-1]
        dst = scratch[0] if with_sum else out_ref
        me, peers = _peers()
        copies = [pltpu.make_async_copy(blk_ref, dst.at[me], sems.at[7])]
        for k, (_, slot) in enumerate(peers):
            copies.append(pltpu.make_async_copy(land_ref.at[slot], dst.at[slot], sems.at[k]))
        for cp in copies:
            cp.start()
        for cp in copies:
            cp.wait()
        if with_sum:
            acc = dst[0]
            for dev in range(1, 8):
                acc = acc + dst[dev]
            out_ref[...] = acc

    all_shape = (8, m, LANES)
    return _pcall(
        body, name=name, in_specs=[ANY, ANY], out_specs=pl.BlockSpec(memory_space=pltpu.VMEM),
        out_shape=jax.ShapeDtypeStruct((m, LANES) if with_sum else all_shape, F32),
        scratch_shapes=([pltpu.VMEM(all_shape, F32)] if with_sum else []) + [pltpu.SemaphoreType.DMA((8,))],
    )(blk, land)


def _pack(arrays, row_counts):
    rows = []
    for a, n in zip(arrays, row_counts):
        flat = a.reshape(-1, LANES)
        rows.append(jnp.pad(flat, ((0, n - flat.shape[0]), (0, 0))))
    return jnp.concatenate(rows, axis=0)


def _unpack(buf, shapes, row_counts):
    out, r0 = [], 0
    for sh, n in zip(shapes, row_counts):
        size = 1
        for dim in sh:
            size *= dim
        out.append(buf[r0:r0 + size // LANES].reshape(sh))
        r0 += n
    return out


REPL_NAMES = ("norm_mix_g", "norm_ffn_g", "even_conv_b", "even_ln_g", "even_ln_b", "odd_sg_w", "odd_sg_b", "final_g")
REPL_SHAPES = ((2, D), (2, D), (1, 512), (1, 512), (1, 512), (1, SG_GROUPS, CHUNK, CHUNK), (1, SG_GROUPS, CHUNK), (D,))
REPL_ROWS = (16, 16, 8, 8, 8, 512, 8, 8)
SHARDED_NAMES = ("even_conv_k", "odd_conv_k", "odd_ln_g", "odd_ln_b")
SHARDED_SHARD_SHAPES = ((1, CONV_W, LANES), (1, SCONV_W, LANES), (1, LANES), (1, LANES))
SHARDED_SHARD_ROWS = (32, 8, 8, 8)
SHARDED_FULL_SHAPES = ((CONV_W, 512), (SCONV_W, 512), (1, 512), (1, 512))
SHARDED_FULL_ROWS = (128, 16, 8, 8)


def kernel(x, norm_mix_g, norm_ffn_g, even_w_in, even_conv_k, even_conv_b, even_ln_g, even_ln_b, even_w_out, odd_w_in, odd_conv_k, odd_ln_g, odd_ln_b, odd_sg_w, odd_sg_b, odd_w_out, ffn_w1, ffn_w2, final_g, loss_target, m_norm_mix_g, m_norm_ffn_g, m_even_w_in, m_even_conv_k, m_even_conv_b, m_even_ln_g, m_even_ln_b, m_even_w_out, m_odd_w_in, m_odd_conv_k, m_odd_ln_g, m_odd_ln_b, m_odd_sg_w, m_odd_sg_b, m_odd_w_out, m_ffn_w1, m_ffn_w2, m_final_g, v_norm_mix_g, v_norm_ffn_g, v_even_w_in, v_even_conv_k, v_even_conv_b, v_even_ln_g, v_even_ln_b, v_even_w_out, v_odd_w_in, v_odd_conv_k, v_odd_ln_g, v_odd_ln_b, v_odd_sg_w, v_odd_sg_b, v_odd_w_out, v_ffn_w1, v_ffn_w2, v_final_g):
    names = ("norm_mix_g", "norm_ffn_g", "even_w_in", "even_conv_k", "even_conv_b", "even_ln_g", "even_ln_b", "even_w_out",
             "odd_w_in", "odd_conv_k", "odd_ln_g", "odd_ln_b", "odd_sg_w", "odd_sg_b", "odd_w_out", "ffn_w1", "ffn_w2", "final_g")
    w = dict(zip(names, (norm_mix_g, norm_ffn_g, even_w_in, even_conv_k, even_conv_b, even_ln_g, even_ln_b, even_w_out,
                         odd_w_in, odd_conv_k, odd_ln_g, odd_ln_b, odd_sg_w, odd_sg_b, odd_w_out, ffn_w1, ffn_w2, final_g)))
    mom = dict(zip(names, (m_norm_mix_g, m_norm_ffn_g, m_even_w_in, m_even_conv_k, m_even_conv_b, m_even_ln_g, m_even_ln_b,
                           m_even_w_out, m_odd_w_in, m_odd_conv_k, m_odd_ln_g, m_odd_ln_b, m_odd_sg_w, m_odd_sg_b, m_odd_w_out,
                           m_ffn_w1, m_ffn_w2, m_final_g)))
    vel = dict(zip(names, (v_norm_mix_g, v_norm_ffn_g, v_even_w_in, v_even_conv_k, v_even_conv_b, v_even_ln_g, v_even_ln_b,
                           v_even_w_out, v_odd_w_in, v_odd_conv_k, v_odd_ln_g, v_odd_ln_b, v_odd_sg_w, v_odd_sg_b, v_odd_w_out,
                           v_ffn_w1, v_ffn_w2, v_final_g)))
    big_names = ("even_w_in", "even_w_out", "odd_w_in", "odd_w_out", "ffn_w1", "ffn_w2")
    chip = 2 * lax.axis_index("x") + lax.axis_index("y")

    def shard2d(t, name):
        return t[name].reshape(SHARD_OPERAND_SHAPES[big_names.index(name)])

    chip_op = jnp.reshape(chip, (1,)).astype(jnp.int32)
    small_pack = _pack([w[n] for n in SHARDED_NAMES], SHARDED_SHARD_ROWS)
    small_blk, small_land, small_send, small_recv, small_token = _exchange8_start("gather_small_start", small_pack)
    first = _cast_place(PIECES[0], shard2d(w, big_names[PIECES[0].src]), chip_op, tie=small_token)
    fly0, send0, recv0, token = _gather_start("gather_start_first", PIECES[:1], [first])
    placed = [_cast_place(pc, shard2d(w, big_names[pc.src]), chip_op, tie=token) for pc in PIECES[1:]]
    fly1, send1, recv1, all_started = _gather_start("gather_start_rest", PIECES[1:], placed)
    flying, gather_send, gather_recv = fly0 + fly1, send0 + send1, recv0 + recv1
    ready = {}

    names_in_order = [pc.name for pc in PIECES]

    def weight(name, after):
        if name not in ready:
            i = names_in_order.index(name)
            group = next(grp for grp in FORWARD_GROUPS if i in grp)
            if i == 0:
                after = all_started
            landed = [_gather_wait(PIECES[j], flying[j], gather_send[j], gather_recv[j], after) for j in group]
            ready.update(zip((PIECES[j].name for j in group), _core_forward([PIECES[j] for j in group], landed)))
        return ready[name]

    scattering = []
    held = []

    def emit(name, partial):
        held.append((PIECES[names_in_order.index(name)], partial))
        if name in HOLD_BACK:
            return None
        pieces = [pc for pc, _ in held]
        started, token = _scatter_start(pieces, _chipsum(pieces, [part for _, part in held]))
        scattering.extend((pc,) + tuple(st) for pc, st in zip(pieces, started))
        held.clear()
        return token

    full = {}
    small_blk, small_land = _exchange8_wait("gather_small_wait", small_blk, small_land, small_send, small_recv, all_started)
    gathered = _collect8("gather_small_collect", small_blk, small_land, False)
    gathered = gathered.reshape(4, 2, sum(SHARDED_SHARD_ROWS), LANES)[:, 0]
    r0 = 0
    for n, sh, rows, full_sh in zip(SHARDED_NAMES, SHARDED_SHARD_SHAPES, SHARDED_SHARD_ROWS, SHARDED_FULL_SHAPES):
        per_chip = gathered[:, r0:r0 + rows].reshape(4, -1)[:, :full_sh[0] * LANES].reshape(4, full_sh[0], LANES)
        full[n] = jnp.transpose(per_chip, (1, 0, 2)).reshape(full_sh)
        r0 += rows
    p = dict(full)
    p.update(norm_mix_g0=norm_mix_g[0:1], norm_mix_g1=norm_mix_g[1:2], norm_ffn_g0=norm_ffn_g[0:1], norm_ffn_g1=norm_ffn_g[1:2],
             even_conv_b=even_conv_b, even_ln_g=even_ln_g, even_ln_b=even_ln_b,
             odd_sg_w=odd_sg_w[0], odd_sg_bt=odd_sg_b[0].T, final_g=final_g[None, :])

    small = {}

    def emit_small(loss_row, g):
        parts = [loss_row, g["norm_mix_g1"], g["norm_ffn_g0"], g["norm_ffn_g1"], g["even_conv_b"], g["even_ln_g"],
                 g["even_ln_b"], g["odd_sg_w"], g["odd_sg_bt"].T, g["final_g"],
                 g["even_conv_k"], g["odd_conv_k"], g["odd_ln_g"], g["odd_ln_b"]]
        pack = _pack(parts, (8, 8, 8, 8) + REPL_ROWS[2:] + SHARDED_FULL_ROWS)
        small["blk"], small["land"], small["send"], small["recv"], token = _exchange8_start("allreduce_small_start", pack)
        return token

    dx, dg0 = _local_step(x[0], loss_target[0], p, weight, emit, emit_small)
    last_blk, last_land, last_send, last_recv, grad_token = _exchange8_start("allreduce_last_start", _pack([dg0], (8,)))

    landed = {pc.name: _scatter_wait(pc, chip_sum, land, send_sems, recv_sems, grad_token)
              for pc, chip_sum, land, send_sems, recv_sems in scattering}
    big_grads = {}
    for operands in JOIN_GROUPS:
        pieces = [pc for pc in PIECES if pc.src in operands]
        joined = _allsum_join(operands, [landed[pc.name][0] for pc in pieces], [landed[pc.name][1] for pc in pieces])
        big_grads.update(zip((big_names[o] for o in operands), joined))

    joined_last = big_grads[big_names[JOIN_GROUPS[-1][-1]]]
    grad_blk, grad_land = _exchange8_wait("allreduce_small_wait", small["blk"], small["land"], small["send"],
                                          small["recv"], joined_last)
    grad_sum = _collect8("allreduce_small_sum", grad_blk, grad_land, True)
    last_blk, last_land = _exchange8_wait("allreduce_last_wait", last_blk, last_land, last_send, last_recv, joined_last)
    dg0_sum = _collect8("allreduce_last_sum", last_blk, last_land, True)
    loss = grad_sum[0, 0]
    shapes = ((1, D),) + REPL_SHAPES[1:] + SHARDED_FULL_SHAPES
    parts = _unpack(grad_sum[8:], shapes, (8,) + REPL_ROWS[1:] + SHARDED_FULL_ROWS)
    grads = dict(zip(REPL_NAMES, parts[:len(REPL_NAMES)]))
    grads["norm_mix_g"] = (jnp.pad(dg0_sum[:8].reshape(1, D), ((0, 1), (0, 0)))
                           + jnp.pad(grads["norm_mix_g"], ((1, 0), (0, 0))))
    for n, full_g, sh in zip(SHARDED_NAMES, parts[len(REPL_NAMES):], SHARDED_SHARD_SHAPES):
        grads[n] = lax.dynamic_slice_in_dim(full_g, chip * LANES, LANES, axis=1).reshape(sh)

    delta, new_m, new_v = {}, {}, {}
    for n in big_names:
        d2, m2, v2, g2 = _adamw(f"adamw_{n}", shard2d(w, n), big_grads[n], shard2d(mom, n), shard2d(vel, n), True)
        delta[n], new_m[n], new_v[n], grads[n] = (t.reshape(w[n].shape) for t in (d2, m2, v2, g2))
    for tag, group, rows, shapes in (("repl", REPL_NAMES, REPL_ROWS, [w[n].shape for n in REPL_NAMES]),
                                     ("sharded", SHARDED_NAMES, SHARDED_SHARD_ROWS, SHARDED_SHARD_SHAPES)):
        packs = [_pack([t[n] for n in group], rows) for t in (w, grads, mom, vel)]
        outs = _adamw(f"adamw_{tag}", *packs)
        for res, o in zip((delta, new_m, new_v), outs):
            res.update(zip(group, _unpack(o, shapes, rows)))

    out = [loss, dx[None]]
    for res in (grads, delta, new_m, new_v):
        out.extend(res[n] for n in names)
    return tuple(out)
```

```python
import functools

import jax
import jax.numpy as jnp
from jax import lax
from jax.experimental import pallas as pl
from jax.experimental.pallas import tpu as pltpu

F32 = jnp.float32
BF16 = jnp.bfloat16

T = 2048
D = 1024
CONV_CH = 512
CONV_W = 31
HEAD_DIM = 64
ATT_W = 1536
EVEN_IN = 5632
ODD_IN = 2560
SCONV_W = 3
SG_GROUPS = 4
CHUNK = 128
D_FF = 4096
EPS = 1e-6
DILATIONS = (1, 4, 16)
BAND = 128
SCALE = HEAD_DIM ** -0.5
NEG = -1e30

ADAM_LR = 0.001
ADAM_B1 = 0.9
ADAM_B2 = 0.999
ADAM_EPS = 1e-08
ADAM_WD = 0.01
ADAM_STEP = 10

V7X_VMEM_BYTES = 64 * 2 ** 20
VMEM_LIMIT = V7X_VMEM_BYTES - 8 * 2 ** 20
LANES = 128
TOKEN_SHAPE = (8, LANES)


def _pcall(body, **kw):
    return pl.pallas_call(body, **kw)


def _params(*sem):
    return pltpu.CompilerParams(dimension_semantics=sem, vmem_limit_bytes=VMEM_LIMIT)


def _dot(a, b, dims):
    return lax.dot_general(a, b, (dims, ((), ())), preferred_element_type=F32)


def _nn(a, b):
    return _dot(a, b, ((1,), (0,)))


def _nt(a, b):
    return _dot(a, b, ((1,), (1,)))


def _tn(a, b):
    return _dot(a, b, ((0,), (0,)))


def _sigmoid(x):
    return 1.0 / (1.0 + jnp.exp(-x))


MM_VMEM_BUDGET = 40 * 2 ** 20


def _mm_tiles(mode, m, n, k, a_bytes, b_bytes, extra_bytes, out_bytes):
    def divisors(total, unit):
        return [t for t in range(unit, total + 1, unit) if total % t == 0]

    best = None
    for tm in divisors(m, LANES if mode == "tn" else 8):
        for tn in divisors(n, LANES):
            blocks = tm * k * a_bytes + tn * k * b_bytes + tm * tn * (extra_bytes + out_bytes)
            casts = (tm * k * 2 if a_bytes == 4 else 0) + (tn * k * 2 if b_bytes == 4 else 0)
            if 2 * blocks + casts + tm * tn * 4 > MM_VMEM_BUDGET:
                continue
            key = ((m // tm) * (n // tn), (m // tm) * n * k * b_bytes, abs(tm - tn))
            if best is None or key < best[0]:
                best = (key, tm, tn)
    return best[1], best[2]


def _mm(name, mode, a, b, m, n, k, out_dtypes, *, b_off=0, extras=(), epi=None, tie=None):
    tm, tn = _mm_tiles(mode, m, n, k, a.dtype.itemsize, b.dtype.itemsize, sum(e.dtype.itemsize for e in extras),
                       sum(jnp.dtype(dt).itemsize for dt in out_dtypes))
    assert b_off % tn == 0
    b_off //= tn
    if mode == "nn":
        a_spec = pl.BlockSpec((tm, k), lambda i, j: (i, 0))
        b_spec = pl.BlockSpec((k, tn), lambda i, j: (0, j + b_off))
        dims = ((1,), (0,))
    elif mode == "nt":
        a_spec = pl.BlockSpec((tm, k), lambda i, j: (i, 0))
        b_spec = pl.BlockSpec((tn, k), lambda i, j: (j, 0))
        dims = ((1,), (1,))
    else:
        a_spec = pl.BlockSpec((k, tm), lambda i, j: (0, i))
        b_spec = pl.BlockSpec((k, tn), lambda i, j: (0, j))
        dims = ((0,), (0,))
    o_spec = pl.BlockSpec((tm, tn), lambda i, j: (i, j))
    n_extra = len(extras)
    ties = () if tie is None else (tie,)

    def body(a_ref, b_ref, *rest):
        rest = rest[len(ties):]
        acc = _dot(a_ref[...].astype(BF16), b_ref[...].astype(BF16), dims)
        vals = epi(acc, *[e[...] for e in rest[:n_extra]]) if epi is not None else (acc,)
        for o_ref, v in zip(rest[n_extra:], vals):
            o_ref[...] = v.astype(o_ref.dtype)

    outs = _pcall(
        body, name=name, grid=(m // tm, n // tn),
        in_specs=[a_spec, b_spec] + [pl.BlockSpec(TOKEN_SHAPE, lambda i, j: (0, 0))] * len(ties) + [o_spec] * n_extra,
        out_specs=[o_spec] * len(out_dtypes),
        out_shape=[jax.ShapeDtypeStruct((m, n), dt) for dt in out_dtypes],
        compiler_params=_params("parallel", "parallel"),
    )(a, b, *ties, *extras)
    return outs[0] if len(out_dtypes) == 1 else outs


def _row_tile(k, a_bytes, n_row_blocks):
    for tm in (1024, 512, 256, 128):
        if 2 * (tm * k * a_bytes + D * k * 2 + n_row_blocks * tm * D * 4) + tm * D * 4 <= MM_VMEM_BUDGET + 4 * 2 ** 20:
            return tm
    raise ValueError("no row tile fits")


def _mm_out_norm(name, a, b, k, res, g_next):
    tm = _row_tile(k, a.dtype.itemsize, 3)

    def body(a_ref, b_ref, r_ref, g_ref, h_ref, hn_ref):
        h = _nn(a_ref[...].astype(BF16), b_ref[...]) + r_ref[...]
        h_ref[...] = h
        r = lax.rsqrt(jnp.mean(h * h, axis=-1, keepdims=True) + EPS)
        hn_ref[...] = ((h * r) * g_ref[...]).astype(BF16)

    row = pl.BlockSpec((tm, D), lambda i: (i, 0))
    return _pcall(
        body, name=name, grid=(T // tm,),
        in_specs=[pl.BlockSpec((tm, k), lambda i: (i, 0)), pl.BlockSpec((k, D), lambda i: (0, 0)), row,
                  pl.BlockSpec((1, D), lambda i: (0, 0))],
        out_specs=[row, row],
        out_shape=[jax.ShapeDtypeStruct((T, D), F32), jax.ShapeDtypeStruct((T, D), BF16)],
        compiler_params=_params("parallel"),
    )(a, b, res, g_next)


def _mm_dx_norm(name, dz, w, k, h, g, dres, tie=None):
    tm = _row_tile(k, dz.dtype.itemsize, 3)
    ties = () if tie is None else (tie,)

    def body(a_ref, b_ref, *rest):
        h_ref, g_ref, r_ref, dh_ref, dg_ref = rest[len(ties):]
        dy = _nt(a_ref[...].astype(BF16), b_ref[...])
        x = h_ref[...]
        r = lax.rsqrt(jnp.mean(x * x, axis=-1, keepdims=True) + EPS)
        nrm = x * r
        dn = dy * g_ref[...]
        dh_ref[...] = r_ref[...] + r * (dn - nrm * jnp.mean(dn * nrm, axis=-1, keepdims=True))

        @pl.when(pl.program_id(0) == 0)
        def _():
            dg_ref[...] = jnp.zeros_like(dg_ref)

        dg_ref[...] += jnp.sum(dy * nrm, axis=0, keepdims=True)

    row = pl.BlockSpec((tm, D), lambda i: (i, 0))
    vec = pl.BlockSpec((1, D), lambda i: (0, 0))
    return _pcall(
        body, name=name, grid=(T // tm,),
        in_specs=[pl.BlockSpec((tm, k), lambda i: (i, 0)), pl.BlockSpec((D, k), lambda i: (0, 0))]
        + [pl.BlockSpec(TOKEN_SHAPE, lambda i: (0, 0))] * len(ties) + [row, vec, row],
        out_specs=[row, vec],
        out_shape=[jax.ShapeDtypeStruct((T, D), F32), jax.ShapeDtypeStruct((1, D), F32)],
        compiler_params=_params("arbitrary"),
    )(dz, w, *ties, h, g, dres)


def _rms_fwd(name, h, g, tm=512):
    def body(h_ref, g_ref, o_ref):
        x = h_ref[...]
        r = lax.rsqrt(jnp.mean(x * x, axis=-1, keepdims=True) + EPS)
        o_ref[...] = ((x * r) * g_ref[...]).astype(BF16)

    return _pcall(
        body, name=name, grid=(T // tm,),
        in_specs=[pl.BlockSpec((tm, D), lambda i: (i, 0)), pl.BlockSpec((1, D), lambda i: (0, 0))],
        out_specs=pl.BlockSpec((tm, D), lambda i: (i, 0)),
        out_shape=jax.ShapeDtypeStruct((T, D), BF16),
        compiler_params=_params("parallel"),
    )(h, g)


def _loss_head(h, g, target, tm=512):
    def body(h_ref, g_ref, t_ref, dh_ref, dg_ref, loss_ref):
        x = h_ref[...]
        r = lax.rsqrt(jnp.mean(x * x, axis=-1, keepdims=True) + EPS)
        nrm = x * r
        gain = g_ref[...]
        err = nrm * gain - t_ref[...]
        dy = err * (1.0 / D)
        dn = dy * gain
        dh_ref[...] = r * (dn - nrm * jnp.mean(dn * nrm, axis=-1, keepdims=True))

        @pl.when(pl.program_id(0) == 0)
        def _():
            dg_ref[...] = jnp.zeros_like(dg_ref)
            loss_ref[...] = jnp.zeros_like(loss_ref)

        dg_ref[...] += jnp.sum(dy * nrm, axis=0, keepdims=True)
        part = jnp.sum(jnp.sum(err * err, axis=1, keepdims=True), axis=0, keepdims=True) * (0.5 / D)
        loss_ref[...] += jnp.broadcast_to(part, (1, LANES))

    row = pl.BlockSpec((tm, D), lambda i: (i, 0))
    vec = pl.BlockSpec((1, D), lambda i: (0, 0))
    return _pcall(
        body, name="loss_head", grid=(T // tm,),
        in_specs=[row, vec, row], out_specs=[row, vec, pl.BlockSpec((1, LANES), lambda i: (0, 0))],
        out_shape=[jax.ShapeDtypeStruct((T, D), F32), jax.ShapeDtypeStruct((1, D), F32),
                   jax.ShapeDtypeStruct((1, LANES), F32)],
        compiler_params=_params("arbitrary"),
    )(h, g, target)


CONV_TILE = 256
CONV_HALO = 32


def _glu(z):
    return z[:, :CONV_CH] * _sigmoid(z[:, CONV_CH:])


SUBLANES = 8


def _sublane_shifts(win):
    n = win.shape[0]
    return [win] + [win[r:r + n - SUBLANES, :] for r in range(1, SUBLANES)]


def _rows_from(shifts, off, n):
    q, r = divmod(off, SUBLANES)
    return shifts[r][q * SUBLANES:q * SUBLANES + n, :]


def _econv_fwd(zc, conv_k, conv_b, ln_g, ln_b):
    R, H = CONV_TILE, CONV_HALO

    def body(z_ref, zh_ref, k_ref, b_ref, g_ref, be_ref, cv_ref, cat_ref):
        i = pl.program_id(0)
        glu = _glu(z_ref[...])
        halo = _glu(zh_ref[...]) * (i > 0).astype(F32)
        win = _sublane_shifts(jnp.concatenate([halo, glu], axis=0))
        acc = jnp.zeros((R, CONV_CH), F32) + b_ref[...]
        for j in range(CONV_W):
            acc = acc + k_ref[j:j + 1, :] * _rows_from(win, H - (CONV_W - 1) + j, R)
        cv_ref[...] = acc
        mu = jnp.mean(acc, axis=-1, keepdims=True)
        xc = acc - mu
        rstd = lax.rsqrt(jnp.mean(xc * xc, axis=-1, keepdims=True) + EPS)
        ln = xc * rstd * g_ref[...] + be_ref[...]
        cat_ref[...] = (ln * _sigmoid(ln)).astype(BF16)

    vec = pl.BlockSpec((1, CONV_CH), lambda i: (0, 0))
    return _pcall(
        body, name="econv_fwd", grid=(T // R,),
        in_specs=[pl.BlockSpec((R, 2 * CONV_CH), lambda i: (i, 0)),
                  pl.BlockSpec((H, 2 * CONV_CH), lambda i: (jnp.maximum(i * (R // H) - 1, 0), 0)),
                  pl.BlockSpec((CONV_W, CONV_CH), lambda i: (0, 0)), vec, vec, vec],
        out_specs=[pl.BlockSpec((R, CONV_CH), lambda i: (i, 0)), pl.BlockSpec((R, CONV_CH), lambda i: (i, 0))],
        out_shape=[jax.ShapeDtypeStruct((T, CONV_CH), F32), jax.ShapeDtypeStruct((T, D), BF16)],
        compiler_params=_params("parallel"),
    )(zc, zc, conv_k, conv_b, ln_g, ln_b)


def _econv_bwd_ln(cv, dcat, ln_g, ln_b):
    R = CONV_TILE

    def body(cv_ref, d_ref, g_ref, be_ref, dcv_ref, dg_ref, dbe_ref, dcb_ref):
        cv_t = cv_ref[...]
        mu = jnp.mean(cv_t, axis=-1, keepdims=True)
        xc = cv_t - mu
        rstd = lax.rsqrt(jnp.mean(xc * xc, axis=-1, keepdims=True) + EPS)
        xh = xc * rstd
        ln = xh * g_ref[...] + be_ref[...]
        sg = _sigmoid(ln)
        dln = d_ref[...] * (sg * (1.0 + ln * (1.0 - sg)))
        dxh = dln * g_ref[...]
        dcv = rstd * (dxh - jnp.mean(dxh, axis=-1, keepdims=True) - xh * jnp.mean(dxh * xh, axis=-1, keepdims=True))
        dcv_ref[...] = dcv

        @pl.when(pl.program_id(0) == 0)
        def _():
            dg_ref[...] = jnp.zeros_like(dg_ref)
            dbe_ref[...] = jnp.zeros_like(dbe_ref)
            dcb_ref[...] = jnp.zeros_like(dcb_ref)

        dg_ref[...] += jnp.sum(dln * xh, axis=0, keepdims=True)
        dbe_ref[...] += jnp.sum(dln, axis=0, keepdims=True)
        dcb_ref[...] += jnp.sum(dcv, axis=0, keepdims=True)

    vec = pl.BlockSpec((1, CONV_CH), lambda i: (0, 0))
    row = pl.BlockSpec((R, CONV_CH), lambda i: (i, 0))
    vshape = jax.ShapeDtypeStruct((1, CONV_CH), F32)
    return _pcall(
        body, name="econv_bwd_ln", grid=(T // R,),
        in_specs=[row, row, vec, vec], out_specs=[row, vec, vec, vec],
        out_shape=[jax.ShapeDtypeStruct((T, CONV_CH), F32), vshape, vshape, vshape],
        compiler_params=_params("arbitrary"),
    )(cv, dcat, ln_g, ln_b)


def _econv_bwd_conv(dcv, zc, conv_k):
    R, H = CONV_TILE, CONV_HALO
    last = T // R - 1

    def body(d_ref, dn_ref, z_ref, zh_ref, k_ref, dz_ref, dk_ref):
        i = pl.program_id(0)
        z = z_ref[...]
        a_lin = z[:, :CONV_CH]
        sg = _sigmoid(z[:, CONV_CH:])
        glu = a_lin * sg
        halo = _glu(zh_ref[...]) * (i > 0).astype(F32)
        win = _sublane_shifts(jnp.concatenate([halo, glu], axis=0))
        dcv_t = d_ref[...]
        nxt = dn_ref[...] * (i < last).astype(F32)
        winb = _sublane_shifts(jnp.concatenate([dcv_t, nxt], axis=0))

        @pl.when(i == 0)
        def _():
            dk_ref[...] = jnp.zeros_like(dk_ref)

        dglu = jnp.zeros((R, CONV_CH), F32)
        for j in range(CONV_W):
            dk_ref[j:j + 1, :] += jnp.sum(dcv_t * _rows_from(win, H - (CONV_W - 1) + j, R), axis=0, keepdims=True)
            dglu = dglu + k_ref[j:j + 1, :] * _rows_from(winb, CONV_W - 1 - j, R)
        dz_ref[...] = jnp.concatenate([dglu * sg, dglu * a_lin * sg * (1.0 - sg)], axis=1).astype(BF16)

    return _pcall(
        body, name="econv_bwd_conv", grid=(T // R,),
        in_specs=[pl.BlockSpec((R, CONV_CH), lambda i: (i, 0)),
                  pl.BlockSpec((H, CONV_CH), lambda i: (jnp.minimum((i + 1) * (R // H), T // H - 1), 0)),
                  pl.BlockSpec((R, 2 * CONV_CH), lambda i: (i, 0)),
                  pl.BlockSpec((H, 2 * CONV_CH), lambda i: (jnp.maximum(i * (R // H) - 1, 0), 0)),
                  pl.BlockSpec((CONV_W, CONV_CH), lambda i: (0, 0))],
        out_specs=[pl.BlockSpec((R, 2 * CONV_CH), lambda i: (i, 0)), pl.BlockSpec((CONV_W, CONV_CH), lambda i: (0, 0))],
        out_shape=[jax.ShapeDtypeStruct((T, EVEN_IN), BF16), jax.ShapeDtypeStruct((CONV_W, CONV_CH), F32)],
        compiler_params=_params("arbitrary"),
    )(dcv, dcv, zc, zc, conv_k)


def _swap_halves(v):
    lane = lax.broadcasted_iota(jnp.int32, v.shape, 1)
    return jnp.where((lane % HEAD_DIM) < HEAD_DIM // 2, pltpu.roll(v, LANES - HEAD_DIM // 2, 1),
                     pltpu.roll(v, HEAD_DIM // 2, 1))


def _qkv_proj(hn, w_in, rope_c, rope_s, tm=T):
    tn = 4 * LANES

    def body(a_ref, b_ref, c_ref, s_ref, o_ref):
        j = pl.program_id(1)
        acc = _nn(a_ref[...], b_ref[...])
        for p in range(4):
            v = acc[:, p * LANES:(p + 1) * LANES]
            rot = v * c_ref[...] + _swap_halves(v) * s_ref[...]
            o_ref[p] = jnp.where(j < 6, rot, v)

    tab = pl.BlockSpec((tm, LANES), lambda i, j: (i, 0))
    return _pcall(
        body, name="qkv_proj", grid=(T // tm, 9),
        in_specs=[pl.BlockSpec((tm, D), lambda i, j: (i, 0)),
                  pl.BlockSpec((D, tn), lambda i, j: (0, j + (2 * CONV_CH) // tn)), tab, tab],
        out_specs=pl.BlockSpec((None, 4, tm, LANES), lambda i, j: (j, 0, i, 0)),
        out_shape=jax.ShapeDtypeStruct((9, 4, T, LANES), F32),
        compiler_params=_params("parallel", "parallel"),
    )(hn, w_in, rope_c, rope_s)


ATTN_FWD_UNROLL = 4
ATTN_BWD_UNROLL = 4


def _band_rows(start, d):
    if d == 1:
        return pl.ds(pl.multiple_of(start, BAND), BAND)
    return pl.ds(start, BAND, stride=d)


def _band_masks(n):
    row = lax.broadcasted_iota(jnp.int32, (BAND, BAND), 0)
    col = lax.broadcasted_iota(jnp.int32, (BAND, BAND), 1)
    no_prev = (n == 0).astype(jnp.int32) * (2 * BAND)
    return col <= row, col >= row + no_prev


def _attn_fwd(qkv, g):
    d = DILATIONS[g]
    nb = T // d // BAND

    def body(q_ref, k_ref, v_ref, o_ref, l_ref):
        lane_lo = lax.broadcasted_iota(jnp.int32, (BAND, LANES), 1) < HEAD_DIM

        heads = (lane_lo, jnp.logical_not(lane_lo))
        ones = jnp.ones((BAND, LANES), BF16)

        def step(it, carry):
            tiles = []
            for u in range(ATTN_FWD_UNROLL):
                idx = it * ATTN_FWD_UNROLL + u
                r = idx // nb
                n = idx % nb
                cur = _band_rows(n * (BAND * d) + r, d)
                prev = _band_rows(jnp.maximum(n - 1, 0) * (BAND * d) + r, d)
                mc, mp = _band_masks(n)
                tiles.append((cur, mc, mp, q_ref[cur, :], k_ref[cur, :].astype(BF16), v_ref[cur, :].astype(BF16),
                              k_ref[prev, :].astype(BF16), v_ref[prev, :].astype(BF16)))
            scores = []
            for cur, mc, mp, q, kc, vc, kp, vp in tiles:
                for hm in heads:
                    qm = jnp.where(hm, q, 0.0).astype(BF16)
                    scores.append((jnp.where(mc, _nt(qm, kc) * SCALE, NEG), jnp.where(mp, _nt(qm, kp) * SCALE, NEG)))
            maxes = [jnp.maximum(jnp.max(sc, axis=1, keepdims=True), jnp.max(sp, axis=1, keepdims=True))
                     for sc, sp in scores]
            probs = [(jnp.exp(sc - mx).astype(BF16), jnp.exp(sp - mx).astype(BF16))
                     for (sc, sp), mx in zip(scores, maxes)]
            dens = [_nn(pc, ones) + _nn(pp, ones) for pc, pp in probs]
            for t, (cur, mc, mp, q, kc, vc, kp, vp) in enumerate(tiles):
                outs, lses = [], []
                for h in range(2):
                    pc, pp = probs[2 * t + h]
                    outs.append((_nn(pc, vc) + _nn(pp, vp)) / dens[2 * t + h])
                    lses.append(maxes[2 * t + h] + jnp.log(dens[2 * t + h]))
                o_ref[cur, :] = jnp.where(lane_lo, outs[0], outs[1])
                l_ref[cur, :] = jnp.where(lane_lo, lses[0], lses[1])
            return carry

        lax.fori_loop(0, d * nb // ATTN_FWD_UNROLL, step, 0)

    def slab(which):
        return pl.BlockSpec((None, None, T, LANES), lambda p: (which * 3 + g, p, 0, 0))

    out = pl.BlockSpec((None, T, LANES), lambda p: (p, 0, 0))
    shape = jax.ShapeDtypeStruct((4, T, LANES), F32)
    return _pcall(
        body, name=f"attn_fwd{g}", grid=(4,),
        in_specs=[slab(0), slab(1), slab(2)], out_specs=[out, out], out_shape=[shape, shape],
        compiler_params=_params("parallel"),
    )(qkv, qkv, qkv)


def _attn_merge(outs, lses, cat, tm=1024):
    def body(o0, o1, o2, l0, l1, l2, cat_in, cat_ref, att_ref, w0, w1, w2):
        del cat_in
        la, lb, lc = l0[...], l1[...], l2[...]
        mx = jnp.maximum(jnp.maximum(la, lb), lc)
        ea, eb, ec = jnp.exp(la - mx), jnp.exp(lb - mx), jnp.exp(lc - mx)
        inv = 1.0 / (ea + eb + ec)
        wa, wb, wc = ea * inv, eb * inv, ec * inv
        att = wa * o0[...] + wb * o1[...] + wc * o2[...]
        att_ref[...] = att
        cat_ref[...] = att.astype(BF16)
        w0[...] = wa
        w1[...] = wb
        w2[...] = wc

    slab = pl.BlockSpec((None, tm, LANES), lambda p, i: (p, i, 0))
    shape = jax.ShapeDtypeStruct((4, T, LANES), F32)
    return _pcall(
        body, name="attn_merge", grid=(4, T // tm),
        in_specs=[slab] * 6 + [pl.BlockSpec(memory_space=pl.ANY)],
        out_specs=[pl.BlockSpec((tm, LANES), lambda p, i: (i, CONV_CH // LANES + p)), slab, slab, slab, slab],
        out_shape=[jax.ShapeDtypeStruct((T, D), BF16), shape, shape, shape, shape],
        input_output_aliases={6: 0},
        compiler_params=_params("parallel", "parallel"),
    )(*outs, *lses, cat)


def _attn_bwd(qkv, lse, wgt, att, dcat, dqkv, g):
    d = DILATIONS[g]
    nb = T // d // BAND

    def body(q_ref, k_ref, v_ref, l_ref, w_ref, a_ref, da_ref, dq_in, o_ref):
        del dq_in
        lane = lax.broadcasted_iota(jnp.int32, (BAND, LANES), 1)
        lane_lo = lane < HEAD_DIM
        row = lax.broadcasted_iota(jnp.int32, (LANES, LANES), 0)
        same_head = ((row // HEAD_DIM) == (lane // HEAD_DIM)).astype(BF16)
        dq_ref, dk_ref, dv_ref = o_ref.at[0], o_ref.at[1], o_ref.at[2]
        dk_ref[...] = jnp.zeros((T, LANES), F32)
        dv_ref[...] = jnp.zeros((T, LANES), F32)

        heads = (lane_lo, jnp.logical_not(lane_lo))

        def step(it, carry):
            tiles = []
            for u in range(ATTN_BWD_UNROLL):
                idx = it * ATTN_BWD_UNROLL + u
                r = idx // nb
                n = idx % nb
                cur = _band_rows(n * (BAND * d) + r, d)
                prev = _band_rows(jnp.maximum(n - 1, 0) * (BAND * d) + r, d)
                mc, mp = _band_masks(n)
                da = da_ref[cur, :]
                prod = da * a_ref[cur, :]
                hi = prod.astype(BF16)
                lo = (prod - hi.astype(F32)).astype(BF16)
                tiles.append(dict(cur=cur, prev=prev, mc=mc, mp=mp, da=da, hi=hi, lo=lo, q=q_ref[cur, :],
                                  kc=k_ref[cur, :].astype(BF16), vc=v_ref[cur, :].astype(BF16),
                                  kp=k_ref[prev, :].astype(BF16), vp=v_ref[prev, :].astype(BF16),
                                  lse=l_ref[cur, :], w=w_ref[cur, :]))
            for t in tiles:
                t["csum"] = _nn(t["hi"], same_head) + _nn(t["lo"], same_head)
            chains = []
            for t in tiles:
                for h, hm in enumerate(heads):
                    qm = jnp.where(hm, t["q"], 0.0).astype(BF16)
                    dam = jnp.where(hm, t["da"], 0.0).astype(BF16)
                    chains.append(dict(t=t, h=h, qm=qm, dam=dam,
                                       sc=jnp.where(t["mc"], _nt(qm, t["kc"]) * SCALE, NEG),
                                       sp=jnp.where(t["mp"], _nt(qm, t["kp"]) * SCALE, NEG),
                                       dpc=_nt(dam, t["vc"]), dpp=_nt(dam, t["vp"])))
            for ch in chains:
                t, col0 = ch["t"], ch["h"] * HEAD_DIM
                lse_h = t["lse"][:, col0:col0 + 1]
                w_h = t["w"][:, col0:col0 + 1]
                c_h = t["csum"][:, col0:col0 + 1]
                pwc = w_h * jnp.exp(ch["sc"] - lse_h)
                pwp = w_h * jnp.exp(ch["sp"] - lse_h)
                ch["dsc"] = (pwc * (ch["dpc"] - c_h) * SCALE).astype(BF16)
                ch["dsp"] = (pwp * (ch["dpp"] - c_h) * SCALE).astype(BF16)
                ch["pwc"] = pwc.astype(BF16)
                ch["pwp"] = pwp.astype(BF16)
            for ch in chains:
                t = ch["t"]
                ch["dq"] = _nn(ch["dsc"], t["kc"]) + _nn(ch["dsp"], t["kp"])
                ch["dkc"] = _tn(ch["dsc"], ch["qm"])
                ch["dkp"] = _tn(ch["dsp"], ch["qm"])
                ch["dvc"] = _tn(ch["pwc"], ch["dam"])
                ch["dvp"] = _tn(ch["pwp"], ch["dam"])
            for i, t in enumerate(tiles):
                c0, c1 = chains[2 * i], chains[2 * i + 1]
                dq_ref[t["cur"], :] = jnp.where(lane_lo, c0["dq"], c1["dq"])
                dk_ref[t["cur"], :] += c0["dkc"] + c1["dkc"]
                dk_ref[t["prev"], :] += c0["dkp"] + c1["dkp"]
                dv_ref[t["cur"], :] += c0["dvc"] + c1["dvc"]
                dv_ref[t["prev"], :] += c0["dvp"] + c1["dvp"]
            return carry

        lax.fori_loop(0, d * nb // ATTN_BWD_UNROLL, step, 0)

    def slab(which):
        return pl.BlockSpec((None, None, T, LANES), lambda p: (which * 3 + g, p, 0, 0))

    per_pair = pl.BlockSpec((None, T, LANES), lambda p: (p, 0, 0))
    return _pcall(
        body, name=f"attn_bwd{g}", grid=(4,),
        in_specs=[slab(0), slab(1), slab(2), per_pair, per_pair, per_pair,
                  pl.BlockSpec((T, LANES), lambda p: (0, CONV_CH // LANES + p)),
                  pl.BlockSpec(memory_space=pl.ANY)],
        out_specs=pl.BlockSpec((None, 3, None, T, LANES), lambda p: (g, 0, p, 0, 0)),
        out_shape=jax.ShapeDtypeStruct((3, 3, 4, T, LANES), F32),
        input_output_aliases={7: 0},
        compiler_params=_params("parallel"),
    )(qkv, qkv, qkv, lse, wgt, att, dcat, dqkv)


def _rope_bwd(dqkv, rope_c, rope_s, dz):
    wide = 4 * LANES

    def body(d_ref, c_ref, s_ref, dz_in, o_ref):
        del dz_in
        w = pl.program_id(1)
        for p in range(4):
            v = d_ref[p]
            rot = v * c_ref[...] + _swap_halves(v * s_ref[...])
            o_ref[:, p * LANES:(p + 1) * LANES] = jnp.where(w < 2, rot, v).astype(BF16)

    tab = pl.BlockSpec((T, LANES), lambda g, w: (0, 0))
    return _pcall(
        body, name="rope_bwd", grid=(3, 3),
        in_specs=[pl.BlockSpec((None, None, 4, T, LANES), lambda g, w: (g, w, 0, 0, 0)), tab, tab,
                  pl.BlockSpec(memory_space=pl.ANY)],
        out_specs=pl.BlockSpec((T, wide), lambda g, w: (0, (2 * CONV_CH) // wide + w * 3 + g)),
        out_shape=jax.ShapeDtypeStruct((T, EVEN_IN), BF16),
        input_output_aliases={3: 0},
        compiler_params=_params("parallel", "parallel"),
    )(dqkv, rope_c, rope_s, dz)


ODD_TILE = 256
ODD_HALO = 8
GELU_C = 0.7978845608028654
GELU_A = 0.044715


def _gelu(x):
    return 0.5 * x * (1.0 + jnp.tanh(GELU_C * (x + GELU_A * x * x * x)))


def _gelu_grad(x):
    th = jnp.tanh(GELU_C * (x + GELU_A * x * x * x))
    return 0.5 * (1.0 + th) + 0.5 * x * (1.0 - th * th) * GELU_C * (1.0 + 3.0 * GELU_A * x * x)


def _tril():
    row = lax.broadcasted_iota(jnp.int32, (CHUNK, CHUNK), 0)
    col = lax.broadcasted_iota(jnp.int32, (CHUNK, CHUNK), 1)
    return (col <= row).astype(F32)


def _odd_parts(z, zh, i, k_ref, g_ref, be_ref, w_ref, bt_ref):
    R, H = ODD_TILE, ODD_HALO
    gb, gc, xs, uv = z[:, :512], z[:, 512:1024], z[:, 1024:1536], z[:, 1536:]
    halo = zh[:, 512:1024] * zh[:, 1024:1536] * (i > 0).astype(F32)
    win = jnp.concatenate([halo, gc * xs], axis=0)
    cv = jnp.zeros((R, 512), F32)
    for j in range(SCONV_W):
        off = H - (SCONV_W - 1) + j
        cv = cv + k_ref[j:j + 1, :] * win[off:off + R, :]
    ge = _gelu(uv)
    u, v = ge[:, :512], ge[:, 512:]
    mu = jnp.mean(v, axis=-1, keepdims=True)
    xc = v - mu
    rstd = lax.rsqrt(jnp.mean(xc * xc, axis=-1, keepdims=True) + EPS)
    xh = xc * rstd
    vn = xh * g_ref[...] + be_ref[...]
    tril = _tril()
    wms = [(w_ref[g] * tril).astype(BF16) for g in range(SG_GROUPS)]
    rows = []
    for ci in range(R // CHUNK):
        blocks = []
        for g in range(SG_GROUPS):
            blk = vn[ci * CHUNK:(ci + 1) * CHUNK, g * LANES:(g + 1) * LANES].astype(BF16)
            blocks.append(_nn(wms[g], blk) + bt_ref[:, g:g + 1])
        rows.append(jnp.concatenate(blocks, axis=1))
    vmix = jnp.concatenate(rows, axis=0)
    return gb, gc, xs, uv, win, cv, u, rstd, xh, vn, vmix, wms


def _odd_mid_fwd(z, conv_k, ln_g, ln_b, sg_w, sg_bt):
    R, H = ODD_TILE, ODD_HALO

    def body(z_ref, zh_ref, k_ref, g_ref, be_ref, w_ref, bt_ref, o_ref):
        i = pl.program_id(0)
        gb, _, _, _, _, cv, u, _, _, _, vmix, _ = _odd_parts(z_ref[...], zh_ref[...], i, k_ref, g_ref, be_ref, w_ref, bt_ref)
        o_ref[...] = jnp.concatenate([gb * cv, u * vmix], axis=1).astype(BF16)

    vec = pl.BlockSpec((1, 512), lambda i: (0, 0))
    return _pcall(
        body, name="odd_mid_fwd", grid=(T // R,),
        in_specs=[pl.BlockSpec((R, ODD_IN), lambda i: (i, 0)),
                  pl.BlockSpec((H, ODD_IN), lambda i: (jnp.maximum(i * (R // H) - 1, 0), 0)),
                  pl.BlockSpec((SCONV_W, 512), lambda i: (0, 0)), vec, vec,
                  pl.BlockSpec((SG_GROUPS, CHUNK, CHUNK), lambda i: (0, 0, 0)),
                  pl.BlockSpec((CHUNK, SG_GROUPS), lambda i: (0, 0))],
        out_specs=pl.BlockSpec((R, D), lambda i: (i, 0)),
        out_shape=jax.ShapeDtypeStruct((T, D), BF16),
        compiler_params=_params("parallel"),
    )(z, z, conv_k, ln_g, ln_b, sg_w, sg_bt)


def _odd_mid_bwd(z, dcat, conv_k, ln_g, ln_b, sg_w, sg_bt):
    R, H = ODD_TILE, ODD_HALO
    last = T // R - 1

    def body(z_ref, zh_ref, zn_ref, d_ref, dn_ref, k_ref, g_ref, be_ref, w_ref, bt_ref,
             dz_ref, dk_ref, dg_ref, dbe_ref, dw_ref, dbt_ref):
        i = pl.program_id(0)
        z = z_ref[...]
        gb, gc, xs, uv, win, cv, u, rstd, xh, vn, vmix, wms = _odd_parts(z, zh_ref[...], i, k_ref, g_ref, be_ref, w_ref, bt_ref)
        dcat_t = d_ref[...]
        dc, dd = dcat_t[:, :512], dcat_t[:, 512:]

        @pl.when(i == 0)
        def _():
            dk_ref[...] = jnp.zeros_like(dk_ref)
            dg_ref[...] = jnp.zeros_like(dg_ref)
            dbe_ref[...] = jnp.zeros_like(dbe_ref)
            dw_ref[...] = jnp.zeros_like(dw_ref)
            dbt_ref[...] = jnp.zeros_like(dbt_ref)

        dgb = dc * cv
        dcv = dc * gb
        nxt = dn_ref[:, :512] * zn_ref[:, :512] * (i < last).astype(F32)
        winb = jnp.concatenate([dcv, nxt], axis=0)
        dp = jnp.zeros((R, 512), F32)
        for j in range(SCONV_W):
            off = H - (SCONV_W - 1) + j
            dk_ref[j:j + 1, :] += jnp.sum(dcv * win[off:off + R, :], axis=0, keepdims=True)
            ob = SCONV_W - 1 - j
            dp = dp + k_ref[j:j + 1, :] * winb[ob:ob + R, :]
        dgc = dp * xs
        dxs = dp * gc
        du = dd * vmix
        dvmix = dd * u
        tril = _tril()
        rows = []
        for ci in range(R // CHUNK):
            blocks = []
            for g in range(SG_GROUPS):
                sl = (slice(ci * CHUNK, (ci + 1) * CHUNK), slice(g * LANES, (g + 1) * LANES))
                dblk = dvmix[sl]
                dblk16 = dblk.astype(BF16)
                blocks.append(_tn(wms[g], dblk16))
                dw_ref[g] += _nt(dblk16, vn[sl].astype(BF16)) * tril
                dbt_ref[:, g:g + 1] += jnp.sum(dblk, axis=1, keepdims=True)
            rows.append(jnp.concatenate(blocks, axis=1))
        dvn = jnp.concatenate(rows, axis=0)
        dg_ref[...] += jnp.sum(dvn * xh, axis=0, keepdims=True)
        dbe_ref[...] += jnp.sum(dvn, axis=0, keepdims=True)
        dxh = dvn * g_ref[...]
        dv = rstd * (dxh - jnp.mean(dxh, axis=-1, keepdims=True) - xh * jnp.mean(dxh * xh, axis=-1, keepdims=True))
        duv = jnp.concatenate([du, dv], axis=1) * _gelu_grad(uv)
        dz_ref[...] = jnp.concatenate([dgb, dgc, dxs, duv], axis=1).astype(BF16)

    vec = pl.BlockSpec((1, 512), lambda i: (0, 0))
    kspec = pl.BlockSpec((SCONV_W, 512), lambda i: (0, 0))
    wspec = pl.BlockSpec((SG_GROUPS, CHUNK, CHUNK), lambda i: (0, 0, 0))
    bspec = pl.BlockSpec((CHUNK, SG_GROUPS), lambda i: (0, 0))
    nxt_blk = lambda i: (jnp.minimum((i + 1) * (R // H), T // H - 1), 0)
    return _pcall(
        body, name="odd_mid_bwd", grid=(T // R,),
        in_specs=[pl.BlockSpec((R, ODD_IN), lambda i: (i, 0)),
                  pl.BlockSpec((H, ODD_IN), lambda i: (jnp.maximum(i * (R // H) - 1, 0), 0)),
                  pl.BlockSpec((H, ODD_IN), nxt_blk),
                  pl.BlockSpec((R, D), lambda i: (i, 0)),
                  pl.BlockSpec((H, D), nxt_blk),
                  kspec, vec, vec, wspec, bspec],
        out_specs=[pl.BlockSpec((R, ODD_IN), lambda i: (i, 0)), kspec, vec, vec, wspec, bspec],
        out_shape=[jax.ShapeDtypeStruct((T, ODD_IN), BF16), jax.ShapeDtypeStruct((SCONV_W, 512), F32),
                   jax.ShapeDtypeStruct((1, 512), F32), jax.ShapeDtypeStruct((1, 512), F32),
                   jax.ShapeDtypeStruct((SG_GROUPS, CHUNK, CHUNK), F32), jax.ShapeDtypeStruct((CHUNK, SG_GROUPS), F32)],
        compiler_params=_params("arbitrary"),
    )(z, z, z, dcat, dcat, conv_k, ln_g, ln_b, sg_w, sg_bt)


def _ffn_up(tag, hn, weight):
    def act(acc):
        r = jnp.maximum(acc, 0.0)
        return (r * r,)

    return _mm(f"ffn{tag}_up", "nn", hn, weight(f"ffn_w1_{tag}", hn), T, D_FF, D, (BF16,), epi=act)


def _ffn_bwd(tag, h, g, weight, emit, saved, dout):
    hn, f = saved
    du = _mm(f"ffn{tag}_dact", "nt", dout, weight(f"ffn_w2_{tag}", dout), T, D_FF, D, (BF16,),
             epi=lambda acc, ff: (acc * (2.0 * jnp.sqrt(ff.astype(F32))),), extras=(f,))
    tok = emit(f"ffn_w2_{tag}", f, dout)
    tok = emit(f"ffn_w1_{tag}", hn, du, tie=tok)
    return _mm_dx_norm(f"ffn{tag}_dhn", du, weight(f"ffn_w1_{tag}", du), D_FF, h, g, dout, tie=tok)


def _rope_tables():
    half = HEAD_DIM // 2
    inv = 10000.0 ** (-jnp.arange(half, dtype=F32) / half)
    ang = jnp.arange(T, dtype=F32)[:, None] * inv[None, :]
    cos, sin = jnp.cos(ang), jnp.sin(ang)
    c = jnp.tile(jnp.concatenate([cos, cos], axis=1), (1, LANES // HEAD_DIM))
    s = jnp.tile(jnp.concatenate([-sin, sin], axis=1), (1, LANES // HEAD_DIM))
    return c, s


def _local_step(x, target, p, weight, emit, emit_small):
    rope_c, rope_s = _rope_tables()
    grads = {}
    residual = lambda acc, res: (acc + res,)

    hn0 = _rms_fwd("mix0_norm", x, p["norm_mix_g0"])
    zc = _mm("even_in_conv", "nn", hn0, weight("even_w_in", hn0), T, 2 * CONV_CH, D, (F32,))
    qkv = _qkv_proj(hn0, weight("even_w_in", hn0), rope_c, rope_s)
    cv, cat0 = _econv_fwd(zc, p["even_conv_k"], p["even_conv_b"], p["even_ln_g"], p["even_ln_b"])
    att_parts = [_attn_fwd(qkv, g) for g in range(3)]
    outs = [a[0] for a in att_parts]
    lses = [a[1] for a in att_parts]
    cat0, att, w0, w1, w2 = _attn_merge(outs, lses, cat0)
    wgts = (w0, w1, w2)
    h1, hnf0 = _mm_out_norm("even_out", cat0, weight("even_w_out", cat0), D, x, p["norm_ffn_g0"])
    f0 = _ffn_up(0, hnf0, weight)
    h2, hn1 = _mm_out_norm("ffn0_down", f0, weight("ffn_w2_0", f0), D_FF, h1, p["norm_mix_g1"])

    z1 = _mm("odd_in", "nn", hn1, weight("odd_w_in", hn1), T, ODD_IN, D, (F32,))
    cat1 = _odd_mid_fwd(z1, p["odd_conv_k"], p["odd_ln_g"], p["odd_ln_b"], p["odd_sg_w"], p["odd_sg_bt"])
    h3, hnf1 = _mm_out_norm("odd_out", cat1, weight("odd_w_out", cat1), D, h2, p["norm_ffn_g1"])
    f1 = _ffn_up(1, hnf1, weight)
    h4 = _mm("ffn1_down", "nn", f1, weight("ffn_w2_1", f1), T, D, D_FF, (F32,), epi=residual, extras=(h3,))

    dh4, grads["final_g"], loss = _loss_head(h4, p["final_g"], target)

    dh3, grads["norm_ffn_g1"] = _ffn_bwd(1, h3, p["norm_ffn_g1"], weight, emit, (hnf1, f1), dh4)
    tok = emit("odd_w_out", cat1, dh3)
    dcat1 = _mm("odd_out_dx", "nt", dh3, weight("odd_w_out", dh3), T, D, D, (F32,), tie=tok)
    dz1, grads["odd_conv_k"], grads["odd_ln_g"], grads["odd_ln_b"], grads["odd_sg_w"], grads["odd_sg_bt"] = _odd_mid_bwd(
        z1, dcat1, p["odd_conv_k"], p["odd_ln_g"], p["odd_ln_b"], p["odd_sg_w"], p["odd_sg_bt"])
    tok = emit("odd_w_in", hn1, dz1)
    dh2, grads["norm_mix_g1"] = _mm_dx_norm("odd_in_dx", dz1, weight("odd_w_in", dz1), ODD_IN, h2, p["norm_mix_g1"],
                                            dh3, tie=tok)

    dh1, grads["norm_ffn_g0"] = _ffn_bwd(0, h1, p["norm_ffn_g0"], weight, emit, (hnf0, f0), dh2)
    tok = emit("even_w_out", cat0, dh1)
    dcat0 = _mm("even_out_dx", "nt", dh1, weight("even_w_out", dh1), T, D, D, (F32,), tie=tok)
    dcv, grads["even_ln_g"], grads["even_ln_b"], grads["even_conv_b"] = _econv_bwd_ln(
        cv, dcat0, p["even_ln_g"], p["even_ln_b"])
    dz0, grads["even_conv_k"] = _econv_bwd_conv(dcv, zc, p["even_conv_k"])
    tok = emit_small(loss, grads)
    dqkv = lax.empty((3, 3, 4, T, LANES), F32)
    for g in range(3):
        dqkv = _attn_bwd(qkv, lses[g], wgts[g], att, dcat0, dqkv, g)
    dz0 = _rope_bwd(dqkv, rope_c, rope_s, dz0)
    tok = emit("even_w_in", hn0, dz0, tie=tok)
    dx, dg0 = _mm_dx_norm("even_in_dx", dz0, weight("even_w_in", dz0), EVEN_IN, x, p["norm_mix_g0"], dh1, tie=tok)
    return dx, dg0


def _rowwise(name, fn, ins, out_dtypes, tm=256):
    rows, cols = ins[0].shape
    tm = tm if rows % tm == 0 else rows
    n_in = len(ins)

    def body(*refs):
        vals = fn(*[r[...] for r in refs[:n_in]])
        for o_ref, v in zip(refs[n_in:], vals):
            o_ref[...] = v.astype(o_ref.dtype)

    spec = pl.BlockSpec((tm, cols), lambda i: (i, 0))
    outs = _pcall(
        body, name=name, grid=(rows // tm,),
        in_specs=[spec] * n_in, out_specs=[spec] * len(out_dtypes),
        out_shape=[jax.ShapeDtypeStruct((rows, cols), dt) for dt in out_dtypes],
        compiler_params=_params("parallel"),
    )(*ins)
    return outs[0] if len(out_dtypes) == 1 else outs


def _adamw(name, w, g, m, v, with_grad=False):
    c1 = 1.0 - ADAM_B1 ** ADAM_STEP
    c2 = 1.0 - ADAM_B2 ** ADAM_STEP

    def fn(w_t, g_t, m_t, v_t):
        m_new = ADAM_B1 * m_t + (1.0 - ADAM_B1) * g_t
        v_new = ADAM_B2 * v_t + (1.0 - ADAM_B2) * (g_t * g_t)
        delta = -ADAM_LR * ((m_new / c1) / (jnp.sqrt(v_new / c2) + ADAM_EPS) + ADAM_WD * w_t)
        return (delta, m_new, v_new, g_t) if with_grad else (delta, m_new, v_new)

    return _rowwise(name, fn, (w, g, m, v), (F32,) * (4 if with_grad else 3))


class _Piece:
    def __init__(self, name, rows, cols, axis, src, src_row0):
        self.name, self.rows, self.cols, self.axis = name, rows, cols, axis
        self.width = (cols if axis == 1 else rows) // 4
        self.src, self.src_row0 = src, src_row0

    @property
    def full_shape(self):
        return (self.rows, self.cols)

    @property
    def half_shape(self):
        return (self.rows // 2, self.cols) if self.axis == 1 else (self.rows, self.cols // 2)

    @property
    def shard_half_shape(self):
        return (self.rows // 2, self.width) if self.axis == 1 else (self.width, self.cols // 2)

    def shard_whole(self, ref):
        n = self.rows if self.axis == 1 else self.width
        return ref.at[pl.ds(self.src_row0, n), :]

    def shard_half(self, ref, h):
        if self.axis == 1:
            return ref.at[pl.ds(self.src_row0 + h * (self.rows // 2), self.rows // 2), :]
        return ref.at[pl.ds(self.src_row0, self.width), pl.ds(h * (self.cols // 2), self.cols // 2)]

    def full_shard(self, ref, s):
        if self.axis == 1:
            return ref.at[:, pl.ds(s * self.width, self.width)]
        return ref.at[pl.ds(s * self.width, self.width), :]

    def full_shard_half(self, ref, s, h):
        if self.axis == 1:
            return ref.at[pl.ds(h * (self.rows // 2), self.rows // 2), pl.ds(s * self.width, self.width)]
        return ref.at[pl.ds(s * self.width, self.width), pl.ds(h * (self.cols // 2), self.cols // 2)]

    def full_half(self, ref, h):
        if self.axis == 1:
            return ref.at[pl.ds(h * (self.rows // 2), self.rows // 2), :]
        return ref.at[:, pl.ds(h * (self.cols // 2), self.cols // 2)]

    def full_half_rows(self, ref, h, r0, n):
        if self.axis == 1:
            return ref.at[pl.ds(h * (self.rows // 2) + r0, n), :]
        return ref.at[pl.ds(r0, n), pl.ds(h * (self.cols // 2), self.cols // 2)]

    def half_shard(self, ref, s):
        return self.full_shard(ref, s)


PIECES = (
    _Piece("even_w_in", D, EVEN_IN, 1, 0, 0),
    _Piece("even_w_out", D, D, 0, 1, 0),
    _Piece("ffn_w1_0", D, D_FF, 1, 4, 0),
    _Piece("ffn_w2_0", D_FF, D, 0, 5, 0),
    _Piece("odd_w_in", D, ODD_IN, 1, 2, 0),
    _Piece("odd_w_out", D, D, 0, 3, 0),
    _Piece("ffn_w1_1", D, D_FF, 1, 4, D),
    _Piece("ffn_w2_1", D_FF, D, 0, 5, D_FF // 4),
)
N_PIECES = len(PIECES)
FORWARD_GROUPS = ((0,), (1, 2, 3), (4, 5, 6, 7))
JOIN_GROUPS = ((0, 1, 2, 3), (4, 5))
HOLD_BACK = ("ffn_w2_0", "ffn_w2_1", "odd_w_out")
N_SHARD_OPERANDS = 6
ANY = pl.BlockSpec(memory_space=pl.ANY)
MESH = pl.DeviceIdType.MESH


def _mesh_place():
    x, y, c = lax.axis_index("x"), lax.axis_index("y"), lax.axis_index("c")
    chips = [(1 - x, y), (x, 1 - y), (1 - x, 1 - y)]
    return x, y, c, chips


def _remote(src, dst, send_sem, recv_sem, dev):
    return pltpu.make_async_remote_copy(src_ref=src, dst_ref=dst, send_sem=send_sem, recv_sem=recv_sem,
                                        device_id=dev, device_id_type=MESH)


HBM = pl.BlockSpec(memory_space=pltpu.HBM)
SEM = pl.BlockSpec(memory_space=pltpu.SEMAPHORE)
SPLIT_PARAMS = pltpu.CompilerParams(has_side_effects=pltpu.SideEffectType.DATAFLOW_SIDE_EFFECTING)
CAST_TILE = 256


def _in_hbm(a):
    return pltpu.with_memory_space_constraint(a, pltpu.HBM)


def _cast_place(pc, shard_operand, chip, tie=None):
    rows, cols = (pc.rows, pc.width) if pc.axis == 1 else (pc.width, pc.cols)
    nblk = rows // CAST_TILE
    blk0 = pc.src_row0 // CAST_TILE
    ties = () if tie is None else (tie,)

    def body(chip_ref, x_ref, *rest):
        del chip_ref
        rest[-1][...] = x_ref[...].astype(BF16)

    if pc.axis == 1:
        out_map = lambda i, chip_ref: (i, chip_ref[0])
    else:
        out_map = lambda i, chip_ref: (chip_ref[0] * nblk + i, 0)
    return _pcall(
        body, name=f"cast_{pc.name}",
        grid_spec=pltpu.PrefetchScalarGridSpec(
            num_scalar_prefetch=1, grid=(nblk,),
            in_specs=[pl.BlockSpec((CAST_TILE, cols), lambda i, chip_ref: (blk0 + i, 0))]
            + [pl.BlockSpec(TOKEN_SHAPE, lambda i, chip_ref: (0, 0))] * len(ties),
            out_specs=pl.BlockSpec((CAST_TILE, cols), out_map)),
        out_shape=jax.ShapeDtypeStruct(pc.full_shape, BF16),
        compiler_params=_params("parallel"),
    )(chip, shard_operand, *ties)


def _gather_start(name, pieces, fulls):
    n = len(pieces)

    def body(*refs):
        ins = refs[:n]
        sends = refs[2 * n:3 * n]
        recvs = refs[3 * n:4 * n]
        token = refs[4 * n]
        x, y, c, chips = _mesh_place()
        s = 2 * x + y
        for i, pc in enumerate(pieces):
            win = pc.full_shard_half(ins[i], s, c)
            for k, (cx, cy) in enumerate(chips):
                _remote(win, win, sends[i].at[k], recvs[i].at[k], (cx, cy, c)).start()
        token[...] = jnp.zeros(TOKEN_SHAPE, F32)

    sems = [pltpu.SemaphoreType.DMA((3,))] * (2 * n)
    outs = _pcall(
        body, name=name,
        in_specs=[HBM] * n,
        out_specs=[HBM] * n + [SEM] * (2 * n) + [pl.BlockSpec(memory_space=pltpu.VMEM)],
        out_shape=[pltpu.HBM(pc.full_shape, BF16) for pc in pieces] + sems + [jax.ShapeDtypeStruct(TOKEN_SHAPE, F32)],
        input_output_aliases={i: i for i in range(n)},
        compiler_params=SPLIT_PARAMS,
    )(*[_in_hbm(f) for f in fulls])
    return outs[:n], outs[n:2 * n], outs[2 * n:3 * n], outs[3 * n]


def _gather_wait(pc, full, send_sems, recv_sems, after):
    def body(full_ref, send_ref, recv_ref, after_ref, out_ref):
        del after_ref, out_ref
        x, y, c, chips = _mesh_place()
        for k, (cx, cy) in enumerate(chips):
            win = pc.full_shard_half(full_ref, 2 * cx + cy, c)
            cp = _remote(win, win, send_ref.at[k], recv_ref.at[k], (cx, cy, c))
            cp.wait_send()
            cp.wait_recv()

    return _pcall(
        body, name=f"gather_wait_{pc.name}",
        in_specs=[HBM, SEM, SEM, ANY], out_specs=HBM, out_shape=pltpu.HBM(pc.full_shape, BF16),
        input_output_aliases={0: 0}, compiler_params=SPLIT_PARAMS,
    )(full, send_sems, recv_sems, after)


def _core_forward(pieces, fulls):
    n = len(pieces)

    def body(*refs):
        ins, outs = refs[:n], refs[n:2 * n]
        send_bufs, recv_bufs = refs[2 * n:3 * n], refs[3 * n:4 * n]
        load_sems, send_sems, recv_sems, store_sems = refs[4 * n:]
        x, y, c, chips = _mesh_place()
        loads, sends, stores = [], [], []
        for i, pc in enumerate(pieces):
            for k, (cx, cy) in enumerate(chips):
                cp = pltpu.make_async_copy(pc.full_shard_half(ins[i], 2 * cx + cy, c), send_bufs[i].at[k],
                                           load_sems.at[3 * i + k])
                cp.start()
                loads.append(cp)
        for i in range(n):
            for k in range(3):
                j = 3 * i + k
                loads[j].wait()
                cp = _remote(send_bufs[i].at[k], recv_bufs[i].at[k], send_sems.at[j], recv_sems.at[j], (x, y, 1 - c))
                cp.start()
                sends.append(cp)
        for i, pc in enumerate(pieces):
            for k, (cx, cy) in enumerate(chips):
                j = 3 * i + k
                sends[j].wait_recv()
                cp = pltpu.make_async_copy(recv_bufs[i].at[k], pc.full_shard_half(outs[i], 2 * cx + cy, 1 - c),
                                           store_sems.at[j])
                cp.start()
                stores.append(cp)
        for j in range(3 * n):
            sends[j].wait_send()
            stores[j].wait()

    sems = pltpu.SemaphoreType.DMA((3 * n,))
    bufs = [pltpu.VMEM((3,) + pc.shard_half_shape, BF16) for pc in pieces]
    return _pcall(
        body, name="core_forward_" + pieces[0].name, in_specs=[ANY] * n, out_specs=[ANY] * n,
        out_shape=[jax.ShapeDtypeStruct(pc.full_shape, BF16) for pc in pieces],
        scratch_shapes=bufs + bufs + [sems, sems, sems, sems],
        input_output_aliases={i: i for i in range(n)},
        compiler_params=pltpu.CompilerParams(vmem_limit_bytes=VMEM_LIMIT),
    )(*fulls)


def _dw_tile(pc):
    if pc.axis == 1:
        tn = max(t for t in range(LANES, pc.cols + 1, LANES) if pc.cols % t == 0 and t <= 1408)
        return pc.rows // 2, tn
    return min(pc.rows, 1024), pc.cols // 2


def _mm_dw_chipsum(pc, a, b, core, tie=None):
    tm, tn = _dw_tile(pc)
    hr, hc = pc.half_shape
    tiles_r, tiles_c = hr // tm, hc // tn
    th = tiles_r * tiles_c
    ties = () if tie is None else (tie,)

    def tile_of(s, core_ref):
        mine = s >= th
        half = jnp.where(mine, core_ref[0], 1 - core_ref[0])
        local = s % th
        li, lj = local // tiles_c, local % tiles_c
        if pc.axis == 1:
            return half * tiles_r + li, lj, li, lj, mine
        return li, half * tiles_c + lj, li, lj, mine

    def body(core_ref, a_ref, b_ref, *rest):
        o_ref, send_buf, recv_buf, send_sems, recv_sems = rest[len(ties):]
        s = pl.program_id(0)
        local = s % th
        x, y, c = lax.axis_index("x"), lax.axis_index("y"), lax.axis_index("c")
        acc = _tn(a_ref[...].astype(BF16), b_ref[...].astype(BF16))

        def push(slot):
            return _remote(send_buf.at[slot], recv_buf.at[slot], send_sems.at[slot], recv_sems.at[slot], (x, y, 1 - c))

        @pl.when(s < th)
        def _():
            send_buf[local] = acc.astype(BF16)
            push(local).start()

        @pl.when(s >= th)
        def _():
            push(local).wait_recv()
            o_ref[...] = (acc + recv_buf[local].astype(F32)).astype(BF16)

        @pl.when(s == 2 * th - 1)
        def _():
            for slot in range(th):
                push(slot).wait_send()

    def a_map(s, core_ref):
        return 0, tile_of(s, core_ref)[0]

    def b_map(s, core_ref):
        return 0, tile_of(s, core_ref)[1]

    def o_map(s, core_ref):
        _, _, li, lj, mine = tile_of(s, core_ref)
        return jnp.where(mine, li, 0), jnp.where(mine, lj, 0)

    return _pcall(
        body, name=f"dw_{pc.name}",
        grid_spec=pltpu.PrefetchScalarGridSpec(
            num_scalar_prefetch=1, grid=(2 * th,),
            in_specs=[pl.BlockSpec((T, tm), a_map), pl.BlockSpec((T, tn), b_map)]
            + [pl.BlockSpec(TOKEN_SHAPE, lambda s, core_ref: (0, 0))] * len(ties),
            out_specs=pl.BlockSpec((tm, tn), o_map),
            scratch_shapes=[pltpu.VMEM((th, tm, tn), BF16), pltpu.VMEM((th, tm, tn), BF16),
                            pltpu.SemaphoreType.DMA((th,)), pltpu.SemaphoreType.DMA((th,))]),
        out_shape=jax.ShapeDtypeStruct((hr, hc), BF16),
        compiler_params=_params("arbitrary"),
    )(core, a, b, *ties)


def _scatter_start(pieces, chip_sums):
    n = len(pieces)

    def body(*refs):
        sums, lands = refs[:n], refs[n:2 * n]
        sends, recvs = refs[4 * n:5 * n], refs[5 * n:6 * n]
        token = refs[6 * n]
        x, y, c, chips = _mesh_place()
        for i, pc in enumerate(pieces):
            for k, (cx, cy) in enumerate(chips):
                _remote(pc.half_shard(sums[i], 2 * cx + cy), lands[i].at[k], sends[i].at[k], recvs[i].at[k],
                        (cx, cy, c)).start()
        token[...] = jnp.zeros(TOKEN_SHAPE, F32)

    land_shapes = [(3,) + pc.shard_half_shape for pc in pieces]
    sems = [pltpu.SemaphoreType.DMA((3,))] * (2 * n)
    outs = _pcall(
        body, name="scatter_start_" + pieces[0].name,
        in_specs=[HBM] * (2 * n), out_specs=[HBM] * (2 * n) + [SEM] * (2 * n) + [pl.BlockSpec(memory_space=pltpu.VMEM)],
        out_shape=[pltpu.HBM(pc.half_shape, BF16) for pc in pieces] + [pltpu.HBM(sh, BF16) for sh in land_shapes]
        + sems + [jax.ShapeDtypeStruct(TOKEN_SHAPE, F32)],
        input_output_aliases={i: i for i in range(2 * n)}, compiler_params=SPLIT_PARAMS,
    )(*[_in_hbm(cs) for cs in chip_sums], *[_in_hbm(lax.empty(sh, BF16)) for sh in land_shapes])
    return [(outs[i], outs[n + i], outs[2 * n + i], outs[3 * n + i]) for i in range(n)], outs[4 * n]


def _scatter_wait(pc, chip_sum, land, send_sems, recv_sems, after):
    def body(sum_ref, land_ref, send_ref, recv_ref, after_ref, sum_out, land_out):
        del after_ref, sum_out, land_out
        x, y, c, chips = _mesh_place()
        for k, (cx, cy) in enumerate(chips):
            cp = _remote(pc.half_shard(sum_ref, 2 * cx + cy), land_ref.at[k], send_ref.at[k], recv_ref.at[k], (cx, cy, c))
            cp.wait_send()
            cp.wait_recv()

    return _pcall(
        body, name=f"scatter_wait_{pc.name}",
        in_specs=[HBM, HBM, SEM, SEM, ANY], out_specs=[HBM, HBM],
        out_shape=[pltpu.HBM(pc.half_shape, BF16), pltpu.HBM((3,) + pc.shard_half_shape, BF16)],
        input_output_aliases={0: 0, 1: 1}, compiler_params=SPLIT_PARAMS,
    )(chip_sum, land, send_sems, recv_sems, after)


SHARD_OPERAND_SHAPES = ((D, EVEN_IN // 4), (D // 4, D), (D, ODD_IN // 4), (D // 4, D), (2 * D, D_FF // 4), (2 * D_FF // 4, D))


def _allsum_join(operands, chip_sums, lands):
    pieces = [pc for pc in PIECES if pc.src in operands]
    n = len(pieces)
    n_out = len(operands)

    def body(*refs):
        sum_refs = refs[:n]
        land_refs = refs[n:2 * n]
        out_refs = dict(zip(operands, refs[2 * n:2 * n + n_out]))
        refs = refs[2 * n + n_out:]
        in_bufs, fin_bufs, recv_bufs = refs[:n], refs[n:2 * n], refs[2 * n:3 * n]
        load_sems, send_sems, recv_sems, out_sems = refs[3 * n:]
        x, y, c, _ = _mesh_place()
        s = 2 * x + y
        loads, sends, stores = [], [], []
        for j, pc in enumerate(pieces):
            cp = pltpu.make_async_copy(land_refs[j], in_bufs[j].at[pl.ds(0, 3)], load_sems.at[2 * j])
            cp.start()
            loads.append(cp)
            cp = pltpu.make_async_copy(pc.half_shard(sum_refs[j], s), in_bufs[j].at[3], load_sems.at[2 * j + 1])
            cp.start()
            loads.append(cp)
        for j, pc in enumerate(pieces):
            loads[2 * j].wait()
            loads[2 * j + 1].wait()
            acc = in_bufs[j][0].astype(F32)
            for k in range(1, 4):
                acc = acc + in_bufs[j][k].astype(F32)
            fin_bufs[j][...] = acc
            cp = pltpu.make_async_copy(fin_bufs[j], pc.shard_half(out_refs[pc.src], c), out_sems.at[2 * j])
            cp.start()
            stores.append(cp)
            cp = _remote(fin_bufs[j], recv_bufs[j], send_sems.at[j], recv_sems.at[j], (x, y, 1 - c))
            cp.start()
            sends.append(cp)
        for j, pc in enumerate(pieces):
            sends[j].wait_recv()
            cp = pltpu.make_async_copy(recv_bufs[j], pc.shard_half(out_refs[pc.src], 1 - c), out_sems.at[2 * j + 1])
            cp.start()
            stores.append(cp)
        for cp in sends:
            cp.wait_send()
        for cp in stores:
            cp.wait()

    halves = [pc.shard_half_shape for pc in pieces]
    return _pcall(
        body, name=f"allsum_join_{operands[0]}", in_specs=[ANY] * (2 * n), out_specs=[ANY] * n_out,
        out_shape=[jax.ShapeDtypeStruct(SHARD_OPERAND_SHAPES[o], F32) for o in operands],
        scratch_shapes=[pltpu.VMEM((4,) + sh, BF16) for sh in halves] + [pltpu.VMEM(sh, F32) for sh in halves] * 2
        + [pltpu.SemaphoreType.DMA((2 * n,)), pltpu.SemaphoreType.DMA((n,)), pltpu.SemaphoreType.DMA((n,)),
           pltpu.SemaphoreType.DMA((2 * n,))],
        compiler_params=pltpu.CompilerParams(vmem_limit_bytes=VMEM_LIMIT),
    )(*chip_sums, *lands)


PEER_FLIPS = tuple((a, b, e) for a in (0, 1) for b in (0, 1) for e in (0, 1) if (a, b, e) != (0, 0, 0))


def _peers():
    x, y, c = lax.axis_index("x"), lax.axis_index("y"), lax.axis_index("c")
    me = 4 * x + 2 * y + c
    out = []
    for a, b, e in PEER_FLIPS:
        px, py, pc = (1 - x if a else x), (1 - y if b else y), (1 - c if e else c)
        out.append(((px, py, pc), 4 * px + 2 * py + pc))
    return me, out


def _exchange8_start(name, blk):
    m = blk.shape[0]

    def body(blk_ref, land_ref, blk_out, land_out, sends, recvs, token):
        del blk_out, land_out
        me, peers = _peers()
        for k, (dev, _) in enumerate(peers):
            _remote(blk_ref, land_ref.at[me], sends.at[k], recvs.at[k], dev).start()
        token[...] = jnp.zeros(TOKEN_SHAPE, F32)

    sems = pltpu.SemaphoreType.DMA((7,))
    return _pcall(
        body, name=name,
        in_specs=[HBM, HBM], out_specs=[HBM, HBM, SEM, SEM, pl.BlockSpec(memory_space=pltpu.VMEM)],
        out_shape=[pltpu.HBM((m, LANES), F32), pltpu.HBM((8, m, LANES), F32), sems, sems,
                   jax.ShapeDtypeStruct(TOKEN_SHAPE, F32)],
        input_output_aliases={0: 0, 1: 1}, compiler_params=SPLIT_PARAMS,
    )(_in_hbm(blk), _in_hbm(lax.empty((8, m, LANES), F32)))


def _exchange8_wait(name, blk, land, send_sems, recv_sems, after):
    def body(blk_ref, land_ref, send_ref, recv_ref, after_ref, blk_out, land_out):
        del after_ref, blk_out, land_out
        _, peers = _peers()
        for k, (dev, slot) in enumerate(peers):
            cp = _remote(blk_ref, land_ref.at[slot], send_ref.at[k], recv_ref.at[k], dev)
            cp.wait_send()
            cp.wait_recv()

    m = blk.shape[0]
    return _pcall(
        body, name=name,
        in_specs=[HBM, HBM, SEM, SEM, ANY], out_specs=[HBM, HBM],
        out_shape=[pltpu.HBM((m, LANES), F32), pltpu.HBM((8, m, LANES), F32)],
        input_output_aliases={0: 0, 1: 1}, compiler_params=SPLIT_PARAMS,
    )(blk, land, send_sems, recv_sems, after)


def _collect8(name, blk, land, with_sum):
    m = blk.shape[0]

    def body(blk_ref, land_ref, out_ref, *scratch):
        sems = scratch[-1]
        dst = scratch[0] if with_sum else out_ref
        me, peers = _peers()
        copies = [pltpu.make_async_copy(blk_ref, dst.at[me], sems.at[7])]
        for k, (_, slot) in enumerate(peers):
            copies.append(pltpu.make_async_copy(land_ref.at[slot], dst.at[slot], sems.at[k]))
        for cp in copies:
            cp.start()
        for cp in copies:
            cp.wait()
        if with_sum:
            acc = dst[0]
            for dev in range(1, 8):
                acc = acc + dst[dev]
            out_ref[...] = acc

    all_shape = (8, m, LANES)
    return _pcall(
        body, name=name, in_specs=[ANY, ANY], out_specs=pl.BlockSpec(memory_space=pltpu.VMEM),
        out_shape=jax.ShapeDtypeStruct((m, LANES) if with_sum else all_shape, F32),
        scratch_shapes=([pltpu.VMEM(all_shape, F32)] if with_sum else []) + [pltpu.SemaphoreType.DMA((8,))],
    )(blk, land)


def _pack(arrays, row_counts):
    rows = []
    for a, n in zip(arrays, row_counts):
        flat = a.reshape(-1, LANES)
        rows.append(jnp.pad(flat, ((0, n - flat.shape[0]), (0, 0))))
    return jnp.concatenate(rows, axis=0)


def _unpack(buf, shapes, row_counts):
    out, r0 = [], 0
    for sh, n in zip(shapes, row_counts):
        size = 1
        for dim in sh:
            size *= dim
        out.append(buf[r0:r0 + size // LANES].reshape(sh))
        r0 += n
    return out


REPL_NAMES = ("norm_mix_g", "norm_ffn_g", "even_conv_b", "even_ln_g", "even_ln_b", "odd_sg_w", "odd_sg_b", "final_g")
REPL_SHAPES = ((2, D), (2, D), (1, 512), (1, 512), (1, 512), (1, SG_GROUPS, CHUNK, CHUNK), (1, SG_GROUPS, CHUNK), (D,))
REPL_ROWS = (16, 16, 8, 8, 8, 512, 8, 8)
SHARDED_NAMES = ("even_conv_k", "odd_conv_k", "odd_ln_g", "odd_ln_b")
SHARDED_SHARD_SHAPES = ((1, CONV_W, LANES), (1, SCONV_W, LANES), (1, LANES), (1, LANES))
SHARDED_SHARD_ROWS = (32, 8, 8, 8)
SHARDED_FULL_SHAPES = ((CONV_W, 512), (SCONV_W, 512), (1, 512), (1, 512))
SHARDED_FULL_ROWS = (128, 16, 8, 8)


def kernel(x, norm_mix_g, norm_ffn_g, even_w_in, even_conv_k, even_conv_b, even_ln_g, even_ln_b, even_w_out, odd_w_in, odd_conv_k, odd_ln_g, odd_ln_b, odd_sg_w, odd_sg_b, odd_w_out, ffn_w1, ffn_w2, final_g, loss_target, m_norm_mix_g, m_norm_ffn_g, m_even_w_in, m_even_conv_k, m_even_conv_b, m_even_ln_g, m_even_ln_b, m_even_w_out, m_odd_w_in, m_odd_conv_k, m_odd_ln_g, m_odd_ln_b, m_odd_sg_w, m_odd_sg_b, m_odd_w_out, m_ffn_w1, m_ffn_w2, m_final_g, v_norm_mix_g, v_norm_ffn_g, v_even_w_in, v_even_conv_k, v_even_conv_b, v_even_ln_g, v_even_ln_b, v_even_w_out, v_odd_w_in, v_odd_conv_k, v_odd_ln_g, v_odd_ln_b, v_odd_sg_w, v_odd_sg_b, v_odd_w_out, v_ffn_w1, v_ffn_w2, v_final_g):
    names = ("norm_mix_g", "norm_ffn_g", "even_w_in", "even_conv_k", "even_conv_b", "even_ln_g", "even_ln_b", "even_w_out",
             "odd_w_in", "odd_conv_k", "odd_ln_g", "odd_ln_b", "odd_sg_w", "odd_sg_b", "odd_w_out", "ffn_w1", "ffn_w2", "final_g")
    w = dict(zip(names, (norm_mix_g, norm_ffn_g, even_w_in, even_conv_k, even_conv_b, even_ln_g, even_ln_b, even_w_out,
                         odd_w_in, odd_conv_k, odd_ln_g, odd_ln_b, odd_sg_w, odd_sg_b, odd_w_out, ffn_w1, ffn_w2, final_g)))
    mom = dict(zip(names, (m_norm_mix_g, m_norm_ffn_g, m_even_w_in, m_even_conv_k, m_even_conv_b, m_even_ln_g, m_even_ln_b,
                           m_even_w_out, m_odd_w_in, m_odd_conv_k, m_odd_ln_g, m_odd_ln_b, m_odd_sg_w, m_odd_sg_b, m_odd_w_out,
                           m_ffn_w1, m_ffn_w2, m_final_g)))
    vel = dict(zip(names, (v_norm_mix_g, v_norm_ffn_g, v_even_w_in, v_even_conv_k, v_even_conv_b, v_even_ln_g, v_even_ln_b,
                           v_even_w_out, v_odd_w_in, v_odd_conv_k, v_odd_ln_g, v_odd_ln_b, v_odd_sg_w, v_odd_sg_b, v_odd_w_out,
                           v_ffn_w1, v_ffn_w2, v_final_g)))
    big_names = ("even_w_in", "even_w_out", "odd_w_in", "odd_w_out", "ffn_w1", "ffn_w2")
    chip = 2 * lax.axis_index("x") + lax.axis_index("y")

    def shard2d(t, name):
        return t[name].reshape(SHARD_OPERAND_SHAPES[big_names.index(name)])

    chip_op = jnp.reshape(chip, (1,)).astype(jnp.int32)
    small_pack = _pack([w[n] for n in SHARDED_NAMES], SHARDED_SHARD_ROWS)
    small_blk, small_land, small_send, small_recv, small_token = _exchange8_start("gather_small_start", small_pack)
    first = _cast_place(PIECES[0], shard2d(w, big_names[PIECES[0].src]), chip_op, tie=small_token)
    fly0, send0, recv0, token = _gather_start("gather_start_first", PIECES[:1], [first])
    placed = [_cast_place(pc, shard2d(w, big_names[pc.src]), chip_op, tie=token) for pc in PIECES[1:]]
    fly1, send1, recv1, all_started = _gather_start("gather_start_rest", PIECES[1:], placed)
    flying, gather_send, gather_recv = fly0 + fly1, send0 + send1, recv0 + recv1
    ready = {}

    names_in_order = [pc.name for pc in PIECES]

    def weight(name, after):
        if name not in ready:
            i = names_in_order.index(name)
            group = next(grp for grp in FORWARD_GROUPS if i in grp)
            if i == 0:
                after = all_started
            landed = [_gather_wait(PIECES[j], flying[j], gather_send[j], gather_recv[j], after) for j in group]
            ready.update(zip((PIECES[j].name for j in group), _core_forward([PIECES[j] for j in group], landed)))
        return ready[name]

    scattering = []
    held = []

    core_op = jnp.reshape(lax.axis_index("c"), (1,)).astype(jnp.int32)

    def emit(name, a, b, tie=None):
        pc = PIECES[names_in_order.index(name)]
        held.append((pc, _mm_dw_chipsum(pc, a, b, core_op, tie)))
        if name in HOLD_BACK:
            return None
        pieces = [pc for pc, _ in held]
        started, token = _scatter_start(pieces, [chip_sum for _, chip_sum in held])
        scattering.extend((pc,) + tuple(st) for pc, st in zip(pieces, started))
        held.clear()
        return token

    full = {}
    small_blk, small_land = _exchange8_wait("gather_small_wait", small_blk, small_land, small_send, small_recv, all_started)
    gathered = _collect8("gather_small_collect", small_blk, small_land, False)
    gathered = gathered.reshape(4, 2, sum(SHARDED_SHARD_ROWS), LANES)[:, 0]
    r0 = 0
    for n, sh, rows, full_sh in zip(SHARDED_NAMES, SHARDED_SHARD_SHAPES, SHARDED_SHARD_ROWS, SHARDED_FULL_SHAPES):
        per_chip = gathered[:, r0:r0 + rows].reshape(4, -1)[:, :full_sh[0] * LANES].reshape(4, full_sh[0], LANES)
        full[n] = jnp.transpose(per_chip, (1, 0, 2)).reshape(full_sh)
        r0 += rows
    p = dict(full)
    p.update(norm_mix_g0=norm_mix_g[0:1], norm_mix_g1=norm_mix_g[1:2], norm_ffn_g0=norm_ffn_g[0:1], norm_ffn_g1=norm_ffn_g[1:2],
             even_conv_b=even_conv_b, even_ln_g=even_ln_g, even_ln_b=even_ln_b,
             odd_sg_w=odd_sg_w[0], odd_sg_bt=odd_sg_b[0].T, final_g=final_g[None, :])

    small = {}

    def emit_small(loss_row, g):
        parts = [loss_row, g["norm_mix_g1"], g["norm_ffn_g0"], g["norm_ffn_g1"], g["even_conv_b"], g["even_ln_g"],
                 g["even_ln_b"], g["odd_sg_w"], g["odd_sg_bt"].T, g["final_g"],
                 g["even_conv_k"], g["odd_conv_k"], g["odd_ln_g"], g["odd_ln_b"]]
        pack = _pack(parts, (8, 8, 8, 8) + REPL_ROWS[2:] + SHARDED_FULL_ROWS)
        small["blk"], small["land"], small["send"], small["recv"], token = _exchange8_start("allreduce_small_start", pack)
        return token

    dx, dg0 = _local_step(x[0], loss_target[0], p, weight, emit, emit_small)
    last_blk, last_land, last_send, last_recv, grad_token = _exchange8_start("allreduce_last_start", _pack([dg0], (8,)))

    landed = {pc.name: _scatter_wait(pc, chip_sum, land, send_sems, recv_sems, grad_token)
              for pc, chip_sum, land, send_sems, recv_sems in scattering}
    big_grads = {}
    for operands in JOIN_GROUPS:
        pieces = [pc for pc in PIECES if pc.src in operands]
        joined = _allsum_join(operands, [landed[pc.name][0] for pc in pieces], [landed[pc.name][1] for pc in pieces])
        big_grads.update(zip((big_names[o] for o in operands), joined))

    joined_last = big_grads[big_names[JOIN_GROUPS[-1][-1]]]
    grad_blk, grad_land = _exchange8_wait("allreduce_small_wait", small["blk"], small["land"], small["send"],
                                          small["recv"], joined_last)
    grad_sum = _collect8("allreduce_small_sum", grad_blk, grad_land, True)
    last_blk, last_land = _exchange8_wait("allreduce_last_wait", last_blk, last_land, last_send, last_recv, joined_last)
    dg0_sum = _collect8("allreduce_last_sum", last_blk, last_land, True)
    loss = grad_sum[0, 0]
    shapes = ((1, D),) + REPL_SHAPES[1:] + SHARDED_FULL_SHAPES
    parts = _unpack(grad_sum[8:], shapes, (8,) + REPL_ROWS[1:] + SHARDED_FULL_ROWS)
    grads = dict(zip(REPL_NAMES, parts[:len(REPL_NAMES)]))
    grads["norm_mix_g"] = (jnp.pad(dg0_sum[:8].reshape(1, D), ((0, 1), (0, 0)))
                           + jnp.pad(grads["norm_mix_g"], ((1, 0), (0, 0))))
    for n, full_g, sh in zip(SHARDED_NAMES, parts[len(REPL_NAMES):], SHARDED_SHARD_SHAPES):
        grads[n] = lax.dynamic_slice_in_dim(full_g, chip * LANES, LANES, axis=1).reshape(sh)

    delta, new_m, new_v = {}, {}, {}
    for n in big_names:
        d2, m2, v2, g2 = _adamw(f"adamw_{n}", shard2d(w, n), big_grads[n], shard2d(mom, n), shard2d(vel, n), True)
        delta[n], new_m[n], new_v[n], grads[n] = (t.reshape(w[n].shape) for t in (d2, m2, v2, g2))
    for tag, group, rows, shapes in (("repl", REPL_NAMES, REPL_ROWS, [w[n].shape for n in REPL_NAMES]),
                                     ("sharded", SHARDED_NAMES, SHARDED_SHARD_ROWS, SHARDED_SHARD_SHAPES)):
        packs = [_pack([t[n] for n in group], rows) for t in (w, grads, mom, vel)]
        outs = _adamw(f"adamw_{tag}", *packs)
        for res, o in zip((delta, new_m, new_v), outs):
            res.update(zip(group, _unpack(o, shapes, rows)))

    out = [loss, dx[None]]
    for res in (grads, delta, new_m, new_v):
        out.extend(res[n] for n in names)
    return tuple(out)
```

```python
import functools

import jax
import jax.numpy as jnp
from jax import lax
from jax.experimental import pallas as pl
from jax.experimental.pallas import tpu as pltpu

F32 = jnp.float32
BF16 = jnp.bfloat16

T = 2048
D = 1024
CONV_CH = 512
CONV_W = 31
HEAD_DIM = 64
ATT_W = 1536
EVEN_IN = 5632
ODD_IN = 2560
SCONV_W = 3
SG_GROUPS = 4
CHUNK = 128
D_FF = 4096
EPS = 1e-6
DILATIONS = (1, 4, 16)
BAND = 128
SCALE = HEAD_DIM ** -0.5
NEG = -1e30

ADAM_LR = 0.001
ADAM_B1 = 0.9
ADAM_B2 = 0.999
ADAM_EPS = 1e-08
ADAM_WD = 0.01
ADAM_STEP = 10

V7X_VMEM_BYTES = 64 * 2 ** 20
VMEM_LIMIT = V7X_VMEM_BYTES - 8 * 2 ** 20
LANES = 128
TOKEN_SHAPE = (8, LANES)


class _SideJob:
    def __init__(self, inputs, out_shape, scratch_shapes, aliases, begin, finish):
        self.inputs, self.out_shape, self.scratch_shapes = list(inputs), list(out_shape), list(scratch_shapes)
        self.aliases, self.begin, self.finish = dict(aliases), begin, finish
        self.results = None


_PENDING_JOBS = []


def _ride_next_call(job):
    _PENDING_JOBS.append(job)


def _pcall(body, **kw):
    if not _PENDING_JOBS or "grid" not in kw:
        return pl.pallas_call(body, **kw)
    job = _PENDING_JOBS.pop()
    as_list = lambda v: list(v) if isinstance(v, (list, tuple)) else [v]
    single_out = not isinstance(kw["out_shape"], (list, tuple))
    in_specs, out_specs, out_shape = as_list(kw["in_specs"]), as_list(kw["out_specs"]), as_list(kw["out_shape"])
    scratch = list(kw.get("scratch_shapes", ()))
    grid = kw["grid"]
    n_in, n_out, n_scr = len(in_specs), len(out_specs), len(scratch)
    j_in, j_out = len(job.inputs), len(job.out_shape)
    any_spec = pl.BlockSpec(memory_space=pl.ANY)

    def hosted(*refs):
        ins, j_ins = refs[:n_in], refs[n_in:n_in + j_in]
        outs = refs[n_in + j_in:n_in + j_in + n_out]
        j_outs = refs[n_in + j_in + n_out:n_in + j_in + n_out + j_out]
        scr = refs[n_in + j_in + n_out + j_out:n_in + j_in + n_out + j_out + n_scr]
        j_scr = refs[n_in + j_in + n_out + j_out + n_scr:]
        first = pl.program_id(0) == 0
        last = pl.program_id(0) == grid[0] - 1
        for axis in range(1, len(grid)):
            first = jnp.logical_and(first, pl.program_id(axis) == 0)
            last = jnp.logical_and(last, pl.program_id(axis) == grid[axis] - 1)

        @pl.when(first)
        def _():
            job.begin(j_ins, j_outs, j_scr)

        body(*ins, *outs, *scr)

        @pl.when(last)
        def _():
            job.finish(j_ins, j_outs, j_scr)

    aliases = dict(kw.get("input_output_aliases", {}))
    aliases.update({n_in + a: n_out + b for a, b in job.aliases.items()})
    call = pl.pallas_call(
        hosted, name=kw["name"], grid=grid,
        in_specs=in_specs + [any_spec] * j_in, out_specs=out_specs + [any_spec] * j_out,
        out_shape=out_shape + job.out_shape, scratch_shapes=scratch + job.scratch_shapes,
        input_output_aliases=aliases,
        compiler_params=pltpu.CompilerParams(dimension_semantics=("arbitrary",) * len(grid), vmem_limit_bytes=VMEM_LIMIT))

    def run(*args):
        res = call(*args, *job.inputs)
        job.results = list(res[n_out:])
        return res[0] if single_out else list(res[:n_out])

    return run


def _params(*sem):
    return pltpu.CompilerParams(dimension_semantics=sem, vmem_limit_bytes=VMEM_LIMIT)


def _dot(a, b, dims):
    return lax.dot_general(a, b, (dims, ((), ())), preferred_element_type=F32)


def _nn(a, b):
    return _dot(a, b, ((1,), (0,)))


def _nt(a, b):
    return _dot(a, b, ((1,), (1,)))


def _tn(a, b):
    return _dot(a, b, ((0,), (0,)))


def _sigmoid(x):
    return 1.0 / (1.0 + jnp.exp(-x))


MM_VMEM_BUDGET = 40 * 2 ** 20


def _mm_tiles(mode, m, n, k, a_bytes, b_bytes, extra_bytes, out_bytes):
    def divisors(total, unit):
        return [t for t in range(unit, total + 1, unit) if total % t == 0]

    best = None
    for tm in divisors(m, LANES if mode == "tn" else 8):
        for tn in divisors(n, LANES):
            blocks = tm * k * a_bytes + tn * k * b_bytes + tm * tn * (extra_bytes + out_bytes)
            casts = (tm * k * 2 if a_bytes == 4 else 0) + (tn * k * 2 if b_bytes == 4 else 0)
            if 2 * blocks + casts + tm * tn * 4 > MM_VMEM_BUDGET:
                continue
            key = ((m // tm) * (n // tn), (m // tm) * n * k * b_bytes, abs(tm - tn))
            if best is None or key < best[0]:
                best = (key, tm, tn)
    return best[1], best[2]


def _mm(name, mode, a, b, m, n, k, out_dtypes, *, b_off=0, extras=(), epi=None, tie=None):
    tm, tn = _mm_tiles(mode, m, n, k, a.dtype.itemsize, b.dtype.itemsize, sum(e.dtype.itemsize for e in extras),
                       sum(jnp.dtype(dt).itemsize for dt in out_dtypes))
    assert b_off % tn == 0
    b_off //= tn
    if mode == "nn":
        a_spec = pl.BlockSpec((tm, k), lambda i, j: (i, 0))
        b_spec = pl.BlockSpec((k, tn), lambda i, j: (0, j + b_off))
        dims = ((1,), (0,))
    elif mode == "nt":
        a_spec = pl.BlockSpec((tm, k), lambda i, j: (i, 0))
        b_spec = pl.BlockSpec((tn, k), lambda i, j: (j, 0))
        dims = ((1,), (1,))
    else:
        a_spec = pl.BlockSpec((k, tm), lambda i, j: (0, i))
        b_spec = pl.BlockSpec((k, tn), lambda i, j: (0, j))
        dims = ((0,), (0,))
    o_spec = pl.BlockSpec((tm, tn), lambda i, j: (i, j))
    n_extra = len(extras)
    ties = () if tie is None else (tie,)

    def body(a_ref, b_ref, *rest):
        rest = rest[len(ties):]
        acc = _dot(a_ref[...].astype(BF16), b_ref[...].astype(BF16), dims)
        vals = epi(acc, *[e[...] for e in rest[:n_extra]]) if epi is not None else (acc,)
        for o_ref, v in zip(rest[n_extra:], vals):
            o_ref[...] = v.astype(o_ref.dtype)

    outs = _pcall(
        body, name=name, grid=(m // tm, n // tn),
        in_specs=[a_spec, b_spec] + [pl.BlockSpec(TOKEN_SHAPE, lambda i, j: (0, 0))] * len(ties) + [o_spec] * n_extra,
        out_specs=[o_spec] * len(out_dtypes),
        out_shape=[jax.ShapeDtypeStruct((m, n), dt) for dt in out_dtypes],
        compiler_params=_params("parallel", "parallel"),
    )(a, b, *ties, *extras)
    return outs[0] if len(out_dtypes) == 1 else outs


def _row_tile(k, a_bytes, n_row_blocks):
    for tm in (1024, 512, 256, 128):
        if 2 * (tm * k * a_bytes + D * k * 2 + n_row_blocks * tm * D * 4) + tm * D * 4 <= MM_VMEM_BUDGET + 4 * 2 ** 20:
            return tm
    raise ValueError("no row tile fits")


def _mm_out_norm(name, a, b, k, res, g_next):
    tm = _row_tile(k, a.dtype.itemsize, 3)

    def body(a_ref, b_ref, r_ref, g_ref, h_ref, hn_ref):
        h = _nn(a_ref[...].astype(BF16), b_ref[...]) + r_ref[...]
        h_ref[...] = h
        r = lax.rsqrt(jnp.mean(h * h, axis=-1, keepdims=True) + EPS)
        hn_ref[...] = ((h * r) * g_ref[...]).astype(BF16)

    row = pl.BlockSpec((tm, D), lambda i: (i, 0))
    return _pcall(
        body, name=name, grid=(T // tm,),
        in_specs=[pl.BlockSpec((tm, k), lambda i: (i, 0)), pl.BlockSpec((k, D), lambda i: (0, 0)), row,
                  pl.BlockSpec((1, D), lambda i: (0, 0))],
        out_specs=[row, row],
        out_shape=[jax.ShapeDtypeStruct((T, D), F32), jax.ShapeDtypeStruct((T, D), BF16)],
        compiler_params=_params("parallel"),
    )(a, b, res, g_next)


def _mm_dx_norm(name, dz, w, k, h, g, dres, tie=None):
    tm = _row_tile(k, dz.dtype.itemsize, 3)
    ties = () if tie is None else (tie,)

    def body(a_ref, b_ref, *rest):
        h_ref, g_ref, r_ref, dh_ref, dg_ref = rest[len(ties):]
        dy = _nt(a_ref[...].astype(BF16), b_ref[...])
        x = h_ref[...]
        r = lax.rsqrt(jnp.mean(x * x, axis=-1, keepdims=True) + EPS)
        nrm = x * r
        dn = dy * g_ref[...]
        dh_ref[...] = r_ref[...] + r * (dn - nrm * jnp.mean(dn * nrm, axis=-1, keepdims=True))

        @pl.when(pl.program_id(0) == 0)
        def _():
            dg_ref[...] = jnp.zeros_like(dg_ref)

        dg_ref[...] += jnp.sum(dy * nrm, axis=0, keepdims=True)

    row = pl.BlockSpec((tm, D), lambda i: (i, 0))
    vec = pl.BlockSpec((1, D), lambda i: (0, 0))
    return _pcall(
        body, name=name, grid=(T // tm,),
        in_specs=[pl.BlockSpec((tm, k), lambda i: (i, 0)), pl.BlockSpec((D, k), lambda i: (0, 0))]
        + [pl.BlockSpec(TOKEN_SHAPE, lambda i: (0, 0))] * len(ties) + [row, vec, row],
        out_specs=[row, vec],
        out_shape=[jax.ShapeDtypeStruct((T, D), F32), jax.ShapeDtypeStruct((1, D), F32)],
        compiler_params=_params("arbitrary"),
    )(dz, w, *ties, h, g, dres)


def _rms_fwd(name, h, g, tm=512):
    def body(h_ref, g_ref, o_ref):
        x = h_ref[...]
        r = lax.rsqrt(jnp.mean(x * x, axis=-1, keepdims=True) + EPS)
        o_ref[...] = ((x * r) * g_ref[...]).astype(BF16)

    return _pcall(
        body, name=name, grid=(T // tm,),
        in_specs=[pl.BlockSpec((tm, D), lambda i: (i, 0)), pl.BlockSpec((1, D), lambda i: (0, 0))],
        out_specs=pl.BlockSpec((tm, D), lambda i: (i, 0)),
        out_shape=jax.ShapeDtypeStruct((T, D), BF16),
        compiler_params=_params("parallel"),
    )(h, g)


def _loss_head(h, g, target, tm=512):
    def body(h_ref, g_ref, t_ref, dh_ref, dg_ref, loss_ref):
        x = h_ref[...]
        r = lax.rsqrt(jnp.mean(x * x, axis=-1, keepdims=True) + EPS)
        nrm = x * r
        gain = g_ref[...]
        err = nrm * gain - t_ref[...]
        dy = err * (1.0 / D)
        dn = dy * gain
        dh_ref[...] = r * (dn - nrm * jnp.mean(dn * nrm, axis=-1, keepdims=True))

        @pl.when(pl.program_id(0) == 0)
        def _():
            dg_ref[...] = jnp.zeros_like(dg_ref)
            loss_ref[...] = jnp.zeros_like(loss_ref)

        dg_ref[...] += jnp.sum(dy * nrm, axis=0, keepdims=True)
        part = jnp.sum(jnp.sum(err * err, axis=1, keepdims=True), axis=0, keepdims=True) * (0.5 / D)
        loss_ref[...] += jnp.broadcast_to(part, (1, LANES))

    row = pl.BlockSpec((tm, D), lambda i: (i, 0))
    vec = pl.BlockSpec((1, D), lambda i: (0, 0))
    return _pcall(
        body, name="loss_head", grid=(T // tm,),
        in_specs=[row, vec, row], out_specs=[row, vec, pl.BlockSpec((1, LANES), lambda i: (0, 0))],
        out_shape=[jax.ShapeDtypeStruct((T, D), F32), jax.ShapeDtypeStruct((1, D), F32),
                   jax.ShapeDtypeStruct((1, LANES), F32)],
        compiler_params=_params("arbitrary"),
    )(h, g, target)


CONV_TILE = 256
CONV_HALO = 32


def _glu(z):
    return z[:, :CONV_CH] * _sigmoid(z[:, CONV_CH:])


SUBLANES = 8


def _sublane_shifts(win):
    n = win.shape[0]
    return [win] + [win[r:r + n - SUBLANES, :] for r in range(1, SUBLANES)]


def _rows_from(shifts, off, n):
    q, r = divmod(off, SUBLANES)
    return shifts[r][q * SUBLANES:q * SUBLANES + n, :]


def _econv_fwd(zc, conv_k, conv_b, ln_g, ln_b):
    R, H = CONV_TILE, CONV_HALO

    def body(z_ref, zh_ref, k_ref, b_ref, g_ref, be_ref, cv_ref, cat_ref):
        i = pl.program_id(0)
        glu = _glu(z_ref[...])
        halo = _glu(zh_ref[...]) * (i > 0).astype(F32)
        win = _sublane_shifts(jnp.concatenate([halo, glu], axis=0))
        acc = jnp.zeros((R, CONV_CH), F32) + b_ref[...]
        for j in range(CONV_W):
            acc = acc + k_ref[j:j + 1, :] * _rows_from(win, H - (CONV_W - 1) + j, R)
        cv_ref[...] = acc
        mu = jnp.mean(acc, axis=-1, keepdims=True)
        xc = acc - mu
        rstd = lax.rsqrt(jnp.mean(xc * xc, axis=-1, keepdims=True) + EPS)
        ln = xc * rstd * g_ref[...] + be_ref[...]
        cat_ref[...] = (ln * _sigmoid(ln)).astype(BF16)

    vec = pl.BlockSpec((1, CONV_CH), lambda i: (0, 0))
    return _pcall(
        body, name="econv_fwd", grid=(T // R,),
        in_specs=[pl.BlockSpec((R, 2 * CONV_CH), lambda i: (i, 0)),
                  pl.BlockSpec((H, 2 * CONV_CH), lambda i: (jnp.maximum(i * (R // H) - 1, 0), 0)),
                  pl.BlockSpec((CONV_W, CONV_CH), lambda i: (0, 0)), vec, vec, vec],
        out_specs=[pl.BlockSpec((R, CONV_CH), lambda i: (i, 0)), pl.BlockSpec((R, CONV_CH), lambda i: (i, 0))],
        out_shape=[jax.ShapeDtypeStruct((T, CONV_CH), F32), jax.ShapeDtypeStruct((T, D), BF16)],
        compiler_params=_params("parallel"),
    )(zc, zc, conv_k, conv_b, ln_g, ln_b)


def _econv_bwd_ln(cv, dcat, ln_g, ln_b):
    R = CONV_TILE

    def body(cv_ref, d_ref, g_ref, be_ref, dcv_ref, dg_ref, dbe_ref, dcb_ref):
        cv_t = cv_ref[...]
        mu = jnp.mean(cv_t, axis=-1, keepdims=True)
        xc = cv_t - mu
        rstd = lax.rsqrt(jnp.mean(xc * xc, axis=-1, keepdims=True) + EPS)
        xh = xc * rstd
        ln = xh * g_ref[...] + be_ref[...]
        sg = _sigmoid(ln)
        dln = d_ref[...] * (sg * (1.0 + ln * (1.0 - sg)))
        dxh = dln * g_ref[...]
        dcv = rstd * (dxh - jnp.mean(dxh, axis=-1, keepdims=True) - xh * jnp.mean(dxh * xh, axis=-1, keepdims=True))
        dcv_ref[...] = dcv

        @pl.when(pl.program_id(0) == 0)
        def _():
            dg_ref[...] = jnp.zeros_like(dg_ref)
            dbe_ref[...] = jnp.zeros_like(dbe_ref)
            dcb_ref[...] = jnp.zeros_like(dcb_ref)

        dg_ref[...] += jnp.sum(dln * xh, axis=0, keepdims=True)
        dbe_ref[...] += jnp.sum(dln, axis=0, keepdims=True)
        dcb_ref[...] += jnp.sum(dcv, axis=0, keepdims=True)

    vec = pl.BlockSpec((1, CONV_CH), lambda i: (0, 0))
    row = pl.BlockSpec((R, CONV_CH), lambda i: (i, 0))
    vshape = jax.ShapeDtypeStruct((1, CONV_CH), F32)
    return _pcall(
        body, name="econv_bwd_ln", grid=(T // R,),
        in_specs=[row, row, vec, vec], out_specs=[row, vec, vec, vec],
        out_shape=[jax.ShapeDtypeStruct((T, CONV_CH), F32), vshape, vshape, vshape],
        compiler_params=_params("arbitrary"),
    )(cv, dcat, ln_g, ln_b)


def _econv_bwd_conv(dcv, zc, conv_k):
    R, H = CONV_TILE, CONV_HALO
    last = T // R - 1

    def body(d_ref, dn_ref, z_ref, zh_ref, k_ref, dz_ref, dk_ref):
        i = pl.program_id(0)
        z = z_ref[...]
        a_lin = z[:, :CONV_CH]
        sg = _sigmoid(z[:, CONV_CH:])
        glu = a_lin * sg
        halo = _glu(zh_ref[...]) * (i > 0).astype(F32)
        win = _sublane_shifts(jnp.concatenate([halo, glu], axis=0))
        dcv_t = d_ref[...]
        nxt = dn_ref[...] * (i < last).astype(F32)
        winb = _sublane_shifts(jnp.concatenate([dcv_t, nxt], axis=0))

        @pl.when(i == 0)
        def _():
            dk_ref[...] = jnp.zeros_like(dk_ref)

        dglu = jnp.zeros((R, CONV_CH), F32)
        for j in range(CONV_W):
            dk_ref[j:j + 1, :] += jnp.sum(dcv_t * _rows_from(win, H - (CONV_W - 1) + j, R), axis=0, keepdims=True)
            dglu = dglu + k_ref[j:j + 1, :] * _rows_from(winb, CONV_W - 1 - j, R)
        dz_ref[...] = jnp.concatenate([dglu * sg, dglu * a_lin * sg * (1.0 - sg)], axis=1).astype(BF16)

    return _pcall(
        body, name="econv_bwd_conv", grid=(T // R,),
        in_specs=[pl.BlockSpec((R, CONV_CH), lambda i: (i, 0)),
                  pl.BlockSpec((H, CONV_CH), lambda i: (jnp.minimum((i + 1) * (R // H), T // H - 1), 0)),
                  pl.BlockSpec((R, 2 * CONV_CH), lambda i: (i, 0)),
                  pl.BlockSpec((H, 2 * CONV_CH), lambda i: (jnp.maximum(i * (R // H) - 1, 0), 0)),
                  pl.BlockSpec((CONV_W, CONV_CH), lambda i: (0, 0))],
        out_specs=[pl.BlockSpec((R, 2 * CONV_CH), lambda i: (i, 0)), pl.BlockSpec((CONV_W, CONV_CH), lambda i: (0, 0))],
        out_shape=[jax.ShapeDtypeStruct((T, EVEN_IN), BF16), jax.ShapeDtypeStruct((CONV_W, CONV_CH), F32)],
        compiler_params=_params("arbitrary"),
    )(dcv, dcv, zc, zc, conv_k)


def _swap_halves(v):
    lane = lax.broadcasted_iota(jnp.int32, v.shape, 1)
    return jnp.where((lane % HEAD_DIM) < HEAD_DIM // 2, pltpu.roll(v, LANES - HEAD_DIM // 2, 1),
                     pltpu.roll(v, HEAD_DIM // 2, 1))


def _qkv_proj(hn, w_in, rope_c, rope_s, tm=T):
    tn = 4 * LANES

    def body(a_ref, b_ref, c_ref, s_ref, o_ref):
        j = pl.program_id(1)
        acc = _nn(a_ref[...], b_ref[...])
        for p in range(4):
            v = acc[:, p * LANES:(p + 1) * LANES]
            rot = v * c_ref[...] + _swap_halves(v) * s_ref[...]
            o_ref[p] = jnp.where(j < 6, rot, v)

    tab = pl.BlockSpec((tm, LANES), lambda i, j: (i, 0))
    return _pcall(
        body, name="qkv_proj", grid=(T // tm, 9),
        in_specs=[pl.BlockSpec((tm, D), lambda i, j: (i, 0)),
                  pl.BlockSpec((D, tn), lambda i, j: (0, j + (2 * CONV_CH) // tn)), tab, tab],
        out_specs=pl.BlockSpec((None, 4, tm, LANES), lambda i, j: (j, 0, i, 0)),
        out_shape=jax.ShapeDtypeStruct((9, 4, T, LANES), F32),
        compiler_params=_params("parallel", "parallel"),
    )(hn, w_in, rope_c, rope_s)


ATTN_FWD_UNROLL = 4
ATTN_BWD_UNROLL = 4


def _band_rows(start, d):
    if d == 1:
        return pl.ds(pl.multiple_of(start, BAND), BAND)
    return pl.ds(start, BAND, stride=d)


def _band_masks(n):
    row = lax.broadcasted_iota(jnp.int32, (BAND, BAND), 0)
    col = lax.broadcasted_iota(jnp.int32, (BAND, BAND), 1)
    no_prev = (n == 0).astype(jnp.int32) * (2 * BAND)
    return col <= row, col >= row + no_prev


def _attn_fwd(qkv, g):
    d = DILATIONS[g]
    nb = T // d // BAND

    def body(q_ref, k_ref, v_ref, o_ref, l_ref):
        lane_lo = lax.broadcasted_iota(jnp.int32, (BAND, LANES), 1) < HEAD_DIM

        heads = (lane_lo, jnp.logical_not(lane_lo))
        ones = jnp.ones((BAND, LANES), BF16)

        def step(it, carry):
            tiles = []
            for u in range(ATTN_FWD_UNROLL):
                idx = it * ATTN_FWD_UNROLL + u
                r = idx // nb
                n = idx % nb
                cur = _band_rows(n * (BAND * d) + r, d)
                prev = _band_rows(jnp.maximum(n - 1, 0) * (BAND * d) + r, d)
                mc, mp = _band_masks(n)
                tiles.append((cur, mc, mp, q_ref[cur, :], k_ref[cur, :].astype(BF16), v_ref[cur, :].astype(BF16),
                              k_ref[prev, :].astype(BF16), v_ref[prev, :].astype(BF16)))
            scores = []
            for cur, mc, mp, q, kc, vc, kp, vp in tiles:
                for hm in heads:
                    qm = jnp.where(hm, q, 0.0).astype(BF16)
                    scores.append((jnp.where(mc, _nt(qm, kc) * SCALE, NEG), jnp.where(mp, _nt(qm, kp) * SCALE, NEG)))
            maxes = [jnp.maximum(jnp.max(sc, axis=1, keepdims=True), jnp.max(sp, axis=1, keepdims=True))
                     for sc, sp in scores]
            probs = [(jnp.exp(sc - mx).astype(BF16), jnp.exp(sp - mx).astype(BF16))
                     for (sc, sp), mx in zip(scores, maxes)]
            dens = [_nn(pc, ones) + _nn(pp, ones) for pc, pp in probs]
            for t, (cur, mc, mp, q, kc, vc, kp, vp) in enumerate(tiles):
                outs, lses = [], []
                for h in range(2):
                    pc, pp = probs[2 * t + h]
                    outs.append((_nn(pc, vc) + _nn(pp, vp)) / dens[2 * t + h])
                    lses.append(maxes[2 * t + h] + jnp.log(dens[2 * t + h]))
                o_ref[cur, :] = jnp.where(lane_lo, outs[0], outs[1])
                l_ref[cur, :] = jnp.where(lane_lo, lses[0], lses[1])
            return carry

        lax.fori_loop(0, d * nb // ATTN_FWD_UNROLL, step, 0)

    def slab(which):
        return pl.BlockSpec((None, None, T, LANES), lambda p: (which * 3 + g, p, 0, 0))

    out = pl.BlockSpec((None, T, LANES), lambda p: (p, 0, 0))
    shape = jax.ShapeDtypeStruct((4, T, LANES), F32)
    return _pcall(
        body, name=f"attn_fwd{g}", grid=(4,),
        in_specs=[slab(0), slab(1), slab(2)], out_specs=[out, out], out_shape=[shape, shape],
        compiler_params=_params("parallel"),
    )(qkv, qkv, qkv)


def _attn_merge(outs, lses, cat, tm=1024):
    def body(o0, o1, o2, l0, l1, l2, cat_in, cat_ref, att_ref, w0, w1, w2):
        del cat_in
        la, lb, lc = l0[...], l1[...], l2[...]
        mx = jnp.maximum(jnp.maximum(la, lb), lc)
        ea, eb, ec = jnp.exp(la - mx), jnp.exp(lb - mx), jnp.exp(lc - mx)
        inv = 1.0 / (ea + eb + ec)
        wa, wb, wc = ea * inv, eb * inv, ec * inv
        att = wa * o0[...] + wb * o1[...] + wc * o2[...]
        att_ref[...] = att
        cat_ref[...] = att.astype(BF16)
        w0[...] = wa
        w1[...] = wb
        w2[...] = wc

    slab = pl.BlockSpec((None, tm, LANES), lambda p, i: (p, i, 0))
    shape = jax.ShapeDtypeStruct((4, T, LANES), F32)
    return _pcall(
        body, name="attn_merge", grid=(4, T // tm),
        in_specs=[slab] * 6 + [pl.BlockSpec(memory_space=pl.ANY)],
        out_specs=[pl.BlockSpec((tm, LANES), lambda p, i: (i, CONV_CH // LANES + p)), slab, slab, slab, slab],
        out_shape=[jax.ShapeDtypeStruct((T, D), BF16), shape, shape, shape, shape],
        input_output_aliases={6: 0},
        compiler_params=_params("parallel", "parallel"),
    )(*outs, *lses, cat)


def _attn_bwd(qkv, lse, wgt, att, dcat, dqkv, g):
    d = DILATIONS[g]
    nb = T // d // BAND

    def body(q_ref, k_ref, v_ref, l_ref, w_ref, a_ref, da_ref, dq_in, o_ref):
        del dq_in
        lane = lax.broadcasted_iota(jnp.int32, (BAND, LANES), 1)
        lane_lo = lane < HEAD_DIM
        row = lax.broadcasted_iota(jnp.int32, (LANES, LANES), 0)
        same_head = ((row // HEAD_DIM) == (lane // HEAD_DIM)).astype(BF16)
        dq_ref, dk_ref, dv_ref = o_ref.at[0], o_ref.at[1], o_ref.at[2]
        dk_ref[...] = jnp.zeros((T, LANES), F32)
        dv_ref[...] = jnp.zeros((T, LANES), F32)

        heads = (lane_lo, jnp.logical_not(lane_lo))

        def step(it, carry):
            tiles = []
            for u in range(ATTN_BWD_UNROLL):
                idx = it * ATTN_BWD_UNROLL + u
                r = idx // nb
                n = idx % nb
                cur = _band_rows(n * (BAND * d) + r, d)
                prev = _band_rows(jnp.maximum(n - 1, 0) * (BAND * d) + r, d)
                mc, mp = _band_masks(n)
                da = da_ref[cur, :]
                prod = da * a_ref[cur, :]
                hi = prod.astype(BF16)
                lo = (prod - hi.astype(F32)).astype(BF16)
                tiles.append(dict(cur=cur, prev=prev, mc=mc, mp=mp, da=da, hi=hi, lo=lo, q=q_ref[cur, :],
                                  kc=k_ref[cur, :].astype(BF16), vc=v_ref[cur, :].astype(BF16),
                                  kp=k_ref[prev, :].astype(BF16), vp=v_ref[prev, :].astype(BF16),
                                  lse=l_ref[cur, :], w=w_ref[cur, :]))
            for t in tiles:
                t["csum"] = _nn(t["hi"], same_head) + _nn(t["lo"], same_head)
            chains = []
            for t in tiles:
                for h, hm in enumerate(heads):
                    qm = jnp.where(hm, t["q"], 0.0).astype(BF16)
                    dam = jnp.where(hm, t["da"], 0.0).astype(BF16)
                    chains.append(dict(t=t, h=h, qm=qm, dam=dam,
                                       sc=jnp.where(t["mc"], _nt(qm, t["kc"]) * SCALE, NEG),
                                       sp=jnp.where(t["mp"], _nt(qm, t["kp"]) * SCALE, NEG),
                                       dpc=_nt(dam, t["vc"]), dpp=_nt(dam, t["vp"])))
            for ch in chains:
                t, col0 = ch["t"], ch["h"] * HEAD_DIM
                lse_h = t["lse"][:, col0:col0 + 1]
                w_h = t["w"][:, col0:col0 + 1]
                c_h = t["csum"][:, col0:col0 + 1]
                pwc = w_h * jnp.exp(ch["sc"] - lse_h)
                pwp = w_h * jnp.exp(ch["sp"] - lse_h)
                ch["dsc"] = (pwc * (ch["dpc"] - c_h) * SCALE).astype(BF16)
                ch["dsp"] = (pwp * (ch["dpp"] - c_h) * SCALE).astype(BF16)
                ch["pwc"] = pwc.astype(BF16)
                ch["pwp"] = pwp.astype(BF16)
            for ch in chains:
                t = ch["t"]
                ch["dq"] = _nn(ch["dsc"], t["kc"]) + _nn(ch["dsp"], t["kp"])
                ch["dkc"] = _tn(ch["dsc"], ch["qm"])
                ch["dkp"] = _tn(ch["dsp"], ch["qm"])
                ch["dvc"] = _tn(ch["pwc"], ch["dam"])
                ch["dvp"] = _tn(ch["pwp"], ch["dam"])
            for i, t in enumerate(tiles):
                c0, c1 = chains[2 * i], chains[2 * i + 1]
                dq_ref[t["cur"], :] = jnp.where(lane_lo, c0["dq"], c1["dq"])
                dk_ref[t["cur"], :] += c0["dkc"] + c1["dkc"]
                dk_ref[t["prev"], :] += c0["dkp"] + c1["dkp"]
                dv_ref[t["cur"], :] += c0["dvc"] + c1["dvc"]
                dv_ref[t["prev"], :] += c0["dvp"] + c1["dvp"]
            return carry

        lax.fori_loop(0, d * nb // ATTN_BWD_UNROLL, step, 0)

    def slab(which):
        return pl.BlockSpec((None, None, T, LANES), lambda p: (which * 3 + g, p, 0, 0))

    per_pair = pl.BlockSpec((None, T, LANES), lambda p: (p, 0, 0))
    return _pcall(
        body, name=f"attn_bwd{g}", grid=(4,),
        in_specs=[slab(0), slab(1), slab(2), per_pair, per_pair, per_pair,
                  pl.BlockSpec((T, LANES), lambda p: (0, CONV_CH // LANES + p)),
                  pl.BlockSpec(memory_space=pl.ANY)],
        out_specs=pl.BlockSpec((None, 3, None, T, LANES), lambda p: (g, 0, p, 0, 0)),
        out_shape=jax.ShapeDtypeStruct((3, 3, 4, T, LANES), F32),
        input_output_aliases={7: 0},
        compiler_params=_params("parallel"),
    )(qkv, qkv, qkv, lse, wgt, att, dcat, dqkv)


def _rope_bwd(dqkv, rope_c, rope_s, dz):
    wide = 4 * LANES

    def body(d_ref, c_ref, s_ref, dz_in, o_ref):
        del dz_in
        w = pl.program_id(1)
        for p in range(4):
            v = d_ref[p]
            rot = v * c_ref[...] + _swap_halves(v * s_ref[...])
            o_ref[:, p * LANES:(p + 1) * LANES] = jnp.where(w < 2, rot, v).astype(BF16)

    tab = pl.BlockSpec((T, LANES), lambda g, w: (0, 0))
    return _pcall(
        body, name="rope_bwd", grid=(3, 3),
        in_specs=[pl.BlockSpec((None, None, 4, T, LANES), lambda g, w: (g, w, 0, 0, 0)), tab, tab,
                  pl.BlockSpec(memory_space=pl.ANY)],
        out_specs=pl.BlockSpec((T, wide), lambda g, w: (0, (2 * CONV_CH) // wide + w * 3 + g)),
        out_shape=jax.ShapeDtypeStruct((T, EVEN_IN), BF16),
        input_output_aliases={3: 0},
        compiler_params=_params("parallel", "parallel"),
    )(dqkv, rope_c, rope_s, dz)


ODD_TILE = 256
ODD_HALO = 8
GELU_C = 0.7978845608028654
GELU_A = 0.044715


def _gelu(x):
    return 0.5 * x * (1.0 + jnp.tanh(GELU_C * (x + GELU_A * x * x * x)))


def _gelu_grad(x):
    th = jnp.tanh(GELU_C * (x + GELU_A * x * x * x))
    return 0.5 * (1.0 + th) + 0.5 * x * (1.0 - th * th) * GELU_C * (1.0 + 3.0 * GELU_A * x * x)


def _tril():
    row = lax.broadcasted_iota(jnp.int32, (CHUNK, CHUNK), 0)
    col = lax.broadcasted_iota(jnp.int32, (CHUNK, CHUNK), 1)
    return (col <= row).astype(F32)


def _odd_parts(z, zh, i, k_ref, g_ref, be_ref, w_ref, bt_ref):
    R, H = ODD_TILE, ODD_HALO
    gb, gc, xs, uv = z[:, :512], z[:, 512:1024], z[:, 1024:1536], z[:, 1536:]
    halo = zh[:, 512:1024] * zh[:, 1024:1536] * (i > 0).astype(F32)
    win = jnp.concatenate([halo, gc * xs], axis=0)
    cv = jnp.zeros((R, 512), F32)
    for j in range(SCONV_W):
        off = H - (SCONV_W - 1) + j
        cv = cv + k_ref[j:j + 1, :] * win[off:off + R, :]
    ge = _gelu(uv)
    u, v = ge[:, :512], ge[:, 512:]
    mu = jnp.mean(v, axis=-1, keepdims=True)
    xc = v - mu
    rstd = lax.rsqrt(jnp.mean(xc * xc, axis=-1, keepdims=True) + EPS)
    xh = xc * rstd
    vn = xh * g_ref[...] + be_ref[...]
    tril = _tril()
    wms = [(w_ref[g] * tril).astype(BF16) for g in range(SG_GROUPS)]
    rows = []
    for ci in range(R // CHUNK):
        blocks = []
        for g in range(SG_GROUPS):
            blk = vn[ci * CHUNK:(ci + 1) * CHUNK, g * LANES:(g + 1) * LANES].astype(BF16)
            blocks.append(_nn(wms[g], blk) + bt_ref[:, g:g + 1])
        rows.append(jnp.concatenate(blocks, axis=1))
    vmix = jnp.concatenate(rows, axis=0)
    return gb, gc, xs, uv, win, cv, u, rstd, xh, vn, vmix, wms


def _odd_mid_fwd(z, conv_k, ln_g, ln_b, sg_w, sg_bt):
    R, H = ODD_TILE, ODD_HALO

    def body(z_ref, zh_ref, k_ref, g_ref, be_ref, w_ref, bt_ref, o_ref):
        i = pl.program_id(0)
        gb, _, _, _, _, cv, u, _, _, _, vmix, _ = _odd_parts(z_ref[...], zh_ref[...], i, k_ref, g_ref, be_ref, w_ref, bt_ref)
        o_ref[...] = jnp.concatenate([gb * cv, u * vmix], axis=1).astype(BF16)

    vec = pl.BlockSpec((1, 512), lambda i: (0, 0))
    return _pcall(
        body, name="odd_mid_fwd", grid=(T // R,),
        in_specs=[pl.BlockSpec((R, ODD_IN), lambda i: (i, 0)),
                  pl.BlockSpec((H, ODD_IN), lambda i: (jnp.maximum(i * (R // H) - 1, 0), 0)),
                  pl.BlockSpec((SCONV_W, 512), lambda i: (0, 0)), vec, vec,
                  pl.BlockSpec((SG_GROUPS, CHUNK, CHUNK), lambda i: (0, 0, 0)),
                  pl.BlockSpec((CHUNK, SG_GROUPS), lambda i: (0, 0))],
        out_specs=pl.BlockSpec((R, D), lambda i: (i, 0)),
        out_shape=jax.ShapeDtypeStruct((T, D), BF16),
        compiler_params=_params("parallel"),
    )(z, z, conv_k, ln_g, ln_b, sg_w, sg_bt)


def _odd_mid_bwd(z, dcat, conv_k, ln_g, ln_b, sg_w, sg_bt):
    R, H = ODD_TILE, ODD_HALO
    last = T // R - 1

    def body(z_ref, zh_ref, zn_ref, d_ref, dn_ref, k_ref, g_ref, be_ref, w_ref, bt_ref,
             dz_ref, dk_ref, dg_ref, dbe_ref, dw_ref, dbt_ref):
        i = pl.program_id(0)
        z = z_ref[...]
        gb, gc, xs, uv, win, cv, u, rstd, xh, vn, vmix, wms = _odd_parts(z, zh_ref[...], i, k_ref, g_ref, be_ref, w_ref, bt_ref)
        dcat_t = d_ref[...]
        dc, dd = dcat_t[:, :512], dcat_t[:, 512:]

        @pl.when(i == 0)
        def _():
            dk_ref[...] = jnp.zeros_like(dk_ref)
            dg_ref[...] = jnp.zeros_like(dg_ref)
            dbe_ref[...] = jnp.zeros_like(dbe_ref)
            dw_ref[...] = jnp.zeros_like(dw_ref)
            dbt_ref[...] = jnp.zeros_like(dbt_ref)

        dgb = dc * cv
        dcv = dc * gb
        nxt = dn_ref[:, :512] * zn_ref[:, :512] * (i < last).astype(F32)
        winb = jnp.concatenate([dcv, nxt], axis=0)
        dp = jnp.zeros((R, 512), F32)
        for j in range(SCONV_W):
            off = H - (SCONV_W - 1) + j
            dk_ref[j:j + 1, :] += jnp.sum(dcv * win[off:off + R, :], axis=0, keepdims=True)
            ob = SCONV_W - 1 - j
            dp = dp + k_ref[j:j + 1, :] * winb[ob:ob + R, :]
        dgc = dp * xs
        dxs = dp * gc
        du = dd * vmix
        dvmix = dd * u
        tril = _tril()
        rows = []
        for ci in range(R // CHUNK):
            blocks = []
            for g in range(SG_GROUPS):
                sl = (slice(ci * CHUNK, (ci + 1) * CHUNK), slice(g * LANES, (g + 1) * LANES))
                dblk = dvmix[sl]
                dblk16 = dblk.astype(BF16)
                blocks.append(_tn(wms[g], dblk16))
                dw_ref[g] += _nt(dblk16, vn[sl].astype(BF16)) * tril
                dbt_ref[:, g:g + 1] += jnp.sum(dblk, axis=1, keepdims=True)
            rows.append(jnp.concatenate(blocks, axis=1))
        dvn = jnp.concatenate(rows, axis=0)
        dg_ref[...] += jnp.sum(dvn * xh, axis=0, keepdims=True)
        dbe_ref[...] += jnp.sum(dvn, axis=0, keepdims=True)
        dxh = dvn * g_ref[...]
        dv = rstd * (dxh - jnp.mean(dxh, axis=-1, keepdims=True) - xh * jnp.mean(dxh * xh, axis=-1, keepdims=True))
        duv = jnp.concatenate([du, dv], axis=1) * _gelu_grad(uv)
        dz_ref[...] = jnp.concatenate([dgb, dgc, dxs, duv], axis=1).astype(BF16)

    vec = pl.BlockSpec((1, 512), lambda i: (0, 0))
    kspec = pl.BlockSpec((SCONV_W, 512), lambda i: (0, 0))
    wspec = pl.BlockSpec((SG_GROUPS, CHUNK, CHUNK), lambda i: (0, 0, 0))
    bspec = pl.BlockSpec((CHUNK, SG_GROUPS), lambda i: (0, 0))
    nxt_blk = lambda i: (jnp.minimum((i + 1) * (R // H), T // H - 1), 0)
    return _pcall(
        body, name="odd_mid_bwd", grid=(T // R,),
        in_specs=[pl.BlockSpec((R, ODD_IN), lambda i: (i, 0)),
                  pl.BlockSpec((H, ODD_IN), lambda i: (jnp.maximum(i * (R // H) - 1, 0), 0)),
                  pl.BlockSpec((H, ODD_IN), nxt_blk),
                  pl.BlockSpec((R, D), lambda i: (i, 0)),
                  pl.BlockSpec((H, D), nxt_blk),
                  kspec, vec, vec, wspec, bspec],
        out_specs=[pl.BlockSpec((R, ODD_IN), lambda i: (i, 0)), kspec, vec, vec, wspec, bspec],
        out_shape=[jax.ShapeDtypeStruct((T, ODD_IN), BF16), jax.ShapeDtypeStruct((SCONV_W, 512), F32),
                   jax.ShapeDtypeStruct((1, 512), F32), jax.ShapeDtypeStruct((1, 512), F32),
                   jax.ShapeDtypeStruct((SG_GROUPS, CHUNK, CHUNK), F32), jax.ShapeDtypeStruct((CHUNK, SG_GROUPS), F32)],
        compiler_params=_params("arbitrary"),
    )(z, z, z, dcat, dcat, conv_k, ln_g, ln_b, sg_w, sg_bt)


def _ffn_up(tag, hn, weight):
    def act(acc):
        r = jnp.maximum(acc, 0.0)
        return (r * r,)

    return _mm(f"ffn{tag}_up", "nn", hn, weight(f"ffn_w1_{tag}", hn), T, D_FF, D, (BF16,), epi=act)


def _ffn_bwd(tag, h, g, weight, emit, saved, dout):
    hn, f = saved
    du = _mm(f"ffn{tag}_dact", "nt", dout, weight(f"ffn_w2_{tag}", dout), T, D_FF, D, (BF16,),
             epi=lambda acc, ff: (acc * (2.0 * jnp.sqrt(ff.astype(F32))),), extras=(f,))
    tok = emit(f"ffn_w2_{tag}", f, dout)
    tok = emit(f"ffn_w1_{tag}", hn, du, tie=tok)
    return _mm_dx_norm(f"ffn{tag}_dhn", du, weight(f"ffn_w1_{tag}", du), D_FF, h, g, dout, tie=tok)


def _rope_tables():
    half = HEAD_DIM // 2
    inv = 10000.0 ** (-jnp.arange(half, dtype=F32) / half)
    ang = jnp.arange(T, dtype=F32)[:, None] * inv[None, :]
    cos, sin = jnp.cos(ang), jnp.sin(ang)
    c = jnp.tile(jnp.concatenate([cos, cos], axis=1), (1, LANES // HEAD_DIM))
    s = jnp.tile(jnp.concatenate([-sin, sin], axis=1), (1, LANES // HEAD_DIM))
    return c, s


def _local_step(x, target, p, weight, emit, emit_small, before=lambda name, after: None):
    rope_c, rope_s = _rope_tables()
    grads = {}
    residual = lambda acc, res: (acc + res,)

    hn0 = _rms_fwd("mix0_norm", x, p["norm_mix_g0"])
    zc = _mm("even_in_conv", "nn", hn0, weight("even_w_in", hn0), T, 2 * CONV_CH, D, (F32,))
    qkv = _qkv_proj(hn0, weight("even_w_in", hn0), rope_c, rope_s)
    cv, cat0 = _econv_fwd(zc, p["even_conv_k"], p["even_conv_b"], p["even_ln_g"], p["even_ln_b"])
    att_parts = [_attn_fwd(qkv, 0)]
    before("attn_fwd1", att_parts[0][0])
    att_parts += [_attn_fwd(qkv, 1), _attn_fwd(qkv, 2)]
    outs = [a[0] for a in att_parts]
    lses = [a[1] for a in att_parts]
    cat0, att, w0, w1, w2 = _attn_merge(outs, lses, cat0)
    wgts = (w0, w1, w2)
    h1, hnf0 = _mm_out_norm("even_out", cat0, weight("even_w_out", cat0), D, x, p["norm_ffn_g0"])
    before("ffn0_up", hnf0)
    f0 = _ffn_up(0, hnf0, weight)
    h2, hn1 = _mm_out_norm("ffn0_down", f0, weight("ffn_w2_0", f0), D_FF, h1, p["norm_mix_g1"])

    before("odd_in", hn1)
    z1 = _mm("odd_in", "nn", hn1, weight("odd_w_in", hn1), T, ODD_IN, D, (F32,))
    cat1 = _odd_mid_fwd(z1, p["odd_conv_k"], p["odd_ln_g"], p["odd_ln_b"], p["odd_sg_w"], p["odd_sg_bt"])
    h3, hnf1 = _mm_out_norm("odd_out", cat1, weight("odd_w_out", cat1), D, h2, p["norm_ffn_g1"])
    f1 = _ffn_up(1, hnf1, weight)
    h4 = _mm("ffn1_down", "nn", f1, weight("ffn_w2_1", f1), T, D, D_FF, (F32,), epi=residual, extras=(h3,))

    dh4, grads["final_g"], loss = _loss_head(h4, p["final_g"], target)

    dh3, grads["norm_ffn_g1"] = _ffn_bwd(1, h3, p["norm_ffn_g1"], weight, emit, (hnf1, f1), dh4)
    tok = emit("odd_w_out", cat1, dh3)
    dcat1 = _mm("odd_out_dx", "nt", dh3, weight("odd_w_out", dh3), T, D, D, (F32,), tie=tok)
    dz1, grads["odd_conv_k"], grads["odd_ln_g"], grads["odd_ln_b"], grads["odd_sg_w"], grads["odd_sg_bt"] = _odd_mid_bwd(
        z1, dcat1, p["odd_conv_k"], p["odd_ln_g"], p["odd_ln_b"], p["odd_sg_w"], p["odd_sg_bt"])
    tok = emit("odd_w_in", hn1, dz1)
    dh2, grads["norm_mix_g1"] = _mm_dx_norm("odd_in_dx", dz1, weight("odd_w_in", dz1), ODD_IN, h2, p["norm_mix_g1"],
                                            dh3, tie=tok)

    dh1, grads["norm_ffn_g0"] = _ffn_bwd(0, h1, p["norm_ffn_g0"], weight, emit, (hnf0, f0), dh2)
    tok = emit("even_w_out", cat0, dh1)
    dcat0 = _mm("even_out_dx", "nt", dh1, weight("even_w_out", dh1), T, D, D, (F32,), tie=tok)
    dcv, grads["even_ln_g"], grads["even_ln_b"], grads["even_conv_b"] = _econv_bwd_ln(
        cv, dcat0, p["even_ln_g"], p["even_ln_b"])
    dz0, grads["even_conv_k"] = _econv_bwd_conv(dcv, zc, p["even_conv_k"])
    tok = emit_small(loss, grads)
    dqkv = lax.empty((3, 3, 4, T, LANES), F32)
    for g in range(3):
        dqkv = _attn_bwd(qkv, lses[g], wgts[g], att, dcat0, dqkv, g)
    before("rope_bwd", dqkv)
    dz0 = _rope_bwd(dqkv, rope_c, rope_s, dz0)
    tok = emit("even_w_in", hn0, dz0, tie=tok)
    dx, dg0 = _mm_dx_norm("even_in_dx", dz0, weight("even_w_in", dz0), EVEN_IN, x, p["norm_mix_g0"], dh1, tie=tok)
    return dx, dg0


def _rowwise(name, fn, ins, out_dtypes, tm=256):
    rows, cols = ins[0].shape
    tm = tm if rows % tm == 0 else rows
    n_in = len(ins)

    def body(*refs):
        vals = fn(*[r[...] for r in refs[:n_in]])
        for o_ref, v in zip(refs[n_in:], vals):
            o_ref[...] = v.astype(o_ref.dtype)

    spec = pl.BlockSpec((tm, cols), lambda i: (i, 0))
    outs = _pcall(
        body, name=name, grid=(rows // tm,),
        in_specs=[spec] * n_in, out_specs=[spec] * len(out_dtypes),
        out_shape=[jax.ShapeDtypeStruct((rows, cols), dt) for dt in out_dtypes],
        compiler_params=_params("parallel"),
    )(*ins)
    return outs[0] if len(out_dtypes) == 1 else outs


def _adamw(name, w, g, m, v, with_grad=False):
    c1 = 1.0 - ADAM_B1 ** ADAM_STEP
    c2 = 1.0 - ADAM_B2 ** ADAM_STEP

    def fn(w_t, g_t, m_t, v_t):
        m_new = ADAM_B1 * m_t + (1.0 - ADAM_B1) * g_t
        v_new = ADAM_B2 * v_t + (1.0 - ADAM_B2) * (g_t * g_t)
        delta = -ADAM_LR * ((m_new / c1) / (jnp.sqrt(v_new / c2) + ADAM_EPS) + ADAM_WD * w_t)
        return (delta, m_new, v_new, g_t) if with_grad else (delta, m_new, v_new)

    return _rowwise(name, fn, (w, g, m, v), (F32,) * (4 if with_grad else 3))


class _Piece:
    def __init__(self, name, rows, cols, axis, src, src_row0):
        self.name, self.rows, self.cols, self.axis = name, rows, cols, axis
        self.width = (cols if axis == 1 else rows) // 4
        self.src, self.src_row0 = src, src_row0

    @property
    def full_shape(self):
        return (self.rows, self.cols)

    @property
    def half_shape(self):
        return (self.rows // 2, self.cols) if self.axis == 1 else (self.rows, self.cols // 2)

    @property
    def shard_half_shape(self):
        return (self.rows // 2, self.width) if self.axis == 1 else (self.width, self.cols // 2)

    def shard_whole(self, ref):
        n = self.rows if self.axis == 1 else self.width
        return ref.at[pl.ds(self.src_row0, n), :]

    def shard_half(self, ref, h):
        if self.axis == 1:
            return ref.at[pl.ds(self.src_row0 + h * (self.rows // 2), self.rows // 2), :]
        return ref.at[pl.ds(self.src_row0, self.width), pl.ds(h * (self.cols // 2), self.cols // 2)]

    def full_shard(self, ref, s):
        if self.axis == 1:
            return ref.at[:, pl.ds(s * self.width, self.width)]
        return ref.at[pl.ds(s * self.width, self.width), :]

    def full_shard_half(self, ref, s, h):
        if self.axis == 1:
            return ref.at[pl.ds(h * (self.rows // 2), self.rows // 2), pl.ds(s * self.width, self.width)]
        return ref.at[pl.ds(s * self.width, self.width), pl.ds(h * (self.cols // 2), self.cols // 2)]

    def full_half(self, ref, h):
        if self.axis == 1:
            return ref.at[pl.ds(h * (self.rows // 2), self.rows // 2), :]
        return ref.at[:, pl.ds(h * (self.cols // 2), self.cols // 2)]

    def full_half_rows(self, ref, h, r0, n):
        if self.axis == 1:
            return ref.at[pl.ds(h * (self.rows // 2) + r0, n), :]
        return ref.at[pl.ds(r0, n), pl.ds(h * (self.cols // 2), self.cols // 2)]

    def half_shard(self, ref, s):
        return self.full_shard(ref, s)


PIECES = (
    _Piece("even_w_in", D, EVEN_IN, 1, 0, 0),
    _Piece("even_w_out", D, D, 0, 1, 0),
    _Piece("ffn_w1_0", D, D_FF, 1, 4, 0),
    _Piece("ffn_w2_0", D_FF, D, 0, 5, 0),
    _Piece("odd_w_in", D, ODD_IN, 1, 2, 0),
    _Piece("odd_w_out", D, D, 0, 3, 0),
    _Piece("ffn_w1_1", D, D_FF, 1, 4, D),
    _Piece("ffn_w2_1", D_FF, D, 0, 5, D_FF // 4),
)
N_PIECES = len(PIECES)
FORWARD_RIDES = {"attn_fwd1": (1, 2, 3), "ffn0_up": (4, 5, 6), "odd_in": (7,)}
JOIN_GROUPS = ((0, 1, 2, 3), (4, 5))
JOIN_RIDES_IN = "rope_bwd"
HOLD_BACK = ("ffn_w2_0", "ffn_w2_1", "odd_w_out")
N_SHARD_OPERANDS = 6
ANY = pl.BlockSpec(memory_space=pl.ANY)
MESH = pl.DeviceIdType.MESH


def _mesh_place():
    x, y, c = lax.axis_index("x"), lax.axis_index("y"), lax.axis_index("c")
    chips = [(1 - x, y), (x, 1 - y), (1 - x, 1 - y)]
    return x, y, c, chips


def _remote(src, dst, send_sem, recv_sem, dev):
    return pltpu.make_async_remote_copy(src_ref=src, dst_ref=dst, send_sem=send_sem, recv_sem=recv_sem,
                                        device_id=dev, device_id_type=MESH)


HBM = pl.BlockSpec(memory_space=pltpu.HBM)
SEM = pl.BlockSpec(memory_space=pltpu.SEMAPHORE)
SPLIT_PARAMS = pltpu.CompilerParams(has_side_effects=pltpu.SideEffectType.DATAFLOW_SIDE_EFFECTING)
CAST_TILE = 256


def _in_hbm(a):
    return pltpu.with_memory_space_constraint(a, pltpu.HBM)


def _cast_place(pc, shard_operand, chip, tie=None):
    rows, cols = (pc.rows, pc.width) if pc.axis == 1 else (pc.width, pc.cols)
    nblk = rows // CAST_TILE
    blk0 = pc.src_row0 // CAST_TILE
    ties = () if tie is None else (tie,)

    def body(chip_ref, x_ref, *rest):
        del chip_ref
        rest[-1][...] = x_ref[...].astype(BF16)

    if pc.axis == 1:
        out_map = lambda i, chip_ref: (i, chip_ref[0])
    else:
        out_map = lambda i, chip_ref: (chip_ref[0] * nblk + i, 0)
    return _pcall(
        body, name=f"cast_{pc.name}",
        grid_spec=pltpu.PrefetchScalarGridSpec(
            num_scalar_prefetch=1, grid=(nblk,),
            in_specs=[pl.BlockSpec((CAST_TILE, cols), lambda i, chip_ref: (blk0 + i, 0))]
            + [pl.BlockSpec(TOKEN_SHAPE, lambda i, chip_ref: (0, 0))] * len(ties),
            out_specs=pl.BlockSpec((CAST_TILE, cols), out_map)),
        out_shape=jax.ShapeDtypeStruct(pc.full_shape, BF16),
        compiler_params=_params("parallel"),
    )(chip, shard_operand, *ties)


def _gather_start(name, pieces, fulls):
    n = len(pieces)

    def body(*refs):
        ins = refs[:n]
        sends = refs[2 * n:3 * n]
        recvs = refs[3 * n:4 * n]
        token = refs[4 * n]
        x, y, c, chips = _mesh_place()
        s = 2 * x + y
        for i, pc in enumerate(pieces):
            win = pc.full_shard_half(ins[i], s, c)
            for k, (cx, cy) in enumerate(chips):
                _remote(win, win, sends[i].at[k], recvs[i].at[k], (cx, cy, c)).start()
        token[...] = jnp.zeros(TOKEN_SHAPE, F32)

    sems = [pltpu.SemaphoreType.DMA((3,))] * (2 * n)
    outs = _pcall(
        body, name=name,
        in_specs=[HBM] * n,
        out_specs=[HBM] * n + [SEM] * (2 * n) + [pl.BlockSpec(memory_space=pltpu.VMEM)],
        out_shape=[pltpu.HBM(pc.full_shape, BF16) for pc in pieces] + sems + [jax.ShapeDtypeStruct(TOKEN_SHAPE, F32)],
        input_output_aliases={i: i for i in range(n)},
        compiler_params=SPLIT_PARAMS,
    )(*[_in_hbm(f) for f in fulls])
    return outs[:n], outs[n:2 * n], outs[2 * n:3 * n], outs[3 * n]


def _gather_wait(pc, full, send_sems, recv_sems, after):
    def body(full_ref, send_ref, recv_ref, after_ref, out_ref):
        del after_ref, out_ref
        x, y, c, chips = _mesh_place()
        for k, (cx, cy) in enumerate(chips):
            win = pc.full_shard_half(full_ref, 2 * cx + cy, c)
            cp = _remote(win, win, send_ref.at[k], recv_ref.at[k], (cx, cy, c))
            cp.wait_send()
            cp.wait_recv()

    return _pcall(
        body, name=f"gather_wait_{pc.name}",
        in_specs=[HBM, SEM, SEM, ANY], out_specs=HBM, out_shape=pltpu.HBM(pc.full_shape, BF16),
        input_output_aliases={0: 0}, compiler_params=SPLIT_PARAMS,
    )(full, send_sems, recv_sems, after)


def _core_forward_job(pieces, fulls):
    n = len(pieces)

    def copies(ins, outs, scr):
        send_bufs, recv_bufs = scr[:n], scr[n:2 * n]
        load_sems, send_sems, recv_sems, store_sems = scr[2 * n:]
        x, y, c, chips = _mesh_place()
        loads, sends, stores = [], [], []
        for i, pc in enumerate(pieces):
            for k, (cx, cy) in enumerate(chips):
                j = 3 * i + k
                loads.append(pltpu.make_async_copy(pc.full_shard_half(ins[i], 2 * cx + cy, c), send_bufs[i].at[k],
                                                   load_sems.at[j]))
                sends.append(_remote(send_bufs[i].at[k], recv_bufs[i].at[k], send_sems.at[j], recv_sems.at[j],
                                     (x, y, 1 - c)))
                stores.append(pltpu.make_async_copy(recv_bufs[i].at[k], pc.full_shard_half(outs[i], 2 * cx + cy, 1 - c),
                                                    store_sems.at[j]))
        return loads, sends, stores

    def begin(ins, outs, scr):
        loads, sends, _ = copies(ins, outs, scr)
        for cp in loads:
            cp.start()
        for load, send in zip(loads, sends):
            load.wait()
            send.start()

    def finish(ins, outs, scr):
        _, sends, stores = copies(ins, outs, scr)
        for send, store in zip(sends, stores):
            send.wait_recv()
            store.start()
        for send, store in zip(sends, stores):
            send.wait_send()
            store.wait()

    sems = pltpu.SemaphoreType.DMA((3 * n,))
    bufs = [pltpu.VMEM((3,) + pc.shard_half_shape, BF16) for pc in pieces]
    return _SideJob(fulls, [jax.ShapeDtypeStruct(pc.full_shape, BF16) for pc in pieces],
                    bufs + bufs + [sems, sems, sems, sems], {i: i for i in range(n)}, begin, finish)


def _core_forward(pieces, fulls):
    n = len(pieces)

    def body(*refs):
        ins, outs = refs[:n], refs[n:2 * n]
        send_bufs, recv_bufs = refs[2 * n:3 * n], refs[3 * n:4 * n]
        load_sems, send_sems, recv_sems, store_sems = refs[4 * n:]
        x, y, c, chips = _mesh_place()
        loads, sends, stores = [], [], []
        for i, pc in enumerate(pieces):
            for k, (cx, cy) in enumerate(chips):
                cp = pltpu.make_async_copy(pc.full_shard_half(ins[i], 2 * cx + cy, c), send_bufs[i].at[k],
                                           load_sems.at[3 * i + k])
                cp.start()
                loads.append(cp)
        for i in range(n):
            for k in range(3):
                j = 3 * i + k
                loads[j].wait()
                cp = _remote(send_bufs[i].at[k], recv_bufs[i].at[k], send_sems.at[j], recv_sems.at[j], (x, y, 1 - c))
                cp.start()
                sends.append(cp)
        for i, pc in enumerate(pieces):
            for k, (cx, cy) in enumerate(chips):
                j = 3 * i + k
                sends[j].wait_recv()
                cp = pltpu.make_async_copy(recv_bufs[i].at[k], pc.full_shard_half(outs[i], 2 * cx + cy, 1 - c),
                                           store_sems.at[j])
                cp.start()
                stores.append(cp)
        for j in range(3 * n):
            sends[j].wait_send()
            stores[j].wait()

    sems = pltpu.SemaphoreType.DMA((3 * n,))
    bufs = [pltpu.VMEM((3,) + pc.shard_half_shape, BF16) for pc in pieces]
    return _pcall(
        body, name="core_forward_" + pieces[0].name, in_specs=[ANY] * n, out_specs=[ANY] * n,
        out_shape=[jax.ShapeDtypeStruct(pc.full_shape, BF16) for pc in pieces],
        scratch_shapes=bufs + bufs + [sems, sems, sems, sems],
        input_output_aliases={i: i for i in range(n)},
        compiler_params=pltpu.CompilerParams(vmem_limit_bytes=VMEM_LIMIT),
    )(*fulls)


def _dw_tile(pc):
    if pc.axis == 1:
        tn = max(t for t in range(LANES, pc.cols + 1, LANES) if pc.cols % t == 0 and t <= 1408)
        return pc.rows // 2, tn
    return min(pc.rows, 1024), pc.cols // 2


def _mm_dw_chipsum(pc, a, b, core, tie=None):
    tm, tn = _dw_tile(pc)
    hr, hc = pc.half_shape
    tiles_r, tiles_c = hr // tm, hc // tn
    th = tiles_r * tiles_c
    ties = () if tie is None else (tie,)

    def tile_of(s, core_ref):
        mine = s >= th
        half = jnp.where(mine, core_ref[0], 1 - core_ref[0])
        local = s % th
        li, lj = local // tiles_c, local % tiles_c
        if pc.axis == 1:
            return half * tiles_r + li, lj, li, lj, mine
        return li, half * tiles_c + lj, li, lj, mine

    def body(core_ref, a_ref, b_ref, *rest):
        o_ref, send_buf, recv_buf, send_sems, recv_sems = rest[len(ties):]
        s = pl.program_id(0)
        local = s % th
        x, y, c = lax.axis_index("x"), lax.axis_index("y"), lax.axis_index("c")
        acc = _tn(a_ref[...].astype(BF16), b_ref[...].astype(BF16))

        def push(slot):
            return _remote(send_buf.at[slot], recv_buf.at[slot], send_sems.at[slot], recv_sems.at[slot], (x, y, 1 - c))

        @pl.when(s < th)
        def _():
            send_buf[local] = acc.astype(BF16)
            push(local).start()

        @pl.when(s >= th)
        def _():
            push(local).wait_recv()
            o_ref[...] = (acc + recv_buf[local].astype(F32)).astype(BF16)

        @pl.when(s == 2 * th - 1)
        def _():
            for slot in range(th):
                push(slot).wait_send()

    def a_map(s, core_ref):
        return 0, tile_of(s, core_ref)[0]

    def b_map(s, core_ref):
        return 0, tile_of(s, core_ref)[1]

    def o_map(s, core_ref):
        _, _, li, lj, mine = tile_of(s, core_ref)
        return jnp.where(mine, li, 0), jnp.where(mine, lj, 0)

    return _pcall(
        body, name=f"dw_{pc.name}",
        grid_spec=pltpu.PrefetchScalarGridSpec(
            num_scalar_prefetch=1, grid=(2 * th,),
            in_specs=[pl.BlockSpec((T, tm), a_map), pl.BlockSpec((T, tn), b_map)]
            + [pl.BlockSpec(TOKEN_SHAPE, lambda s, core_ref: (0, 0))] * len(ties),
            out_specs=pl.BlockSpec((tm, tn), o_map),
            scratch_shapes=[pltpu.VMEM((th, tm, tn), BF16), pltpu.VMEM((th, tm, tn), BF16),
                            pltpu.SemaphoreType.DMA((th,)), pltpu.SemaphoreType.DMA((th,))]),
        out_shape=jax.ShapeDtypeStruct((hr, hc), BF16),
        compiler_params=_params("arbitrary"),
    )(core, a, b, *ties)


def _scatter_start(pieces, chip_sums):
    n = len(pieces)

    def body(*refs):
        sums, lands = refs[:n], refs[n:2 * n]
        sends, recvs = refs[4 * n:5 * n], refs[5 * n:6 * n]
        token = refs[6 * n]
        x, y, c, chips = _mesh_place()
        for i, pc in enumerate(pieces):
            for k, (cx, cy) in enumerate(chips):
                _remote(pc.half_shard(sums[i], 2 * cx + cy), lands[i].at[k], sends[i].at[k], recvs[i].at[k],
                        (cx, cy, c)).start()
        token[...] = jnp.zeros(TOKEN_SHAPE, F32)

    land_shapes = [(3,) + pc.shard_half_shape for pc in pieces]
    sems = [pltpu.SemaphoreType.DMA((3,))] * (2 * n)
    outs = _pcall(
        body, name="scatter_start_" + pieces[0].name,
        in_specs=[HBM] * (2 * n), out_specs=[HBM] * (2 * n) + [SEM] * (2 * n) + [pl.BlockSpec(memory_space=pltpu.VMEM)],
        out_shape=[pltpu.HBM(pc.half_shape, BF16) for pc in pieces] + [pltpu.HBM(sh, BF16) for sh in land_shapes]
        + sems + [jax.ShapeDtypeStruct(TOKEN_SHAPE, F32)],
        input_output_aliases={i: i for i in range(2 * n)}, compiler_params=SPLIT_PARAMS,
    )(*[_in_hbm(cs) for cs in chip_sums], *[_in_hbm(lax.empty(sh, BF16)) for sh in land_shapes])
    return [(outs[i], outs[n + i], outs[2 * n + i], outs[3 * n + i]) for i in range(n)], outs[4 * n]


def _scatter_wait(pc, chip_sum, land, send_sems, recv_sems, after):
    def body(sum_ref, land_ref, send_ref, recv_ref, after_ref, sum_out, land_out):
        del after_ref, sum_out, land_out
        x, y, c, chips = _mesh_place()
        for k, (cx, cy) in enumerate(chips):
            cp = _remote(pc.half_shard(sum_ref, 2 * cx + cy), land_ref.at[k], send_ref.at[k], recv_ref.at[k], (cx, cy, c))
            cp.wait_send()
            cp.wait_recv()

    return _pcall(
        body, name=f"scatter_wait_{pc.name}",
        in_specs=[HBM, HBM, SEM, SEM, ANY], out_specs=[HBM, HBM],
        out_shape=[pltpu.HBM(pc.half_shape, BF16), pltpu.HBM((3,) + pc.shard_half_shape, BF16)],
        input_output_aliases={0: 0, 1: 1}, compiler_params=SPLIT_PARAMS,
    )(chip_sum, land, send_sems, recv_sems, after)


SHARD_OPERAND_SHAPES = ((D, EVEN_IN // 4), (D // 4, D), (D, ODD_IN // 4), (D // 4, D), (2 * D, D_FF // 4), (2 * D_FF // 4, D))


def _allsum_join_job(operands, chip_sums, lands):
    pieces = [pc for pc in PIECES if pc.src in operands]
    n = len(pieces)

    def copies(ins, outs, scr):
        sum_refs, land_refs = ins[:n], ins[n:]
        out_refs = dict(zip(operands, outs))
        in_bufs, fin_bufs, recv_bufs = scr[:n], scr[n:2 * n], scr[2 * n:3 * n]
        load_sems, send_sems, recv_sems, out_sems = scr[3 * n:]
        x, y, c, _ = _mesh_place()
        s = 2 * x + y
        loads, sends, mine, theirs = [], [], [], []
        for j, pc in enumerate(pieces):
            loads.append((pltpu.make_async_copy(land_refs[j], in_bufs[j].at[pl.ds(0, 3)], load_sems.at[2 * j]),
                          pltpu.make_async_copy(pc.half_shard(sum_refs[j], s), in_bufs[j].at[3], load_sems.at[2 * j + 1])))
            sends.append(_remote(fin_bufs[j], recv_bufs[j], send_sems.at[j], recv_sems.at[j], (x, y, 1 - c)))
            mine.append(pltpu.make_async_copy(fin_bufs[j], pc.shard_half(out_refs[pc.src], c), out_sems.at[2 * j]))
            theirs.append(pltpu.make_async_copy(recv_bufs[j], pc.shard_half(out_refs[pc.src], 1 - c), out_sems.at[2 * j + 1]))
        return loads, sends, mine, theirs, in_bufs, fin_bufs

    def begin(ins, outs, scr):
        loads, sends, mine, _, in_bufs, fin_bufs = copies(ins, outs, scr)
        for a, b in loads:
            a.start()
            b.start()
        for j in range(n):
            loads[j][0].wait()
            loads[j][1].wait()
            acc = in_bufs[j][0].astype(F32)
            for k in range(1, 4):
                acc = acc + in_bufs[j][k].astype(F32)
            fin_bufs[j][...] = acc
            mine[j].start()
            sends[j].start()

    def finish(ins, outs, scr):
        _, sends, mine, theirs, _, _ = copies(ins, outs, scr)
        for j in range(n):
            sends[j].wait_recv()
            theirs[j].start()
        for j in range(n):
            sends[j].wait_send()
            mine[j].wait()
            theirs[j].wait()

    halves = [pc.shard_half_shape for pc in pieces]
    scratch = ([pltpu.VMEM((4,) + sh, BF16) for sh in halves] + [pltpu.VMEM(sh, F32) for sh in halves] * 2
               + [pltpu.SemaphoreType.DMA((2 * n,)), pltpu.SemaphoreType.DMA((n,)), pltpu.SemaphoreType.DMA((n,)),
                  pltpu.SemaphoreType.DMA((2 * n,))])
    return _SideJob(list(chip_sums) + list(lands), [jax.ShapeDtypeStruct(SHARD_OPERAND_SHAPES[o], F32) for o in operands],
                    scratch, {}, begin, finish)


PEER_FLIPS = tuple((a, b, e) for a in (0, 1) for b in (0, 1) for e in (0, 1) if (a, b, e) != (0, 0, 0))


def _peers():
    x, y, c = lax.axis_index("x"), lax.axis_index("y"), lax.axis_index("c")
    me = 4 * x + 2 * y + c
    out = []
    for a, b, e in PEER_FLIPS:
        px, py, pc = (1 - x if a else x), (1 - y if b else y), (1 - c if e else c)
        out.append(((px, py, pc), 4 * px + 2 * py + pc))
    return me, out


def _exchange8_start(name, blk):
    m = blk.shape[0]

    def body(blk_ref, land_ref, blk_out, land_out, sends, recvs, token):
        del blk_out, land_out
        me, peers = _peers()
        for k, (dev, _) in enumerate(peers):
            _remote(blk_ref, land_ref.at[me], sends.at[k], recvs.at[k], dev).start()
        token[...] = jnp.zeros(TOKEN_SHAPE, F32)

    sems = pltpu.SemaphoreType.DMA((7,))
    return _pcall(
        body, name=name,
        in_specs=[HBM, HBM], out_specs=[HBM, HBM, SEM, SEM, pl.BlockSpec(memory_space=pltpu.VMEM)],
        out_shape=[pltpu.HBM((m, LANES), F32), pltpu.HBM((8, m, LANES), F32), sems, sems,
                   jax.ShapeDtypeStruct(TOKEN_SHAPE, F32)],
        input_output_aliases={0: 0, 1: 1}, compiler_params=SPLIT_PARAMS,
    )(_in_hbm(blk), _in_hbm(lax.empty((8, m, LANES), F32)))


def _exchange8_wait(name, blk, land, send_sems, recv_sems, after):
    def body(blk_ref, land_ref, send_ref, recv_ref, after_ref, blk_out, land_out):
        del after_ref, blk_out, land_out
        _, peers = _peers()
        for k, (dev, slot) in enumerate(peers):
            cp = _remote(blk_ref, land_ref.at[slot], send_ref.at[k], recv_ref.at[k], dev)
            cp.wait_send()
            cp.wait_recv()

    m = blk.shape[0]
    return _pcall(
        body, name=name,
        in_specs=[HBM, HBM, SEM, SEM, ANY], out_specs=[HBM, HBM],
        out_shape=[pltpu.HBM((m, LANES), F32), pltpu.HBM((8, m, LANES), F32)],
        input_output_aliases={0: 0, 1: 1}, compiler_params=SPLIT_PARAMS,
    )(blk, land, send_sems, recv_sems, after)


def _collect8(name, blk, land, with_sum):
    m = blk.shape[0]

    def body(blk_ref, land_ref, out_ref, *scratch):
        sems = scratch[-1]
        dst = scratch[0] if with_sum else out_ref
        me, peers = _peers()
        copies = [pltpu.make_async_copy(blk_ref, dst.at[me], sems.at[7])]
        for k, (_, slot) in enumerate(peers):
            copies.append(pltpu.make_async_copy(land_ref.at[slot], dst.at[slot], sems.at[k]))
        for cp in copies:
            cp.start()
        for cp in copies:
            cp.wait()
        if with_sum:
            acc = dst[0]
            for dev in range(1, 8):
                acc = acc + dst[dev]
            out_ref[...] = acc

    all_shape = (8, m, LANES)
    return _pcall(
        body, name=name, in_specs=[ANY, ANY], out_specs=pl.BlockSpec(memory_space=pltpu.VMEM),
        out_shape=jax.ShapeDtypeStruct((m, LANES) if with_sum else all_shape, F32),
        scratch_shapes=([pltpu.VMEM(all_shape, F32)] if with_sum else []) + [pltpu.SemaphoreType.DMA((8,))],
    )(blk, land)


def _pack(arrays, row_counts):
    rows = []
    for a, n in zip(arrays, row_counts):
        flat = a.reshape(-1, LANES)
        rows.append(jnp.pad(flat, ((0, n - flat.shape[0]), (0, 0))))
    return jnp.concatenate(rows, axis=0)


def _unpack(buf, shapes, row_counts):
    out, r0 = [], 0
    for sh, n in zip(shapes, row_counts):
        size = 1
        for dim in sh:
            size *= dim
        out.append(buf[r0:r0 + size // LANES].reshape(sh))
        r0 += n
    return out


REPL_NAMES = ("norm_mix_g", "norm_ffn_g", "even_conv_b", "even_ln_g", "even_ln_b", "odd_sg_w", "odd_sg_b", "final_g")
REPL_SHAPES = ((2, D), (2, D), (1, 512), (1, 512), (1, 512), (1, SG_GROUPS, CHUNK, CHUNK), (1, SG_GROUPS, CHUNK), (D,))
REPL_ROWS = (16, 16, 8, 8, 8, 512, 8, 8)
SHARDED_NAMES = ("even_conv_k", "odd_conv_k", "odd_ln_g", "odd_ln_b")
SHARDED_SHARD_SHAPES = ((1, CONV_W, LANES), (1, SCONV_W, LANES), (1, LANES), (1, LANES))
SHARDED_SHARD_ROWS = (32, 8, 8, 8)
SHARDED_FULL_SHAPES = ((CONV_W, 512), (SCONV_W, 512), (1, 512), (1, 512))
SHARDED_FULL_ROWS = (128, 16, 8, 8)


def kernel(x, norm_mix_g, norm_ffn_g, even_w_in, even_conv_k, even_conv_b, even_ln_g, even_ln_b, even_w_out, odd_w_in, odd_conv_k, odd_ln_g, odd_ln_b, odd_sg_w, odd_sg_b, odd_w_out, ffn_w1, ffn_w2, final_g, loss_target, m_norm_mix_g, m_norm_ffn_g, m_even_w_in, m_even_conv_k, m_even_conv_b, m_even_ln_g, m_even_ln_b, m_even_w_out, m_odd_w_in, m_odd_conv_k, m_odd_ln_g, m_odd_ln_b, m_odd_sg_w, m_odd_sg_b, m_odd_w_out, m_ffn_w1, m_ffn_w2, m_final_g, v_norm_mix_g, v_norm_ffn_g, v_even_w_in, v_even_conv_k, v_even_conv_b, v_even_ln_g, v_even_ln_b, v_even_w_out, v_odd_w_in, v_odd_conv_k, v_odd_ln_g, v_odd_ln_b, v_odd_sg_w, v_odd_sg_b, v_odd_w_out, v_ffn_w1, v_ffn_w2, v_final_g):
    names = ("norm_mix_g", "norm_ffn_g", "even_w_in", "even_conv_k", "even_conv_b", "even_ln_g", "even_ln_b", "even_w_out",
             "odd_w_in", "odd_conv_k", "odd_ln_g", "odd_ln_b", "odd_sg_w", "odd_sg_b", "odd_w_out", "ffn_w1", "ffn_w2", "final_g")
    w = dict(zip(names, (norm_mix_g, norm_ffn_g, even_w_in, even_conv_k, even_conv_b, even_ln_g, even_ln_b, even_w_out,
                         odd_w_in, odd_conv_k, odd_ln_g, odd_ln_b, odd_sg_w, odd_sg_b, odd_w_out, ffn_w1, ffn_w2, final_g)))
    mom = dict(zip(names, (m_norm_mix_g, m_norm_ffn_g, m_even_w_in, m_even_conv_k, m_even_conv_b, m_even_ln_g, m_even_ln_b,
                           m_even_w_out, m_odd_w_in, m_odd_conv_k, m_odd_ln_g, m_odd_ln_b, m_odd_sg_w, m_odd_sg_b, m_odd_w_out,
                           m_ffn_w1, m_ffn_w2, m_final_g)))
    vel = dict(zip(names, (v_norm_mix_g, v_norm_ffn_g, v_even_w_in, v_even_conv_k, v_even_conv_b, v_even_ln_g, v_even_ln_b,
                           v_even_w_out, v_odd_w_in, v_odd_conv_k, v_odd_ln_g, v_odd_ln_b, v_odd_sg_w, v_odd_sg_b, v_odd_w_out,
                           v_ffn_w1, v_ffn_w2, v_final_g)))
    big_names = ("even_w_in", "even_w_out", "odd_w_in", "odd_w_out", "ffn_w1", "ffn_w2")
    chip = 2 * lax.axis_index("x") + lax.axis_index("y")

    def shard2d(t, name):
        return t[name].reshape(SHARD_OPERAND_SHAPES[big_names.index(name)])

    chip_op = jnp.reshape(chip, (1,)).astype(jnp.int32)
    small_pack = _pack([w[n] for n in SHARDED_NAMES], SHARDED_SHARD_ROWS)
    small_blk, small_land, small_send, small_recv, small_token = _exchange8_start("gather_small_start", small_pack)
    first = _cast_place(PIECES[0], shard2d(w, big_names[PIECES[0].src]), chip_op, tie=small_token)
    fly0, send0, recv0, token = _gather_start("gather_start_first", PIECES[:1], [first])
    placed = [_cast_place(pc, shard2d(w, big_names[pc.src]), chip_op, tie=token) for pc in PIECES[1:]]
    fly1, send1, recv1, all_started = _gather_start("gather_start_rest", PIECES[1:], placed)
    flying, gather_send, gather_recv = fly0 + fly1, send0 + send1, recv0 + recv1
    ready = {}

    names_in_order = [pc.name for pc in PIECES]

    riding = {}

    def weight(name, after):
        if name in riding:
            job, k = riding.pop(name)
            ready[name] = job.results[k]
        if name not in ready:
            landed = _gather_wait(PIECES[0], flying[0], gather_send[0], gather_recv[0], all_started)
            ready[name], = _core_forward(PIECES[:1], [landed])
        return ready[name]

    def before(call, after):
        if call in FORWARD_RIDES:
            group = FORWARD_RIDES[call]
            landed = [_gather_wait(PIECES[j], flying[j], gather_send[j], gather_recv[j], after) for j in group]
            job = _core_forward_job([PIECES[j] for j in group], landed)
            riding.update((PIECES[j].name, (job, k)) for k, j in enumerate(group))
            _ride_next_call(job)
        elif call == JOIN_RIDES_IN:
            early_join.append(join_job(JOIN_GROUPS[1], after))
            _ride_next_call(early_join[0])

    early_join = []

    def join_job(operands, after):
        pieces = [pc for pc in PIECES if pc.src in operands]
        done = {}
        for entry in list(scattering):
            if entry[0] in pieces:
                done[entry[0].name] = _scatter_wait(*entry, after)
                scattering.remove(entry)
        return _allsum_join_job(operands, [done[pc.name][0] for pc in pieces], [done[pc.name][1] for pc in pieces])

    scattering = []
    held = []

    core_op = jnp.reshape(lax.axis_index("c"), (1,)).astype(jnp.int32)

    def emit(name, a, b, tie=None):
        pc = PIECES[names_in_order.index(name)]
        held.append((pc, _mm_dw_chipsum(pc, a, b, core_op, tie)))
        if name in HOLD_BACK:
            return None
        pieces = [pc for pc, _ in held]
        started, token = _scatter_start(pieces, [chip_sum for _, chip_sum in held])
        scattering.extend((pc,) + tuple(st) for pc, st in zip(pieces, started))
        held.clear()
        return token

    full = {}
    small_blk, small_land = _exchange8_wait("gather_small_wait", small_blk, small_land, small_send, small_recv, all_started)
    gathered = _collect8("gather_small_collect", small_blk, small_land, False)
    gathered = gathered.reshape(4, 2, sum(SHARDED_SHARD_ROWS), LANES)[:, 0]
    r0 = 0
    for n, sh, rows, full_sh in zip(SHARDED_NAMES, SHARDED_SHARD_SHAPES, SHARDED_SHARD_ROWS, SHARDED_FULL_SHAPES):
        per_chip = gathered[:, r0:r0 + rows].reshape(4, -1)[:, :full_sh[0] * LANES].reshape(4, full_sh[0], LANES)
        full[n] = jnp.transpose(per_chip, (1, 0, 2)).reshape(full_sh)
        r0 += rows
    p = dict(full)
    p.update(norm_mix_g0=norm_mix_g[0:1], norm_mix_g1=norm_mix_g[1:2], norm_ffn_g0=norm_ffn_g[0:1], norm_ffn_g1=norm_ffn_g[1:2],
             even_conv_b=even_conv_b, even_ln_g=even_ln_g, even_ln_b=even_ln_b,
             odd_sg_w=odd_sg_w[0], odd_sg_bt=odd_sg_b[0].T, final_g=final_g[None, :])

    small = {}

    def emit_small(loss_row, g):
        parts = [loss_row, g["norm_mix_g1"], g["norm_ffn_g0"], g["norm_ffn_g1"], g["even_conv_b"], g["even_ln_g"],
                 g["even_ln_b"], g["odd_sg_w"], g["odd_sg_bt"].T, g["final_g"],
                 g["even_conv_k"], g["odd_conv_k"], g["odd_ln_g"], g["odd_ln_b"]]
        pack = _pack(parts, (8, 8, 8, 8) + REPL_ROWS[2:] + SHARDED_FULL_ROWS)
        small["blk"], small["land"], small["send"], small["recv"], token = _exchange8_start("allreduce_small_start", pack)
        return token

    dx, dg0 = _local_step(x[0], loss_target[0], p, weight, emit, emit_small, before)
    last_blk, last_land, last_send, last_recv, grad_token = _exchange8_start("allreduce_last_start", _pack([dg0], (8,)))

    big_grads = dict(zip((big_names[o] for o in JOIN_GROUPS[1]), early_join[0].results))
    late_join = join_job(JOIN_GROUPS[0], grad_token)
    delta, new_m, new_v, grads_big = {}, {}, {}, {}

    def adamw_big(n):
        d2, m2, v2, g2 = _adamw(f"adamw_{n}", shard2d(w, n), big_grads[n], shard2d(mom, n), shard2d(vel, n), True)
        delta[n], new_m[n], new_v[n], grads_big[n] = (t.reshape(w[n].shape) for t in (d2, m2, v2, g2))

    _ride_next_call(late_join)
    for o in JOIN_GROUPS[1]:
        adamw_big(big_names[o])
    big_grads.update(zip((big_names[o] for o in JOIN_GROUPS[0]), late_join.results))
    for o in JOIN_GROUPS[0]:
        adamw_big(big_names[o])

    joined_last = big_grads[big_names[JOIN_GROUPS[0][-1]]]
    grad_blk, grad_land = _exchange8_wait("allreduce_small_wait", small["blk"], small["land"], small["send"],
                                          small["recv"], joined_last)
    grad_sum = _collect8("allreduce_small_sum", grad_blk, grad_land, True)
    last_blk, last_land = _exchange8_wait("allreduce_last_wait", last_blk, last_land, last_send, last_recv, joined_last)
    dg0_sum = _collect8("allreduce_last_sum", last_blk, last_land, True)
    loss = grad_sum[0, 0]
    shapes = ((1, D),) + REPL_SHAPES[1:] + SHARDED_FULL_SHAPES
    parts = _unpack(grad_sum[8:], shapes, (8,) + REPL_ROWS[1:] + SHARDED_FULL_ROWS)
    grads = dict(zip(REPL_NAMES, parts[:len(REPL_NAMES)]))
    grads["norm_mix_g"] = (jnp.pad(dg0_sum[:8].reshape(1, D), ((0, 1), (0, 0)))
                           + jnp.pad(grads["norm_mix_g"], ((1, 0), (0, 0))))
    for n, full_g, sh in zip(SHARDED_NAMES, parts[len(REPL_NAMES):], SHARDED_SHARD_SHAPES):
        grads[n] = lax.dynamic_slice_in_dim(full_g, chip * LANES, LANES, axis=1).reshape(sh)

    grads.update(grads_big)
    for tag, group, rows, shapes in (("repl", REPL_NAMES, REPL_ROWS, [w[n].shape for n in REPL_NAMES]),
                                     ("sharded", SHARDED_NAMES, SHARDED_SHARD_ROWS, SHARDED_SHARD_SHAPES)):
        packs = [_pack([t[n] for n in group], rows) for t in (w, grads, mom, vel)]
        outs = _adamw(f"adamw_{tag}", *packs)
        for res, o in zip((delta, new_m, new_v), outs):
            res.update(zip(group, _unpack(o, shapes, rows)))

    out = [loss, dx[None]]
    for res in (grads, delta, new_m, new_v):
        out.extend(res[n] for n in names)
    return tuple(out)
```

```python
import functools

import jax
import jax.numpy as jnp
from jax import lax
from jax.experimental import pallas as pl
from jax.experimental.pallas import tpu as pltpu

F32 = jnp.float32
BF16 = jnp.bfloat16

T = 2048
D = 1024
CONV_CH = 512
CONV_W = 31
HEAD_DIM = 64
ATT_W = 1536
EVEN_IN = 5632
ODD_IN = 2560
SCONV_W = 3
SG_GROUPS = 4
CHUNK = 128
D_FF = 4096
EPS = 1e-6
DILATIONS = (1, 4, 16)
BAND = 128
SCALE = HEAD_DIM ** -0.5
NEG = -1e30

ADAM_LR = 0.001
ADAM_B1 = 0.9
ADAM_B2 = 0.999
ADAM_EPS = 1e-08
ADAM_WD = 0.01
ADAM_STEP = 10

V7X_VMEM_BYTES = 64 * 2 ** 20
VMEM_LIMIT = V7X_VMEM_BYTES - 8 * 2 ** 20
LANES = 128
TOKEN_SHAPE = (8, LANES)


class _SideJob:
    def __init__(self, inputs, out_shape, scratch_shapes, aliases, begin, advance, finish):
        self.inputs, self.out_shape, self.scratch_shapes = list(inputs), list(out_shape), list(scratch_shapes)
        self.aliases, self.begin, self.advance, self.finish = dict(aliases), begin, advance, finish
        self.results = None


_PENDING_JOBS = []


def _ride_next_call(job):
    _PENDING_JOBS.append(job)


def _pcall(body, **kw):
    if not _PENDING_JOBS or "grid" not in kw:
        return pl.pallas_call(body, **kw)
    job = _PENDING_JOBS.pop()
    as_list = lambda v: list(v) if isinstance(v, (list, tuple)) else [v]
    single_out = not isinstance(kw["out_shape"], (list, tuple))
    in_specs, out_specs, out_shape = as_list(kw["in_specs"]), as_list(kw["out_specs"]), as_list(kw["out_shape"])
    scratch = list(kw.get("scratch_shapes", ()))
    grid = kw["grid"]
    n_steps = 1
    for extent in grid:
        n_steps *= extent
    assert n_steps >= 3
    n_in, n_out, n_scr = len(in_specs), len(out_specs), len(scratch)
    j_in, j_out = len(job.inputs), len(job.out_shape)
    any_spec = pl.BlockSpec(memory_space=pl.ANY)

    def hosted(*refs):
        ins, j_ins = refs[:n_in], refs[n_in:n_in + j_in]
        outs = refs[n_in + j_in:n_in + j_in + n_out]
        j_outs = refs[n_in + j_in + n_out:n_in + j_in + n_out + j_out]
        scr = refs[n_in + j_in + n_out + j_out:n_in + j_in + n_out + j_out + n_scr]
        j_scr = refs[n_in + j_in + n_out + j_out + n_scr:]
        step = pl.program_id(0)
        for axis in range(1, len(grid)):
            step = step * grid[axis] + pl.program_id(axis)

        @pl.when(step == 0)
        def _():
            job.begin(j_ins, j_outs, j_scr)

        @pl.when(step == 1)
        def _():
            job.advance(j_ins, j_outs, j_scr)

        body(*ins, *outs, *scr)

        @pl.when(step == n_steps - 1)
        def _():
            job.finish(j_ins, j_outs, j_scr)

    aliases = dict(kw.get("input_output_aliases", {}))
    aliases.update({n_in + a: n_out + b for a, b in job.aliases.items()})
    call = pl.pallas_call(
        hosted, name=kw["name"], grid=grid,
        in_specs=in_specs + [any_spec] * j_in, out_specs=out_specs + [any_spec] * j_out,
        out_shape=out_shape + job.out_shape, scratch_shapes=scratch + job.scratch_shapes,
        input_output_aliases=aliases,
        compiler_params=pltpu.CompilerParams(dimension_semantics=("arbitrary",) * len(grid), vmem_limit_bytes=VMEM_LIMIT))

    def run(*args):
        res = call(*args, *job.inputs)
        job.results = list(res[n_out:])
        return res[0] if single_out else list(res[:n_out])

    return run


def _params(*sem):
    return pltpu.CompilerParams(dimension_semantics=sem, vmem_limit_bytes=VMEM_LIMIT)


def _dot(a, b, dims):
    return lax.dot_general(a, b, (dims, ((), ())), preferred_element_type=F32)


def _nn(a, b):
    return _dot(a, b, ((1,), (0,)))


def _nt(a, b):
    return _dot(a, b, ((1,), (1,)))


def _tn(a, b):
    return _dot(a, b, ((0,), (0,)))


def _sigmoid(x):
    return 1.0 / (1.0 + jnp.exp(-x))


MM_VMEM_BUDGET = 40 * 2 ** 20


def _mm_tiles(mode, m, n, k, a_bytes, b_bytes, extra_bytes, out_bytes):
    def divisors(total, unit):
        return [t for t in range(unit, total + 1, unit) if total % t == 0]

    best = None
    for tm in divisors(m, LANES if mode == "tn" else 8):
        for tn in divisors(n, LANES):
            blocks = tm * k * a_bytes + tn * k * b_bytes + tm * tn * (extra_bytes + out_bytes)
            casts = (tm * k * 2 if a_bytes == 4 else 0) + (tn * k * 2 if b_bytes == 4 else 0)
            if 2 * blocks + casts + tm * tn * 4 > MM_VMEM_BUDGET:
                continue
            key = ((m // tm) * (n // tn), (m // tm) * n * k * b_bytes, abs(tm - tn))
            if best is None or key < best[0]:
                best = (key, tm, tn)
    return best[1], best[2]


def _mm(name, mode, a, b, m, n, k, out_dtypes, *, b_off=0, extras=(), epi=None, tie=None):
    tm, tn = _mm_tiles(mode, m, n, k, a.dtype.itemsize, b.dtype.itemsize, sum(e.dtype.itemsize for e in extras),
                       sum(jnp.dtype(dt).itemsize for dt in out_dtypes))
    assert b_off % tn == 0
    b_off //= tn
    if mode == "nn":
        a_spec = pl.BlockSpec((tm, k), lambda i, j: (i, 0))
        b_spec = pl.BlockSpec((k, tn), lambda i, j: (0, j + b_off))
        dims = ((1,), (0,))
    elif mode == "nt":
        a_spec = pl.BlockSpec((tm, k), lambda i, j: (i, 0))
        b_spec = pl.BlockSpec((tn, k), lambda i, j: (j, 0))
        dims = ((1,), (1,))
    else:
        a_spec = pl.BlockSpec((k, tm), lambda i, j: (0, i))
        b_spec = pl.BlockSpec((k, tn), lambda i, j: (0, j))
        dims = ((0,), (0,))
    o_spec = pl.BlockSpec((tm, tn), lambda i, j: (i, j))
    n_extra = len(extras)
    ties = () if tie is None else (tie,)

    def body(a_ref, b_ref, *rest):
        rest = rest[len(ties):]
        acc = _dot(a_ref[...].astype(BF16), b_ref[...].astype(BF16), dims)
        vals = epi(acc, *[e[...] for e in rest[:n_extra]]) if epi is not None else (acc,)
        for o_ref, v in zip(rest[n_extra:], vals):
            o_ref[...] = v.astype(o_ref.dtype)

    outs = _pcall(
        body, name=name, grid=(m // tm, n // tn),
        in_specs=[a_spec, b_spec] + [pl.BlockSpec(TOKEN_SHAPE, lambda i, j: (0, 0))] * len(ties) + [o_spec] * n_extra,
        out_specs=[o_spec] * len(out_dtypes),
        out_shape=[jax.ShapeDtypeStruct((m, n), dt) for dt in out_dtypes],
        compiler_params=_params("parallel", "parallel"),
    )(a, b, *ties, *extras)
    return outs[0] if len(out_dtypes) == 1 else outs


def _row_tile(k, a_bytes, n_row_blocks):
    for tm in (1024, 512, 256, 128):
        if 2 * (tm * k * a_bytes + D * k * 2 + n_row_blocks * tm * D * 4) + tm * D * 4 <= MM_VMEM_BUDGET + 4 * 2 ** 20:
            return tm
    raise ValueError("no row tile fits")


FFN0_DOWN_TILE = 256


def _mm_out_norm(name, a, b, k, res, g_next, tm=None):
    tm = tm or _row_tile(k, a.dtype.itemsize, 3)

    def body(a_ref, b_ref, r_ref, g_ref, h_ref, hn_ref):
        h = _nn(a_ref[...].astype(BF16), b_ref[...]) + r_ref[...]
        h_ref[...] = h
        r = lax.rsqrt(jnp.mean(h * h, axis=-1, keepdims=True) + EPS)
        hn_ref[...] = ((h * r) * g_ref[...]).astype(BF16)

    row = pl.BlockSpec((tm, D), lambda i: (i, 0))
    return _pcall(
        body, name=name, grid=(T // tm,),
        in_specs=[pl.BlockSpec((tm, k), lambda i: (i, 0)), pl.BlockSpec((k, D), lambda i: (0, 0)), row,
                  pl.BlockSpec((1, D), lambda i: (0, 0))],
        out_specs=[row, row],
        out_shape=[jax.ShapeDtypeStruct((T, D), F32), jax.ShapeDtypeStruct((T, D), BF16)],
        compiler_params=_params("parallel"),
    )(a, b, res, g_next)


def _mm_dx_norm(name, dz, w, k, h, g, dres, tie=None):
    tm = _row_tile(k, dz.dtype.itemsize, 3)
    ties = () if tie is None else (tie,)

    def body(a_ref, b_ref, *rest):
        h_ref, g_ref, r_ref, dh_ref, dg_ref = rest[len(ties):]
        dy = _nt(a_ref[...].astype(BF16), b_ref[...])
        x = h_ref[...]
        r = lax.rsqrt(jnp.mean(x * x, axis=-1, keepdims=True) + EPS)
        nrm = x * r
        dn = dy * g_ref[...]
        dh_ref[...] = r_ref[...] + r * (dn - nrm * jnp.mean(dn * nrm, axis=-1, keepdims=True))

        @pl.when(pl.program_id(0) == 0)
        def _():
            dg_ref[...] = jnp.zeros_like(dg_ref)

        dg_ref[...] += jnp.sum(dy * nrm, axis=0, keepdims=True)

    row = pl.BlockSpec((tm, D), lambda i: (i, 0))
    vec = pl.BlockSpec((1, D), lambda i: (0, 0))
    return _pcall(
        body, name=name, grid=(T // tm,),
        in_specs=[pl.BlockSpec((tm, k), lambda i: (i, 0)), pl.BlockSpec((D, k), lambda i: (0, 0))]
        + [pl.BlockSpec(TOKEN_SHAPE, lambda i: (0, 0))] * len(ties) + [row, vec, row],
        out_specs=[row, vec],
        out_shape=[jax.ShapeDtypeStruct((T, D), F32), jax.ShapeDtypeStruct((1, D), F32)],
        compiler_params=_params("arbitrary"),
    )(dz, w, *ties, h, g, dres)


def _rms_fwd(name, h, g, tm=512):
    def body(h_ref, g_ref, o_ref):
        x = h_ref[...]
        r = lax.rsqrt(jnp.mean(x * x, axis=-1, keepdims=True) + EPS)
        o_ref[...] = ((x * r) * g_ref[...]).astype(BF16)

    return _pcall(
        body, name=name, grid=(T // tm,),
        in_specs=[pl.BlockSpec((tm, D), lambda i: (i, 0)), pl.BlockSpec((1, D), lambda i: (0, 0))],
        out_specs=pl.BlockSpec((tm, D), lambda i: (i, 0)),
        out_shape=jax.ShapeDtypeStruct((T, D), BF16),
        compiler_params=_params("parallel"),
    )(h, g)


def _loss_head(h, g, target, tm=512):
    def body(h_ref, g_ref, t_ref, dh_ref, dg_ref, loss_ref):
        x = h_ref[...]
        r = lax.rsqrt(jnp.mean(x * x, axis=-1, keepdims=True) + EPS)
        nrm = x * r
        gain = g_ref[...]
        err = nrm * gain - t_ref[...]
        dy = err * (1.0 / D)
        dn = dy * gain
        dh_ref[...] = r * (dn - nrm * jnp.mean(dn * nrm, axis=-1, keepdims=True))

        @pl.when(pl.program_id(0) == 0)
        def _():
            dg_ref[...] = jnp.zeros_like(dg_ref)
            loss_ref[...] = jnp.zeros_like(loss_ref)

        dg_ref[...] += jnp.sum(dy * nrm, axis=0, keepdims=True)
        part = jnp.sum(jnp.sum(err * err, axis=1, keepdims=True), axis=0, keepdims=True) * (0.5 / D)
        loss_ref[...] += jnp.broadcast_to(part, (1, LANES))

    row = pl.BlockSpec((tm, D), lambda i: (i, 0))
    vec = pl.BlockSpec((1, D), lambda i: (0, 0))
    return _pcall(
        body, name="loss_head", grid=(T // tm,),
        in_specs=[row, vec, row], out_specs=[row, vec, pl.BlockSpec((1, LANES), lambda i: (0, 0))],
        out_shape=[jax.ShapeDtypeStruct((T, D), F32), jax.ShapeDtypeStruct((1, D), F32),
                   jax.ShapeDtypeStruct((1, LANES), F32)],
        compiler_params=_params("arbitrary"),
    )(h, g, target)


CONV_TILE = 256
CONV_HALO = 32


def _glu(z):
    return z[:, :CONV_CH] * _sigmoid(z[:, CONV_CH:])


SUBLANES = 8


def _sublane_shifts(win):
    n = win.shape[0]
    return [win] + [win[r:r + n - SUBLANES, :] for r in range(1, SUBLANES)]


def _rows_from(shifts, off, n):
    q, r = divmod(off, SUBLANES)
    return shifts[r][q * SUBLANES:q * SUBLANES + n, :]


def _econv_fwd(zc, conv_k, conv_b, ln_g, ln_b):
    R, H = CONV_TILE, CONV_HALO

    def body(z_ref, zh_ref, k_ref, b_ref, g_ref, be_ref, cv_ref, cat_ref):
        i = pl.program_id(0)
        glu = _glu(z_ref[...])
        halo = _glu(zh_ref[...]) * (i > 0).astype(F32)
        win = _sublane_shifts(jnp.concatenate([halo, glu], axis=0))
        acc = jnp.zeros((R, CONV_CH), F32) + b_ref[...]
        for j in range(CONV_W):
            acc = acc + k_ref[j:j + 1, :] * _rows_from(win, H - (CONV_W - 1) + j, R)
        cv_ref[...] = acc
        mu = jnp.mean(acc, axis=-1, keepdims=True)
        xc = acc - mu
        rstd = lax.rsqrt(jnp.mean(xc * xc, axis=-1, keepdims=True) + EPS)
        ln = xc * rstd * g_ref[...] + be_ref[...]
        cat_ref[...] = (ln * _sigmoid(ln)).astype(BF16)

    vec = pl.BlockSpec((1, CONV_CH), lambda i: (0, 0))
    return _pcall(
        body, name="econv_fwd", grid=(T // R,),
        in_specs=[pl.BlockSpec((R, 2 * CONV_CH), lambda i: (i, 0)),
                  pl.BlockSpec((H, 2 * CONV_CH), lambda i: (jnp.maximum(i * (R // H) - 1, 0), 0)),
                  pl.BlockSpec((CONV_W, CONV_CH), lambda i: (0, 0)), vec, vec, vec],
        out_specs=[pl.BlockSpec((R, CONV_CH), lambda i: (i, 0)), pl.BlockSpec((R, CONV_CH), lambda i: (i, 0))],
        out_shape=[jax.ShapeDtypeStruct((T, CONV_CH), F32), jax.ShapeDtypeStruct((T, D), BF16)],
        compiler_params=_params("parallel"),
    )(zc, zc, conv_k, conv_b, ln_g, ln_b)


def _econv_bwd_ln(cv, dcat, ln_g, ln_b):
    R = CONV_TILE

    def body(cv_ref, d_ref, g_ref, be_ref, dcv_ref, dg_ref, dbe_ref, dcb_ref):
        cv_t = cv_ref[...]
        mu = jnp.mean(cv_t, axis=-1, keepdims=True)
        xc = cv_t - mu
        rstd = lax.rsqrt(jnp.mean(xc * xc, axis=-1, keepdims=True) + EPS)
        xh = xc * rstd
        ln = xh * g_ref[...] + be_ref[...]
        sg = _sigmoid(ln)
        dln = d_ref[...] * (sg * (1.0 + ln * (1.0 - sg)))
        dxh = dln * g_ref[...]
        dcv = rstd * (dxh - jnp.mean(dxh, axis=-1, keepdims=True) - xh * jnp.mean(dxh * xh, axis=-1, keepdims=True))
        dcv_ref[...] = dcv

        @pl.when(pl.program_id(0) == 0)
        def _():
            dg_ref[...] = jnp.zeros_like(dg_ref)
            dbe_ref[...] = jnp.zeros_like(dbe_ref)
            dcb_ref[...] = jnp.zeros_like(dcb_ref)

        dg_ref[...] += jnp.sum(dln * xh, axis=0, keepdims=True)
        dbe_ref[...] += jnp.sum(dln, axis=0, keepdims=True)
        dcb_ref[...] += jnp.sum(dcv, axis=0, keepdims=True)

    vec = pl.BlockSpec((1, CONV_CH), lambda i: (0, 0))
    row = pl.BlockSpec((R, CONV_CH), lambda i: (i, 0))
    vshape = jax.ShapeDtypeStruct((1, CONV_CH), F32)
    return _pcall(
        body, name="econv_bwd_ln", grid=(T // R,),
        in_specs=[row, row, vec, vec], out_specs=[row, vec, vec, vec],
        out_shape=[jax.ShapeDtypeStruct((T, CONV_CH), F32), vshape, vshape, vshape],
        compiler_params=_params("arbitrary"),
    )(cv, dcat, ln_g, ln_b)


def _econv_bwd_conv(dcv, zc, conv_k):
    R, H = CONV_TILE, CONV_HALO
    last = T // R - 1

    def body(d_ref, dn_ref, z_ref, zh_ref, k_ref, dz_ref, dk_ref):
        i = pl.program_id(0)
        z = z_ref[...]
        a_lin = z[:, :CONV_CH]
        sg = _sigmoid(z[:, CONV_CH:])
        glu = a_lin * sg
        halo = _glu(zh_ref[...]) * (i > 0).astype(F32)
        win = _sublane_shifts(jnp.concatenate([halo, glu], axis=0))
        dcv_t = d_ref[...]
        nxt = dn_ref[...] * (i < last).astype(F32)
        winb = _sublane_shifts(jnp.concatenate([dcv_t, nxt], axis=0))

        @pl.when(i == 0)
        def _():
            dk_ref[...] = jnp.zeros_like(dk_ref)

        dglu = jnp.zeros((R, CONV_CH), F32)
        for j in range(CONV_W):
            dk_ref[j:j + 1, :] += jnp.sum(dcv_t * _rows_from(win, H - (CONV_W - 1) + j, R), axis=0, keepdims=True)
            dglu = dglu + k_ref[j:j + 1, :] * _rows_from(winb, CONV_W - 1 - j, R)
        dz_ref[...] = jnp.concatenate([dglu * sg, dglu * a_lin * sg * (1.0 - sg)], axis=1).astype(BF16)

    return _pcall(
        body, name="econv_bwd_conv", grid=(T // R,),
        in_specs=[pl.BlockSpec((R, CONV_CH), lambda i: (i, 0)),
                  pl.BlockSpec((H, CONV_CH), lambda i: (jnp.minimum((i + 1) * (R // H), T // H - 1), 0)),
                  pl.BlockSpec((R, 2 * CONV_CH), lambda i: (i, 0)),
                  pl.BlockSpec((H, 2 * CONV_CH), lambda i: (jnp.maximum(i * (R // H) - 1, 0), 0)),
                  pl.BlockSpec((CONV_W, CONV_CH), lambda i: (0, 0))],
        out_specs=[pl.BlockSpec((R, 2 * CONV_CH), lambda i: (i, 0)), pl.BlockSpec((CONV_W, CONV_CH), lambda i: (0, 0))],
        out_shape=[jax.ShapeDtypeStruct((T, EVEN_IN), BF16), jax.ShapeDtypeStruct((CONV_W, CONV_CH), F32)],
        compiler_params=_params("arbitrary"),
    )(dcv, dcv, zc, zc, conv_k)


def _swap_halves(v):
    lane = lax.broadcasted_iota(jnp.int32, v.shape, 1)
    return jnp.where((lane % HEAD_DIM) < HEAD_DIM // 2, pltpu.roll(v, LANES - HEAD_DIM // 2, 1),
                     pltpu.roll(v, HEAD_DIM // 2, 1))


def _qkv_proj(hn, w_in, rope_c, rope_s, tm=T):
    tn = 4 * LANES

    def body(a_ref, b_ref, c_ref, s_ref, o_ref):
        j = pl.program_id(1)
        acc = _nn(a_ref[...], b_ref[...])
        for p in range(4):
            v = acc[:, p * LANES:(p + 1) * LANES]
            rot = v * c_ref[...] + _swap_halves(v) * s_ref[...]
            o_ref[p] = jnp.where(j < 6, rot, v)

    tab = pl.BlockSpec((tm, LANES), lambda i, j: (i, 0))
    return _pcall(
        body, name="qkv_proj", grid=(T // tm, 9),
        in_specs=[pl.BlockSpec((tm, D), lambda i, j: (i, 0)),
                  pl.BlockSpec((D, tn), lambda i, j: (0, j + (2 * CONV_CH) // tn)), tab, tab],
        out_specs=pl.BlockSpec((None, 4, tm, LANES), lambda i, j: (j, 0, i, 0)),
        out_shape=jax.ShapeDtypeStruct((9, 4, T, LANES), F32),
        compiler_params=_params("parallel", "parallel"),
    )(hn, w_in, rope_c, rope_s)


ATTN_FWD_UNROLL = 4
ATTN_BWD_UNROLL = 4


def _band_rows(start, d):
    if d == 1:
        return pl.ds(pl.multiple_of(start, BAND), BAND)
    return pl.ds(start, BAND, stride=d)


def _band_masks(n):
    row = lax.broadcasted_iota(jnp.int32, (BAND, BAND), 0)
    col = lax.broadcasted_iota(jnp.int32, (BAND, BAND), 1)
    no_prev = (n == 0).astype(jnp.int32) * (2 * BAND)
    return col <= row, col >= row + no_prev


def _attn_fwd(qkv, g):
    d = DILATIONS[g]
    nb = T // d // BAND

    def body(q_ref, k_ref, v_ref, o_ref, l_ref):
        lane_lo = lax.broadcasted_iota(jnp.int32, (BAND, LANES), 1) < HEAD_DIM

        heads = (lane_lo, jnp.logical_not(lane_lo))
        ones = jnp.ones((BAND, LANES), BF16)

        def step(it, carry):
            tiles = []
            for u in range(ATTN_FWD_UNROLL):
                idx = it * ATTN_FWD_UNROLL + u
                r = idx // nb
                n = idx % nb
                cur = _band_rows(n * (BAND * d) + r, d)
                prev = _band_rows(jnp.maximum(n - 1, 0) * (BAND * d) + r, d)
                mc, mp = _band_masks(n)
                tiles.append((cur, mc, mp, q_ref[cur, :], k_ref[cur, :].astype(BF16), v_ref[cur, :].astype(BF16),
                              k_ref[prev, :].astype(BF16), v_ref[prev, :].astype(BF16)))
            scores = []
            for cur, mc, mp, q, kc, vc, kp, vp in tiles:
                for hm in heads:
                    qm = jnp.where(hm, q, 0.0).astype(BF16)
                    scores.append((jnp.where(mc, _nt(qm, kc) * SCALE, NEG), jnp.where(mp, _nt(qm, kp) * SCALE, NEG)))
            maxes = [jnp.maximum(jnp.max(sc, axis=1, keepdims=True), jnp.max(sp, axis=1, keepdims=True))
                     for sc, sp in scores]
            probs = [(jnp.exp(sc - mx).astype(BF16), jnp.exp(sp - mx).astype(BF16))
                     for (sc, sp), mx in zip(scores, maxes)]
            dens = [_nn(pc, ones) + _nn(pp, ones) for pc, pp in probs]
            for t, (cur, mc, mp, q, kc, vc, kp, vp) in enumerate(tiles):
                outs, lses = [], []
                for h in range(2):
                    pc, pp = probs[2 * t + h]
                    outs.append((_nn(pc, vc) + _nn(pp, vp)) / dens[2 * t + h])
                    lses.append(maxes[2 * t + h] + jnp.log(dens[2 * t + h]))
                o_ref[cur, :] = jnp.where(lane_lo, outs[0], outs[1])
                l_ref[cur, :] = jnp.where(lane_lo, lses[0], lses[1])
            return carry

        lax.fori_loop(0, d * nb // ATTN_FWD_UNROLL, step, 0)

    def slab(which):
        return pl.BlockSpec((None, None, T, LANES), lambda p: (which * 3 + g, p, 0, 0))

    out = pl.BlockSpec((None, T, LANES), lambda p: (p, 0, 0))
    shape = jax.ShapeDtypeStruct((4, T, LANES), F32)
    return _pcall(
        body, name=f"attn_fwd{g}", grid=(4,),
        in_specs=[slab(0), slab(1), slab(2)], out_specs=[out, out], out_shape=[shape, shape],
        compiler_params=_params("parallel"),
    )(qkv, qkv, qkv)


def _attn_merge(outs, lses, cat, tm=1024):
    def body(o0, o1, o2, l0, l1, l2, cat_in, cat_ref, att_ref, w0, w1, w2):
        del cat_in
        la, lb, lc = l0[...], l1[...], l2[...]
        mx = jnp.maximum(jnp.maximum(la, lb), lc)
        ea, eb, ec = jnp.exp(la - mx), jnp.exp(lb - mx), jnp.exp(lc - mx)
        inv = 1.0 / (ea + eb + ec)
        wa, wb, wc = ea * inv, eb * inv, ec * inv
        att = wa * o0[...] + wb * o1[...] + wc * o2[...]
        att_ref[...] = att
        cat_ref[...] = att.astype(BF16)
        w0[...] = wa
        w1[...] = wb
        w2[...] = wc

    slab = pl.BlockSpec((None, tm, LANES), lambda p, i: (p, i, 0))
    shape = jax.ShapeDtypeStruct((4, T, LANES), F32)
    return _pcall(
        body, name="attn_merge", grid=(4, T // tm),
        in_specs=[slab] * 6 + [pl.BlockSpec(memory_space=pl.ANY)],
        out_specs=[pl.BlockSpec((tm, LANES), lambda p, i: (i, CONV_CH // LANES + p)), slab, slab, slab, slab],
        out_shape=[jax.ShapeDtypeStruct((T, D), BF16), shape, shape, shape, shape],
        input_output_aliases={6: 0},
        compiler_params=_params("parallel", "parallel"),
    )(*outs, *lses, cat)


def _attn_bwd(qkv, lse, wgt, att, dcat, dqkv, g):
    d = DILATIONS[g]
    nb = T // d // BAND

    def body(q_ref, k_ref, v_ref, l_ref, w_ref, a_ref, da_ref, dq_in, o_ref):
        del dq_in
        lane = lax.broadcasted_iota(jnp.int32, (BAND, LANES), 1)
        lane_lo = lane < HEAD_DIM
        row = lax.broadcasted_iota(jnp.int32, (LANES, LANES), 0)
        same_head = ((row // HEAD_DIM) == (lane // HEAD_DIM)).astype(BF16)
        dq_ref, dk_ref, dv_ref = o_ref.at[0], o_ref.at[1], o_ref.at[2]
        dk_ref[...] = jnp.zeros((T, LANES), F32)
        dv_ref[...] = jnp.zeros((T, LANES), F32)

        heads = (lane_lo, jnp.logical_not(lane_lo))

        def step(it, carry):
            tiles = []
            for u in range(ATTN_BWD_UNROLL):
                idx = it * ATTN_BWD_UNROLL + u
                r = idx // nb
                n = idx % nb
                cur = _band_rows(n * (BAND * d) + r, d)
                prev = _band_rows(jnp.maximum(n - 1, 0) * (BAND * d) + r, d)
                mc, mp = _band_masks(n)
                da = da_ref[cur, :]
                prod = da * a_ref[cur, :]
                hi = prod.astype(BF16)
                lo = (prod - hi.astype(F32)).astype(BF16)
                tiles.append(dict(cur=cur, prev=prev, mc=mc, mp=mp, da=da, hi=hi, lo=lo, q=q_ref[cur, :],
                                  kc=k_ref[cur, :].astype(BF16), vc=v_ref[cur, :].astype(BF16),
                                  kp=k_ref[prev, :].astype(BF16), vp=v_ref[prev, :].astype(BF16),
                                  lse=l_ref[cur, :], w=w_ref[cur, :]))
            for t in tiles:
                t["csum"] = _nn(t["hi"], same_head) + _nn(t["lo"], same_head)
            chains = []
            for t in tiles:
                for h, hm in enumerate(heads):
                    qm = jnp.where(hm, t["q"], 0.0).astype(BF16)
                    dam = jnp.where(hm, t["da"], 0.0).astype(BF16)
                    chains.append(dict(t=t, h=h, qm=qm, dam=dam,
                                       sc=jnp.where(t["mc"], _nt(qm, t["kc"]) * SCALE, NEG),
                                       sp=jnp.where(t["mp"], _nt(qm, t["kp"]) * SCALE, NEG),
                                       dpc=_nt(dam, t["vc"]), dpp=_nt(dam, t["vp"])))
            for ch in chains:
                t, col0 = ch["t"], ch["h"] * HEAD_DIM
                lse_h = t["lse"][:, col0:col0 + 1]
                w_h = t["w"][:, col0:col0 + 1]
                c_h = t["csum"][:, col0:col0 + 1]
                pwc = w_h * jnp.exp(ch["sc"] - lse_h)
                pwp = w_h * jnp.exp(ch["sp"] - lse_h)
                ch["dsc"] = (pwc * (ch["dpc"] - c_h) * SCALE).astype(BF16)
                ch["dsp"] = (pwp * (ch["dpp"] - c_h) * SCALE).astype(BF16)
                ch["pwc"] = pwc.astype(BF16)
                ch["pwp"] = pwp.astype(BF16)
            for ch in chains:
                t = ch["t"]
                ch["dq"] = _nn(ch["dsc"], t["kc"]) + _nn(ch["dsp"], t["kp"])
                ch["dkc"] = _tn(ch["dsc"], ch["qm"])
                ch["dkp"] = _tn(ch["dsp"], ch["qm"])
                ch["dvc"] = _tn(ch["pwc"], ch["dam"])
                ch["dvp"] = _tn(ch["pwp"], ch["dam"])
            for i, t in enumerate(tiles):
                c0, c1 = chains[2 * i], chains[2 * i + 1]
                dq_ref[t["cur"], :] = jnp.where(lane_lo, c0["dq"], c1["dq"])
                dk_ref[t["cur"], :] += c0["dkc"] + c1["dkc"]
                dk_ref[t["prev"], :] += c0["dkp"] + c1["dkp"]
                dv_ref[t["cur"], :] += c0["dvc"] + c1["dvc"]
                dv_ref[t["prev"], :] += c0["dvp"] + c1["dvp"]
            return carry

        lax.fori_loop(0, d * nb // ATTN_BWD_UNROLL, step, 0)

    def slab(which):
        return pl.BlockSpec((None, None, T, LANES), lambda p: (which * 3 + g, p, 0, 0))

    per_pair = pl.BlockSpec((None, T, LANES), lambda p: (p, 0, 0))
    return _pcall(
        body, name=f"attn_bwd{g}", grid=(4,),
        in_specs=[slab(0), slab(1), slab(2), per_pair, per_pair, per_pair,
                  pl.BlockSpec((T, LANES), lambda p: (0, CONV_CH // LANES + p)),
                  pl.BlockSpec(memory_space=pl.ANY)],
        out_specs=pl.BlockSpec((None, 3, None, T, LANES), lambda p: (g, 0, p, 0, 0)),
        out_shape=jax.ShapeDtypeStruct((3, 3, 4, T, LANES), F32),
        input_output_aliases={7: 0},
        compiler_params=_params("parallel"),
    )(qkv, qkv, qkv, lse, wgt, att, dcat, dqkv)


def _rope_bwd(dqkv, rope_c, rope_s, dz):
    wide = 4 * LANES

    def body(d_ref, c_ref, s_ref, dz_in, o_ref):
        del dz_in
        w = pl.program_id(1)
        for p in range(4):
            v = d_ref[p]
            rot = v * c_ref[...] + _swap_halves(v * s_ref[...])
            o_ref[:, p * LANES:(p + 1) * LANES] = jnp.where(w < 2, rot, v).astype(BF16)

    tab = pl.BlockSpec((T, LANES), lambda g, w: (0, 0))
    return _pcall(
        body, name="rope_bwd", grid=(3, 3),
        in_specs=[pl.BlockSpec((None, None, 4, T, LANES), lambda g, w: (g, w, 0, 0, 0)), tab, tab,
                  pl.BlockSpec(memory_space=pl.ANY)],
        out_specs=pl.BlockSpec((T, wide), lambda g, w: (0, (2 * CONV_CH) // wide + w * 3 + g)),
        out_shape=jax.ShapeDtypeStruct((T, EVEN_IN), BF16),
        input_output_aliases={3: 0},
        compiler_params=_params("parallel", "parallel"),
    )(dqkv, rope_c, rope_s, dz)


ODD_TILE = 256
ODD_HALO = 8
GELU_C = 0.7978845608028654
GELU_A = 0.044715


def _gelu(x):
    return 0.5 * x * (1.0 + jnp.tanh(GELU_C * (x + GELU_A * x * x * x)))


def _gelu_grad(x):
    th = jnp.tanh(GELU_C * (x + GELU_A * x * x * x))
    return 0.5 * (1.0 + th) + 0.5 * x * (1.0 - th * th) * GELU_C * (1.0 + 3.0 * GELU_A * x * x)


def _tril():
    row = lax.broadcasted_iota(jnp.int32, (CHUNK, CHUNK), 0)
    col = lax.broadcasted_iota(jnp.int32, (CHUNK, CHUNK), 1)
    return (col <= row).astype(F32)


def _odd_parts(z, zh, i, k_ref, g_ref, be_ref, w_ref, bt_ref):
    R, H = ODD_TILE, ODD_HALO
    gb, gc, xs, uv = z[:, :512], z[:, 512:1024], z[:, 1024:1536], z[:, 1536:]
    halo = zh[:, 512:1024] * zh[:, 1024:1536] * (i > 0).astype(F32)
    win = jnp.concatenate([halo, gc * xs], axis=0)
    cv = jnp.zeros((R, 512), F32)
    for j in range(SCONV_W):
        off = H - (SCONV_W - 1) + j
        cv = cv + k_ref[j:j + 1, :] * win[off:off + R, :]
    ge = _gelu(uv)
    u, v = ge[:, :512], ge[:, 512:]
    mu = jnp.mean(v, axis=-1, keepdims=True)
    xc = v - mu
    rstd = lax.rsqrt(jnp.mean(xc * xc, axis=-1, keepdims=True) + EPS)
    xh = xc * rstd
    vn = xh * g_ref[...] + be_ref[...]
    tril = _tril()
    wms = [(w_ref[g] * tril).astype(BF16) for g in range(SG_GROUPS)]
    rows = []
    for ci in range(R // CHUNK):
        blocks = []
        for g in range(SG_GROUPS):
            blk = vn[ci * CHUNK:(ci + 1) * CHUNK, g * LANES:(g + 1) * LANES].astype(BF16)
            blocks.append(_nn(wms[g], blk) + bt_ref[:, g:g + 1])
        rows.append(jnp.concatenate(blocks, axis=1))
    vmix = jnp.concatenate(rows, axis=0)
    return gb, gc, xs, uv, win, cv, u, rstd, xh, vn, vmix, wms


def _odd_mid_fwd(z, conv_k, ln_g, ln_b, sg_w, sg_bt):
    R, H = ODD_TILE, ODD_HALO

    def body(z_ref, zh_ref, k_ref, g_ref, be_ref, w_ref, bt_ref, o_ref):
        i = pl.program_id(0)
        gb, _, _, _, _, cv, u, _, _, _, vmix, _ = _odd_parts(z_ref[...], zh_ref[...], i, k_ref, g_ref, be_ref, w_ref, bt_ref)
        o_ref[...] = jnp.concatenate([gb * cv, u * vmix], axis=1).astype(BF16)

    vec = pl.BlockSpec((1, 512), lambda i: (0, 0))
    return _pcall(
        body, name="odd_mid_fwd", grid=(T // R,),
        in_specs=[pl.BlockSpec((R, ODD_IN), lambda i: (i, 0)),
                  pl.BlockSpec((H, ODD_IN), lambda i: (jnp.maximum(i * (R // H) - 1, 0), 0)),
                  pl.BlockSpec((SCONV_W, 512), lambda i: (0, 0)), vec, vec,
                  pl.BlockSpec((SG_GROUPS, CHUNK, CHUNK), lambda i: (0, 0, 0)),
                  pl.BlockSpec((CHUNK, SG_GROUPS), lambda i: (0, 0))],
        out_specs=pl.BlockSpec((R, D), lambda i: (i, 0)),
        out_shape=jax.ShapeDtypeStruct((T, D), BF16),
        compiler_params=_params("parallel"),
    )(z, z, conv_k, ln_g, ln_b, sg_w, sg_bt)


def _odd_mid_bwd(z, dcat, conv_k, ln_g, ln_b, sg_w, sg_bt):
    R, H = ODD_TILE, ODD_HALO
    last = T // R - 1

    def body(z_ref, zh_ref, zn_ref, d_ref, dn_ref, k_ref, g_ref, be_ref, w_ref, bt_ref,
             dz_ref, dk_ref, dg_ref, dbe_ref, dw_ref, dbt_ref):
        i = pl.program_id(0)
        z = z_ref[...]
        gb, gc, xs, uv, win, cv, u, rstd, xh, vn, vmix, wms = _odd_parts(z, zh_ref[...], i, k_ref, g_ref, be_ref, w_ref, bt_ref)
        dcat_t = d_ref[...]
        dc, dd = dcat_t[:, :512], dcat_t[:, 512:]

        @pl.when(i == 0)
        def _():
            dk_ref[...] = jnp.zeros_like(dk_ref)
            dg_ref[...] = jnp.zeros_like(dg_ref)
            dbe_ref[...] = jnp.zeros_like(dbe_ref)
            dw_ref[...] = jnp.zeros_like(dw_ref)
            dbt_ref[...] = jnp.zeros_like(dbt_ref)

        dgb = dc * cv
        dcv = dc * gb
        nxt = dn_ref[:, :512] * zn_ref[:, :512] * (i < last).astype(F32)
        winb = jnp.concatenate([dcv, nxt], axis=0)
        dp = jnp.zeros((R, 512), F32)
        for j in range(SCONV_W):
            off = H - (SCONV_W - 1) + j
            dk_ref[j:j + 1, :] += jnp.sum(dcv * win[off:off + R, :], axis=0, keepdims=True)
            ob = SCONV_W - 1 - j
            dp = dp + k_ref[j:j + 1, :] * winb[ob:ob + R, :]
        dgc = dp * xs
        dxs = dp * gc
        du = dd * vmix
        dvmix = dd * u
        tril = _tril()
        rows = []
        for ci in range(R // CHUNK):
            blocks = []
            for g in range(SG_GROUPS):
                sl = (slice(ci * CHUNK, (ci + 1) * CHUNK), slice(g * LANES, (g + 1) * LANES))
                dblk = dvmix[sl]
                dblk16 = dblk.astype(BF16)
                blocks.append(_tn(wms[g], dblk16))
                dw_ref[g] += _nt(dblk16, vn[sl].astype(BF16)) * tril
                dbt_ref[:, g:g + 1] += jnp.sum(dblk, axis=1, keepdims=True)
            rows.append(jnp.concatenate(blocks, axis=1))
        dvn = jnp.concatenate(rows, axis=0)
        dg_ref[...] += jnp.sum(dvn * xh, axis=0, keepdims=True)
        dbe_ref[...] += jnp.sum(dvn, axis=0, keepdims=True)
        dxh = dvn * g_ref[...]
        dv = rstd * (dxh - jnp.mean(dxh, axis=-1, keepdims=True) - xh * jnp.mean(dxh * xh, axis=-1, keepdims=True))
        duv = jnp.concatenate([du, dv], axis=1) * _gelu_grad(uv)
        dz_ref[...] = jnp.concatenate([dgb, dgc, dxs, duv], axis=1).astype(BF16)

    vec = pl.BlockSpec((1, 512), lambda i: (0, 0))
    kspec = pl.BlockSpec((SCONV_W, 512), lambda i: (0, 0))
    wspec = pl.BlockSpec((SG_GROUPS, CHUNK, CHUNK), lambda i: (0, 0, 0))
    bspec = pl.BlockSpec((CHUNK, SG_GROUPS), lambda i: (0, 0))
    nxt_blk = lambda i: (jnp.minimum((i + 1) * (R // H), T // H - 1), 0)
    return _pcall(
        body, name="odd_mid_bwd", grid=(T // R,),
        in_specs=[pl.BlockSpec((R, ODD_IN), lambda i: (i, 0)),
                  pl.BlockSpec((H, ODD_IN), lambda i: (jnp.maximum(i * (R // H) - 1, 0), 0)),
                  pl.BlockSpec((H, ODD_IN), nxt_blk),
                  pl.BlockSpec((R, D), lambda i: (i, 0)),
                  pl.BlockSpec((H, D), nxt_blk),
                  kspec, vec, vec, wspec, bspec],
        out_specs=[pl.BlockSpec((R, ODD_IN), lambda i: (i, 0)), kspec, vec, vec, wspec, bspec],
        out_shape=[jax.ShapeDtypeStruct((T, ODD_IN), BF16), jax.ShapeDtypeStruct((SCONV_W, 512), F32),
                   jax.ShapeDtypeStruct((1, 512), F32), jax.ShapeDtypeStruct((1, 512), F32),
                   jax.ShapeDtypeStruct((SG_GROUPS, CHUNK, CHUNK), F32), jax.ShapeDtypeStruct((CHUNK, SG_GROUPS), F32)],
        compiler_params=_params("arbitrary"),
    )(z, z, z, dcat, dcat, conv_k, ln_g, ln_b, sg_w, sg_bt)


def _ffn_up(tag, hn, weight):
    def act(acc):
        r = jnp.maximum(acc, 0.0)
        return (r * r,)

    return _mm(f"ffn{tag}_up", "nn", hn, weight(f"ffn_w1_{tag}", hn), T, D_FF, D, (BF16,), epi=act)


def _ffn_bwd(tag, h, g, weight, emit, saved, dout):
    hn, f = saved
    du = _mm(f"ffn{tag}_dact", "nt", dout, weight(f"ffn_w2_{tag}", dout), T, D_FF, D, (BF16,),
             epi=lambda acc, ff: (acc * (2.0 * jnp.sqrt(ff.astype(F32))),), extras=(f,))
    tok = emit(f"ffn_w2_{tag}", f, dout)
    tok = emit(f"ffn_w1_{tag}", hn, du, tie=tok)
    return _mm_dx_norm(f"ffn{tag}_dhn", du, weight(f"ffn_w1_{tag}", du), D_FF, h, g, dout, tie=tok)


def _rope_tables():
    half = HEAD_DIM // 2
    inv = 10000.0 ** (-jnp.arange(half, dtype=F32) / half)
    ang = jnp.arange(T, dtype=F32)[:, None] * inv[None, :]
    cos, sin = jnp.cos(ang), jnp.sin(ang)
    c = jnp.tile(jnp.concatenate([cos, cos], axis=1), (1, LANES // HEAD_DIM))
    s = jnp.tile(jnp.concatenate([-sin, sin], axis=1), (1, LANES // HEAD_DIM))
    return c, s


def _local_step(x, target, p, weight, emit, emit_small, before=lambda name, after: None):
    rope_c, rope_s = _rope_tables()
    grads = {}
    residual = lambda acc, res: (acc + res,)

    hn0 = _rms_fwd("mix0_norm", x, p["norm_mix_g0"])
    zc = _mm("even_in_conv", "nn", hn0, weight("even_w_in", hn0), T, 2 * CONV_CH, D, (F32,))
    qkv = _qkv_proj(hn0, weight("even_w_in", hn0), rope_c, rope_s)
    cv, cat0 = _econv_fwd(zc, p["even_conv_k"], p["even_conv_b"], p["even_ln_g"], p["even_ln_b"])
    att_parts = [_attn_fwd(qkv, 0)]
    before("attn_fwd1", att_parts[0][0])
    att_parts += [_attn_fwd(qkv, 1), _attn_fwd(qkv, 2)]
    outs = [a[0] for a in att_parts]
    lses = [a[1] for a in att_parts]
    cat0, att, w0, w1, w2 = _attn_merge(outs, lses, cat0)
    wgts = (w0, w1, w2)
    h1, hnf0 = _mm_out_norm("even_out", cat0, weight("even_w_out", cat0), D, x, p["norm_ffn_g0"])
    f0 = _ffn_up(0, hnf0, weight)
    before("ffn0_down", f0)
    h2, hn1 = _mm_out_norm("ffn0_down", f0, weight("ffn_w2_0", f0), D_FF, h1, p["norm_mix_g1"], tm=FFN0_DOWN_TILE)

    z1 = _mm("odd_in", "nn", hn1, weight("odd_w_in", hn1), T, ODD_IN, D, (F32,))
    cat1 = _odd_mid_fwd(z1, p["odd_conv_k"], p["odd_ln_g"], p["odd_ln_b"], p["odd_sg_w"], p["odd_sg_bt"])
    h3, hnf1 = _mm_out_norm("odd_out", cat1, weight("odd_w_out", cat1), D, h2, p["norm_ffn_g1"])
    f1 = _ffn_up(1, hnf1, weight)
    h4 = _mm("ffn1_down", "nn", f1, weight("ffn_w2_1", f1), T, D, D_FF, (F32,), epi=residual, extras=(h3,))

    dh4, grads["final_g"], loss = _loss_head(h4, p["final_g"], target)

    dh3, grads["norm_ffn_g1"] = _ffn_bwd(1, h3, p["norm_ffn_g1"], weight, emit, (hnf1, f1), dh4)
    tok = emit("odd_w_out", cat1, dh3)
    dcat1 = _mm("odd_out_dx", "nt", dh3, weight("odd_w_out", dh3), T, D, D, (F32,), tie=tok)
    dz1, grads["odd_conv_k"], grads["odd_ln_g"], grads["odd_ln_b"], grads["odd_sg_w"], grads["odd_sg_bt"] = _odd_mid_bwd(
        z1, dcat1, p["odd_conv_k"], p["odd_ln_g"], p["odd_ln_b"], p["odd_sg_w"], p["odd_sg_bt"])
    tok = emit("odd_w_in", hn1, dz1)
    dh2, grads["norm_mix_g1"] = _mm_dx_norm("odd_in_dx", dz1, weight("odd_w_in", dz1), ODD_IN, h2, p["norm_mix_g1"],
                                            dh3, tie=tok)

    dh1, grads["norm_ffn_g0"] = _ffn_bwd(0, h1, p["norm_ffn_g0"], weight, emit, (hnf0, f0), dh2)
    tok = emit("even_w_out", cat0, dh1)
    dcat0 = _mm("even_out_dx", "nt", dh1, weight("even_w_out", dh1), T, D, D, (F32,), tie=tok)
    dcv, grads["even_ln_g"], grads["even_ln_b"], grads["even_conv_b"] = _econv_bwd_ln(
        cv, dcat0, p["even_ln_g"], p["even_ln_b"])
    dz0, grads["even_conv_k"] = _econv_bwd_conv(dcv, zc, p["even_conv_k"])
    tok = emit_small(loss, grads)
    dqkv = lax.empty((3, 3, 4, T, LANES), F32)
    for g in range(3):
        dqkv = _attn_bwd(qkv, lses[g], wgts[g], att, dcat0, dqkv, g)
    before("rope_bwd", dqkv)
    dz0 = _rope_bwd(dqkv, rope_c, rope_s, dz0)
    tok = emit("even_w_in", hn0, dz0, tie=tok)
    dx, dg0 = _mm_dx_norm("even_in_dx", dz0, weight("even_w_in", dz0), EVEN_IN, x, p["norm_mix_g0"], dh1, tie=tok)
    return dx, dg0


def _rowwise(name, fn, ins, out_dtypes, tm=256, tie=None):
    rows, cols = ins[0].shape
    tm = tm if rows % tm == 0 else rows
    n_in = len(ins)
    ties = () if tie is None else (tie,)

    def body(*refs):
        vals = fn(*[r[...] for r in refs[:n_in]])
        for o_ref, v in zip(refs[n_in + len(ties):], vals):
            o_ref[...] = v.astype(o_ref.dtype)

    spec = pl.BlockSpec((tm, cols), lambda i: (i, 0))
    outs = _pcall(
        body, name=name, grid=(rows // tm,),
        in_specs=[spec] * n_in + [pl.BlockSpec(TOKEN_SHAPE, lambda i: (0, 0))] * len(ties),
        out_specs=[spec] * len(out_dtypes),
        out_shape=[jax.ShapeDtypeStruct((rows, cols), dt) for dt in out_dtypes],
        compiler_params=_params("parallel"),
    )(*ins, *ties)
    return outs[0] if len(out_dtypes) == 1 else outs


def _adamw(name, w, g, m, v, with_grad=False, tie=None):
    c1 = 1.0 - ADAM_B1 ** ADAM_STEP
    c2 = 1.0 - ADAM_B2 ** ADAM_STEP

    def fn(w_t, g_t, m_t, v_t):
        m_new = ADAM_B1 * m_t + (1.0 - ADAM_B1) * g_t
        v_new = ADAM_B2 * v_t + (1.0 - ADAM_B2) * (g_t * g_t)
        delta = -ADAM_LR * ((m_new / c1) / (jnp.sqrt(v_new / c2) + ADAM_EPS) + ADAM_WD * w_t)
        return (delta, m_new, v_new, g_t) if with_grad else (delta, m_new, v_new)

    return _rowwise(name, fn, (w, g, m, v), (F32,) * (4 if with_grad else 3), tie=tie)


class _Piece:
    def __init__(self, name, rows, cols, axis, src, src_row0):
        self.name, self.rows, self.cols, self.axis = name, rows, cols, axis
        self.width = (cols if axis == 1 else rows) // 4
        self.src, self.src_row0 = src, src_row0

    @property
    def full_shape(self):
        return (self.rows, self.cols)

    @property
    def half_shape(self):
        return (self.rows // 2, self.cols) if self.axis == 1 else (self.rows, self.cols // 2)

    @property
    def shard_half_shape(self):
        return (self.rows // 2, self.width) if self.axis == 1 else (self.width, self.cols // 2)

    def shard_whole(self, ref):
        n = self.rows if self.axis == 1 else self.width
        return ref.at[pl.ds(self.src_row0, n), :]

    def shard_half(self, ref, h):
        if self.axis == 1:
            return ref.at[pl.ds(self.src_row0 + h * (self.rows // 2), self.rows // 2), :]
        return ref.at[pl.ds(self.src_row0, self.width), pl.ds(h * (self.cols // 2), self.cols // 2)]

    def full_shard(self, ref, s):
        if self.axis == 1:
            return ref.at[:, pl.ds(s * self.width, self.width)]
        return ref.at[pl.ds(s * self.width, self.width), :]

    def full_shard_half(self, ref, s, h):
        if self.axis == 1:
            return ref.at[pl.ds(h * (self.rows // 2), self.rows // 2), pl.ds(s * self.width, self.width)]
        return ref.at[pl.ds(s * self.width, self.width), pl.ds(h * (self.cols // 2), self.cols // 2)]

    def full_half(self, ref, h):
        if self.axis == 1:
            return ref.at[pl.ds(h * (self.rows // 2), self.rows // 2), :]
        return ref.at[:, pl.ds(h * (self.cols // 2), self.cols // 2)]

    def full_half_rows(self, ref, h, r0, n):
        if self.axis == 1:
            return ref.at[pl.ds(h * (self.rows // 2) + r0, n), :]
        return ref.at[pl.ds(r0, n), pl.ds(h * (self.cols // 2), self.cols // 2)]

    def half_shard(self, ref, s):
        return self.full_shard(ref, s)


PIECES = (
    _Piece("even_w_in", D, EVEN_IN, 1, 0, 0),
    _Piece("even_w_out", D, D, 0, 1, 0),
    _Piece("ffn_w1_0", D, D_FF, 1, 4, 0),
    _Piece("ffn_w2_0", D_FF, D, 0, 5, 0),
    _Piece("odd_w_in", D, ODD_IN, 1, 2, 0),
    _Piece("odd_w_out", D, D, 0, 3, 0),
    _Piece("ffn_w1_1", D, D_FF, 1, 4, D),
    _Piece("ffn_w2_1", D_FF, D, 0, 5, D_FF // 4),
)
N_PIECES = len(PIECES)
FORWARD_RIDES = {"attn_fwd1": (1, 2, 3), "ffn0_down": (4, 5, 6, 7)}
JOIN_GROUPS = ((0, 1, 2, 3), (4, 5))
JOIN_RIDES_IN = "rope_bwd"
HOLD_BACK = ("ffn_w2_0", "ffn_w2_1", "odd_w_out")
N_SHARD_OPERANDS = 6
ANY = pl.BlockSpec(memory_space=pl.ANY)
MESH = pl.DeviceIdType.MESH


def _mesh_place():
    x, y, c = lax.axis_index("x"), lax.axis_index("y"), lax.axis_index("c")
    chips = [(1 - x, y), (x, 1 - y), (1 - x, 1 - y)]
    return x, y, c, chips


def _remote(src, dst, send_sem, recv_sem, dev):
    return pltpu.make_async_remote_copy(src_ref=src, dst_ref=dst, send_sem=send_sem, recv_sem=recv_sem,
                                        device_id=dev, device_id_type=MESH)


HBM = pl.BlockSpec(memory_space=pltpu.HBM)
SEM = pl.BlockSpec(memory_space=pltpu.SEMAPHORE)
SPLIT_PARAMS = pltpu.CompilerParams(has_side_effects=pltpu.SideEffectType.DATAFLOW_SIDE_EFFECTING)
CAST_TILE = 256


def _in_hbm(a):
    return pltpu.with_memory_space_constraint(a, pltpu.HBM)


def _cast_place(pc, shard_operand, chip, tie=None):
    rows, cols = (pc.rows, pc.width) if pc.axis == 1 else (pc.width, pc.cols)
    nblk = rows // CAST_TILE
    blk0 = pc.src_row0 // CAST_TILE
    ties = () if tie is None else (tie,)

    def body(chip_ref, x_ref, *rest):
        del chip_ref
        rest[-1][...] = x_ref[...].astype(BF16)

    if pc.axis == 1:
        out_map = lambda i, chip_ref: (i, chip_ref[0])
    else:
        out_map = lambda i, chip_ref: (chip_ref[0] * nblk + i, 0)
    return _pcall(
        body, name=f"cast_{pc.name}",
        grid_spec=pltpu.PrefetchScalarGridSpec(
            num_scalar_prefetch=1, grid=(nblk,),
            in_specs=[pl.BlockSpec((CAST_TILE, cols), lambda i, chip_ref: (blk0 + i, 0))]
            + [pl.BlockSpec(TOKEN_SHAPE, lambda i, chip_ref: (0, 0))] * len(ties),
            out_specs=pl.BlockSpec((CAST_TILE, cols), out_map)),
        out_shape=jax.ShapeDtypeStruct(pc.full_shape, BF16),
        compiler_params=_params("parallel"),
    )(chip, shard_operand, *ties)


def _gather_start(name, pieces, fulls):
    n = len(pieces)

    def body(*refs):
        ins = refs[:n]
        sends = refs[2 * n:3 * n]
        recvs = refs[3 * n:4 * n]
        token = refs[4 * n]
        x, y, c, chips = _mesh_place()
        s = 2 * x + y
        for i, pc in enumerate(pieces):
            win = pc.full_shard_half(ins[i], s, c)
            for k, (cx, cy) in enumerate(chips):
                _remote(win, win, sends[i].at[k], recvs[i].at[k], (cx, cy, c)).start()
        token[...] = jnp.zeros(TOKEN_SHAPE, F32)

    sems = [pltpu.SemaphoreType.DMA((3,))] * (2 * n)
    outs = _pcall(
        body, name=name,
        in_specs=[HBM] * n,
        out_specs=[HBM] * n + [SEM] * (2 * n) + [pl.BlockSpec(memory_space=pltpu.VMEM)],
        out_shape=[pltpu.HBM(pc.full_shape, BF16) for pc in pieces] + sems + [jax.ShapeDtypeStruct(TOKEN_SHAPE, F32)],
        input_output_aliases={i: i for i in range(n)},
        compiler_params=SPLIT_PARAMS,
    )(*[_in_hbm(f) for f in fulls])
    return outs[:n], outs[n:2 * n], outs[2 * n:3 * n], outs[3 * n]


def _gather_wait(pc, full, send_sems, recv_sems, after):
    def body(full_ref, send_ref, recv_ref, after_ref, out_ref):
        del after_ref, out_ref
        x, y, c, chips = _mesh_place()
        for k, (cx, cy) in enumerate(chips):
            win = pc.full_shard_half(full_ref, 2 * cx + cy, c)
            cp = _remote(win, win, send_ref.at[k], recv_ref.at[k], (cx, cy, c))
            cp.wait_send()
            cp.wait_recv()

    return _pcall(
        body, name=f"gather_wait_{pc.name}",
        in_specs=[HBM, SEM, SEM, ANY], out_specs=HBM, out_shape=pltpu.HBM(pc.full_shape, BF16),
        input_output_aliases={0: 0}, compiler_params=SPLIT_PARAMS,
    )(full, send_sems, recv_sems, after)


def _core_forward_job(pieces, fulls):
    n = len(pieces)

    def copies(ins, outs, scr):
        send_bufs, recv_bufs = scr[:n], scr[n:2 * n]
        load_sems, send_sems, recv_sems, store_sems = scr[2 * n:]
        x, y, c, chips = _mesh_place()
        loads, sends, stores = [], [], []
        for i, pc in enumerate(pieces):
            for k, (cx, cy) in enumerate(chips):
                j = 3 * i + k
                loads.append(pltpu.make_async_copy(pc.full_shard_half(ins[i], 2 * cx + cy, c), send_bufs[i].at[k],
                                                   load_sems.at[j]))
                sends.append(_remote(send_bufs[i].at[k], recv_bufs[i].at[k], send_sems.at[j], recv_sems.at[j],
                                     (x, y, 1 - c)))
                stores.append(pltpu.make_async_copy(recv_bufs[i].at[k], pc.full_shard_half(outs[i], 2 * cx + cy, 1 - c),
                                                    store_sems.at[j]))
        return loads, sends, stores

    def begin(ins, outs, scr):
        for cp in copies(ins, outs, scr)[0]:
            cp.start()

    def advance(ins, outs, scr):
        loads, sends, _ = copies(ins, outs, scr)
        for load, send in zip(loads, sends):
            load.wait()
            send.start()

    def finish(ins, outs, scr):
        _, sends, stores = copies(ins, outs, scr)
        for send, store in zip(sends, stores):
            send.wait_recv()
            store.start()
        for send, store in zip(sends, stores):
            send.wait_send()
            store.wait()

    sems = pltpu.SemaphoreType.DMA((3 * n,))
    bufs = [pltpu.VMEM((3,) + pc.shard_half_shape, BF16) for pc in pieces]
    return _SideJob(fulls, [jax.ShapeDtypeStruct(pc.full_shape, BF16) for pc in pieces],
                    bufs + bufs + [sems, sems, sems, sems], {i: i for i in range(n)}, begin, advance, finish)


def _run_job(name, job):
    def body(o_ref):
        o_ref[...] = jnp.zeros(TOKEN_SHAPE, F32)

    _ride_next_call(job)
    _pcall(body, name=name, grid=(3,), in_specs=[], out_specs=pl.BlockSpec(TOKEN_SHAPE, lambda i: (0, 0)),
           out_shape=jax.ShapeDtypeStruct(TOKEN_SHAPE, F32))()
    return job.results


def _core_forward(pieces, fulls):
    n = len(pieces)

    def body(*refs):
        ins, outs = refs[:n], refs[n:2 * n]
        send_bufs, recv_bufs = refs[2 * n:3 * n], refs[3 * n:4 * n]
        load_sems, send_sems, recv_sems, store_sems = refs[4 * n:]
        x, y, c, chips = _mesh_place()
        loads, sends, stores = [], [], []
        for i, pc in enumerate(pieces):
            for k, (cx, cy) in enumerate(chips):
                cp = pltpu.make_async_copy(pc.full_shard_half(ins[i], 2 * cx + cy, c), send_bufs[i].at[k],
                                           load_sems.at[3 * i + k])
                cp.start()
                loads.append(cp)
        for i in range(n):
            for k in range(3):
                j = 3 * i + k
                loads[j].wait()
                cp = _remote(send_bufs[i].at[k], recv_bufs[i].at[k], send_sems.at[j], recv_sems.at[j], (x, y, 1 - c))
                cp.start()
                sends.append(cp)
        for i, pc in enumerate(pieces):
            for k, (cx, cy) in enumerate(chips):
                j = 3 * i + k
                sends[j].wait_recv()
                cp = pltpu.make_async_copy(recv_bufs[i].at[k], pc.full_shard_half(outs[i], 2 * cx + cy, 1 - c),
                                           store_sems.at[j])
                cp.start()
                stores.append(cp)
        for j in range(3 * n):
            sends[j].wait_send()
            stores[j].wait()

    sems = pltpu.SemaphoreType.DMA((3 * n,))
    bufs = [pltpu.VMEM((3,) + pc.shard_half_shape, BF16) for pc in pieces]
    return _pcall(
        body, name="core_forward_" + pieces[0].name, in_specs=[ANY] * n, out_specs=[ANY] * n,
        out_shape=[jax.ShapeDtypeStruct(pc.full_shape, BF16) for pc in pieces],
        scratch_shapes=bufs + bufs + [sems, sems, sems, sems],
        input_output_aliases={i: i for i in range(n)},
        compiler_params=pltpu.CompilerParams(vmem_limit_bytes=VMEM_LIMIT),
    )(*fulls)


def _dw_tile(pc):
    if pc.axis == 1:
        tn = max(t for t in range(LANES, pc.cols + 1, LANES) if pc.cols % t == 0 and t <= 1408)
        return pc.rows // 2, tn
    return min(pc.rows, 1024), pc.cols // 2


def _mm_dw_chipsum(pc, a, b, core, tie=None):
    tm, tn = _dw_tile(pc)
    hr, hc = pc.half_shape
    tiles_r, tiles_c = hr // tm, hc // tn
    th = tiles_r * tiles_c
    ties = () if tie is None else (tie,)

    def tile_of(s, core_ref):
        mine = s >= th
        half = jnp.where(mine, core_ref[0], 1 - core_ref[0])
        local = s % th
        li, lj = local // tiles_c, local % tiles_c
        if pc.axis == 1:
            return half * tiles_r + li, lj, li, lj, mine
        return li, half * tiles_c + lj, li, lj, mine

    def body(core_ref, a_ref, b_ref, *rest):
        o_ref, send_buf, recv_buf, send_sems, recv_sems = rest[len(ties):]
        s = pl.program_id(0)
        local = s % th
        x, y, c = lax.axis_index("x"), lax.axis_index("y"), lax.axis_index("c")
        acc = _tn(a_ref[...].astype(BF16), b_ref[...].astype(BF16))

        def push(slot):
            return _remote(send_buf.at[slot], recv_buf.at[slot], send_sems.at[slot], recv_sems.at[slot], (x, y, 1 - c))

        @pl.when(s < th)
        def _():
            send_buf[local] = acc.astype(BF16)
            push(local).start()

        @pl.when(s >= th)
        def _():
            push(local).wait_recv()
            o_ref[...] = (acc + recv_buf[local].astype(F32)).astype(BF16)

        @pl.when(s == 2 * th - 1)
        def _():
            for slot in range(th):
                push(slot).wait_send()

    def a_map(s, core_ref):
        return 0, tile_of(s, core_ref)[0]

    def b_map(s, core_ref):
        return 0, tile_of(s, core_ref)[1]

    def o_map(s, core_ref):
        _, _, li, lj, mine = tile_of(s, core_ref)
        return jnp.where(mine, li, 0), jnp.where(mine, lj, 0)

    return _pcall(
        body, name=f"dw_{pc.name}",
        grid_spec=pltpu.PrefetchScalarGridSpec(
            num_scalar_prefetch=1, grid=(2 * th,),
            in_specs=[pl.BlockSpec((T, tm), a_map), pl.BlockSpec((T, tn), b_map)]
            + [pl.BlockSpec(TOKEN_SHAPE, lambda s, core_ref: (0, 0))] * len(ties),
            out_specs=pl.BlockSpec((tm, tn), o_map),
            scratch_shapes=[pltpu.VMEM((th, tm, tn), BF16), pltpu.VMEM((th, tm, tn), BF16),
                            pltpu.SemaphoreType.DMA((th,)), pltpu.SemaphoreType.DMA((th,))]),
        out_shape=jax.ShapeDtypeStruct((hr, hc), BF16),
        compiler_params=_params("arbitrary"),
    )(core, a, b, *ties)


def _scatter_start(pieces, chip_sums):
    n = len(pieces)

    def body(*refs):
        sums, lands = refs[:n], refs[n:2 * n]
        sends, recvs = refs[4 * n:5 * n], refs[5 * n:6 * n]
        token = refs[6 * n]
        x, y, c, chips = _mesh_place()
        for i, pc in enumerate(pieces):
            for k, (cx, cy) in enumerate(chips):
                _remote(pc.half_shard(sums[i], 2 * cx + cy), lands[i].at[k], sends[i].at[k], recvs[i].at[k],
                        (cx, cy, c)).start()
        token[...] = jnp.zeros(TOKEN_SHAPE, F32)

    land_shapes = [(3,) + pc.shard_half_shape for pc in pieces]
    sems = [pltpu.SemaphoreType.DMA((3,))] * (2 * n)
    outs = _pcall(
        body, name="scatter_start_" + pieces[0].name,
        in_specs=[HBM] * (2 * n), out_specs=[HBM] * (2 * n) + [SEM] * (2 * n) + [pl.BlockSpec(memory_space=pltpu.VMEM)],
        out_shape=[pltpu.HBM(pc.half_shape, BF16) for pc in pieces] + [pltpu.HBM(sh, BF16) for sh in land_shapes]
        + sems + [jax.ShapeDtypeStruct(TOKEN_SHAPE, F32)],
        input_output_aliases={i: i for i in range(2 * n)}, compiler_params=SPLIT_PARAMS,
    )(*[_in_hbm(cs) for cs in chip_sums], *[_in_hbm(lax.empty(sh, BF16)) for sh in land_shapes])
    return [(outs[i], outs[n + i], outs[2 * n + i], outs[3 * n + i]) for i in range(n)], outs[4 * n]


def _scatter_wait(pc, chip_sum, land, send_sems, recv_sems, after):
    def body(sum_ref, land_ref, send_ref, recv_ref, after_ref, sum_out, land_out):
        del after_ref, sum_out, land_out
        x, y, c, chips = _mesh_place()
        for k, (cx, cy) in enumerate(chips):
            cp = _remote(pc.half_shard(sum_ref, 2 * cx + cy), land_ref.at[k], send_ref.at[k], recv_ref.at[k], (cx, cy, c))
            cp.wait_send()
            cp.wait_recv()

    return _pcall(
        body, name=f"scatter_wait_{pc.name}",
        in_specs=[HBM, HBM, SEM, SEM, ANY], out_specs=[HBM, HBM],
        out_shape=[pltpu.HBM(pc.half_shape, BF16), pltpu.HBM((3,) + pc.shard_half_shape, BF16)],
        input_output_aliases={0: 0, 1: 1}, compiler_params=SPLIT_PARAMS,
    )(chip_sum, land, send_sems, recv_sems, after)


SHARD_OPERAND_SHAPES = ((D, EVEN_IN // 4), (D // 4, D), (D, ODD_IN // 4), (D // 4, D), (2 * D, D_FF // 4), (2 * D_FF // 4, D))


def _allsum_join_job(operands, chip_sums, lands):
    pieces = [pc for pc in PIECES if pc.src in operands]
    n = len(pieces)

    def copies(ins, outs, scr):
        sum_refs, land_refs = ins[:n], ins[n:]
        out_refs = dict(zip(operands, outs))
        in_bufs, fin_bufs, recv_bufs = scr[:n], scr[n:2 * n], scr[2 * n:3 * n]
        load_sems, send_sems, recv_sems, out_sems = scr[3 * n:]
        x, y, c, _ = _mesh_place()
        s = 2 * x + y
        loads, sends, mine, theirs = [], [], [], []
        for j, pc in enumerate(pieces):
            loads.append((pltpu.make_async_copy(land_refs[j], in_bufs[j].at[pl.ds(0, 3)], load_sems.at[2 * j]),
                          pltpu.make_async_copy(pc.half_shard(sum_refs[j], s), in_bufs[j].at[3], load_sems.at[2 * j + 1])))
            sends.append(_remote(fin_bufs[j], recv_bufs[j], send_sems.at[j], recv_sems.at[j], (x, y, 1 - c)))
            mine.append(pltpu.make_async_copy(fin_bufs[j], pc.shard_half(out_refs[pc.src], c), out_sems.at[2 * j]))
            theirs.append(pltpu.make_async_copy(recv_bufs[j], pc.shard_half(out_refs[pc.src], 1 - c), out_sems.at[2 * j + 1]))
        return loads, sends, mine, theirs, in_bufs, fin_bufs

    def begin(ins, outs, scr):
        for a, b in copies(ins, outs, scr)[0]:
            a.start()
            b.start()

    def advance(ins, outs, scr):
        loads, sends, mine, _, in_bufs, fin_bufs = copies(ins, outs, scr)
        for j in range(n):
            loads[j][0].wait()
            loads[j][1].wait()
            acc = in_bufs[j][0].astype(F32)
            for k in range(1, 4):
                acc = acc + in_bufs[j][k].astype(F32)
            fin_bufs[j][...] = acc
            mine[j].start()
            sends[j].start()

    def finish(ins, outs, scr):
        _, sends, mine, theirs, _, _ = copies(ins, outs, scr)
        for j in range(n):
            sends[j].wait_recv()
            theirs[j].start()
        for j in range(n):
            sends[j].wait_send()
            mine[j].wait()
            theirs[j].wait()

    halves = [pc.shard_half_shape for pc in pieces]
    scratch = ([pltpu.VMEM((4,) + sh, BF16) for sh in halves] + [pltpu.VMEM(sh, F32) for sh in halves] * 2
               + [pltpu.SemaphoreType.DMA((2 * n,)), pltpu.SemaphoreType.DMA((n,)), pltpu.SemaphoreType.DMA((n,)),
                  pltpu.SemaphoreType.DMA((2 * n,))])
    return _SideJob(list(chip_sums) + list(lands), [jax.ShapeDtypeStruct(SHARD_OPERAND_SHAPES[o], F32) for o in operands],
                    scratch, {}, begin, advance, finish)


PEER_FLIPS = tuple((a, b, e) for a in (0, 1) for b in (0, 1) for e in (0, 1) if (a, b, e) != (0, 0, 0))


def _peers():
    x, y, c = lax.axis_index("x"), lax.axis_index("y"), lax.axis_index("c")
    me = 4 * x + 2 * y + c
    out = []
    for a, b, e in PEER_FLIPS:
        px, py, pc = (1 - x if a else x), (1 - y if b else y), (1 - c if e else c)
        out.append(((px, py, pc), 4 * px + 2 * py + pc))
    return me, out


def _exchange8_start(name, blk):
    m = blk.shape[0]

    def body(blk_ref, land_ref, blk_out, land_out, sends, recvs, token):
        del blk_out, land_out
        me, peers = _peers()
        for k, (dev, _) in enumerate(peers):
            _remote(blk_ref, land_ref.at[me], sends.at[k], recvs.at[k], dev).start()
        token[...] = jnp.zeros(TOKEN_SHAPE, F32)

    sems = pltpu.SemaphoreType.DMA((7,))
    return _pcall(
        body, name=name,
        in_specs=[HBM, HBM], out_specs=[HBM, HBM, SEM, SEM, pl.BlockSpec(memory_space=pltpu.VMEM)],
        out_shape=[pltpu.HBM((m, LANES), F32), pltpu.HBM((8, m, LANES), F32), sems, sems,
                   jax.ShapeDtypeStruct(TOKEN_SHAPE, F32)],
        input_output_aliases={0: 0, 1: 1}, compiler_params=SPLIT_PARAMS,
    )(_in_hbm(blk), _in_hbm(lax.empty((8, m, LANES), F32)))


def _exchange8_wait(name, blk, land, send_sems, recv_sems, after):
    def body(blk_ref, land_ref, send_ref, recv_ref, after_ref, blk_out, land_out):
        del after_ref, blk_out, land_out
        _, peers = _peers()
        for k, (dev, slot) in enumerate(peers):
            cp = _remote(blk_ref, land_ref.at[slot], send_ref.at[k], recv_ref.at[k], dev)
            cp.wait_send()
            cp.wait_recv()

    m = blk.shape[0]
    return _pcall(
        body, name=name,
        in_specs=[HBM, HBM, SEM, SEM, ANY], out_specs=[HBM, HBM],
        out_shape=[pltpu.HBM((m, LANES), F32), pltpu.HBM((8, m, LANES), F32)],
        input_output_aliases={0: 0, 1: 1}, compiler_params=SPLIT_PARAMS,
    )(blk, land, send_sems, recv_sems, after)


def _collect8(name, blk, land, with_sum):
    m = blk.shape[0]

    def body(blk_ref, land_ref, out_ref, *scratch):
        sems = scratch[-1]
        dst = scratch[0] if with_sum else out_ref
        me, peers = _peers()
        copies = [pltpu.make_async_copy(blk_ref, dst.at[me], sems.at[7])]
        for k, (_, slot) in enumerate(peers):
            copies.append(pltpu.make_async_copy(land_ref.at[slot], dst.at[slot], sems.at[k]))
        for cp in copies:
            cp.start()
        for cp in copies:
            cp.wait()
        if with_sum:
            acc = dst[0]
            for dev in range(1, 8):
                acc = acc + dst[dev]
            out_ref[...] = acc

    all_shape = (8, m, LANES)
    return _pcall(
        body, name=name, in_specs=[ANY, ANY], out_specs=pl.BlockSpec(memory_space=pltpu.VMEM),
        out_shape=jax.ShapeDtypeStruct((m, LANES) if with_sum else all_shape, F32),
        scratch_shapes=([pltpu.VMEM(all_shape, F32)] if with_sum else []) + [pltpu.SemaphoreType.DMA((8,))],
    )(blk, land)


def _pack(arrays, row_counts):
    rows = []
    for a, n in zip(arrays, row_counts):
        flat = a.reshape(-1, LANES)
        rows.append(jnp.pad(flat, ((0, n - flat.shape[0]), (0, 0))))
    return jnp.concatenate(rows, axis=0)


def _unpack(buf, shapes, row_counts):
    out, r0 = [], 0
    for sh, n in zip(shapes, row_counts):
        size = 1
        for dim in sh:
            size *= dim
        out.append(buf[r0:r0 + size // LANES].reshape(sh))
        r0 += n
    return out


REPL_NAMES = ("norm_mix_g", "norm_ffn_g", "even_conv_b", "even_ln_g", "even_ln_b", "odd_sg_w", "odd_sg_b", "final_g")
REPL_SHAPES = ((2, D), (2, D), (1, 512), (1, 512), (1, 512), (1, SG_GROUPS, CHUNK, CHUNK), (1, SG_GROUPS, CHUNK), (D,))
REPL_ROWS = (16, 16, 8, 8, 8, 512, 8, 8)
SHARDED_NAMES = ("even_conv_k", "odd_conv_k", "odd_ln_g", "odd_ln_b")
SHARDED_SHARD_SHAPES = ((1, CONV_W, LANES), (1, SCONV_W, LANES), (1, LANES), (1, LANES))
SHARDED_SHARD_ROWS = (32, 8, 8, 8)
SHARDED_FULL_SHAPES = ((CONV_W, 512), (SCONV_W, 512), (1, 512), (1, 512))
SHARDED_FULL_ROWS = (128, 16, 8, 8)


def kernel(x, norm_mix_g, norm_ffn_g, even_w_in, even_conv_k, even_conv_b, even_ln_g, even_ln_b, even_w_out, odd_w_in, odd_conv_k, odd_ln_g, odd_ln_b, odd_sg_w, odd_sg_b, odd_w_out, ffn_w1, ffn_w2, final_g, loss_target, m_norm_mix_g, m_norm_ffn_g, m_even_w_in, m_even_conv_k, m_even_conv_b, m_even_ln_g, m_even_ln_b, m_even_w_out, m_odd_w_in, m_odd_conv_k, m_odd_ln_g, m_odd_ln_b, m_odd_sg_w, m_odd_sg_b, m_odd_w_out, m_ffn_w1, m_ffn_w2, m_final_g, v_norm_mix_g, v_norm_ffn_g, v_even_w_in, v_even_conv_k, v_even_conv_b, v_even_ln_g, v_even_ln_b, v_even_w_out, v_odd_w_in, v_odd_conv_k, v_odd_ln_g, v_odd_ln_b, v_odd_sg_w, v_odd_sg_b, v_odd_w_out, v_ffn_w1, v_ffn_w2, v_final_g):
    names = ("norm_mix_g", "norm_ffn_g", "even_w_in", "even_conv_k", "even_conv_b", "even_ln_g", "even_ln_b", "even_w_out",
             "odd_w_in", "odd_conv_k", "odd_ln_g", "odd_ln_b", "odd_sg_w", "odd_sg_b", "odd_w_out", "ffn_w1", "ffn_w2", "final_g")
    w = dict(zip(names, (norm_mix_g, norm_ffn_g, even_w_in, even_conv_k, even_conv_b, even_ln_g, even_ln_b, even_w_out,
                         odd_w_in, odd_conv_k, odd_ln_g, odd_ln_b, odd_sg_w, odd_sg_b, odd_w_out, ffn_w1, ffn_w2, final_g)))
    mom = dict(zip(names, (m_norm_mix_g, m_norm_ffn_g, m_even_w_in, m_even_conv_k, m_even_conv_b, m_even_ln_g, m_even_ln_b,
                           m_even_w_out, m_odd_w_in, m_odd_conv_k, m_odd_ln_g, m_odd_ln_b, m_odd_sg_w, m_odd_sg_b, m_odd_w_out,
                           m_ffn_w1, m_ffn_w2, m_final_g)))
    vel = dict(zip(names, (v_norm_mix_g, v_norm_ffn_g, v_even_w_in, v_even_conv_k, v_even_conv_b, v_even_ln_g, v_even_ln_b,
                           v_even_w_out, v_odd_w_in, v_odd_conv_k, v_odd_ln_g, v_odd_ln_b, v_odd_sg_w, v_odd_sg_b, v_odd_w_out,
                           v_ffn_w1, v_ffn_w2, v_final_g)))
    big_names = ("even_w_in", "even_w_out", "odd_w_in", "odd_w_out", "ffn_w1", "ffn_w2")
    chip = 2 * lax.axis_index("x") + lax.axis_index("y")

    def shard2d(t, name):
        return t[name].reshape(SHARD_OPERAND_SHAPES[big_names.index(name)])

    chip_op = jnp.reshape(chip, (1,)).astype(jnp.int32)
    small_pack = _pack([w[n] for n in SHARDED_NAMES], SHARDED_SHARD_ROWS)
    small_blk, small_land, small_send, small_recv, small_token = _exchange8_start("gather_small_start", small_pack)
    first = _cast_place(PIECES[0], shard2d(w, big_names[PIECES[0].src]), chip_op, tie=small_token)
    fly0, send0, recv0, token = _gather_start("gather_start_first", PIECES[:1], [first])
    placed = [_cast_place(pc, shard2d(w, big_names[pc.src]), chip_op, tie=token) for pc in PIECES[1:]]
    fly1, send1, recv1, all_started = _gather_start("gather_start_rest", PIECES[1:], placed)
    flying, gather_send, gather_recv = fly0 + fly1, send0 + send1, recv0 + recv1
    ready = {}

    names_in_order = [pc.name for pc in PIECES]

    riding = {}

    def weight(name, after):
        if name in riding:
            job, k = riding.pop(name)
            ready[name] = job.results[k]
        if name not in ready:
            landed = _gather_wait(PIECES[0], flying[0], gather_send[0], gather_recv[0], all_started)
            ready[name], = _core_forward(PIECES[:1], [landed])
        return ready[name]

    def before(call, after):
        if call in FORWARD_RIDES:
            group = FORWARD_RIDES[call]
            landed = [_gather_wait(PIECES[j], flying[j], gather_send[j], gather_recv[j], after) for j in group]
            job = _core_forward_job([PIECES[j] for j in group], landed)
            riding.update((PIECES[j].name, (job, k)) for k, j in enumerate(group))
            _ride_next_call(job)
        elif call == JOIN_RIDES_IN:
            early_join.append(join_job(JOIN_GROUPS[1], after))
            _ride_next_call(early_join[0])

    early_join = []

    def join_job(operands, after):
        pieces = [pc for pc in PIECES if pc.src in operands]
        done = {}
        for entry in list(scattering):
            if entry[0] in pieces:
                done[entry[0].name] = _scatter_wait(*entry, after)
                scattering.remove(entry)
        return _allsum_join_job(operands, [done[pc.name][0] for pc in pieces], [done[pc.name][1] for pc in pieces])

    scattering = []
    held = []

    core_op = jnp.reshape(lax.axis_index("c"), (1,)).astype(jnp.int32)

    def emit(name, a, b, tie=None):
        pc = PIECES[names_in_order.index(name)]
        held.append((pc, _mm_dw_chipsum(pc, a, b, core_op, tie)))
        if name in HOLD_BACK:
            return None
        pieces = [pc for pc, _ in held]
        started, token = _scatter_start(pieces, [chip_sum for _, chip_sum in held])
        scattering.extend((pc,) + tuple(st) for pc, st in zip(pieces, started))
        held.clear()
        return token

    full = {}
    small_blk, small_land = _exchange8_wait("gather_small_wait", small_blk, small_land, small_send, small_recv, all_started)
    gathered = _collect8("gather_small_collect", small_blk, small_land, False)
    gathered = gathered.reshape(4, 2, sum(SHARDED_SHARD_ROWS), LANES)[:, 0]
    r0 = 0
    for n, sh, rows, full_sh in zip(SHARDED_NAMES, SHARDED_SHARD_SHAPES, SHARDED_SHARD_ROWS, SHARDED_FULL_SHAPES):
        per_chip = gathered[:, r0:r0 + rows].reshape(4, -1)[:, :full_sh[0] * LANES].reshape(4, full_sh[0], LANES)
        full[n] = jnp.transpose(per_chip, (1, 0, 2)).reshape(full_sh)
        r0 += rows
    p = dict(full)
    p.update(norm_mix_g0=norm_mix_g[0:1], norm_mix_g1=norm_mix_g[1:2], norm_ffn_g0=norm_ffn_g[0:1], norm_ffn_g1=norm_ffn_g[1:2],
             even_conv_b=even_conv_b, even_ln_g=even_ln_g, even_ln_b=even_ln_b,
             odd_sg_w=odd_sg_w[0], odd_sg_bt=odd_sg_b[0].T, final_g=final_g[None, :])

    small = {}

    def emit_small(loss_row, g):
        parts = [loss_row, g["norm_mix_g1"], g["norm_ffn_g0"], g["norm_ffn_g1"], g["even_conv_b"], g["even_ln_g"],
                 g["even_ln_b"], g["odd_sg_w"], g["odd_sg_bt"].T, g["final_g"],
                 g["even_conv_k"], g["odd_conv_k"], g["odd_ln_g"], g["odd_ln_b"]]
        pack = _pack(parts, (8, 8, 8, 8) + REPL_ROWS[2:] + SHARDED_FULL_ROWS)
        small["blk"], small["land"], small["send"], small["recv"], token = _exchange8_start("allreduce_small_start", pack)
        return token

    dx, dg0 = _local_step(x[0], loss_target[0], p, weight, emit, emit_small, before)
    last_blk, last_land, last_send, last_recv, grad_token = _exchange8_start("allreduce_last_start", _pack([dg0], (8,)))

    big_grads = dict(zip((big_names[o] for o in JOIN_GROUPS[1]), early_join[0].results))
    delta, new_m, new_v, grads_big = {}, {}, {}, {}

    def adamw_big(n):
        d2, m2, v2, g2 = _adamw(f"adamw_{n}", shard2d(w, n), big_grads[n], shard2d(mom, n), shard2d(vel, n), True,
                                tie=grad_token)
        delta[n], new_m[n], new_v[n], grads_big[n] = (t.reshape(w[n].shape) for t in (d2, m2, v2, g2))

    for o in JOIN_GROUPS[1]:
        adamw_big(big_names[o])
    late_join = join_job(JOIN_GROUPS[0], new_v[big_names[JOIN_GROUPS[1][-1]]])
    big_grads.update(zip((big_names[o] for o in JOIN_GROUPS[0]), _run_job("allsum_join_late", late_join)))
    for o in JOIN_GROUPS[0]:
        adamw_big(big_names[o])

    joined_last = big_grads[big_names[JOIN_GROUPS[0][-1]]]
    grad_blk, grad_land = _exchange8_wait("allreduce_small_wait", small["blk"], small["land"], small["send"],
                                          small["recv"], joined_last)
    grad_sum = _collect8("allreduce_small_sum", grad_blk, grad_land, True)
    last_blk, last_land = _exchange8_wait("allreduce_last_wait", last_blk, last_land, last_send, last_recv, joined_last)
    dg0_sum = _collect8("allreduce_last_sum", last_blk, last_land, True)
    loss = grad_sum[0, 0]
    shapes = ((1, D),) + REPL_SHAPES[1:] + SHARDED_FULL_SHAPES
    parts = _unpack(grad_sum[8:], shapes, (8,) + REPL_ROWS[1:] + SHARDED_FULL_ROWS)
    grads = dict(zip(REPL_NAMES, parts[:len(REPL_NAMES)]))
    grads["norm_mix_g"] = (jnp.pad(dg0_sum[:8].reshape(1, D), ((0, 1), (0, 0)))
                           + jnp.pad(grads["norm_mix_g"], ((1, 0), (0, 0))))
    for n, full_g, sh in zip(SHARDED_NAMES, parts[len(REPL_NAMES):], SHARDED_SHARD_SHAPES):
        grads[n] = lax.dynamic_slice_in_dim(full_g, chip * LANES, LANES, axis=1).reshape(sh)

    grads.update(grads_big)
    for tag, group, rows, shapes in (("repl", REPL_NAMES, REPL_ROWS, [w[n].shape for n in REPL_NAMES]),
                                     ("sharded", SHARDED_NAMES, SHARDED_SHARD_ROWS, SHARDED_SHARD_SHAPES)):
        packs = [_pack([t[n] for n in group], rows) for t in (w, grads, mom, vel)]
        outs = _adamw(f"adamw_{tag}", *packs)
        for res, o in zip((delta, new_m, new_v), outs):
            res.update(zip(group, _unpack(o, shapes, rows)))

    out = [loss, dx[None]]
    for res in (grads, delta, new_m, new_v):
        out.extend(res[n] for n in names)
    return tuple(out)
```

```python
import functools

import jax
import jax.numpy as jnp
from jax import lax
from jax.experimental import pallas as pl
from jax.experimental.pallas import tpu as pltpu

F32 = jnp.float32
BF16 = jnp.bfloat16

T = 2048
D = 1024
CONV_CH = 512
CONV_W = 31
HEAD_DIM = 64
ATT_W = 1536
EVEN_IN = 5632
ODD_IN = 2560
SCONV_W = 3
SG_GROUPS = 4
CHUNK = 128
D_FF = 4096
EPS = 1e-6
DILATIONS = (1, 4, 16)
BAND = 128
SCALE = HEAD_DIM ** -0.5
NEG = -1e30

ADAM_LR = 0.001
ADAM_B1 = 0.9
ADAM_B2 = 0.999
ADAM_EPS = 1e-08
ADAM_WD = 0.01
ADAM_STEP = 10

V7X_VMEM_BYTES = 64 * 2 ** 20
VMEM_LIMIT = V7X_VMEM_BYTES - 8 * 2 ** 20
LANES = 128
TOKEN_SHAPE = (8, LANES)


class _SideJob:
    def __init__(self, inputs, out_shape, scratch_shapes, aliases, begin, advance, finish):
        self.inputs, self.out_shape, self.scratch_shapes = list(inputs), list(out_shape), list(scratch_shapes)
        self.aliases, self.begin, self.advance, self.finish = dict(aliases), begin, advance, finish
        self.results = None


_PENDING_JOBS = []


def _ride_next_call(job):
    _PENDING_JOBS.append(job)


def _pcall(body, **kw):
    if not _PENDING_JOBS or "grid" not in kw:
        return pl.pallas_call(body, **kw)
    job = _PENDING_JOBS.pop()
    as_list = lambda v: list(v) if isinstance(v, (list, tuple)) else [v]
    single_out = not isinstance(kw["out_shape"], (list, tuple))
    in_specs, out_specs, out_shape = as_list(kw["in_specs"]), as_list(kw["out_specs"]), as_list(kw["out_shape"])
    scratch = list(kw.get("scratch_shapes", ()))
    grid = kw["grid"]
    n_steps = 1
    for extent in grid:
        n_steps *= extent
    assert n_steps >= 3
    n_in, n_out, n_scr = len(in_specs), len(out_specs), len(scratch)
    j_in, j_out = len(job.inputs), len(job.out_shape)
    any_spec = pl.BlockSpec(memory_space=pl.ANY)

    def hosted(*refs):
        ins, j_ins = refs[:n_in], refs[n_in:n_in + j_in]
        outs = refs[n_in + j_in:n_in + j_in + n_out]
        j_outs = refs[n_in + j_in + n_out:n_in + j_in + n_out + j_out]
        scr = refs[n_in + j_in + n_out + j_out:n_in + j_in + n_out + j_out + n_scr]
        j_scr = refs[n_in + j_in + n_out + j_out + n_scr:]
        step = pl.program_id(0)
        for axis in range(1, len(grid)):
            step = step * grid[axis] + pl.program_id(axis)

        @pl.when(step == 0)
        def _():
            job.begin(j_ins, j_outs, j_scr)

        @pl.when(step == 1)
        def _():
            job.advance(j_ins, j_outs, j_scr)

        body(*ins, *outs, *scr)

        @pl.when(step == n_steps - 1)
        def _():
            job.finish(j_ins, j_outs, j_scr)

    aliases = dict(kw.get("input_output_aliases", {}))
    aliases.update({n_in + a: n_out + b for a, b in job.aliases.items()})
    call = pl.pallas_call(
        hosted, name=kw["name"], grid=grid,
        in_specs=in_specs + [any_spec] * j_in, out_specs=out_specs + [any_spec] * j_out,
        out_shape=out_shape + job.out_shape, scratch_shapes=scratch + job.scratch_shapes,
        input_output_aliases=aliases,
        compiler_params=pltpu.CompilerParams(dimension_semantics=("arbitrary",) * len(grid), vmem_limit_bytes=VMEM_LIMIT))

    def run(*args):
        res = call(*args, *job.inputs)
        job.results = list(res[n_out:])
        return res[0] if single_out else list(res[:n_out])

    return run


def _params(*sem):
    return pltpu.CompilerParams(dimension_semantics=sem, vmem_limit_bytes=VMEM_LIMIT)


def _dot(a, b, dims):
    return lax.dot_general(a, b, (dims, ((), ())), preferred_element_type=F32)


def _nn(a, b):
    return _dot(a, b, ((1,), (0,)))


def _nt(a, b):
    return _dot(a, b, ((1,), (1,)))


def _tn(a, b):
    return _dot(a, b, ((0,), (0,)))


def _sigmoid(x):
    return 1.0 / (1.0 + jnp.exp(-x))


MM_VMEM_BUDGET = 40 * 2 ** 20


def _mm_tiles(mode, m, n, k, a_bytes, b_bytes, extra_bytes, out_bytes):
    def divisors(total, unit):
        return [t for t in range(unit, total + 1, unit) if total % t == 0]

    best = None
    for tm in divisors(m, LANES if mode == "tn" else 8):
        for tn in divisors(n, LANES):
            blocks = tm * k * a_bytes + tn * k * b_bytes + tm * tn * (extra_bytes + out_bytes)
            casts = (tm * k * 2 if a_bytes == 4 else 0) + (tn * k * 2 if b_bytes == 4 else 0)
            if 2 * blocks + casts + tm * tn * 4 > MM_VMEM_BUDGET:
                continue
            key = ((m // tm) * (n // tn), (m // tm) * n * k * b_bytes, abs(tm - tn))
            if best is None or key < best[0]:
                best = (key, tm, tn)
    return best[1], best[2]


def _mm(name, mode, a, b, m, n, k, out_dtypes, *, b_off=0, extras=(), epi=None, tie=None):
    tm, tn = _mm_tiles(mode, m, n, k, a.dtype.itemsize, b.dtype.itemsize, sum(e.dtype.itemsize for e in extras),
                       sum(jnp.dtype(dt).itemsize for dt in out_dtypes))
    assert b_off % tn == 0
    b_off //= tn
    if mode == "nn":
        a_spec = pl.BlockSpec((tm, k), lambda i, j: (i, 0))
        b_spec = pl.BlockSpec((k, tn), lambda i, j: (0, j + b_off))
        dims = ((1,), (0,))
    elif mode == "nt":
        a_spec = pl.BlockSpec((tm, k), lambda i, j: (i, 0))
        b_spec = pl.BlockSpec((tn, k), lambda i, j: (j, 0))
        dims = ((1,), (1,))
    else:
        a_spec = pl.BlockSpec((k, tm), lambda i, j: (0, i))
        b_spec = pl.BlockSpec((k, tn), lambda i, j: (0, j))
        dims = ((0,), (0,))
    o_spec = pl.BlockSpec((tm, tn), lambda i, j: (i, j))
    n_extra = len(extras)
    ties = () if tie is None else (tie,)

    def body(a_ref, b_ref, *rest):
        rest = rest[len(ties):]
        acc = _dot(a_ref[...].astype(BF16), b_ref[...].astype(BF16), dims)
        vals = epi(acc, *[e[...] for e in rest[:n_extra]]) if epi is not None else (acc,)
        for o_ref, v in zip(rest[n_extra:], vals):
            o_ref[...] = v.astype(o_ref.dtype)

    outs = _pcall(
        body, name=name, grid=(m // tm, n // tn),
        in_specs=[a_spec, b_spec] + [pl.BlockSpec(TOKEN_SHAPE, lambda i, j: (0, 0))] * len(ties) + [o_spec] * n_extra,
        out_specs=[o_spec] * len(out_dtypes),
        out_shape=[jax.ShapeDtypeStruct((m, n), dt) for dt in out_dtypes],
        compiler_params=_params("parallel", "parallel"),
    )(a, b, *ties, *extras)
    return outs[0] if len(out_dtypes) == 1 else outs


def _row_tile(k, a_bytes, n_row_blocks):
    for tm in (1024, 512, 256, 128):
        if 2 * (tm * k * a_bytes + D * k * 2 + n_row_blocks * tm * D * 4) + tm * D * 4 <= MM_VMEM_BUDGET + 4 * 2 ** 20:
            return tm
    raise ValueError("no row tile fits")


FFN0_DOWN_TILE = 256


def _mm_out_norm(name, a, b, k, res, g_next, tm=None):
    tm = tm or _row_tile(k, a.dtype.itemsize, 3)

    def body(a_ref, b_ref, r_ref, g_ref, h_ref, hn_ref):
        h = _nn(a_ref[...].astype(BF16), b_ref[...]) + r_ref[...]
        h_ref[...] = h
        r = lax.rsqrt(jnp.mean(h * h, axis=-1, keepdims=True) + EPS)
        hn_ref[...] = ((h * r) * g_ref[...]).astype(BF16)

    row = pl.BlockSpec((tm, D), lambda i: (i, 0))
    return _pcall(
        body, name=name, grid=(T // tm,),
        in_specs=[pl.BlockSpec((tm, k), lambda i: (i, 0)), pl.BlockSpec((k, D), lambda i: (0, 0)), row,
                  pl.BlockSpec((1, D), lambda i: (0, 0))],
        out_specs=[row, row],
        out_shape=[jax.ShapeDtypeStruct((T, D), F32), jax.ShapeDtypeStruct((T, D), BF16)],
        compiler_params=_params("parallel"),
    )(a, b, res, g_next)


def _mm_out_loss(name, a, b, k, res, g, target):
    tm = _row_tile(k, a.dtype.itemsize, 3)

    def body(a_ref, b_ref, r_ref, g_ref, t_ref, dh_ref, dg_ref, loss_ref):
        x = _nn(a_ref[...].astype(BF16), b_ref[...]) + r_ref[...]
        r = lax.rsqrt(jnp.mean(x * x, axis=-1, keepdims=True) + EPS)
        nrm = x * r
        gain = g_ref[...]
        err = nrm * gain - t_ref[...]
        dy = err * (1.0 / D)
        dn = dy * gain
        dh_ref[...] = r * (dn - nrm * jnp.mean(dn * nrm, axis=-1, keepdims=True))

        @pl.when(pl.program_id(0) == 0)
        def _():
            dg_ref[...] = jnp.zeros_like(dg_ref)
            loss_ref[...] = jnp.zeros_like(loss_ref)

        dg_ref[...] += jnp.sum(dy * nrm, axis=0, keepdims=True)
        part = jnp.sum(jnp.sum(err * err, axis=1, keepdims=True), axis=0, keepdims=True) * (0.5 / D)
        loss_ref[...] += jnp.broadcast_to(part, (1, LANES))

    row = pl.BlockSpec((tm, D), lambda i: (i, 0))
    vec = pl.BlockSpec((1, D), lambda i: (0, 0))
    return _pcall(
        body, name=name, grid=(T // tm,),
        in_specs=[pl.BlockSpec((tm, k), lambda i: (i, 0)), pl.BlockSpec((k, D), lambda i: (0, 0)), row, vec, row],
        out_specs=[row, vec, pl.BlockSpec((1, LANES), lambda i: (0, 0))],
        out_shape=[jax.ShapeDtypeStruct((T, D), F32), jax.ShapeDtypeStruct((1, D), F32),
                   jax.ShapeDtypeStruct((1, LANES), F32)],
        compiler_params=_params("arbitrary"),
    )(a, b, res, g, target)


def _mm_dx_norm(name, dz, w, k, h, g, dres, tie=None):
    tm = _row_tile(k, dz.dtype.itemsize, 3)
    ties = () if tie is None else (tie,)

    def body(a_ref, b_ref, *rest):
        h_ref, g_ref, r_ref, dh_ref, dg_ref = rest[len(ties):]
        dy = _nt(a_ref[...].astype(BF16), b_ref[...])
        x = h_ref[...]
        r = lax.rsqrt(jnp.mean(x * x, axis=-1, keepdims=True) + EPS)
        nrm = x * r
        dn = dy * g_ref[...]
        dh_ref[...] = r_ref[...] + r * (dn - nrm * jnp.mean(dn * nrm, axis=-1, keepdims=True))

        @pl.when(pl.program_id(0) == 0)
        def _():
            dg_ref[...] = jnp.zeros_like(dg_ref)

        dg_ref[...] += jnp.sum(dy * nrm, axis=0, keepdims=True)

    row = pl.BlockSpec((tm, D), lambda i: (i, 0))
    vec = pl.BlockSpec((1, D), lambda i: (0, 0))
    return _pcall(
        body, name=name, grid=(T // tm,),
        in_specs=[pl.BlockSpec((tm, k), lambda i: (i, 0)), pl.BlockSpec((D, k), lambda i: (0, 0))]
        + [pl.BlockSpec(TOKEN_SHAPE, lambda i: (0, 0))] * len(ties) + [row, vec, row],
        out_specs=[row, vec],
        out_shape=[jax.ShapeDtypeStruct((T, D), F32), jax.ShapeDtypeStruct((1, D), F32)],
        compiler_params=_params("arbitrary"),
    )(dz, w, *ties, h, g, dres)


def _rms_fwd(name, h, g, tm=512):
    def body(h_ref, g_ref, o_ref):
        x = h_ref[...]
        r = lax.rsqrt(jnp.mean(x * x, axis=-1, keepdims=True) + EPS)
        o_ref[...] = ((x * r) * g_ref[...]).astype(BF16)

    return _pcall(
        body, name=name, grid=(T // tm,),
        in_specs=[pl.BlockSpec((tm, D), lambda i: (i, 0)), pl.BlockSpec((1, D), lambda i: (0, 0))],
        out_specs=pl.BlockSpec((tm, D), lambda i: (i, 0)),
        out_shape=jax.ShapeDtypeStruct((T, D), BF16),
        compiler_params=_params("parallel"),
    )(h, g)


CONV_TILE = 256
CONV_HALO = 32


def _glu(z):
    return z[:, :CONV_CH] * _sigmoid(z[:, CONV_CH:])


SUBLANES = 8


def _sublane_shifts(win):
    n = win.shape[0]
    return [win] + [win[r:r + n - SUBLANES, :] for r in range(1, SUBLANES)]


def _rows_from(shifts, off, n):
    q, r = divmod(off, SUBLANES)
    return shifts[r][q * SUBLANES:q * SUBLANES + n, :]


def _econv_fwd(zc, conv_k, conv_b, ln_g, ln_b):
    R, H = CONV_TILE, CONV_HALO

    def body(z_ref, zh_ref, k_ref, b_ref, g_ref, be_ref, cv_ref, cat_ref):
        i = pl.program_id(0)
        glu = _glu(z_ref[...])
        halo = _glu(zh_ref[...]) * (i > 0).astype(F32)
        win = _sublane_shifts(jnp.concatenate([halo, glu], axis=0))
        acc = jnp.zeros((R, CONV_CH), F32) + b_ref[...]
        for j in range(CONV_W):
            acc = acc + k_ref[j:j + 1, :] * _rows_from(win, H - (CONV_W - 1) + j, R)
        cv_ref[...] = acc
        mu = jnp.mean(acc, axis=-1, keepdims=True)
        xc = acc - mu
        rstd = lax.rsqrt(jnp.mean(xc * xc, axis=-1, keepdims=True) + EPS)
        ln = xc * rstd * g_ref[...] + be_ref[...]
        cat_ref[...] = (ln * _sigmoid(ln)).astype(BF16)

    vec = pl.BlockSpec((1, CONV_CH), lambda i: (0, 0))
    return _pcall(
        body, name="econv_fwd", grid=(T // R,),
        in_specs=[pl.BlockSpec((R, 2 * CONV_CH), lambda i: (i, 0)),
                  pl.BlockSpec((H, 2 * CONV_CH), lambda i: (jnp.maximum(i * (R // H) - 1, 0), 0)),
                  pl.BlockSpec((CONV_W, CONV_CH), lambda i: (0, 0)), vec, vec, vec],
        out_specs=[pl.BlockSpec((R, CONV_CH), lambda i: (i, 0)), pl.BlockSpec((R, CONV_CH), lambda i: (i, 0))],
        out_shape=[jax.ShapeDtypeStruct((T, CONV_CH), F32), jax.ShapeDtypeStruct((T, D), BF16)],
        compiler_params=_params("parallel"),
    )(zc, zc, conv_k, conv_b, ln_g, ln_b)


def _econv_bwd_ln(cv, dcat, ln_g, ln_b):
    R = CONV_TILE

    def body(cv_ref, d_ref, g_ref, be_ref, dcv_ref, dg_ref, dbe_ref, dcb_ref):
        cv_t = cv_ref[...]
        mu = jnp.mean(cv_t, axis=-1, keepdims=True)
        xc = cv_t - mu
        rstd = lax.rsqrt(jnp.mean(xc * xc, axis=-1, keepdims=True) + EPS)
        xh = xc * rstd
        ln = xh * g_ref[...] + be_ref[...]
        sg = _sigmoid(ln)
        dln = d_ref[...] * (sg * (1.0 + ln * (1.0 - sg)))
        dxh = dln * g_ref[...]
        dcv = rstd * (dxh - jnp.mean(dxh, axis=-1, keepdims=True) - xh * jnp.mean(dxh * xh, axis=-1, keepdims=True))
        dcv_ref[...] = dcv

        @pl.when(pl.program_id(0) == 0)
        def _():
            dg_ref[...] = jnp.zeros_like(dg_ref)
            dbe_ref[...] = jnp.zeros_like(dbe_ref)
            dcb_ref[...] = jnp.zeros_like(dcb_ref)

        dg_ref[...] += jnp.sum(dln * xh, axis=0, keepdims=True)
        dbe_ref[...] += jnp.sum(dln, axis=0, keepdims=True)
        dcb_ref[...] += jnp.sum(dcv, axis=0, keepdims=True)

    vec = pl.BlockSpec((1, CONV_CH), lambda i: (0, 0))
    row = pl.BlockSpec((R, CONV_CH), lambda i: (i, 0))
    vshape = jax.ShapeDtypeStruct((1, CONV_CH), F32)
    return _pcall(
        body, name="econv_bwd_ln", grid=(T // R,),
        in_specs=[row, row, vec, vec], out_specs=[row, vec, vec, vec],
        out_shape=[jax.ShapeDtypeStruct((T, CONV_CH), F32), vshape, vshape, vshape],
        compiler_params=_params("arbitrary"),
    )(cv, dcat, ln_g, ln_b)


def _econv_bwd_conv(dcv, zc, conv_k):
    R, H = CONV_TILE, CONV_HALO
    last = T // R - 1

    def body(d_ref, dn_ref, z_ref, zh_ref, k_ref, dz_ref, dk_ref):
        i = pl.program_id(0)
        z = z_ref[...]
        a_lin = z[:, :CONV_CH]
        sg = _sigmoid(z[:, CONV_CH:])
        glu = a_lin * sg
        halo = _glu(zh_ref[...]) * (i > 0).astype(F32)
        win = _sublane_shifts(jnp.concatenate([halo, glu], axis=0))
        dcv_t = d_ref[...]
        nxt = dn_ref[...] * (i < last).astype(F32)
        winb = _sublane_shifts(jnp.concatenate([dcv_t, nxt], axis=0))

        @pl.when(i == 0)
        def _():
            dk_ref[...] = jnp.zeros_like(dk_ref)

        dglu = jnp.zeros((R, CONV_CH), F32)
        for j in range(CONV_W):
            dk_ref[j:j + 1, :] += jnp.sum(dcv_t * _rows_from(win, H - (CONV_W - 1) + j, R), axis=0, keepdims=True)
            dglu = dglu + k_ref[j:j + 1, :] * _rows_from(winb, CONV_W - 1 - j, R)
        dz_ref[...] = jnp.concatenate([dglu * sg, dglu * a_lin * sg * (1.0 - sg)], axis=1).astype(BF16)

    return _pcall(
        body, name="econv_bwd_conv", grid=(T // R,),
        in_specs=[pl.BlockSpec((R, CONV_CH), lambda i: (i, 0)),
                  pl.BlockSpec((H, CONV_CH), lambda i: (jnp.minimum((i + 1) * (R // H), T // H - 1), 0)),
                  pl.BlockSpec((R, 2 * CONV_CH), lambda i: (i, 0)),
                  pl.BlockSpec((H, 2 * CONV_CH), lambda i: (jnp.maximum(i * (R // H) - 1, 0), 0)),
                  pl.BlockSpec((CONV_W, CONV_CH), lambda i: (0, 0))],
        out_specs=[pl.BlockSpec((R, 2 * CONV_CH), lambda i: (i, 0)), pl.BlockSpec((CONV_W, CONV_CH), lambda i: (0, 0))],
        out_shape=[jax.ShapeDtypeStruct((T, EVEN_IN), BF16), jax.ShapeDtypeStruct((CONV_W, CONV_CH), F32)],
        compiler_params=_params("arbitrary"),
    )(dcv, dcv, zc, zc, conv_k)


def _swap_halves(v):
    lane = lax.broadcasted_iota(jnp.int32, v.shape, 1)
    return jnp.where((lane % HEAD_DIM) < HEAD_DIM // 2, pltpu.roll(v, LANES - HEAD_DIM // 2, 1),
                     pltpu.roll(v, HEAD_DIM // 2, 1))


def _qkv_proj(hn, w_in, rope_c, rope_s, tm=T):
    tn = 4 * LANES

    def body(a_ref, b_ref, c_ref, s_ref, o_ref):
        j = pl.program_id(1)
        acc = _nn(a_ref[...], b_ref[...])
        for p in range(4):
            v = acc[:, p * LANES:(p + 1) * LANES]
            rot = v * c_ref[...] + _swap_halves(v) * s_ref[...]
            o_ref[p] = jnp.where(j < 6, rot, v)

    tab = pl.BlockSpec((tm, LANES), lambda i, j: (i, 0))
    return _pcall(
        body, name="qkv_proj", grid=(T // tm, 9),
        in_specs=[pl.BlockSpec((tm, D), lambda i, j: (i, 0)),
                  pl.BlockSpec((D, tn), lambda i, j: (0, j + (2 * CONV_CH) // tn)), tab, tab],
        out_specs=pl.BlockSpec((None, 4, tm, LANES), lambda i, j: (j, 0, i, 0)),
        out_shape=jax.ShapeDtypeStruct((9, 4, T, LANES), F32),
        compiler_params=_params("parallel", "parallel"),
    )(hn, w_in, rope_c, rope_s)


ATTN_FWD_UNROLL = 4
ATTN_BWD_UNROLL = 4


def _band_rows(start, d):
    if d == 1:
        return pl.ds(pl.multiple_of(start, BAND), BAND)
    return pl.ds(start, BAND, stride=d)


def _band_masks(n):
    row = lax.broadcasted_iota(jnp.int32, (BAND, BAND), 0)
    col = lax.broadcasted_iota(jnp.int32, (BAND, BAND), 1)
    no_prev = (n == 0).astype(jnp.int32) * (2 * BAND)
    return col <= row, col >= row + no_prev


def _attn_fwd(qkv, g):
    d = DILATIONS[g]
    nb = T // d // BAND

    def body(q_ref, k_ref, v_ref, o_ref, l_ref):
        lane_lo = lax.broadcasted_iota(jnp.int32, (BAND, LANES), 1) < HEAD_DIM

        heads = (lane_lo, jnp.logical_not(lane_lo))
        ones = jnp.ones((BAND, LANES), BF16)

        def step(it, carry):
            tiles = []
            for u in range(ATTN_FWD_UNROLL):
                idx = it * ATTN_FWD_UNROLL + u
                r = idx // nb
                n = idx % nb
                cur = _band_rows(n * (BAND * d) + r, d)
                prev = _band_rows(jnp.maximum(n - 1, 0) * (BAND * d) + r, d)
                mc, mp = _band_masks(n)
                tiles.append((cur, mc, mp, q_ref[cur, :], k_ref[cur, :].astype(BF16), v_ref[cur, :].astype(BF16),
                              k_ref[prev, :].astype(BF16), v_ref[prev, :].astype(BF16)))
            scores = []
            for cur, mc, mp, q, kc, vc, kp, vp in tiles:
                for hm in heads:
                    qm = jnp.where(hm, q, 0.0).astype(BF16)
                    scores.append((jnp.where(mc, _nt(qm, kc) * SCALE, NEG), jnp.where(mp, _nt(qm, kp) * SCALE, NEG)))
            maxes = [jnp.maximum(jnp.max(sc, axis=1, keepdims=True), jnp.max(sp, axis=1, keepdims=True))
                     for sc, sp in scores]
            probs = [(jnp.exp(sc - mx).astype(BF16), jnp.exp(sp - mx).astype(BF16))
                     for (sc, sp), mx in zip(scores, maxes)]
            dens = [_nn(pc, ones) + _nn(pp, ones) for pc, pp in probs]
            for t, (cur, mc, mp, q, kc, vc, kp, vp) in enumerate(tiles):
                outs, lses = [], []
                for h in range(2):
                    pc, pp = probs[2 * t + h]
                    outs.append((_nn(pc, vc) + _nn(pp, vp)) / dens[2 * t + h])
                    lses.append(maxes[2 * t + h] + jnp.log(dens[2 * t + h]))
                o_ref[cur, :] = jnp.where(lane_lo, outs[0], outs[1])
                l_ref[cur, :] = jnp.where(lane_lo, lses[0], lses[1])
            return carry

        lax.fori_loop(0, d * nb // ATTN_FWD_UNROLL, step, 0)

    def slab(which):
        return pl.BlockSpec((None, None, T, LANES), lambda p: (which * 3 + g, p, 0, 0))

    out = pl.BlockSpec((None, T, LANES), lambda p: (p, 0, 0))
    shape = jax.ShapeDtypeStruct((4, T, LANES), F32)
    return _pcall(
        body, name=f"attn_fwd{g}", grid=(4,),
        in_specs=[slab(0), slab(1), slab(2)], out_specs=[out, out], out_shape=[shape, shape],
        compiler_params=_params("parallel"),
    )(qkv, qkv, qkv)


def _attn_merge(outs, lses, cat, tm=1024):
    def body(o0, o1, o2, l0, l1, l2, cat_in, cat_ref, att_ref, w0, w1, w2):
        del cat_in
        la, lb, lc = l0[...], l1[...], l2[...]
        mx = jnp.maximum(jnp.maximum(la, lb), lc)
        ea, eb, ec = jnp.exp(la - mx), jnp.exp(lb - mx), jnp.exp(lc - mx)
        inv = 1.0 / (ea + eb + ec)
        wa, wb, wc = ea * inv, eb * inv, ec * inv
        att = wa * o0[...] + wb * o1[...] + wc * o2[...]
        att_ref[...] = att
        cat_ref[...] = att.astype(BF16)
        w0[...] = wa
        w1[...] = wb
        w2[...] = wc

    slab = pl.BlockSpec((None, tm, LANES), lambda p, i: (p, i, 0))
    shape = jax.ShapeDtypeStruct((4, T, LANES), F32)
    return _pcall(
        body, name="attn_merge", grid=(4, T // tm),
        in_specs=[slab] * 6 + [pl.BlockSpec(memory_space=pl.ANY)],
        out_specs=[pl.BlockSpec((tm, LANES), lambda p, i: (i, CONV_CH // LANES + p)), slab, slab, slab, slab],
        out_shape=[jax.ShapeDtypeStruct((T, D), BF16), shape, shape, shape, shape],
        input_output_aliases={6: 0},
        compiler_params=_params("parallel", "parallel"),
    )(*outs, *lses, cat)


def _attn_bwd(qkv, lse, wgt, att, dcat, dqkv, g):
    d = DILATIONS[g]
    nb = T // d // BAND

    def body(q_ref, k_ref, v_ref, l_ref, w_ref, a_ref, da_ref, dq_in, o_ref):
        del dq_in
        lane = lax.broadcasted_iota(jnp.int32, (BAND, LANES), 1)
        lane_lo = lane < HEAD_DIM
        row = lax.broadcasted_iota(jnp.int32, (LANES, LANES), 0)
        same_head = ((row // HEAD_DIM) == (lane // HEAD_DIM)).astype(BF16)
        dq_ref, dk_ref, dv_ref = o_ref.at[0], o_ref.at[1], o_ref.at[2]
        dk_ref[...] = jnp.zeros((T, LANES), F32)
        dv_ref[...] = jnp.zeros((T, LANES), F32)

        heads = (lane_lo, jnp.logical_not(lane_lo))

        def step(it, carry):
            tiles = []
            for u in range(ATTN_BWD_UNROLL):
                idx = it * ATTN_BWD_UNROLL + u
                r = idx // nb
                n = idx % nb
                cur = _band_rows(n * (BAND * d) + r, d)
                prev = _band_rows(jnp.maximum(n - 1, 0) * (BAND * d) + r, d)
                mc, mp = _band_masks(n)
                da = da_ref[cur, :]
                prod = da * a_ref[cur, :]
                hi = prod.astype(BF16)
                lo = (prod - hi.astype(F32)).astype(BF16)
                tiles.append(dict(cur=cur, prev=prev, mc=mc, mp=mp, da=da, hi=hi, lo=lo, q=q_ref[cur, :],
                                  kc=k_ref[cur, :].astype(BF16), vc=v_ref[cur, :].astype(BF16),
                                  kp=k_ref[prev, :].astype(BF16), vp=v_ref[prev, :].astype(BF16),
                                  lse=l_ref[cur, :], w=w_ref[cur, :]))
            for t in tiles:
                t["csum"] = _nn(t["hi"], same_head) + _nn(t["lo"], same_head)
            chains = []
            for t in tiles:
                for h, hm in enumerate(heads):
                    qm = jnp.where(hm, t["q"], 0.0).astype(BF16)
                    dam = jnp.where(hm, t["da"], 0.0).astype(BF16)
                    chains.append(dict(t=t, h=h, qm=qm, dam=dam,
                                       sc=jnp.where(t["mc"], _nt(qm, t["kc"]) * SCALE, NEG),
                                       sp=jnp.where(t["mp"], _nt(qm, t["kp"]) * SCALE, NEG),
                                       dpc=_nt(dam, t["vc"]), dpp=_nt(dam, t["vp"])))
            for ch in chains:
                t, col0 = ch["t"], ch["h"] * HEAD_DIM
                lse_h = t["lse"][:, col0:col0 + 1]
                w_h = t["w"][:, col0:col0 + 1]
                c_h = t["csum"][:, col0:col0 + 1]
                pwc = w_h * jnp.exp(ch["sc"] - lse_h)
                pwp = w_h * jnp.exp(ch["sp"] - lse_h)
                ch["dsc"] = (pwc * (ch["dpc"] - c_h) * SCALE).astype(BF16)
                ch["dsp"] = (pwp * (ch["dpp"] - c_h) * SCALE).astype(BF16)
                ch["pwc"] = pwc.astype(BF16)
                ch["pwp"] = pwp.astype(BF16)
            for ch in chains:
                t = ch["t"]
                ch["dq"] = _nn(ch["dsc"], t["kc"]) + _nn(ch["dsp"], t["kp"])
                ch["dkc"] = _tn(ch["dsc"], ch["qm"])
                ch["dkp"] = _tn(ch["dsp"], ch["qm"])
                ch["dvc"] = _tn(ch["pwc"], ch["dam"])
                ch["dvp"] = _tn(ch["pwp"], ch["dam"])
            for i, t in enumerate(tiles):
                c0, c1 = chains[2 * i], chains[2 * i + 1]
                dq_ref[t["cur"], :] = jnp.where(lane_lo, c0["dq"], c1["dq"])
                dk_ref[t["cur"], :] += c0["dkc"] + c1["dkc"]
                dk_ref[t["prev"], :] += c0["dkp"] + c1["dkp"]
                dv_ref[t["cur"], :] += c0["dvc"] + c1["dvc"]
                dv_ref[t["prev"], :] += c0["dvp"] + c1["dvp"]
            return carry

        lax.fori_loop(0, d * nb // ATTN_BWD_UNROLL, step, 0)

    def slab(which):
        return pl.BlockSpec((None, None, T, LANES), lambda p: (which * 3 + g, p, 0, 0))

    per_pair = pl.BlockSpec((None, T, LANES), lambda p: (p, 0, 0))
    return _pcall(
        body, name=f"attn_bwd{g}", grid=(4,),
        in_specs=[slab(0), slab(1), slab(2), per_pair, per_pair, per_pair,
                  pl.BlockSpec((T, LANES), lambda p: (0, CONV_CH // LANES + p)),
                  pl.BlockSpec(memory_space=pl.ANY)],
        out_specs=pl.BlockSpec((None, 3, None, T, LANES), lambda p: (g, 0, p, 0, 0)),
        out_shape=jax.ShapeDtypeStruct((3, 3, 4, T, LANES), F32),
        input_output_aliases={7: 0},
        compiler_params=_params("parallel"),
    )(qkv, qkv, qkv, lse, wgt, att, dcat, dqkv)


def _rope_bwd(dqkv, rope_c, rope_s, dz):
    wide = 4 * LANES

    def body(d_ref, c_ref, s_ref, dz_in, o_ref):
        del dz_in
        w = pl.program_id(1)
        for p in range(4):
            v = d_ref[p]
            rot = v * c_ref[...] + _swap_halves(v * s_ref[...])
            o_ref[:, p * LANES:(p + 1) * LANES] = jnp.where(w < 2, rot, v).astype(BF16)

    tab = pl.BlockSpec((T, LANES), lambda g, w: (0, 0))
    return _pcall(
        body, name="rope_bwd", grid=(3, 3),
        in_specs=[pl.BlockSpec((None, None, 4, T, LANES), lambda g, w: (g, w, 0, 0, 0)), tab, tab,
                  pl.BlockSpec(memory_space=pl.ANY)],
        out_specs=pl.BlockSpec((T, wide), lambda g, w: (0, (2 * CONV_CH) // wide + w * 3 + g)),
        out_shape=jax.ShapeDtypeStruct((T, EVEN_IN), BF16),
        input_output_aliases={3: 0},
        compiler_params=_params("parallel", "parallel"),
    )(dqkv, rope_c, rope_s, dz)


ODD_TILE = 256
ODD_HALO = 8
GELU_C = 0.7978845608028654
GELU_A = 0.044715


def _gelu(x):
    return 0.5 * x * (1.0 + jnp.tanh(GELU_C * (x + GELU_A * x * x * x)))


def _gelu_grad(x):
    th = jnp.tanh(GELU_C * (x + GELU_A * x * x * x))
    return 0.5 * (1.0 + th) + 0.5 * x * (1.0 - th * th) * GELU_C * (1.0 + 3.0 * GELU_A * x * x)


def _tril():
    row = lax.broadcasted_iota(jnp.int32, (CHUNK, CHUNK), 0)
    col = lax.broadcasted_iota(jnp.int32, (CHUNK, CHUNK), 1)
    return (col <= row).astype(F32)


def _odd_parts(z, zh, i, k_ref, g_ref, be_ref, w_ref, bt_ref):
    R, H = ODD_TILE, ODD_HALO
    gb, gc, xs, uv = z[:, :512], z[:, 512:1024], z[:, 1024:1536], z[:, 1536:]
    halo = zh[:, 512:1024] * zh[:, 1024:1536] * (i > 0).astype(F32)
    win = jnp.concatenate([halo, gc * xs], axis=0)
    cv = jnp.zeros((R, 512), F32)
    for j in range(SCONV_W):
        off = H - (SCONV_W - 1) + j
        cv = cv + k_ref[j:j + 1, :] * win[off:off + R, :]
    ge = _gelu(uv)
    u, v = ge[:, :512], ge[:, 512:]
    mu = jnp.mean(v, axis=-1, keepdims=True)
    xc = v - mu
    rstd = lax.rsqrt(jnp.mean(xc * xc, axis=-1, keepdims=True) + EPS)
    xh = xc * rstd
    vn = xh * g_ref[...] + be_ref[...]
    tril = _tril()
    wms = [(w_ref[g] * tril).astype(BF16) for g in range(SG_GROUPS)]
    rows = []
    for ci in range(R // CHUNK):
        blocks = []
        for g in range(SG_GROUPS):
            blk = vn[ci * CHUNK:(ci + 1) * CHUNK, g * LANES:(g + 1) * LANES].astype(BF16)
            blocks.append(_nn(wms[g], blk) + bt_ref[:, g:g + 1])
        rows.append(jnp.concatenate(blocks, axis=1))
    vmix = jnp.concatenate(rows, axis=0)
    return gb, gc, xs, uv, win, cv, u, rstd, xh, vn, vmix, wms


def _odd_mid_fwd(z, conv_k, ln_g, ln_b, sg_w, sg_bt):
    R, H = ODD_TILE, ODD_HALO

    def body(z_ref, zh_ref, k_ref, g_ref, be_ref, w_ref, bt_ref, o_ref):
        i = pl.program_id(0)
        gb, _, _, _, _, cv, u, _, _, _, vmix, _ = _odd_parts(z_ref[...], zh_ref[...], i, k_ref, g_ref, be_ref, w_ref, bt_ref)
        o_ref[...] = jnp.concatenate([gb * cv, u * vmix], axis=1).astype(BF16)

    vec = pl.BlockSpec((1, 512), lambda i: (0, 0))
    return _pcall(
        body, name="odd_mid_fwd", grid=(T // R,),
        in_specs=[pl.BlockSpec((R, ODD_IN), lambda i: (i, 0)),
                  pl.BlockSpec((H, ODD_IN), lambda i: (jnp.maximum(i * (R // H) - 1, 0), 0)),
                  pl.BlockSpec((SCONV_W, 512), lambda i: (0, 0)), vec, vec,
                  pl.BlockSpec((SG_GROUPS, CHUNK, CHUNK), lambda i: (0, 0, 0)),
                  pl.BlockSpec((CHUNK, SG_GROUPS), lambda i: (0, 0))],
        out_specs=pl.BlockSpec((R, D), lambda i: (i, 0)),
        out_shape=jax.ShapeDtypeStruct((T, D), BF16),
        compiler_params=_params("parallel"),
    )(z, z, conv_k, ln_g, ln_b, sg_w, sg_bt)


def _odd_mid_bwd(z, dcat, conv_k, ln_g, ln_b, sg_w, sg_bt):
    R, H = ODD_TILE, ODD_HALO
    last = T // R - 1

    def body(z_ref, zh_ref, zn_ref, d_ref, dn_ref, k_ref, g_ref, be_ref, w_ref, bt_ref,
             dz_ref, dk_ref, dg_ref, dbe_ref, dw_ref, dbt_ref):
        i = pl.program_id(0)
        z = z_ref[...]
        gb, gc, xs, uv, win, cv, u, rstd, xh, vn, vmix, wms = _odd_parts(z, zh_ref[...], i, k_ref, g_ref, be_ref, w_ref, bt_ref)
        dcat_t = d_ref[...]
        dc, dd = dcat_t[:, :512], dcat_t[:, 512:]

        @pl.when(i == 0)
        def _():
            dk_ref[...] = jnp.zeros_like(dk_ref)
            dg_ref[...] = jnp.zeros_like(dg_ref)
            dbe_ref[...] = jnp.zeros_like(dbe_ref)
            dw_ref[...] = jnp.zeros_like(dw_ref)
            dbt_ref[...] = jnp.zeros_like(dbt_ref)

        dgb = dc * cv
        dcv = dc * gb
        nxt = dn_ref[:, :512] * zn_ref[:, :512] * (i < last).astype(F32)
        winb = jnp.concatenate([dcv, nxt], axis=0)
        dp = jnp.zeros((R, 512), F32)
        for j in range(SCONV_W):
            off = H - (SCONV_W - 1) + j
            dk_ref[j:j + 1, :] += jnp.sum(dcv * win[off:off + R, :], axis=0, keepdims=True)
            ob = SCONV_W - 1 - j
            dp = dp + k_ref[j:j + 1, :] * winb[ob:ob + R, :]
        dgc = dp * xs
        dxs = dp * gc
        du = dd * vmix
        dvmix = dd * u
        tril = _tril()
        rows = []
        for ci in range(R // CHUNK):
            blocks = []
            for g in range(SG_GROUPS):
                sl = (slice(ci * CHUNK, (ci + 1) * CHUNK), slice(g * LANES, (g + 1) * LANES))
                dblk = dvmix[sl]
                dblk16 = dblk.astype(BF16)
                blocks.append(_tn(wms[g], dblk16))
                dw_ref[g] += _nt(dblk16, vn[sl].astype(BF16)) * tril
                dbt_ref[:, g:g + 1] += jnp.sum(dblk, axis=1, keepdims=True)
            rows.append(jnp.concatenate(blocks, axis=1))
        dvn = jnp.concatenate(rows, axis=0)
        dg_ref[...] += jnp.sum(dvn * xh, axis=0, keepdims=True)
        dbe_ref[...] += jnp.sum(dvn, axis=0, keepdims=True)
        dxh = dvn * g_ref[...]
        dv = rstd * (dxh - jnp.mean(dxh, axis=-1, keepdims=True) - xh * jnp.mean(dxh * xh, axis=-1, keepdims=True))
        duv = jnp.concatenate([du, dv], axis=1) * _gelu_grad(uv)
        dz_ref[...] = jnp.concatenate([dgb, dgc, dxs, duv], axis=1).astype(BF16)

    vec = pl.BlockSpec((1, 512), lambda i: (0, 0))
    kspec = pl.BlockSpec((SCONV_W, 512), lambda i: (0, 0))
    wspec = pl.BlockSpec((SG_GROUPS, CHUNK, CHUNK), lambda i: (0, 0, 0))
    bspec = pl.BlockSpec((CHUNK, SG_GROUPS), lambda i: (0, 0))
    nxt_blk = lambda i: (jnp.minimum((i + 1) * (R // H), T // H - 1), 0)
    return _pcall(
        body, name="odd_mid_bwd", grid=(T // R,),
        in_specs=[pl.BlockSpec((R, ODD_IN), lambda i: (i, 0)),
                  pl.BlockSpec((H, ODD_IN), lambda i: (jnp.maximum(i * (R // H) - 1, 0), 0)),
                  pl.BlockSpec((H, ODD_IN), nxt_blk),
                  pl.BlockSpec((R, D), lambda i: (i, 0)),
                  pl.BlockSpec((H, D), nxt_blk),
                  kspec, vec, vec, wspec, bspec],
        out_specs=[pl.BlockSpec((R, ODD_IN), lambda i: (i, 0)), kspec, vec, vec, wspec, bspec],
        out_shape=[jax.ShapeDtypeStruct((T, ODD_IN), BF16), jax.ShapeDtypeStruct((SCONV_W, 512), F32),
                   jax.ShapeDtypeStruct((1, 512), F32), jax.ShapeDtypeStruct((1, 512), F32),
                   jax.ShapeDtypeStruct((SG_GROUPS, CHUNK, CHUNK), F32), jax.ShapeDtypeStruct((CHUNK, SG_GROUPS), F32)],
        compiler_params=_params("arbitrary"),
    )(z, z, z, dcat, dcat, conv_k, ln_g, ln_b, sg_w, sg_bt)


def _ffn_up(tag, hn, weight):
    def act(acc):
        r = jnp.maximum(acc, 0.0)
        return (r * r,)

    return _mm(f"ffn{tag}_up", "nn", hn, weight(f"ffn_w1_{tag}", hn), T, D_FF, D, (BF16,), epi=act)


def _ffn_bwd(tag, h, g, weight, emit, saved, dout):
    hn, f = saved
    du = _mm(f"ffn{tag}_dact", "nt", dout, weight(f"ffn_w2_{tag}", dout), T, D_FF, D, (BF16,),
             epi=lambda acc, ff: (acc * (2.0 * jnp.sqrt(ff.astype(F32))),), extras=(f,))
    tok = emit(f"ffn_w2_{tag}", f, dout)
    tok = emit(f"ffn_w1_{tag}", hn, du, tie=tok)
    return _mm_dx_norm(f"ffn{tag}_dhn", du, weight(f"ffn_w1_{tag}", du), D_FF, h, g, dout, tie=tok)


def _rope_tables():
    half = HEAD_DIM // 2
    inv = 10000.0 ** (-jnp.arange(half, dtype=F32) / half)
    ang = jnp.arange(T, dtype=F32)[:, None] * inv[None, :]
    cos, sin = jnp.cos(ang), jnp.sin(ang)
    c = jnp.tile(jnp.concatenate([cos, cos], axis=1), (1, LANES // HEAD_DIM))
    s = jnp.tile(jnp.concatenate([-sin, sin], axis=1), (1, LANES // HEAD_DIM))
    return c, s


def _local_step(x, target, p, weight, emit, emit_small, before=lambda name, after: None):
    rope_c, rope_s = _rope_tables()
    grads = {}

    hn0 = _rms_fwd("mix0_norm", x, p["norm_mix_g0"])
    zc = _mm("even_in_conv", "nn", hn0, weight("even_w_in", hn0), T, 2 * CONV_CH, D, (F32,))
    qkv = _qkv_proj(hn0, weight("even_w_in", hn0), rope_c, rope_s)
    cv, cat0 = _econv_fwd(zc, p["even_conv_k"], p["even_conv_b"], p["even_ln_g"], p["even_ln_b"])
    att_parts = [_attn_fwd(qkv, 0)]
    before("attn_fwd1", att_parts[0][0])
    att_parts += [_attn_fwd(qkv, 1), _attn_fwd(qkv, 2)]
    outs = [a[0] for a in att_parts]
    lses = [a[1] for a in att_parts]
    cat0, att, w0, w1, w2 = _attn_merge(outs, lses, cat0)
    wgts = (w0, w1, w2)
    h1, hnf0 = _mm_out_norm("even_out", cat0, weight("even_w_out", cat0), D, x, p["norm_ffn_g0"])
    f0 = _ffn_up(0, hnf0, weight)
    before("ffn0_down", f0)
    h2, hn1 = _mm_out_norm("ffn0_down", f0, weight("ffn_w2_0", f0), D_FF, h1, p["norm_mix_g1"], tm=FFN0_DOWN_TILE)

    z1 = _mm("odd_in", "nn", hn1, weight("odd_w_in", hn1), T, ODD_IN, D, (F32,))
    cat1 = _odd_mid_fwd(z1, p["odd_conv_k"], p["odd_ln_g"], p["odd_ln_b"], p["odd_sg_w"], p["odd_sg_bt"])
    h3, hnf1 = _mm_out_norm("odd_out", cat1, weight("odd_w_out", cat1), D, h2, p["norm_ffn_g1"])
    f1 = _ffn_up(1, hnf1, weight)
    dh4, grads["final_g"], loss = _mm_out_loss("ffn1_down_loss", f1, weight("ffn_w2_1", f1), D_FF, h3, p["final_g"], target)

    dh3, grads["norm_ffn_g1"] = _ffn_bwd(1, h3, p["norm_ffn_g1"], weight, emit, (hnf1, f1), dh4)
    tok = emit("odd_w_out", cat1, dh3)
    dcat1 = _mm("odd_out_dx", "nt", dh3, weight("odd_w_out", dh3), T, D, D, (F32,), tie=tok)
    dz1, grads["odd_conv_k"], grads["odd_ln_g"], grads["odd_ln_b"], grads["odd_sg_w"], grads["odd_sg_bt"] = _odd_mid_bwd(
        z1, dcat1, p["odd_conv_k"], p["odd_ln_g"], p["odd_ln_b"], p["odd_sg_w"], p["odd_sg_bt"])
    tok = emit("odd_w_in", hn1, dz1)
    dh2, grads["norm_mix_g1"] = _mm_dx_norm("odd_in_dx", dz1, weight("odd_w_in", dz1), ODD_IN, h2, p["norm_mix_g1"],
                                            dh3, tie=tok)

    dh1, grads["norm_ffn_g0"] = _ffn_bwd(0, h1, p["norm_ffn_g0"], weight, emit, (hnf0, f0), dh2)
    tok = emit("even_w_out", cat0, dh1)
    dcat0 = _mm("even_out_dx", "nt", dh1, weight("even_w_out", dh1), T, D, D, (F32,), tie=tok)
    dcv, grads["even_ln_g"], grads["even_ln_b"], grads["even_conv_b"] = _econv_bwd_ln(
        cv, dcat0, p["even_ln_g"], p["even_ln_b"])
    dz0, grads["even_conv_k"] = _econv_bwd_conv(dcv, zc, p["even_conv_k"])
    tok = emit_small(loss, grads)
    dqkv = lax.empty((3, 3, 4, T, LANES), F32)
    for g in range(3):
        dqkv = _attn_bwd(qkv, lses[g], wgts[g], att, dcat0, dqkv, g)
    before("rope_bwd", dqkv)
    dz0 = _rope_bwd(dqkv, rope_c, rope_s, dz0)
    tok = emit("even_w_in", hn0, dz0, tie=tok)
    dx, dg0 = _mm_dx_norm("even_in_dx", dz0, weight("even_w_in", dz0), EVEN_IN, x, p["norm_mix_g0"], dh1, tie=tok)
    return dx, dg0


def _rowwise(name, fn, ins, out_dtypes, tm=256, tie=None):
    rows, cols = ins[0].shape
    tm = tm if rows % tm == 0 else rows
    n_in = len(ins)
    ties = () if tie is None else (tie,)

    def body(*refs):
        vals = fn(*[r[...] for r in refs[:n_in]])
        for o_ref, v in zip(refs[n_in + len(ties):], vals):
            o_ref[...] = v.astype(o_ref.dtype)

    spec = pl.BlockSpec((tm, cols), lambda i: (i, 0))
    outs = _pcall(
        body, name=name, grid=(rows // tm,),
        in_specs=[spec] * n_in + [pl.BlockSpec(TOKEN_SHAPE, lambda i: (0, 0))] * len(ties),
        out_specs=[spec] * len(out_dtypes),
        out_shape=[jax.ShapeDtypeStruct((rows, cols), dt) for dt in out_dtypes],
        compiler_params=_params("parallel"),
    )(*ins, *ties)
    return outs[0] if len(out_dtypes) == 1 else outs


def _adamw(name, w, g, m, v, with_grad=False, tie=None):
    c1 = 1.0 - ADAM_B1 ** ADAM_STEP
    c2 = 1.0 - ADAM_B2 ** ADAM_STEP

    def fn(w_t, g_t, m_t, v_t):
        m_new = ADAM_B1 * m_t + (1.0 - ADAM_B1) * g_t
        v_new = ADAM_B2 * v_t + (1.0 - ADAM_B2) * (g_t * g_t)
        delta = -ADAM_LR * ((m_new / c1) / (jnp.sqrt(v_new / c2) + ADAM_EPS) + ADAM_WD * w_t)
        return (delta, m_new, v_new, g_t) if with_grad else (delta, m_new, v_new)

    return _rowwise(name, fn, (w, g, m, v), (F32,) * (4 if with_grad else 3), tie=tie)


class _Piece:
    def __init__(self, name, rows, cols, axis, src, src_row0):
        self.name, self.rows, self.cols, self.axis = name, rows, cols, axis
        self.width = (cols if axis == 1 else rows) // 4
        self.src, self.src_row0 = src, src_row0

    @property
    def full_shape(self):
        return (self.rows, self.cols)

    @property
    def half_shape(self):
        return (self.rows // 2, self.cols) if self.axis == 1 else (self.rows, self.cols // 2)

    @property
    def shard_half_shape(self):
        return (self.rows // 2, self.width) if self.axis == 1 else (self.width, self.cols // 2)

    def shard_whole(self, ref):
        n = self.rows if self.axis == 1 else self.width
        return ref.at[pl.ds(self.src_row0, n), :]

    def shard_half(self, ref, h):
        if self.axis == 1:
            return ref.at[pl.ds(self.src_row0 + h * (self.rows // 2), self.rows // 2), :]
        return ref.at[pl.ds(self.src_row0, self.width), pl.ds(h * (self.cols // 2), self.cols // 2)]

    def full_shard(self, ref, s):
        if self.axis == 1:
            return ref.at[:, pl.ds(s * self.width, self.width)]
        return ref.at[pl.ds(s * self.width, self.width), :]

    def full_shard_half(self, ref, s, h):
        if self.axis == 1:
            return ref.at[pl.ds(h * (self.rows // 2), self.rows // 2), pl.ds(s * self.width, self.width)]
        return ref.at[pl.ds(s * self.width, self.width), pl.ds(h * (self.cols // 2), self.cols // 2)]

    def full_half(self, ref, h):
        if self.axis == 1:
            return ref.at[pl.ds(h * (self.rows // 2), self.rows // 2), :]
        return ref.at[:, pl.ds(h * (self.cols // 2), self.cols // 2)]

    def full_half_rows(self, ref, h, r0, n):
        if self.axis == 1:
            return ref.at[pl.ds(h * (self.rows // 2) + r0, n), :]
        return ref.at[pl.ds(r0, n), pl.ds(h * (self.cols // 2), self.cols // 2)]

    def half_shard(self, ref, s):
        return self.full_shard(ref, s)


PIECES = (
    _Piece("even_w_in", D, EVEN_IN, 1, 0, 0),
    _Piece("even_w_out", D, D, 0, 1, 0),
    _Piece("ffn_w1_0", D, D_FF, 1, 4, 0),
    _Piece("ffn_w2_0", D_FF, D, 0, 5, 0),
    _Piece("odd_w_in", D, ODD_IN, 1, 2, 0),
    _Piece("odd_w_out", D, D, 0, 3, 0),
    _Piece("ffn_w1_1", D, D_FF, 1, 4, D),
    _Piece("ffn_w2_1", D_FF, D, 0, 5, D_FF // 4),
)
N_PIECES = len(PIECES)
FORWARD_RIDES = {"attn_fwd1": (1, 2, 3), "ffn0_down": (4, 5, 6, 7)}
JOIN_GROUPS = ((0, 1, 2, 3), (4, 5))
JOIN_RIDES_IN = "rope_bwd"
HOLD_BACK = ("ffn_w2_0", "ffn_w2_1", "odd_w_out")
N_SHARD_OPERANDS = 6
ANY = pl.BlockSpec(memory_space=pl.ANY)
MESH = pl.DeviceIdType.MESH


def _mesh_place():
    x, y, c = lax.axis_index("x"), lax.axis_index("y"), lax.axis_index("c")
    chips = [(1 - x, y), (x, 1 - y), (1 - x, 1 - y)]
    return x, y, c, chips


def _remote(src, dst, send_sem, recv_sem, dev):
    return pltpu.make_async_remote_copy(src_ref=src, dst_ref=dst, send_sem=send_sem, recv_sem=recv_sem,
                                        device_id=dev, device_id_type=MESH)


HBM = pl.BlockSpec(memory_space=pltpu.HBM)
SEM = pl.BlockSpec(memory_space=pltpu.SEMAPHORE)
SPLIT_PARAMS = pltpu.CompilerParams(has_side_effects=pltpu.SideEffectType.DATAFLOW_SIDE_EFFECTING)
CAST_TILE = 256


def _in_hbm(a):
    return pltpu.with_memory_space_constraint(a, pltpu.HBM)


def _cast_place(pc, shard_operand, chip, tie=None):
    rows, cols = (pc.rows, pc.width) if pc.axis == 1 else (pc.width, pc.cols)
    nblk = rows // CAST_TILE
    blk0 = pc.src_row0 // CAST_TILE
    ties = () if tie is None else (tie,)

    def body(chip_ref, x_ref, *rest):
        del chip_ref
        rest[-1][...] = x_ref[...].astype(BF16)

    if pc.axis == 1:
        out_map = lambda i, chip_ref: (i, chip_ref[0])
    else:
        out_map = lambda i, chip_ref: (chip_ref[0] * nblk + i, 0)
    return _pcall(
        body, name=f"cast_{pc.name}",
        grid_spec=pltpu.PrefetchScalarGridSpec(
            num_scalar_prefetch=1, grid=(nblk,),
            in_specs=[pl.BlockSpec((CAST_TILE, cols), lambda i, chip_ref: (blk0 + i, 0))]
            + [pl.BlockSpec(TOKEN_SHAPE, lambda i, chip_ref: (0, 0))] * len(ties),
            out_specs=pl.BlockSpec((CAST_TILE, cols), out_map)),
        out_shape=jax.ShapeDtypeStruct(pc.full_shape, BF16),
        compiler_params=_params("parallel"),
    )(chip, shard_operand, *ties)


def _gather_start(name, pieces, fulls):
    n = len(pieces)

    def body(*refs):
        ins = refs[:n]
        sends = refs[2 * n:3 * n]
        recvs = refs[3 * n:4 * n]
        token = refs[4 * n]
        x, y, c, chips = _mesh_place()
        s = 2 * x + y
        for i, pc in enumerate(pieces):
            win = pc.full_shard_half(ins[i], s, c)
            for k, (cx, cy) in enumerate(chips):
                _remote(win, win, sends[i].at[k], recvs[i].at[k], (cx, cy, c)).start()
        token[...] = jnp.zeros(TOKEN_SHAPE, F32)

    sems = [pltpu.SemaphoreType.DMA((3,))] * (2 * n)
    outs = _pcall(
        body, name=name,
        in_specs=[HBM] * n,
        out_specs=[HBM] * n + [SEM] * (2 * n) + [pl.BlockSpec(memory_space=pltpu.VMEM)],
        out_shape=[pltpu.HBM(pc.full_shape, BF16) for pc in pieces] + sems + [jax.ShapeDtypeStruct(TOKEN_SHAPE, F32)],
        input_output_aliases={i: i for i in range(n)},
        compiler_params=SPLIT_PARAMS,
    )(*[_in_hbm(f) for f in fulls])
    return outs[:n], outs[n:2 * n], outs[2 * n:3 * n], outs[3 * n]


def _gather_wait(pc, full, send_sems, recv_sems, after):
    def body(full_ref, send_ref, recv_ref, after_ref, out_ref):
        del after_ref, out_ref
        x, y, c, chips = _mesh_place()
        for k, (cx, cy) in enumerate(chips):
            win = pc.full_shard_half(full_ref, 2 * cx + cy, c)
            cp = _remote(win, win, send_ref.at[k], recv_ref.at[k], (cx, cy, c))
            cp.wait_send()
            cp.wait_recv()

    return _pcall(
        body, name=f"gather_wait_{pc.name}",
        in_specs=[HBM, SEM, SEM, ANY], out_specs=HBM, out_shape=pltpu.HBM(pc.full_shape, BF16),
        input_output_aliases={0: 0}, compiler_params=SPLIT_PARAMS,
    )(full, send_sems, recv_sems, after)


def _core_forward_job(pieces, fulls):
    n = len(pieces)

    def copies(ins, outs, scr):
        send_bufs, recv_bufs = scr[:n], scr[n:2 * n]
        load_sems, send_sems, recv_sems, store_sems = scr[2 * n:]
        x, y, c, chips = _mesh_place()
        loads, sends, stores = [], [], []
        for i, pc in enumerate(pieces):
            for k, (cx, cy) in enumerate(chips):
                j = 3 * i + k
                loads.append(pltpu.make_async_copy(pc.full_shard_half(ins[i], 2 * cx + cy, c), send_bufs[i].at[k],
                                                   load_sems.at[j]))
                sends.append(_remote(send_bufs[i].at[k], recv_bufs[i].at[k], send_sems.at[j], recv_sems.at[j],
                                     (x, y, 1 - c)))
                stores.append(pltpu.make_async_copy(recv_bufs[i].at[k], pc.full_shard_half(outs[i], 2 * cx + cy, 1 - c),
                                                    store_sems.at[j]))
        return loads, sends, stores

    def begin(ins, outs, scr):
        for cp in copies(ins, outs, scr)[0]:
            cp.start()

    def advance(ins, outs, scr):
        loads, sends, _ = copies(ins, outs, scr)
        for load, send in zip(loads, sends):
            load.wait()
            send.start()

    def finish(ins, outs, scr):
        _, sends, stores = copies(ins, outs, scr)
        for send, store in zip(sends, stores):
            send.wait_recv()
            store.start()
        for send, store in zip(sends, stores):
            send.wait_send()
            store.wait()

    sems = pltpu.SemaphoreType.DMA((3 * n,))
    bufs = [pltpu.VMEM((3,) + pc.shard_half_shape, BF16) for pc in pieces]
    return _SideJob(fulls, [jax.ShapeDtypeStruct(pc.full_shape, BF16) for pc in pieces],
                    bufs + bufs + [sems, sems, sems, sems], {i: i for i in range(n)}, begin, advance, finish)


def _run_job(name, job):
    def body(o_ref):
        o_ref[...] = jnp.zeros(TOKEN_SHAPE, F32)

    _ride_next_call(job)
    _pcall(body, name=name, grid=(3,), in_specs=[], out_specs=pl.BlockSpec(TOKEN_SHAPE, lambda i: (0, 0)),
           out_shape=jax.ShapeDtypeStruct(TOKEN_SHAPE, F32))()
    return job.results


def _core_forward(pieces, fulls):
    n = len(pieces)

    def body(*refs):
        ins, outs = refs[:n], refs[n:2 * n]
        send_bufs, recv_bufs = refs[2 * n:3 * n], refs[3 * n:4 * n]
        load_sems, send_sems, recv_sems, store_sems = refs[4 * n:]
        x, y, c, chips = _mesh_place()
        loads, sends, stores = [], [], []
        for i, pc in enumerate(pieces):
            for k, (cx, cy) in enumerate(chips):
                cp = pltpu.make_async_copy(pc.full_shard_half(ins[i], 2 * cx + cy, c), send_bufs[i].at[k],
                                           load_sems.at[3 * i + k])
                cp.start()
                loads.append(cp)
        for i in range(n):
            for k in range(3):
                j = 3 * i + k
                loads[j].wait()
                cp = _remote(send_bufs[i].at[k], recv_bufs[i].at[k], send_sems.at[j], recv_sems.at[j], (x, y, 1 - c))
                cp.start()
                sends.append(cp)
        for i, pc in enumerate(pieces):
            for k, (cx, cy) in enumerate(chips):
                j = 3 * i + k
                sends[j].wait_recv()
                cp = pltpu.make_async_copy(recv_bufs[i].at[k], pc.full_shard_half(outs[i], 2 * cx + cy, 1 - c),
                                           store_sems.at[j])
                cp.start()
                stores.append(cp)
        for j in range(3 * n):
            sends[j].wait_send()
            stores[j].wait()

    sems = pltpu.SemaphoreType.DMA((3 * n,))
    bufs = [pltpu.VMEM((3,) + pc.shard_half_shape, BF16) for pc in pieces]
    return _pcall(
        body, name="core_forward_" + pieces[0].name, in_specs=[ANY] * n, out_specs=[ANY] * n,
        out_shape=[jax.ShapeDtypeStruct(pc.full_shape, BF16) for pc in pieces],
        scratch_shapes=bufs + bufs + [sems, sems, sems, sems],
        input_output_aliases={i: i for i in range(n)},
        compiler_params=pltpu.CompilerParams(vmem_limit_bytes=VMEM_LIMIT),
    )(*fulls)


def _dw_tile(pc):
    if pc.axis == 1:
        tn = max(t for t in range(LANES, pc.cols + 1, LANES) if pc.cols % t == 0 and t <= 1408)
        return pc.rows // 2, tn
    return min(pc.rows, 1024), pc.cols // 2


def _mm_dw_chipsum(pc, a, b, core, tie=None):
    tm, tn = _dw_tile(pc)
    hr, hc = pc.half_shape
    tiles_r, tiles_c = hr // tm, hc // tn
    th = tiles_r * tiles_c
    ties = () if tie is None else (tie,)

    def tile_of(s, core_ref):
        mine = s >= th
        half = jnp.where(mine, core_ref[0], 1 - core_ref[0])
        local = s % th
        li, lj = local // tiles_c, local % tiles_c
        if pc.axis == 1:
            return half * tiles_r + li, lj, li, lj, mine
        return li, half * tiles_c + lj, li, lj, mine

    def body(core_ref, a_ref, b_ref, *rest):
        o_ref, send_buf, recv_buf, send_sems, recv_sems = rest[len(ties):]
        s = pl.program_id(0)
        local = s % th
        x, y, c = lax.axis_index("x"), lax.axis_index("y"), lax.axis_index("c")
        acc = _tn(a_ref[...].astype(BF16), b_ref[...].astype(BF16))

        def push(slot):
            return _remote(send_buf.at[slot], recv_buf.at[slot], send_sems.at[slot], recv_sems.at[slot], (x, y, 1 - c))

        @pl.when(s < th)
        def _():
            send_buf[local] = acc.astype(BF16)
            push(local).start()

        @pl.when(s >= th)
        def _():
            push(local).wait_recv()
            o_ref[...] = (acc + recv_buf[local].astype(F32)).astype(BF16)

        @pl.when(s == 2 * th - 1)
        def _():
            for slot in range(th):
                push(slot).wait_send()

    def a_map(s, core_ref):
        return 0, tile_of(s, core_ref)[0]

    def b_map(s, core_ref):
        return 0, tile_of(s, core_ref)[1]

    def o_map(s, core_ref):
        _, _, li, lj, mine = tile_of(s, core_ref)
        return jnp.where(mine, li, 0), jnp.where(mine, lj, 0)

    return _pcall(
        body, name=f"dw_{pc.name}",
        grid_spec=pltpu.PrefetchScalarGridSpec(
            num_scalar_prefetch=1, grid=(2 * th,),
            in_specs=[pl.BlockSpec((T, tm), a_map), pl.BlockSpec((T, tn), b_map)]
            + [pl.BlockSpec(TOKEN_SHAPE, lambda s, core_ref: (0, 0))] * len(ties),
            out_specs=pl.BlockSpec((tm, tn), o_map),
            scratch_shapes=[pltpu.VMEM((th, tm, tn), BF16), pltpu.VMEM((th, tm, tn), BF16),
                            pltpu.SemaphoreType.DMA((th,)), pltpu.SemaphoreType.DMA((th,))]),
        out_shape=jax.ShapeDtypeStruct((hr, hc), BF16),
        compiler_params=_params("arbitrary"),
    )(core, a, b, *ties)


def _scatter_start(pieces, chip_sums):
    n = len(pieces)

    def body(*refs):
        sums, lands = refs[:n], refs[n:2 * n]
        sends, recvs = refs[4 * n:5 * n], refs[5 * n:6 * n]
        token = refs[6 * n]
        x, y, c, chips = _mesh_place()
        for i, pc in enumerate(pieces):
            for k, (cx, cy) in enumerate(chips):
                _remote(pc.half_shard(sums[i], 2 * cx + cy), lands[i].at[k], sends[i].at[k], recvs[i].at[k],
                        (cx, cy, c)).start()
        token[...] = jnp.zeros(TOKEN_SHAPE, F32)

    land_shapes = [(3,) + pc.shard_half_shape for pc in pieces]
    sems = [pltpu.SemaphoreType.DMA((3,))] * (2 * n)
    outs = _pcall(
        body, name="scatter_start_" + pieces[0].name,
        in_specs=[HBM] * (2 * n), out_specs=[HBM] * (2 * n) + [SEM] * (2 * n) + [pl.BlockSpec(memory_space=pltpu.VMEM)],
        out_shape=[pltpu.HBM(pc.half_shape, BF16) for pc in pieces] + [pltpu.HBM(sh, BF16) for sh in land_shapes]
        + sems + [jax.ShapeDtypeStruct(TOKEN_SHAPE, F32)],
        input_output_aliases={i: i for i in range(2 * n)}, compiler_params=SPLIT_PARAMS,
    )(*[_in_hbm(cs) for cs in chip_sums], *[_in_hbm(lax.empty(sh, BF16)) for sh in land_shapes])
    return [(outs[i], outs[n + i], outs[2 * n + i], outs[3 * n + i]) for i in range(n)], outs[4 * n]


def _scatter_wait(pc, chip_sum, land, send_sems, recv_sems, after):
    def body(sum_ref, land_ref, send_ref, recv_ref, after_ref, sum_out, land_out):
        del after_ref, sum_out, land_out
        x, y, c, chips = _mesh_place()
        for k, (cx, cy) in enumerate(chips):
            cp = _remote(pc.half_shard(sum_ref, 2 * cx + cy), land_ref.at[k], send_ref.at[k], recv_ref.at[k], (cx, cy, c))
            cp.wait_send()
            cp.wait_recv()

    return _pcall(
        body, name=f"scatter_wait_{pc.name}",
        in_specs=[HBM, HBM, SEM, SEM, ANY], out_specs=[HBM, HBM],
        out_shape=[pltpu.HBM(pc.half_shape, BF16), pltpu.HBM((3,) + pc.shard_half_shape, BF16)],
        input_output_aliases={0: 0, 1: 1}, compiler_params=SPLIT_PARAMS,
    )(chip_sum, land, send_sems, recv_sems, after)


SHARD_OPERAND_SHAPES = ((D, EVEN_IN // 4), (D // 4, D), (D, ODD_IN // 4), (D // 4, D), (2 * D, D_FF // 4), (2 * D_FF // 4, D))


def _allsum_join_job(operands, chip_sums, lands):
    pieces = [pc for pc in PIECES if pc.src in operands]
    n = len(pieces)

    def copies(ins, outs, scr):
        sum_refs, land_refs = ins[:n], ins[n:]
        out_refs = dict(zip(operands, outs))
        in_bufs, fin_bufs, recv_bufs = scr[:n], scr[n:2 * n], scr[2 * n:3 * n]
        load_sems, send_sems, recv_sems, out_sems = scr[3 * n:]
        x, y, c, _ = _mesh_place()
        s = 2 * x + y
        loads, sends, mine, theirs = [], [], [], []
        for j, pc in enumerate(pieces):
            loads.append((pltpu.make_async_copy(land_refs[j], in_bufs[j].at[pl.ds(0, 3)], load_sems.at[2 * j]),
                          pltpu.make_async_copy(pc.half_shard(sum_refs[j], s), in_bufs[j].at[3], load_sems.at[2 * j + 1])))
            sends.append(_remote(fin_bufs[j], recv_bufs[j], send_sems.at[j], recv_sems.at[j], (x, y, 1 - c)))
            mine.append(pltpu.make_async_copy(fin_bufs[j], pc.shard_half(out_refs[pc.src], c), out_sems.at[2 * j]))
            theirs.append(pltpu.make_async_copy(recv_bufs[j], pc.shard_half(out_refs[pc.src], 1 - c), out_sems.at[2 * j + 1]))
        return loads, sends, mine, theirs, in_bufs, fin_bufs

    def begin(ins, outs, scr):
        for a, b in copies(ins, outs, scr)[0]:
            a.start()
            b.start()

    def advance(ins, outs, scr):
        loads, sends, mine, _, in_bufs, fin_bufs = copies(ins, outs, scr)
        for j in range(n):
            loads[j][0].wait()
            loads[j][1].wait()
            acc = in_bufs[j][0].astype(F32)
            for k in range(1, 4):
                acc = acc + in_bufs[j][k].astype(F32)
            fin_bufs[j][...] = acc
            mine[j].start()
            sends[j].start()

    def finish(ins, outs, scr):
        _, sends, mine, theirs, _, _ = copies(ins, outs, scr)
        for j in range(n):
            sends[j].wait_recv()
            theirs[j].start()
        for j in range(n):
            sends[j].wait_send()
            mine[j].wait()
            theirs[j].wait()

    halves = [pc.shard_half_shape for pc in pieces]
    scratch = ([pltpu.VMEM((4,) + sh, BF16) for sh in halves] + [pltpu.VMEM(sh, F32) for sh in halves] * 2
               + [pltpu.SemaphoreType.DMA((2 * n,)), pltpu.SemaphoreType.DMA((n,)), pltpu.SemaphoreType.DMA((n,)),
                  pltpu.SemaphoreType.DMA((2 * n,))])
    return _SideJob(list(chip_sums) + list(lands), [jax.ShapeDtypeStruct(SHARD_OPERAND_SHAPES[o], F32) for o in operands],
                    scratch, {}, begin, advance, finish)


PEER_FLIPS = tuple((a, b, e) for a in (0, 1) for b in (0, 1) for e in (0, 1) if (a, b, e) != (0, 0, 0))


def _peers():
    x, y, c = lax.axis_index("x"), lax.axis_index("y"), lax.axis_index("c")
    me = 4 * x + 2 * y + c
    out = []
    for a, b, e in PEER_FLIPS:
        px, py, pc = (1 - x if a else x), (1 - y if b else y), (1 - c if e else c)
        out.append(((px, py, pc), 4 * px + 2 * py + pc))
    return me, out


def _exchange8_start(name, blk):
    m = blk.shape[0]

    def body(blk_ref, land_ref, blk_out, land_out, sends, recvs, token):
        del blk_out, land_out
        me, peers = _peers()
        for k, (dev, _) in enumerate(peers):
            _remote(blk_ref, land_ref.at[me], sends.at[k], recvs.at[k], dev).start()
        token[...] = jnp.zeros(TOKEN_SHAPE, F32)

    sems = pltpu.SemaphoreType.DMA((7,))
    return _pcall(
        body, name=name,
        in_specs=[HBM, HBM], out_specs=[HBM, HBM, SEM, SEM, pl.BlockSpec(memory_space=pltpu.VMEM)],
        out_shape=[pltpu.HBM((m, LANES), F32), pltpu.HBM((8, m, LANES), F32), sems, sems,
                   jax.ShapeDtypeStruct(TOKEN_SHAPE, F32)],
        input_output_aliases={0: 0, 1: 1}, compiler_params=SPLIT_PARAMS,
    )(_in_hbm(blk), _in_hbm(lax.empty((8, m, LANES), F32)))


def _exchange8_wait(name, blk, land, send_sems, recv_sems, after):
    def body(blk_ref, land_ref, send_ref, recv_ref, after_ref, blk_out, land_out):
        del after_ref, blk_out, land_out
        _, peers = _peers()
        for k, (dev, slot) in enumerate(peers):
            cp = _remote(blk_ref, land_ref.at[slot], send_ref.at[k], recv_ref.at[k], dev)
            cp.wait_send()
            cp.wait_recv()

    m = blk.shape[0]
    return _pcall(
        body, name=name,
        in_specs=[HBM, HBM, SEM, SEM, ANY], out_specs=[HBM, HBM],
        out_shape=[pltpu.HBM((m, LANES), F32), pltpu.HBM((8, m, LANES), F32)],
        input_output_aliases={0: 0, 1: 1}, compiler_params=SPLIT_PARAMS,
    )(blk, land, send_sems, recv_sems, after)


def _collect8(name, blk, land, with_sum):
    m = blk.shape[0]

    def body(blk_ref, land_ref, out_ref, *scratch):
        sems = scratch[-1]
        dst = scratch[0] if with_sum else out_ref
        me, peers = _peers()
        copies = [pltpu.make_async_copy(blk_ref, dst.at[me], sems.at[7])]
        for k, (_, slot) in enumerate(peers):
            copies.append(pltpu.make_async_copy(land_ref.at[slot], dst.at[slot], sems.at[k]))
        for cp in copies:
            cp.start()
        for cp in copies:
            cp.wait()
        if with_sum:
            acc = dst[0]
            for dev in range(1, 8):
                acc = acc + dst[dev]
            out_ref[...] = acc

    all_shape = (8, m, LANES)
    return _pcall(
        body, name=name, in_specs=[ANY, ANY], out_specs=pl.BlockSpec(memory_space=pltpu.VMEM),
        out_shape=jax.ShapeDtypeStruct((m, LANES) if with_sum else all_shape, F32),
        scratch_shapes=([pltpu.VMEM(all_shape, F32)] if with_sum else []) + [pltpu.SemaphoreType.DMA((8,))],
    )(blk, land)


def _pack(arrays, row_counts):
    rows = []
    for a, n in zip(arrays, row_counts):
        flat = a.reshape(-1, LANES)
        rows.append(jnp.pad(flat, ((0, n - flat.shape[0]), (0, 0))))
    return jnp.concatenate(rows, axis=0)


REPL_NAMES = ("norm_mix_g", "norm_ffn_g", "even_conv_b", "even_ln_g", "even_ln_b", "odd_sg_w", "odd_sg_b", "final_g")
REPL_SHAPES = ((2, D), (2, D), (1, 512), (1, 512), (1, 512), (1, SG_GROUPS, CHUNK, CHUNK), (1, SG_GROUPS, CHUNK), (D,))
REPL_ROWS = (16, 16, 8, 8, 8, 512, 8, 8)
SHARDED_NAMES = ("even_conv_k", "odd_conv_k", "odd_ln_g", "odd_ln_b")
SHARDED_SHARD_SHAPES = ((1, CONV_W, LANES), (1, SCONV_W, LANES), (1, LANES), (1, LANES))
SHARDED_SHARD_ROWS = (32, 8, 8, 8)
SHARDED_FULL_SHAPES = ((CONV_W, 512), (SCONV_W, 512), (1, 512), (1, 512))
SHARDED_FULL_ROWS = (128, 16, 8, 8)
SMALL_NAMES = REPL_NAMES + SHARDED_NAMES
SMALL_ROWS = REPL_ROWS + SHARDED_SHARD_ROWS
SMALL_OUT_SHAPES = REPL_SHAPES[:-1] + ((1, D),) + SHARDED_SHARD_SHAPES
LOSS_ROWS = 8


def _offsets(rows):
    out, r0 = [], 0
    for n in rows:
        out.append(r0)
        r0 += n
    return out


def _adamw_small(w_pack, m_pack, v_pack, grad_sum, first_gain_sum):
    n_rows = sum(SMALL_ROWS)
    state_at = _offsets(SMALL_ROWS)
    grad_at = _offsets((LOSS_ROWS, 8) + REPL_ROWS[1:] + SHARDED_FULL_ROWS)[1:]
    c1 = 1.0 - ADAM_B1 ** ADAM_STEP
    c2 = 1.0 - ADAM_B2 ** ADAM_STEP
    n_repl = len(REPL_NAMES)

    def body(w_ref, m_ref, v_ref, g_ref, g0_ref, *rest):
        outs, gbuf = rest[:-1], rest[-1]
        chip = 2 * lax.axis_index("x") + lax.axis_index("y")
        gbuf[...] = jnp.zeros((n_rows, LANES), F32)
        gbuf[0:8, :] = g0_ref[...]
        gbuf[8:16, :] = g_ref[grad_at[0]:grad_at[0] + 8, :]
        for i in range(1, n_repl):
            gbuf[state_at[i]:state_at[i] + REPL_ROWS[i], :] = g_ref[grad_at[i]:grad_at[i] + REPL_ROWS[i], :]
        for k, shape in enumerate(SHARDED_SHARD_SHAPES):
            used = shape[-2] if len(shape) == 3 else 1
            src = pl.ds(grad_at[n_repl + k] + chip, used, stride=4) if used > 1 else pl.ds(grad_at[n_repl + k] + chip, 1)
            gbuf[state_at[n_repl + k]:state_at[n_repl + k] + used, :] = g_ref[src, :]
        g = gbuf[...]
        m_new = ADAM_B1 * m_ref[...] + (1.0 - ADAM_B1) * g
        v_new = ADAM_B2 * v_ref[...] + (1.0 - ADAM_B2) * (g * g)
        delta = -ADAM_LR * ((m_new / c1) / (jnp.sqrt(v_new / c2) + ADAM_EPS) + ADAM_WD * w_ref[...])
        for i, shape in enumerate(SMALL_OUT_SHAPES):
            for j, val in enumerate((g, delta, m_new, v_new)):
                o_ref = outs[4 * i + j]
                rows = val[state_at[i]:state_at[i] + SMALL_ROWS[i], :]
                if len(shape) == 2 and shape[1] > LANES:
                    per = shape[1] // LANES
                    for r in range(shape[0]):
                        for q in range(per):
                            o_ref[r:r + 1, q * LANES:(q + 1) * LANES] = rows[r * per + q:r * per + q + 1, :]
                elif len(shape) == 4:
                    for grp in range(shape[1]):
                        o_ref[0, grp] = rows[grp * shape[2]:(grp + 1) * shape[2], :]
                elif len(shape) == 3:
                    o_ref[0] = rows[:shape[1], :]
                else:
                    o_ref[...] = rows[:1, :]

    vm = pl.BlockSpec(memory_space=pltpu.VMEM)
    out_shape = [jax.ShapeDtypeStruct(sh, F32) for sh in SMALL_OUT_SHAPES for _ in range(4)]
    outs = _pcall(body, name="adamw_small", in_specs=[vm] * 5, out_specs=[vm] * len(out_shape), out_shape=out_shape,
                  scratch_shapes=[pltpu.VMEM((n_rows, LANES), F32)])(w_pack, m_pack, v_pack, grad_sum, first_gain_sum)
    return {n: outs[4 * i:4 * i + 4] for i, n in enumerate(SMALL_NAMES)}


def _touch(arrays):
    n = len(arrays)

    def body(*refs):
        refs[-1][...] = jnp.zeros(TOKEN_SHAPE, F32)

    outs = _pcall(
        body, name="touch", in_specs=[ANY] * n, out_specs=[ANY] * n + [pl.BlockSpec(memory_space=pltpu.VMEM)],
        out_shape=[jax.ShapeDtypeStruct(a.shape, a.dtype) for a in arrays] + [jax.ShapeDtypeStruct(TOKEN_SHAPE, F32)],
        input_output_aliases={i: i for i in range(n)})(*arrays)
    return outs[:n], outs[n]


def kernel(x, norm_mix_g, norm_ffn_g, even_w_in, even_conv_k, even_conv_b, even_ln_g, even_ln_b, even_w_out, odd_w_in, odd_conv_k, odd_ln_g, odd_ln_b, odd_sg_w, odd_sg_b, odd_w_out, ffn_w1, ffn_w2, final_g, loss_target, m_norm_mix_g, m_norm_ffn_g, m_even_w_in, m_even_conv_k, m_even_conv_b, m_even_ln_g, m_even_ln_b, m_even_w_out, m_odd_w_in, m_odd_conv_k, m_odd_ln_g, m_odd_ln_b, m_odd_sg_w, m_odd_sg_b, m_odd_w_out, m_ffn_w1, m_ffn_w2, m_final_g, v_norm_mix_g, v_norm_ffn_g, v_even_w_in, v_even_conv_k, v_even_conv_b, v_even_ln_g, v_even_ln_b, v_even_w_out, v_odd_w_in, v_odd_conv_k, v_odd_ln_g, v_odd_ln_b, v_odd_sg_w, v_odd_sg_b, v_odd_w_out, v_ffn_w1, v_ffn_w2, v_final_g):
    names = ("norm_mix_g", "norm_ffn_g", "even_w_in", "even_conv_k", "even_conv_b", "even_ln_g", "even_ln_b", "even_w_out",
             "odd_w_in", "odd_conv_k", "odd_ln_g", "odd_ln_b", "odd_sg_w", "odd_sg_b", "odd_w_out", "ffn_w1", "ffn_w2", "final_g")
    w = dict(zip(names, (norm_mix_g, norm_ffn_g, even_w_in, even_conv_k, even_conv_b, even_ln_g, even_ln_b, even_w_out,
                         odd_w_in, odd_conv_k, odd_ln_g, odd_ln_b, odd_sg_w, odd_sg_b, odd_w_out, ffn_w1, ffn_w2, final_g)))
    mom = dict(zip(names, (m_norm_mix_g, m_norm_ffn_g, m_even_w_in, m_even_conv_k, m_even_conv_b, m_even_ln_g, m_even_ln_b,
                           m_even_w_out, m_odd_w_in, m_odd_conv_k, m_odd_ln_g, m_odd_ln_b, m_odd_sg_w, m_odd_sg_b, m_odd_w_out,
                           m_ffn_w1, m_ffn_w2, m_final_g)))
    vel = dict(zip(names, (v_norm_mix_g, v_norm_ffn_g, v_even_w_in, v_even_conv_k, v_even_conv_b, v_even_ln_g, v_even_ln_b,
                           v_even_w_out, v_odd_w_in, v_odd_conv_k, v_odd_ln_g, v_odd_ln_b, v_odd_sg_w, v_odd_sg_b, v_odd_w_out,
                           v_ffn_w1, v_ffn_w2, v_final_g)))
    big_names = ("even_w_in", "even_w_out", "odd_w_in", "odd_w_out", "ffn_w1", "ffn_w2")
    chip = 2 * lax.axis_index("x") + lax.axis_index("y")

    def shard2d(t, name):
        return t[name].reshape(SHARD_OPERAND_SHAPES[big_names.index(name)])

    chip_op = jnp.reshape(chip, (1,)).astype(jnp.int32)
    small_pack = _pack([w[n] for n in SHARDED_NAMES], SHARDED_SHARD_ROWS)
    small_blk, small_land, small_send, small_recv, small_token = _exchange8_start("gather_small_start", small_pack)
    first = _cast_place(PIECES[0], shard2d(w, big_names[PIECES[0].src]), chip_op, tie=small_token)
    fly0, send0, recv0, token = _gather_start("gather_start_first", PIECES[:1], [first])
    placed = [_cast_place(pc, shard2d(w, big_names[pc.src]), chip_op, tie=token) for pc in PIECES[1:]]
    fly1, send1, recv1, all_started = _gather_start("gather_start_rest", PIECES[1:], placed)
    flying, gather_send, gather_recv = fly0 + fly1, send0 + send1, recv0 + recv1
    ready = {}

    names_in_order = [pc.name for pc in PIECES]

    riding = {}

    (*state_packs, _), idle_work_done = _touch(
        [_pack([t[n] for n in SMALL_NAMES], SMALL_ROWS) for t in (w, mom, vel)] + [all_started])

    def weight(name, after):
        if name in riding:
            job, k = riding.pop(name)
            ready[name] = job.results[k]
        if name not in ready:
            landed = _gather_wait(PIECES[0], flying[0], gather_send[0], gather_recv[0], idle_work_done)
            ready[name], = _core_forward(PIECES[:1], [landed])
        return ready[name]

    def before(call, after):
        if call in FORWARD_RIDES:
            group = FORWARD_RIDES[call]
            landed = [_gather_wait(PIECES[j], flying[j], gather_send[j], gather_recv[j], after) for j in group]
            job = _core_forward_job([PIECES[j] for j in group], landed)
            riding.update((PIECES[j].name, (job, k)) for k, j in enumerate(group))
            _ride_next_call(job)
        elif call == JOIN_RIDES_IN:
            early_join.append(join_job(JOIN_GROUPS[1], after))
            _ride_next_call(early_join[0])

    early_join = []

    def join_job(operands, after):
        pieces = [pc for pc in PIECES if pc.src in operands]
        done = {}
        for entry in list(scattering):
            if entry[0] in pieces:
                done[entry[0].name] = _scatter_wait(*entry, after)
                scattering.remove(entry)
        return _allsum_join_job(operands, [done[pc.name][0] for pc in pieces], [done[pc.name][1] for pc in pieces])

    scattering = []
    held = []

    core_op = jnp.reshape(lax.axis_index("c"), (1,)).astype(jnp.int32)

    def emit(name, a, b, tie=None):
        pc = PIECES[names_in_order.index(name)]
        held.append((pc, _mm_dw_chipsum(pc, a, b, core_op, tie)))
        if name in HOLD_BACK:
            return None
        pieces = [pc for pc, _ in held]
        started, token = _scatter_start(pieces, [chip_sum for _, chip_sum in held])
        scattering.extend((pc,) + tuple(st) for pc, st in zip(pieces, started))
        held.clear()
        return token

    full = {}
    small_blk, small_land = _exchange8_wait("gather_small_wait", small_blk, small_land, small_send, small_recv, all_started)
    gathered = _collect8("gather_small_collect", small_blk, small_land, False)
    gathered = gathered.reshape(4, 2, sum(SHARDED_SHARD_ROWS), LANES)[:, 0]
    r0 = 0
    for n, sh, rows, full_sh in zip(SHARDED_NAMES, SHARDED_SHARD_SHAPES, SHARDED_SHARD_ROWS, SHARDED_FULL_SHAPES):
        per_chip = gathered[:, r0:r0 + rows].reshape(4, -1)[:, :full_sh[0] * LANES].reshape(4, full_sh[0], LANES)
        full[n] = jnp.transpose(per_chip, (1, 0, 2)).reshape(full_sh)
        r0 += rows
    p = dict(full)
    p.update(norm_mix_g0=norm_mix_g[0:1], norm_mix_g1=norm_mix_g[1:2], norm_ffn_g0=norm_ffn_g[0:1], norm_ffn_g1=norm_ffn_g[1:2],
             even_conv_b=even_conv_b, even_ln_g=even_ln_g, even_ln_b=even_ln_b,
             odd_sg_w=odd_sg_w[0], odd_sg_bt=odd_sg_b[0].T, final_g=final_g[None, :])

    small = {}

    def emit_small(loss_row, g):
        parts = [loss_row, g["norm_mix_g1"], g["norm_ffn_g0"], g["norm_ffn_g1"], g["even_conv_b"], g["even_ln_g"],
                 g["even_ln_b"], g["odd_sg_w"], g["odd_sg_bt"].T, g["final_g"],
                 g["even_conv_k"], g["odd_conv_k"], g["odd_ln_g"], g["odd_ln_b"]]
        pack = _pack(parts, (8, 8, 8, 8) + REPL_ROWS[2:] + SHARDED_FULL_ROWS)
        small["blk"], small["land"], small["send"], small["recv"], token = _exchange8_start("allreduce_small_start", pack)
        return token

    dx, dg0 = _local_step(x[0], loss_target[0], p, weight, emit, emit_small, before)
    last_blk, last_land, last_send, last_recv, grad_token = _exchange8_start("allreduce_last_start", _pack([dg0], (8,)))

    big_grads = dict(zip((big_names[o] for o in JOIN_GROUPS[1]), early_join[0].results))
    delta, new_m, new_v, grads_big = {}, {}, {}, {}

    def adamw_big(n):
        d2, m2, v2, g2 = _adamw(f"adamw_{n}", shard2d(w, n), big_grads[n], shard2d(mom, n), shard2d(vel, n), True,
                                tie=grad_token)
        delta[n], new_m[n], new_v[n], grads_big[n] = (t.reshape(w[n].shape) for t in (d2, m2, v2, g2))

    for o in JOIN_GROUPS[1]:
        adamw_big(big_names[o])
    late_join = join_job(JOIN_GROUPS[0], new_v[big_names[JOIN_GROUPS[1][-1]]])
    big_grads.update(zip((big_names[o] for o in JOIN_GROUPS[0]), _run_job("allsum_join_late", late_join)))
    for o in JOIN_GROUPS[0]:
        adamw_big(big_names[o])

    joined_last = big_grads[big_names[JOIN_GROUPS[0][-1]]]
    grad_blk, grad_land = _exchange8_wait("allreduce_small_wait", small["blk"], small["land"], small["send"],
                                          small["recv"], joined_last)
    grad_sum = _collect8("allreduce_small_sum", grad_blk, grad_land, True)
    last_blk, last_land = _exchange8_wait("allreduce_last_wait", last_blk, last_land, last_send, last_recv, joined_last)
    dg0_sum = _collect8("allreduce_last_sum", last_blk, last_land, True)
    loss = grad_sum[0, 0]

    grads = dict(grads_big)
    for n, results in _adamw_small(*state_packs, grad_sum, dg0_sum).items():
        grads[n], delta[n], new_m[n], new_v[n] = (t.reshape(w[n].shape) for t in results)

    out = [loss, dx[None]]
    for res in (grads, delta, new_m, new_v):
        out.extend(res[n] for n in names)
    return tuple(out)
```

```python
import functools

import jax
import jax.numpy as jnp
from jax import lax
from jax.experimental import pallas as pl
from jax.experimental.pallas import tpu as pltpu

F32 = jnp.float32
BF16 = jnp.bfloat16

T = 2048
D = 1024
CONV_CH = 512
CONV_W = 31
HEAD_DIM = 64
ATT_W = 1536
EVEN_IN = 5632
ODD_IN = 2560
SCONV_W = 3
SG_GROUPS = 4
CHUNK = 128
D_FF = 4096
EPS = 1e-6
DILATIONS = (1, 4, 16)
BAND = 128
SCALE = HEAD_DIM ** -0.5
NEG = -1e30

ADAM_LR = 0.001
ADAM_B1 = 0.9
ADAM_B2 = 0.999
ADAM_EPS = 1e-08
ADAM_WD = 0.01
ADAM_STEP = 10

V7X_VMEM_BYTES = 64 * 2 ** 20
VMEM_LIMIT = V7X_VMEM_BYTES - 8 * 2 ** 20
LANES = 128
TOKEN_SHAPE = (8, LANES)


class _SideJob:
    def __init__(self, inputs, out_shape, scratch_shapes, aliases, begin, advance, finish):
        self.inputs, self.out_shape, self.scratch_shapes = list(inputs), list(out_shape), list(scratch_shapes)
        self.aliases, self.begin, self.advance, self.finish = dict(aliases), begin, advance, finish
        self.results = None


_PENDING_JOBS = []


def _ride_next_call(job):
    _PENDING_JOBS.append(job)


def _pcall(body, **kw):
    if not _PENDING_JOBS or "grid" not in kw:
        return pl.pallas_call(body, **kw)
    job = _PENDING_JOBS.pop()
    as_list = lambda v: list(v) if isinstance(v, (list, tuple)) else [v]
    single_out = not isinstance(kw["out_shape"], (list, tuple))
    in_specs, out_specs, out_shape = as_list(kw["in_specs"]), as_list(kw["out_specs"]), as_list(kw["out_shape"])
    scratch = list(kw.get("scratch_shapes", ()))
    grid = kw["grid"]
    n_steps = 1
    for extent in grid:
        n_steps *= extent
    assert n_steps >= 3
    n_in, n_out, n_scr = len(in_specs), len(out_specs), len(scratch)
    j_in, j_out = len(job.inputs), len(job.out_shape)
    any_spec = pl.BlockSpec(memory_space=pl.ANY)

    def hosted(*refs):
        ins, j_ins = refs[:n_in], refs[n_in:n_in + j_in]
        outs = refs[n_in + j_in:n_in + j_in + n_out]
        j_outs = refs[n_in + j_in + n_out:n_in + j_in + n_out + j_out]
        scr = refs[n_in + j_in + n_out + j_out:n_in + j_in + n_out + j_out + n_scr]
        j_scr = refs[n_in + j_in + n_out + j_out + n_scr:]
        step = pl.program_id(0)
        for axis in range(1, len(grid)):
            step = step * grid[axis] + pl.program_id(axis)

        @pl.when(step == 0)
        def _():
            job.begin(j_ins, j_outs, j_scr)

        @pl.when(step == 1)
        def _():
            job.advance(j_ins, j_outs, j_scr)

        body(*ins, *outs, *scr)

        @pl.when(step == n_steps - 1)
        def _():
            job.finish(j_ins, j_outs, j_scr)

    aliases = dict(kw.get("input_output_aliases", {}))
    aliases.update({n_in + a: n_out + b for a, b in job.aliases.items()})
    call = pl.pallas_call(
        hosted, name=kw["name"], grid=grid,
        in_specs=in_specs + [any_spec] * j_in, out_specs=out_specs + [any_spec] * j_out,
        out_shape=out_shape + job.out_shape, scratch_shapes=scratch + job.scratch_shapes,
        input_output_aliases=aliases,
        compiler_params=pltpu.CompilerParams(dimension_semantics=("arbitrary",) * len(grid), vmem_limit_bytes=VMEM_LIMIT))

    def run(*args):
        res = call(*args, *job.inputs)
        job.results = list(res[n_out:])
        return res[0] if single_out else list(res[:n_out])

    return run


def _params(*sem):
    return pltpu.CompilerParams(dimension_semantics=sem, vmem_limit_bytes=VMEM_LIMIT)


def _dot(a, b, dims):
    return lax.dot_general(a, b, (dims, ((), ())), preferred_element_type=F32)


def _nn(a, b):
    return _dot(a, b, ((1,), (0,)))


def _nt(a, b):
    return _dot(a, b, ((1,), (1,)))


def _tn(a, b):
    return _dot(a, b, ((0,), (0,)))


def _sigmoid(x):
    return 1.0 / (1.0 + jnp.exp(-x))


MM_VMEM_BUDGET = 40 * 2 ** 20


def _mm_tiles(mode, m, n, k, a_bytes, b_bytes, extra_bytes, out_bytes):
    def divisors(total, unit):
        return [t for t in range(unit, total + 1, unit) if total % t == 0]

    best = None
    for tm in divisors(m, LANES if mode == "tn" else 8):
        for tn in divisors(n, LANES):
            blocks = tm * k * a_bytes + tn * k * b_bytes + tm * tn * (extra_bytes + out_bytes)
            casts = (tm * k * 2 if a_bytes == 4 else 0) + (tn * k * 2 if b_bytes == 4 else 0)
            if 2 * blocks + casts + tm * tn * 4 > MM_VMEM_BUDGET:
                continue
            key = ((m // tm) * (n // tn), (m // tm) * n * k * b_bytes, abs(tm - tn))
            if best is None or key < best[0]:
                best = (key, tm, tn)
    return best[1], best[2]


def _mm(name, mode, a, b, m, n, k, out_dtypes, *, b_off=0, extras=(), epi=None, tie=None):
    tm, tn = _mm_tiles(mode, m, n, k, a.dtype.itemsize, b.dtype.itemsize, sum(e.dtype.itemsize for e in extras),
                       sum(jnp.dtype(dt).itemsize for dt in out_dtypes))
    assert b_off % tn == 0
    b_off //= tn
    if mode == "nn":
        a_spec = pl.BlockSpec((tm, k), lambda i, j: (i, 0))
        b_spec = pl.BlockSpec((k, tn), lambda i, j: (0, j + b_off))
        dims = ((1,), (0,))
    elif mode == "nt":
        a_spec = pl.BlockSpec((tm, k), lambda i, j: (i, 0))
        b_spec = pl.BlockSpec((tn, k), lambda i, j: (j, 0))
        dims = ((1,), (1,))
    else:
        a_spec = pl.BlockSpec((k, tm), lambda i, j: (0, i))
        b_spec = pl.BlockSpec((k, tn), lambda i, j: (0, j))
        dims = ((0,), (0,))
    o_spec = pl.BlockSpec((tm, tn), lambda i, j: (i, j))
    n_extra = len(extras)
    ties = () if tie is None else (tie,)

    def body(a_ref, b_ref, *rest):
        rest = rest[len(ties):]
        acc = _dot(a_ref[...].astype(BF16), b_ref[...].astype(BF16), dims)
        vals = epi(acc, *[e[...] for e in rest[:n_extra]]) if epi is not None else (acc,)
        for o_ref, v in zip(rest[n_extra:], vals):
            o_ref[...] = v.astype(o_ref.dtype)

    outs = _pcall(
        body, name=name, grid=(m // tm, n // tn),
        in_specs=[a_spec, b_spec] + [pl.BlockSpec(TOKEN_SHAPE, lambda i, j: (0, 0))] * len(ties) + [o_spec] * n_extra,
        out_specs=[o_spec] * len(out_dtypes),
        out_shape=[jax.ShapeDtypeStruct((m, n), dt) for dt in out_dtypes],
        compiler_params=_params("parallel", "parallel"),
    )(a, b, *ties, *extras)
    return outs[0] if len(out_dtypes) == 1 else outs


def _row_tile(k, a_bytes, n_row_blocks):
    for tm in (1024, 512, 256, 128):
        if 2 * (tm * k * a_bytes + D * k * 2 + n_row_blocks * tm * D * 4) + tm * D * 4 <= MM_VMEM_BUDGET + 4 * 2 ** 20:
            return tm
    raise ValueError("no row tile fits")


FFN0_DOWN_TILE = 256


def _mm_out_norm(name, a, b, k, res, g_next, tm=None):
    tm = tm or _row_tile(k, a.dtype.itemsize, 3)

    def body(a_ref, b_ref, r_ref, g_ref, h_ref, hn_ref):
        h = _nn(a_ref[...].astype(BF16), b_ref[...]) + r_ref[...]
        h_ref[...] = h
        r = lax.rsqrt(jnp.mean(h * h, axis=-1, keepdims=True) + EPS)
        hn_ref[...] = ((h * r) * g_ref[...]).astype(BF16)

    row = pl.BlockSpec((tm, D), lambda i: (i, 0))
    return _pcall(
        body, name=name, grid=(T // tm,),
        in_specs=[pl.BlockSpec((tm, k), lambda i: (i, 0)), pl.BlockSpec((k, D), lambda i: (0, 0)), row,
                  pl.BlockSpec((1, D), lambda i: (0, 0))],
        out_specs=[row, row],
        out_shape=[jax.ShapeDtypeStruct((T, D), F32), jax.ShapeDtypeStruct((T, D), BF16)],
        compiler_params=_params("parallel"),
    )(a, b, res, g_next)


def _mm_out_loss(name, a, b, k, res, g, target):
    tm = _row_tile(k, a.dtype.itemsize, 3)

    def body(a_ref, b_ref, r_ref, g_ref, t_ref, dh_ref, dg_ref, loss_ref):
        x = _nn(a_ref[...].astype(BF16), b_ref[...]) + r_ref[...]
        r = lax.rsqrt(jnp.mean(x * x, axis=-1, keepdims=True) + EPS)
        nrm = x * r
        gain = g_ref[...]
        err = nrm * gain - t_ref[...]
        dy = err * (1.0 / D)
        dn = dy * gain
        dh_ref[...] = r * (dn - nrm * jnp.mean(dn * nrm, axis=-1, keepdims=True))

        @pl.when(pl.program_id(0) == 0)
        def _():
            dg_ref[...] = jnp.zeros_like(dg_ref)
            loss_ref[...] = jnp.zeros_like(loss_ref)

        dg_ref[...] += jnp.sum(dy * nrm, axis=0, keepdims=True)
        part = jnp.sum(jnp.sum(err * err, axis=1, keepdims=True), axis=0, keepdims=True) * (0.5 / D)
        loss_ref[...] += jnp.broadcast_to(part, (1, LANES))

    row = pl.BlockSpec((tm, D), lambda i: (i, 0))
    vec = pl.BlockSpec((1, D), lambda i: (0, 0))
    return _pcall(
        body, name=name, grid=(T // tm,),
        in_specs=[pl.BlockSpec((tm, k), lambda i: (i, 0)), pl.BlockSpec((k, D), lambda i: (0, 0)), row, vec, row],
        out_specs=[row, vec, pl.BlockSpec((1, LANES), lambda i: (0, 0))],
        out_shape=[jax.ShapeDtypeStruct((T, D), F32), jax.ShapeDtypeStruct((1, D), F32),
                   jax.ShapeDtypeStruct((1, LANES), F32)],
        compiler_params=_params("arbitrary"),
    )(a, b, res, g, target)


def _mm_dx_norm(name, dz, w, k, h, g, dres, tie=None):
    tm = _row_tile(k, dz.dtype.itemsize, 3)
    ties = () if tie is None else (tie,)

    def body(a_ref, b_ref, *rest):
        h_ref, g_ref, r_ref, dh_ref, dg_ref = rest[len(ties):]
        dy = _nt(a_ref[...].astype(BF16), b_ref[...])
        x = h_ref[...]
        r = lax.rsqrt(jnp.mean(x * x, axis=-1, keepdims=True) + EPS)
        nrm = x * r
        dn = dy * g_ref[...]
        dh_ref[...] = r_ref[...] + r * (dn - nrm * jnp.mean(dn * nrm, axis=-1, keepdims=True))

        @pl.when(pl.program_id(0) == 0)
        def _():
            dg_ref[...] = jnp.zeros_like(dg_ref)

        dg_ref[...] += jnp.sum(dy * nrm, axis=0, keepdims=True)

    row = pl.BlockSpec((tm, D), lambda i: (i, 0))
    vec = pl.BlockSpec((1, D), lambda i: (0, 0))
    return _pcall(
        body, name=name, grid=(T // tm,),
        in_specs=[pl.BlockSpec((tm, k), lambda i: (i, 0)), pl.BlockSpec((D, k), lambda i: (0, 0))]
        + [pl.BlockSpec(TOKEN_SHAPE, lambda i: (0, 0))] * len(ties) + [row, vec, row],
        out_specs=[row, vec],
        out_shape=[jax.ShapeDtypeStruct((T, D), F32), jax.ShapeDtypeStruct((1, D), F32)],
        compiler_params=_params("arbitrary"),
    )(dz, w, *ties, h, g, dres)


def _rms_fwd(name, h, g, tm=512):
    def body(h_ref, g_ref, o_ref):
        x = h_ref[...]
        r = lax.rsqrt(jnp.mean(x * x, axis=-1, keepdims=True) + EPS)
        o_ref[...] = ((x * r) * g_ref[...]).astype(BF16)

    return _pcall(
        body, name=name, grid=(T // tm,),
        in_specs=[pl.BlockSpec((tm, D), lambda i: (i, 0)), pl.BlockSpec((1, D), lambda i: (0, 0))],
        out_specs=pl.BlockSpec((tm, D), lambda i: (i, 0)),
        out_shape=jax.ShapeDtypeStruct((T, D), BF16),
        compiler_params=_params("parallel"),
    )(h, g)


CONV_TILE = 256
CONV_HALO = 32


def _glu(z):
    return z[:, :CONV_CH] * _sigmoid(z[:, CONV_CH:])


SUBLANES = 8


def _sublane_shifts(win):
    n = win.shape[0]
    return [win] + [win[r:r + n - SUBLANES, :] for r in range(1, SUBLANES)]


def _rows_from(shifts, off, n):
    q, r = divmod(off, SUBLANES)
    return shifts[r][q * SUBLANES:q * SUBLANES + n, :]


def _econv_fwd(zc, conv_k, conv_b, ln_g, ln_b):
    R, H = CONV_TILE, CONV_HALO

    def body(z_ref, zh_ref, k_ref, b_ref, g_ref, be_ref, cv_ref, cat_ref):
        i = pl.program_id(0)
        glu = _glu(z_ref[...])
        halo = _glu(zh_ref[...]) * (i > 0).astype(F32)
        win = _sublane_shifts(jnp.concatenate([halo, glu], axis=0))
        acc = jnp.zeros((R, CONV_CH), F32) + b_ref[...]
        for j in range(CONV_W):
            acc = acc + k_ref[j:j + 1, :] * _rows_from(win, H - (CONV_W - 1) + j, R)
        cv_ref[...] = acc
        mu = jnp.mean(acc, axis=-1, keepdims=True)
        xc = acc - mu
        rstd = lax.rsqrt(jnp.mean(xc * xc, axis=-1, keepdims=True) + EPS)
        ln = xc * rstd * g_ref[...] + be_ref[...]
        cat_ref[...] = (ln * _sigmoid(ln)).astype(BF16)

    vec = pl.BlockSpec((1, CONV_CH), lambda i: (0, 0))
    return _pcall(
        body, name="econv_fwd", grid=(T // R,),
        in_specs=[pl.BlockSpec((R, 2 * CONV_CH), lambda i: (i, 0)),
                  pl.BlockSpec((H, 2 * CONV_CH), lambda i: (jnp.maximum(i * (R // H) - 1, 0), 0)),
                  pl.BlockSpec((CONV_W, CONV_CH), lambda i: (0, 0)), vec, vec, vec],
        out_specs=[pl.BlockSpec((R, CONV_CH), lambda i: (i, 0)), pl.BlockSpec((R, CONV_CH), lambda i: (i, 0))],
        out_shape=[jax.ShapeDtypeStruct((T, CONV_CH), F32), jax.ShapeDtypeStruct((T, D), BF16)],
        compiler_params=_params("parallel"),
    )(zc, zc, conv_k, conv_b, ln_g, ln_b)


def _econv_bwd_ln(cv, dcat, ln_g, ln_b):
    R = CONV_TILE

    def body(cv_ref, d_ref, g_ref, be_ref, dcv_ref, dg_ref, dbe_ref, dcb_ref):
        cv_t = cv_ref[...]
        mu = jnp.mean(cv_t, axis=-1, keepdims=True)
        xc = cv_t - mu
        rstd = lax.rsqrt(jnp.mean(xc * xc, axis=-1, keepdims=True) + EPS)
        xh = xc * rstd
        ln = xh * g_ref[...] + be_ref[...]
        sg = _sigmoid(ln)
        dln = d_ref[...] * (sg * (1.0 + ln * (1.0 - sg)))
        dxh = dln * g_ref[...]
        dcv = rstd * (dxh - jnp.mean(dxh, axis=-1, keepdims=True) - xh * jnp.mean(dxh * xh, axis=-1, keepdims=True))
        dcv_ref[...] = dcv

        @pl.when(pl.program_id(0) == 0)
        def _():
            dg_ref[...] = jnp.zeros_like(dg_ref)
            dbe_ref[...] = jnp.zeros_like(dbe_ref)
            dcb_ref[...] = jnp.zeros_like(dcb_ref)

        dg_ref[...] += jnp.sum(dln * xh, axis=0, keepdims=True)
        dbe_ref[...] += jnp.sum(dln, axis=0, keepdims=True)
        dcb_ref[...] += jnp.sum(dcv, axis=0, keepdims=True)

    vec = pl.BlockSpec((1, CONV_CH), lambda i: (0, 0))
    row = pl.BlockSpec((R, CONV_CH), lambda i: (i, 0))
    vshape = jax.ShapeDtypeStruct((1, CONV_CH), F32)
    return _pcall(
        body, name="econv_bwd_ln", grid=(T // R,),
        in_specs=[row, row, vec, vec], out_specs=[row, vec, vec, vec],
        out_shape=[jax.ShapeDtypeStruct((T, CONV_CH), F32), vshape, vshape, vshape],
        compiler_params=_params("arbitrary"),
    )(cv, dcat, ln_g, ln_b)


def _econv_bwd_conv(dcv, zc, conv_k):
    R, H = CONV_TILE, CONV_HALO
    last = T // R - 1

    def body(d_ref, dn_ref, z_ref, zh_ref, k_ref, dz_ref, dk_ref):
        i = pl.program_id(0)
        z = z_ref[...]
        a_lin = z[:, :CONV_CH]
        sg = _sigmoid(z[:, CONV_CH:])
        glu = a_lin * sg
        halo = _glu(zh_ref[...]) * (i > 0).astype(F32)
        win = _sublane_shifts(jnp.concatenate([halo, glu], axis=0))
        dcv_t = d_ref[...]
        nxt = dn_ref[...] * (i < last).astype(F32)
        winb = _sublane_shifts(jnp.concatenate([dcv_t, nxt], axis=0))

        @pl.when(i == 0)
        def _():
            dk_ref[...] = jnp.zeros_like(dk_ref)

        dglu = jnp.zeros((R, CONV_CH), F32)
        for j in range(CONV_W):
            dk_ref[j:j + 1, :] += jnp.sum(dcv_t * _rows_from(win, H - (CONV_W - 1) + j, R), axis=0, keepdims=True)
            dglu = dglu + k_ref[j:j + 1, :] * _rows_from(winb, CONV_W - 1 - j, R)
        dz_ref[...] = jnp.concatenate([dglu * sg, dglu * a_lin * sg * (1.0 - sg)], axis=1).astype(BF16)

    return _pcall(
        body, name="econv_bwd_conv", grid=(T // R,),
        in_specs=[pl.BlockSpec((R, CONV_CH), lambda i: (i, 0)),
                  pl.BlockSpec((H, CONV_CH), lambda i: (jnp.minimum((i + 1) * (R // H), T // H - 1), 0)),
                  pl.BlockSpec((R, 2 * CONV_CH), lambda i: (i, 0)),
                  pl.BlockSpec((H, 2 * CONV_CH), lambda i: (jnp.maximum(i * (R // H) - 1, 0), 0)),
                  pl.BlockSpec((CONV_W, CONV_CH), lambda i: (0, 0))],
        out_specs=[pl.BlockSpec((R, 2 * CONV_CH), lambda i: (i, 0)), pl.BlockSpec((CONV_W, CONV_CH), lambda i: (0, 0))],
        out_shape=[jax.ShapeDtypeStruct((T, EVEN_IN), BF16), jax.ShapeDtypeStruct((CONV_W, CONV_CH), F32)],
        compiler_params=_params("arbitrary"),
    )(dcv, dcv, zc, zc, conv_k)


def _swap_halves(v):
    lane = lax.broadcasted_iota(jnp.int32, v.shape, 1)
    return jnp.where((lane % HEAD_DIM) < HEAD_DIM // 2, pltpu.roll(v, LANES - HEAD_DIM // 2, 1),
                     pltpu.roll(v, HEAD_DIM // 2, 1))


def _qkv_proj(hn, w_in, rope_c, rope_s, tm=T):
    tn = 4 * LANES

    def body(a_ref, b_ref, c_ref, s_ref, o_ref):
        j = pl.program_id(1)
        acc = _nn(a_ref[...], b_ref[...])
        for p in range(4):
            v = acc[:, p * LANES:(p + 1) * LANES]
            rot = v * c_ref[...] + _swap_halves(v) * s_ref[...]
            o_ref[p] = jnp.where(j < 6, rot, v)

    tab = pl.BlockSpec((tm, LANES), lambda i, j: (i, 0))
    return _pcall(
        body, name="qkv_proj", grid=(T // tm, 9),
        in_specs=[pl.BlockSpec((tm, D), lambda i, j: (i, 0)),
                  pl.BlockSpec((D, tn), lambda i, j: (0, j + (2 * CONV_CH) // tn)), tab, tab],
        out_specs=pl.BlockSpec((None, 4, tm, LANES), lambda i, j: (j, 0, i, 0)),
        out_shape=jax.ShapeDtypeStruct((9, 4, T, LANES), F32),
        compiler_params=_params("parallel", "parallel"),
    )(hn, w_in, rope_c, rope_s)


ATTN_FWD_UNROLL = 4
ATTN_BWD_UNROLL = 4


def _band_rows(start, d):
    if d == 1:
        return pl.ds(pl.multiple_of(start, BAND), BAND)
    return pl.ds(start, BAND, stride=d)


def _band_masks(n):
    row = lax.broadcasted_iota(jnp.int32, (BAND, BAND), 0)
    col = lax.broadcasted_iota(jnp.int32, (BAND, BAND), 1)
    no_prev = (n == 0).astype(jnp.int32) * (2 * BAND)
    return col <= row, col >= row + no_prev


def _attn_fwd(qkv, g):
    d = DILATIONS[g]
    nb = T // d // BAND
    has_prev = nb > 1

    def body(q_ref, k_ref, v_ref, o_ref, l_ref):
        lane_lo = lax.broadcasted_iota(jnp.int32, (BAND, LANES), 1) < HEAD_DIM

        heads = (lane_lo, jnp.logical_not(lane_lo))
        ones = jnp.ones((BAND, LANES), BF16)

        def step(it, carry):
            tiles = []
            for u in range(ATTN_FWD_UNROLL):
                idx = it * ATTN_FWD_UNROLL + u
                r = idx // nb
                n = idx % nb
                cur = _band_rows(n * (BAND * d) + r, d)
                prev = _band_rows(jnp.maximum(n - 1, 0) * (BAND * d) + r, d)
                mc, mp = _band_masks(n)
                kp = k_ref[prev, :].astype(BF16) if has_prev else None
                vp = v_ref[prev, :].astype(BF16) if has_prev else None
                tiles.append((cur, mc, mp, q_ref[cur, :], k_ref[cur, :].astype(BF16), v_ref[cur, :].astype(BF16), kp, vp))
            scores = []
            for cur, mc, mp, q, kc, vc, kp, vp in tiles:
                for hm in heads:
                    qm = jnp.where(hm, q, 0.0).astype(BF16)
                    sc = jnp.where(mc, _nt(qm, kc) * SCALE, NEG)
                    scores.append((sc, jnp.where(mp, _nt(qm, kp) * SCALE, NEG)) if has_prev else (sc,))
            maxes = [functools.reduce(jnp.maximum, [jnp.max(sx, axis=1, keepdims=True) for sx in ss]) for ss in scores]
            probs = [[jnp.exp(sx - mx).astype(BF16) for sx in ss] for ss, mx in zip(scores, maxes)]
            dens = [functools.reduce(jnp.add, [_nn(px, ones) for px in ps]) for ps in probs]
            for t, (cur, mc, mp, q, kc, vc, kp, vp) in enumerate(tiles):
                outs, lses = [], []
                for h in range(2):
                    ps = probs[2 * t + h]
                    acc = _nn(ps[0], vc) + _nn(ps[1], vp) if has_prev else _nn(ps[0], vc)
                    outs.append(acc / dens[2 * t + h])
                    lses.append(maxes[2 * t + h] + jnp.log(dens[2 * t + h]))
                o_ref[cur, :] = jnp.where(lane_lo, outs[0], outs[1])
                l_ref[cur, :] = jnp.where(lane_lo, lses[0], lses[1])
            return carry

        lax.fori_loop(0, d * nb // ATTN_FWD_UNROLL, step, 0)

    def slab(which):
        return pl.BlockSpec((None, None, T, LANES), lambda p: (which * 3 + g, p, 0, 0))

    out = pl.BlockSpec((None, T, LANES), lambda p: (p, 0, 0))
    shape = jax.ShapeDtypeStruct((4, T, LANES), F32)
    return _pcall(
        body, name=f"attn_fwd{g}", grid=(4,),
        in_specs=[slab(0), slab(1), slab(2)], out_specs=[out, out], out_shape=[shape, shape],
        compiler_params=_params("parallel"),
    )(qkv, qkv, qkv)


def _attn_merge(outs, lses, cat, tm=1024):
    def body(o0, o1, o2, l0, l1, l2, cat_in, cat_ref, att_ref, w0, w1, w2):
        del cat_in
        la, lb, lc = l0[...], l1[...], l2[...]
        mx = jnp.maximum(jnp.maximum(la, lb), lc)
        ea, eb, ec = jnp.exp(la - mx), jnp.exp(lb - mx), jnp.exp(lc - mx)
        inv = 1.0 / (ea + eb + ec)
        wa, wb, wc = ea * inv, eb * inv, ec * inv
        att = wa * o0[...] + wb * o1[...] + wc * o2[...]
        att_ref[...] = att
        cat_ref[...] = att.astype(BF16)
        w0[...] = wa
        w1[...] = wb
        w2[...] = wc

    slab = pl.BlockSpec((None, tm, LANES), lambda p, i: (p, i, 0))
    shape = jax.ShapeDtypeStruct((4, T, LANES), F32)
    return _pcall(
        body, name="attn_merge", grid=(4, T // tm),
        in_specs=[slab] * 6 + [pl.BlockSpec(memory_space=pl.ANY)],
        out_specs=[pl.BlockSpec((tm, LANES), lambda p, i: (i, CONV_CH // LANES + p)), slab, slab, slab, slab],
        out_shape=[jax.ShapeDtypeStruct((T, D), BF16), shape, shape, shape, shape],
        input_output_aliases={6: 0},
        compiler_params=_params("parallel", "parallel"),
    )(*outs, *lses, cat)


def _attn_bwd(qkv, lse, wgt, att, dcat, dqkv, g):
    d = DILATIONS[g]
    nb = T // d // BAND
    has_prev = nb > 1

    def body(q_ref, k_ref, v_ref, l_ref, w_ref, a_ref, da_ref, dq_in, o_ref):
        del dq_in
        lane = lax.broadcasted_iota(jnp.int32, (BAND, LANES), 1)
        lane_lo = lane < HEAD_DIM
        row = lax.broadcasted_iota(jnp.int32, (LANES, LANES), 0)
        same_head = ((row // HEAD_DIM) == (lane // HEAD_DIM)).astype(BF16)
        dq_ref, dk_ref, dv_ref = o_ref.at[0], o_ref.at[1], o_ref.at[2]
        if has_prev:
            dk_ref[...] = jnp.zeros((T, LANES), F32)
            dv_ref[...] = jnp.zeros((T, LANES), F32)

        heads = (lane_lo, jnp.logical_not(lane_lo))

        def step(it, carry):
            tiles = []
            for u in range(ATTN_BWD_UNROLL):
                idx = it * ATTN_BWD_UNROLL + u
                r = idx // nb
                n = idx % nb
                cur = _band_rows(n * (BAND * d) + r, d)
                prev = _band_rows(jnp.maximum(n - 1, 0) * (BAND * d) + r, d)
                mc, mp = _band_masks(n)
                da = da_ref[cur, :]
                prod = da * a_ref[cur, :]
                hi = prod.astype(BF16)
                lo = (prod - hi.astype(F32)).astype(BF16)
                tiles.append(dict(cur=cur, prev=prev, mc=mc, mp=mp, da=da, hi=hi, lo=lo, q=q_ref[cur, :],
                                  kc=k_ref[cur, :].astype(BF16), vc=v_ref[cur, :].astype(BF16),
                                  kp=k_ref[prev, :].astype(BF16) if has_prev else None,
                                  vp=v_ref[prev, :].astype(BF16) if has_prev else None,
                                  lse=l_ref[cur, :], w=w_ref[cur, :]))
            for t in tiles:
                t["csum"] = _nn(t["hi"], same_head) + _nn(t["lo"], same_head)
            chains = []
            for t in tiles:
                for h, hm in enumerate(heads):
                    qm = jnp.where(hm, t["q"], 0.0).astype(BF16)
                    dam = jnp.where(hm, t["da"], 0.0).astype(BF16)
                    ch = dict(t=t, h=h, qm=qm, dam=dam, sc=jnp.where(t["mc"], _nt(qm, t["kc"]) * SCALE, NEG),
                              dpc=_nt(dam, t["vc"]))
                    if has_prev:
                        ch.update(sp=jnp.where(t["mp"], _nt(qm, t["kp"]) * SCALE, NEG), dpp=_nt(dam, t["vp"]))
                    chains.append(ch)
            for ch in chains:
                t, col0 = ch["t"], ch["h"] * HEAD_DIM
                lse_h = t["lse"][:, col0:col0 + 1]
                w_h = t["w"][:, col0:col0 + 1]
                c_h = t["csum"][:, col0:col0 + 1]
                pwc = w_h * jnp.exp(ch["sc"] - lse_h)
                ch["dsc"] = (pwc * (ch["dpc"] - c_h) * SCALE).astype(BF16)
                ch["pwc"] = pwc.astype(BF16)
                if has_prev:
                    pwp = w_h * jnp.exp(ch["sp"] - lse_h)
                    ch["dsp"] = (pwp * (ch["dpp"] - c_h) * SCALE).astype(BF16)
                    ch["pwp"] = pwp.astype(BF16)
            for ch in chains:
                t = ch["t"]
                ch["dq"] = _nn(ch["dsc"], t["kc"])
                ch["dkc"] = _tn(ch["dsc"], ch["qm"])
                ch["dvc"] = _tn(ch["pwc"], ch["dam"])
                if has_prev:
                    ch["dq"] = ch["dq"] + _nn(ch["dsp"], t["kp"])
                    ch["dkp"] = _tn(ch["dsp"], ch["qm"])
                    ch["dvp"] = _tn(ch["pwp"], ch["dam"])
            for i, t in enumerate(tiles):
                c0, c1 = chains[2 * i], chains[2 * i + 1]
                dq_ref[t["cur"], :] = jnp.where(lane_lo, c0["dq"], c1["dq"])
                if has_prev:
                    dk_ref[t["cur"], :] += c0["dkc"] + c1["dkc"]
                    dk_ref[t["prev"], :] += c0["dkp"] + c1["dkp"]
                    dv_ref[t["cur"], :] += c0["dvc"] + c1["dvc"]
                    dv_ref[t["prev"], :] += c0["dvp"] + c1["dvp"]
                else:
                    dk_ref[t["cur"], :] = c0["dkc"] + c1["dkc"]
                    dv_ref[t["cur"], :] = c0["dvc"] + c1["dvc"]
            return carry

        lax.fori_loop(0, d * nb // ATTN_BWD_UNROLL, step, 0)

    def slab(which):
        return pl.BlockSpec((None, None, T, LANES), lambda p: (which * 3 + g, p, 0, 0))

    per_pair = pl.BlockSpec((None, T, LANES), lambda p: (p, 0, 0))
    return _pcall(
        body, name=f"attn_bwd{g}", grid=(4,),
        in_specs=[slab(0), slab(1), slab(2), per_pair, per_pair, per_pair,
                  pl.BlockSpec((T, LANES), lambda p: (0, CONV_CH // LANES + p)),
                  pl.BlockSpec(memory_space=pl.ANY)],
        out_specs=pl.BlockSpec((None, 3, None, T, LANES), lambda p: (g, 0, p, 0, 0)),
        out_shape=jax.ShapeDtypeStruct((3, 3, 4, T, LANES), F32),
        input_output_aliases={7: 0},
        compiler_params=_params("parallel"),
    )(qkv, qkv, qkv, lse, wgt, att, dcat, dqkv)


def _rope_bwd(dqkv, rope_c, rope_s, dz):
    wide = 4 * LANES

    def body(d_ref, c_ref, s_ref, dz_in, o_ref):
        del dz_in
        w = pl.program_id(1)
        for p in range(4):
            v = d_ref[p]
            rot = v * c_ref[...] + _swap_halves(v * s_ref[...])
            o_ref[:, p * LANES:(p + 1) * LANES] = jnp.where(w < 2, rot, v).astype(BF16)

    tab = pl.BlockSpec((T, LANES), lambda g, w: (0, 0))
    return _pcall(
        body, name="rope_bwd", grid=(3, 3),
        in_specs=[pl.BlockSpec((None, None, 4, T, LANES), lambda g, w: (g, w, 0, 0, 0)), tab, tab,
                  pl.BlockSpec(memory_space=pl.ANY)],
        out_specs=pl.BlockSpec((T, wide), lambda g, w: (0, (2 * CONV_CH) // wide + w * 3 + g)),
        out_shape=jax.ShapeDtypeStruct((T, EVEN_IN), BF16),
        input_output_aliases={3: 0},
        compiler_params=_params("parallel", "parallel"),
    )(dqkv, rope_c, rope_s, dz)


ODD_TILE = 256
ODD_HALO = 8
GELU_C = 0.7978845608028654
GELU_A = 0.044715


def _gelu(x):
    return 0.5 * x * (1.0 + jnp.tanh(GELU_C * (x + GELU_A * x * x * x)))


def _gelu_grad(x):
    th = jnp.tanh(GELU_C * (x + GELU_A * x * x * x))
    return 0.5 * (1.0 + th) + 0.5 * x * (1.0 - th * th) * GELU_C * (1.0 + 3.0 * GELU_A * x * x)


def _tril():
    row = lax.broadcasted_iota(jnp.int32, (CHUNK, CHUNK), 0)
    col = lax.broadcasted_iota(jnp.int32, (CHUNK, CHUNK), 1)
    return (col <= row).astype(F32)


def _odd_parts(z, zh, i, k_ref, g_ref, be_ref, w_ref, bt_ref):
    R, H = ODD_TILE, ODD_HALO
    gb, gc, xs, uv = z[:, :512], z[:, 512:1024], z[:, 1024:1536], z[:, 1536:]
    halo = zh[:, 512:1024] * zh[:, 1024:1536] * (i > 0).astype(F32)
    win = jnp.concatenate([halo, gc * xs], axis=0)
    cv = jnp.zeros((R, 512), F32)
    for j in range(SCONV_W):
        off = H - (SCONV_W - 1) + j
        cv = cv + k_ref[j:j + 1, :] * win[off:off + R, :]
    ge = _gelu(uv)
    u, v = ge[:, :512], ge[:, 512:]
    mu = jnp.mean(v, axis=-1, keepdims=True)
    xc = v - mu
    rstd = lax.rsqrt(jnp.mean(xc * xc, axis=-1, keepdims=True) + EPS)
    xh = xc * rstd
    vn = xh * g_ref[...] + be_ref[...]
    tril = _tril()
    wms = [(w_ref[g] * tril).astype(BF16) for g in range(SG_GROUPS)]
    rows = []
    for ci in range(R // CHUNK):
        blocks = []
        for g in range(SG_GROUPS):
            blk = vn[ci * CHUNK:(ci + 1) * CHUNK, g * LANES:(g + 1) * LANES].astype(BF16)
            blocks.append(_nn(wms[g], blk) + bt_ref[:, g:g + 1])
        rows.append(jnp.concatenate(blocks, axis=1))
    vmix = jnp.concatenate(rows, axis=0)
    return gb, gc, xs, uv, win, cv, u, rstd, xh, vn, vmix, wms


def _odd_mid_fwd(z, conv_k, ln_g, ln_b, sg_w, sg_bt):
    R, H = ODD_TILE, ODD_HALO

    def body(z_ref, zh_ref, k_ref, g_ref, be_ref, w_ref, bt_ref, o_ref):
        i = pl.program_id(0)
        gb, _, _, _, _, cv, u, _, _, _, vmix, _ = _odd_parts(z_ref[...], zh_ref[...], i, k_ref, g_ref, be_ref, w_ref, bt_ref)
        o_ref[...] = jnp.concatenate([gb * cv, u * vmix], axis=1).astype(BF16)

    vec = pl.BlockSpec((1, 512), lambda i: (0, 0))
    return _pcall(
        body, name="odd_mid_fwd", grid=(T // R,),
        in_specs=[pl.BlockSpec((R, ODD_IN), lambda i: (i, 0)),
                  pl.BlockSpec((H, ODD_IN), lambda i: (jnp.maximum(i * (R // H) - 1, 0), 0)),
                  pl.BlockSpec((SCONV_W, 512), lambda i: (0, 0)), vec, vec,
                  pl.BlockSpec((SG_GROUPS, CHUNK, CHUNK), lambda i: (0, 0, 0)),
                  pl.BlockSpec((CHUNK, SG_GROUPS), lambda i: (0, 0))],
        out_specs=pl.BlockSpec((R, D), lambda i: (i, 0)),
        out_shape=jax.ShapeDtypeStruct((T, D), BF16),
        compiler_params=_params("parallel"),
    )(z, z, conv_k, ln_g, ln_b, sg_w, sg_bt)


def _odd_mid_bwd(z, dcat, conv_k, ln_g, ln_b, sg_w, sg_bt):
    R, H = ODD_TILE, ODD_HALO
    last = T // R - 1

    def body(z_ref, zh_ref, zn_ref, d_ref, dn_ref, k_ref, g_ref, be_ref, w_ref, bt_ref,
             dz_ref, dk_ref, dg_ref, dbe_ref, dw_ref, dbt_ref):
        i = pl.program_id(0)
        z = z_ref[...]
        gb, gc, xs, uv, win, cv, u, rstd, xh, vn, vmix, wms = _odd_parts(z, zh_ref[...], i, k_ref, g_ref, be_ref, w_ref, bt_ref)
        dcat_t = d_ref[...]
        dc, dd = dcat_t[:, :512], dcat_t[:, 512:]

        @pl.when(i == 0)
        def _():
            dk_ref[...] = jnp.zeros_like(dk_ref)
            dg_ref[...] = jnp.zeros_like(dg_ref)
            dbe_ref[...] = jnp.zeros_like(dbe_ref)
            dw_ref[...] = jnp.zeros_like(dw_ref)
            dbt_ref[...] = jnp.zeros_like(dbt_ref)

        dgb = dc * cv
        dcv = dc * gb
        nxt = dn_ref[:, :512] * zn_ref[:, :512] * (i < last).astype(F32)
        winb = jnp.concatenate([dcv, nxt], axis=0)
        dp = jnp.zeros((R, 512), F32)
        for j in range(SCONV_W):
            off = H - (SCONV_W - 1) + j
            dk_ref[j:j + 1, :] += jnp.sum(dcv * win[off:off + R, :], axis=0, keepdims=True)
            ob = SCONV_W - 1 - j
            dp = dp + k_ref[j:j + 1, :] * winb[ob:ob + R, :]
        dgc = dp * xs
        dxs = dp * gc
        du = dd * vmix
        dvmix = dd * u
        tril = _tril()
        rows = []
        for ci in range(R // CHUNK):
            blocks = []
            for g in range(SG_GROUPS):
                sl = (slice(ci * CHUNK, (ci + 1) * CHUNK), slice(g * LANES, (g + 1) * LANES))
                dblk = dvmix[sl]
                dblk16 = dblk.astype(BF16)
                blocks.append(_tn(wms[g], dblk16))
                dw_ref[g] += _nt(dblk16, vn[sl].astype(BF16)) * tril
                dbt_ref[:, g:g + 1] += jnp.sum(dblk, axis=1, keepdims=True)
            rows.append(jnp.concatenate(blocks, axis=1))
        dvn = jnp.concatenate(rows, axis=0)
        dg_ref[...] += jnp.sum(dvn * xh, axis=0, keepdims=True)
        dbe_ref[...] += jnp.sum(dvn, axis=0, keepdims=True)
        dxh = dvn * g_ref[...]
        dv = rstd * (dxh - jnp.mean(dxh, axis=-1, keepdims=True) - xh * jnp.mean(dxh * xh, axis=-1, keepdims=True))
        duv = jnp.concatenate([du, dv], axis=1) * _gelu_grad(uv)
        dz_ref[...] = jnp.concatenate([dgb, dgc, dxs, duv], axis=1).astype(BF16)

    vec = pl.BlockSpec((1, 512), lambda i: (0, 0))
    kspec = pl.BlockSpec((SCONV_W, 512), lambda i: (0, 0))
    wspec = pl.BlockSpec((SG_GROUPS, CHUNK, CHUNK), lambda i: (0, 0, 0))
    bspec = pl.BlockSpec((CHUNK, SG_GROUPS), lambda i: (0, 0))
    nxt_blk = lambda i: (jnp.minimum((i + 1) * (R // H), T // H - 1), 0)
    return _pcall(
        body, name="odd_mid_bwd", grid=(T // R,),
        in_specs=[pl.BlockSpec((R, ODD_IN), lambda i: (i, 0)),
                  pl.BlockSpec((H, ODD_IN), lambda i: (jnp.maximum(i * (R // H) - 1, 0), 0)),
                  pl.BlockSpec((H, ODD_IN), nxt_blk),
                  pl.BlockSpec((R, D), lambda i: (i, 0)),
                  pl.BlockSpec((H, D), nxt_blk),
                  kspec, vec, vec, wspec, bspec],
        out_specs=[pl.BlockSpec((R, ODD_IN), lambda i: (i, 0)), kspec, vec, vec, wspec, bspec],
        out_shape=[jax.ShapeDtypeStruct((T, ODD_IN), BF16), jax.ShapeDtypeStruct((SCONV_W, 512), F32),
                   jax.ShapeDtypeStruct((1, 512), F32), jax.ShapeDtypeStruct((1, 512), F32),
                   jax.ShapeDtypeStruct((SG_GROUPS, CHUNK, CHUNK), F32), jax.ShapeDtypeStruct((CHUNK, SG_GROUPS), F32)],
        compiler_params=_params("arbitrary"),
    )(z, z, z, dcat, dcat, conv_k, ln_g, ln_b, sg_w, sg_bt)


def _ffn_up(tag, hn, weight):
    def act(acc):
        r = jnp.maximum(acc, 0.0)
        return (r * r,)

    return _mm(f"ffn{tag}_up", "nn", hn, weight(f"ffn_w1_{tag}", hn), T, D_FF, D, (BF16,), epi=act)


FFN_BWD_TILE = 256


def _ffn_dx(name, dout, w2, w1, f, h, g, tie=None):
    tm = FFN_BWD_TILE
    ties = () if tie is None else (tie,)

    def body(d_ref, w2_ref, w1_ref, f_ref, h_ref, g_ref, *rest):
        du_ref, dh_ref, dg_ref = rest[len(ties):]
        dres = d_ref[...]
        du = (_nt(dres.astype(BF16), w2_ref[...]) * (2.0 * jnp.sqrt(f_ref[...].astype(F32)))).astype(BF16)
        du_ref[...] = du
        dy = _nt(du, w1_ref[...])
        x = h_ref[...]
        r = lax.rsqrt(jnp.mean(x * x, axis=-1, keepdims=True) + EPS)
        nrm = x * r
        dn = dy * g_ref[...]
        dh_ref[...] = dres + r * (dn - nrm * jnp.mean(dn * nrm, axis=-1, keepdims=True))

        @pl.when(pl.program_id(0) == 0)
        def _():
            dg_ref[...] = jnp.zeros_like(dg_ref)

        dg_ref[...] += jnp.sum(dy * nrm, axis=0, keepdims=True)

    row = pl.BlockSpec((tm, D), lambda i: (i, 0))
    wide = pl.BlockSpec((tm, D_FF), lambda i: (i, 0))
    vec = pl.BlockSpec((1, D), lambda i: (0, 0))
    resident = lambda shape: pl.BlockSpec(shape, lambda i: (0, 0), pipeline_mode=pl.Buffered(1))
    return _pcall(
        body, name=name, grid=(T // tm,),
        in_specs=[row, resident((D_FF, D)), resident((D, D_FF)), wide, row, vec]
        + [pl.BlockSpec(TOKEN_SHAPE, lambda i: (0, 0))] * len(ties),
        out_specs=[wide, row, vec],
        out_shape=[jax.ShapeDtypeStruct((T, D_FF), BF16), jax.ShapeDtypeStruct((T, D), F32),
                   jax.ShapeDtypeStruct((1, D), F32)],
        compiler_params=_params("arbitrary"),
    )(dout, w2, w1, f, h, g, *ties)


def _ffn_bwd(tag, h, g, weight, emit, saved, dout, tie=None):
    hn, f = saved
    du, dh, dg = _ffn_dx(f"ffn{tag}_dx", dout, weight(f"ffn_w2_{tag}", dout), weight(f"ffn_w1_{tag}", dout), f, h, g, tie)
    emit(f"ffn_w2_{tag}", f, dout)
    return dh, dg, emit(f"ffn_w1_{tag}", hn, du)


def _rope_tables():
    half = HEAD_DIM // 2
    inv = 10000.0 ** (-jnp.arange(half, dtype=F32) / half)
    ang = jnp.arange(T, dtype=F32)[:, None] * inv[None, :]
    cos, sin = jnp.cos(ang), jnp.sin(ang)
    c = jnp.tile(jnp.concatenate([cos, cos], axis=1), (1, LANES // HEAD_DIM))
    s = jnp.tile(jnp.concatenate([-sin, sin], axis=1), (1, LANES // HEAD_DIM))
    return c, s


def _local_step(x, target, p, weight, emit, emit_small, before=lambda name, after: None):
    rope_c, rope_s = _rope_tables()
    grads = {}

    hn0 = _rms_fwd("mix0_norm", x, p["norm_mix_g0"])
    zc = _mm("even_in_conv", "nn", hn0, weight("even_w_in", hn0), T, 2 * CONV_CH, D, (F32,))
    qkv = _qkv_proj(hn0, weight("even_w_in", hn0), rope_c, rope_s)
    cv, cat0 = _econv_fwd(zc, p["even_conv_k"], p["even_conv_b"], p["even_ln_g"], p["even_ln_b"])
    att_parts = [_attn_fwd(qkv, 0)]
    before("attn_fwd1", att_parts[0][0])
    att_parts += [_attn_fwd(qkv, 1), _attn_fwd(qkv, 2)]
    outs = [a[0] for a in att_parts]
    lses = [a[1] for a in att_parts]
    cat0, att, w0, w1, w2 = _attn_merge(outs, lses, cat0)
    wgts = (w0, w1, w2)
    h1, hnf0 = _mm_out_norm("even_out", cat0, weight("even_w_out", cat0), D, x, p["norm_ffn_g0"])
    f0 = _ffn_up(0, hnf0, weight)
    before("ffn0_down", f0)
    h2, hn1 = _mm_out_norm("ffn0_down", f0, weight("ffn_w2_0", f0), D_FF, h1, p["norm_mix_g1"], tm=FFN0_DOWN_TILE)

    z1 = _mm("odd_in", "nn", hn1, weight("odd_w_in", hn1), T, ODD_IN, D, (F32,))
    cat1 = _odd_mid_fwd(z1, p["odd_conv_k"], p["odd_ln_g"], p["odd_ln_b"], p["odd_sg_w"], p["odd_sg_bt"])
    h3, hnf1 = _mm_out_norm("odd_out", cat1, weight("odd_w_out", cat1), D, h2, p["norm_ffn_g1"])
    f1 = _ffn_up(1, hnf1, weight)
    dh4, grads["final_g"], loss = _mm_out_loss("ffn1_down_loss", f1, weight("ffn_w2_1", f1), D_FF, h3, p["final_g"], target)

    dh3, grads["norm_ffn_g1"], tok = _ffn_bwd(1, h3, p["norm_ffn_g1"], weight, emit, (hnf1, f1), dh4)
    tok = emit("odd_w_out", cat1, dh3, tie=tok)
    dcat1 = _mm("odd_out_dx", "nt", dh3, weight("odd_w_out", dh3), T, D, D, (F32,), tie=tok)
    dz1, grads["odd_conv_k"], grads["odd_ln_g"], grads["odd_ln_b"], grads["odd_sg_w"], grads["odd_sg_bt"] = _odd_mid_bwd(
        z1, dcat1, p["odd_conv_k"], p["odd_ln_g"], p["odd_ln_b"], p["odd_sg_w"], p["odd_sg_bt"])
    tok = emit("odd_w_in", hn1, dz1)
    dh2, grads["norm_mix_g1"] = _mm_dx_norm("odd_in_dx", dz1, weight("odd_w_in", dz1), ODD_IN, h2, p["norm_mix_g1"],
                                            dh3, tie=tok)

    dh1, grads["norm_ffn_g0"], tok = _ffn_bwd(0, h1, p["norm_ffn_g0"], weight, emit, (hnf0, f0), dh2)
    tok = emit("even_w_out", cat0, dh1, tie=tok)
    dcat0 = _mm("even_out_dx", "nt", dh1, weight("even_w_out", dh1), T, D, D, (F32,), tie=tok)
    dcv, grads["even_ln_g"], grads["even_ln_b"], grads["even_conv_b"] = _econv_bwd_ln(
        cv, dcat0, p["even_ln_g"], p["even_ln_b"])
    dz0, grads["even_conv_k"] = _econv_bwd_conv(dcv, zc, p["even_conv_k"])
    tok = emit_small(loss, grads)
    dqkv = lax.empty((3, 3, 4, T, LANES), F32)
    for g in range(3):
        dqkv = _attn_bwd(qkv, lses[g], wgts[g], att, dcat0, dqkv, g)
    before("rope_bwd", dqkv)
    dz0 = _rope_bwd(dqkv, rope_c, rope_s, dz0)
    tok = emit("even_w_in", hn0, dz0, tie=tok)
    dx, dg0 = _mm_dx_norm("even_in_dx", dz0, weight("even_w_in", dz0), EVEN_IN, x, p["norm_mix_g0"], dh1, tie=tok)
    return dx, dg0


def _rowwise(name, fn, ins, out_dtypes, tm=256, tie=None):
    rows, cols = ins[0].shape
    tm = tm if rows % tm == 0 else rows
    n_in = len(ins)
    ties = () if tie is None else (tie,)

    def body(*refs):
        vals = fn(*[r[...] for r in refs[:n_in]])
        for o_ref, v in zip(refs[n_in + len(ties):], vals):
            o_ref[...] = v.astype(o_ref.dtype)

    spec = pl.BlockSpec((tm, cols), lambda i: (i, 0))
    outs = _pcall(
        body, name=name, grid=(rows // tm,),
        in_specs=[spec] * n_in + [pl.BlockSpec(TOKEN_SHAPE, lambda i: (0, 0))] * len(ties),
        out_specs=[spec] * len(out_dtypes),
        out_shape=[jax.ShapeDtypeStruct((rows, cols), dt) for dt in out_dtypes],
        compiler_params=_params("parallel"),
    )(*ins, *ties)
    return outs[0] if len(out_dtypes) == 1 else outs


def _adamw(name, w, g, m, v, with_grad=False, tie=None):
    c1 = 1.0 - ADAM_B1 ** ADAM_STEP
    c2 = 1.0 - ADAM_B2 ** ADAM_STEP

    def fn(w_t, g_t, m_t, v_t):
        m_new = ADAM_B1 * m_t + (1.0 - ADAM_B1) * g_t
        v_new = ADAM_B2 * v_t + (1.0 - ADAM_B2) * (g_t * g_t)
        delta = -ADAM_LR * ((m_new / c1) / (jnp.sqrt(v_new / c2) + ADAM_EPS) + ADAM_WD * w_t)
        return (delta, m_new, v_new, g_t) if with_grad else (delta, m_new, v_new)

    return _rowwise(name, fn, (w, g, m, v), (F32,) * (4 if with_grad else 3), tie=tie)


class _Piece:
    def __init__(self, name, rows, cols, axis, src, src_row0):
        self.name, self.rows, self.cols, self.axis = name, rows, cols, axis
        self.width = (cols if axis == 1 else rows) // 4
        self.src, self.src_row0 = src, src_row0

    @property
    def full_shape(self):
        return (self.rows, self.cols)

    @property
    def half_shape(self):
        return (self.rows // 2, self.cols) if self.axis == 1 else (self.rows, self.cols // 2)

    @property
    def shard_half_shape(self):
        return (self.rows // 2, self.width) if self.axis == 1 else (self.width, self.cols // 2)

    def shard_whole(self, ref):
        n = self.rows if self.axis == 1 else self.width
        return ref.at[pl.ds(self.src_row0, n), :]

    def shard_half(self, ref, h):
        if self.axis == 1:
            return ref.at[pl.ds(self.src_row0 + h * (self.rows // 2), self.rows // 2), :]
        return ref.at[pl.ds(self.src_row0, self.width), pl.ds(h * (self.cols // 2), self.cols // 2)]

    def full_shard(self, ref, s):
        if self.axis == 1:
            return ref.at[:, pl.ds(s * self.width, self.width)]
        return ref.at[pl.ds(s * self.width, self.width), :]

    def full_shard_half(self, ref, s, h):
        if self.axis == 1:
            return ref.at[pl.ds(h * (self.rows // 2), self.rows // 2), pl.ds(s * self.width, self.width)]
        return ref.at[pl.ds(s * self.width, self.width), pl.ds(h * (self.cols // 2), self.cols // 2)]

    def full_half(self, ref, h):
        if self.axis == 1:
            return ref.at[pl.ds(h * (self.rows // 2), self.rows // 2), :]
        return ref.at[:, pl.ds(h * (self.cols // 2), self.cols // 2)]

    def full_half_rows(self, ref, h, r0, n):
        if self.axis == 1:
            return ref.at[pl.ds(h * (self.rows // 2) + r0, n), :]
        return ref.at[pl.ds(r0, n), pl.ds(h * (self.cols // 2), self.cols // 2)]

    def half_shard(self, ref, s):
        return self.full_shard(ref, s)


PIECES = (
    _Piece("even_w_in", D, EVEN_IN, 1, 0, 0),
    _Piece("even_w_out", D, D, 0, 1, 0),
    _Piece("ffn_w1_0", D, D_FF, 1, 4, 0),
    _Piece("ffn_w2_0", D_FF, D, 0, 5, 0),
    _Piece("odd_w_in", D, ODD_IN, 1, 2, 0),
    _Piece("odd_w_out", D, D, 0, 3, 0),
    _Piece("ffn_w1_1", D, D_FF, 1, 4, D),
    _Piece("ffn_w2_1", D_FF, D, 0, 5, D_FF // 4),
)
N_PIECES = len(PIECES)
FORWARD_RIDES = {"attn_fwd1": (1, 2, 3), "ffn0_down": (4, 5, 6, 7)}
JOIN_GROUPS = ((0, 1, 2, 3), (4, 5))
JOIN_RIDES_IN = "rope_bwd"
HOLD_BACK = ("ffn_w2_0", "ffn_w2_1", "odd_w_out")
N_SHARD_OPERANDS = 6
ANY = pl.BlockSpec(memory_space=pl.ANY)
MESH = pl.DeviceIdType.MESH


def _mesh_place():
    x, y, c = lax.axis_index("x"), lax.axis_index("y"), lax.axis_index("c")
    chips = [(1 - x, y), (x, 1 - y), (1 - x, 1 - y)]
    return x, y, c, chips


def _remote(src, dst, send_sem, recv_sem, dev):
    return pltpu.make_async_remote_copy(src_ref=src, dst_ref=dst, send_sem=send_sem, recv_sem=recv_sem,
                                        device_id=dev, device_id_type=MESH)


HBM = pl.BlockSpec(memory_space=pltpu.HBM)
SEM = pl.BlockSpec(memory_space=pltpu.SEMAPHORE)
SPLIT_PARAMS = pltpu.CompilerParams(has_side_effects=pltpu.SideEffectType.DATAFLOW_SIDE_EFFECTING)
CAST_TILE = 256


def _in_hbm(a):
    return pltpu.with_memory_space_constraint(a, pltpu.HBM)


def _cast_place(pc, shard_operand, chip, tie=None):
    rows, cols = (pc.rows, pc.width) if pc.axis == 1 else (pc.width, pc.cols)
    nblk = rows // CAST_TILE
    blk0 = pc.src_row0 // CAST_TILE
    ties = () if tie is None else (tie,)

    def body(chip_ref, x_ref, *rest):
        del chip_ref
        rest[-1][...] = x_ref[...].astype(BF16)

    if pc.axis == 1:
        out_map = lambda i, chip_ref: (i, chip_ref[0])
    else:
        out_map = lambda i, chip_ref: (chip_ref[0] * nblk + i, 0)
    return _pcall(
        body, name=f"cast_{pc.name}",
        grid_spec=pltpu.PrefetchScalarGridSpec(
            num_scalar_prefetch=1, grid=(nblk,),
            in_specs=[pl.BlockSpec((CAST_TILE, cols), lambda i, chip_ref: (blk0 + i, 0))]
            + [pl.BlockSpec(TOKEN_SHAPE, lambda i, chip_ref: (0, 0))] * len(ties),
            out_specs=pl.BlockSpec((CAST_TILE, cols), out_map)),
        out_shape=jax.ShapeDtypeStruct(pc.full_shape, BF16),
        compiler_params=_params("parallel"),
    )(chip, shard_operand, *ties)


def _gather_start(name, pieces, fulls):
    n = len(pieces)

    def body(*refs):
        ins = refs[:n]
        sends = refs[2 * n:3 * n]
        recvs = refs[3 * n:4 * n]
        token = refs[4 * n]
        x, y, c, chips = _mesh_place()
        s = 2 * x + y
        for i, pc in enumerate(pieces):
            win = pc.full_shard_half(ins[i], s, c)
            for k, (cx, cy) in enumerate(chips):
                _remote(win, win, sends[i].at[k], recvs[i].at[k], (cx, cy, c)).start()
        token[...] = jnp.zeros(TOKEN_SHAPE, F32)

    sems = [pltpu.SemaphoreType.DMA((3,))] * (2 * n)
    outs = _pcall(
        body, name=name,
        in_specs=[HBM] * n,
        out_specs=[HBM] * n + [SEM] * (2 * n) + [pl.BlockSpec(memory_space=pltpu.VMEM)],
        out_shape=[pltpu.HBM(pc.full_shape, BF16) for pc in pieces] + sems + [jax.ShapeDtypeStruct(TOKEN_SHAPE, F32)],
        input_output_aliases={i: i for i in range(n)},
        compiler_params=SPLIT_PARAMS,
    )(*[_in_hbm(f) for f in fulls])
    return outs[:n], outs[n:2 * n], outs[2 * n:3 * n], outs[3 * n]


def _gather_wait(pc, full, send_sems, recv_sems, after):
    def body(full_ref, send_ref, recv_ref, after_ref, out_ref):
        del after_ref, out_ref
        x, y, c, chips = _mesh_place()
        for k, (cx, cy) in enumerate(chips):
            win = pc.full_shard_half(full_ref, 2 * cx + cy, c)
            cp = _remote(win, win, send_ref.at[k], recv_ref.at[k], (cx, cy, c))
            cp.wait_send()
            cp.wait_recv()

    return _pcall(
        body, name=f"gather_wait_{pc.name}",
        in_specs=[HBM, SEM, SEM, ANY], out_specs=HBM, out_shape=pltpu.HBM(pc.full_shape, BF16),
        input_output_aliases={0: 0}, compiler_params=SPLIT_PARAMS,
    )(full, send_sems, recv_sems, after)


def _core_forward_job(pieces, fulls):
    n = len(pieces)

    def copies(ins, outs, scr):
        send_bufs, recv_bufs = scr[:n], scr[n:2 * n]
        load_sems, send_sems, recv_sems, store_sems = scr[2 * n:]
        x, y, c, chips = _mesh_place()
        loads, sends, stores = [], [], []
        for i, pc in enumerate(pieces):
            for k, (cx, cy) in enumerate(chips):
                j = 3 * i + k
                loads.append(pltpu.make_async_copy(pc.full_shard_half(ins[i], 2 * cx + cy, c), send_bufs[i].at[k],
                                                   load_sems.at[j]))
                sends.append(_remote(send_bufs[i].at[k], recv_bufs[i].at[k], send_sems.at[j], recv_sems.at[j],
                                     (x, y, 1 - c)))
                stores.append(pltpu.make_async_copy(recv_bufs[i].at[k], pc.full_shard_half(outs[i], 2 * cx + cy, 1 - c),
                                                    store_sems.at[j]))
        return loads, sends, stores

    def begin(ins, outs, scr):
        for cp in copies(ins, outs, scr)[0]:
            cp.start()

    def advance(ins, outs, scr):
        loads, sends, _ = copies(ins, outs, scr)
        for load, send in zip(loads, sends):
            load.wait()
            send.start()

    def finish(ins, outs, scr):
        _, sends, stores = copies(ins, outs, scr)
        for send, store in zip(sends, stores):
            send.wait_recv()
            store.start()
        for send, store in zip(sends, stores):
            send.wait_send()
            store.wait()

    sems = pltpu.SemaphoreType.DMA((3 * n,))
    bufs = [pltpu.VMEM((3,) + pc.shard_half_shape, BF16) for pc in pieces]
    return _SideJob(fulls, [jax.ShapeDtypeStruct(pc.full_shape, BF16) for pc in pieces],
                    bufs + bufs + [sems, sems, sems, sems], {i: i for i in range(n)}, begin, advance, finish)


def _run_job(name, job):
    def body(o_ref):
        o_ref[...] = jnp.zeros(TOKEN_SHAPE, F32)

    _ride_next_call(job)
    _pcall(body, name=name, grid=(3,), in_specs=[], out_specs=pl.BlockSpec(TOKEN_SHAPE, lambda i: (0, 0)),
           out_shape=jax.ShapeDtypeStruct(TOKEN_SHAPE, F32))()
    return job.results


def _core_forward(pieces, fulls):
    n = len(pieces)

    def body(*refs):
        ins, outs = refs[:n], refs[n:2 * n]
        send_bufs, recv_bufs = refs[2 * n:3 * n], refs[3 * n:4 * n]
        load_sems, send_sems, recv_sems, store_sems = refs[4 * n:]
        x, y, c, chips = _mesh_place()
        loads, sends, stores = [], [], []
        for i, pc in enumerate(pieces):
            for k, (cx, cy) in enumerate(chips):
                cp = pltpu.make_async_copy(pc.full_shard_half(ins[i], 2 * cx + cy, c), send_bufs[i].at[k],
                                           load_sems.at[3 * i + k])
                cp.start()
                loads.append(cp)
        for i in range(n):
            for k in range(3):
                j = 3 * i + k
                loads[j].wait()
                cp = _remote(send_bufs[i].at[k], recv_bufs[i].at[k], send_sems.at[j], recv_sems.at[j], (x, y, 1 - c))
                cp.start()
                sends.append(cp)
        for i, pc in enumerate(pieces):
            for k, (cx, cy) in enumerate(chips):
                j = 3 * i + k
                sends[j].wait_recv()
                cp = pltpu.make_async_copy(recv_bufs[i].at[k], pc.full_shard_half(outs[i], 2 * cx + cy, 1 - c),
                                           store_sems.at[j])
                cp.start()
                stores.append(cp)
        for j in range(3 * n):
            sends[j].wait_send()
            stores[j].wait()

    sems = pltpu.SemaphoreType.DMA((3 * n,))
    bufs = [pltpu.VMEM((3,) + pc.shard_half_shape, BF16) for pc in pieces]
    return _pcall(
        body, name="core_forward_" + pieces[0].name, in_specs=[ANY] * n, out_specs=[ANY] * n,
        out_shape=[jax.ShapeDtypeStruct(pc.full_shape, BF16) for pc in pieces],
        scratch_shapes=bufs + bufs + [sems, sems, sems, sems],
        input_output_aliases={i: i for i in range(n)},
        compiler_params=pltpu.CompilerParams(vmem_limit_bytes=VMEM_LIMIT),
    )(*fulls)


def _dw_tile(pc):
    if pc.axis == 1:
        tn = max(t for t in range(LANES, pc.cols + 1, LANES) if pc.cols % t == 0 and t <= 1408)
        return pc.rows // 2, tn
    return min(pc.rows, 1024), pc.cols // 2


def _mm_dw_chipsum(pc, a, b, core, tie=None):
    tm, tn = _dw_tile(pc)
    hr, hc = pc.half_shape
    tiles_r, tiles_c = hr // tm, hc // tn
    th = tiles_r * tiles_c
    ties = () if tie is None else (tie,)

    def tile_of(s, core_ref):
        mine = s >= th
        half = jnp.where(mine, core_ref[0], 1 - core_ref[0])
        local = s % th
        li, lj = local // tiles_c, local % tiles_c
        if pc.axis == 1:
            return half * tiles_r + li, lj, li, lj, mine
        return li, half * tiles_c + lj, li, lj, mine

    def body(core_ref, a_ref, b_ref, *rest):
        o_ref, send_buf, recv_buf, send_sems, recv_sems = rest[len(ties):]
        s = pl.program_id(0)
        local = s % th
        x, y, c = lax.axis_index("x"), lax.axis_index("y"), lax.axis_index("c")
        acc = _tn(a_ref[...].astype(BF16), b_ref[...].astype(BF16))

        def push(slot):
            return _remote(send_buf.at[slot], recv_buf.at[slot], send_sems.at[slot], recv_sems.at[slot], (x, y, 1 - c))

        @pl.when(s < th)
        def _():
            send_buf[local] = acc.astype(BF16)
            push(local).start()

        @pl.when(s >= th)
        def _():
            push(local).wait_recv()
            o_ref[...] = (acc + recv_buf[local].astype(F32)).astype(BF16)

        @pl.when(s == 2 * th - 1)
        def _():
            for slot in range(th):
                push(slot).wait_send()

    def a_map(s, core_ref):
        return 0, tile_of(s, core_ref)[0]

    def b_map(s, core_ref):
        return 0, tile_of(s, core_ref)[1]

    def o_map(s, core_ref):
        _, _, li, lj, mine = tile_of(s, core_ref)
        return jnp.where(mine, li, 0), jnp.where(mine, lj, 0)

    return _pcall(
        body, name=f"dw_{pc.name}",
        grid_spec=pltpu.PrefetchScalarGridSpec(
            num_scalar_prefetch=1, grid=(2 * th,),
            in_specs=[pl.BlockSpec((T, tm), a_map), pl.BlockSpec((T, tn), b_map)]
            + [pl.BlockSpec(TOKEN_SHAPE, lambda s, core_ref: (0, 0))] * len(ties),
            out_specs=pl.BlockSpec((tm, tn), o_map),
            scratch_shapes=[pltpu.VMEM((th, tm, tn), BF16), pltpu.VMEM((th, tm, tn), BF16),
                            pltpu.SemaphoreType.DMA((th,)), pltpu.SemaphoreType.DMA((th,))]),
        out_shape=jax.ShapeDtypeStruct((hr, hc), BF16),
        compiler_params=_params("arbitrary"),
    )(core, a, b, *ties)


def _scatter_start(pieces, chip_sums):
    n = len(pieces)

    def body(*refs):
        sums, lands = refs[:n], refs[n:2 * n]
        sends, recvs = refs[4 * n:5 * n], refs[5 * n:6 * n]
        token = refs[6 * n]
        x, y, c, chips = _mesh_place()
        for i, pc in enumerate(pieces):
            for k, (cx, cy) in enumerate(chips):
                _remote(pc.half_shard(sums[i], 2 * cx + cy), lands[i].at[k], sends[i].at[k], recvs[i].at[k],
                        (cx, cy, c)).start()
        token[...] = jnp.zeros(TOKEN_SHAPE, F32)

    land_shapes = [(3,) + pc.shard_half_shape for pc in pieces]
    sems = [pltpu.SemaphoreType.DMA((3,))] * (2 * n)
    outs = _pcall(
        body, name="scatter_start_" + pieces[0].name,
        in_specs=[HBM] * (2 * n), out_specs=[HBM] * (2 * n) + [SEM] * (2 * n) + [pl.BlockSpec(memory_space=pltpu.VMEM)],
        out_shape=[pltpu.HBM(pc.half_shape, BF16) for pc in pieces] + [pltpu.HBM(sh, BF16) for sh in land_shapes]
        + sems + [jax.ShapeDtypeStruct(TOKEN_SHAPE, F32)],
        input_output_aliases={i: i for i in range(2 * n)}, compiler_params=SPLIT_PARAMS,
    )(*[_in_hbm(cs) for cs in chip_sums], *[_in_hbm(lax.empty(sh, BF16)) for sh in land_shapes])
    return [(outs[i], outs[n + i], outs[2 * n + i], outs[3 * n + i]) for i in range(n)], outs[4 * n]


def _scatter_wait(pc, chip_sum, land, send_sems, recv_sems, after):
    def body(sum_ref, land_ref, send_ref, recv_ref, after_ref, sum_out, land_out):
        del after_ref, sum_out, land_out
        x, y, c, chips = _mesh_place()
        for k, (cx, cy) in enumerate(chips):
            cp = _remote(pc.half_shard(sum_ref, 2 * cx + cy), land_ref.at[k], send_ref.at[k], recv_ref.at[k], (cx, cy, c))
            cp.wait_send()
            cp.wait_recv()

    return _pcall(
        body, name=f"scatter_wait_{pc.name}",
        in_specs=[HBM, HBM, SEM, SEM, ANY], out_specs=[HBM, HBM],
        out_shape=[pltpu.HBM(pc.half_shape, BF16), pltpu.HBM((3,) + pc.shard_half_shape, BF16)],
        input_output_aliases={0: 0, 1: 1}, compiler_params=SPLIT_PARAMS,
    )(chip_sum, land, send_sems, recv_sems, after)


SHARD_OPERAND_SHAPES = ((D, EVEN_IN // 4), (D // 4, D), (D, ODD_IN // 4), (D // 4, D), (2 * D, D_FF // 4), (2 * D_FF // 4, D))


def _allsum_join_job(operands, chip_sums, lands):
    pieces = [pc for pc in PIECES if pc.src in operands]
    n = len(pieces)

    def copies(ins, outs, scr):
        sum_refs, land_refs = ins[:n], ins[n:]
        out_refs = dict(zip(operands, outs))
        in_bufs, fin_bufs, recv_bufs = scr[:n], scr[n:2 * n], scr[2 * n:3 * n]
        load_sems, send_sems, recv_sems, out_sems = scr[3 * n:]
        x, y, c, _ = _mesh_place()
        s = 2 * x + y
        loads, sends, mine, theirs = [], [], [], []
        for j, pc in enumerate(pieces):
            loads.append((pltpu.make_async_copy(land_refs[j], in_bufs[j].at[pl.ds(0, 3)], load_sems.at[2 * j]),
                          pltpu.make_async_copy(pc.half_shard(sum_refs[j], s), in_bufs[j].at[3], load_sems.at[2 * j + 1])))
            sends.append(_remote(fin_bufs[j], recv_bufs[j], send_sems.at[j], recv_sems.at[j], (x, y, 1 - c)))
            mine.append(pltpu.make_async_copy(fin_bufs[j], pc.shard_half(out_refs[pc.src], c), out_sems.at[2 * j]))
            theirs.append(pltpu.make_async_copy(recv_bufs[j], pc.shard_half(out_refs[pc.src], 1 - c), out_sems.at[2 * j + 1]))
        return loads, sends, mine, theirs, in_bufs, fin_bufs

    def begin(ins, outs, scr):
        for a, b in copies(ins, outs, scr)[0]:
            a.start()
            b.start()

    def advance(ins, outs, scr):
        loads, sends, mine, _, in_bufs, fin_bufs = copies(ins, outs, scr)
        for j in range(n):
            loads[j][0].wait()
            loads[j][1].wait()
            acc = in_bufs[j][0].astype(F32)
            for k in range(1, 4):
                acc = acc + in_bufs[j][k].astype(F32)
            fin_bufs[j][...] = acc
            mine[j].start()
            sends[j].start()

    def finish(ins, outs, scr):
        _, sends, mine, theirs, _, _ = copies(ins, outs, scr)
        for j in range(n):
            sends[j].wait_recv()
            theirs[j].start()
        for j in range(n):
            sends[j].wait_send()
            mine[j].wait()
            theirs[j].wait()

    halves = [pc.shard_half_shape for pc in pieces]
    scratch = ([pltpu.VMEM((4,) + sh, BF16) for sh in halves] + [pltpu.VMEM(sh, F32) for sh in halves] * 2
               + [pltpu.SemaphoreType.DMA((2 * n,)), pltpu.SemaphoreType.DMA((n,)), pltpu.SemaphoreType.DMA((n,)),
                  pltpu.SemaphoreType.DMA((2 * n,))])
    return _SideJob(list(chip_sums) + list(lands), [jax.ShapeDtypeStruct(SHARD_OPERAND_SHAPES[o], F32) for o in operands],
                    scratch, {}, begin, advance, finish)


PEER_FLIPS = tuple((a, b, e) for a in (0, 1) for b in (0, 1) for e in (0, 1) if (a, b, e) != (0, 0, 0))


def _peers():
    x, y, c = lax.axis_index("x"), lax.axis_index("y"), lax.axis_index("c")
    me = 4 * x + 2 * y + c
    out = []
    for a, b, e in PEER_FLIPS:
        px, py, pc = (1 - x if a else x), (1 - y if b else y), (1 - c if e else c)
        out.append(((px, py, pc), 4 * px + 2 * py + pc))
    return me, out


def _exchange8_start(name, blk):
    m = blk.shape[0]

    def body(blk_ref, land_ref, blk_out, land_out, sends, recvs, token):
        del blk_out, land_out
        me, peers = _peers()
        for k, (dev, _) in enumerate(peers):
            _remote(blk_ref, land_ref.at[me], sends.at[k], recvs.at[k], dev).start()
        token[...] = jnp.zeros(TOKEN_SHAPE, F32)

    sems = pltpu.SemaphoreType.DMA((7,))
    return _pcall(
        body, name=name,
        in_specs=[HBM, HBM], out_specs=[HBM, HBM, SEM, SEM, pl.BlockSpec(memory_space=pltpu.VMEM)],
        out_shape=[pltpu.HBM((m, LANES), F32), pltpu.HBM((8, m, LANES), F32), sems, sems,
                   jax.ShapeDtypeStruct(TOKEN_SHAPE, F32)],
        input_output_aliases={0: 0, 1: 1}, compiler_params=SPLIT_PARAMS,
    )(_in_hbm(blk), _in_hbm(lax.empty((8, m, LANES), F32)))


def _exchange8_wait(name, blk, land, send_sems, recv_sems, after):
    def body(blk_ref, land_ref, send_ref, recv_ref, after_ref, blk_out, land_out):
        del after_ref, blk_out, land_out
        _, peers = _peers()
        for k, (dev, slot) in enumerate(peers):
            cp = _remote(blk_ref, land_ref.at[slot], send_ref.at[k], recv_ref.at[k], dev)
            cp.wait_send()
            cp.wait_recv()

    m = blk.shape[0]
    return _pcall(
        body, name=name,
        in_specs=[HBM, HBM, SEM, SEM, ANY], out_specs=[HBM, HBM],
        out_shape=[pltpu.HBM((m, LANES), F32), pltpu.HBM((8, m, LANES), F32)],
        input_output_aliases={0: 0, 1: 1}, compiler_params=SPLIT_PARAMS,
    )(blk, land, send_sems, recv_sems, after)


def _collect8(name, blk, land, with_sum):
    m = blk.shape[0]

    def body(blk_ref, land_ref, out_ref, *scratch):
        sems = scratch[-1]
        dst = scratch[0] if with_sum else out_ref
        me, peers = _peers()
        copies = [pltpu.make_async_copy(blk_ref, dst.at[me], sems.at[7])]
        for k, (_, slot) in enumerate(peers):
            copies.append(pltpu.make_async_copy(land_ref.at[slot], dst.at[slot], sems.at[k]))
        for cp in copies:
            cp.start()
        for cp in copies:
            cp.wait()
        if with_sum:
            acc = dst[0]
            for dev in range(1, 8):
                acc = acc + dst[dev]
            out_ref[...] = acc

    all_shape = (8, m, LANES)
    return _pcall(
        body, name=name, in_specs=[ANY, ANY], out_specs=pl.BlockSpec(memory_space=pltpu.VMEM),
        out_shape=jax.ShapeDtypeStruct((m, LANES) if with_sum else all_shape, F32),
        scratch_shapes=([pltpu.VMEM(all_shape, F32)] if with_sum else []) + [pltpu.SemaphoreType.DMA((8,))],
    )(blk, land)


def _pack(arrays, row_counts):
    rows = []
    for a, n in zip(arrays, row_counts):
        flat = a.reshape(-1, LANES)
        rows.append(jnp.pad(flat, ((0, n - flat.shape[0]), (0, 0))))
    return jnp.concatenate(rows, axis=0)


REPL_NAMES = ("norm_mix_g", "norm_ffn_g", "even_conv_b", "even_ln_g", "even_ln_b", "odd_sg_w", "odd_sg_b", "final_g")
REPL_SHAPES = ((2, D), (2, D), (1, 512), (1, 512), (1, 512), (1, SG_GROUPS, CHUNK, CHUNK), (1, SG_GROUPS, CHUNK), (D,))
REPL_ROWS = (16, 16, 8, 8, 8, 512, 8, 8)
SHARDED_NAMES = ("even_conv_k", "odd_conv_k", "odd_ln_g", "odd_ln_b")
SHARDED_SHARD_SHAPES = ((1, CONV_W, LANES), (1, SCONV_W, LANES), (1, LANES), (1, LANES))
SHARDED_SHARD_ROWS = (32, 8, 8, 8)
SHARDED_FULL_SHAPES = ((CONV_W, 512), (SCONV_W, 512), (1, 512), (1, 512))
SHARDED_FULL_ROWS = (128, 16, 8, 8)
SMALL_NAMES = REPL_NAMES + SHARDED_NAMES
SMALL_ROWS = REPL_ROWS + SHARDED_SHARD_ROWS
SMALL_OUT_SHAPES = REPL_SHAPES[:-1] + ((1, D),) + SHARDED_SHARD_SHAPES
LOSS_ROWS = 8


def _offsets(rows):
    out, r0 = [], 0
    for n in rows:
        out.append(r0)
        r0 += n
    return out


def _adamw_small(w_pack, m_pack, v_pack, grad_sum, first_gain_sum):
    n_rows = sum(SMALL_ROWS)
    state_at = _offsets(SMALL_ROWS)
    grad_at = _offsets((LOSS_ROWS, 8) + REPL_ROWS[1:] + SHARDED_FULL_ROWS)[1:]
    c1 = 1.0 - ADAM_B1 ** ADAM_STEP
    c2 = 1.0 - ADAM_B2 ** ADAM_STEP
    n_repl = len(REPL_NAMES)

    def body(w_ref, m_ref, v_ref, g_ref, g0_ref, *rest):
        outs, gbuf = rest[:-1], rest[-1]
        chip = 2 * lax.axis_index("x") + lax.axis_index("y")
        gbuf[...] = jnp.zeros((n_rows, LANES), F32)
        gbuf[0:8, :] = g0_ref[...]
        gbuf[8:16, :] = g_ref[grad_at[0]:grad_at[0] + 8, :]
        for i in range(1, n_repl):
            gbuf[state_at[i]:state_at[i] + REPL_ROWS[i], :] = g_ref[grad_at[i]:grad_at[i] + REPL_ROWS[i], :]
        for k, shape in enumerate(SHARDED_SHARD_SHAPES):
            used = shape[-2] if len(shape) == 3 else 1
            src = pl.ds(grad_at[n_repl + k] + chip, used, stride=4) if used > 1 else pl.ds(grad_at[n_repl + k] + chip, 1)
            gbuf[state_at[n_repl + k]:state_at[n_repl + k] + used, :] = g_ref[src, :]
        g = gbuf[...]
        m_new = ADAM_B1 * m_ref[...] + (1.0 - ADAM_B1) * g
        v_new = ADAM_B2 * v_ref[...] + (1.0 - ADAM_B2) * (g * g)
        delta = -ADAM_LR * ((m_new / c1) / (jnp.sqrt(v_new / c2) + ADAM_EPS) + ADAM_WD * w_ref[...])
        for i, shape in enumerate(SMALL_OUT_SHAPES):
            for j, val in enumerate((g, delta, m_new, v_new)):
                o_ref = outs[4 * i + j]
                rows = val[state_at[i]:state_at[i] + SMALL_ROWS[i], :]
                if len(shape) == 2 and shape[1] > LANES:
                    per = shape[1] // LANES
                    for r in range(shape[0]):
                        for q in range(per):
                            o_ref[r:r + 1, q * LANES:(q + 1) * LANES] = rows[r * per + q:r * per + q + 1, :]
                elif len(shape) == 4:
                    for grp in range(shape[1]):
                        o_ref[0, grp] = rows[grp * shape[2]:(grp + 1) * shape[2], :]
                elif len(shape) == 3:
                    o_ref[0] = rows[:shape[1], :]
                else:
                    o_ref[...] = rows[:1, :]

    vm = pl.BlockSpec(memory_space=pltpu.VMEM)
    out_shape = [jax.ShapeDtypeStruct(sh, F32) for sh in SMALL_OUT_SHAPES for _ in range(4)]
    outs = _pcall(body, name="adamw_small", in_specs=[vm] * 5, out_specs=[vm] * len(out_shape), out_shape=out_shape,
                  scratch_shapes=[pltpu.VMEM((n_rows, LANES), F32)])(w_pack, m_pack, v_pack, grad_sum, first_gain_sum)
    return {n: outs[4 * i:4 * i + 4] for i, n in enumerate(SMALL_NAMES)}


def _touch(arrays):
    n = len(arrays)

    def body(*refs):
        refs[-1][...] = jnp.zeros(TOKEN_SHAPE, F32)

    outs = _pcall(
        body, name="touch", in_specs=[ANY] * n, out_specs=[ANY] * n + [pl.BlockSpec(memory_space=pltpu.VMEM)],
        out_shape=[jax.ShapeDtypeStruct(a.shape, a.dtype) for a in arrays] + [jax.ShapeDtypeStruct(TOKEN_SHAPE, F32)],
        input_output_aliases={i: i for i in range(n)})(*arrays)
    return outs[:n], outs[n]


def kernel(x, norm_mix_g, norm_ffn_g, even_w_in, even_conv_k, even_conv_b, even_ln_g, even_ln_b, even_w_out, odd_w_in, odd_conv_k, odd_ln_g, odd_ln_b, odd_sg_w, odd_sg_b, odd_w_out, ffn_w1, ffn_w2, final_g, loss_target, m_norm_mix_g, m_norm_ffn_g, m_even_w_in, m_even_conv_k, m_even_conv_b, m_even_ln_g, m_even_ln_b, m_even_w_out, m_odd_w_in, m_odd_conv_k, m_odd_ln_g, m_odd_ln_b, m_odd_sg_w, m_odd_sg_b, m_odd_w_out, m_ffn_w1, m_ffn_w2, m_final_g, v_norm_mix_g, v_norm_ffn_g, v_even_w_in, v_even_conv_k, v_even_conv_b, v_even_ln_g, v_even_ln_b, v_even_w_out, v_odd_w_in, v_odd_conv_k, v_odd_ln_g, v_odd_ln_b, v_odd_sg_w, v_odd_sg_b, v_odd_w_out, v_ffn_w1, v_ffn_w2, v_final_g):
    names = ("norm_mix_g", "norm_ffn_g", "even_w_in", "even_conv_k", "even_conv_b", "even_ln_g", "even_ln_b", "even_w_out",
             "odd_w_in", "odd_conv_k", "odd_ln_g", "odd_ln_b", "odd_sg_w", "odd_sg_b", "odd_w_out", "ffn_w1", "ffn_w2", "final_g")
    w = dict(zip(names, (norm_mix_g, norm_ffn_g, even_w_in, even_conv_k, even_conv_b, even_ln_g, even_ln_b, even_w_out,
                         odd_w_in, odd_conv_k, odd_ln_g, odd_ln_b, odd_sg_w, odd_sg_b, odd_w_out, ffn_w1, ffn_w2, final_g)))
    mom = dict(zip(names, (m_norm_mix_g, m_norm_ffn_g, m_even_w_in, m_even_conv_k, m_even_conv_b, m_even_ln_g, m_even_ln_b,
                           m_even_w_out, m_odd_w_in, m_odd_conv_k, m_odd_ln_g, m_odd_ln_b, m_odd_sg_w, m_odd_sg_b, m_odd_w_out,
                           m_ffn_w1, m_ffn_w2, m_final_g)))
    vel = dict(zip(names, (v_norm_mix_g, v_norm_ffn_g, v_even_w_in, v_even_conv_k, v_even_conv_b, v_even_ln_g, v_even_ln_b,
                           v_even_w_out, v_odd_w_in, v_odd_conv_k, v_odd_ln_g, v_odd_ln_b, v_odd_sg_w, v_odd_sg_b, v_odd_w_out,
                           v_ffn_w1, v_ffn_w2, v_final_g)))
    big_names = ("even_w_in", "even_w_out", "odd_w_in", "odd_w_out", "ffn_w1", "ffn_w2")
    chip = 2 * lax.axis_index("x") + lax.axis_index("y")

    def shard2d(t, name):
        return t[name].reshape(SHARD_OPERAND_SHAPES[big_names.index(name)])

    chip_op = jnp.reshape(chip, (1,)).astype(jnp.int32)
    small_pack = _pack([w[n] for n in SHARDED_NAMES], SHARDED_SHARD_ROWS)
    small_blk, small_land, small_send, small_recv, small_token = _exchange8_start("gather_small_start", small_pack)
    first = _cast_place(PIECES[0], shard2d(w, big_names[PIECES[0].src]), chip_op, tie=small_token)
    fly0, send0, recv0, token = _gather_start("gather_start_first", PIECES[:1], [first])
    placed = [_cast_place(pc, shard2d(w, big_names[pc.src]), chip_op, tie=token) for pc in PIECES[1:]]
    fly1, send1, recv1, all_started = _gather_start("gather_start_rest", PIECES[1:], placed)
    flying, gather_send, gather_recv = fly0 + fly1, send0 + send1, recv0 + recv1
    ready = {}

    names_in_order = [pc.name for pc in PIECES]

    riding = {}

    (*state_packs, _), idle_work_done = _touch(
        [_pack([t[n] for n in SMALL_NAMES], SMALL_ROWS) for t in (w, mom, vel)] + [all_started])

    def weight(name, after):
        if name in riding:
            job, k = riding.pop(name)
            ready[name] = job.results[k]
        if name not in ready:
            landed = _gather_wait(PIECES[0], flying[0], gather_send[0], gather_recv[0], idle_work_done)
            ready[name], = _core_forward(PIECES[:1], [landed])
        return ready[name]

    def before(call, after):
        if call in FORWARD_RIDES:
            group = FORWARD_RIDES[call]
            landed = [_gather_wait(PIECES[j], flying[j], gather_send[j], gather_recv[j], after) for j in group]
            job = _core_forward_job([PIECES[j] for j in group], landed)
            riding.update((PIECES[j].name, (job, k)) for k, j in enumerate(group))
            _ride_next_call(job)
        elif call == JOIN_RIDES_IN:
            early_join.append(join_job(JOIN_GROUPS[1], after))
            _ride_next_call(early_join[0])

    early_join = []

    def join_job(operands, after):
        pieces = [pc for pc in PIECES if pc.src in operands]
        done = {}
        for entry in list(scattering):
            if entry[0] in pieces:
                done[entry[0].name] = _scatter_wait(*entry, after)
                scattering.remove(entry)
        return _allsum_join_job(operands, [done[pc.name][0] for pc in pieces], [done[pc.name][1] for pc in pieces])

    scattering = []
    held = []

    core_op = jnp.reshape(lax.axis_index("c"), (1,)).astype(jnp.int32)

    def emit(name, a, b, tie=None):
        pc = PIECES[names_in_order.index(name)]
        held.append((pc, _mm_dw_chipsum(pc, a, b, core_op, tie)))
        if name in HOLD_BACK:
            return None
        pieces = [pc for pc, _ in held]
        started, token = _scatter_start(pieces, [chip_sum for _, chip_sum in held])
        scattering.extend((pc,) + tuple(st) for pc, st in zip(pieces, started))
        held.clear()
        return token

    full = {}
    small_blk, small_land = _exchange8_wait("gather_small_wait", small_blk, small_land, small_send, small_recv, all_started)
    gathered = _collect8("gather_small_collect", small_blk, small_land, False)
    gathered = gathered.reshape(4, 2, sum(SHARDED_SHARD_ROWS), LANES)[:, 0]
    r0 = 0
    for n, sh, rows, full_sh in zip(SHARDED_NAMES, SHARDED_SHARD_SHAPES, SHARDED_SHARD_ROWS, SHARDED_FULL_SHAPES):
        per_chip = gathered[:, r0:r0 + rows].reshape(4, -1)[:, :full_sh[0] * LANES].reshape(4, full_sh[0], LANES)
        full[n] = jnp.transpose(per_chip, (1, 0, 2)).reshape(full_sh)
        r0 += rows
    p = dict(full)
    p.update(norm_mix_g0=norm_mix_g[0:1], norm_mix_g1=norm_mix_g[1:2], norm_ffn_g0=norm_ffn_g[0:1], norm_ffn_g1=norm_ffn_g[1:2],
             even_conv_b=even_conv_b, even_ln_g=even_ln_g, even_ln_b=even_ln_b,
             odd_sg_w=odd_sg_w[0], odd_sg_bt=odd_sg_b[0].T, final_g=final_g[None, :])

    small = {}

    def emit_small(loss_row, g):
        parts = [loss_row, g["norm_mix_g1"], g["norm_ffn_g0"], g["norm_ffn_g1"], g["even_conv_b"], g["even_ln_g"],
                 g["even_ln_b"], g["odd_sg_w"], g["odd_sg_bt"].T, g["final_g"],
                 g["even_conv_k"], g["odd_conv_k"], g["odd_ln_g"], g["odd_ln_b"]]
        pack = _pack(parts, (8, 8, 8, 8) + REPL_ROWS[2:] + SHARDED_FULL_ROWS)
        small["blk"], small["land"], small["send"], small["recv"], token = _exchange8_start("allreduce_small_start", pack)
        return token

    dx, dg0 = _local_step(x[0], loss_target[0], p, weight, emit, emit_small, before)
    last_blk, last_land, last_send, last_recv, grad_token = _exchange8_start("allreduce_last_start", _pack([dg0], (8,)))

    big_grads = dict(zip((big_names[o] for o in JOIN_GROUPS[1]), early_join[0].results))
    delta, new_m, new_v, grads_big = {}, {}, {}, {}

    def adamw_big(n):
        d2, m2, v2, g2 = _adamw(f"adamw_{n}", shard2d(w, n), big_grads[n], shard2d(mom, n), shard2d(vel, n), True,
                                tie=grad_token)
        delta[n], new_m[n], new_v[n], grads_big[n] = (t.reshape(w[n].shape) for t in (d2, m2, v2, g2))

    for o in JOIN_GROUPS[1]:
        adamw_big(big_names[o])
    late_join = join_job(JOIN_GROUPS[0], new_v[big_names[JOIN_GROUPS[1][-1]]])
    big_grads.update(zip((big_names[o] for o in JOIN_GROUPS[0]), _run_job("allsum_join_late", late_join)))
    for o in JOIN_GROUPS[0]:
        adamw_big(big_names[o])

    joined_last = big_grads[big_names[JOIN_GROUPS[0][-1]]]
    grad_blk, grad_land = _exchange8_wait("allreduce_small_wait", small["blk"], small["land"], small["send"],
                                          small["recv"], joined_last)
    grad_sum = _collect8("allreduce_small_sum", grad_blk, grad_land, True)
    last_blk, last_land = _exchange8_wait("allreduce_last_wait", last_blk, last_land, last_send, last_recv, joined_last)
    dg0_sum = _collect8("allreduce_last_sum", last_blk, last_land, True)
    loss = grad_sum[0, 0]

    grads = dict(grads_big)
    for n, results in _adamw_small(*state_packs, grad_sum, dg0_sum).items():
        grads[n], delta[n], new_m[n], new_v[n] = (t.reshape(w[n].shape) for t in results)

    out = [loss, dx[None]]
    for res in (grads, delta, new_m, new_v):
        out.extend(res[n] for n in names)
    return tuple(out)
```

```python
import functools

import jax
import jax.numpy as jnp
from jax import lax
from jax.experimental import pallas as pl
from jax.experimental.pallas import tpu as pltpu

F32 = jnp.float32
BF16 = jnp.bfloat16

T = 2048
D = 1024
CONV_CH = 512
CONV_W = 31
HEAD_DIM = 64
ATT_W = 1536
EVEN_IN = 5632
ODD_IN = 2560
SCONV_W = 3
SG_GROUPS = 4
CHUNK = 128
D_FF = 4096
EPS = 1e-6
DILATIONS = (1, 4, 16)
BAND = 128
SCALE = HEAD_DIM ** -0.5
NEG = -1e30

ADAM_LR = 0.001
ADAM_B1 = 0.9
ADAM_B2 = 0.999
ADAM_EPS = 1e-08
ADAM_WD = 0.01
ADAM_STEP = 10

V7X_VMEM_BYTES = 64 * 2 ** 20
VMEM_LIMIT = V7X_VMEM_BYTES - 8 * 2 ** 20
LANES = 128
TOKEN_SHAPE = (8, LANES)


SIBLING_COLLECTIVE_ID = 0


def _sibling_handshake():
    x, y, c = lax.axis_index("x"), lax.axis_index("y"), lax.axis_index("c")
    barrier = pltpu.get_barrier_semaphore()
    pl.semaphore_signal(barrier, inc=1, device_id=(x, y, 1 - c), device_id_type=pl.DeviceIdType.MESH)
    pl.semaphore_wait(barrier, 1)


class _SideJob:
    def __init__(self, inputs, out_shape, scratch_shapes, aliases, begin, advance, finish):
        self.inputs, self.out_shape, self.scratch_shapes = list(inputs), list(out_shape), list(scratch_shapes)
        self.aliases, self.begin, self.advance, self.finish = dict(aliases), begin, advance, finish
        self.results = None


_PENDING_JOBS = []


def _ride_next_call(job):
    _PENDING_JOBS.append(job)


def _pcall(body, **kw):
    if not _PENDING_JOBS or "grid" not in kw:
        return pl.pallas_call(body, **kw)
    job = _PENDING_JOBS.pop()
    as_list = lambda v: list(v) if isinstance(v, (list, tuple)) else [v]
    single_out = not isinstance(kw["out_shape"], (list, tuple))
    in_specs, out_specs, out_shape = as_list(kw["in_specs"]), as_list(kw["out_specs"]), as_list(kw["out_shape"])
    scratch = list(kw.get("scratch_shapes", ()))
    grid = kw["grid"]
    n_steps = 1
    for extent in grid:
        n_steps *= extent
    assert n_steps >= 3
    n_in, n_out, n_scr = len(in_specs), len(out_specs), len(scratch)
    j_in, j_out = len(job.inputs), len(job.out_shape)
    any_spec = pl.BlockSpec(memory_space=pl.ANY)

    def hosted(*refs):
        ins, j_ins = refs[:n_in], refs[n_in:n_in + j_in]
        outs = refs[n_in + j_in:n_in + j_in + n_out]
        j_outs = refs[n_in + j_in + n_out:n_in + j_in + n_out + j_out]
        scr = refs[n_in + j_in + n_out + j_out:n_in + j_in + n_out + j_out + n_scr]
        j_scr = refs[n_in + j_in + n_out + j_out + n_scr:]
        step = pl.program_id(0)
        for axis in range(1, len(grid)):
            step = step * grid[axis] + pl.program_id(axis)

        @pl.when(step == 0)
        def _():
            _sibling_handshake()
            job.begin(j_ins, j_outs, j_scr)

        @pl.when(step == 1)
        def _():
            job.advance(j_ins, j_outs, j_scr)

        body(*ins, *outs, *scr)

        @pl.when(step == n_steps - 1)
        def _():
            job.finish(j_ins, j_outs, j_scr)

    aliases = dict(kw.get("input_output_aliases", {}))
    aliases.update({n_in + a: n_out + b for a, b in job.aliases.items()})
    call = pl.pallas_call(
        hosted, name=kw["name"], grid=grid,
        in_specs=in_specs + [any_spec] * j_in, out_specs=out_specs + [any_spec] * j_out,
        out_shape=out_shape + job.out_shape, scratch_shapes=scratch + job.scratch_shapes,
        input_output_aliases=aliases,
        compiler_params=pltpu.CompilerParams(dimension_semantics=("arbitrary",) * len(grid), vmem_limit_bytes=VMEM_LIMIT,
                                             collective_id=SIBLING_COLLECTIVE_ID))

    def run(*args):
        res = call(*args, *job.inputs)
        job.results = list(res[n_out:])
        return res[0] if single_out else list(res[:n_out])

    return run


def _params(*sem):
    return pltpu.CompilerParams(dimension_semantics=sem, vmem_limit_bytes=VMEM_LIMIT)


def _dot(a, b, dims):
    return lax.dot_general(a, b, (dims, ((), ())), preferred_element_type=F32)


def _nn(a, b):
    return _dot(a, b, ((1,), (0,)))


def _nt(a, b):
    return _dot(a, b, ((1,), (1,)))


def _tn(a, b):
    return _dot(a, b, ((0,), (0,)))


def _sigmoid(x):
    return 1.0 / (1.0 + jnp.exp(-x))


MM_VMEM_BUDGET = 40 * 2 ** 20


def _mm_tiles(mode, m, n, k, a_bytes, b_bytes, extra_bytes, out_bytes):
    def divisors(total, unit):
        return [t for t in range(unit, total + 1, unit) if total % t == 0]

    best = None
    for tm in divisors(m, LANES if mode == "tn" else 8):
        for tn in divisors(n, LANES):
            blocks = tm * k * a_bytes + tn * k * b_bytes + tm * tn * (extra_bytes + out_bytes)
            casts = (tm * k * 2 if a_bytes == 4 else 0) + (tn * k * 2 if b_bytes == 4 else 0)
            if 2 * blocks + casts + tm * tn * 4 > MM_VMEM_BUDGET:
                continue
            key = ((m // tm) * (n // tn), (m // tm) * n * k * b_bytes, abs(tm - tn))
            if best is None or key < best[0]:
                best = (key, tm, tn)
    return best[1], best[2]


def _mm(name, mode, a, b, m, n, k, out_dtypes, *, b_off=0, extras=(), epi=None, tie=None):
    tm, tn = _mm_tiles(mode, m, n, k, a.dtype.itemsize, b.dtype.itemsize, sum(e.dtype.itemsize for e in extras),
                       sum(jnp.dtype(dt).itemsize for dt in out_dtypes))
    assert b_off % tn == 0
    b_off //= tn
    if mode == "nn":
        a_spec = pl.BlockSpec((tm, k), lambda i, j: (i, 0))
        b_spec = pl.BlockSpec((k, tn), lambda i, j: (0, j + b_off))
        dims = ((1,), (0,))
    elif mode == "nt":
        a_spec = pl.BlockSpec((tm, k), lambda i, j: (i, 0))
        b_spec = pl.BlockSpec((tn, k), lambda i, j: (j, 0))
        dims = ((1,), (1,))
    else:
        a_spec = pl.BlockSpec((k, tm), lambda i, j: (0, i))
        b_spec = pl.BlockSpec((k, tn), lambda i, j: (0, j))
        dims = ((0,), (0,))
    o_spec = pl.BlockSpec((tm, tn), lambda i, j: (i, j))
    n_extra = len(extras)
    ties = () if tie is None else (tie,)

    def body(a_ref, b_ref, *rest):
        rest = rest[len(ties):]
        acc = _dot(a_ref[...].astype(BF16), b_ref[...].astype(BF16), dims)
        vals = epi(acc, *[e[...] for e in rest[:n_extra]]) if epi is not None else (acc,)
        for o_ref, v in zip(rest[n_extra:], vals):
            o_ref[...] = v.astype(o_ref.dtype)

    outs = _pcall(
        body, name=name, grid=(m // tm, n // tn),
        in_specs=[a_spec, b_spec] + [pl.BlockSpec(TOKEN_SHAPE, lambda i, j: (0, 0))] * len(ties) + [o_spec] * n_extra,
        out_specs=[o_spec] * len(out_dtypes),
        out_shape=[jax.ShapeDtypeStruct((m, n), dt) for dt in out_dtypes],
        compiler_params=_params("parallel", "parallel"),
    )(a, b, *ties, *extras)
    return outs[0] if len(out_dtypes) == 1 else outs


def _row_tile(k, a_bytes, n_row_blocks):
    for tm in (1024, 512, 256, 128):
        if 2 * (tm * k * a_bytes + n_row_blocks * tm * D * 4) + D * k * 2 + tm * D * 4 <= MM_VMEM_BUDGET + 4 * 2 ** 20:
            return tm
    raise ValueError("no row tile fits")


def _resident(shape):
    return pl.BlockSpec(shape, lambda i: (0, 0), pipeline_mode=pl.Buffered(1))


FFN0_DOWN_TILE = 256


def _mm_out_norm(name, a, b, k, res, g_next, tm=None):
    tm = tm or _row_tile(k, a.dtype.itemsize, 3)

    def body(a_ref, b_ref, r_ref, g_ref, h_ref, hn_ref):
        h = _nn(a_ref[...].astype(BF16), b_ref[...]) + r_ref[...]
        h_ref[...] = h
        r = lax.rsqrt(jnp.mean(h * h, axis=-1, keepdims=True) + EPS)
        hn_ref[...] = ((h * r) * g_ref[...]).astype(BF16)

    row = pl.BlockSpec((tm, D), lambda i: (i, 0))
    return _pcall(
        body, name=name, grid=(T // tm,),
        in_specs=[pl.BlockSpec((tm, k), lambda i: (i, 0)), _resident((k, D)), row,
                  pl.BlockSpec((1, D), lambda i: (0, 0))],
        out_specs=[row, row],
        out_shape=[jax.ShapeDtypeStruct((T, D), F32), jax.ShapeDtypeStruct((T, D), BF16)],
        compiler_params=_params("parallel"),
    )(a, b, res, g_next)


def _ffn_last(name, hn, w1, w2, res, g, target):
    tm = FFN_BWD_TILE

    def body(a_ref, w1_ref, w2_ref, r_ref, g_ref, t_ref, f_ref, dh_ref, dg_ref, loss_ref):
        u = jnp.maximum(_nn(a_ref[...], w1_ref[...]), 0.0)
        f = (u * u).astype(BF16)
        f_ref[...] = f
        x = _nn(f, w2_ref[...]) + r_ref[...]
        r = lax.rsqrt(jnp.mean(x * x, axis=-1, keepdims=True) + EPS)
        nrm = x * r
        gain = g_ref[...]
        err = nrm * gain - t_ref[...]
        dy = err * (1.0 / D)
        dn = dy * gain
        dh_ref[...] = r * (dn - nrm * jnp.mean(dn * nrm, axis=-1, keepdims=True))

        @pl.when(pl.program_id(0) == 0)
        def _():
            dg_ref[...] = jnp.zeros_like(dg_ref)
            loss_ref[...] = jnp.zeros_like(loss_ref)

        dg_ref[...] += jnp.sum(dy * nrm, axis=0, keepdims=True)
        part = jnp.sum(jnp.sum(err * err, axis=1, keepdims=True), axis=0, keepdims=True) * (0.5 / D)
        loss_ref[...] += jnp.broadcast_to(part, (1, LANES))

    row = pl.BlockSpec((tm, D), lambda i: (i, 0))
    wide = pl.BlockSpec((tm, D_FF), lambda i: (i, 0))
    vec = pl.BlockSpec((1, D), lambda i: (0, 0))
    return _pcall(
        body, name=name, grid=(T // tm,),
        in_specs=[row, _resident((D, D_FF)), _resident((D_FF, D)), row, vec, row],
        out_specs=[wide, row, vec, pl.BlockSpec((1, LANES), lambda i: (0, 0))],
        out_shape=[jax.ShapeDtypeStruct((T, D_FF), BF16), jax.ShapeDtypeStruct((T, D), F32),
                   jax.ShapeDtypeStruct((1, D), F32), jax.ShapeDtypeStruct((1, LANES), F32)],
        compiler_params=_params("arbitrary"),
    )(hn, w1, w2, res, g, target)


def _mm_dx_norm(name, dz, w, k, h, g, dres, tie=None):
    tm = _row_tile(k, dz.dtype.itemsize, 3)
    ties = () if tie is None else (tie,)

    def body(a_ref, b_ref, *rest):
        h_ref, g_ref, r_ref, dh_ref, dg_ref = rest[len(ties):]
        dy = _nt(a_ref[...].astype(BF16), b_ref[...])
        x = h_ref[...]
        r = lax.rsqrt(jnp.mean(x * x, axis=-1, keepdims=True) + EPS)
        nrm = x * r
        dn = dy * g_ref[...]
        dh_ref[...] = r_ref[...] + r * (dn - nrm * jnp.mean(dn * nrm, axis=-1, keepdims=True))

        @pl.when(pl.program_id(0) == 0)
        def _():
            dg_ref[...] = jnp.zeros_like(dg_ref)

        dg_ref[...] += jnp.sum(dy * nrm, axis=0, keepdims=True)

    row = pl.BlockSpec((tm, D), lambda i: (i, 0))
    vec = pl.BlockSpec((1, D), lambda i: (0, 0))
    return _pcall(
        body, name=name, grid=(T // tm,),
        in_specs=[pl.BlockSpec((tm, k), lambda i: (i, 0)), _resident((D, k))]
        + [pl.BlockSpec(TOKEN_SHAPE, lambda i: (0, 0))] * len(ties) + [row, vec, row],
        out_specs=[row, vec],
        out_shape=[jax.ShapeDtypeStruct((T, D), F32), jax.ShapeDtypeStruct((1, D), F32)],
        compiler_params=_params("arbitrary"),
    )(dz, w, *ties, h, g, dres)


def _rms_fwd(name, h, g, tm=512):
    def body(h_ref, g_ref, o_ref):
        x = h_ref[...]
        r = lax.rsqrt(jnp.mean(x * x, axis=-1, keepdims=True) + EPS)
        o_ref[...] = ((x * r) * g_ref[...]).astype(BF16)

    return _pcall(
        body, name=name, grid=(T // tm,),
        in_specs=[pl.BlockSpec((tm, D), lambda i: (i, 0)), pl.BlockSpec((1, D), lambda i: (0, 0))],
        out_specs=pl.BlockSpec((tm, D), lambda i: (i, 0)),
        out_shape=jax.ShapeDtypeStruct((T, D), BF16),
        compiler_params=_params("parallel"),
    )(h, g)


CONV_TILE = 256
CONV_HALO = 32


def _glu(z):
    return z[:, :CONV_CH] * _sigmoid(z[:, CONV_CH:])


SUBLANES = 8


def _sublane_shifts(win):
    n = win.shape[0]
    return [win] + [win[r:r + n - SUBLANES, :] for r in range(1, SUBLANES)]


def _rows_from(shifts, off, n):
    q, r = divmod(off, SUBLANES)
    return shifts[r][q * SUBLANES:q * SUBLANES + n, :]


def _econv_fwd(zc, conv_k, conv_b, ln_g, ln_b):
    R, H = CONV_TILE, CONV_HALO

    def body(z_ref, zh_ref, k_ref, b_ref, g_ref, be_ref, cv_ref, cat_ref):
        i = pl.program_id(0)
        glu = _glu(z_ref[...])
        halo = _glu(zh_ref[...]) * (i > 0).astype(F32)
        win = _sublane_shifts(jnp.concatenate([halo, glu], axis=0))
        acc = jnp.zeros((R, CONV_CH), F32) + b_ref[...]
        for j in range(CONV_W):
            acc = acc + k_ref[j:j + 1, :] * _rows_from(win, H - (CONV_W - 1) + j, R)
        cv_ref[...] = acc
        mu = jnp.mean(acc, axis=-1, keepdims=True)
        xc = acc - mu
        rstd = lax.rsqrt(jnp.mean(xc * xc, axis=-1, keepdims=True) + EPS)
        ln = xc * rstd * g_ref[...] + be_ref[...]
        cat_ref[...] = (ln * _sigmoid(ln)).astype(BF16)

    vec = pl.BlockSpec((1, CONV_CH), lambda i: (0, 0))
    return _pcall(
        body, name="econv_fwd", grid=(T // R,),
        in_specs=[pl.BlockSpec((R, 2 * CONV_CH), lambda i: (i, 0)),
                  pl.BlockSpec((H, 2 * CONV_CH), lambda i: (jnp.maximum(i * (R // H) - 1, 0), 0)),
                  pl.BlockSpec((CONV_W, CONV_CH), lambda i: (0, 0)), vec, vec, vec],
        out_specs=[pl.BlockSpec((R, CONV_CH), lambda i: (i, 0)), pl.BlockSpec((R, CONV_CH), lambda i: (i, 0))],
        out_shape=[jax.ShapeDtypeStruct((T, CONV_CH), F32), jax.ShapeDtypeStruct((T, D), BF16)],
        compiler_params=_params("parallel"),
    )(zc, zc, conv_k, conv_b, ln_g, ln_b)


def _econv_bwd_ln(cv, dcat, ln_g, ln_b):
    R = CONV_TILE

    def body(cv_ref, d_ref, g_ref, be_ref, dcv_ref, dg_ref, dbe_ref, dcb_ref):
        cv_t = cv_ref[...]
        mu = jnp.mean(cv_t, axis=-1, keepdims=True)
        xc = cv_t - mu
        rstd = lax.rsqrt(jnp.mean(xc * xc, axis=-1, keepdims=True) + EPS)
        xh = xc * rstd
        ln = xh * g_ref[...] + be_ref[...]
        sg = _sigmoid(ln)
        dln = d_ref[...] * (sg * (1.0 + ln * (1.0 - sg)))
        dxh = dln * g_ref[...]
        dcv = rstd * (dxh - jnp.mean(dxh, axis=-1, keepdims=True) - xh * jnp.mean(dxh * xh, axis=-1, keepdims=True))
        dcv_ref[...] = dcv

        @pl.when(pl.program_id(0) == 0)
        def _():
            dg_ref[...] = jnp.zeros_like(dg_ref)
            dbe_ref[...] = jnp.zeros_like(dbe_ref)
            dcb_ref[...] = jnp.zeros_like(dcb_ref)

        dg_ref[...] += jnp.sum(dln * xh, axis=0, keepdims=True)
        dbe_ref[...] += jnp.sum(dln, axis=0, keepdims=True)
        dcb_ref[...] += jnp.sum(dcv, axis=0, keepdims=True)

    vec = pl.BlockSpec((1, CONV_CH), lambda i: (0, 0))
    row = pl.BlockSpec((R, CONV_CH), lambda i: (i, 0))
    vshape = jax.ShapeDtypeStruct((1, CONV_CH), F32)
    return _pcall(
        body, name="econv_bwd_ln", grid=(T // R,),
        in_specs=[row, row, vec, vec], out_specs=[row, vec, vec, vec],
        out_shape=[jax.ShapeDtypeStruct((T, CONV_CH), F32), vshape, vshape, vshape],
        compiler_params=_params("arbitrary"),
    )(cv, dcat, ln_g, ln_b)


def _econv_bwd_conv(dcv, zc, conv_k):
    R, H = CONV_TILE, CONV_HALO
    last = T // R - 1

    def body(d_ref, dn_ref, z_ref, zh_ref, k_ref, dz_ref, dk_ref):
        i = pl.program_id(0)
        z = z_ref[...]
        a_lin = z[:, :CONV_CH]
        sg = _sigmoid(z[:, CONV_CH:])
        glu = a_lin * sg
        halo = _glu(zh_ref[...]) * (i > 0).astype(F32)
        win = _sublane_shifts(jnp.concatenate([halo, glu], axis=0))
        dcv_t = d_ref[...]
        nxt = dn_ref[...] * (i < last).astype(F32)
        winb = _sublane_shifts(jnp.concatenate([dcv_t, nxt], axis=0))

        @pl.when(i == 0)
        def _():
            dk_ref[...] = jnp.zeros_like(dk_ref)

        dglu = jnp.zeros((R, CONV_CH), F32)
        for j in range(CONV_W):
            dk_ref[j:j + 1, :] += jnp.sum(dcv_t * _rows_from(win, H - (CONV_W - 1) + j, R), axis=0, keepdims=True)
            dglu = dglu + k_ref[j:j + 1, :] * _rows_from(winb, CONV_W - 1 - j, R)
        dz_ref[...] = jnp.concatenate([dglu * sg, dglu * a_lin * sg * (1.0 - sg)], axis=1).astype(BF16)

    return _pcall(
        body, name="econv_bwd_conv", grid=(T // R,),
        in_specs=[pl.BlockSpec((R, CONV_CH), lambda i: (i, 0)),
                  pl.BlockSpec((H, CONV_CH), lambda i: (jnp.minimum((i + 1) * (R // H), T // H - 1), 0)),
                  pl.BlockSpec((R, 2 * CONV_CH), lambda i: (i, 0)),
                  pl.BlockSpec((H, 2 * CONV_CH), lambda i: (jnp.maximum(i * (R // H) - 1, 0), 0)),
                  pl.BlockSpec((CONV_W, CONV_CH), lambda i: (0, 0))],
        out_specs=[pl.BlockSpec((R, 2 * CONV_CH), lambda i: (i, 0)), pl.BlockSpec((CONV_W, CONV_CH), lambda i: (0, 0))],
        out_shape=[jax.ShapeDtypeStruct((T, EVEN_IN), BF16), jax.ShapeDtypeStruct((CONV_W, CONV_CH), F32)],
        compiler_params=_params("arbitrary"),
    )(dcv, dcv, zc, zc, conv_k)


def _swap_halves(v):
    lane = lax.broadcasted_iota(jnp.int32, v.shape, 1)
    return jnp.where((lane % HEAD_DIM) < HEAD_DIM // 2, pltpu.roll(v, LANES - HEAD_DIM // 2, 1),
                     pltpu.roll(v, HEAD_DIM // 2, 1))


def _qkv_proj(hn, w_in, rope_c, rope_s, tm=T):
    tn = 4 * LANES

    def body(a_ref, b_ref, c_ref, s_ref, o_ref):
        j = pl.program_id(1)
        acc = _nn(a_ref[...], b_ref[...])
        for p in range(4):
            v = acc[:, p * LANES:(p + 1) * LANES]
            rot = v * c_ref[...] + _swap_halves(v) * s_ref[...]
            o_ref[p] = jnp.where(j < 6, rot, v)

    tab = pl.BlockSpec((tm, LANES), lambda i, j: (i, 0))
    return _pcall(
        body, name="qkv_proj", grid=(T // tm, 9),
        in_specs=[pl.BlockSpec((tm, D), lambda i, j: (i, 0)),
                  pl.BlockSpec((D, tn), lambda i, j: (0, j + (2 * CONV_CH) // tn)), tab, tab],
        out_specs=pl.BlockSpec((None, 4, tm, LANES), lambda i, j: (j, 0, i, 0)),
        out_shape=jax.ShapeDtypeStruct((9, 4, T, LANES), F32),
        compiler_params=_params("parallel", "parallel"),
    )(hn, w_in, rope_c, rope_s)


ATTN_FWD_UNROLL = 4
ATTN_BWD_UNROLL = 4


def _band_rows(start, d):
    if d == 1:
        return pl.ds(pl.multiple_of(start, BAND), BAND)
    return pl.ds(start, BAND, stride=d)


def _band_masks(n):
    row = lax.broadcasted_iota(jnp.int32, (BAND, BAND), 0)
    col = lax.broadcasted_iota(jnp.int32, (BAND, BAND), 1)
    no_prev = (n == 0).astype(jnp.int32) * (2 * BAND)
    return col <= row, col >= row + no_prev


def _attn_fwd(qkv, g):
    d = DILATIONS[g]
    nb = T // d // BAND
    has_prev = nb > 1

    def body(q_ref, k_ref, v_ref, o_ref, l_ref):
        lane_lo = lax.broadcasted_iota(jnp.int32, (BAND, LANES), 1) < HEAD_DIM

        heads = (lane_lo, jnp.logical_not(lane_lo))
        ones = jnp.ones((BAND, LANES), BF16)

        def step(it, carry):
            tiles = []
            for u in range(ATTN_FWD_UNROLL):
                idx = it * ATTN_FWD_UNROLL + u
                r = idx // nb
                n = idx % nb
                cur = _band_rows(n * (BAND * d) + r, d)
                prev = _band_rows(jnp.maximum(n - 1, 0) * (BAND * d) + r, d)
                mc, mp = _band_masks(n)
                kp = k_ref[prev, :].astype(BF16) if has_prev else None
                vp = v_ref[prev, :].astype(BF16) if has_prev else None
                tiles.append((cur, mc, mp, q_ref[cur, :], k_ref[cur, :].astype(BF16), v_ref[cur, :].astype(BF16), kp, vp))
            scores = []
            for cur, mc, mp, q, kc, vc, kp, vp in tiles:
                for hm in heads:
                    qm = jnp.where(hm, q, 0.0).astype(BF16)
                    sc = jnp.where(mc, _nt(qm, kc) * SCALE, NEG)
                    scores.append((sc, jnp.where(mp, _nt(qm, kp) * SCALE, NEG)) if has_prev else (sc,))
            maxes = [functools.reduce(jnp.maximum, [jnp.max(sx, axis=1, keepdims=True) for sx in ss]) for ss in scores]
            probs = [[jnp.exp(sx - mx).astype(BF16) for sx in ss] for ss, mx in zip(scores, maxes)]
            dens = [functools.reduce(jnp.add, [_nn(px, ones) for px in ps]) for ps in probs]
            for t, (cur, mc, mp, q, kc, vc, kp, vp) in enumerate(tiles):
                outs, lses = [], []
                for h in range(2):
                    ps = probs[2 * t + h]
                    acc = _nn(ps[0], vc) + _nn(ps[1], vp) if has_prev else _nn(ps[0], vc)
                    outs.append(acc / dens[2 * t + h])
                    lses.append(maxes[2 * t + h] + jnp.log(dens[2 * t + h]))
                o_ref[cur, :] = jnp.where(lane_lo, outs[0], outs[1])
                l_ref[cur, :] = jnp.where(lane_lo, lses[0], lses[1])
            return carry

        lax.fori_loop(0, d * nb // ATTN_FWD_UNROLL, step, 0)

    def slab(which):
        return pl.BlockSpec((None, None, T, LANES), lambda p: (which * 3 + g, p, 0, 0))

    out = pl.BlockSpec((None, T, LANES), lambda p: (p, 0, 0))
    shape = jax.ShapeDtypeStruct((4, T, LANES), F32)
    return _pcall(
        body, name=f"attn_fwd{g}", grid=(4,),
        in_specs=[slab(0), slab(1), slab(2)], out_specs=[out, out], out_shape=[shape, shape],
        compiler_params=_params("parallel"),
    )(qkv, qkv, qkv)


def _attn_merge(outs, lses, cat, tm=1024):
    def body(o0, o1, o2, l0, l1, l2, cat_in, cat_ref, att_ref, w0, w1, w2):
        del cat_in
        la, lb, lc = l0[...], l1[...], l2[...]
        mx = jnp.maximum(jnp.maximum(la, lb), lc)
        ea, eb, ec = jnp.exp(la - mx), jnp.exp(lb - mx), jnp.exp(lc - mx)
        inv = 1.0 / (ea + eb + ec)
        wa, wb, wc = ea * inv, eb * inv, ec * inv
        att = wa * o0[...] + wb * o1[...] + wc * o2[...]
        att_ref[...] = att
        cat_ref[...] = att.astype(BF16)
        w0[...] = wa
        w1[...] = wb
        w2[...] = wc

    slab = pl.BlockSpec((None, tm, LANES), lambda p, i: (p, i, 0))
    shape = jax.ShapeDtypeStruct((4, T, LANES), F32)
    return _pcall(
        body, name="attn_merge", grid=(4, T // tm),
        in_specs=[slab] * 6 + [pl.BlockSpec(memory_space=pl.ANY)],
        out_specs=[pl.BlockSpec((tm, LANES), lambda p, i: (i, CONV_CH // LANES + p)), slab, slab, slab, slab],
        out_shape=[jax.ShapeDtypeStruct((T, D), BF16), shape, shape, shape, shape],
        input_output_aliases={6: 0},
        compiler_params=_params("parallel", "parallel"),
    )(*outs, *lses, cat)


def _attn_bwd(qkv, lse, wgt, att, dcat, dqkv, g):
    d = DILATIONS[g]
    nb = T // d // BAND
    has_prev = nb > 1

    def body(q_ref, k_ref, v_ref, l_ref, w_ref, a_ref, da_ref, dq_in, o_ref):
        del dq_in
        lane = lax.broadcasted_iota(jnp.int32, (BAND, LANES), 1)
        lane_lo = lane < HEAD_DIM
        row = lax.broadcasted_iota(jnp.int32, (LANES, LANES), 0)
        same_head = ((row // HEAD_DIM) == (lane // HEAD_DIM)).astype(BF16)
        dq_ref, dk_ref, dv_ref = o_ref.at[0], o_ref.at[1], o_ref.at[2]
        if has_prev:
            dk_ref[...] = jnp.zeros((T, LANES), F32)
            dv_ref[...] = jnp.zeros((T, LANES), F32)

        heads = (lane_lo, jnp.logical_not(lane_lo))

        def step(it, carry):
            tiles = []
            for u in range(ATTN_BWD_UNROLL):
                idx = it * ATTN_BWD_UNROLL + u
                r = idx // nb
                n = idx % nb
                cur = _band_rows(n * (BAND * d) + r, d)
                prev = _band_rows(jnp.maximum(n - 1, 0) * (BAND * d) + r, d)
                mc, mp = _band_masks(n)
                da = da_ref[cur, :]
                prod = da * a_ref[cur, :]
                hi = prod.astype(BF16)
                lo = (prod - hi.astype(F32)).astype(BF16)
                tiles.append(dict(cur=cur, prev=prev, mc=mc, mp=mp, da=da, hi=hi, lo=lo, q=q_ref[cur, :],
                                  kc=k_ref[cur, :].astype(BF16), vc=v_ref[cur, :].astype(BF16),
                                  kp=k_ref[prev, :].astype(BF16) if has_prev else None,
                                  vp=v_ref[prev, :].astype(BF16) if has_prev else None,
                                  lse=l_ref[cur, :], w=w_ref[cur, :]))
            for t in tiles:
                t["csum"] = _nn(t["hi"], same_head) + _nn(t["lo"], same_head)
            chains = []
            for t in tiles:
                for h, hm in enumerate(heads):
                    qm = jnp.where(hm, t["q"], 0.0).astype(BF16)
                    dam = jnp.where(hm, t["da"], 0.0).astype(BF16)
                    ch = dict(t=t, h=h, qm=qm, dam=dam, sc=jnp.where(t["mc"], _nt(qm, t["kc"]) * SCALE, NEG),
                              dpc=_nt(dam, t["vc"]))
                    if has_prev:
                        ch.update(sp=jnp.where(t["mp"], _nt(qm, t["kp"]) * SCALE, NEG), dpp=_nt(dam, t["vp"]))
                    chains.append(ch)
            for ch in chains:
                t, col0 = ch["t"], ch["h"] * HEAD_DIM
                lse_h = t["lse"][:, col0:col0 + 1]
                w_h = t["w"][:, col0:col0 + 1]
                c_h = t["csum"][:, col0:col0 + 1]
                pwc = w_h * jnp.exp(ch["sc"] - lse_h)
                ch["dsc"] = (pwc * (ch["dpc"] - c_h) * SCALE).astype(BF16)
                ch["pwc"] = pwc.astype(BF16)
                if has_prev:
                    pwp = w_h * jnp.exp(ch["sp"] - lse_h)
                    ch["dsp"] = (pwp * (ch["dpp"] - c_h) * SCALE).astype(BF16)
                    ch["pwp"] = pwp.astype(BF16)
            for ch in chains:
                t = ch["t"]
                ch["dq"] = _nn(ch["dsc"], t["kc"])
                ch["dkc"] = _tn(ch["dsc"], ch["qm"])
                ch["dvc"] = _tn(ch["pwc"], ch["dam"])
                if has_prev:
                    ch["dq"] = ch["dq"] + _nn(ch["dsp"], t["kp"])
                    ch["dkp"] = _tn(ch["dsp"], ch["qm"])
                    ch["dvp"] = _tn(ch["pwp"], ch["dam"])
            for i, t in enumerate(tiles):
                c0, c1 = chains[2 * i], chains[2 * i + 1]
                dq_ref[t["cur"], :] = jnp.where(lane_lo, c0["dq"], c1["dq"])
                if has_prev:
                    dk_ref[t["cur"], :] += c0["dkc"] + c1["dkc"]
                    dk_ref[t["prev"], :] += c0["dkp"] + c1["dkp"]
                    dv_ref[t["cur"], :] += c0["dvc"] + c1["dvc"]
                    dv_ref[t["prev"], :] += c0["dvp"] + c1["dvp"]
                else:
                    dk_ref[t["cur"], :] = c0["dkc"] + c1["dkc"]
                    dv_ref[t["cur"], :] = c0["dvc"] + c1["dvc"]
            return carry

        lax.fori_loop(0, d * nb // ATTN_BWD_UNROLL, step, 0)

    def slab(which):
        return pl.BlockSpec((None, None, T, LANES), lambda p: (which * 3 + g, p, 0, 0))

    per_pair = pl.BlockSpec((None, T, LANES), lambda p: (p, 0, 0))
    return _pcall(
        body, name=f"attn_bwd{g}", grid=(4,),
        in_specs=[slab(0), slab(1), slab(2), per_pair, per_pair, per_pair,
                  pl.BlockSpec((T, LANES), lambda p: (0, CONV_CH // LANES + p)),
                  pl.BlockSpec(memory_space=pl.ANY)],
        out_specs=pl.BlockSpec((None, 3, None, T, LANES), lambda p: (g, 0, p, 0, 0)),
        out_shape=jax.ShapeDtypeStruct((3, 3, 4, T, LANES), F32),
        input_output_aliases={7: 0},
        compiler_params=_params("parallel"),
    )(qkv, qkv, qkv, lse, wgt, att, dcat, dqkv)


def _rope_bwd(dqkv, rope_c, rope_s, dz):
    wide = 4 * LANES

    def body(d_ref, c_ref, s_ref, dz_in, o_ref):
        del dz_in
        w = pl.program_id(1)
        for p in range(4):
            v = d_ref[p]
            rot = v * c_ref[...] + _swap_halves(v * s_ref[...])
            o_ref[:, p * LANES:(p + 1) * LANES] = jnp.where(w < 2, rot, v).astype(BF16)

    tab = pl.BlockSpec((T, LANES), lambda g, w: (0, 0))
    return _pcall(
        body, name="rope_bwd", grid=(3, 3),
        in_specs=[pl.BlockSpec((None, None, 4, T, LANES), lambda g, w: (g, w, 0, 0, 0)), tab, tab,
                  pl.BlockSpec(memory_space=pl.ANY)],
        out_specs=pl.BlockSpec((T, wide), lambda g, w: (0, (2 * CONV_CH) // wide + w * 3 + g)),
        out_shape=jax.ShapeDtypeStruct((T, EVEN_IN), BF16),
        input_output_aliases={3: 0},
        compiler_params=_params("parallel", "parallel"),
    )(dqkv, rope_c, rope_s, dz)


ODD_TILE = 256
ODD_HALO = 8
GELU_C = 0.7978845608028654
GELU_A = 0.044715


def _gelu(x):
    return 0.5 * x * (1.0 + jnp.tanh(GELU_C * (x + GELU_A * x * x * x)))


def _gelu_grad(x):
    th = jnp.tanh(GELU_C * (x + GELU_A * x * x * x))
    return 0.5 * (1.0 + th) + 0.5 * x * (1.0 - th * th) * GELU_C * (1.0 + 3.0 * GELU_A * x * x)


def _tril():
    row = lax.broadcasted_iota(jnp.int32, (CHUNK, CHUNK), 0)
    col = lax.broadcasted_iota(jnp.int32, (CHUNK, CHUNK), 1)
    return (col <= row).astype(F32)


def _odd_parts(z, zh, i, k_ref, g_ref, be_ref, w_ref, bt_ref):
    R, H = ODD_TILE, ODD_HALO
    gb, gc, xs, uv = z[:, :512], z[:, 512:1024], z[:, 1024:1536], z[:, 1536:]
    halo = zh[:, 512:1024] * zh[:, 1024:1536] * (i > 0).astype(F32)
    win = jnp.concatenate([halo, gc * xs], axis=0)
    cv = jnp.zeros((R, 512), F32)
    for j in range(SCONV_W):
        off = H - (SCONV_W - 1) + j
        cv = cv + k_ref[j:j + 1, :] * win[off:off + R, :]
    ge = _gelu(uv)
    u, v = ge[:, :512], ge[:, 512:]
    mu = jnp.mean(v, axis=-1, keepdims=True)
    xc = v - mu
    rstd = lax.rsqrt(jnp.mean(xc * xc, axis=-1, keepdims=True) + EPS)
    xh = xc * rstd
    vn = xh * g_ref[...] + be_ref[...]
    tril = _tril()
    wms = [(w_ref[g] * tril).astype(BF16) for g in range(SG_GROUPS)]
    rows = []
    for ci in range(R // CHUNK):
        blocks = []
        for g in range(SG_GROUPS):
            blk = vn[ci * CHUNK:(ci + 1) * CHUNK, g * LANES:(g + 1) * LANES].astype(BF16)
            blocks.append(_nn(wms[g], blk) + bt_ref[:, g:g + 1])
        rows.append(jnp.concatenate(blocks, axis=1))
    vmix = jnp.concatenate(rows, axis=0)
    return gb, gc, xs, uv, win, cv, u, rstd, xh, vn, vmix, wms


def _odd_mid_fwd(z, conv_k, ln_g, ln_b, sg_w, sg_bt):
    R, H = ODD_TILE, ODD_HALO

    def body(z_ref, zh_ref, k_ref, g_ref, be_ref, w_ref, bt_ref, o_ref):
        i = pl.program_id(0)
        gb, _, _, _, _, cv, u, _, _, _, vmix, _ = _odd_parts(z_ref[...], zh_ref[...], i, k_ref, g_ref, be_ref, w_ref, bt_ref)
        o_ref[...] = jnp.concatenate([gb * cv, u * vmix], axis=1).astype(BF16)

    vec = pl.BlockSpec((1, 512), lambda i: (0, 0))
    return _pcall(
        body, name="odd_mid_fwd", grid=(T // R,),
        in_specs=[pl.BlockSpec((R, ODD_IN), lambda i: (i, 0)),
                  pl.BlockSpec((H, ODD_IN), lambda i: (jnp.maximum(i * (R // H) - 1, 0), 0)),
                  pl.BlockSpec((SCONV_W, 512), lambda i: (0, 0)), vec, vec,
                  pl.BlockSpec((SG_GROUPS, CHUNK, CHUNK), lambda i: (0, 0, 0)),
                  pl.BlockSpec((CHUNK, SG_GROUPS), lambda i: (0, 0))],
        out_specs=pl.BlockSpec((R, D), lambda i: (i, 0)),
        out_shape=jax.ShapeDtypeStruct((T, D), BF16),
        compiler_params=_params("parallel"),
    )(z, z, conv_k, ln_g, ln_b, sg_w, sg_bt)


def _odd_mid_bwd(z, dcat, conv_k, ln_g, ln_b, sg_w, sg_bt):
    R, H = ODD_TILE, ODD_HALO
    last = T // R - 1

    def body(z_ref, zh_ref, zn_ref, d_ref, dn_ref, k_ref, g_ref, be_ref, w_ref, bt_ref,
             dz_ref, dk_ref, dg_ref, dbe_ref, dw_ref, dbt_ref):
        i = pl.program_id(0)
        z = z_ref[...]
        gb, gc, xs, uv, win, cv, u, rstd, xh, vn, vmix, wms = _odd_parts(z, zh_ref[...], i, k_ref, g_ref, be_ref, w_ref, bt_ref)
        dcat_t = d_ref[...]
        dc, dd = dcat_t[:, :512], dcat_t[:, 512:]

        @pl.when(i == 0)
        def _():
            dk_ref[...] = jnp.zeros_like(dk_ref)
            dg_ref[...] = jnp.zeros_like(dg_ref)
            dbe_ref[...] = jnp.zeros_like(dbe_ref)
            dw_ref[...] = jnp.zeros_like(dw_ref)
            dbt_ref[...] = jnp.zeros_like(dbt_ref)

        dgb = dc * cv
        dcv = dc * gb
        nxt = dn_ref[:, :512] * zn_ref[:, :512] * (i < last).astype(F32)
        winb = jnp.concatenate([dcv, nxt], axis=0)
        dp = jnp.zeros((R, 512), F32)
        for j in range(SCONV_W):
            off = H - (SCONV_W - 1) + j
            dk_ref[j:j + 1, :] += jnp.sum(dcv * win[off:off + R, :], axis=0, keepdims=True)
            ob = SCONV_W - 1 - j
            dp = dp + k_ref[j:j + 1, :] * winb[ob:ob + R, :]
        dgc = dp * xs
        dxs = dp * gc
        du = dd * vmix
        dvmix = dd * u
        tril = _tril()
        rows = []
        for ci in range(R // CHUNK):
            blocks = []
            for g in range(SG_GROUPS):
                sl = (slice(ci * CHUNK, (ci + 1) * CHUNK), slice(g * LANES, (g + 1) * LANES))
                dblk = dvmix[sl]
                dblk16 = dblk.astype(BF16)
                blocks.append(_tn(wms[g], dblk16))
                dw_ref[g] += _nt(dblk16, vn[sl].astype(BF16)) * tril
                dbt_ref[:, g:g + 1] += jnp.sum(dblk, axis=1, keepdims=True)
            rows.append(jnp.concatenate(blocks, axis=1))
        dvn = jnp.concatenate(rows, axis=0)
        dg_ref[...] += jnp.sum(dvn * xh, axis=0, keepdims=True)
        dbe_ref[...] += jnp.sum(dvn, axis=0, keepdims=True)
        dxh = dvn * g_ref[...]
        dv = rstd * (dxh - jnp.mean(dxh, axis=-1, keepdims=True) - xh * jnp.mean(dxh * xh, axis=-1, keepdims=True))
        duv = jnp.concatenate([du, dv], axis=1) * _gelu_grad(uv)
        dz_ref[...] = jnp.concatenate([dgb, dgc, dxs, duv], axis=1).astype(BF16)

    vec = pl.BlockSpec((1, 512), lambda i: (0, 0))
    kspec = pl.BlockSpec((SCONV_W, 512), lambda i: (0, 0))
    wspec = pl.BlockSpec((SG_GROUPS, CHUNK, CHUNK), lambda i: (0, 0, 0))
    bspec = pl.BlockSpec((CHUNK, SG_GROUPS), lambda i: (0, 0))
    nxt_blk = lambda i: (jnp.minimum((i + 1) * (R // H), T // H - 1), 0)
    return _pcall(
        body, name="odd_mid_bwd", grid=(T // R,),
        in_specs=[pl.BlockSpec((R, ODD_IN), lambda i: (i, 0)),
                  pl.BlockSpec((H, ODD_IN), lambda i: (jnp.maximum(i * (R // H) - 1, 0), 0)),
                  pl.BlockSpec((H, ODD_IN), nxt_blk),
                  pl.BlockSpec((R, D), lambda i: (i, 0)),
                  pl.BlockSpec((H, D), nxt_blk),
                  kspec, vec, vec, wspec, bspec],
        out_specs=[pl.BlockSpec((R, ODD_IN), lambda i: (i, 0)), kspec, vec, vec, wspec, bspec],
        out_shape=[jax.ShapeDtypeStruct((T, ODD_IN), BF16), jax.ShapeDtypeStruct((SCONV_W, 512), F32),
                   jax.ShapeDtypeStruct((1, 512), F32), jax.ShapeDtypeStruct((1, 512), F32),
                   jax.ShapeDtypeStruct((SG_GROUPS, CHUNK, CHUNK), F32), jax.ShapeDtypeStruct((CHUNK, SG_GROUPS), F32)],
        compiler_params=_params("arbitrary"),
    )(z, z, z, dcat, dcat, conv_k, ln_g, ln_b, sg_w, sg_bt)


def _ffn_up(tag, hn, weight):
    def act(acc):
        r = jnp.maximum(acc, 0.0)
        return (r * r,)

    return _mm(f"ffn{tag}_up", "nn", hn, weight(f"ffn_w1_{tag}", hn), T, D_FF, D, (BF16,), epi=act)


FFN_BWD_TILE = 256


def _ffn_dx(name, dout, w2, w1, f, h, g, tie=None):
    tm = FFN_BWD_TILE
    ties = () if tie is None else (tie,)

    def body(d_ref, w2_ref, w1_ref, f_ref, h_ref, g_ref, *rest):
        du_ref, dh_ref, dg_ref = rest[len(ties):]
        dres = d_ref[...]
        du = (_nt(dres.astype(BF16), w2_ref[...]) * (2.0 * jnp.sqrt(f_ref[...].astype(F32)))).astype(BF16)
        du_ref[...] = du
        dy = _nt(du, w1_ref[...])
        x = h_ref[...]
        r = lax.rsqrt(jnp.mean(x * x, axis=-1, keepdims=True) + EPS)
        nrm = x * r
        dn = dy * g_ref[...]
        dh_ref[...] = dres + r * (dn - nrm * jnp.mean(dn * nrm, axis=-1, keepdims=True))

        @pl.when(pl.program_id(0) == 0)
        def _():
            dg_ref[...] = jnp.zeros_like(dg_ref)

        dg_ref[...] += jnp.sum(dy * nrm, axis=0, keepdims=True)

    row = pl.BlockSpec((tm, D), lambda i: (i, 0))
    wide = pl.BlockSpec((tm, D_FF), lambda i: (i, 0))
    vec = pl.BlockSpec((1, D), lambda i: (0, 0))
    return _pcall(
        body, name=name, grid=(T // tm,),
        in_specs=[row, _resident((D_FF, D)), _resident((D, D_FF)), wide, row, vec]
        + [pl.BlockSpec(TOKEN_SHAPE, lambda i: (0, 0))] * len(ties),
        out_specs=[wide, row, vec],
        out_shape=[jax.ShapeDtypeStruct((T, D_FF), BF16), jax.ShapeDtypeStruct((T, D), F32),
                   jax.ShapeDtypeStruct((1, D), F32)],
        compiler_params=_params("arbitrary"),
    )(dout, w2, w1, f, h, g, *ties)


def _ffn_bwd(tag, h, g, weight, emit, saved, dout, tie=None):
    hn, f = saved
    du, dh, dg = _ffn_dx(f"ffn{tag}_dx", dout, weight(f"ffn_w2_{tag}", dout), weight(f"ffn_w1_{tag}", dout), f, h, g, tie)
    emit(f"ffn_w2_{tag}", f, dout)
    return dh, dg, emit(f"ffn_w1_{tag}", hn, du)


def _rope_tables():
    half = HEAD_DIM // 2
    inv = 10000.0 ** (-jnp.arange(half, dtype=F32) / half)
    ang = jnp.arange(T, dtype=F32)[:, None] * inv[None, :]
    cos, sin = jnp.cos(ang), jnp.sin(ang)
    c = jnp.tile(jnp.concatenate([cos, cos], axis=1), (1, LANES // HEAD_DIM))
    s = jnp.tile(jnp.concatenate([-sin, sin], axis=1), (1, LANES // HEAD_DIM))
    return c, s


def _local_step(x, target, p, weight, emit, emit_small, before=lambda name, after: None):
    rope_c, rope_s = _rope_tables()
    grads = {}

    hn0 = _rms_fwd("mix0_norm", x, p["norm_mix_g0"])
    zc = _mm("even_in_conv", "nn", hn0, weight("even_w_in", hn0), T, 2 * CONV_CH, D, (F32,))
    qkv = _qkv_proj(hn0, weight("even_w_in", hn0), rope_c, rope_s)
    cv, cat0 = _econv_fwd(zc, p["even_conv_k"], p["even_conv_b"], p["even_ln_g"], p["even_ln_b"])
    att_parts = [_attn_fwd(qkv, 0)]
    before("attn_fwd1", att_parts[0][0])
    att_parts += [_attn_fwd(qkv, 1), _attn_fwd(qkv, 2)]
    outs = [a[0] for a in att_parts]
    lses = [a[1] for a in att_parts]
    cat0, att, w0, w1, w2 = _attn_merge(outs, lses, cat0)
    wgts = (w0, w1, w2)
    h1, hnf0 = _mm_out_norm("even_out", cat0, weight("even_w_out", cat0), D, x, p["norm_ffn_g0"])
    f0 = _ffn_up(0, hnf0, weight)
    before("ffn0_down", f0)
    h2, hn1 = _mm_out_norm("ffn0_down", f0, weight("ffn_w2_0", f0), D_FF, h1, p["norm_mix_g1"], tm=FFN0_DOWN_TILE)

    z1 = _mm("odd_in", "nn", hn1, weight("odd_w_in", hn1), T, ODD_IN, D, (F32,))
    cat1 = _odd_mid_fwd(z1, p["odd_conv_k"], p["odd_ln_g"], p["odd_ln_b"], p["odd_sg_w"], p["odd_sg_bt"])
    h3, hnf1 = _mm_out_norm("odd_out", cat1, weight("odd_w_out", cat1), D, h2, p["norm_ffn_g1"])
    f1, dh4, grads["final_g"], loss = _ffn_last("ffn1_loss", hnf1, weight("ffn_w1_1", hnf1), weight("ffn_w2_1", hnf1),
                                                h3, p["final_g"], target)

    dh3, grads["norm_ffn_g1"], tok = _ffn_bwd(1, h3, p["norm_ffn_g1"], weight, emit, (hnf1, f1), dh4)
    tok = emit("odd_w_out", cat1, dh3, tie=tok)
    dcat1 = _mm("odd_out_dx", "nt", dh3, weight("odd_w_out", dh3), T, D, D, (F32,), tie=tok)
    dz1, grads["odd_conv_k"], grads["odd_ln_g"], grads["odd_ln_b"], grads["odd_sg_w"], grads["odd_sg_bt"] = _odd_mid_bwd(
        z1, dcat1, p["odd_conv_k"], p["odd_ln_g"], p["odd_ln_b"], p["odd_sg_w"], p["odd_sg_bt"])
    tok = emit("odd_w_in", hn1, dz1)
    dh2, grads["norm_mix_g1"] = _mm_dx_norm("odd_in_dx", dz1, weight("odd_w_in", dz1), ODD_IN, h2, p["norm_mix_g1"],
                                            dh3, tie=tok)

    dh1, grads["norm_ffn_g0"], tok = _ffn_bwd(0, h1, p["norm_ffn_g0"], weight, emit, (hnf0, f0), dh2)
    tok = emit("even_w_out", cat0, dh1, tie=tok)
    dcat0 = _mm("even_out_dx", "nt", dh1, weight("even_w_out", dh1), T, D, D, (F32,), tie=tok)
    dcv, grads["even_ln_g"], grads["even_ln_b"], grads["even_conv_b"] = _econv_bwd_ln(
        cv, dcat0, p["even_ln_g"], p["even_ln_b"])
    dz0, grads["even_conv_k"] = _econv_bwd_conv(dcv, zc, p["even_conv_k"])
    tok = emit_small(loss, grads)
    dqkv = lax.empty((3, 3, 4, T, LANES), F32)
    for g in range(3):
        dqkv = _attn_bwd(qkv, lses[g], wgts[g], att, dcat0, dqkv, g)
    before("rope_bwd", dqkv)
    dz0 = _rope_bwd(dqkv, rope_c, rope_s, dz0)
    tok = emit("even_w_in", hn0, dz0, tie=tok)
    dx, dg0 = _mm_dx_norm("even_in_dx", dz0, weight("even_w_in", dz0), EVEN_IN, x, p["norm_mix_g0"], dh1, tie=tok)
    return dx, dg0


def _rowwise(name, fn, ins, out_dtypes, tm=256, tie=None):
    rows, cols = ins[0].shape
    tm = tm if rows % tm == 0 else rows
    n_in = len(ins)
    ties = () if tie is None else (tie,)

    def body(*refs):
        vals = fn(*[r[...] for r in refs[:n_in]])
        for o_ref, v in zip(refs[n_in + len(ties):], vals):
            o_ref[...] = v.astype(o_ref.dtype)

    spec = pl.BlockSpec((tm, cols), lambda i: (i, 0))
    outs = _pcall(
        body, name=name, grid=(rows // tm,),
        in_specs=[spec] * n_in + [pl.BlockSpec(TOKEN_SHAPE, lambda i: (0, 0))] * len(ties),
        out_specs=[spec] * len(out_dtypes),
        out_shape=[jax.ShapeDtypeStruct((rows, cols), dt) for dt in out_dtypes],
        compiler_params=_params("parallel"),
    )(*ins, *ties)
    return outs[0] if len(out_dtypes) == 1 else outs


def _adamw(name, w, g, m, v, with_grad=False, tie=None):
    c1 = 1.0 - ADAM_B1 ** ADAM_STEP
    c2 = 1.0 - ADAM_B2 ** ADAM_STEP

    def fn(w_t, g_t, m_t, v_t):
        m_new = ADAM_B1 * m_t + (1.0 - ADAM_B1) * g_t
        v_new = ADAM_B2 * v_t + (1.0 - ADAM_B2) * (g_t * g_t)
        delta = -ADAM_LR * ((m_new / c1) / (jnp.sqrt(v_new / c2) + ADAM_EPS) + ADAM_WD * w_t)
        return (delta, m_new, v_new, g_t) if with_grad else (delta, m_new, v_new)

    return _rowwise(name, fn, (w, g, m, v), (F32,) * (4 if with_grad else 3), tie=tie)


class _Piece:
    def __init__(self, name, rows, cols, axis, src, src_row0):
        self.name, self.rows, self.cols, self.axis = name, rows, cols, axis
        self.width = (cols if axis == 1 else rows) // 4
        self.src, self.src_row0 = src, src_row0

    @property
    def full_shape(self):
        return (self.rows, self.cols)

    @property
    def half_shape(self):
        return (self.rows // 2, self.cols) if self.axis == 1 else (self.rows, self.cols // 2)

    @property
    def shard_half_shape(self):
        return (self.rows // 2, self.width) if self.axis == 1 else (self.width, self.cols // 2)

    def shard_whole(self, ref):
        n = self.rows if self.axis == 1 else self.width
        return ref.at[pl.ds(self.src_row0, n), :]

    def shard_half(self, ref, h):
        if self.axis == 1:
            return ref.at[pl.ds(self.src_row0 + h * (self.rows // 2), self.rows // 2), :]
        return ref.at[pl.ds(self.src_row0, self.width), pl.ds(h * (self.cols // 2), self.cols // 2)]

    def full_shard(self, ref, s):
        if self.axis == 1:
            return ref.at[:, pl.ds(s * self.width, self.width)]
        return ref.at[pl.ds(s * self.width, self.width), :]

    def full_shard_half(self, ref, s, h):
        if self.axis == 1:
            return ref.at[pl.ds(h * (self.rows // 2), self.rows // 2), pl.ds(s * self.width, self.width)]
        return ref.at[pl.ds(s * self.width, self.width), pl.ds(h * (self.cols // 2), self.cols // 2)]

    def full_half(self, ref, h):
        if self.axis == 1:
            return ref.at[pl.ds(h * (self.rows // 2), self.rows // 2), :]
        return ref.at[:, pl.ds(h * (self.cols // 2), self.cols // 2)]

    def full_half_rows(self, ref, h, r0, n):
        if self.axis == 1:
            return ref.at[pl.ds(h * (self.rows // 2) + r0, n), :]
        return ref.at[pl.ds(r0, n), pl.ds(h * (self.cols // 2), self.cols // 2)]

    def half_shard(self, ref, s):
        return self.full_shard(ref, s)


PIECES = (
    _Piece("even_w_in", D, EVEN_IN, 1, 0, 0),
    _Piece("even_w_out", D, D, 0, 1, 0),
    _Piece("ffn_w1_0", D, D_FF, 1, 4, 0),
    _Piece("ffn_w2_0", D_FF, D, 0, 5, 0),
    _Piece("odd_w_in", D, ODD_IN, 1, 2, 0),
    _Piece("odd_w_out", D, D, 0, 3, 0),
    _Piece("ffn_w1_1", D, D_FF, 1, 4, D),
    _Piece("ffn_w2_1", D_FF, D, 0, 5, D_FF // 4),
)
N_PIECES = len(PIECES)
FORWARD_RIDES = {"attn_fwd1": (1, 2, 3), "ffn0_down": (4, 5, 6, 7)}
JOIN_GROUPS = ((0, 1, 2, 3), (4, 5))
JOIN_RIDES_IN = "rope_bwd"
HOLD_BACK = ("ffn_w2_0", "ffn_w2_1", "odd_w_out")
N_SHARD_OPERANDS = 6
ANY = pl.BlockSpec(memory_space=pl.ANY)
MESH = pl.DeviceIdType.MESH


def _mesh_place():
    x, y, c = lax.axis_index("x"), lax.axis_index("y"), lax.axis_index("c")
    chips = [(1 - x, y), (x, 1 - y), (1 - x, 1 - y)]
    return x, y, c, chips


def _remote(src, dst, send_sem, recv_sem, dev):
    return pltpu.make_async_remote_copy(src_ref=src, dst_ref=dst, send_sem=send_sem, recv_sem=recv_sem,
                                        device_id=dev, device_id_type=MESH)


HBM = pl.BlockSpec(memory_space=pltpu.HBM)
SEM = pl.BlockSpec(memory_space=pltpu.SEMAPHORE)
SPLIT_PARAMS = pltpu.CompilerParams(has_side_effects=pltpu.SideEffectType.DATAFLOW_SIDE_EFFECTING)
CAST_TILE = 256


def _in_hbm(a):
    return pltpu.with_memory_space_constraint(a, pltpu.HBM)


def _cast_place(pc, shard_operand, chip, tie=None):
    rows, cols = (pc.rows, pc.width) if pc.axis == 1 else (pc.width, pc.cols)
    nblk = rows // CAST_TILE
    blk0 = pc.src_row0 // CAST_TILE
    ties = () if tie is None else (tie,)

    def body(chip_ref, x_ref, *rest):
        del chip_ref
        rest[-1][...] = x_ref[...].astype(BF16)

    if pc.axis == 1:
        out_map = lambda i, chip_ref: (i, chip_ref[0])
    else:
        out_map = lambda i, chip_ref: (chip_ref[0] * nblk + i, 0)
    return _pcall(
        body, name=f"cast_{pc.name}",
        grid_spec=pltpu.PrefetchScalarGridSpec(
            num_scalar_prefetch=1, grid=(nblk,),
            in_specs=[pl.BlockSpec((CAST_TILE, cols), lambda i, chip_ref: (blk0 + i, 0))]
            + [pl.BlockSpec(TOKEN_SHAPE, lambda i, chip_ref: (0, 0))] * len(ties),
            out_specs=pl.BlockSpec((CAST_TILE, cols), out_map)),
        out_shape=jax.ShapeDtypeStruct(pc.full_shape, BF16),
        compiler_params=_params("parallel"),
    )(chip, shard_operand, *ties)


def _gather_start(name, pieces, fulls):
    n = len(pieces)

    def body(*refs):
        ins = refs[:n]
        sends = refs[2 * n:3 * n]
        recvs = refs[3 * n:4 * n]
        token = refs[4 * n]
        x, y, c, chips = _mesh_place()
        s = 2 * x + y
        for i, pc in enumerate(pieces):
            win = pc.full_shard_half(ins[i], s, c)
            for k, (cx, cy) in enumerate(chips):
                _remote(win, win, sends[i].at[k], recvs[i].at[k], (cx, cy, c)).start()
        token[...] = jnp.zeros(TOKEN_SHAPE, F32)

    sems = [pltpu.SemaphoreType.DMA((3,))] * (2 * n)
    outs = _pcall(
        body, name=name,
        in_specs=[HBM] * n,
        out_specs=[HBM] * n + [SEM] * (2 * n) + [pl.BlockSpec(memory_space=pltpu.VMEM)],
        out_shape=[pltpu.HBM(pc.full_shape, BF16) for pc in pieces] + sems + [jax.ShapeDtypeStruct(TOKEN_SHAPE, F32)],
        input_output_aliases={i: i for i in range(n)},
        compiler_params=SPLIT_PARAMS,
    )(*[_in_hbm(f) for f in fulls])
    return outs[:n], outs[n:2 * n], outs[2 * n:3 * n], outs[3 * n]


def _gather_wait(pc, full, send_sems, recv_sems, after):
    def body(full_ref, send_ref, recv_ref, after_ref, out_ref):
        del after_ref, out_ref
        x, y, c, chips = _mesh_place()
        for k, (cx, cy) in enumerate(chips):
            win = pc.full_shard_half(full_ref, 2 * cx + cy, c)
            cp = _remote(win, win, send_ref.at[k], recv_ref.at[k], (cx, cy, c))
            cp.wait_send()
            cp.wait_recv()

    return _pcall(
        body, name=f"gather_wait_{pc.name}",
        in_specs=[HBM, SEM, SEM, ANY], out_specs=HBM, out_shape=pltpu.HBM(pc.full_shape, BF16),
        input_output_aliases={0: 0}, compiler_params=SPLIT_PARAMS,
    )(full, send_sems, recv_sems, after)


def _core_forward_job(pieces, fulls):
    n = len(pieces)

    def copies(ins, outs, scr):
        send_bufs, recv_bufs = scr[:n], scr[n:2 * n]
        load_sems, send_sems, recv_sems, store_sems = scr[2 * n:]
        x, y, c, chips = _mesh_place()
        loads, sends, stores = [], [], []
        for i, pc in enumerate(pieces):
            for k, (cx, cy) in enumerate(chips):
                j = 3 * i + k
                loads.append(pltpu.make_async_copy(pc.full_shard_half(ins[i], 2 * cx + cy, c), send_bufs[i].at[k],
                                                   load_sems.at[j]))
                sends.append(_remote(send_bufs[i].at[k], recv_bufs[i].at[k], send_sems.at[j], recv_sems.at[j],
                                     (x, y, 1 - c)))
                stores.append(pltpu.make_async_copy(recv_bufs[i].at[k], pc.full_shard_half(outs[i], 2 * cx + cy, 1 - c),
                                                    store_sems.at[j]))
        return loads, sends, stores

    def begin(ins, outs, scr):
        for cp in copies(ins, outs, scr)[0]:
            cp.start()

    def advance(ins, outs, scr):
        loads, sends, _ = copies(ins, outs, scr)
        for load, send in zip(loads, sends):
            load.wait()
            send.start()

    def finish(ins, outs, scr):
        _, sends, stores = copies(ins, outs, scr)
        for send, store in zip(sends, stores):
            send.wait_recv()
            store.start()
        for send, store in zip(sends, stores):
            send.wait_send()
            store.wait()

    sems = pltpu.SemaphoreType.DMA((3 * n,))
    bufs = [pltpu.VMEM((3,) + pc.shard_half_shape, BF16) for pc in pieces]
    return _SideJob(fulls, [jax.ShapeDtypeStruct(pc.full_shape, BF16) for pc in pieces],
                    bufs + bufs + [sems, sems, sems, sems], {i: i for i in range(n)}, begin, advance, finish)


def _run_job(name, job):
    def body(o_ref):
        o_ref[...] = jnp.zeros(TOKEN_SHAPE, F32)

    _ride_next_call(job)
    _pcall(body, name=name, grid=(3,), in_specs=[], out_specs=pl.BlockSpec(TOKEN_SHAPE, lambda i: (0, 0)),
           out_shape=jax.ShapeDtypeStruct(TOKEN_SHAPE, F32))()
    return job.results


def _core_forward(pieces, fulls):
    n = len(pieces)

    def body(*refs):
        ins, outs = refs[:n], refs[n:2 * n]
        send_bufs, recv_bufs = refs[2 * n:3 * n], refs[3 * n:4 * n]
        load_sems, send_sems, recv_sems, store_sems = refs[4 * n:]
        x, y, c, chips = _mesh_place()
        loads, sends, stores = [], [], []
        for i, pc in enumerate(pieces):
            for k, (cx, cy) in enumerate(chips):
                cp = pltpu.make_async_copy(pc.full_shard_half(ins[i], 2 * cx + cy, c), send_bufs[i].at[k],
                                           load_sems.at[3 * i + k])
                cp.start()
                loads.append(cp)
        _sibling_handshake()
        for i in range(n):
            for k in range(3):
                j = 3 * i + k
                loads[j].wait()
                cp = _remote(send_bufs[i].at[k], recv_bufs[i].at[k], send_sems.at[j], recv_sems.at[j], (x, y, 1 - c))
                cp.start()
                sends.append(cp)
        for i, pc in enumerate(pieces):
            for k, (cx, cy) in enumerate(chips):
                j = 3 * i + k
                sends[j].wait_recv()
                cp = pltpu.make_async_copy(recv_bufs[i].at[k], pc.full_shard_half(outs[i], 2 * cx + cy, 1 - c),
                                           store_sems.at[j])
                cp.start()
                stores.append(cp)
        for j in range(3 * n):
            sends[j].wait_send()
            stores[j].wait()

    sems = pltpu.SemaphoreType.DMA((3 * n,))
    bufs = [pltpu.VMEM((3,) + pc.shard_half_shape, BF16) for pc in pieces]
    return _pcall(
        body, name="core_forward_" + pieces[0].name, in_specs=[ANY] * n, out_specs=[ANY] * n,
        out_shape=[jax.ShapeDtypeStruct(pc.full_shape, BF16) for pc in pieces],
        scratch_shapes=bufs + bufs + [sems, sems, sems, sems],
        input_output_aliases={i: i for i in range(n)},
        compiler_params=pltpu.CompilerParams(vmem_limit_bytes=VMEM_LIMIT, collective_id=SIBLING_COLLECTIVE_ID),
    )(*fulls)


def _dw_tile(pc):
    if pc.axis == 1:
        tn = max(t for t in range(LANES, pc.cols + 1, LANES) if pc.cols % t == 0 and t <= 1408)
        return pc.rows // 2, tn
    return min(pc.rows, 1024), pc.cols // 2


def _mm_dw_chipsum(pc, a, b, core, tie=None):
    tm, tn = _dw_tile(pc)
    hr, hc = pc.half_shape
    tiles_r, tiles_c = hr // tm, hc // tn
    th = tiles_r * tiles_c
    ties = () if tie is None else (tie,)

    def tile_of(s, core_ref):
        mine = s >= th
        half = jnp.where(mine, core_ref[0], 1 - core_ref[0])
        local = s % th
        li, lj = local // tiles_c, local % tiles_c
        if pc.axis == 1:
            return half * tiles_r + li, lj, li, lj, mine
        return li, half * tiles_c + lj, li, lj, mine

    def body(core_ref, a_ref, b_ref, *rest):
        o_ref, send_buf, recv_buf, send_sems, recv_sems = rest[len(ties):]
        s = pl.program_id(0)
        local = s % th
        x, y, c = lax.axis_index("x"), lax.axis_index("y"), lax.axis_index("c")
        acc = _tn(a_ref[...].astype(BF16), b_ref[...].astype(BF16))

        def push(slot):
            return _remote(send_buf.at[slot], recv_buf.at[slot], send_sems.at[slot], recv_sems.at[slot], (x, y, 1 - c))

        @pl.when(s == 0)
        def _():
            _sibling_handshake()

        @pl.when(s < th)
        def _():
            send_buf[local] = acc.astype(BF16)
            push(local).start()

        @pl.when(s >= th)
        def _():
            push(local).wait_recv()
            o_ref[...] = (acc + recv_buf[local].astype(F32)).astype(BF16)

        @pl.when(s == 2 * th - 1)
        def _():
            for slot in range(th):
                push(slot).wait_send()

    def a_map(s, core_ref):
        return 0, tile_of(s, core_ref)[0]

    def b_map(s, core_ref):
        return 0, tile_of(s, core_ref)[1]

    def o_map(s, core_ref):
        _, _, li, lj, mine = tile_of(s, core_ref)
        return jnp.where(mine, li, 0), jnp.where(mine, lj, 0)

    return _pcall(
        body, name=f"dw_{pc.name}",
        grid_spec=pltpu.PrefetchScalarGridSpec(
            num_scalar_prefetch=1, grid=(2 * th,),
            in_specs=[pl.BlockSpec((T, tm), a_map), pl.BlockSpec((T, tn), b_map)]
            + [pl.BlockSpec(TOKEN_SHAPE, lambda s, core_ref: (0, 0))] * len(ties),
            out_specs=pl.BlockSpec((tm, tn), o_map),
            scratch_shapes=[pltpu.VMEM((th, tm, tn), BF16), pltpu.VMEM((th, tm, tn), BF16),
                            pltpu.SemaphoreType.DMA((th,)), pltpu.SemaphoreType.DMA((th,))]),
        out_shape=jax.ShapeDtypeStruct((hr, hc), BF16),
        compiler_params=pltpu.CompilerParams(dimension_semantics=("arbitrary",), vmem_limit_bytes=VMEM_LIMIT,
                                             collective_id=SIBLING_COLLECTIVE_ID),
    )(core, a, b, *ties)


def _scatter_start(pieces, chip_sums):
    n = len(pieces)

    def body(*refs):
        sums, lands = refs[:n], refs[n:2 * n]
        sends, recvs = refs[4 * n:5 * n], refs[5 * n:6 * n]
        token = refs[6 * n]
        x, y, c, chips = _mesh_place()
        for i, pc in enumerate(pieces):
            for k, (cx, cy) in enumerate(chips):
                _remote(pc.half_shard(sums[i], 2 * cx + cy), lands[i].at[k], sends[i].at[k], recvs[i].at[k],
                        (cx, cy, c)).start()
        token[...] = jnp.zeros(TOKEN_SHAPE, F32)

    land_shapes = [(3,) + pc.shard_half_shape for pc in pieces]
    sems = [pltpu.SemaphoreType.DMA((3,))] * (2 * n)
    outs = _pcall(
        body, name="scatter_start_" + pieces[0].name,
        in_specs=[HBM] * (2 * n), out_specs=[HBM] * (2 * n) + [SEM] * (2 * n) + [pl.BlockSpec(memory_space=pltpu.VMEM)],
        out_shape=[pltpu.HBM(pc.half_shape, BF16) for pc in pieces] + [pltpu.HBM(sh, BF16) for sh in land_shapes]
        + sems + [jax.ShapeDtypeStruct(TOKEN_SHAPE, F32)],
        input_output_aliases={i: i for i in range(2 * n)}, compiler_params=SPLIT_PARAMS,
    )(*[_in_hbm(cs) for cs in chip_sums], *[_in_hbm(lax.empty(sh, BF16)) for sh in land_shapes])
    return [(outs[i], outs[n + i], outs[2 * n + i], outs[3 * n + i]) for i in range(n)], outs[4 * n]


def _scatter_wait(pc, chip_sum, land, send_sems, recv_sems, after):
    def body(sum_ref, land_ref, send_ref, recv_ref, after_ref, sum_out, land_out):
        del after_ref, sum_out, land_out
        x, y, c, chips = _mesh_place()
        for k, (cx, cy) in enumerate(chips):
            cp = _remote(pc.half_shard(sum_ref, 2 * cx + cy), land_ref.at[k], send_ref.at[k], recv_ref.at[k], (cx, cy, c))
            cp.wait_send()
            cp.wait_recv()

    return _pcall(
        body, name=f"scatter_wait_{pc.name}",
        in_specs=[HBM, HBM, SEM, SEM, ANY], out_specs=[HBM, HBM],
        out_shape=[pltpu.HBM(pc.half_shape, BF16), pltpu.HBM((3,) + pc.shard_half_shape, BF16)],
        input_output_aliases={0: 0, 1: 1}, compiler_params=SPLIT_PARAMS,
    )(chip_sum, land, send_sems, recv_sems, after)


SHARD_OPERAND_SHAPES = ((D, EVEN_IN // 4), (D // 4, D), (D, ODD_IN // 4), (D // 4, D), (2 * D, D_FF // 4), (2 * D_FF // 4, D))


def _allsum_join_job(operands, chip_sums, lands):
    pieces = [pc for pc in PIECES if pc.src in operands]
    n = len(pieces)

    def copies(ins, outs, scr):
        sum_refs, land_refs = ins[:n], ins[n:]
        out_refs = dict(zip(operands, outs))
        in_bufs, fin_bufs, recv_bufs = scr[:n], scr[n:2 * n], scr[2 * n:3 * n]
        load_sems, send_sems, recv_sems, out_sems = scr[3 * n:]
        x, y, c, _ = _mesh_place()
        s = 2 * x + y
        loads, sends, mine, theirs = [], [], [], []
        for j, pc in enumerate(pieces):
            loads.append((pltpu.make_async_copy(land_refs[j], in_bufs[j].at[pl.ds(0, 3)], load_sems.at[2 * j]),
                          pltpu.make_async_copy(pc.half_shard(sum_refs[j], s), in_bufs[j].at[3], load_sems.at[2 * j + 1])))
            sends.append(_remote(fin_bufs[j], recv_bufs[j], send_sems.at[j], recv_sems.at[j], (x, y, 1 - c)))
            mine.append(pltpu.make_async_copy(fin_bufs[j], pc.shard_half(out_refs[pc.src], c), out_sems.at[2 * j]))
            theirs.append(pltpu.make_async_copy(recv_bufs[j], pc.shard_half(out_refs[pc.src], 1 - c), out_sems.at[2 * j + 1]))
        return loads, sends, mine, theirs, in_bufs, fin_bufs

    def begin(ins, outs, scr):
        for a, b in copies(ins, outs, scr)[0]:
            a.start()
            b.start()

    def advance(ins, outs, scr):
        loads, sends, mine, _, in_bufs, fin_bufs = copies(ins, outs, scr)
        for j in range(n):
            loads[j][0].wait()
            loads[j][1].wait()
            acc = in_bufs[j][0].astype(F32)
            for k in range(1, 4):
                acc = acc + in_bufs[j][k].astype(F32)
            fin_bufs[j][...] = acc
            mine[j].start()
            sends[j].start()

    def finish(ins, outs, scr):
        _, sends, mine, theirs, _, _ = copies(ins, outs, scr)
        for j in range(n):
            sends[j].wait_recv()
            theirs[j].start()
        for j in range(n):
            sends[j].wait_send()
            mine[j].wait()
            theirs[j].wait()

    halves = [pc.shard_half_shape for pc in pieces]
    scratch = ([pltpu.VMEM((4,) + sh, BF16) for sh in halves] + [pltpu.VMEM(sh, F32) for sh in halves] * 2
               + [pltpu.SemaphoreType.DMA((2 * n,)), pltpu.SemaphoreType.DMA((n,)), pltpu.SemaphoreType.DMA((n,)),
                  pltpu.SemaphoreType.DMA((2 * n,))])
    return _SideJob(list(chip_sums) + list(lands), [jax.ShapeDtypeStruct(SHARD_OPERAND_SHAPES[o], F32) for o in operands],
                    scratch, {}, begin, advance, finish)


PEER_FLIPS = tuple((a, b, e) for a in (0, 1) for b in (0, 1) for e in (0, 1) if (a, b, e) != (0, 0, 0))


def _peers():
    x, y, c = lax.axis_index("x"), lax.axis_index("y"), lax.axis_index("c")
    me = 4 * x + 2 * y + c
    out = []
    for a, b, e in PEER_FLIPS:
        px, py, pc = (1 - x if a else x), (1 - y if b else y), (1 - c if e else c)
        out.append(((px, py, pc), 4 * px + 2 * py + pc))
    return me, out


def _exchange8_start(name, blk):
    m = blk.shape[0]

    def body(blk_ref, land_ref, blk_out, land_out, sends, recvs, token):
        del blk_out, land_out
        me, peers = _peers()
        for k, (dev, _) in enumerate(peers):
            _remote(blk_ref, land_ref.at[me], sends.at[k], recvs.at[k], dev).start()
        token[...] = jnp.zeros(TOKEN_SHAPE, F32)

    sems = pltpu.SemaphoreType.DMA((7,))
    return _pcall(
        body, name=name,
        in_specs=[HBM, HBM], out_specs=[HBM, HBM, SEM, SEM, pl.BlockSpec(memory_space=pltpu.VMEM)],
        out_shape=[pltpu.HBM((m, LANES), F32), pltpu.HBM((8, m, LANES), F32), sems, sems,
                   jax.ShapeDtypeStruct(TOKEN_SHAPE, F32)],
        input_output_aliases={0: 0, 1: 1}, compiler_params=SPLIT_PARAMS,
    )(_in_hbm(blk), _in_hbm(lax.empty((8, m, LANES), F32)))


def _exchange8_wait(name, blk, land, send_sems, recv_sems, after):
    def body(blk_ref, land_ref, send_ref, recv_ref, after_ref, blk_out, land_out):
        del after_ref, blk_out, land_out
        _, peers = _peers()
        for k, (dev, slot) in enumerate(peers):
            cp = _remote(blk_ref, land_ref.at[slot], send_ref.at[k], recv_ref.at[k], dev)
            cp.wait_send()
            cp.wait_recv()

    m = blk.shape[0]
    return _pcall(
        body, name=name,
        in_specs=[HBM, HBM, SEM, SEM, ANY], out_specs=[HBM, HBM],
        out_shape=[pltpu.HBM((m, LANES), F32), pltpu.HBM((8, m, LANES), F32)],
        input_output_aliases={0: 0, 1: 1}, compiler_params=SPLIT_PARAMS,
    )(blk, land, send_sems, recv_sems, after)


def _collect8(name, blk, land, with_sum):
    m = blk.shape[0]

    def body(blk_ref, land_ref, out_ref, *scratch):
        sems = scratch[-1]
        dst = scratch[0] if with_sum else out_ref
        me, peers = _peers()
        copies = [pltpu.make_async_copy(blk_ref, dst.at[me], sems.at[7])]
        for k, (_, slot) in enumerate(peers):
            copies.append(pltpu.make_async_copy(land_ref.at[slot], dst.at[slot], sems.at[k]))
        for cp in copies:
            cp.start()
        for cp in copies:
            cp.wait()
        if with_sum:
            acc = dst[0]
            for dev in range(1, 8):
                acc = acc + dst[dev]
            out_ref[...] = acc

    all_shape = (8, m, LANES)
    return _pcall(
        body, name=name, in_specs=[ANY, ANY], out_specs=pl.BlockSpec(memory_space=pltpu.VMEM),
        out_shape=jax.ShapeDtypeStruct((m, LANES) if with_sum else all_shape, F32),
        scratch_shapes=([pltpu.VMEM(all_shape, F32)] if with_sum else []) + [pltpu.SemaphoreType.DMA((8,))],
    )(blk, land)


def _pack(arrays, row_counts):
    rows = []
    for a, n in zip(arrays, row_counts):
        flat = a.reshape(-1, LANES)
        rows.append(jnp.pad(flat, ((0, n - flat.shape[0]), (0, 0))))
    return jnp.concatenate(rows, axis=0)


REPL_NAMES = ("norm_mix_g", "norm_ffn_g", "even_conv_b", "even_ln_g", "even_ln_b", "odd_sg_w", "odd_sg_b", "final_g")
REPL_SHAPES = ((2, D), (2, D), (1, 512), (1, 512), (1, 512), (1, SG_GROUPS, CHUNK, CHUNK), (1, SG_GROUPS, CHUNK), (D,))
REPL_ROWS = (16, 16, 8, 8, 8, 512, 8, 8)
SHARDED_NAMES = ("even_conv_k", "odd_conv_k", "odd_ln_g", "odd_ln_b")
SHARDED_SHARD_SHAPES = ((1, CONV_W, LANES), (1, SCONV_W, LANES), (1, LANES), (1, LANES))
SHARDED_SHARD_ROWS = (32, 8, 8, 8)
SHARDED_FULL_SHAPES = ((CONV_W, 512), (SCONV_W, 512), (1, 512), (1, 512))
SHARDED_FULL_ROWS = (128, 16, 8, 8)
SMALL_NAMES = REPL_NAMES + SHARDED_NAMES
SMALL_ROWS = REPL_ROWS + SHARDED_SHARD_ROWS
SMALL_OUT_SHAPES = REPL_SHAPES[:-1] + ((1, D),) + SHARDED_SHARD_SHAPES
LOSS_ROWS = 8


def _offsets(rows):
    out, r0 = [], 0
    for n in rows:
        out.append(r0)
        r0 += n
    return out


def _adamw_small(w_pack, m_pack, v_pack, grad_sum, first_gain_sum):
    n_rows = sum(SMALL_ROWS)
    state_at = _offsets(SMALL_ROWS)
    grad_at = _offsets((LOSS_ROWS, 8) + REPL_ROWS[1:] + SHARDED_FULL_ROWS)[1:]
    c1 = 1.0 - ADAM_B1 ** ADAM_STEP
    c2 = 1.0 - ADAM_B2 ** ADAM_STEP
    n_repl = len(REPL_NAMES)

    def body(w_ref, m_ref, v_ref, g_ref, g0_ref, *rest):
        outs, gbuf = rest[:-1], rest[-1]
        chip = 2 * lax.axis_index("x") + lax.axis_index("y")
        gbuf[...] = jnp.zeros((n_rows, LANES), F32)
        gbuf[0:8, :] = g0_ref[...]
        gbuf[8:16, :] = g_ref[grad_at[0]:grad_at[0] + 8, :]
        for i in range(1, n_repl):
            gbuf[state_at[i]:state_at[i] + REPL_ROWS[i], :] = g_ref[grad_at[i]:grad_at[i] + REPL_ROWS[i], :]
        for k, shape in enumerate(SHARDED_SHARD_SHAPES):
            used = shape[-2] if len(shape) == 3 else 1
            src = pl.ds(grad_at[n_repl + k] + chip, used, stride=4) if used > 1 else pl.ds(grad_at[n_repl + k] + chip, 1)
            gbuf[state_at[n_repl + k]:state_at[n_repl + k] + used, :] = g_ref[src, :]
        g = gbuf[...]
        m_new = ADAM_B1 * m_ref[...] + (1.0 - ADAM_B1) * g
        v_new = ADAM_B2 * v_ref[...] + (1.0 - ADAM_B2) * (g * g)
        delta = -ADAM_LR * ((m_new / c1) / (jnp.sqrt(v_new / c2) + ADAM_EPS) + ADAM_WD * w_ref[...])
        for i, shape in enumerate(SMALL_OUT_SHAPES):
            for j, val in enumerate((g, delta, m_new, v_new)):
                o_ref = outs[4 * i + j]
                rows = val[state_at[i]:state_at[i] + SMALL_ROWS[i], :]
                if len(shape) == 2 and shape[1] > LANES:
                    per = shape[1] // LANES
                    for r in range(shape[0]):
                        for q in range(per):
                            o_ref[r:r + 1, q * LANES:(q + 1) * LANES] = rows[r * per + q:r * per + q + 1, :]
                elif len(shape) == 4:
                    for grp in range(shape[1]):
                        o_ref[0, grp] = rows[grp * shape[2]:(grp + 1) * shape[2], :]
                elif len(shape) == 3:
                    o_ref[0] = rows[:shape[1], :]
                else:
                    o_ref[...] = rows[:1, :]

    vm = pl.BlockSpec(memory_space=pltpu.VMEM)
    out_shape = [jax.ShapeDtypeStruct(sh, F32) for sh in SMALL_OUT_SHAPES for _ in range(4)]
    outs = _pcall(body, name="adamw_small", in_specs=[vm] * 5, out_specs=[vm] * len(out_shape), out_shape=out_shape,
                  scratch_shapes=[pltpu.VMEM((n_rows, LANES), F32)])(w_pack, m_pack, v_pack, grad_sum, first_gain_sum)
    return {n: outs[4 * i:4 * i + 4] for i, n in enumerate(SMALL_NAMES)}


def _touch(arrays):
    n = len(arrays)

    def body(*refs):
        refs[-1][...] = jnp.zeros(TOKEN_SHAPE, F32)

    outs = _pcall(
        body, name="touch", in_specs=[ANY] * n, out_specs=[ANY] * n + [pl.BlockSpec(memory_space=pltpu.VMEM)],
        out_shape=[jax.ShapeDtypeStruct(a.shape, a.dtype) for a in arrays] + [jax.ShapeDtypeStruct(TOKEN_SHAPE, F32)],
        input_output_aliases={i: i for i in range(n)})(*arrays)
    return outs[:n], outs[n]


def kernel(x, norm_mix_g, norm_ffn_g, even_w_in, even_conv_k, even_conv_b, even_ln_g, even_ln_b, even_w_out, odd_w_in, odd_conv_k, odd_ln_g, odd_ln_b, odd_sg_w, odd_sg_b, odd_w_out, ffn_w1, ffn_w2, final_g, loss_target, m_norm_mix_g, m_norm_ffn_g, m_even_w_in, m_even_conv_k, m_even_conv_b, m_even_ln_g, m_even_ln_b, m_even_w_out, m_odd_w_in, m_odd_conv_k, m_odd_ln_g, m_odd_ln_b, m_odd_sg_w, m_odd_sg_b, m_odd_w_out, m_ffn_w1, m_ffn_w2, m_final_g, v_norm_mix_g, v_norm_ffn_g, v_even_w_in, v_even_conv_k, v_even_conv_b, v_even_ln_g, v_even_ln_b, v_even_w_out, v_odd_w_in, v_odd_conv_k, v_odd_ln_g, v_odd_ln_b, v_odd_sg_w, v_odd_sg_b, v_odd_w_out, v_ffn_w1, v_ffn_w2, v_final_g):
    names = ("norm_mix_g", "norm_ffn_g", "even_w_in", "even_conv_k", "even_conv_b", "even_ln_g", "even_ln_b", "even_w_out",
             "odd_w_in", "odd_conv_k", "odd_ln_g", "odd_ln_b", "odd_sg_w", "odd_sg_b", "odd_w_out", "ffn_w1", "ffn_w2", "final_g")
    w = dict(zip(names, (norm_mix_g, norm_ffn_g, even_w_in, even_conv_k, even_conv_b, even_ln_g, even_ln_b, even_w_out,
                         odd_w_in, odd_conv_k, odd_ln_g, odd_ln_b, odd_sg_w, odd_sg_b, odd_w_out, ffn_w1, ffn_w2, final_g)))
    mom = dict(zip(names, (m_norm_mix_g, m_norm_ffn_g, m_even_w_in, m_even_conv_k, m_even_conv_b, m_even_ln_g, m_even_ln_b,
                           m_even_w_out, m_odd_w_in, m_odd_conv_k, m_odd_ln_g, m_odd_ln_b, m_odd_sg_w, m_odd_sg_b, m_odd_w_out,
                           m_ffn_w1, m_ffn_w2, m_final_g)))
    vel = dict(zip(names, (v_norm_mix_g, v_norm_ffn_g, v_even_w_in, v_even_conv_k, v_even_conv_b, v_even_ln_g, v_even_ln_b,
                           v_even_w_out, v_odd_w_in, v_odd_conv_k, v_odd_ln_g, v_odd_ln_b, v_odd_sg_w, v_odd_sg_b, v_odd_w_out,
                           v_ffn_w1, v_ffn_w2, v_final_g)))
    big_names = ("even_w_in", "even_w_out", "odd_w_in", "odd_w_out", "ffn_w1", "ffn_w2")
    chip = 2 * lax.axis_index("x") + lax.axis_index("y")

    def shard2d(t, name):
        return t[name].reshape(SHARD_OPERAND_SHAPES[big_names.index(name)])

    chip_op = jnp.reshape(chip, (1,)).astype(jnp.int32)
    small_pack = _pack([w[n] for n in SHARDED_NAMES], SHARDED_SHARD_ROWS)
    small_blk, small_land, small_send, small_recv, small_token = _exchange8_start("gather_small_start", small_pack)
    first = _cast_place(PIECES[0], shard2d(w, big_names[PIECES[0].src]), chip_op, tie=small_token)
    fly0, send0, recv0, token = _gather_start("gather_start_first", PIECES[:1], [first])
    placed = [_cast_place(pc, shard2d(w, big_names[pc.src]), chip_op, tie=token) for pc in PIECES[1:]]
    fly1, send1, recv1, all_started = _gather_start("gather_start_rest", PIECES[1:], placed)
    flying, gather_send, gather_recv = fly0 + fly1, send0 + send1, recv0 + recv1
    ready = {}

    names_in_order = [pc.name for pc in PIECES]

    riding = {}

    (*state_packs, _), idle_work_done = _touch(
        [_pack([t[n] for n in SMALL_NAMES], SMALL_ROWS) for t in (w, mom, vel)] + [all_started])

    def weight(name, after):
        if name in riding:
            job, k = riding.pop(name)
            ready[name] = job.results[k]
        if name not in ready:
            landed = _gather_wait(PIECES[0], flying[0], gather_send[0], gather_recv[0], idle_work_done)
            ready[name], = _core_forward(PIECES[:1], [landed])
        return ready[name]

    def before(call, after):
        if call in FORWARD_RIDES:
            group = FORWARD_RIDES[call]
            landed = [_gather_wait(PIECES[j], flying[j], gather_send[j], gather_recv[j], after) for j in group]
            job = _core_forward_job([PIECES[j] for j in group], landed)
            riding.update((PIECES[j].name, (job, k)) for k, j in enumerate(group))
            _ride_next_call(job)
        elif call == JOIN_RIDES_IN:
            early_join.append(join_job(JOIN_GROUPS[1], after))
            _ride_next_call(early_join[0])

    early_join = []

    def join_job(operands, after):
        pieces = [pc for pc in PIECES if pc.src in operands]
        done = {}
        for entry in list(scattering):
            if entry[0] in pieces:
                done[entry[0].name] = _scatter_wait(*entry, after)
                scattering.remove(entry)
        return _allsum_join_job(operands, [done[pc.name][0] for pc in pieces], [done[pc.name][1] for pc in pieces])

    scattering = []
    held = []

    core_op = jnp.reshape(lax.axis_index("c"), (1,)).astype(jnp.int32)

    def emit(name, a, b, tie=None):
        pc = PIECES[names_in_order.index(name)]
        held.append((pc, _mm_dw_chipsum(pc, a, b, core_op, tie)))
        if name in HOLD_BACK:
            return None
        pieces = [pc for pc, _ in held]
        started, token = _scatter_start(pieces, [chip_sum for _, chip_sum in held])
        scattering.extend((pc,) + tuple(st) for pc, st in zip(pieces, started))
        held.clear()
        return token

    full = {}
    small_blk, small_land = _exchange8_wait("gather_small_wait", small_blk, small_land, small_send, small_recv, all_started)
    gathered = _collect8("gather_small_collect", small_blk, small_land, False)
    gathered = gathered.reshape(4, 2, sum(SHARDED_SHARD_ROWS), LANES)[:, 0]
    r0 = 0
    for n, sh, rows, full_sh in zip(SHARDED_NAMES, SHARDED_SHARD_SHAPES, SHARDED_SHARD_ROWS, SHARDED_FULL_SHAPES):
        per_chip = gathered[:, r0:r0 + rows].reshape(4, -1)[:, :full_sh[0] * LANES].reshape(4, full_sh[0], LANES)
        full[n] = jnp.transpose(per_chip, (1, 0, 2)).reshape(full_sh)
        r0 += rows
    p = dict(full)
    p.update(norm_mix_g0=norm_mix_g[0:1], norm_mix_g1=norm_mix_g[1:2], norm_ffn_g0=norm_ffn_g[0:1], norm_ffn_g1=norm_ffn_g[1:2],
             even_conv_b=even_conv_b, even_ln_g=even_ln_g, even_ln_b=even_ln_b,
             odd_sg_w=odd_sg_w[0], odd_sg_bt=odd_sg_b[0].T, final_g=final_g[None, :])

    small = {}

    def emit_small(loss_row, g):
        parts = [loss_row, g["norm_mix_g1"], g["norm_ffn_g0"], g["norm_ffn_g1"], g["even_conv_b"], g["even_ln_g"],
                 g["even_ln_b"], g["odd_sg_w"], g["odd_sg_bt"].T, g["final_g"],
                 g["even_conv_k"], g["odd_conv_k"], g["odd_ln_g"], g["odd_ln_b"]]
        pack = _pack(parts, (8, 8, 8, 8) + REPL_ROWS[2:] + SHARDED_FULL_ROWS)
        small["blk"], small["land"], small["send"], small["recv"], token = _exchange8_start("allreduce_small_start", pack)
        return token

    dx, dg0 = _local_step(x[0], loss_target[0], p, weight, emit, emit_small, before)
    last_blk, last_land, last_send, last_recv, grad_token = _exchange8_start("allreduce_last_start", _pack([dg0], (8,)))

    big_grads = dict(zip((big_names[o] for o in JOIN_GROUPS[1]), early_join[0].results))
    delta, new_m, new_v, grads_big = {}, {}, {}, {}

    def adamw_big(n):
        d2, m2, v2, g2 = _adamw(f"adamw_{n}", shard2d(w, n), big_grads[n], shard2d(mom, n), shard2d(vel, n), True,
                                tie=grad_token)
        delta[n], new_m[n], new_v[n], grads_big[n] = (t.reshape(w[n].shape) for t in (d2, m2, v2, g2))

    for o in JOIN_GROUPS[1]:
        adamw_big(big_names[o])
    late_join = join_job(JOIN_GROUPS[0], new_v[big_names[JOIN_GROUPS[1][-1]]])
    big_grads.update(zip((big_names[o] for o in JOIN_GROUPS[0]), _run_job("allsum_join_late", late_join)))
    for o in JOIN_GROUPS[0]:
        adamw_big(big_names[o])

    joined_last = big_grads[big_names[JOIN_GROUPS[0][-1]]]
    grad_blk, grad_land = _exchange8_wait("allreduce_small_wait", small["blk"], small["land"], small["send"],
                                          small["recv"], joined_last)
    grad_sum = _collect8("allreduce_small_sum", grad_blk, grad_land, True)
    last_blk, last_land = _exchange8_wait("allreduce_last_wait", last_blk, last_land, last_send, last_recv, joined_last)
    dg0_sum = _collect8("allreduce_last_sum", last_blk, last_land, True)
    loss = grad_sum[0, 0]

    grads = dict(grads_big)
    for n, results in _adamw_small(*state_packs, grad_sum, dg0_sum).items():
        grads[n], delta[n], new_m[n], new_v[n] = (t.reshape(w[n].shape) for t in results)

    out = [loss, dx[None]]
    for res in (grads, delta, new_m, new_v):
        out.extend(res[n] for n in names)
    return tuple(out)
```

```python
import functools

import jax
import jax.numpy as jnp
from jax import lax
from jax.experimental import pallas as pl
from jax.experimental.pallas import tpu as pltpu

F32 = jnp.float32
BF16 = jnp.bfloat16

T = 2048
D = 1024
CONV_CH = 512
CONV_W = 31
HEAD_DIM = 64
ATT_W = 1536
EVEN_IN = 5632
ODD_IN = 2560
SCONV_W = 3
SG_GROUPS = 4
CHUNK = 128
D_FF = 4096
EPS = 1e-6
DILATIONS = (1, 4, 16)
BAND = 128
SCALE = HEAD_DIM ** -0.5
NEG = -1e30

ADAM_LR = 0.001
ADAM_B1 = 0.9
ADAM_B2 = 0.999
ADAM_EPS = 1e-08
ADAM_WD = 0.01
ADAM_STEP = 10

V7X_VMEM_BYTES = 64 * 2 ** 20
VMEM_LIMIT = V7X_VMEM_BYTES - 8 * 2 ** 20
LANES = 128
TOKEN_SHAPE = (8, LANES)


SIBLING_COLLECTIVE_ID = 0


def _sibling_handshake():
    x, y, c = lax.axis_index("x"), lax.axis_index("y"), lax.axis_index("c")
    barrier = pltpu.get_barrier_semaphore()
    pl.semaphore_signal(barrier, inc=1, device_id=(x, y, 1 - c), device_id_type=pl.DeviceIdType.MESH)
    pl.semaphore_wait(barrier, 1)


class _SideJob:
    def __init__(self, inputs, out_shape, scratch_shapes, aliases, begin, advance, finish):
        self.inputs, self.out_shape, self.scratch_shapes = list(inputs), list(out_shape), list(scratch_shapes)
        self.aliases, self.begin, self.advance, self.finish = dict(aliases), begin, advance, finish
        self.results = None


_PENDING_JOBS = []


def _ride_next_call(job):
    _PENDING_JOBS.append(job)


def _pcall(body, **kw):
    if not _PENDING_JOBS or "grid" not in kw:
        return pl.pallas_call(body, **kw)
    job = _PENDING_JOBS.pop()
    as_list = lambda v: list(v) if isinstance(v, (list, tuple)) else [v]
    single_out = not isinstance(kw["out_shape"], (list, tuple))
    in_specs, out_specs, out_shape = as_list(kw["in_specs"]), as_list(kw["out_specs"]), as_list(kw["out_shape"])
    scratch = list(kw.get("scratch_shapes", ()))
    grid = kw["grid"]
    n_steps = 1
    for extent in grid:
        n_steps *= extent
    assert n_steps >= 3
    n_in, n_out, n_scr = len(in_specs), len(out_specs), len(scratch)
    j_in, j_out = len(job.inputs), len(job.out_shape)
    any_spec = pl.BlockSpec(memory_space=pl.ANY)

    def hosted(*refs):
        ins, j_ins = refs[:n_in], refs[n_in:n_in + j_in]
        outs = refs[n_in + j_in:n_in + j_in + n_out]
        j_outs = refs[n_in + j_in + n_out:n_in + j_in + n_out + j_out]
        scr = refs[n_in + j_in + n_out + j_out:n_in + j_in + n_out + j_out + n_scr]
        j_scr = refs[n_in + j_in + n_out + j_out + n_scr:]
        step = pl.program_id(0)
        for axis in range(1, len(grid)):
            step = step * grid[axis] + pl.program_id(axis)

        @pl.when(step == 0)
        def _():
            _sibling_handshake()
            job.begin(j_ins, j_outs, j_scr)

        @pl.when(step == 1)
        def _():
            job.advance(j_ins, j_outs, j_scr)

        body(*ins, *outs, *scr)

        @pl.when(step == n_steps - 1)
        def _():
            job.finish(j_ins, j_outs, j_scr)

    aliases = dict(kw.get("input_output_aliases", {}))
    aliases.update({n_in + a: n_out + b for a, b in job.aliases.items()})
    call = pl.pallas_call(
        hosted, name=kw["name"], grid=grid,
        in_specs=in_specs + [any_spec] * j_in, out_specs=out_specs + [any_spec] * j_out,
        out_shape=out_shape + job.out_shape, scratch_shapes=scratch + job.scratch_shapes,
        input_output_aliases=aliases,
        compiler_params=pltpu.CompilerParams(dimension_semantics=("arbitrary",) * len(grid), vmem_limit_bytes=VMEM_LIMIT,
                                             collective_id=SIBLING_COLLECTIVE_ID))

    def run(*args):
        res = call(*args, *job.inputs)
        job.results = list(res[n_out:])
        return res[0] if single_out else list(res[:n_out])

    return run


def _params(*sem):
    return pltpu.CompilerParams(dimension_semantics=sem, vmem_limit_bytes=VMEM_LIMIT)


def _dot(a, b, dims):
    return lax.dot_general(a, b, (dims, ((), ())), preferred_element_type=F32)


def _nn(a, b):
    return _dot(a, b, ((1,), (0,)))


def _nt(a, b):
    return _dot(a, b, ((1,), (1,)))


def _tn(a, b):
    return _dot(a, b, ((0,), (0,)))


def _sigmoid(x):
    return 1.0 / (1.0 + jnp.exp(-x))


MM_VMEM_BUDGET = 40 * 2 ** 20


def _mm_tiles(mode, m, n, k, a_bytes, b_bytes, extra_bytes, out_bytes):
    def divisors(total, unit):
        return [t for t in range(unit, total + 1, unit) if total % t == 0]

    best = None
    for tm in divisors(m, LANES if mode == "tn" else 8):
        for tn in divisors(n, LANES):
            blocks = tm * k * a_bytes + tn * k * b_bytes + tm * tn * (extra_bytes + out_bytes)
            casts = (tm * k * 2 if a_bytes == 4 else 0) + (tn * k * 2 if b_bytes == 4 else 0)
            if 2 * blocks + casts + tm * tn * 4 > MM_VMEM_BUDGET:
                continue
            key = ((m // tm) * (n // tn), (m // tm) * n * k * b_bytes, abs(tm - tn))
            if best is None or key < best[0]:
                best = (key, tm, tn)
    return best[1], best[2]


def _mm(name, mode, a, b, m, n, k, out_dtypes, *, b_off=0, extras=(), epi=None, tie=None):
    tm, tn = _mm_tiles(mode, m, n, k, a.dtype.itemsize, b.dtype.itemsize, sum(e.dtype.itemsize for e in extras),
                       sum(jnp.dtype(dt).itemsize for dt in out_dtypes))
    assert b_off % tn == 0
    b_off //= tn
    if mode == "nn":
        a_spec = pl.BlockSpec((tm, k), lambda i, j: (i, 0))
        b_spec = pl.BlockSpec((k, tn), lambda i, j: (0, j + b_off))
        dims = ((1,), (0,))
    elif mode == "nt":
        a_spec = pl.BlockSpec((tm, k), lambda i, j: (i, 0))
        b_spec = pl.BlockSpec((tn, k), lambda i, j: (j, 0))
        dims = ((1,), (1,))
    else:
        a_spec = pl.BlockSpec((k, tm), lambda i, j: (0, i))
        b_spec = pl.BlockSpec((k, tn), lambda i, j: (0, j))
        dims = ((0,), (0,))
    o_spec = pl.BlockSpec((tm, tn), lambda i, j: (i, j))
    n_extra = len(extras)
    ties = () if tie is None else (tie,)

    def body(a_ref, b_ref, *rest):
        rest = rest[len(ties):]
        acc = _dot(a_ref[...].astype(BF16), b_ref[...].astype(BF16), dims)
        vals = epi(acc, *[e[...] for e in rest[:n_extra]]) if epi is not None else (acc,)
        for o_ref, v in zip(rest[n_extra:], vals):
            o_ref[...] = v.astype(o_ref.dtype)

    outs = _pcall(
        body, name=name, grid=(m // tm, n // tn),
        in_specs=[a_spec, b_spec] + [pl.BlockSpec(TOKEN_SHAPE, lambda i, j: (0, 0))] * len(ties) + [o_spec] * n_extra,
        out_specs=[o_spec] * len(out_dtypes),
        out_shape=[jax.ShapeDtypeStruct((m, n), dt) for dt in out_dtypes],
        compiler_params=_params("parallel", "parallel"),
    )(a, b, *ties, *extras)
    return outs[0] if len(out_dtypes) == 1 else outs


def _row_tile(k, a_bytes, n_row_blocks):
    for tm in (1024, 512, 256, 128):
        if 2 * (tm * k * a_bytes + n_row_blocks * tm * D * 4) + D * k * 2 + tm * D * 4 <= MM_VMEM_BUDGET + 4 * 2 ** 20:
            return tm
    raise ValueError("no row tile fits")


def _resident(shape):
    return pl.BlockSpec(shape, lambda i: (0, 0), pipeline_mode=pl.Buffered(1))


FFN0_DOWN_TILE = 512


def _mm_out_norm(name, a, b, k, res, g_next, tm=None):
    tm = tm or _row_tile(k, a.dtype.itemsize, 3)

    def body(a_ref, b_ref, r_ref, g_ref, h_ref, hn_ref):
        h = _nn(a_ref[...].astype(BF16), b_ref[...]) + r_ref[...]
        h_ref[...] = h
        r = lax.rsqrt(jnp.mean(h * h, axis=-1, keepdims=True) + EPS)
        hn_ref[...] = ((h * r) * g_ref[...]).astype(BF16)

    row = pl.BlockSpec((tm, D), lambda i: (i, 0))
    return _pcall(
        body, name=name, grid=(T // tm,),
        in_specs=[pl.BlockSpec((tm, k), lambda i: (i, 0)), _resident((k, D)), row,
                  pl.BlockSpec((1, D), lambda i: (0, 0))],
        out_specs=[row, row],
        out_shape=[jax.ShapeDtypeStruct((T, D), F32), jax.ShapeDtypeStruct((T, D), BF16)],
        compiler_params=_params("parallel"),
    )(a, b, res, g_next)


def _ffn_last(name, hn, w1, w2, res, g, target):
    tm = FFN_BWD_TILE

    def body(a_ref, w1_ref, w2_ref, r_ref, g_ref, t_ref, f_ref, dh_ref, dg_ref, loss_ref):
        u = jnp.maximum(_nn(a_ref[...], w1_ref[...]), 0.0)
        f = (u * u).astype(BF16)
        f_ref[...] = f
        x = _nn(f, w2_ref[...]) + r_ref[...]
        r = lax.rsqrt(jnp.mean(x * x, axis=-1, keepdims=True) + EPS)
        nrm = x * r
        gain = g_ref[...]
        err = nrm * gain - t_ref[...]
        dy = err * (1.0 / D)
        dn = dy * gain
        dh_ref[...] = r * (dn - nrm * jnp.mean(dn * nrm, axis=-1, keepdims=True))

        @pl.when(pl.program_id(0) == 0)
        def _():
            dg_ref[...] = jnp.zeros_like(dg_ref)
            loss_ref[...] = jnp.zeros_like(loss_ref)

        dg_ref[...] += jnp.sum(dy * nrm, axis=0, keepdims=True)
        part = jnp.sum(jnp.sum(err * err, axis=1, keepdims=True), axis=0, keepdims=True) * (0.5 / D)
        loss_ref[...] += jnp.broadcast_to(part, (1, LANES))

    row = pl.BlockSpec((tm, D), lambda i: (i, 0))
    wide = pl.BlockSpec((tm, D_FF), lambda i: (i, 0))
    vec = pl.BlockSpec((1, D), lambda i: (0, 0))
    return _pcall(
        body, name=name, grid=(T // tm,),
        in_specs=[row, _resident((D, D_FF)), _resident((D_FF, D)), row, vec, row],
        out_specs=[wide, row, vec, pl.BlockSpec((1, LANES), lambda i: (0, 0))],
        out_shape=[jax.ShapeDtypeStruct((T, D_FF), BF16), jax.ShapeDtypeStruct((T, D), F32),
                   jax.ShapeDtypeStruct((1, D), F32), jax.ShapeDtypeStruct((1, LANES), F32)],
        compiler_params=_params("arbitrary"),
    )(hn, w1, w2, res, g, target)


def _mm_dx_norm(name, dz, w, k, h, g, dres, tie=None):
    tm = _row_tile(k, dz.dtype.itemsize, 3)
    ties = () if tie is None else (tie,)

    def body(a_ref, b_ref, *rest):
        h_ref, g_ref, r_ref, dh_ref, dg_ref = rest[len(ties):]
        dy = _nt(a_ref[...].astype(BF16), b_ref[...])
        x = h_ref[...]
        r = lax.rsqrt(jnp.mean(x * x, axis=-1, keepdims=True) + EPS)
        nrm = x * r
        dn = dy * g_ref[...]
        dh_ref[...] = r_ref[...] + r * (dn - nrm * jnp.mean(dn * nrm, axis=-1, keepdims=True))

        @pl.when(pl.program_id(0) == 0)
        def _():
            dg_ref[...] = jnp.zeros_like(dg_ref)

        dg_ref[...] += jnp.sum(dy * nrm, axis=0, keepdims=True)

    row = pl.BlockSpec((tm, D), lambda i: (i, 0))
    vec = pl.BlockSpec((1, D), lambda i: (0, 0))
    return _pcall(
        body, name=name, grid=(T // tm,),
        in_specs=[pl.BlockSpec((tm, k), lambda i: (i, 0)), _resident((D, k))]
        + [pl.BlockSpec(TOKEN_SHAPE, lambda i: (0, 0))] * len(ties) + [row, vec, row],
        out_specs=[row, vec],
        out_shape=[jax.ShapeDtypeStruct((T, D), F32), jax.ShapeDtypeStruct((1, D), F32)],
        compiler_params=_params("arbitrary"),
    )(dz, w, *ties, h, g, dres)


def _rms_fwd(name, h, g, tm=512):
    def body(h_ref, g_ref, o_ref):
        x = h_ref[...]
        r = lax.rsqrt(jnp.mean(x * x, axis=-1, keepdims=True) + EPS)
        o_ref[...] = ((x * r) * g_ref[...]).astype(BF16)

    return _pcall(
        body, name=name, grid=(T // tm,),
        in_specs=[pl.BlockSpec((tm, D), lambda i: (i, 0)), pl.BlockSpec((1, D), lambda i: (0, 0))],
        out_specs=pl.BlockSpec((tm, D), lambda i: (i, 0)),
        out_shape=jax.ShapeDtypeStruct((T, D), BF16),
        compiler_params=_params("parallel"),
    )(h, g)


CONV_TILE = 256
CONV_HALO = 32


def _glu(z):
    return z[:, :CONV_CH] * _sigmoid(z[:, CONV_CH:])


SUBLANES = 8


def _sublane_shifts(win):
    n = win.shape[0]
    return [win] + [win[r:r + n - SUBLANES, :] for r in range(1, SUBLANES)]


def _rows_from(shifts, off, n):
    q, r = divmod(off, SUBLANES)
    return shifts[r][q * SUBLANES:q * SUBLANES + n, :]


def _econv_fwd(zc, conv_k, conv_b, ln_g, ln_b):
    R, H = CONV_TILE, CONV_HALO

    def body(z_ref, zh_ref, k_ref, b_ref, g_ref, be_ref, cv_ref, cat_ref):
        i = pl.program_id(0)
        glu = _glu(z_ref[...])
        halo = _glu(zh_ref[...]) * (i > 0).astype(F32)
        win = _sublane_shifts(jnp.concatenate([halo, glu], axis=0))
        acc = jnp.zeros((R, CONV_CH), F32) + b_ref[...]
        for j in range(CONV_W):
            acc = acc + k_ref[j:j + 1, :] * _rows_from(win, H - (CONV_W - 1) + j, R)
        cv_ref[...] = acc
        mu = jnp.mean(acc, axis=-1, keepdims=True)
        xc = acc - mu
        rstd = lax.rsqrt(jnp.mean(xc * xc, axis=-1, keepdims=True) + EPS)
        ln = xc * rstd * g_ref[...] + be_ref[...]
        cat_ref[...] = (ln * _sigmoid(ln)).astype(BF16)

    vec = pl.BlockSpec((1, CONV_CH), lambda i: (0, 0))
    return _pcall(
        body, name="econv_fwd", grid=(T // R,),
        in_specs=[pl.BlockSpec((R, 2 * CONV_CH), lambda i: (i, 0)),
                  pl.BlockSpec((H, 2 * CONV_CH), lambda i: (jnp.maximum(i * (R // H) - 1, 0), 0)),
                  pl.BlockSpec((CONV_W, CONV_CH), lambda i: (0, 0)), vec, vec, vec],
        out_specs=[pl.BlockSpec((R, CONV_CH), lambda i: (i, 0)), pl.BlockSpec((R, CONV_CH), lambda i: (i, 0))],
        out_shape=[jax.ShapeDtypeStruct((T, CONV_CH), F32), jax.ShapeDtypeStruct((T, D), BF16)],
        compiler_params=_params("parallel"),
    )(zc, zc, conv_k, conv_b, ln_g, ln_b)


def _econv_bwd_ln(cv, dcat, ln_g, ln_b):
    R = CONV_TILE

    def body(cv_ref, d_ref, g_ref, be_ref, dcv_ref, dg_ref, dbe_ref, dcb_ref):
        cv_t = cv_ref[...]
        mu = jnp.mean(cv_t, axis=-1, keepdims=True)
        xc = cv_t - mu
        rstd = lax.rsqrt(jnp.mean(xc * xc, axis=-1, keepdims=True) + EPS)
        xh = xc * rstd
        ln = xh * g_ref[...] + be_ref[...]
        sg = _sigmoid(ln)
        dln = d_ref[...] * (sg * (1.0 + ln * (1.0 - sg)))
        dxh = dln * g_ref[...]
        dcv = rstd * (dxh - jnp.mean(dxh, axis=-1, keepdims=True) - xh * jnp.mean(dxh * xh, axis=-1, keepdims=True))
        dcv_ref[...] = dcv

        @pl.when(pl.program_id(0) == 0)
        def _():
            dg_ref[...] = jnp.zeros_like(dg_ref)
            dbe_ref[...] = jnp.zeros_like(dbe_ref)
            dcb_ref[...] = jnp.zeros_like(dcb_ref)

        dg_ref[...] += jnp.sum(dln * xh, axis=0, keepdims=True)
        dbe_ref[...] += jnp.sum(dln, axis=0, keepdims=True)
        dcb_ref[...] += jnp.sum(dcv, axis=0, keepdims=True)

    vec = pl.BlockSpec((1, CONV_CH), lambda i: (0, 0))
    row = pl.BlockSpec((R, CONV_CH), lambda i: (i, 0))
    vshape = jax.ShapeDtypeStruct((1, CONV_CH), F32)
    return _pcall(
        body, name="econv_bwd_ln", grid=(T // R,),
        in_specs=[row, row, vec, vec], out_specs=[row, vec, vec, vec],
        out_shape=[jax.ShapeDtypeStruct((T, CONV_CH), F32), vshape, vshape, vshape],
        compiler_params=_params("arbitrary"),
    )(cv, dcat, ln_g, ln_b)


def _econv_bwd_conv(dcv, zc, conv_k):
    R, H = CONV_TILE, CONV_HALO
    last = T // R - 1

    def body(d_ref, dn_ref, z_ref, zh_ref, k_ref, dz_ref, dk_ref):
        i = pl.program_id(0)
        z = z_ref[...]
        a_lin = z[:, :CONV_CH]
        sg = _sigmoid(z[:, CONV_CH:])
        glu = a_lin * sg
        halo = _glu(zh_ref[...]) * (i > 0).astype(F32)
        win = _sublane_shifts(jnp.concatenate([halo, glu], axis=0))
        dcv_t = d_ref[...]
        nxt = dn_ref[...] * (i < last).astype(F32)
        winb = _sublane_shifts(jnp.concatenate([dcv_t, nxt], axis=0))

        @pl.when(i == 0)
        def _():
            dk_ref[...] = jnp.zeros_like(dk_ref)

        dglu = jnp.zeros((R, CONV_CH), F32)
        for j in range(CONV_W):
            dk_ref[j:j + 1, :] += jnp.sum(dcv_t * _rows_from(win, H - (CONV_W - 1) + j, R), axis=0, keepdims=True)
            dglu = dglu + k_ref[j:j + 1, :] * _rows_from(winb, CONV_W - 1 - j, R)
        dz_ref[...] = jnp.concatenate([dglu * sg, dglu * a_lin * sg * (1.0 - sg)], axis=1).astype(BF16)

    return _pcall(
        body, name="econv_bwd_conv", grid=(T // R,),
        in_specs=[pl.BlockSpec((R, CONV_CH), lambda i: (i, 0)),
                  pl.BlockSpec((H, CONV_CH), lambda i: (jnp.minimum((i + 1) * (R // H), T // H - 1), 0)),
                  pl.BlockSpec((R, 2 * CONV_CH), lambda i: (i, 0)),
                  pl.BlockSpec((H, 2 * CONV_CH), lambda i: (jnp.maximum(i * (R // H) - 1, 0), 0)),
                  pl.BlockSpec((CONV_W, CONV_CH), lambda i: (0, 0))],
        out_specs=[pl.BlockSpec((R, 2 * CONV_CH), lambda i: (i, 0)), pl.BlockSpec((CONV_W, CONV_CH), lambda i: (0, 0))],
        out_shape=[jax.ShapeDtypeStruct((T, EVEN_IN), BF16), jax.ShapeDtypeStruct((CONV_W, CONV_CH), F32)],
        compiler_params=_params("arbitrary"),
    )(dcv, dcv, zc, zc, conv_k)


def _swap_halves(v):
    lane = lax.broadcasted_iota(jnp.int32, (1, v.shape[1]), 1)
    return jnp.where((lane % HEAD_DIM) < HEAD_DIM // 2, pltpu.roll(v, LANES - HEAD_DIM // 2, 1),
                     pltpu.roll(v, HEAD_DIM // 2, 1))


def _qkv_proj(hn, w_in, rope_c, rope_s, tm=T):
    tn = 4 * LANES

    def body(a_ref, b_ref, c_ref, s_ref, o_ref):
        j = pl.program_id(1)
        acc = _nn(a_ref[...], b_ref[...])
        for p in range(4):
            v = acc[:, p * LANES:(p + 1) * LANES]
            rot = v * c_ref[...] + _swap_halves(v) * s_ref[...]
            o_ref[p] = jnp.where(j < 6, rot, v)

    tab = pl.BlockSpec((tm, LANES), lambda i, j: (i, 0))
    return _pcall(
        body, name="qkv_proj", grid=(T // tm, 9),
        in_specs=[pl.BlockSpec((tm, D), lambda i, j: (i, 0)),
                  pl.BlockSpec((D, tn), lambda i, j: (0, j + (2 * CONV_CH) // tn)), tab, tab],
        out_specs=pl.BlockSpec((None, 4, tm, LANES), lambda i, j: (j, 0, i, 0)),
        out_shape=jax.ShapeDtypeStruct((9, 4, T, LANES), F32),
        compiler_params=_params("parallel", "parallel"),
    )(hn, w_in, rope_c, rope_s)


ATTN_FWD_UNROLL = 4
ATTN_BWD_UNROLL = 4


def _band_rows(start, d):
    if d == 1:
        return pl.ds(pl.multiple_of(start, BAND), BAND)
    return pl.ds(start, BAND, stride=d)


def _band_masks(n):
    row = lax.broadcasted_iota(jnp.int32, (BAND, BAND), 0)
    col = lax.broadcasted_iota(jnp.int32, (BAND, BAND), 1)
    no_prev = (n == 0).astype(jnp.int32) * (2 * BAND)
    return col <= row, col >= row + no_prev


def _attn_fwd(qkv, g):
    d = DILATIONS[g]
    nb = T // d // BAND
    has_prev = nb > 1

    def body(q_ref, k_ref, v_ref, o_ref, l_ref):
        lane_lo = lax.broadcasted_iota(jnp.int32, (BAND, LANES), 1) < HEAD_DIM

        heads = (lane_lo, jnp.logical_not(lane_lo))
        ones = jnp.ones((BAND, LANES), BF16)

        def step(it, carry):
            tiles = []
            for u in range(ATTN_FWD_UNROLL):
                idx = it * ATTN_FWD_UNROLL + u
                r = idx // nb
                n = idx % nb
                cur = _band_rows(n * (BAND * d) + r, d)
                prev = _band_rows(jnp.maximum(n - 1, 0) * (BAND * d) + r, d)
                mc, mp = _band_masks(n)
                kp = k_ref[prev, :].astype(BF16) if has_prev else None
                vp = v_ref[prev, :].astype(BF16) if has_prev else None
                tiles.append((cur, mc, mp, q_ref[cur, :], k_ref[cur, :].astype(BF16), v_ref[cur, :].astype(BF16), kp, vp))
            scores = []
            for cur, mc, mp, q, kc, vc, kp, vp in tiles:
                for hm in heads:
                    qm = jnp.where(hm, q, 0.0).astype(BF16)
                    sc = jnp.where(mc, _nt(qm, kc) * SCALE, NEG)
                    scores.append((sc, jnp.where(mp, _nt(qm, kp) * SCALE, NEG)) if has_prev else (sc,))
            maxes = [functools.reduce(jnp.maximum, [jnp.max(sx, axis=1, keepdims=True) for sx in ss]) for ss in scores]
            probs = [[jnp.exp(sx - mx).astype(BF16) for sx in ss] for ss, mx in zip(scores, maxes)]
            dens = [functools.reduce(jnp.add, [_nn(px, ones) for px in ps]) for ps in probs]
            for t, (cur, mc, mp, q, kc, vc, kp, vp) in enumerate(tiles):
                outs, lses = [], []
                for h in range(2):
                    ps = probs[2 * t + h]
                    acc = _nn(ps[0], vc) + _nn(ps[1], vp) if has_prev else _nn(ps[0], vc)
                    outs.append(acc / dens[2 * t + h])
                    lses.append(maxes[2 * t + h] + jnp.log(dens[2 * t + h]))
                o_ref[cur, :] = jnp.where(lane_lo, outs[0], outs[1])
                l_ref[cur, :] = jnp.where(lane_lo, lses[0], lses[1])
            return carry

        lax.fori_loop(0, d * nb // ATTN_FWD_UNROLL, step, 0)

    def slab(which):
        return pl.BlockSpec((None, None, T, LANES), lambda p: (which * 3 + g, p, 0, 0))

    out = pl.BlockSpec((None, T, LANES), lambda p: (p, 0, 0))
    shape = jax.ShapeDtypeStruct((4, T, LANES), F32)
    return _pcall(
        body, name=f"attn_fwd{g}", grid=(4,),
        in_specs=[slab(0), slab(1), slab(2)], out_specs=[out, out], out_shape=[shape, shape],
        compiler_params=_params("parallel"),
    )(qkv, qkv, qkv)


def _attn_merge(outs, lses, cat, tm=1024):
    def body(o0, o1, o2, l0, l1, l2, cat_in, cat_ref, att_ref, w0, w1, w2):
        del cat_in
        la, lb, lc = l0[...], l1[...], l2[...]
        mx = jnp.maximum(jnp.maximum(la, lb), lc)
        ea, eb, ec = jnp.exp(la - mx), jnp.exp(lb - mx), jnp.exp(lc - mx)
        inv = 1.0 / (ea + eb + ec)
        wa, wb, wc = ea * inv, eb * inv, ec * inv
        att = wa * o0[...] + wb * o1[...] + wc * o2[...]
        att_ref[...] = att
        cat_ref[...] = att.astype(BF16)
        w0[...] = wa
        w1[...] = wb
        w2[...] = wc

    slab = pl.BlockSpec((None, tm, LANES), lambda p, i: (p, i, 0))
    shape = jax.ShapeDtypeStruct((4, T, LANES), F32)
    return _pcall(
        body, name="attn_merge", grid=(4, T // tm),
        in_specs=[slab] * 6 + [pl.BlockSpec(memory_space=pl.ANY)],
        out_specs=[pl.BlockSpec((tm, LANES), lambda p, i: (i, CONV_CH // LANES + p)), slab, slab, slab, slab],
        out_shape=[jax.ShapeDtypeStruct((T, D), BF16), shape, shape, shape, shape],
        input_output_aliases={6: 0},
        compiler_params=_params("parallel", "parallel"),
    )(*outs, *lses, cat)


def _attn_bwd(qkv, lse, wgt, att, dcat, dqkv, g):
    d = DILATIONS[g]
    nb = T // d // BAND
    has_prev = nb > 1

    def body(q_ref, k_ref, v_ref, l_ref, w_ref, a_ref, da_ref, dq_in, o_ref):
        del dq_in
        lane = lax.broadcasted_iota(jnp.int32, (BAND, LANES), 1)
        lane_lo = lane < HEAD_DIM
        row = lax.broadcasted_iota(jnp.int32, (LANES, LANES), 0)
        same_head = ((row // HEAD_DIM) == (lane // HEAD_DIM)).astype(BF16)
        dq_ref, dk_ref, dv_ref = o_ref.at[0], o_ref.at[1], o_ref.at[2]
        if has_prev:
            dk_ref[...] = jnp.zeros((T, LANES), F32)
            dv_ref[...] = jnp.zeros((T, LANES), F32)

        heads = (lane_lo, jnp.logical_not(lane_lo))

        def step(it, carry):
            tiles = []
            for u in range(ATTN_BWD_UNROLL):
                idx = it * ATTN_BWD_UNROLL + u
                r = idx // nb
                n = idx % nb
                cur = _band_rows(n * (BAND * d) + r, d)
                prev = _band_rows(jnp.maximum(n - 1, 0) * (BAND * d) + r, d)
                mc, mp = _band_masks(n)
                da = da_ref[cur, :]
                prod = da * a_ref[cur, :]
                hi = prod.astype(BF16)
                lo = (prod - hi.astype(F32)).astype(BF16)
                tiles.append(dict(cur=cur, prev=prev, mc=mc, mp=mp, da=da, hi=hi, lo=lo, q=q_ref[cur, :],
                                  kc=k_ref[cur, :].astype(BF16), vc=v_ref[cur, :].astype(BF16),
                                  kp=k_ref[prev, :].astype(BF16) if has_prev else None,
                                  vp=v_ref[prev, :].astype(BF16) if has_prev else None,
                                  lse=l_ref[cur, :], w=w_ref[cur, :]))
            for t in tiles:
                t["csum"] = _nn(t["hi"], same_head) + _nn(t["lo"], same_head)
            chains = []
            for t in tiles:
                for h, hm in enumerate(heads):
                    qm = jnp.where(hm, t["q"], 0.0).astype(BF16)
                    dam = jnp.where(hm, t["da"], 0.0).astype(BF16)
                    ch = dict(t=t, h=h, qm=qm, dam=dam, sc=jnp.where(t["mc"], _nt(qm, t["kc"]) * SCALE, NEG),
                              dpc=_nt(dam, t["vc"]))
                    if has_prev:
                        ch.update(sp=jnp.where(t["mp"], _nt(qm, t["kp"]) * SCALE, NEG), dpp=_nt(dam, t["vp"]))
                    chains.append(ch)
            for ch in chains:
                t, col0 = ch["t"], ch["h"] * HEAD_DIM
                lse_h = t["lse"][:, col0:col0 + 1]
                w_h = t["w"][:, col0:col0 + 1]
                c_h = t["csum"][:, col0:col0 + 1]
                pwc = w_h * jnp.exp(ch["sc"] - lse_h)
                ch["dsc"] = (pwc * (ch["dpc"] - c_h) * SCALE).astype(BF16)
                ch["pwc"] = pwc.astype(BF16)
                if has_prev:
                    pwp = w_h * jnp.exp(ch["sp"] - lse_h)
                    ch["dsp"] = (pwp * (ch["dpp"] - c_h) * SCALE).astype(BF16)
                    ch["pwp"] = pwp.astype(BF16)
            for ch in chains:
                t = ch["t"]
                ch["dq"] = _nn(ch["dsc"], t["kc"])
                ch["dkc"] = _tn(ch["dsc"], ch["qm"])
                ch["dvc"] = _tn(ch["pwc"], ch["dam"])
                if has_prev:
                    ch["dq"] = ch["dq"] + _nn(ch["dsp"], t["kp"])
                    ch["dkp"] = _tn(ch["dsp"], ch["qm"])
                    ch["dvp"] = _tn(ch["pwp"], ch["dam"])
            for i, t in enumerate(tiles):
                c0, c1 = chains[2 * i], chains[2 * i + 1]
                dq_ref[t["cur"], :] = jnp.where(lane_lo, c0["dq"], c1["dq"])
                if has_prev:
                    dk_ref[t["cur"], :] += c0["dkc"] + c1["dkc"]
                    dk_ref[t["prev"], :] += c0["dkp"] + c1["dkp"]
                    dv_ref[t["cur"], :] += c0["dvc"] + c1["dvc"]
                    dv_ref[t["prev"], :] += c0["dvp"] + c1["dvp"]
                else:
                    dk_ref[t["cur"], :] = c0["dkc"] + c1["dkc"]
                    dv_ref[t["cur"], :] = c0["dvc"] + c1["dvc"]
            return carry

        lax.fori_loop(0, d * nb // ATTN_BWD_UNROLL, step, 0)

    def slab(which):
        return pl.BlockSpec((None, None, T, LANES), lambda p: (which * 3 + g, p, 0, 0))

    per_pair = pl.BlockSpec((None, T, LANES), lambda p: (p, 0, 0))
    return _pcall(
        body, name=f"attn_bwd{g}", grid=(4,),
        in_specs=[slab(0), slab(1), slab(2), per_pair, per_pair, per_pair,
                  pl.BlockSpec((T, LANES), lambda p: (0, CONV_CH // LANES + p)),
                  pl.BlockSpec(memory_space=pl.ANY)],
        out_specs=pl.BlockSpec((None, 3, None, T, LANES), lambda p: (g, 0, p, 0, 0)),
        out_shape=jax.ShapeDtypeStruct((3, 3, 4, T, LANES), F32),
        input_output_aliases={7: 0},
        compiler_params=_params("parallel"),
    )(qkv, qkv, qkv, lse, wgt, att, dcat, dqkv)


def _rope_bwd(dqkv, rope_c, rope_s, dz):
    wide = 4 * LANES

    def body(d_ref, c_ref, s_ref, dz_in, o_ref):
        del dz_in
        w = pl.program_id(1)
        for p in range(4):
            v = d_ref[p]
            rot = v * c_ref[...] + _swap_halves(v * s_ref[...])
            o_ref[:, p * LANES:(p + 1) * LANES] = jnp.where(w < 2, rot, v).astype(BF16)

    tab = pl.BlockSpec((T, LANES), lambda g, w: (0, 0))
    return _pcall(
        body, name="rope_bwd", grid=(3, 3),
        in_specs=[pl.BlockSpec((None, None, 4, T, LANES), lambda g, w: (g, w, 0, 0, 0)), tab, tab,
                  pl.BlockSpec(memory_space=pl.ANY)],
        out_specs=pl.BlockSpec((T, wide), lambda g, w: (0, (2 * CONV_CH) // wide + w * 3 + g)),
        out_shape=jax.ShapeDtypeStruct((T, EVEN_IN), BF16),
        input_output_aliases={3: 0},
        compiler_params=_params("parallel", "parallel"),
    )(dqkv, rope_c, rope_s, dz)


ODD_TILE = 256
ODD_HALO = 8
GELU_C = 0.7978845608028654
GELU_A = 0.044715


def _gelu(x):
    return 0.5 * x * (1.0 + jnp.tanh(GELU_C * (x + GELU_A * x * x * x)))


def _gelu_grad(x):
    th = jnp.tanh(GELU_C * (x + GELU_A * x * x * x))
    return 0.5 * (1.0 + th) + 0.5 * x * (1.0 - th * th) * GELU_C * (1.0 + 3.0 * GELU_A * x * x)


def _tril():
    row = lax.broadcasted_iota(jnp.int32, (CHUNK, CHUNK), 0)
    col = lax.broadcasted_iota(jnp.int32, (CHUNK, CHUNK), 1)
    return (col <= row).astype(F32)


def _odd_parts(z, zh, i, k_ref, g_ref, be_ref, w_ref, bt_ref):
    R, H = ODD_TILE, ODD_HALO
    gb, gc, xs, uv = z[:, :512], z[:, 512:1024], z[:, 1024:1536], z[:, 1536:]
    halo = zh[:, 512:1024] * zh[:, 1024:1536] * (i > 0).astype(F32)
    win = jnp.concatenate([halo, gc * xs], axis=0)
    cv = jnp.zeros((R, 512), F32)
    for j in range(SCONV_W):
        off = H - (SCONV_W - 1) + j
        cv = cv + k_ref[j:j + 1, :] * win[off:off + R, :]
    ge = _gelu(uv)
    u, v = ge[:, :512], ge[:, 512:]
    mu = jnp.mean(v, axis=-1, keepdims=True)
    xc = v - mu
    rstd = lax.rsqrt(jnp.mean(xc * xc, axis=-1, keepdims=True) + EPS)
    xh = xc * rstd
    vn = xh * g_ref[...] + be_ref[...]
    tril = _tril()
    wms = [(w_ref[g] * tril).astype(BF16) for g in range(SG_GROUPS)]
    rows = []
    for ci in range(R // CHUNK):
        blocks = []
        for g in range(SG_GROUPS):
            blk = vn[ci * CHUNK:(ci + 1) * CHUNK, g * LANES:(g + 1) * LANES].astype(BF16)
            blocks.append(_nn(wms[g], blk) + bt_ref[:, g:g + 1])
        rows.append(jnp.concatenate(blocks, axis=1))
    vmix = jnp.concatenate(rows, axis=0)
    return gb, gc, xs, uv, win, cv, u, rstd, xh, vn, vmix, wms


def _odd_mid_fwd(z, conv_k, ln_g, ln_b, sg_w, sg_bt):
    R, H = ODD_TILE, ODD_HALO

    def body(z_ref, zh_ref, k_ref, g_ref, be_ref, w_ref, bt_ref, o_ref):
        i = pl.program_id(0)
        gb, _, _, _, _, cv, u, _, _, _, vmix, _ = _odd_parts(z_ref[...], zh_ref[...], i, k_ref, g_ref, be_ref, w_ref, bt_ref)
        o_ref[...] = jnp.concatenate([gb * cv, u * vmix], axis=1).astype(BF16)

    vec = pl.BlockSpec((1, 512), lambda i: (0, 0))
    return _pcall(
        body, name="odd_mid_fwd", grid=(T // R,),
        in_specs=[pl.BlockSpec((R, ODD_IN), lambda i: (i, 0)),
                  pl.BlockSpec((H, ODD_IN), lambda i: (jnp.maximum(i * (R // H) - 1, 0), 0)),
                  pl.BlockSpec((SCONV_W, 512), lambda i: (0, 0)), vec, vec,
                  pl.BlockSpec((SG_GROUPS, CHUNK, CHUNK), lambda i: (0, 0, 0)),
                  pl.BlockSpec((CHUNK, SG_GROUPS), lambda i: (0, 0))],
        out_specs=pl.BlockSpec((R, D), lambda i: (i, 0)),
        out_shape=jax.ShapeDtypeStruct((T, D), BF16),
        compiler_params=_params("parallel"),
    )(z, z, conv_k, ln_g, ln_b, sg_w, sg_bt)


def _odd_mid_bwd(z, dcat, conv_k, ln_g, ln_b, sg_w, sg_bt):
    R, H = ODD_TILE, ODD_HALO
    last = T // R - 1

    def body(z_ref, zh_ref, zn_ref, d_ref, dn_ref, k_ref, g_ref, be_ref, w_ref, bt_ref,
             dz_ref, dk_ref, dg_ref, dbe_ref, dw_ref, dbt_ref):
        i = pl.program_id(0)
        z = z_ref[...]
        gb, gc, xs, uv, win, cv, u, rstd, xh, vn, vmix, wms = _odd_parts(z, zh_ref[...], i, k_ref, g_ref, be_ref, w_ref, bt_ref)
        dcat_t = d_ref[...]
        dc, dd = dcat_t[:, :512], dcat_t[:, 512:]

        @pl.when(i == 0)
        def _():
            dk_ref[...] = jnp.zeros_like(dk_ref)
            dg_ref[...] = jnp.zeros_like(dg_ref)
            dbe_ref[...] = jnp.zeros_like(dbe_ref)
            dw_ref[...] = jnp.zeros_like(dw_ref)
            dbt_ref[...] = jnp.zeros_like(dbt_ref)

        dgb = dc * cv
        dcv = dc * gb
        nxt = dn_ref[:, :512] * zn_ref[:, :512] * (i < last).astype(F32)
        winb = jnp.concatenate([dcv, nxt], axis=0)
        dp = jnp.zeros((R, 512), F32)
        for j in range(SCONV_W):
            off = H - (SCONV_W - 1) + j
            dk_ref[j:j + 1, :] += jnp.sum(dcv * win[off:off + R, :], axis=0, keepdims=True)
            ob = SCONV_W - 1 - j
            dp = dp + k_ref[j:j + 1, :] * winb[ob:ob + R, :]
        dgc = dp * xs
        dxs = dp * gc
        du = dd * vmix
        dvmix = dd * u
        tril = _tril()
        rows = []
        for ci in range(R // CHUNK):
            blocks = []
            for g in range(SG_GROUPS):
                sl = (slice(ci * CHUNK, (ci + 1) * CHUNK), slice(g * LANES, (g + 1) * LANES))
                dblk = dvmix[sl]
                dblk16 = dblk.astype(BF16)
                blocks.append(_tn(wms[g], dblk16))
                dw_ref[g] += _nt(dblk16, vn[sl].astype(BF16)) * tril
                dbt_ref[:, g:g + 1] += jnp.sum(dblk, axis=1, keepdims=True)
            rows.append(jnp.concatenate(blocks, axis=1))
        dvn = jnp.concatenate(rows, axis=0)
        dg_ref[...] += jnp.sum(dvn * xh, axis=0, keepdims=True)
        dbe_ref[...] += jnp.sum(dvn, axis=0, keepdims=True)
        dxh = dvn * g_ref[...]
        dv = rstd * (dxh - jnp.mean(dxh, axis=-1, keepdims=True) - xh * jnp.mean(dxh * xh, axis=-1, keepdims=True))
        duv = jnp.concatenate([du, dv], axis=1) * _gelu_grad(uv)
        dz_ref[...] = jnp.concatenate([dgb, dgc, dxs, duv], axis=1).astype(BF16)

    vec = pl.BlockSpec((1, 512), lambda i: (0, 0))
    kspec = pl.BlockSpec((SCONV_W, 512), lambda i: (0, 0))
    wspec = pl.BlockSpec((SG_GROUPS, CHUNK, CHUNK), lambda i: (0, 0, 0))
    bspec = pl.BlockSpec((CHUNK, SG_GROUPS), lambda i: (0, 0))
    nxt_blk = lambda i: (jnp.minimum((i + 1) * (R // H), T // H - 1), 0)
    return _pcall(
        body, name="odd_mid_bwd", grid=(T // R,),
        in_specs=[pl.BlockSpec((R, ODD_IN), lambda i: (i, 0)),
                  pl.BlockSpec((H, ODD_IN), lambda i: (jnp.maximum(i * (R // H) - 1, 0), 0)),
                  pl.BlockSpec((H, ODD_IN), nxt_blk),
                  pl.BlockSpec((R, D), lambda i: (i, 0)),
                  pl.BlockSpec((H, D), nxt_blk),
                  kspec, vec, vec, wspec, bspec],
        out_specs=[pl.BlockSpec((R, ODD_IN), lambda i: (i, 0)), kspec, vec, vec, wspec, bspec],
        out_shape=[jax.ShapeDtypeStruct((T, ODD_IN), BF16), jax.ShapeDtypeStruct((SCONV_W, 512), F32),
                   jax.ShapeDtypeStruct((1, 512), F32), jax.ShapeDtypeStruct((1, 512), F32),
                   jax.ShapeDtypeStruct((SG_GROUPS, CHUNK, CHUNK), F32), jax.ShapeDtypeStruct((CHUNK, SG_GROUPS), F32)],
        compiler_params=_params("arbitrary"),
    )(z, z, z, dcat, dcat, conv_k, ln_g, ln_b, sg_w, sg_bt)


def _ffn_up(tag, hn, weight):
    def act(acc):
        r = jnp.maximum(acc, 0.0)
        return (r * r,)

    return _mm(f"ffn{tag}_up", "nn", hn, weight(f"ffn_w1_{tag}", hn), T, D_FF, D, (BF16,), epi=act)


FFN_BWD_TILE = 256


def _ffn_dx(name, dout, w2, w1, f, h, g, tie=None):
    tm = FFN_BWD_TILE
    ties = () if tie is None else (tie,)

    def body(d_ref, w2_ref, w1_ref, f_ref, h_ref, g_ref, *rest):
        du_ref, dh_ref, dg_ref = rest[len(ties):]
        dres = d_ref[...]
        du = (_nt(dres.astype(BF16), w2_ref[...]) * (2.0 * jnp.sqrt(f_ref[...].astype(F32)))).astype(BF16)
        du_ref[...] = du
        dy = _nt(du, w1_ref[...])
        x = h_ref[...]
        r = lax.rsqrt(jnp.mean(x * x, axis=-1, keepdims=True) + EPS)
        nrm = x * r
        dn = dy * g_ref[...]
        dh_ref[...] = dres + r * (dn - nrm * jnp.mean(dn * nrm, axis=-1, keepdims=True))

        @pl.when(pl.program_id(0) == 0)
        def _():
            dg_ref[...] = jnp.zeros_like(dg_ref)

        dg_ref[...] += jnp.sum(dy * nrm, axis=0, keepdims=True)

    row = pl.BlockSpec((tm, D), lambda i: (i, 0))
    wide = pl.BlockSpec((tm, D_FF), lambda i: (i, 0))
    vec = pl.BlockSpec((1, D), lambda i: (0, 0))
    return _pcall(
        body, name=name, grid=(T // tm,),
        in_specs=[row, _resident((D_FF, D)), _resident((D, D_FF)), wide, row, vec]
        + [pl.BlockSpec(TOKEN_SHAPE, lambda i: (0, 0))] * len(ties),
        out_specs=[wide, row, vec],
        out_shape=[jax.ShapeDtypeStruct((T, D_FF), BF16), jax.ShapeDtypeStruct((T, D), F32),
                   jax.ShapeDtypeStruct((1, D), F32)],
        compiler_params=_params("arbitrary"),
    )(dout, w2, w1, f, h, g, *ties)


def _ffn_bwd(tag, h, g, weight, emit, saved, dout, tie=None):
    hn, f = saved
    du, dh, dg = _ffn_dx(f"ffn{tag}_dx", dout, weight(f"ffn_w2_{tag}", dout), weight(f"ffn_w1_{tag}", dout), f, h, g, tie)
    emit(f"ffn_w2_{tag}", f, dout)
    return dh, dg, emit(f"ffn_w1_{tag}", hn, du)


def _rope_tables():
    half = HEAD_DIM // 2
    inv = 10000.0 ** (-jnp.arange(half, dtype=F32) / half)
    ang = jnp.arange(T, dtype=F32)[:, None] * inv[None, :]
    cos, sin = jnp.cos(ang), jnp.sin(ang)
    c = jnp.tile(jnp.concatenate([cos, cos], axis=1), (1, LANES // HEAD_DIM))
    s = jnp.tile(jnp.concatenate([-sin, sin], axis=1), (1, LANES // HEAD_DIM))
    return c, s


def _local_step(x, target, p, weight, emit, emit_small, before=lambda name, after: None):
    rope_c, rope_s = _rope_tables()
    grads = {}

    hn0 = _rms_fwd("mix0_norm", x, p["norm_mix_g0"])
    zc = _mm("even_in_conv", "nn", hn0, weight("even_w_in", hn0), T, 2 * CONV_CH, D, (F32,))
    qkv = _qkv_proj(hn0, weight("even_w_in", hn0), rope_c, rope_s)
    cv, cat0 = _econv_fwd(zc, p["even_conv_k"], p["even_conv_b"], p["even_ln_g"], p["even_ln_b"])
    att_parts = [_attn_fwd(qkv, 0)]
    before("attn_fwd1", att_parts[0][0])
    att_parts += [_attn_fwd(qkv, 1), _attn_fwd(qkv, 2)]
    outs = [a[0] for a in att_parts]
    lses = [a[1] for a in att_parts]
    cat0, att, w0, w1, w2 = _attn_merge(outs, lses, cat0)
    wgts = (w0, w1, w2)
    h1, hnf0 = _mm_out_norm("even_out", cat0, weight("even_w_out", cat0), D, x, p["norm_ffn_g0"])
    f0 = _ffn_up(0, hnf0, weight)
    before("ffn0_down", f0)
    h2, hn1 = _mm_out_norm("ffn0_down", f0, weight("ffn_w2_0", f0), D_FF, h1, p["norm_mix_g1"], tm=FFN0_DOWN_TILE)

    z1 = _mm("odd_in", "nn", hn1, weight("odd_w_in", hn1), T, ODD_IN, D, (F32,))
    before("odd_mid_fwd", z1)
    cat1 = _odd_mid_fwd(z1, p["odd_conv_k"], p["odd_ln_g"], p["odd_ln_b"], p["odd_sg_w"], p["odd_sg_bt"])
    h3, hnf1 = _mm_out_norm("odd_out", cat1, weight("odd_w_out", cat1), D, h2, p["norm_ffn_g1"])
    f1, dh4, grads["final_g"], loss = _ffn_last("ffn1_loss", hnf1, weight("ffn_w1_1", hnf1), weight("ffn_w2_1", hnf1),
                                                h3, p["final_g"], target)

    dh3, grads["norm_ffn_g1"], tok = _ffn_bwd(1, h3, p["norm_ffn_g1"], weight, emit, (hnf1, f1), dh4)
    tok = emit("odd_w_out", cat1, dh3, tie=tok)
    dcat1 = _mm("odd_out_dx", "nt", dh3, weight("odd_w_out", dh3), T, D, D, (F32,), tie=tok)
    dz1, grads["odd_conv_k"], grads["odd_ln_g"], grads["odd_ln_b"], grads["odd_sg_w"], grads["odd_sg_bt"] = _odd_mid_bwd(
        z1, dcat1, p["odd_conv_k"], p["odd_ln_g"], p["odd_ln_b"], p["odd_sg_w"], p["odd_sg_bt"])
    tok = emit("odd_w_in", hn1, dz1)
    dh2, grads["norm_mix_g1"] = _mm_dx_norm("odd_in_dx", dz1, weight("odd_w_in", dz1), ODD_IN, h2, p["norm_mix_g1"],
                                            dh3, tie=tok)

    dh1, grads["norm_ffn_g0"], tok = _ffn_bwd(0, h1, p["norm_ffn_g0"], weight, emit, (hnf0, f0), dh2)
    tok = emit("even_w_out", cat0, dh1, tie=tok)
    dcat0 = _mm("even_out_dx", "nt", dh1, weight("even_w_out", dh1), T, D, D, (F32,), tie=tok)
    dcv, grads["even_ln_g"], grads["even_ln_b"], grads["even_conv_b"] = _econv_bwd_ln(
        cv, dcat0, p["even_ln_g"], p["even_ln_b"])
    dz0, grads["even_conv_k"] = _econv_bwd_conv(dcv, zc, p["even_conv_k"])
    tok = emit_small(loss, grads)
    dqkv = lax.empty((3, 3, 4, T, LANES), F32)
    for g in range(3):
        dqkv = _attn_bwd(qkv, lses[g], wgts[g], att, dcat0, dqkv, g)
    before("rope_bwd", dqkv)
    dz0 = _rope_bwd(dqkv, rope_c, rope_s, dz0)
    tok = emit("even_w_in", hn0, dz0, tie=tok)
    dx, dg0 = _mm_dx_norm("even_in_dx", dz0, weight("even_w_in", dz0), EVEN_IN, x, p["norm_mix_g0"], dh1, tie=tok)
    return dx, dg0


def _rowwise(name, fn, ins, out_dtypes, tm=256, tie=None):
    rows, cols = ins[0].shape
    tm = tm if rows % tm == 0 else rows
    n_in = len(ins)
    ties = () if tie is None else (tie,)

    def body(*refs):
        vals = fn(*[r[...] for r in refs[:n_in]])
        for o_ref, v in zip(refs[n_in + len(ties):], vals):
            o_ref[...] = v.astype(o_ref.dtype)

    spec = pl.BlockSpec((tm, cols), lambda i: (i, 0))
    outs = _pcall(
        body, name=name, grid=(rows // tm,),
        in_specs=[spec] * n_in + [pl.BlockSpec(TOKEN_SHAPE, lambda i: (0, 0))] * len(ties),
        out_specs=[spec] * len(out_dtypes),
        out_shape=[jax.ShapeDtypeStruct((rows, cols), dt) for dt in out_dtypes],
        compiler_params=_params("parallel"),
    )(*ins, *ties)
    return outs[0] if len(out_dtypes) == 1 else outs


def _adamw(name, w, g, m, v, with_grad=False, tie=None):
    c1 = 1.0 - ADAM_B1 ** ADAM_STEP
    c2 = 1.0 - ADAM_B2 ** ADAM_STEP

    def fn(w_t, g_t, m_t, v_t):
        m_new = ADAM_B1 * m_t + (1.0 - ADAM_B1) * g_t
        v_new = ADAM_B2 * v_t + (1.0 - ADAM_B2) * (g_t * g_t)
        delta = -ADAM_LR * ((m_new / c1) / (jnp.sqrt(v_new / c2) + ADAM_EPS) + ADAM_WD * w_t)
        return (delta, m_new, v_new, g_t) if with_grad else (delta, m_new, v_new)

    return _rowwise(name, fn, (w, g, m, v), (F32,) * (4 if with_grad else 3), tie=tie)


class _Piece:
    def __init__(self, name, rows, cols, axis, src, src_row0):
        self.name, self.rows, self.cols, self.axis = name, rows, cols, axis
        self.width = (cols if axis == 1 else rows) // 4
        self.src, self.src_row0 = src, src_row0

    @property
    def full_shape(self):
        return (self.rows, self.cols)

    @property
    def half_shape(self):
        return (self.rows // 2, self.cols) if self.axis == 1 else (self.rows, self.cols // 2)

    @property
    def shard_half_shape(self):
        return (self.rows // 2, self.width) if self.axis == 1 else (self.width, self.cols // 2)

    def shard_whole(self, ref):
        n = self.rows if self.axis == 1 else self.width
        return ref.at[pl.ds(self.src_row0, n), :]

    def shard_half(self, ref, h):
        if self.axis == 1:
            return ref.at[pl.ds(self.src_row0 + h * (self.rows // 2), self.rows // 2), :]
        return ref.at[pl.ds(self.src_row0, self.width), pl.ds(h * (self.cols // 2), self.cols // 2)]

    def full_shard(self, ref, s):
        if self.axis == 1:
            return ref.at[:, pl.ds(s * self.width, self.width)]
        return ref.at[pl.ds(s * self.width, self.width), :]

    def full_shard_half(self, ref, s, h):
        if self.axis == 1:
            return ref.at[pl.ds(h * (self.rows // 2), self.rows // 2), pl.ds(s * self.width, self.width)]
        return ref.at[pl.ds(s * self.width, self.width), pl.ds(h * (self.cols // 2), self.cols // 2)]

    def full_half(self, ref, h):
        if self.axis == 1:
            return ref.at[pl.ds(h * (self.rows // 2), self.rows // 2), :]
        return ref.at[:, pl.ds(h * (self.cols // 2), self.cols // 2)]

    def full_half_rows(self, ref, h, r0, n):
        if self.axis == 1:
            return ref.at[pl.ds(h * (self.rows // 2) + r0, n), :]
        return ref.at[pl.ds(r0, n), pl.ds(h * (self.cols // 2), self.cols // 2)]

    def half_shard(self, ref, s):
        return self.full_shard(ref, s)


PIECES = (
    _Piece("even_w_in", D, EVEN_IN, 1, 0, 0),
    _Piece("even_w_out", D, D, 0, 1, 0),
    _Piece("ffn_w1_0", D, D_FF, 1, 4, 0),
    _Piece("ffn_w2_0", D_FF, D, 0, 5, 0),
    _Piece("odd_w_in", D, ODD_IN, 1, 2, 0),
    _Piece("odd_w_out", D, D, 0, 3, 0),
    _Piece("ffn_w1_1", D, D_FF, 1, 4, D),
    _Piece("ffn_w2_1", D_FF, D, 0, 5, D_FF // 4),
)
N_PIECES = len(PIECES)
FORWARD_RIDES = {"attn_fwd1": (1, 2, 3), "ffn0_down": (4, 5), "odd_mid_fwd": (6, 7)}
JOIN_GROUPS = ((0, 1, 2, 3), (4, 5))
JOIN_RIDES_IN = "rope_bwd"
HOLD_BACK = ("ffn_w2_0", "ffn_w2_1", "odd_w_out")
N_SHARD_OPERANDS = 6
ANY = pl.BlockSpec(memory_space=pl.ANY)
MESH = pl.DeviceIdType.MESH


def _mesh_place():
    x, y, c = lax.axis_index("x"), lax.axis_index("y"), lax.axis_index("c")
    chips = [(1 - x, y), (x, 1 - y), (1 - x, 1 - y)]
    return x, y, c, chips


def _remote(src, dst, send_sem, recv_sem, dev):
    return pltpu.make_async_remote_copy(src_ref=src, dst_ref=dst, send_sem=send_sem, recv_sem=recv_sem,
                                        device_id=dev, device_id_type=MESH)


HBM = pl.BlockSpec(memory_space=pltpu.HBM)
SEM = pl.BlockSpec(memory_space=pltpu.SEMAPHORE)
SPLIT_PARAMS = pltpu.CompilerParams(has_side_effects=pltpu.SideEffectType.DATAFLOW_SIDE_EFFECTING)
CAST_TILE = 256


def _in_hbm(a):
    return pltpu.with_memory_space_constraint(a, pltpu.HBM)


def _cast_place(pc, shard_operand, chip, tie=None):
    rows, cols = (pc.rows, pc.width) if pc.axis == 1 else (pc.width, pc.cols)
    nblk = rows // CAST_TILE
    blk0 = pc.src_row0 // CAST_TILE
    ties = () if tie is None else (tie,)

    def body(chip_ref, x_ref, *rest):
        del chip_ref
        rest[-1][...] = x_ref[...].astype(BF16)

    if pc.axis == 1:
        out_map = lambda i, chip_ref: (i, chip_ref[0])
    else:
        out_map = lambda i, chip_ref: (chip_ref[0] * nblk + i, 0)
    return _pcall(
        body, name=f"cast_{pc.name}",
        grid_spec=pltpu.PrefetchScalarGridSpec(
            num_scalar_prefetch=1, grid=(nblk,),
            in_specs=[pl.BlockSpec((CAST_TILE, cols), lambda i, chip_ref: (blk0 + i, 0))]
            + [pl.BlockSpec(TOKEN_SHAPE, lambda i, chip_ref: (0, 0))] * len(ties),
            out_specs=pl.BlockSpec((CAST_TILE, cols), out_map)),
        out_shape=jax.ShapeDtypeStruct(pc.full_shape, BF16),
        compiler_params=_params("parallel"),
    )(chip, shard_operand, *ties)


def _gather_start(name, pieces, fulls):
    n = len(pieces)

    def body(*refs):
        ins = refs[:n]
        sends = refs[2 * n:3 * n]
        recvs = refs[3 * n:4 * n]
        token = refs[4 * n]
        x, y, c, chips = _mesh_place()
        s = 2 * x + y
        for i, pc in enumerate(pieces):
            win = pc.full_shard_half(ins[i], s, c)
            for k, (cx, cy) in enumerate(chips):
                _remote(win, win, sends[i].at[k], recvs[i].at[k], (cx, cy, c)).start()
        token[...] = jnp.zeros(TOKEN_SHAPE, F32)

    sems = [pltpu.SemaphoreType.DMA((3,))] * (2 * n)
    outs = _pcall(
        body, name=name,
        in_specs=[HBM] * n,
        out_specs=[HBM] * n + [SEM] * (2 * n) + [pl.BlockSpec(memory_space=pltpu.VMEM)],
        out_shape=[pltpu.HBM(pc.full_shape, BF16) for pc in pieces] + sems + [jax.ShapeDtypeStruct(TOKEN_SHAPE, F32)],
        input_output_aliases={i: i for i in range(n)},
        compiler_params=SPLIT_PARAMS,
    )(*[_in_hbm(f) for f in fulls])
    return outs[:n], outs[n:2 * n], outs[2 * n:3 * n], outs[3 * n]


def _gather_wait(pc, full, send_sems, recv_sems, after):
    def body(full_ref, send_ref, recv_ref, after_ref, out_ref):
        del after_ref, out_ref
        x, y, c, chips = _mesh_place()
        for k, (cx, cy) in enumerate(chips):
            win = pc.full_shard_half(full_ref, 2 * cx + cy, c)
            cp = _remote(win, win, send_ref.at[k], recv_ref.at[k], (cx, cy, c))
            cp.wait_send()
            cp.wait_recv()

    return _pcall(
        body, name=f"gather_wait_{pc.name}",
        in_specs=[HBM, SEM, SEM, ANY], out_specs=HBM, out_shape=pltpu.HBM(pc.full_shape, BF16),
        input_output_aliases={0: 0}, compiler_params=SPLIT_PARAMS,
    )(full, send_sems, recv_sems, after)


def _core_forward_job(pieces, fulls):
    n = len(pieces)

    def copies(ins, outs, scr):
        send_bufs, recv_bufs = scr[:n], scr[n:2 * n]
        load_sems, send_sems, recv_sems, store_sems = scr[2 * n:]
        x, y, c, chips = _mesh_place()
        loads, sends, stores = [], [], []
        for i, pc in enumerate(pieces):
            for k, (cx, cy) in enumerate(chips):
                j = 3 * i + k
                loads.append(pltpu.make_async_copy(pc.full_shard_half(ins[i], 2 * cx + cy, c), send_bufs[i].at[k],
                                                   load_sems.at[j]))
                sends.append(_remote(send_bufs[i].at[k], recv_bufs[i].at[k], send_sems.at[j], recv_sems.at[j],
                                     (x, y, 1 - c)))
                stores.append(pltpu.make_async_copy(recv_bufs[i].at[k], pc.full_shard_half(outs[i], 2 * cx + cy, 1 - c),
                                                    store_sems.at[j]))
        return loads, sends, stores

    def begin(ins, outs, scr):
        for cp in copies(ins, outs, scr)[0]:
            cp.start()

    def advance(ins, outs, scr):
        loads, sends, _ = copies(ins, outs, scr)
        for load, send in zip(loads, sends):
            load.wait()
            send.start()

    def finish(ins, outs, scr):
        _, sends, stores = copies(ins, outs, scr)
        for send, store in zip(sends, stores):
            send.wait_recv()
            store.start()
        for send, store in zip(sends, stores):
            send.wait_send()
            store.wait()

    sems = pltpu.SemaphoreType.DMA((3 * n,))
    bufs = [pltpu.VMEM((3,) + pc.shard_half_shape, BF16) for pc in pieces]
    return _SideJob(fulls, [jax.ShapeDtypeStruct(pc.full_shape, BF16) for pc in pieces],
                    bufs + bufs + [sems, sems, sems, sems], {i: i for i in range(n)}, begin, advance, finish)


def _run_job(name, job):
    def body(o_ref):
        o_ref[...] = jnp.zeros(TOKEN_SHAPE, F32)

    _ride_next_call(job)
    _pcall(body, name=name, grid=(3,), in_specs=[], out_specs=pl.BlockSpec(TOKEN_SHAPE, lambda i: (0, 0)),
           out_shape=jax.ShapeDtypeStruct(TOKEN_SHAPE, F32))()
    return job.results


def _core_forward(pieces, fulls):
    n = len(pieces)

    def body(*refs):
        ins, outs = refs[:n], refs[n:2 * n]
        send_bufs, recv_bufs = refs[2 * n:3 * n], refs[3 * n:4 * n]
        load_sems, send_sems, recv_sems, store_sems = refs[4 * n:]
        x, y, c, chips = _mesh_place()
        loads, sends, stores = [], [], []
        for i, pc in enumerate(pieces):
            for k, (cx, cy) in enumerate(chips):
                cp = pltpu.make_async_copy(pc.full_shard_half(ins[i], 2 * cx + cy, c), send_bufs[i].at[k],
                                           load_sems.at[3 * i + k])
                cp.start()
                loads.append(cp)
        _sibling_handshake()
        for i in range(n):
            for k in range(3):
                j = 3 * i + k
                loads[j].wait()
                cp = _remote(send_bufs[i].at[k], recv_bufs[i].at[k], send_sems.at[j], recv_sems.at[j], (x, y, 1 - c))
                cp.start()
                sends.append(cp)
        for i, pc in enumerate(pieces):
            for k, (cx, cy) in enumerate(chips):
                j = 3 * i + k
                sends[j].wait_recv()
                cp = pltpu.make_async_copy(recv_bufs[i].at[k], pc.full_shard_half(outs[i], 2 * cx + cy, 1 - c),
                                           store_sems.at[j])
                cp.start()
                stores.append(cp)
        for j in range(3 * n):
            sends[j].wait_send()
            stores[j].wait()

    sems = pltpu.SemaphoreType.DMA((3 * n,))
    bufs = [pltpu.VMEM((3,) + pc.shard_half_shape, BF16) for pc in pieces]
    return _pcall(
        body, name="core_forward_" + pieces[0].name, in_specs=[ANY] * n, out_specs=[ANY] * n,
        out_shape=[jax.ShapeDtypeStruct(pc.full_shape, BF16) for pc in pieces],
        scratch_shapes=bufs + bufs + [sems, sems, sems, sems],
        input_output_aliases={i: i for i in range(n)},
        compiler_params=pltpu.CompilerParams(vmem_limit_bytes=VMEM_LIMIT, collective_id=SIBLING_COLLECTIVE_ID),
    )(*fulls)


def _dw_tile(pc):
    if pc.axis == 1:
        tn = max(t for t in range(LANES, pc.cols + 1, LANES) if pc.cols % t == 0 and t <= 1408)
        return pc.rows // 2, tn
    return min(pc.rows, 1024), pc.cols // 2


def _mm_dw_chipsum(pc, a, b, core, tie=None):
    tm, tn = _dw_tile(pc)
    hr, hc = pc.half_shape
    tiles_r, tiles_c = hr // tm, hc // tn
    th = tiles_r * tiles_c
    ties = () if tie is None else (tie,)

    def tile_of(s, core_ref):
        mine = s >= th
        half = jnp.where(mine, core_ref[0], 1 - core_ref[0])
        local = s % th
        li, lj = local // tiles_c, local % tiles_c
        if pc.axis == 1:
            return half * tiles_r + li, lj, li, lj, mine
        return li, half * tiles_c + lj, li, lj, mine

    def body(core_ref, a_ref, b_ref, *rest):
        o_ref, send_buf, recv_buf, send_sems, recv_sems = rest[len(ties):]
        s = pl.program_id(0)
        local = s % th
        x, y, c = lax.axis_index("x"), lax.axis_index("y"), lax.axis_index("c")
        acc = _tn(a_ref[...].astype(BF16), b_ref[...].astype(BF16))

        def push(slot):
            return _remote(send_buf.at[slot], recv_buf.at[slot], send_sems.at[slot], recv_sems.at[slot], (x, y, 1 - c))

        @pl.when(s == 0)
        def _():
            _sibling_handshake()

        @pl.when(s < th)
        def _():
            send_buf[local] = acc.astype(BF16)
            push(local).start()

        @pl.when(s >= th)
        def _():
            push(local).wait_recv()
            o_ref[...] = (acc + recv_buf[local].astype(F32)).astype(BF16)

        @pl.when(s == 2 * th - 1)
        def _():
            for slot in range(th):
                push(slot).wait_send()

    def a_map(s, core_ref):
        return 0, tile_of(s, core_ref)[0]

    def b_map(s, core_ref):
        return 0, tile_of(s, core_ref)[1]

    def o_map(s, core_ref):
        _, _, li, lj, mine = tile_of(s, core_ref)
        return jnp.where(mine, li, 0), jnp.where(mine, lj, 0)

    return _pcall(
        body, name=f"dw_{pc.name}",
        grid_spec=pltpu.PrefetchScalarGridSpec(
            num_scalar_prefetch=1, grid=(2 * th,),
            in_specs=[pl.BlockSpec((T, tm), a_map), pl.BlockSpec((T, tn), b_map)]
            + [pl.BlockSpec(TOKEN_SHAPE, lambda s, core_ref: (0, 0))] * len(ties),
            out_specs=pl.BlockSpec((tm, tn), o_map),
            scratch_shapes=[pltpu.VMEM((th, tm, tn), BF16), pltpu.VMEM((th, tm, tn), BF16),
                            pltpu.SemaphoreType.DMA((th,)), pltpu.SemaphoreType.DMA((th,))]),
        out_shape=jax.ShapeDtypeStruct((hr, hc), BF16),
        compiler_params=pltpu.CompilerParams(dimension_semantics=("arbitrary",), vmem_limit_bytes=VMEM_LIMIT,
                                             collective_id=SIBLING_COLLECTIVE_ID),
    )(core, a, b, *ties)


def _scatter_start(pieces, chip_sums):
    n = len(pieces)

    def body(*refs):
        sums, lands = refs[:n], refs[n:2 * n]
        sends, recvs = refs[4 * n:5 * n], refs[5 * n:6 * n]
        token = refs[6 * n]
        x, y, c, chips = _mesh_place()
        for i, pc in enumerate(pieces):
            for k, (cx, cy) in enumerate(chips):
                _remote(pc.half_shard(sums[i], 2 * cx + cy), lands[i].at[k], sends[i].at[k], recvs[i].at[k],
                        (cx, cy, c)).start()
        token[...] = jnp.zeros(TOKEN_SHAPE, F32)

    land_shapes = [(3,) + pc.shard_half_shape for pc in pieces]
    sems = [pltpu.SemaphoreType.DMA((3,))] * (2 * n)
    outs = _pcall(
        body, name="scatter_start_" + pieces[0].name,
        in_specs=[HBM] * (2 * n), out_specs=[HBM] * (2 * n) + [SEM] * (2 * n) + [pl.BlockSpec(memory_space=pltpu.VMEM)],
        out_shape=[pltpu.HBM(pc.half_shape, BF16) for pc in pieces] + [pltpu.HBM(sh, BF16) for sh in land_shapes]
        + sems + [jax.ShapeDtypeStruct(TOKEN_SHAPE, F32)],
        input_output_aliases={i: i for i in range(2 * n)}, compiler_params=SPLIT_PARAMS,
    )(*[_in_hbm(cs) for cs in chip_sums], *[_in_hbm(lax.empty(sh, BF16)) for sh in land_shapes])
    return [(outs[i], outs[n + i], outs[2 * n + i], outs[3 * n + i]) for i in range(n)], outs[4 * n]


def _scatter_wait(pc, chip_sum, land, send_sems, recv_sems, after):
    def body(sum_ref, land_ref, send_ref, recv_ref, after_ref, sum_out, land_out):
        del after_ref, sum_out, land_out
        x, y, c, chips = _mesh_place()
        for k, (cx, cy) in enumerate(chips):
            cp = _remote(pc.half_shard(sum_ref, 2 * cx + cy), land_ref.at[k], send_ref.at[k], recv_ref.at[k], (cx, cy, c))
            cp.wait_send()
            cp.wait_recv()

    return _pcall(
        body, name=f"scatter_wait_{pc.name}",
        in_specs=[HBM, HBM, SEM, SEM, ANY], out_specs=[HBM, HBM],
        out_shape=[pltpu.HBM(pc.half_shape, BF16), pltpu.HBM((3,) + pc.shard_half_shape, BF16)],
        input_output_aliases={0: 0, 1: 1}, compiler_params=SPLIT_PARAMS,
    )(chip_sum, land, send_sems, recv_sems, after)


SHARD_OPERAND_SHAPES = ((D, EVEN_IN // 4), (D // 4, D), (D, ODD_IN // 4), (D // 4, D), (2 * D, D_FF // 4), (2 * D_FF // 4, D))


def _allsum_join_job(operands, chip_sums, lands):
    pieces = [pc for pc in PIECES if pc.src in operands]
    n = len(pieces)

    def copies(ins, outs, scr):
        sum_refs, land_refs = ins[:n], ins[n:]
        out_refs = dict(zip(operands, outs))
        in_bufs, fin_bufs, recv_bufs = scr[:n], scr[n:2 * n], scr[2 * n:3 * n]
        load_sems, send_sems, recv_sems, out_sems = scr[3 * n:]
        x, y, c, _ = _mesh_place()
        s = 2 * x + y
        loads, sends, mine, theirs = [], [], [], []
        for j, pc in enumerate(pieces):
            loads.append((pltpu.make_async_copy(land_refs[j], in_bufs[j].at[pl.ds(0, 3)], load_sems.at[2 * j]),
                          pltpu.make_async_copy(pc.half_shard(sum_refs[j], s), in_bufs[j].at[3], load_sems.at[2 * j + 1])))
            sends.append(_remote(fin_bufs[j], recv_bufs[j], send_sems.at[j], recv_sems.at[j], (x, y, 1 - c)))
            mine.append(pltpu.make_async_copy(fin_bufs[j], pc.shard_half(out_refs[pc.src], c), out_sems.at[2 * j]))
            theirs.append(pltpu.make_async_copy(recv_bufs[j], pc.shard_half(out_refs[pc.src], 1 - c), out_sems.at[2 * j + 1]))
        return loads, sends, mine, theirs, in_bufs, fin_bufs

    def begin(ins, outs, scr):
        for a, b in copies(ins, outs, scr)[0]:
            a.start()
            b.start()

    def advance(ins, outs, scr):
        loads, sends, mine, _, in_bufs, fin_bufs = copies(ins, outs, scr)
        for j in range(n):
            loads[j][0].wait()
            loads[j][1].wait()
            acc = in_bufs[j][0].astype(F32)
            for k in range(1, 4):
                acc = acc + in_bufs[j][k].astype(F32)
            fin_bufs[j][...] = acc
            mine[j].start()
            sends[j].start()

    def finish(ins, outs, scr):
        _, sends, mine, theirs, _, _ = copies(ins, outs, scr)
        for j in range(n):
            sends[j].wait_recv()
            theirs[j].start()
        for j in range(n):
            sends[j].wait_send()
            mine[j].wait()
            theirs[j].wait()

    halves = [pc.shard_half_shape for pc in pieces]
    scratch = ([pltpu.VMEM((4,) + sh, BF16) for sh in halves] + [pltpu.VMEM(sh, F32) for sh in halves] * 2
               + [pltpu.SemaphoreType.DMA((2 * n,)), pltpu.SemaphoreType.DMA((n,)), pltpu.SemaphoreType.DMA((n,)),
                  pltpu.SemaphoreType.DMA((2 * n,))])
    return _SideJob(list(chip_sums) + list(lands), [jax.ShapeDtypeStruct(SHARD_OPERAND_SHAPES[o], F32) for o in operands],
                    scratch, {}, begin, advance, finish)


PEER_FLIPS = tuple((a, b, e) for a in (0, 1) for b in (0, 1) for e in (0, 1) if (a, b, e) != (0, 0, 0))


def _peers():
    x, y, c = lax.axis_index("x"), lax.axis_index("y"), lax.axis_index("c")
    me = 4 * x + 2 * y + c
    out = []
    for a, b, e in PEER_FLIPS:
        px, py, pc = (1 - x if a else x), (1 - y if b else y), (1 - c if e else c)
        out.append(((px, py, pc), 4 * px + 2 * py + pc))
    return me, out


def _exchange8_start(name, blk):
    m = blk.shape[0]

    def body(blk_ref, land_ref, blk_out, land_out, sends, recvs, token):
        del blk_out, land_out
        me, peers = _peers()
        for k, (dev, _) in enumerate(peers):
            _remote(blk_ref, land_ref.at[me], sends.at[k], recvs.at[k], dev).start()
        token[...] = jnp.zeros(TOKEN_SHAPE, F32)

    sems = pltpu.SemaphoreType.DMA((7,))
    return _pcall(
        body, name=name,
        in_specs=[HBM, HBM], out_specs=[HBM, HBM, SEM, SEM, pl.BlockSpec(memory_space=pltpu.VMEM)],
        out_shape=[pltpu.HBM((m, LANES), F32), pltpu.HBM((8, m, LANES), F32), sems, sems,
                   jax.ShapeDtypeStruct(TOKEN_SHAPE, F32)],
        input_output_aliases={0: 0, 1: 1}, compiler_params=SPLIT_PARAMS,
    )(_in_hbm(blk), _in_hbm(lax.empty((8, m, LANES), F32)))


def _exchange8_wait(name, blk, land, send_sems, recv_sems, after):
    def body(blk_ref, land_ref, send_ref, recv_ref, after_ref, blk_out, land_out):
        del after_ref, blk_out, land_out
        _, peers = _peers()
        for k, (dev, slot) in enumerate(peers):
            cp = _remote(blk_ref, land_ref.at[slot], send_ref.at[k], recv_ref.at[k], dev)
            cp.wait_send()
            cp.wait_recv()

    m = blk.shape[0]
    return _pcall(
        body, name=name,
        in_specs=[HBM, HBM, SEM, SEM, ANY], out_specs=[HBM, HBM],
        out_shape=[pltpu.HBM((m, LANES), F32), pltpu.HBM((8, m, LANES), F32)],
        input_output_aliases={0: 0, 1: 1}, compiler_params=SPLIT_PARAMS,
    )(blk, land, send_sems, recv_sems, after)


def _collect8(name, blk, land, with_sum):
    m = blk.shape[0]

    def body(blk_ref, land_ref, out_ref, *scratch):
        sems = scratch[-1]
        dst = scratch[0] if with_sum else out_ref
        me, peers = _peers()
        copies = [pltpu.make_async_copy(blk_ref, dst.at[me], sems.at[7])]
        for k, (_, slot) in enumerate(peers):
            copies.append(pltpu.make_async_copy(land_ref.at[slot], dst.at[slot], sems.at[k]))
        for cp in copies:
            cp.start()
        for cp in copies:
            cp.wait()
        if with_sum:
            acc = dst[0]
            for dev in range(1, 8):
                acc = acc + dst[dev]
            out_ref[...] = acc

    all_shape = (8, m, LANES)
    return _pcall(
        body, name=name, in_specs=[ANY, ANY], out_specs=pl.BlockSpec(memory_space=pltpu.VMEM),
        out_shape=jax.ShapeDtypeStruct((m, LANES) if with_sum else all_shape, F32),
        scratch_shapes=([pltpu.VMEM(all_shape, F32)] if with_sum else []) + [pltpu.SemaphoreType.DMA((8,))],
    )(blk, land)


def _pack(arrays, row_counts):
    rows = []
    for a, n in zip(arrays, row_counts):
        flat = a.reshape(-1, LANES)
        rows.append(jnp.pad(flat, ((0, n - flat.shape[0]), (0, 0))))
    return jnp.concatenate(rows, axis=0)


REPL_NAMES = ("norm_mix_g", "norm_ffn_g", "even_conv_b", "even_ln_g", "even_ln_b", "odd_sg_w", "odd_sg_b", "final_g")
REPL_SHAPES = ((2, D), (2, D), (1, 512), (1, 512), (1, 512), (1, SG_GROUPS, CHUNK, CHUNK), (1, SG_GROUPS, CHUNK), (D,))
REPL_ROWS = (16, 16, 8, 8, 8, 512, 8, 8)
SHARDED_NAMES = ("even_conv_k", "odd_conv_k", "odd_ln_g", "odd_ln_b")
SHARDED_SHARD_SHAPES = ((1, CONV_W, LANES), (1, SCONV_W, LANES), (1, LANES), (1, LANES))
SHARDED_SHARD_ROWS = (32, 8, 8, 8)
SHARDED_FULL_SHAPES = ((CONV_W, 512), (SCONV_W, 512), (1, 512), (1, 512))
SHARDED_FULL_ROWS = (128, 16, 8, 8)
SMALL_NAMES = REPL_NAMES + SHARDED_NAMES
SMALL_ROWS = REPL_ROWS + SHARDED_SHARD_ROWS
SMALL_OUT_SHAPES = REPL_SHAPES[:-1] + ((1, D),) + SHARDED_SHARD_SHAPES
LOSS_ROWS = 8


def _offsets(rows):
    out, r0 = [], 0
    for n in rows:
        out.append(r0)
        r0 += n
    return out


def _adamw_small(w_pack, m_pack, v_pack, grad_sum, first_gain_sum):
    n_rows = sum(SMALL_ROWS)
    state_at = _offsets(SMALL_ROWS)
    grad_at = _offsets((LOSS_ROWS, 8) + REPL_ROWS[1:] + SHARDED_FULL_ROWS)[1:]
    c1 = 1.0 - ADAM_B1 ** ADAM_STEP
    c2 = 1.0 - ADAM_B2 ** ADAM_STEP
    n_repl = len(REPL_NAMES)

    def body(w_ref, m_ref, v_ref, g_ref, g0_ref, *rest):
        outs, gbuf = rest[:-1], rest[-1]
        chip = 2 * lax.axis_index("x") + lax.axis_index("y")
        gbuf[...] = jnp.zeros((n_rows, LANES), F32)
        gbuf[0:8, :] = g0_ref[...]
        gbuf[8:16, :] = g_ref[grad_at[0]:grad_at[0] + 8, :]
        for i in range(1, n_repl):
            gbuf[state_at[i]:state_at[i] + REPL_ROWS[i], :] = g_ref[grad_at[i]:grad_at[i] + REPL_ROWS[i], :]
        for k, shape in enumerate(SHARDED_SHARD_SHAPES):
            used = shape[-2] if len(shape) == 3 else 1
            src = pl.ds(grad_at[n_repl + k] + chip, used, stride=4) if used > 1 else pl.ds(grad_at[n_repl + k] + chip, 1)
            gbuf[state_at[n_repl + k]:state_at[n_repl + k] + used, :] = g_ref[src, :]
        g = gbuf[...]
        m_new = ADAM_B1 * m_ref[...] + (1.0 - ADAM_B1) * g
        v_new = ADAM_B2 * v_ref[...] + (1.0 - ADAM_B2) * (g * g)
        delta = -ADAM_LR * ((m_new / c1) / (jnp.sqrt(v_new / c2) + ADAM_EPS) + ADAM_WD * w_ref[...])
        for i, shape in enumerate(SMALL_OUT_SHAPES):
            for j, val in enumerate((g, delta, m_new, v_new)):
                o_ref = outs[4 * i + j]
                rows = val[state_at[i]:state_at[i] + SMALL_ROWS[i], :]
                if len(shape) == 2 and shape[1] > LANES:
                    per = shape[1] // LANES
                    for r in range(shape[0]):
                        for q in range(per):
                            o_ref[r:r + 1, q * LANES:(q + 1) * LANES] = rows[r * per + q:r * per + q + 1, :]
                elif len(shape) == 4:
                    for grp in range(shape[1]):
                        o_ref[0, grp] = rows[grp * shape[2]:(grp + 1) * shape[2], :]
                elif len(shape) == 3:
                    o_ref[0] = rows[:shape[1], :]
                else:
                    o_ref[...] = rows[:1, :]

    vm = pl.BlockSpec(memory_space=pltpu.VMEM)
    out_shape = [jax.ShapeDtypeStruct(sh, F32) for sh in SMALL_OUT_SHAPES for _ in range(4)]
    outs = _pcall(body, name="adamw_small", in_specs=[vm] * 5, out_specs=[vm] * len(out_shape), out_shape=out_shape,
                  scratch_shapes=[pltpu.VMEM((n_rows, LANES), F32)])(w_pack, m_pack, v_pack, grad_sum, first_gain_sum)
    return {n: outs[4 * i:4 * i + 4] for i, n in enumerate(SMALL_NAMES)}


def _touch(arrays):
    n = len(arrays)

    def body(*refs):
        refs[-1][...] = jnp.zeros(TOKEN_SHAPE, F32)

    outs = _pcall(
        body, name="touch", in_specs=[ANY] * n, out_specs=[ANY] * n + [pl.BlockSpec(memory_space=pltpu.VMEM)],
        out_shape=[jax.ShapeDtypeStruct(a.shape, a.dtype) for a in arrays] + [jax.ShapeDtypeStruct(TOKEN_SHAPE, F32)],
        input_output_aliases={i: i for i in range(n)})(*arrays)
    return outs[:n], outs[n]


def kernel(x, norm_mix_g, norm_ffn_g, even_w_in, even_conv_k, even_conv_b, even_ln_g, even_ln_b, even_w_out, odd_w_in, odd_conv_k, odd_ln_g, odd_ln_b, odd_sg_w, odd_sg_b, odd_w_out, ffn_w1, ffn_w2, final_g, loss_target, m_norm_mix_g, m_norm_ffn_g, m_even_w_in, m_even_conv_k, m_even_conv_b, m_even_ln_g, m_even_ln_b, m_even_w_out, m_odd_w_in, m_odd_conv_k, m_odd_ln_g, m_odd_ln_b, m_odd_sg_w, m_odd_sg_b, m_odd_w_out, m_ffn_w1, m_ffn_w2, m_final_g, v_norm_mix_g, v_norm_ffn_g, v_even_w_in, v_even_conv_k, v_even_conv_b, v_even_ln_g, v_even_ln_b, v_even_w_out, v_odd_w_in, v_odd_conv_k, v_odd_ln_g, v_odd_ln_b, v_odd_sg_w, v_odd_sg_b, v_odd_w_out, v_ffn_w1, v_ffn_w2, v_final_g):
    names = ("norm_mix_g", "norm_ffn_g", "even_w_in", "even_conv_k", "even_conv_b", "even_ln_g", "even_ln_b", "even_w_out",
             "odd_w_in", "odd_conv_k", "odd_ln_g", "odd_ln_b", "odd_sg_w", "odd_sg_b", "odd_w_out", "ffn_w1", "ffn_w2", "final_g")
    w = dict(zip(names, (norm_mix_g, norm_ffn_g, even_w_in, even_conv_k, even_conv_b, even_ln_g, even_ln_b, even_w_out,
                         odd_w_in, odd_conv_k, odd_ln_g, odd_ln_b, odd_sg_w, odd_sg_b, odd_w_out, ffn_w1, ffn_w2, final_g)))
    mom = dict(zip(names, (m_norm_mix_g, m_norm_ffn_g, m_even_w_in, m_even_conv_k, m_even_conv_b, m_even_ln_g, m_even_ln_b,
                           m_even_w_out, m_odd_w_in, m_odd_conv_k, m_odd_ln_g, m_odd_ln_b, m_odd_sg_w, m_odd_sg_b, m_odd_w_out,
                           m_ffn_w1, m_ffn_w2, m_final_g)))
    vel = dict(zip(names, (v_norm_mix_g, v_norm_ffn_g, v_even_w_in, v_even_conv_k, v_even_conv_b, v_even_ln_g, v_even_ln_b,
                           v_even_w_out, v_odd_w_in, v_odd_conv_k, v_odd_ln_g, v_odd_ln_b, v_odd_sg_w, v_odd_sg_b, v_odd_w_out,
                           v_ffn_w1, v_ffn_w2, v_final_g)))
    big_names = ("even_w_in", "even_w_out", "odd_w_in", "odd_w_out", "ffn_w1", "ffn_w2")
    chip = 2 * lax.axis_index("x") + lax.axis_index("y")

    def shard2d(t, name):
        return t[name].reshape(SHARD_OPERAND_SHAPES[big_names.index(name)])

    chip_op = jnp.reshape(chip, (1,)).astype(jnp.int32)
    small_pack = _pack([w[n] for n in SHARDED_NAMES], SHARDED_SHARD_ROWS)
    small_blk, small_land, small_send, small_recv, small_token = _exchange8_start("gather_small_start", small_pack)
    first = _cast_place(PIECES[0], shard2d(w, big_names[PIECES[0].src]), chip_op, tie=small_token)
    fly0, send0, recv0, token = _gather_start("gather_start_first", PIECES[:1], [first])
    placed = [_cast_place(pc, shard2d(w, big_names[pc.src]), chip_op, tie=token) for pc in PIECES[1:]]
    fly1, send1, recv1, all_started = _gather_start("gather_start_rest", PIECES[1:], placed)
    flying, gather_send, gather_recv = fly0 + fly1, send0 + send1, recv0 + recv1
    ready = {}

    names_in_order = [pc.name for pc in PIECES]

    riding = {}

    (*state_packs, _), idle_work_done = _touch(
        [_pack([t[n] for n in SMALL_NAMES], SMALL_ROWS) for t in (w, mom, vel)] + [all_started])

    def weight(name, after):
        if name in riding:
            job, k = riding.pop(name)
            ready[name] = job.results[k]
        if name not in ready:
            landed = _gather_wait(PIECES[0], flying[0], gather_send[0], gather_recv[0], idle_work_done)
            ready[name], = _core_forward(PIECES[:1], [landed])
        return ready[name]

    def before(call, after):
        if call in FORWARD_RIDES:
            group = FORWARD_RIDES[call]
            landed = [_gather_wait(PIECES[j], flying[j], gather_send[j], gather_recv[j], after) for j in group]
            job = _core_forward_job([PIECES[j] for j in group], landed)
            riding.update((PIECES[j].name, (job, k)) for k, j in enumerate(group))
            _ride_next_call(job)
        elif call == JOIN_RIDES_IN:
            early_join.append(join_job(JOIN_GROUPS[1], after))
            _ride_next_call(early_join[0])

    early_join = []

    def join_job(operands, after):
        pieces = [pc for pc in PIECES if pc.src in operands]
        done = {}
        for entry in list(scattering):
            if entry[0] in pieces:
                done[entry[0].name] = _scatter_wait(*entry, after)
                scattering.remove(entry)
        return _allsum_join_job(operands, [done[pc.name][0] for pc in pieces], [done[pc.name][1] for pc in pieces])

    scattering = []
    held = []

    core_op = jnp.reshape(lax.axis_index("c"), (1,)).astype(jnp.int32)

    def emit(name, a, b, tie=None):
        pc = PIECES[names_in_order.index(name)]
        held.append((pc, _mm_dw_chipsum(pc, a, b, core_op, tie)))
        if name in HOLD_BACK:
            return None
        pieces = [pc for pc, _ in held]
        started, token = _scatter_start(pieces, [chip_sum for _, chip_sum in held])
        scattering.extend((pc,) + tuple(st) for pc, st in zip(pieces, started))
        held.clear()
        return token

    full = {}
    small_blk, small_land = _exchange8_wait("gather_small_wait", small_blk, small_land, small_send, small_recv, all_started)
    gathered = _collect8("gather_small_collect", small_blk, small_land, False)
    gathered = gathered.reshape(4, 2, sum(SHARDED_SHARD_ROWS), LANES)[:, 0]
    r0 = 0
    for n, sh, rows, full_sh in zip(SHARDED_NAMES, SHARDED_SHARD_SHAPES, SHARDED_SHARD_ROWS, SHARDED_FULL_SHAPES):
        per_chip = gathered[:, r0:r0 + rows].reshape(4, -1)[:, :full_sh[0] * LANES].reshape(4, full_sh[0], LANES)
        full[n] = jnp.transpose(per_chip, (1, 0, 2)).reshape(full_sh)
        r0 += rows
    p = dict(full)
    p.update(norm_mix_g0=norm_mix_g[0:1], norm_mix_g1=norm_mix_g[1:2], norm_ffn_g0=norm_ffn_g[0:1], norm_ffn_g1=norm_ffn_g[1:2],
             even_conv_b=even_conv_b, even_ln_g=even_ln_g, even_ln_b=even_ln_b,
             odd_sg_w=odd_sg_w[0], odd_sg_bt=odd_sg_b[0].T, final_g=final_g[None, :])

    small = {}

    def emit_small(loss_row, g):
        parts = [loss_row, g["norm_mix_g1"], g["norm_ffn_g0"], g["norm_ffn_g1"], g["even_conv_b"], g["even_ln_g"],
                 g["even_ln_b"], g["odd_sg_w"], g["odd_sg_bt"].T, g["final_g"],
                 g["even_conv_k"], g["odd_conv_k"], g["odd_ln_g"], g["odd_ln_b"]]
        pack = _pack(parts, (8, 8, 8, 8) + REPL_ROWS[2:] + SHARDED_FULL_ROWS)
        small["blk"], small["land"], small["send"], small["recv"], token = _exchange8_start("allreduce_small_start", pack)
        return token

    dx, dg0 = _local_step(x[0], loss_target[0], p, weight, emit, emit_small, before)
    last_blk, last_land, last_send, last_recv, grad_token = _exchange8_start("allreduce_last_start", _pack([dg0], (8,)))

    big_grads = dict(zip((big_names[o] for o in JOIN_GROUPS[1]), early_join[0].results))
    delta, new_m, new_v, grads_big = {}, {}, {}, {}

    def adamw_big(n):
        d2, m2, v2, g2 = _adamw(f"adamw_{n}", shard2d(w, n), big_grads[n], shard2d(mom, n), shard2d(vel, n), True,
                                tie=grad_token)
        delta[n], new_m[n], new_v[n], grads_big[n] = (t.reshape(w[n].shape) for t in (d2, m2, v2, g2))

    for o in JOIN_GROUPS[1]:
        adamw_big(big_names[o])
    late_join = join_job(JOIN_GROUPS[0], new_v[big_names[JOIN_GROUPS[1][-1]]])
    big_grads.update(zip((big_names[o] for o in JOIN_GROUPS[0]), _run_job("allsum_join_late", late_join)))
    for o in JOIN_GROUPS[0]:
        adamw_big(big_names[o])

    joined_last = big_grads[big_names[JOIN_GROUPS[0][-1]]]
    grad_blk, grad_land = _exchange8_wait("allreduce_small_wait", small["blk"], small["land"], small["send"],
                                          small["recv"], joined_last)
    grad_sum = _collect8("allreduce_small_sum", grad_blk, grad_land, True)
    last_blk, last_land = _exchange8_wait("allreduce_last_wait", last_blk, last_land, last_send, last_recv, joined_last)
    dg0_sum = _collect8("allreduce_last_sum", last_blk, last_land, True)
    loss = grad_sum[0, 0]

    grads = dict(grads_big)
    for n, results in _adamw_small(*state_packs, grad_sum, dg0_sum).items():
        grads[n], delta[n], new_m[n], new_v[n] = (t.reshape(w[n].shape) for t in results)

    out = [loss, dx[None]]
    for res in (grads, delta, new_m, new_v):
        out.extend(res[n] for n in names)
    return tuple(out)
```

```python
import functools

import jax
import jax.numpy as jnp
from jax import lax
from jax.experimental import pallas as pl
from jax.experimental.pallas import tpu as pltpu

F32 = jnp.float32
BF16 = jnp.bfloat16

T = 2048
D = 1024
CONV_CH = 512
CONV_W = 31
HEAD_DIM = 64
ATT_W = 1536
EVEN_IN = 5632
ODD_IN = 2560
SCONV_W = 3
SG_GROUPS = 4
CHUNK = 128
D_FF = 4096
EPS = 1e-6
DILATIONS = (1, 4, 16)
BAND = 128
SCALE = HEAD_DIM ** -0.5
NEG = -1e30

ADAM_LR = 0.001
ADAM_B1 = 0.9
ADAM_B2 = 0.999
ADAM_EPS = 1e-08
ADAM_WD = 0.01
ADAM_STEP = 10

V7X_VMEM_BYTES = 64 * 2 ** 20
VMEM_LIMIT = V7X_VMEM_BYTES - 8 * 2 ** 20
LANES = 128
TOKEN_SHAPE = (8, LANES)


SIBLING_COLLECTIVE_ID = 0


def _sibling_handshake():
    x, y, c = lax.axis_index("x"), lax.axis_index("y"), lax.axis_index("c")
    barrier = pltpu.get_barrier_semaphore()
    pl.semaphore_signal(barrier, inc=1, device_id=(x, y, 1 - c), device_id_type=pl.DeviceIdType.MESH)
    pl.semaphore_wait(barrier, 1)


class _SideJob:
    def __init__(self, inputs, out_shape, scratch_shapes, aliases, begin, advance, finish):
        self.inputs, self.out_shape, self.scratch_shapes = list(inputs), list(out_shape), list(scratch_shapes)
        self.aliases, self.begin, self.advance, self.finish = dict(aliases), begin, advance, finish
        self.results = None


_PENDING_JOBS = []


def _ride_next_call(job):
    _PENDING_JOBS.append(job)


def _pcall(body, **kw):
    if not _PENDING_JOBS or "grid" not in kw:
        return pl.pallas_call(body, **kw)
    job = _PENDING_JOBS.pop()
    as_list = lambda v: list(v) if isinstance(v, (list, tuple)) else [v]
    single_out = not isinstance(kw["out_shape"], (list, tuple))
    in_specs, out_specs, out_shape = as_list(kw["in_specs"]), as_list(kw["out_specs"]), as_list(kw["out_shape"])
    scratch = list(kw.get("scratch_shapes", ()))
    grid = kw["grid"]
    n_steps = 1
    for extent in grid:
        n_steps *= extent
    assert n_steps >= 3
    n_in, n_out, n_scr = len(in_specs), len(out_specs), len(scratch)
    j_in, j_out = len(job.inputs), len(job.out_shape)
    any_spec = pl.BlockSpec(memory_space=pl.ANY)

    def hosted(*refs):
        ins, j_ins = refs[:n_in], refs[n_in:n_in + j_in]
        outs = refs[n_in + j_in:n_in + j_in + n_out]
        j_outs = refs[n_in + j_in + n_out:n_in + j_in + n_out + j_out]
        scr = refs[n_in + j_in + n_out + j_out:n_in + j_in + n_out + j_out + n_scr]
        j_scr = refs[n_in + j_in + n_out + j_out + n_scr:]
        step = pl.program_id(0)
        for axis in range(1, len(grid)):
            step = step * grid[axis] + pl.program_id(axis)

        @pl.when(step == 0)
        def _():
            _sibling_handshake()
            job.begin(j_ins, j_outs, j_scr)

        @pl.when(step == 1)
        def _():
            job.advance(j_ins, j_outs, j_scr)

        body(*ins, *outs, *scr)

        @pl.when(step == n_steps - 1)
        def _():
            job.finish(j_ins, j_outs, j_scr)

    aliases = dict(kw.get("input_output_aliases", {}))
    aliases.update({n_in + a: n_out + b for a, b in job.aliases.items()})
    call = pl.pallas_call(
        hosted, name=kw["name"], grid=grid,
        in_specs=in_specs + [any_spec] * j_in, out_specs=out_specs + [any_spec] * j_out,
        out_shape=out_shape + job.out_shape, scratch_shapes=scratch + job.scratch_shapes,
        input_output_aliases=aliases,
        compiler_params=pltpu.CompilerParams(dimension_semantics=("arbitrary",) * len(grid), vmem_limit_bytes=VMEM_LIMIT,
                                             collective_id=SIBLING_COLLECTIVE_ID))

    def run(*args):
        res = call(*args, *job.inputs)
        job.results = list(res[n_out:])
        return res[0] if single_out else list(res[:n_out])

    return run


def _params(*sem):
    return pltpu.CompilerParams(dimension_semantics=sem, vmem_limit_bytes=VMEM_LIMIT)


def _dot(a, b, dims):
    return lax.dot_general(a, b, (dims, ((), ())), preferred_element_type=F32)


def _nn(a, b):
    return _dot(a, b, ((1,), (0,)))


def _nt(a, b):
    return _dot(a, b, ((1,), (1,)))


def _tn(a, b):
    return _dot(a, b, ((0,), (0,)))


def _sigmoid(x):
    return 1.0 / (1.0 + jnp.exp(-x))


MM_VMEM_BUDGET = 40 * 2 ** 20


def _mm_tiles(mode, m, n, k, a_bytes, b_bytes, extra_bytes, out_bytes):
    def divisors(total, unit):
        return [t for t in range(unit, total + 1, unit) if total % t == 0]

    best = None
    for tm in divisors(m, LANES if mode == "tn" else 8):
        for tn in divisors(n, LANES):
            blocks = tm * k * a_bytes + tn * k * b_bytes + tm * tn * (extra_bytes + out_bytes)
            casts = (tm * k * 2 if a_bytes == 4 else 0) + (tn * k * 2 if b_bytes == 4 else 0)
            if 2 * blocks + casts + tm * tn * 4 > MM_VMEM_BUDGET:
                continue
            key = ((m // tm) * (n // tn), (m // tm) * n * k * b_bytes, abs(tm - tn))
            if best is None or key < best[0]:
                best = (key, tm, tn)
    return best[1], best[2]


def _mm(name, mode, a, b, m, n, k, out_dtypes, *, b_off=0, extras=(), epi=None, tie=None):
    tm, tn = _mm_tiles(mode, m, n, k, a.dtype.itemsize, b.dtype.itemsize, sum(e.dtype.itemsize for e in extras),
                       sum(jnp.dtype(dt).itemsize for dt in out_dtypes))
    assert b_off % tn == 0
    b_off //= tn
    if mode == "nn":
        a_spec = pl.BlockSpec((tm, k), lambda i, j: (i, 0))
        b_spec = pl.BlockSpec((k, tn), lambda i, j: (0, j + b_off))
        dims = ((1,), (0,))
    elif mode == "nt":
        a_spec = pl.BlockSpec((tm, k), lambda i, j: (i, 0))
        b_spec = pl.BlockSpec((tn, k), lambda i, j: (j, 0))
        dims = ((1,), (1,))
    else:
        a_spec = pl.BlockSpec((k, tm), lambda i, j: (0, i))
        b_spec = pl.BlockSpec((k, tn), lambda i, j: (0, j))
        dims = ((0,), (0,))
    o_spec = pl.BlockSpec((tm, tn), lambda i, j: (i, j))
    n_extra = len(extras)
    ties = () if tie is None else (tie,)

    def body(a_ref, b_ref, *rest):
        rest = rest[len(ties):]
        acc = _dot(a_ref[...].astype(BF16), b_ref[...].astype(BF16), dims)
        vals = epi(acc, *[e[...] for e in rest[:n_extra]]) if epi is not None else (acc,)
        for o_ref, v in zip(rest[n_extra:], vals):
            o_ref[...] = v.astype(o_ref.dtype)

    outs = _pcall(
        body, name=name, grid=(m // tm, n // tn),
        in_specs=[a_spec, b_spec] + [pl.BlockSpec(TOKEN_SHAPE, lambda i, j: (0, 0))] * len(ties) + [o_spec] * n_extra,
        out_specs=[o_spec] * len(out_dtypes),
        out_shape=[jax.ShapeDtypeStruct((m, n), dt) for dt in out_dtypes],
        compiler_params=_params("parallel", "parallel"),
    )(a, b, *ties, *extras)
    return outs[0] if len(out_dtypes) == 1 else outs


def _row_tile(k, a_bytes, n_row_blocks):
    for tm in (1024, 512, 256, 128):
        if 2 * (tm * k * a_bytes + n_row_blocks * tm * D * 4) + D * k * 2 + tm * D * 4 <= MM_VMEM_BUDGET + 4 * 2 ** 20:
            return tm
    raise ValueError("no row tile fits")


def _resident(shape):
    return pl.BlockSpec(shape, lambda i: (0, 0), pipeline_mode=pl.Buffered(1))


FFN0_DOWN_TILE = 512


def _mm_out_norm(name, a, b, k, res, g_next, tm=None):
    tm = tm or _row_tile(k, a.dtype.itemsize, 3)

    def body(a_ref, b_ref, r_ref, g_ref, h_ref, hn_ref):
        h = _nn(a_ref[...].astype(BF16), b_ref[...]) + r_ref[...]
        h_ref[...] = h
        r = lax.rsqrt(jnp.mean(h * h, axis=-1, keepdims=True) + EPS)
        hn_ref[...] = ((h * r) * g_ref[...]).astype(BF16)

    row = pl.BlockSpec((tm, D), lambda i: (i, 0))
    return _pcall(
        body, name=name, grid=(T // tm,),
        in_specs=[pl.BlockSpec((tm, k), lambda i: (i, 0)), _resident((k, D)), row,
                  pl.BlockSpec((1, D), lambda i: (0, 0))],
        out_specs=[row, row],
        out_shape=[jax.ShapeDtypeStruct((T, D), F32), jax.ShapeDtypeStruct((T, D), BF16)],
        compiler_params=_params("parallel"),
    )(a, b, res, g_next)


def _ffn_last(name, hn, w1, w2, res, g, target):
    tm = FFN_BWD_TILE

    def body(a_ref, w1_ref, w2_ref, r_ref, g_ref, t_ref, f_ref, dh_ref, dg_ref, loss_ref):
        u = jnp.maximum(_nn(a_ref[...], w1_ref[...]), 0.0)
        f = (u * u).astype(BF16)
        f_ref[...] = f
        x = _nn(f, w2_ref[...]) + r_ref[...]
        r = lax.rsqrt(jnp.mean(x * x, axis=-1, keepdims=True) + EPS)
        nrm = x * r
        gain = g_ref[...]
        err = nrm * gain - t_ref[...]
        dy = err * (1.0 / D)
        dn = dy * gain
        dh_ref[...] = r * (dn - nrm * jnp.mean(dn * nrm, axis=-1, keepdims=True))

        @pl.when(pl.program_id(0) == 0)
        def _():
            dg_ref[...] = jnp.zeros_like(dg_ref)
            loss_ref[...] = jnp.zeros_like(loss_ref)

        dg_ref[...] += jnp.sum(dy * nrm, axis=0, keepdims=True)
        part = jnp.sum(jnp.sum(err * err, axis=1, keepdims=True), axis=0, keepdims=True) * (0.5 / D)
        loss_ref[...] += jnp.broadcast_to(part, (1, LANES))

    row = pl.BlockSpec((tm, D), lambda i: (i, 0))
    wide = pl.BlockSpec((tm, D_FF), lambda i: (i, 0))
    vec = pl.BlockSpec((1, D), lambda i: (0, 0))
    return _pcall(
        body, name=name, grid=(T // tm,),
        in_specs=[row, _resident((D, D_FF)), _resident((D_FF, D)), row, vec, row],
        out_specs=[wide, row, vec, pl.BlockSpec((1, LANES), lambda i: (0, 0))],
        out_shape=[jax.ShapeDtypeStruct((T, D_FF), BF16), jax.ShapeDtypeStruct((T, D), F32),
                   jax.ShapeDtypeStruct((1, D), F32), jax.ShapeDtypeStruct((1, LANES), F32)],
        compiler_params=_params("arbitrary"),
    )(hn, w1, w2, res, g, target)


def _mm_dx_norm(name, dz, w, k, h, g, dres, tie=None):
    tm = _row_tile(k, dz.dtype.itemsize, 3)
    ties = () if tie is None else (tie,)

    def body(a_ref, b_ref, *rest):
        h_ref, g_ref, r_ref, dh_ref, dg_ref = rest[len(ties):]
        dy = _nt(a_ref[...].astype(BF16), b_ref[...])
        x = h_ref[...]
        r = lax.rsqrt(jnp.mean(x * x, axis=-1, keepdims=True) + EPS)
        nrm = x * r
        dn = dy * g_ref[...]
        dh_ref[...] = r_ref[...] + r * (dn - nrm * jnp.mean(dn * nrm, axis=-1, keepdims=True))

        @pl.when(pl.program_id(0) == 0)
        def _():
            dg_ref[...] = jnp.zeros_like(dg_ref)

        dg_ref[...] += jnp.sum(dy * nrm, axis=0, keepdims=True)

    row = pl.BlockSpec((tm, D), lambda i: (i, 0))
    vec = pl.BlockSpec((1, D), lambda i: (0, 0))
    return _pcall(
        body, name=name, grid=(T // tm,),
        in_specs=[pl.BlockSpec((tm, k), lambda i: (i, 0)), _resident((D, k))]
        + [pl.BlockSpec(TOKEN_SHAPE, lambda i: (0, 0))] * len(ties) + [row, vec, row],
        out_specs=[row, vec],
        out_shape=[jax.ShapeDtypeStruct((T, D), F32), jax.ShapeDtypeStruct((1, D), F32)],
        compiler_params=_params("arbitrary"),
    )(dz, w, *ties, h, g, dres)


def _rms_fwd(name, h, g, tm=512, tie=None):
    ties = () if tie is None else (tie,)

    def body(h_ref, g_ref, *rest):
        x = h_ref[...]
        r = lax.rsqrt(jnp.mean(x * x, axis=-1, keepdims=True) + EPS)
        rest[-1][...] = ((x * r) * g_ref[...]).astype(BF16)

    return _pcall(
        body, name=name, grid=(T // tm,),
        in_specs=[pl.BlockSpec((tm, D), lambda i: (i, 0)), pl.BlockSpec((1, D), lambda i: (0, 0))]
        + [pl.BlockSpec(TOKEN_SHAPE, lambda i: (0, 0))] * len(ties),
        out_specs=pl.BlockSpec((tm, D), lambda i: (i, 0)),
        out_shape=jax.ShapeDtypeStruct((T, D), BF16),
        compiler_params=_params("parallel"),
    )(h, g, *ties)


CONV_TILE = 256
CONV_HALO = 32


def _glu(z):
    return z[:, :CONV_CH] * _sigmoid(z[:, CONV_CH:])


SUBLANES = 8


def _sublane_shifts(win):
    n = win.shape[0]
    return [win] + [win[r:r + n - SUBLANES, :] for r in range(1, SUBLANES)]


def _rows_from(shifts, off, n):
    q, r = divmod(off, SUBLANES)
    return shifts[r][q * SUBLANES:q * SUBLANES + n, :]


def _econv_fwd(zc, conv_k, conv_b, ln_g, ln_b):
    R, H = CONV_TILE, CONV_HALO

    def body(z_ref, zh_ref, k_ref, b_ref, g_ref, be_ref, cv_ref, cat_ref):
        i = pl.program_id(0)
        glu = _glu(z_ref[...])
        halo = _glu(zh_ref[...]) * (i > 0).astype(F32)
        win = _sublane_shifts(jnp.concatenate([halo, glu], axis=0))
        acc = jnp.zeros((R, CONV_CH), F32) + b_ref[...]
        for j in range(CONV_W):
            acc = acc + k_ref[j:j + 1, :] * _rows_from(win, H - (CONV_W - 1) + j, R)
        cv_ref[...] = acc
        mu = jnp.mean(acc, axis=-1, keepdims=True)
        xc = acc - mu
        rstd = lax.rsqrt(jnp.mean(xc * xc, axis=-1, keepdims=True) + EPS)
        ln = xc * rstd * g_ref[...] + be_ref[...]
        cat_ref[...] = (ln * _sigmoid(ln)).astype(BF16)

    vec = pl.BlockSpec((1, CONV_CH), lambda i: (0, 0))
    return _pcall(
        body, name="econv_fwd", grid=(T // R,),
        in_specs=[pl.BlockSpec((R, 2 * CONV_CH), lambda i: (i, 0)),
                  pl.BlockSpec((H, 2 * CONV_CH), lambda i: (jnp.maximum(i * (R // H) - 1, 0), 0)),
                  pl.BlockSpec((CONV_W, CONV_CH), lambda i: (0, 0)), vec, vec, vec],
        out_specs=[pl.BlockSpec((R, CONV_CH), lambda i: (i, 0)), pl.BlockSpec((R, CONV_CH), lambda i: (i, 0))],
        out_shape=[jax.ShapeDtypeStruct((T, CONV_CH), F32), jax.ShapeDtypeStruct((T, D), BF16)],
        compiler_params=_params("parallel"),
    )(zc, zc, conv_k, conv_b, ln_g, ln_b)


def _econv_bwd_ln(cv, dcat, ln_g, ln_b):
    R = CONV_TILE

    def body(cv_ref, d_ref, g_ref, be_ref, dcv_ref, dg_ref, dbe_ref, dcb_ref):
        cv_t = cv_ref[...]
        mu = jnp.mean(cv_t, axis=-1, keepdims=True)
        xc = cv_t - mu
        rstd = lax.rsqrt(jnp.mean(xc * xc, axis=-1, keepdims=True) + EPS)
        xh = xc * rstd
        ln = xh * g_ref[...] + be_ref[...]
        sg = _sigmoid(ln)
        dln = d_ref[...] * (sg * (1.0 + ln * (1.0 - sg)))
        dxh = dln * g_ref[...]
        dcv = rstd * (dxh - jnp.mean(dxh, axis=-1, keepdims=True) - xh * jnp.mean(dxh * xh, axis=-1, keepdims=True))
        dcv_ref[...] = dcv

        @pl.when(pl.program_id(0) == 0)
        def _():
            dg_ref[...] = jnp.zeros_like(dg_ref)
            dbe_ref[...] = jnp.zeros_like(dbe_ref)
            dcb_ref[...] = jnp.zeros_like(dcb_ref)

        dg_ref[...] += jnp.sum(dln * xh, axis=0, keepdims=True)
        dbe_ref[...] += jnp.sum(dln, axis=0, keepdims=True)
        dcb_ref[...] += jnp.sum(dcv, axis=0, keepdims=True)

    vec = pl.BlockSpec((1, CONV_CH), lambda i: (0, 0))
    row = pl.BlockSpec((R, CONV_CH), lambda i: (i, 0))
    vshape = jax.ShapeDtypeStruct((1, CONV_CH), F32)
    return _pcall(
        body, name="econv_bwd_ln", grid=(T // R,),
        in_specs=[row, row, vec, vec], out_specs=[row, vec, vec, vec],
        out_shape=[jax.ShapeDtypeStruct((T, CONV_CH), F32), vshape, vshape, vshape],
        compiler_params=_params("arbitrary"),
    )(cv, dcat, ln_g, ln_b)


def _econv_bwd_conv(dcv, zc, conv_k):
    R, H = CONV_TILE, CONV_HALO
    last = T // R - 1

    def body(d_ref, dn_ref, z_ref, zh_ref, k_ref, dz_ref, dk_ref):
        i = pl.program_id(0)
        z = z_ref[...]
        a_lin = z[:, :CONV_CH]
        sg = _sigmoid(z[:, CONV_CH:])
        glu = a_lin * sg
        halo = _glu(zh_ref[...]) * (i > 0).astype(F32)
        win = _sublane_shifts(jnp.concatenate([halo, glu], axis=0))
        dcv_t = d_ref[...]
        nxt = dn_ref[...] * (i < last).astype(F32)
        winb = _sublane_shifts(jnp.concatenate([dcv_t, nxt], axis=0))

        @pl.when(i == 0)
        def _():
            dk_ref[...] = jnp.zeros_like(dk_ref)

        dglu = jnp.zeros((R, CONV_CH), F32)
        for j in range(CONV_W):
            dk_ref[j:j + 1, :] += jnp.sum(dcv_t * _rows_from(win, H - (CONV_W - 1) + j, R), axis=0, keepdims=True)
            dglu = dglu + k_ref[j:j + 1, :] * _rows_from(winb, CONV_W - 1 - j, R)
        dz_ref[...] = jnp.concatenate([dglu * sg, dglu * a_lin * sg * (1.0 - sg)], axis=1).astype(BF16)

    return _pcall(
        body, name="econv_bwd_conv", grid=(T // R,),
        in_specs=[pl.BlockSpec((R, CONV_CH), lambda i: (i, 0)),
                  pl.BlockSpec((H, CONV_CH), lambda i: (jnp.minimum((i + 1) * (R // H), T // H - 1), 0)),
                  pl.BlockSpec((R, 2 * CONV_CH), lambda i: (i, 0)),
                  pl.BlockSpec((H, 2 * CONV_CH), lambda i: (jnp.maximum(i * (R // H) - 1, 0), 0)),
                  pl.BlockSpec((CONV_W, CONV_CH), lambda i: (0, 0))],
        out_specs=[pl.BlockSpec((R, 2 * CONV_CH), lambda i: (i, 0)), pl.BlockSpec((CONV_W, CONV_CH), lambda i: (0, 0))],
        out_shape=[jax.ShapeDtypeStruct((T, EVEN_IN), BF16), jax.ShapeDtypeStruct((CONV_W, CONV_CH), F32)],
        compiler_params=_params("arbitrary"),
    )(dcv, dcv, zc, zc, conv_k)


def _swap_halves(v):
    lane = lax.broadcasted_iota(jnp.int32, (1, v.shape[1]), 1)
    return jnp.where((lane % HEAD_DIM) < HEAD_DIM // 2, pltpu.roll(v, LANES - HEAD_DIM // 2, 1),
                     pltpu.roll(v, HEAD_DIM // 2, 1))


def _qkv_proj(hn, w_in, rope_c, rope_s, tm=T):
    tn = 4 * LANES

    def body(a_ref, b_ref, c_ref, s_ref, o_ref):
        j = pl.program_id(1)
        acc = _nn(a_ref[...], b_ref[...])
        for p in range(4):
            v = acc[:, p * LANES:(p + 1) * LANES]
            rot = v * c_ref[...] + _swap_halves(v) * s_ref[...]
            o_ref[p] = jnp.where(j < 6, rot, v)

    tab = pl.BlockSpec((tm, LANES), lambda i, j: (i, 0))
    return _pcall(
        body, name="qkv_proj", grid=(T // tm, 9),
        in_specs=[pl.BlockSpec((tm, D), lambda i, j: (i, 0)),
                  pl.BlockSpec((D, tn), lambda i, j: (0, j + (2 * CONV_CH) // tn)), tab, tab],
        out_specs=pl.BlockSpec((None, 4, tm, LANES), lambda i, j: (j, 0, i, 0)),
        out_shape=jax.ShapeDtypeStruct((9, 4, T, LANES), F32),
        compiler_params=_params("parallel", "parallel"),
    )(hn, w_in, rope_c, rope_s)


ATTN_FWD_UNROLL = 4
ATTN_BWD_UNROLL = 4


def _band_rows(start, d):
    if d == 1:
        return pl.ds(pl.multiple_of(start, BAND), BAND)
    return pl.ds(start, BAND, stride=d)


def _band_masks(n):
    row = lax.broadcasted_iota(jnp.int32, (BAND, BAND), 0)
    col = lax.broadcasted_iota(jnp.int32, (BAND, BAND), 1)
    no_prev = (n == 0).astype(jnp.int32) * (2 * BAND)
    return col <= row, col >= row + no_prev


def _attn_fwd(qkv, g):
    d = DILATIONS[g]
    nb = T // d // BAND
    has_prev = nb > 1

    def body(q_ref, k_ref, v_ref, o_ref, l_ref):
        lane_lo = lax.broadcasted_iota(jnp.int32, (BAND, LANES), 1) < HEAD_DIM

        heads = (lane_lo, jnp.logical_not(lane_lo))
        ones = jnp.ones((BAND, LANES), BF16)

        def step(it, carry):
            tiles = []
            for u in range(ATTN_FWD_UNROLL):
                idx = it * ATTN_FWD_UNROLL + u
                r = idx // nb
                n = idx % nb
                cur = _band_rows(n * (BAND * d) + r, d)
                prev = _band_rows(jnp.maximum(n - 1, 0) * (BAND * d) + r, d)
                mc, mp = _band_masks(n)
                kp = k_ref[prev, :].astype(BF16) if has_prev else None
                vp = v_ref[prev, :].astype(BF16) if has_prev else None
                tiles.append((cur, mc, mp, q_ref[cur, :], k_ref[cur, :].astype(BF16), v_ref[cur, :].astype(BF16), kp, vp))
            scores = []
            for cur, mc, mp, q, kc, vc, kp, vp in tiles:
                for hm in heads:
                    qm = jnp.where(hm, q, 0.0).astype(BF16)
                    sc = jnp.where(mc, _nt(qm, kc) * SCALE, NEG)
                    scores.append((sc, jnp.where(mp, _nt(qm, kp) * SCALE, NEG)) if has_prev else (sc,))
            maxes = [functools.reduce(jnp.maximum, [jnp.max(sx, axis=1, keepdims=True) for sx in ss]) for ss in scores]
            probs = [[jnp.exp(sx - mx).astype(BF16) for sx in ss] for ss, mx in zip(scores, maxes)]
            dens = [functools.reduce(jnp.add, [_nn(px, ones) for px in ps]) for ps in probs]
            for t, (cur, mc, mp, q, kc, vc, kp, vp) in enumerate(tiles):
                outs, lses = [], []
                for h in range(2):
                    ps = probs[2 * t + h]
                    acc = _nn(ps[0], vc) + _nn(ps[1], vp) if has_prev else _nn(ps[0], vc)
                    outs.append(acc / dens[2 * t + h])
                    lses.append(maxes[2 * t + h] + jnp.log(dens[2 * t + h]))
                o_ref[cur, :] = jnp.where(lane_lo, outs[0], outs[1])
                l_ref[cur, :] = jnp.where(lane_lo, lses[0], lses[1])
            return carry

        lax.fori_loop(0, d * nb // ATTN_FWD_UNROLL, step, 0)

    def slab(which):
        return pl.BlockSpec((None, None, T, LANES), lambda p: (which * 3 + g, p, 0, 0))

    out = pl.BlockSpec((None, T, LANES), lambda p: (p, 0, 0))
    shape = jax.ShapeDtypeStruct((4, T, LANES), F32)
    return _pcall(
        body, name=f"attn_fwd{g}", grid=(4,),
        in_specs=[slab(0), slab(1), slab(2)], out_specs=[out, out], out_shape=[shape, shape],
        compiler_params=_params("parallel"),
    )(qkv, qkv, qkv)


def _attn_merge(outs, lses, cat, tm=1024):
    def body(o0, o1, o2, l0, l1, l2, cat_in, cat_ref, att_ref, w0, w1, w2):
        del cat_in
        la, lb, lc = l0[...], l1[...], l2[...]
        mx = jnp.maximum(jnp.maximum(la, lb), lc)
        ea, eb, ec = jnp.exp(la - mx), jnp.exp(lb - mx), jnp.exp(lc - mx)
        inv = 1.0 / (ea + eb + ec)
        wa, wb, wc = ea * inv, eb * inv, ec * inv
        att = wa * o0[...] + wb * o1[...] + wc * o2[...]
        att_ref[...] = att
        cat_ref[...] = att.astype(BF16)
        w0[...] = wa
        w1[...] = wb
        w2[...] = wc

    slab = pl.BlockSpec((None, tm, LANES), lambda p, i: (p, i, 0))
    shape = jax.ShapeDtypeStruct((4, T, LANES), F32)
    return _pcall(
        body, name="attn_merge", grid=(4, T // tm),
        in_specs=[slab] * 6 + [pl.BlockSpec(memory_space=pl.ANY)],
        out_specs=[pl.BlockSpec((tm, LANES), lambda p, i: (i, CONV_CH // LANES + p)), slab, slab, slab, slab],
        out_shape=[jax.ShapeDtypeStruct((T, D), BF16), shape, shape, shape, shape],
        input_output_aliases={6: 0},
        compiler_params=_params("parallel", "parallel"),
    )(*outs, *lses, cat)


def _attn_bwd(qkv, lse, wgt, att, dcat, dqkv, g):
    d = DILATIONS[g]
    nb = T // d // BAND
    has_prev = nb > 1

    def body(q_ref, k_ref, v_ref, l_ref, w_ref, a_ref, da_ref, dq_in, o_ref):
        del dq_in
        lane = lax.broadcasted_iota(jnp.int32, (BAND, LANES), 1)
        lane_lo = lane < HEAD_DIM
        row = lax.broadcasted_iota(jnp.int32, (LANES, LANES), 0)
        same_head = ((row // HEAD_DIM) == (lane // HEAD_DIM)).astype(BF16)
        dq_ref, dk_ref, dv_ref = o_ref.at[0], o_ref.at[1], o_ref.at[2]
        if has_prev:
            dk_ref[...] = jnp.zeros((T, LANES), F32)
            dv_ref[...] = jnp.zeros((T, LANES), F32)

        heads = (lane_lo, jnp.logical_not(lane_lo))

        def step(it, carry):
            tiles = []
            for u in range(ATTN_BWD_UNROLL):
                idx = it * ATTN_BWD_UNROLL + u
                r = idx // nb
                n = idx % nb
                cur = _band_rows(n * (BAND * d) + r, d)
                prev = _band_rows(jnp.maximum(n - 1, 0) * (BAND * d) + r, d)
                mc, mp = _band_masks(n)
                da = da_ref[cur, :]
                prod = da * a_ref[cur, :]
                hi = prod.astype(BF16)
                lo = (prod - hi.astype(F32)).astype(BF16)
                tiles.append(dict(cur=cur, prev=prev, mc=mc, mp=mp, da=da, hi=hi, lo=lo, q=q_ref[cur, :],
                                  kc=k_ref[cur, :].astype(BF16), vc=v_ref[cur, :].astype(BF16),
                                  kp=k_ref[prev, :].astype(BF16) if has_prev else None,
                                  vp=v_ref[prev, :].astype(BF16) if has_prev else None,
                                  lse=l_ref[cur, :], w=w_ref[cur, :]))
            for t in tiles:
                t["csum"] = _nn(t["hi"], same_head) + _nn(t["lo"], same_head)
            chains = []
            for t in tiles:
                for h, hm in enumerate(heads):
                    qm = jnp.where(hm, t["q"], 0.0).astype(BF16)
                    dam = jnp.where(hm, t["da"], 0.0).astype(BF16)
                    ch = dict(t=t, h=h, qm=qm, dam=dam, sc=jnp.where(t["mc"], _nt(qm, t["kc"]) * SCALE, NEG),
                              dpc=_nt(dam, t["vc"]))
                    if has_prev:
                        ch.update(sp=jnp.where(t["mp"], _nt(qm, t["kp"]) * SCALE, NEG), dpp=_nt(dam, t["vp"]))
                    chains.append(ch)
            for ch in chains:
                t, col0 = ch["t"], ch["h"] * HEAD_DIM
                lse_h = t["lse"][:, col0:col0 + 1]
                w_h = t["w"][:, col0:col0 + 1]
                c_h = t["csum"][:, col0:col0 + 1]
                pwc = w_h * jnp.exp(ch["sc"] - lse_h)
                ch["dsc"] = (pwc * (ch["dpc"] - c_h) * SCALE).astype(BF16)
                ch["pwc"] = pwc.astype(BF16)
                if has_prev:
                    pwp = w_h * jnp.exp(ch["sp"] - lse_h)
                    ch["dsp"] = (pwp * (ch["dpp"] - c_h) * SCALE).astype(BF16)
                    ch["pwp"] = pwp.astype(BF16)
            for ch in chains:
                t = ch["t"]
                ch["dq"] = _nn(ch["dsc"], t["kc"])
                ch["dkc"] = _tn(ch["dsc"], ch["qm"])
                ch["dvc"] = _tn(ch["pwc"], ch["dam"])
                if has_prev:
                    ch["dq"] = ch["dq"] + _nn(ch["dsp"], t["kp"])
                    ch["dkp"] = _tn(ch["dsp"], ch["qm"])
                    ch["dvp"] = _tn(ch["pwp"], ch["dam"])
            for i, t in enumerate(tiles):
                c0, c1 = chains[2 * i], chains[2 * i + 1]
                dq_ref[t["cur"], :] = jnp.where(lane_lo, c0["dq"], c1["dq"])
                if has_prev:
                    dk_ref[t["cur"], :] += c0["dkc"] + c1["dkc"]
                    dk_ref[t["prev"], :] += c0["dkp"] + c1["dkp"]
                    dv_ref[t["cur"], :] += c0["dvc"] + c1["dvc"]
                    dv_ref[t["prev"], :] += c0["dvp"] + c1["dvp"]
                else:
                    dk_ref[t["cur"], :] = c0["dkc"] + c1["dkc"]
                    dv_ref[t["cur"], :] = c0["dvc"] + c1["dvc"]
            return carry

        lax.fori_loop(0, d * nb // ATTN_BWD_UNROLL, step, 0)

    def slab(which):
        return pl.BlockSpec((None, None, T, LANES), lambda p: (which * 3 + g, p, 0, 0))

    per_pair = pl.BlockSpec((None, T, LANES), lambda p: (p, 0, 0))
    return _pcall(
        body, name=f"attn_bwd{g}", grid=(4,),
        in_specs=[slab(0), slab(1), slab(2), per_pair, per_pair, per_pair,
                  pl.BlockSpec((T, LANES), lambda p: (0, CONV_CH // LANES + p)),
                  pl.BlockSpec(memory_space=pl.ANY)],
        out_specs=pl.BlockSpec((None, 3, None, T, LANES), lambda p: (g, 0, p, 0, 0)),
        out_shape=jax.ShapeDtypeStruct((3, 3, 4, T, LANES), F32),
        input_output_aliases={7: 0},
        compiler_params=_params("parallel"),
    )(qkv, qkv, qkv, lse, wgt, att, dcat, dqkv)


def _rope_bwd(dqkv, rope_c, rope_s, dz):
    wide = 4 * LANES

    def body(d_ref, c_ref, s_ref, dz_in, o_ref):
        del dz_in
        w = pl.program_id(1)
        for p in range(4):
            v = d_ref[p]
            rot = v * c_ref[...] + _swap_halves(v * s_ref[...])
            o_ref[:, p * LANES:(p + 1) * LANES] = jnp.where(w < 2, rot, v).astype(BF16)

    tab = pl.BlockSpec((T, LANES), lambda g, w: (0, 0))
    return _pcall(
        body, name="rope_bwd", grid=(3, 3),
        in_specs=[pl.BlockSpec((None, None, 4, T, LANES), lambda g, w: (g, w, 0, 0, 0)), tab, tab,
                  pl.BlockSpec(memory_space=pl.ANY)],
        out_specs=pl.BlockSpec((T, wide), lambda g, w: (0, (2 * CONV_CH) // wide + w * 3 + g)),
        out_shape=jax.ShapeDtypeStruct((T, EVEN_IN), BF16),
        input_output_aliases={3: 0},
        compiler_params=_params("parallel", "parallel"),
    )(dqkv, rope_c, rope_s, dz)


ODD_TILE = 256
ODD_HALO = 8
GELU_C = 0.7978845608028654
GELU_A = 0.044715


def _gelu(x):
    return 0.5 * x * (1.0 + jnp.tanh(GELU_C * (x + GELU_A * x * x * x)))


def _gelu_grad(x):
    th = jnp.tanh(GELU_C * (x + GELU_A * x * x * x))
    return 0.5 * (1.0 + th) + 0.5 * x * (1.0 - th * th) * GELU_C * (1.0 + 3.0 * GELU_A * x * x)


def _tril():
    row = lax.broadcasted_iota(jnp.int32, (CHUNK, CHUNK), 0)
    col = lax.broadcasted_iota(jnp.int32, (CHUNK, CHUNK), 1)
    return (col <= row).astype(F32)


def _odd_parts(z, zh, i, k_ref, g_ref, be_ref, w_ref, bt_ref):
    R, H = ODD_TILE, ODD_HALO
    gb, gc, xs, uv = z[:, :512], z[:, 512:1024], z[:, 1024:1536], z[:, 1536:]
    halo = zh[:, 512:1024] * zh[:, 1024:1536] * (i > 0).astype(F32)
    win = jnp.concatenate([halo, gc * xs], axis=0)
    cv = jnp.zeros((R, 512), F32)
    for j in range(SCONV_W):
        off = H - (SCONV_W - 1) + j
        cv = cv + k_ref[j:j + 1, :] * win[off:off + R, :]
    ge = _gelu(uv)
    u, v = ge[:, :512], ge[:, 512:]
    mu = jnp.mean(v, axis=-1, keepdims=True)
    xc = v - mu
    rstd = lax.rsqrt(jnp.mean(xc * xc, axis=-1, keepdims=True) + EPS)
    xh = xc * rstd
    vn = xh * g_ref[...] + be_ref[...]
    tril = _tril()
    wms = [(w_ref[g] * tril).astype(BF16) for g in range(SG_GROUPS)]
    rows = []
    for ci in range(R // CHUNK):
        blocks = []
        for g in range(SG_GROUPS):
            blk = vn[ci * CHUNK:(ci + 1) * CHUNK, g * LANES:(g + 1) * LANES].astype(BF16)
            blocks.append(_nn(wms[g], blk) + bt_ref[:, g:g + 1])
        rows.append(jnp.concatenate(blocks, axis=1))
    vmix = jnp.concatenate(rows, axis=0)
    return gb, gc, xs, uv, win, cv, u, rstd, xh, vn, vmix, wms


def _odd_mid_fwd(z, conv_k, ln_g, ln_b, sg_w, sg_bt):
    R, H = ODD_TILE, ODD_HALO

    def body(z_ref, zh_ref, k_ref, g_ref, be_ref, w_ref, bt_ref, o_ref):
        i = pl.program_id(0)
        gb, _, _, _, _, cv, u, _, _, _, vmix, _ = _odd_parts(z_ref[...], zh_ref[...], i, k_ref, g_ref, be_ref, w_ref, bt_ref)
        o_ref[...] = jnp.concatenate([gb * cv, u * vmix], axis=1).astype(BF16)

    vec = pl.BlockSpec((1, 512), lambda i: (0, 0))
    return _pcall(
        body, name="odd_mid_fwd", grid=(T // R,),
        in_specs=[pl.BlockSpec((R, ODD_IN), lambda i: (i, 0)),
                  pl.BlockSpec((H, ODD_IN), lambda i: (jnp.maximum(i * (R // H) - 1, 0), 0)),
                  pl.BlockSpec((SCONV_W, 512), lambda i: (0, 0)), vec, vec,
                  pl.BlockSpec((SG_GROUPS, CHUNK, CHUNK), lambda i: (0, 0, 0)),
                  pl.BlockSpec((CHUNK, SG_GROUPS), lambda i: (0, 0))],
        out_specs=pl.BlockSpec((R, D), lambda i: (i, 0)),
        out_shape=jax.ShapeDtypeStruct((T, D), BF16),
        compiler_params=_params("parallel"),
    )(z, z, conv_k, ln_g, ln_b, sg_w, sg_bt)


def _odd_mid_bwd(z, dcat, conv_k, ln_g, ln_b, sg_w, sg_bt):
    R, H = ODD_TILE, ODD_HALO
    last = T // R - 1

    def body(z_ref, zh_ref, zn_ref, d_ref, dn_ref, k_ref, g_ref, be_ref, w_ref, bt_ref,
             dz_ref, dk_ref, dg_ref, dbe_ref, dw_ref, dbt_ref):
        i = pl.program_id(0)
        z = z_ref[...]
        gb, gc, xs, uv, win, cv, u, rstd, xh, vn, vmix, wms = _odd_parts(z, zh_ref[...], i, k_ref, g_ref, be_ref, w_ref, bt_ref)
        dcat_t = d_ref[...]
        dc, dd = dcat_t[:, :512], dcat_t[:, 512:]

        @pl.when(i == 0)
        def _():
            dk_ref[...] = jnp.zeros_like(dk_ref)
            dg_ref[...] = jnp.zeros_like(dg_ref)
            dbe_ref[...] = jnp.zeros_like(dbe_ref)
            dw_ref[...] = jnp.zeros_like(dw_ref)
            dbt_ref[...] = jnp.zeros_like(dbt_ref)

        dgb = dc * cv
        dcv = dc * gb
        nxt = dn_ref[:, :512] * zn_ref[:, :512] * (i < last).astype(F32)
        winb = jnp.concatenate([dcv, nxt], axis=0)
        dp = jnp.zeros((R, 512), F32)
        for j in range(SCONV_W):
            off = H - (SCONV_W - 1) + j
            dk_ref[j:j + 1, :] += jnp.sum(dcv * win[off:off + R, :], axis=0, keepdims=True)
            ob = SCONV_W - 1 - j
            dp = dp + k_ref[j:j + 1, :] * winb[ob:ob + R, :]
        dgc = dp * xs
        dxs = dp * gc
        du = dd * vmix
        dvmix = dd * u
        tril = _tril()
        rows = []
        for ci in range(R // CHUNK):
            blocks = []
            for g in range(SG_GROUPS):
                sl = (slice(ci * CHUNK, (ci + 1) * CHUNK), slice(g * LANES, (g + 1) * LANES))
                dblk = dvmix[sl]
                dblk16 = dblk.astype(BF16)
                blocks.append(_tn(wms[g], dblk16))
                dw_ref[g] += _nt(dblk16, vn[sl].astype(BF16)) * tril
                dbt_ref[:, g:g + 1] += jnp.sum(dblk, axis=1, keepdims=True)
            rows.append(jnp.concatenate(blocks, axis=1))
        dvn = jnp.concatenate(rows, axis=0)
        dg_ref[...] += jnp.sum(dvn * xh, axis=0, keepdims=True)
        dbe_ref[...] += jnp.sum(dvn, axis=0, keepdims=True)
        dxh = dvn * g_ref[...]
        dv = rstd * (dxh - jnp.mean(dxh, axis=-1, keepdims=True) - xh * jnp.mean(dxh * xh, axis=-1, keepdims=True))
        duv = jnp.concatenate([du, dv], axis=1) * _gelu_grad(uv)
        dz_ref[...] = jnp.concatenate([dgb, dgc, dxs, duv], axis=1).astype(BF16)

    vec = pl.BlockSpec((1, 512), lambda i: (0, 0))
    kspec = pl.BlockSpec((SCONV_W, 512), lambda i: (0, 0))
    wspec = pl.BlockSpec((SG_GROUPS, CHUNK, CHUNK), lambda i: (0, 0, 0))
    bspec = pl.BlockSpec((CHUNK, SG_GROUPS), lambda i: (0, 0))
    nxt_blk = lambda i: (jnp.minimum((i + 1) * (R // H), T // H - 1), 0)
    return _pcall(
        body, name="odd_mid_bwd", grid=(T // R,),
        in_specs=[pl.BlockSpec((R, ODD_IN), lambda i: (i, 0)),
                  pl.BlockSpec((H, ODD_IN), lambda i: (jnp.maximum(i * (R // H) - 1, 0), 0)),
                  pl.BlockSpec((H, ODD_IN), nxt_blk),
                  pl.BlockSpec((R, D), lambda i: (i, 0)),
                  pl.BlockSpec((H, D), nxt_blk),
                  kspec, vec, vec, wspec, bspec],
        out_specs=[pl.BlockSpec((R, ODD_IN), lambda i: (i, 0)), kspec, vec, vec, wspec, bspec],
        out_shape=[jax.ShapeDtypeStruct((T, ODD_IN), BF16), jax.ShapeDtypeStruct((SCONV_W, 512), F32),
                   jax.ShapeDtypeStruct((1, 512), F32), jax.ShapeDtypeStruct((1, 512), F32),
                   jax.ShapeDtypeStruct((SG_GROUPS, CHUNK, CHUNK), F32), jax.ShapeDtypeStruct((CHUNK, SG_GROUPS), F32)],
        compiler_params=_params("arbitrary"),
    )(z, z, z, dcat, dcat, conv_k, ln_g, ln_b, sg_w, sg_bt)


def _ffn_up(tag, hn, weight):
    def act(acc):
        r = jnp.maximum(acc, 0.0)
        return (r * r,)

    return _mm(f"ffn{tag}_up", "nn", hn, weight(f"ffn_w1_{tag}", hn), T, D_FF, D, (BF16,), epi=act)


FFN_BWD_TILE = 256


def _ffn_dx(name, dout, w2, w1, f, h, g, tie=None):
    tm = FFN_BWD_TILE
    ties = () if tie is None else (tie,)

    def body(d_ref, w2_ref, w1_ref, f_ref, h_ref, g_ref, *rest):
        du_ref, dh_ref, dg_ref = rest[len(ties):]
        dres = d_ref[...]
        du = (_nt(dres.astype(BF16), w2_ref[...]) * (2.0 * jnp.sqrt(f_ref[...].astype(F32)))).astype(BF16)
        du_ref[...] = du
        dy = _nt(du, w1_ref[...])
        x = h_ref[...]
        r = lax.rsqrt(jnp.mean(x * x, axis=-1, keepdims=True) + EPS)
        nrm = x * r
        dn = dy * g_ref[...]
        dh_ref[...] = dres + r * (dn - nrm * jnp.mean(dn * nrm, axis=-1, keepdims=True))

        @pl.when(pl.program_id(0) == 0)
        def _():
            dg_ref[...] = jnp.zeros_like(dg_ref)

        dg_ref[...] += jnp.sum(dy * nrm, axis=0, keepdims=True)

    row = pl.BlockSpec((tm, D), lambda i: (i, 0))
    wide = pl.BlockSpec((tm, D_FF), lambda i: (i, 0))
    vec = pl.BlockSpec((1, D), lambda i: (0, 0))
    return _pcall(
        body, name=name, grid=(T // tm,),
        in_specs=[row, _resident((D_FF, D)), _resident((D, D_FF)), wide, row, vec]
        + [pl.BlockSpec(TOKEN_SHAPE, lambda i: (0, 0))] * len(ties),
        out_specs=[wide, row, vec],
        out_shape=[jax.ShapeDtypeStruct((T, D_FF), BF16), jax.ShapeDtypeStruct((T, D), F32),
                   jax.ShapeDtypeStruct((1, D), F32)],
        compiler_params=_params("arbitrary"),
    )(dout, w2, w1, f, h, g, *ties)


def _ffn_bwd(tag, h, g, weight, emit, saved, dout, tie=None):
    hn, f = saved
    du, dh, dg = _ffn_dx(f"ffn{tag}_dx", dout, weight(f"ffn_w2_{tag}", dout), weight(f"ffn_w1_{tag}", dout), f, h, g, tie)
    emit(f"ffn_w2_{tag}", f, dout)
    return dh, dg, emit(f"ffn_w1_{tag}", hn, du)


def _rope_tables():
    half = HEAD_DIM // 2
    inv = 10000.0 ** (-jnp.arange(half, dtype=F32) / half)
    ang = jnp.arange(T, dtype=F32)[:, None] * inv[None, :]
    cos, sin = jnp.cos(ang), jnp.sin(ang)
    c = jnp.tile(jnp.concatenate([cos, cos], axis=1), (1, LANES // HEAD_DIM))
    s = jnp.tile(jnp.concatenate([-sin, sin], axis=1), (1, LANES // HEAD_DIM))
    return c, s


def _local_step(x, target, p, weight, emit, emit_small, before=lambda name, after: None):
    rope_c, rope_s = _rope_tables()
    grads = {}

    hn0 = _rms_fwd("mix0_norm", x, p["norm_mix_g0"], tie=p.get("first_norm_after"))
    zc = _mm("even_in_conv", "nn", hn0, weight("even_w_in", hn0), T, 2 * CONV_CH, D, (F32,))
    qkv = _qkv_proj(hn0, weight("even_w_in", hn0), rope_c, rope_s)
    cv, cat0 = _econv_fwd(zc, p["even_conv_k"], p["even_conv_b"], p["even_ln_g"], p["even_ln_b"])
    att_parts = [_attn_fwd(qkv, 0)]
    before("attn_fwd1", att_parts[0][0])
    att_parts += [_attn_fwd(qkv, 1), _attn_fwd(qkv, 2)]
    outs = [a[0] for a in att_parts]
    lses = [a[1] for a in att_parts]
    cat0, att, w0, w1, w2 = _attn_merge(outs, lses, cat0)
    wgts = (w0, w1, w2)
    h1, hnf0 = _mm_out_norm("even_out", cat0, weight("even_w_out", cat0), D, x, p["norm_ffn_g0"])
    f0 = _ffn_up(0, hnf0, weight)
    before("ffn0_down", f0)
    h2, hn1 = _mm_out_norm("ffn0_down", f0, weight("ffn_w2_0", f0), D_FF, h1, p["norm_mix_g1"], tm=FFN0_DOWN_TILE)

    z1 = _mm("odd_in", "nn", hn1, weight("odd_w_in", hn1), T, ODD_IN, D, (F32,))
    before("odd_mid_fwd", z1)
    cat1 = _odd_mid_fwd(z1, p["odd_conv_k"], p["odd_ln_g"], p["odd_ln_b"], p["odd_sg_w"], p["odd_sg_bt"])
    h3, hnf1 = _mm_out_norm("odd_out", cat1, weight("odd_w_out", cat1), D, h2, p["norm_ffn_g1"])
    f1, dh4, grads["final_g"], loss = _ffn_last("ffn1_loss", hnf1, weight("ffn_w1_1", hnf1), weight("ffn_w2_1", hnf1),
                                                h3, p["final_g"], target)

    dh3, grads["norm_ffn_g1"], tok = _ffn_bwd(1, h3, p["norm_ffn_g1"], weight, emit, (hnf1, f1), dh4)
    tok = emit("odd_w_out", cat1, dh3, tie=tok)
    dcat1 = _mm("odd_out_dx", "nt", dh3, weight("odd_w_out", dh3), T, D, D, (F32,), tie=tok)
    dz1, grads["odd_conv_k"], grads["odd_ln_g"], grads["odd_ln_b"], grads["odd_sg_w"], grads["odd_sg_bt"] = _odd_mid_bwd(
        z1, dcat1, p["odd_conv_k"], p["odd_ln_g"], p["odd_ln_b"], p["odd_sg_w"], p["odd_sg_bt"])
    tok = emit("odd_w_in", hn1, dz1)
    dh2, grads["norm_mix_g1"] = _mm_dx_norm("odd_in_dx", dz1, weight("odd_w_in", dz1), ODD_IN, h2, p["norm_mix_g1"],
                                            dh3, tie=tok)

    dh1, grads["norm_ffn_g0"], tok = _ffn_bwd(0, h1, p["norm_ffn_g0"], weight, emit, (hnf0, f0), dh2)
    tok = emit("even_w_out", cat0, dh1, tie=tok)
    dcat0 = _mm("even_out_dx", "nt", dh1, weight("even_w_out", dh1), T, D, D, (F32,), tie=tok)
    dcv, grads["even_ln_g"], grads["even_ln_b"], grads["even_conv_b"] = _econv_bwd_ln(
        cv, dcat0, p["even_ln_g"], p["even_ln_b"])
    dz0, grads["even_conv_k"] = _econv_bwd_conv(dcv, zc, p["even_conv_k"])
    tok = emit_small(loss, grads)
    dqkv = lax.empty((3, 3, 4, T, LANES), F32)
    for g in range(3):
        dqkv = _attn_bwd(qkv, lses[g], wgts[g], att, dcat0, dqkv, g)
    before("rope_bwd", dqkv)
    dz0 = _rope_bwd(dqkv, rope_c, rope_s, dz0)
    tok = emit("even_w_in", hn0, dz0, tie=tok)
    dx, dg0 = _mm_dx_norm("even_in_dx", dz0, weight("even_w_in", dz0), EVEN_IN, x, p["norm_mix_g0"], dh1, tie=tok)
    return dx, dg0


def _rowwise(name, fn, ins, out_dtypes, tm=256, tie=None):
    rows, cols = ins[0].shape
    tm = tm if rows % tm == 0 else rows
    n_in = len(ins)
    ties = () if tie is None else (tie,)

    def body(*refs):
        vals = fn(*[r[...] for r in refs[:n_in]])
        for o_ref, v in zip(refs[n_in + len(ties):], vals):
            o_ref[...] = v.astype(o_ref.dtype)

    spec = pl.BlockSpec((tm, cols), lambda i: (i, 0))
    outs = _pcall(
        body, name=name, grid=(rows // tm,),
        in_specs=[spec] * n_in + [pl.BlockSpec(TOKEN_SHAPE, lambda i: (0, 0))] * len(ties),
        out_specs=[spec] * len(out_dtypes),
        out_shape=[jax.ShapeDtypeStruct((rows, cols), dt) for dt in out_dtypes],
        compiler_params=_params("parallel"),
    )(*ins, *ties)
    return outs[0] if len(out_dtypes) == 1 else outs


def _adamw(name, w, g, m, v, with_grad=False, tie=None):
    c1 = 1.0 - ADAM_B1 ** ADAM_STEP
    c2 = 1.0 - ADAM_B2 ** ADAM_STEP

    def fn(w_t, g_t, m_t, v_t):
        m_new = ADAM_B1 * m_t + (1.0 - ADAM_B1) * g_t
        v_new = ADAM_B2 * v_t + (1.0 - ADAM_B2) * (g_t * g_t)
        delta = -ADAM_LR * ((m_new / c1) / (jnp.sqrt(v_new / c2) + ADAM_EPS) + ADAM_WD * w_t)
        return (delta, m_new, v_new, g_t) if with_grad else (delta, m_new, v_new)

    return _rowwise(name, fn, (w, g, m, v), (F32,) * (4 if with_grad else 3), tie=tie)


class _Piece:
    def __init__(self, name, rows, cols, axis, src, src_row0):
        self.name, self.rows, self.cols, self.axis = name, rows, cols, axis
        self.width = (cols if axis == 1 else rows) // 4
        self.src, self.src_row0 = src, src_row0

    @property
    def full_shape(self):
        return (self.rows, self.cols)

    @property
    def half_shape(self):
        return (self.rows // 2, self.cols) if self.axis == 1 else (self.rows, self.cols // 2)

    @property
    def shard_half_shape(self):
        return (self.rows // 2, self.width) if self.axis == 1 else (self.width, self.cols // 2)

    def shard_whole(self, ref):
        n = self.rows if self.axis == 1 else self.width
        return ref.at[pl.ds(self.src_row0, n), :]

    def shard_half(self, ref, h):
        if self.axis == 1:
            return ref.at[pl.ds(self.src_row0 + h * (self.rows // 2), self.rows // 2), :]
        return ref.at[pl.ds(self.src_row0, self.width), pl.ds(h * (self.cols // 2), self.cols // 2)]

    def full_shard(self, ref, s):
        if self.axis == 1:
            return ref.at[:, pl.ds(s * self.width, self.width)]
        return ref.at[pl.ds(s * self.width, self.width), :]

    def full_shard_half(self, ref, s, h):
        if self.axis == 1:
            return ref.at[pl.ds(h * (self.rows // 2), self.rows // 2), pl.ds(s * self.width, self.width)]
        return ref.at[pl.ds(s * self.width, self.width), pl.ds(h * (self.cols // 2), self.cols // 2)]

    def full_half(self, ref, h):
        if self.axis == 1:
            return ref.at[pl.ds(h * (self.rows // 2), self.rows // 2), :]
        return ref.at[:, pl.ds(h * (self.cols // 2), self.cols // 2)]

    def full_half_rows(self, ref, h, r0, n):
        if self.axis == 1:
            return ref.at[pl.ds(h * (self.rows // 2) + r0, n), :]
        return ref.at[pl.ds(r0, n), pl.ds(h * (self.cols // 2), self.cols // 2)]

    def half_shard(self, ref, s):
        return self.full_shard(ref, s)


PIECES = (
    _Piece("even_w_in", D, EVEN_IN, 1, 0, 0),
    _Piece("even_w_out", D, D, 0, 1, 0),
    _Piece("ffn_w1_0", D, D_FF, 1, 4, 0),
    _Piece("ffn_w2_0", D_FF, D, 0, 5, 0),
    _Piece("odd_w_in", D, ODD_IN, 1, 2, 0),
    _Piece("odd_w_out", D, D, 0, 3, 0),
    _Piece("ffn_w1_1", D, D_FF, 1, 4, D),
    _Piece("ffn_w2_1", D_FF, D, 0, 5, D_FF // 4),
)
N_PIECES = len(PIECES)
FORWARD_RIDES = {"attn_fwd1": (1, 2, 3), "ffn0_down": (4, 5), "odd_mid_fwd": (6, 7)}
JOIN_GROUPS = ((0, 1, 2, 3), (4, 5))
JOIN_RIDES_IN = "rope_bwd"
HOLD_BACK = ("ffn_w2_0", "ffn_w2_1", "odd_w_out")
N_SHARD_OPERANDS = 6
ANY = pl.BlockSpec(memory_space=pl.ANY)
MESH = pl.DeviceIdType.MESH


def _mesh_place():
    x, y, c = lax.axis_index("x"), lax.axis_index("y"), lax.axis_index("c")
    chips = [(1 - x, y), (x, 1 - y), (1 - x, 1 - y)]
    return x, y, c, chips


def _remote(src, dst, send_sem, recv_sem, dev):
    return pltpu.make_async_remote_copy(src_ref=src, dst_ref=dst, send_sem=send_sem, recv_sem=recv_sem,
                                        device_id=dev, device_id_type=MESH)


HBM = pl.BlockSpec(memory_space=pltpu.HBM)
SEM = pl.BlockSpec(memory_space=pltpu.SEMAPHORE)
SPLIT_PARAMS = pltpu.CompilerParams(has_side_effects=pltpu.SideEffectType.DATAFLOW_SIDE_EFFECTING)
CAST_TILE = 256


def _in_hbm(a):
    return pltpu.with_memory_space_constraint(a, pltpu.HBM)


def _cast_place(pc, shard_operand, chip, tie=None):
    rows, cols = (pc.rows, pc.width) if pc.axis == 1 else (pc.width, pc.cols)
    nblk = rows // CAST_TILE
    blk0 = pc.src_row0 // CAST_TILE
    ties = () if tie is None else (tie,)

    def body(chip_ref, x_ref, *rest):
        del chip_ref
        rest[-1][...] = x_ref[...].astype(BF16)

    if pc.axis == 1:
        out_map = lambda i, chip_ref: (i, chip_ref[0])
    else:
        out_map = lambda i, chip_ref: (chip_ref[0] * nblk + i, 0)
    return _pcall(
        body, name=f"cast_{pc.name}",
        grid_spec=pltpu.PrefetchScalarGridSpec(
            num_scalar_prefetch=1, grid=(nblk,),
            in_specs=[pl.BlockSpec((CAST_TILE, cols), lambda i, chip_ref: (blk0 + i, 0))]
            + [pl.BlockSpec(TOKEN_SHAPE, lambda i, chip_ref: (0, 0))] * len(ties),
            out_specs=pl.BlockSpec((CAST_TILE, cols), out_map)),
        out_shape=jax.ShapeDtypeStruct(pc.full_shape, BF16),
        compiler_params=_params("parallel"),
    )(chip, shard_operand, *ties)


def _gather_start(name, pieces, fulls):
    n = len(pieces)

    def body(*refs):
        ins = refs[:n]
        sends = refs[2 * n:3 * n]
        recvs = refs[3 * n:4 * n]
        token = refs[4 * n]
        x, y, c, chips = _mesh_place()
        s = 2 * x + y
        for i, pc in enumerate(pieces):
            win = pc.full_shard_half(ins[i], s, c)
            for k, (cx, cy) in enumerate(chips):
                _remote(win, win, sends[i].at[k], recvs[i].at[k], (cx, cy, c)).start()
        token[...] = jnp.zeros(TOKEN_SHAPE, F32)

    sems = [pltpu.SemaphoreType.DMA((3,))] * (2 * n)
    outs = _pcall(
        body, name=name,
        in_specs=[HBM] * n,
        out_specs=[HBM] * n + [SEM] * (2 * n) + [pl.BlockSpec(memory_space=pltpu.VMEM)],
        out_shape=[pltpu.HBM(pc.full_shape, BF16) for pc in pieces] + sems + [jax.ShapeDtypeStruct(TOKEN_SHAPE, F32)],
        input_output_aliases={i: i for i in range(n)},
        compiler_params=SPLIT_PARAMS,
    )(*[_in_hbm(f) for f in fulls])
    return outs[:n], outs[n:2 * n], outs[2 * n:3 * n], outs[3 * n]


def _gather_wait(pc, full, send_sems, recv_sems, after):
    def body(full_ref, send_ref, recv_ref, after_ref, out_ref):
        del after_ref, out_ref
        x, y, c, chips = _mesh_place()
        for k, (cx, cy) in enumerate(chips):
            win = pc.full_shard_half(full_ref, 2 * cx + cy, c)
            cp = _remote(win, win, send_ref.at[k], recv_ref.at[k], (cx, cy, c))
            cp.wait_send()
            cp.wait_recv()

    return _pcall(
        body, name=f"gather_wait_{pc.name}",
        in_specs=[HBM, SEM, SEM, ANY], out_specs=HBM, out_shape=pltpu.HBM(pc.full_shape, BF16),
        input_output_aliases={0: 0}, compiler_params=SPLIT_PARAMS,
    )(full, send_sems, recv_sems, after)


def _core_forward_job(pieces, fulls):
    n = len(pieces)

    def copies(ins, outs, scr):
        send_bufs, recv_bufs = scr[:n], scr[n:2 * n]
        load_sems, send_sems, recv_sems, store_sems = scr[2 * n:]
        x, y, c, chips = _mesh_place()
        loads, sends, stores = [], [], []
        for i, pc in enumerate(pieces):
            for k, (cx, cy) in enumerate(chips):
                j = 3 * i + k
                loads.append(pltpu.make_async_copy(pc.full_shard_half(ins[i], 2 * cx + cy, c), send_bufs[i].at[k],
                                                   load_sems.at[j]))
                sends.append(_remote(send_bufs[i].at[k], recv_bufs[i].at[k], send_sems.at[j], recv_sems.at[j],
                                     (x, y, 1 - c)))
                stores.append(pltpu.make_async_copy(recv_bufs[i].at[k], pc.full_shard_half(outs[i], 2 * cx + cy, 1 - c),
                                                    store_sems.at[j]))
        return loads, sends, stores

    def begin(ins, outs, scr):
        for cp in copies(ins, outs, scr)[0]:
            cp.start()

    def advance(ins, outs, scr):
        loads, sends, _ = copies(ins, outs, scr)
        for load, send in zip(loads, sends):
            load.wait()
            send.start()

    def finish(ins, outs, scr):
        _, sends, stores = copies(ins, outs, scr)
        for send, store in zip(sends, stores):
            send.wait_recv()
            store.start()
        for send, store in zip(sends, stores):
            send.wait_send()
            store.wait()

    sems = pltpu.SemaphoreType.DMA((3 * n,))
    bufs = [pltpu.VMEM((3,) + pc.shard_half_shape, BF16) for pc in pieces]
    return _SideJob(fulls, [jax.ShapeDtypeStruct(pc.full_shape, BF16) for pc in pieces],
                    bufs + bufs + [sems, sems, sems, sems], {i: i for i in range(n)}, begin, advance, finish)


def _run_job(name, job):
    def body(o_ref):
        o_ref[...] = jnp.zeros(TOKEN_SHAPE, F32)

    _ride_next_call(job)
    _pcall(body, name=name, grid=(3,), in_specs=[], out_specs=pl.BlockSpec(TOKEN_SHAPE, lambda i: (0, 0)),
           out_shape=jax.ShapeDtypeStruct(TOKEN_SHAPE, F32))()
    return job.results


def _core_forward(pieces, fulls):
    n = len(pieces)

    def body(*refs):
        ins, outs = refs[:n], refs[n:2 * n]
        send_bufs, recv_bufs = refs[2 * n:3 * n], refs[3 * n:4 * n]
        load_sems, send_sems, recv_sems, store_sems = refs[4 * n:]
        x, y, c, chips = _mesh_place()
        loads, sends, stores = [], [], []
        for i, pc in enumerate(pieces):
            for k, (cx, cy) in enumerate(chips):
                cp = pltpu.make_async_copy(pc.full_shard_half(ins[i], 2 * cx + cy, c), send_bufs[i].at[k],
                                           load_sems.at[3 * i + k])
                cp.start()
                loads.append(cp)
        _sibling_handshake()
        for i in range(n):
            for k in range(3):
                j = 3 * i + k
                loads[j].wait()
                cp = _remote(send_bufs[i].at[k], recv_bufs[i].at[k], send_sems.at[j], recv_sems.at[j], (x, y, 1 - c))
                cp.start()
                sends.append(cp)
        for i, pc in enumerate(pieces):
            for k, (cx, cy) in enumerate(chips):
                j = 3 * i + k
                sends[j].wait_recv()
                cp = pltpu.make_async_copy(recv_bufs[i].at[k], pc.full_shard_half(outs[i], 2 * cx + cy, 1 - c),
                                           store_sems.at[j])
                cp.start()
                stores.append(cp)
        for j in range(3 * n):
            sends[j].wait_send()
            stores[j].wait()

    sems = pltpu.SemaphoreType.DMA((3 * n,))
    bufs = [pltpu.VMEM((3,) + pc.shard_half_shape, BF16) for pc in pieces]
    return _pcall(
        body, name="core_forward_" + pieces[0].name, in_specs=[ANY] * n, out_specs=[ANY] * n,
        out_shape=[jax.ShapeDtypeStruct(pc.full_shape, BF16) for pc in pieces],
        scratch_shapes=bufs + bufs + [sems, sems, sems, sems],
        input_output_aliases={i: i for i in range(n)},
        compiler_params=pltpu.CompilerParams(vmem_limit_bytes=VMEM_LIMIT, collective_id=SIBLING_COLLECTIVE_ID),
    )(*fulls)


def _dw_tile(pc):
    if pc.axis == 1:
        tn = max(t for t in range(LANES, pc.cols + 1, LANES) if pc.cols % t == 0 and t <= 1408)
        return pc.rows // 2, tn
    return min(pc.rows, 1024), pc.cols // 2


def _mm_dw_chipsum(pc, a, b, core, tie=None):
    tm, tn = _dw_tile(pc)
    hr, hc = pc.half_shape
    tiles_r, tiles_c = hr // tm, hc // tn
    th = tiles_r * tiles_c
    ties = () if tie is None else (tie,)

    def tile_of(s, core_ref):
        mine = s >= th
        half = jnp.where(mine, core_ref[0], 1 - core_ref[0])
        local = s % th
        li, lj = local // tiles_c, local % tiles_c
        if pc.axis == 1:
            return half * tiles_r + li, lj, li, lj, mine
        return li, half * tiles_c + lj, li, lj, mine

    def body(core_ref, a_ref, b_ref, *rest):
        o_ref, send_buf, recv_buf, send_sems, recv_sems = rest[len(ties):]
        s = pl.program_id(0)
        local = s % th
        x, y, c = lax.axis_index("x"), lax.axis_index("y"), lax.axis_index("c")
        acc = _tn(a_ref[...].astype(BF16), b_ref[...].astype(BF16))

        def push(slot):
            return _remote(send_buf.at[slot], recv_buf.at[slot], send_sems.at[slot], recv_sems.at[slot], (x, y, 1 - c))

        @pl.when(s == 0)
        def _():
            _sibling_handshake()

        @pl.when(s < th)
        def _():
            send_buf[local] = acc.astype(BF16)
            push(local).start()

        @pl.when(s >= th)
        def _():
            push(local).wait_recv()
            o_ref[...] = (acc + recv_buf[local].astype(F32)).astype(BF16)

        @pl.when(s == 2 * th - 1)
        def _():
            for slot in range(th):
                push(slot).wait_send()

    def a_map(s, core_ref):
        return 0, tile_of(s, core_ref)[0]

    def b_map(s, core_ref):
        return 0, tile_of(s, core_ref)[1]

    def o_map(s, core_ref):
        _, _, li, lj, mine = tile_of(s, core_ref)
        return jnp.where(mine, li, 0), jnp.where(mine, lj, 0)

    return _pcall(
        body, name=f"dw_{pc.name}",
        grid_spec=pltpu.PrefetchScalarGridSpec(
            num_scalar_prefetch=1, grid=(2 * th,),
            in_specs=[pl.BlockSpec((T, tm), a_map), pl.BlockSpec((T, tn), b_map)]
            + [pl.BlockSpec(TOKEN_SHAPE, lambda s, core_ref: (0, 0))] * len(ties),
            out_specs=pl.BlockSpec((tm, tn), o_map),
            scratch_shapes=[pltpu.VMEM((th, tm, tn), BF16), pltpu.VMEM((th, tm, tn), BF16),
                            pltpu.SemaphoreType.DMA((th,)), pltpu.SemaphoreType.DMA((th,))]),
        out_shape=jax.ShapeDtypeStruct((hr, hc), BF16),
        compiler_params=pltpu.CompilerParams(dimension_semantics=("arbitrary",), vmem_limit_bytes=VMEM_LIMIT,
                                             collective_id=SIBLING_COLLECTIVE_ID),
    )(core, a, b, *ties)


def _scatter_start(pieces, chip_sums):
    n = len(pieces)

    def body(*refs):
        sums, lands = refs[:n], refs[n:2 * n]
        sends, recvs = refs[4 * n:5 * n], refs[5 * n:6 * n]
        token = refs[6 * n]
        x, y, c, chips = _mesh_place()
        for i, pc in enumerate(pieces):
            for k, (cx, cy) in enumerate(chips):
                _remote(pc.half_shard(sums[i], 2 * cx + cy), lands[i].at[k], sends[i].at[k], recvs[i].at[k],
                        (cx, cy, c)).start()
        token[...] = jnp.zeros(TOKEN_SHAPE, F32)

    land_shapes = [(3,) + pc.shard_half_shape for pc in pieces]
    sems = [pltpu.SemaphoreType.DMA((3,))] * (2 * n)
    outs = _pcall(
        body, name="scatter_start_" + pieces[0].name,
        in_specs=[HBM] * (2 * n), out_specs=[HBM] * (2 * n) + [SEM] * (2 * n) + [pl.BlockSpec(memory_space=pltpu.VMEM)],
        out_shape=[pltpu.HBM(pc.half_shape, BF16) for pc in pieces] + [pltpu.HBM(sh, BF16) for sh in land_shapes]
        + sems + [jax.ShapeDtypeStruct(TOKEN_SHAPE, F32)],
        input_output_aliases={i: i for i in range(2 * n)}, compiler_params=SPLIT_PARAMS,
    )(*[_in_hbm(cs) for cs in chip_sums], *[_in_hbm(lax.empty(sh, BF16)) for sh in land_shapes])
    return [(outs[i], outs[n + i], outs[2 * n + i], outs[3 * n + i]) for i in range(n)], outs[4 * n]


def _scatter_wait(pc, chip_sum, land, send_sems, recv_sems, after):
    def body(sum_ref, land_ref, send_ref, recv_ref, after_ref, sum_out, land_out):
        del after_ref, sum_out, land_out
        x, y, c, chips = _mesh_place()
        for k, (cx, cy) in enumerate(chips):
            cp = _remote(pc.half_shard(sum_ref, 2 * cx + cy), land_ref.at[k], send_ref.at[k], recv_ref.at[k], (cx, cy, c))
            cp.wait_send()
            cp.wait_recv()

    return _pcall(
        body, name=f"scatter_wait_{pc.name}",
        in_specs=[HBM, HBM, SEM, SEM, ANY], out_specs=[HBM, HBM],
        out_shape=[pltpu.HBM(pc.half_shape, BF16), pltpu.HBM((3,) + pc.shard_half_shape, BF16)],
        input_output_aliases={0: 0, 1: 1}, compiler_params=SPLIT_PARAMS,
    )(chip_sum, land, send_sems, recv_sems, after)


SHARD_OPERAND_SHAPES = ((D, EVEN_IN // 4), (D // 4, D), (D, ODD_IN // 4), (D // 4, D), (2 * D, D_FF // 4), (2 * D_FF // 4, D))


def _allsum_join_job(operands, chip_sums, lands):
    pieces = [pc for pc in PIECES if pc.src in operands]
    n = len(pieces)

    def copies(ins, outs, scr):
        sum_refs, land_refs = ins[:n], ins[n:]
        out_refs = dict(zip(operands, outs))
        in_bufs, fin_bufs, recv_bufs = scr[:n], scr[n:2 * n], scr[2 * n:3 * n]
        load_sems, send_sems, recv_sems, out_sems = scr[3 * n:]
        x, y, c, _ = _mesh_place()
        s = 2 * x + y
        loads, sends, mine, theirs = [], [], [], []
        for j, pc in enumerate(pieces):
            loads.append((pltpu.make_async_copy(land_refs[j], in_bufs[j].at[pl.ds(0, 3)], load_sems.at[2 * j]),
                          pltpu.make_async_copy(pc.half_shard(sum_refs[j], s), in_bufs[j].at[3], load_sems.at[2 * j + 1])))
            sends.append(_remote(fin_bufs[j], recv_bufs[j], send_sems.at[j], recv_sems.at[j], (x, y, 1 - c)))
            mine.append(pltpu.make_async_copy(fin_bufs[j], pc.shard_half(out_refs[pc.src], c), out_sems.at[2 * j]))
            theirs.append(pltpu.make_async_copy(recv_bufs[j], pc.shard_half(out_refs[pc.src], 1 - c), out_sems.at[2 * j + 1]))
        return loads, sends, mine, theirs, in_bufs, fin_bufs

    def begin(ins, outs, scr):
        for a, b in copies(ins, outs, scr)[0]:
            a.start()
            b.start()

    def advance(ins, outs, scr):
        loads, sends, mine, _, in_bufs, fin_bufs = copies(ins, outs, scr)
        for j in range(n):
            loads[j][0].wait()
            loads[j][1].wait()
            acc = in_bufs[j][0].astype(F32)
            for k in range(1, 4):
                acc = acc + in_bufs[j][k].astype(F32)
            fin_bufs[j][...] = acc
            mine[j].start()
            sends[j].start()

    def finish(ins, outs, scr):
        _, sends, mine, theirs, _, _ = copies(ins, outs, scr)
        for j in range(n):
            sends[j].wait_recv()
            theirs[j].start()
        for j in range(n):
            sends[j].wait_send()
            mine[j].wait()
            theirs[j].wait()

    halves = [pc.shard_half_shape for pc in pieces]
    scratch = ([pltpu.VMEM((4,) + sh, BF16) for sh in halves] + [pltpu.VMEM(sh, F32) for sh in halves] * 2
               + [pltpu.SemaphoreType.DMA((2 * n,)), pltpu.SemaphoreType.DMA((n,)), pltpu.SemaphoreType.DMA((n,)),
                  pltpu.SemaphoreType.DMA((2 * n,))])
    return _SideJob(list(chip_sums) + list(lands), [jax.ShapeDtypeStruct(SHARD_OPERAND_SHAPES[o], F32) for o in operands],
                    scratch, {}, begin, advance, finish)


PEER_FLIPS = tuple((a, b, e) for a in (0, 1) for b in (0, 1) for e in (0, 1) if (a, b, e) != (0, 0, 0))


def _peers():
    x, y, c = lax.axis_index("x"), lax.axis_index("y"), lax.axis_index("c")
    me = 4 * x + 2 * y + c
    out = []
    for a, b, e in PEER_FLIPS:
        px, py, pc = (1 - x if a else x), (1 - y if b else y), (1 - c if e else c)
        out.append(((px, py, pc), 4 * px + 2 * py + pc))
    return me, out


def _exchange8_start(name, blk, tie=None):
    m = blk.shape[0]
    ties = () if tie is None else (tie,)

    def body(blk_ref, land_ref, *rest):
        sends, recvs, token = rest[len(ties) + 2:]
        me, peers = _peers()
        for k, (dev, _) in enumerate(peers):
            _remote(blk_ref, land_ref.at[me], sends.at[k], recvs.at[k], dev).start()
        token[...] = jnp.zeros(TOKEN_SHAPE, F32)

    sems = pltpu.SemaphoreType.DMA((7,))
    return _pcall(
        body, name=name,
        in_specs=[HBM, HBM] + [pl.BlockSpec(memory_space=pltpu.VMEM)] * len(ties),
        out_specs=[HBM, HBM, SEM, SEM, pl.BlockSpec(memory_space=pltpu.VMEM)],
        out_shape=[pltpu.HBM((m, LANES), F32), pltpu.HBM((8, m, LANES), F32), sems, sems,
                   jax.ShapeDtypeStruct(TOKEN_SHAPE, F32)],
        input_output_aliases={0: 0, 1: 1}, compiler_params=SPLIT_PARAMS,
    )(_in_hbm(blk), _in_hbm(lax.empty((8, m, LANES), F32)), *ties)


def _exchange8_wait(name, blk, land, send_sems, recv_sems, after):
    def body(blk_ref, land_ref, send_ref, recv_ref, after_ref, blk_out, land_out):
        del after_ref, blk_out, land_out
        _, peers = _peers()
        for k, (dev, slot) in enumerate(peers):
            cp = _remote(blk_ref, land_ref.at[slot], send_ref.at[k], recv_ref.at[k], dev)
            cp.wait_send()
            cp.wait_recv()

    m = blk.shape[0]
    return _pcall(
        body, name=name,
        in_specs=[HBM, HBM, SEM, SEM, ANY], out_specs=[HBM, HBM],
        out_shape=[pltpu.HBM((m, LANES), F32), pltpu.HBM((8, m, LANES), F32)],
        input_output_aliases={0: 0, 1: 1}, compiler_params=SPLIT_PARAMS,
    )(blk, land, send_sems, recv_sems, after)


def _collect8(name, blk, land, with_sum):
    m = blk.shape[0]

    def body(blk_ref, land_ref, out_ref, *scratch):
        sems = scratch[-1]
        dst = scratch[0] if with_sum else out_ref
        me, peers = _peers()
        copies = [pltpu.make_async_copy(blk_ref, dst.at[me], sems.at[7])]
        for k, (_, slot) in enumerate(peers):
            copies.append(pltpu.make_async_copy(land_ref.at[slot], dst.at[slot], sems.at[k]))
        for cp in copies:
            cp.start()
        for cp in copies:
            cp.wait()
        if with_sum:
            acc = dst[0]
            for dev in range(1, 8):
                acc = acc + dst[dev]
            out_ref[...] = acc

    all_shape = (8, m, LANES)
    return _pcall(
        body, name=name, in_specs=[ANY, ANY], out_specs=pl.BlockSpec(memory_space=pltpu.VMEM),
        out_shape=jax.ShapeDtypeStruct((m, LANES) if with_sum else all_shape, F32),
        scratch_shapes=([pltpu.VMEM(all_shape, F32)] if with_sum else []) + [pltpu.SemaphoreType.DMA((8,))],
    )(blk, land)


def _pack(arrays, row_counts):
    rows = []
    for a, n in zip(arrays, row_counts):
        flat = a.reshape(-1, LANES)
        rows.append(jnp.pad(flat, ((0, n - flat.shape[0]), (0, 0))))
    return jnp.concatenate(rows, axis=0)


REPL_NAMES = ("norm_mix_g", "norm_ffn_g", "even_conv_b", "even_ln_g", "even_ln_b", "odd_sg_w", "odd_sg_b", "final_g")
REPL_SHAPES = ((2, D), (2, D), (1, 512), (1, 512), (1, 512), (1, SG_GROUPS, CHUNK, CHUNK), (1, SG_GROUPS, CHUNK), (D,))
REPL_ROWS = (16, 16, 8, 8, 8, 512, 8, 8)
SHARDED_NAMES = ("even_conv_k", "odd_conv_k", "odd_ln_g", "odd_ln_b")
SHARDED_SHARD_SHAPES = ((1, CONV_W, LANES), (1, SCONV_W, LANES), (1, LANES), (1, LANES))
SHARDED_SHARD_ROWS = (32, 8, 8, 8)
SHARDED_FULL_SHAPES = ((CONV_W, 512), (SCONV_W, 512), (1, 512), (1, 512))
SHARDED_FULL_ROWS = (128, 16, 8, 8)
SMALL_NAMES = REPL_NAMES + SHARDED_NAMES
SMALL_ROWS = REPL_ROWS + SHARDED_SHARD_ROWS
SMALL_OUT_SHAPES = REPL_SHAPES[:-1] + ((1, D),) + SHARDED_SHARD_SHAPES
LOSS_ROWS = 8


def _offsets(rows):
    out, r0 = [], 0
    for n in rows:
        out.append(r0)
        r0 += n
    return out


def _adamw_small(w_pack, m_pack, v_pack, grad_sum, first_gain_sum):
    n_rows = sum(SMALL_ROWS)
    state_at = _offsets(SMALL_ROWS)
    grad_at = _offsets((LOSS_ROWS, 8) + REPL_ROWS[1:] + SHARDED_FULL_ROWS)[1:]
    c1 = 1.0 - ADAM_B1 ** ADAM_STEP
    c2 = 1.0 - ADAM_B2 ** ADAM_STEP
    n_repl = len(REPL_NAMES)

    def body(w_ref, m_ref, v_ref, g_ref, g0_ref, *rest):
        outs, gbuf = rest[:-1], rest[-1]
        chip = 2 * lax.axis_index("x") + lax.axis_index("y")
        gbuf[...] = jnp.zeros((n_rows, LANES), F32)
        gbuf[0:8, :] = g0_ref[...]
        gbuf[8:16, :] = g_ref[grad_at[0]:grad_at[0] + 8, :]
        for i in range(1, n_repl):
            gbuf[state_at[i]:state_at[i] + REPL_ROWS[i], :] = g_ref[grad_at[i]:grad_at[i] + REPL_ROWS[i], :]
        for k, shape in enumerate(SHARDED_SHARD_SHAPES):
            used = shape[-2] if len(shape) == 3 else 1
            src = pl.ds(grad_at[n_repl + k] + chip, used, stride=4) if used > 1 else pl.ds(grad_at[n_repl + k] + chip, 1)
            gbuf[state_at[n_repl + k]:state_at[n_repl + k] + used, :] = g_ref[src, :]
        g = gbuf[...]
        m_new = ADAM_B1 * m_ref[...] + (1.0 - ADAM_B1) * g
        v_new = ADAM_B2 * v_ref[...] + (1.0 - ADAM_B2) * (g * g)
        delta = -ADAM_LR * ((m_new / c1) / (jnp.sqrt(v_new / c2) + ADAM_EPS) + ADAM_WD * w_ref[...])
        for i, shape in enumerate(SMALL_OUT_SHAPES):
            for j, val in enumerate((g, delta, m_new, v_new)):
                o_ref = outs[4 * i + j]
                rows = val[state_at[i]:state_at[i] + SMALL_ROWS[i], :]
                if len(shape) == 2 and shape[1] > LANES:
                    per = shape[1] // LANES
                    for r in range(shape[0]):
                        for q in range(per):
                            o_ref[r:r + 1, q * LANES:(q + 1) * LANES] = rows[r * per + q:r * per + q + 1, :]
                elif len(shape) == 4:
                    for grp in range(shape[1]):
                        o_ref[0, grp] = rows[grp * shape[2]:(grp + 1) * shape[2], :]
                elif len(shape) == 3:
                    o_ref[0] = rows[:shape[1], :]
                else:
                    o_ref[...] = rows[:1, :]

    vm = pl.BlockSpec(memory_space=pltpu.VMEM)
    out_shape = [jax.ShapeDtypeStruct(sh, F32) for sh in SMALL_OUT_SHAPES for _ in range(4)]
    outs = _pcall(body, name="adamw_small", in_specs=[vm] * 5, out_specs=[vm] * len(out_shape), out_shape=out_shape,
                  scratch_shapes=[pltpu.VMEM((n_rows, LANES), F32)])(w_pack, m_pack, v_pack, grad_sum, first_gain_sum)
    return {n: outs[4 * i:4 * i + 4] for i, n in enumerate(SMALL_NAMES)}


def _touch(arrays):
    n = len(arrays)

    def body(*refs):
        refs[-1][...] = jnp.zeros(TOKEN_SHAPE, F32)

    outs = _pcall(
        body, name="touch", in_specs=[ANY] * n, out_specs=[ANY] * n + [pl.BlockSpec(memory_space=pltpu.VMEM)],
        out_shape=[jax.ShapeDtypeStruct(a.shape, a.dtype) for a in arrays] + [jax.ShapeDtypeStruct(TOKEN_SHAPE, F32)],
        input_output_aliases={i: i for i in range(n)})(*arrays)
    return outs[:n], outs[n]


def kernel(x, norm_mix_g, norm_ffn_g, even_w_in, even_conv_k, even_conv_b, even_ln_g, even_ln_b, even_w_out, odd_w_in, odd_conv_k, odd_ln_g, odd_ln_b, odd_sg_w, odd_sg_b, odd_w_out, ffn_w1, ffn_w2, final_g, loss_target, m_norm_mix_g, m_norm_ffn_g, m_even_w_in, m_even_conv_k, m_even_conv_b, m_even_ln_g, m_even_ln_b, m_even_w_out, m_odd_w_in, m_odd_conv_k, m_odd_ln_g, m_odd_ln_b, m_odd_sg_w, m_odd_sg_b, m_odd_w_out, m_ffn_w1, m_ffn_w2, m_final_g, v_norm_mix_g, v_norm_ffn_g, v_even_w_in, v_even_conv_k, v_even_conv_b, v_even_ln_g, v_even_ln_b, v_even_w_out, v_odd_w_in, v_odd_conv_k, v_odd_ln_g, v_odd_ln_b, v_odd_sg_w, v_odd_sg_b, v_odd_w_out, v_ffn_w1, v_ffn_w2, v_final_g):
    names = ("norm_mix_g", "norm_ffn_g", "even_w_in", "even_conv_k", "even_conv_b", "even_ln_g", "even_ln_b", "even_w_out",
             "odd_w_in", "odd_conv_k", "odd_ln_g", "odd_ln_b", "odd_sg_w", "odd_sg_b", "odd_w_out", "ffn_w1", "ffn_w2", "final_g")
    w = dict(zip(names, (norm_mix_g, norm_ffn_g, even_w_in, even_conv_k, even_conv_b, even_ln_g, even_ln_b, even_w_out,
                         odd_w_in, odd_conv_k, odd_ln_g, odd_ln_b, odd_sg_w, odd_sg_b, odd_w_out, ffn_w1, ffn_w2, final_g)))
    mom = dict(zip(names, (m_norm_mix_g, m_norm_ffn_g, m_even_w_in, m_even_conv_k, m_even_conv_b, m_even_ln_g, m_even_ln_b,
                           m_even_w_out, m_odd_w_in, m_odd_conv_k, m_odd_ln_g, m_odd_ln_b, m_odd_sg_w, m_odd_sg_b, m_odd_w_out,
                           m_ffn_w1, m_ffn_w2, m_final_g)))
    vel = dict(zip(names, (v_norm_mix_g, v_norm_ffn_g, v_even_w_in, v_even_conv_k, v_even_conv_b, v_even_ln_g, v_even_ln_b,
                           v_even_w_out, v_odd_w_in, v_odd_conv_k, v_odd_ln_g, v_odd_ln_b, v_odd_sg_w, v_odd_sg_b, v_odd_w_out,
                           v_ffn_w1, v_ffn_w2, v_final_g)))
    big_names = ("even_w_in", "even_w_out", "odd_w_in", "odd_w_out", "ffn_w1", "ffn_w2")
    chip = 2 * lax.axis_index("x") + lax.axis_index("y")

    def shard2d(t, name):
        return t[name].reshape(SHARD_OPERAND_SHAPES[big_names.index(name)])

    chip_op = jnp.reshape(chip, (1,)).astype(jnp.int32)
    first = _cast_place(PIECES[0], shard2d(w, big_names[PIECES[0].src]), chip_op)
    fly0, send0, recv0, token = _gather_start("gather_start_first", PIECES[:1], [first])
    small_pack = _pack([w[n] for n in SHARDED_NAMES], SHARDED_SHARD_ROWS)
    small_blk, small_land, small_send, small_recv, small_token = _exchange8_start("gather_small_start", small_pack, token)
    placed = [_cast_place(pc, shard2d(w, big_names[pc.src]), chip_op, tie=small_token) for pc in PIECES[1:]]
    fly1, send1, recv1, all_started = _gather_start("gather_start_rest", PIECES[1:], placed)
    flying, gather_send, gather_recv = fly0 + fly1, send0 + send1, recv0 + recv1
    ready = {}

    names_in_order = [pc.name for pc in PIECES]

    riding = {}

    (*state_packs, _), idle_work_done = _touch(
        [_pack([t[n] for n in SMALL_NAMES], SMALL_ROWS) for t in (w, mom, vel)] + [all_started])

    def weight(name, after):
        if name in riding:
            job, k = riding.pop(name)
            ready[name] = job.results[k]
        if name not in ready:
            landed = _gather_wait(PIECES[0], flying[0], gather_send[0], gather_recv[0], after)
            ready[name], = _core_forward(PIECES[:1], [landed])
        return ready[name]

    def before(call, after):
        if call in FORWARD_RIDES:
            group = FORWARD_RIDES[call]
            landed = [_gather_wait(PIECES[j], flying[j], gather_send[j], gather_recv[j], after) for j in group]
            job = _core_forward_job([PIECES[j] for j in group], landed)
            riding.update((PIECES[j].name, (job, k)) for k, j in enumerate(group))
            _ride_next_call(job)
        elif call == JOIN_RIDES_IN:
            early_join.append(join_job(JOIN_GROUPS[1], after))
            _ride_next_call(early_join[0])

    early_join = []

    def join_job(operands, after):
        pieces = [pc for pc in PIECES if pc.src in operands]
        done = {}
        for entry in list(scattering):
            if entry[0] in pieces:
                done[entry[0].name] = _scatter_wait(*entry, after)
                scattering.remove(entry)
        return _allsum_join_job(operands, [done[pc.name][0] for pc in pieces], [done[pc.name][1] for pc in pieces])

    scattering = []
    held = []

    core_op = jnp.reshape(lax.axis_index("c"), (1,)).astype(jnp.int32)

    def emit(name, a, b, tie=None):
        pc = PIECES[names_in_order.index(name)]
        held.append((pc, _mm_dw_chipsum(pc, a, b, core_op, tie)))
        if name in HOLD_BACK:
            return None
        pieces = [pc for pc, _ in held]
        started, token = _scatter_start(pieces, [chip_sum for _, chip_sum in held])
        scattering.extend((pc,) + tuple(st) for pc, st in zip(pieces, started))
        held.clear()
        return token

    full = {}
    small_blk, small_land = _exchange8_wait("gather_small_wait", small_blk, small_land, small_send, small_recv, all_started)
    gathered = _collect8("gather_small_collect", small_blk, small_land, False)
    gathered = gathered.reshape(4, 2, sum(SHARDED_SHARD_ROWS), LANES)[:, 0]
    r0 = 0
    for n, sh, rows, full_sh in zip(SHARDED_NAMES, SHARDED_SHARD_SHAPES, SHARDED_SHARD_ROWS, SHARDED_FULL_SHAPES):
        per_chip = gathered[:, r0:r0 + rows].reshape(4, -1)[:, :full_sh[0] * LANES].reshape(4, full_sh[0], LANES)
        full[n] = jnp.transpose(per_chip, (1, 0, 2)).reshape(full_sh)
        r0 += rows
    p = dict(full)
    p.update(norm_mix_g0=norm_mix_g[0:1], norm_mix_g1=norm_mix_g[1:2], norm_ffn_g0=norm_ffn_g[0:1], norm_ffn_g1=norm_ffn_g[1:2],
             even_conv_b=even_conv_b, even_ln_g=even_ln_g, even_ln_b=even_ln_b,
             odd_sg_w=odd_sg_w[0], odd_sg_bt=odd_sg_b[0].T, final_g=final_g[None, :],
             first_norm_after=idle_work_done)

    small = {}

    def emit_small(loss_row, g):
        parts = [loss_row, g["norm_mix_g1"], g["norm_ffn_g0"], g["norm_ffn_g1"], g["even_conv_b"], g["even_ln_g"],
                 g["even_ln_b"], g["odd_sg_w"], g["odd_sg_bt"].T, g["final_g"],
                 g["even_conv_k"], g["odd_conv_k"], g["odd_ln_g"], g["odd_ln_b"]]
        pack = _pack(parts, (8, 8, 8, 8) + REPL_ROWS[2:] + SHARDED_FULL_ROWS)
        small["blk"], small["land"], small["send"], small["recv"], token = _exchange8_start("allreduce_small_start", pack)
        return token

    dx, dg0 = _local_step(x[0], loss_target[0], p, weight, emit, emit_small, before)
    last_blk, last_land, last_send, last_recv, grad_token = _exchange8_start("allreduce_last_start", _pack([dg0], (8,)))

    big_grads = dict(zip((big_names[o] for o in JOIN_GROUPS[1]), early_join[0].results))
    delta, new_m, new_v, grads_big = {}, {}, {}, {}

    def adamw_big(n):
        d2, m2, v2, g2 = _adamw(f"adamw_{n}", shard2d(w, n), big_grads[n], shard2d(mom, n), shard2d(vel, n), True,
                                tie=grad_token)
        delta[n], new_m[n], new_v[n], grads_big[n] = (t.reshape(w[n].shape) for t in (d2, m2, v2, g2))

    for o in JOIN_GROUPS[1]:
        adamw_big(big_names[o])
    late_join = join_job(JOIN_GROUPS[0], new_v[big_names[JOIN_GROUPS[1][-1]]])
    big_grads.update(zip((big_names[o] for o in JOIN_GROUPS[0]), _run_job("allsum_join_late", late_join)))
    for o in JOIN_GROUPS[0]:
        adamw_big(big_names[o])

    joined_last = big_grads[big_names[JOIN_GROUPS[0][-1]]]
    grad_blk, grad_land = _exchange8_wait("allreduce_small_wait", small["blk"], small["land"], small["send"],
                                          small["recv"], joined_last)
    grad_sum = _collect8("allreduce_small_sum", grad_blk, grad_land, True)
    last_blk, last_land = _exchange8_wait("allreduce_last_wait", last_blk, last_land, last_send, last_recv, joined_last)
    dg0_sum = _collect8("allreduce_last_sum", last_blk, last_land, True)
    loss = grad_sum[0, 0]

    grads = dict(grads_big)
    for n, results in _adamw_small(*state_packs, grad_sum, dg0_sum).items():
        grads[n], delta[n], new_m[n], new_v[n] = (t.reshape(w[n].shape) for t in results)

    out = [loss, dx[None]]
    for res in (grads, delta, new_m, new_v):
        out.extend(res[n] for n in names)
    return tuple(out)
```

```python
import functools

import jax
import jax.numpy as jnp
from jax import lax
from jax.experimental import pallas as pl
from jax.experimental.pallas import tpu as pltpu

F32 = jnp.float32
BF16 = jnp.bfloat16

T = 2048
D = 1024
CONV_CH = 512
CONV_W = 31
HEAD_DIM = 64
ATT_W = 1536
EVEN_IN = 5632
ODD_IN = 2560
SCONV_W = 3
SG_GROUPS = 4
CHUNK = 128
D_FF = 4096
EPS = 1e-6
DILATIONS = (1, 4, 16)
BAND = 128
SCALE = HEAD_DIM ** -0.5
NEG = -1e30

ADAM_LR = 0.001
ADAM_B1 = 0.9
ADAM_B2 = 0.999
ADAM_EPS = 1e-08
ADAM_WD = 0.01
ADAM_STEP = 10

V7X_VMEM_BYTES = 64 * 2 ** 20
VMEM_LIMIT = V7X_VMEM_BYTES - 8 * 2 ** 20
LANES = 128
TOKEN_SHAPE = (8, LANES)


SIBLING_COLLECTIVE_ID = 0


def _sibling_handshake():
    x, y, c = lax.axis_index("x"), lax.axis_index("y"), lax.axis_index("c")
    barrier = pltpu.get_barrier_semaphore()
    pl.semaphore_signal(barrier, inc=1, device_id=(x, y, 1 - c), device_id_type=pl.DeviceIdType.MESH)
    pl.semaphore_wait(barrier, 1)


class _SideJob:
    def __init__(self, inputs, out_shape, scratch_shapes, aliases, begin, advance, finish):
        self.inputs, self.out_shape, self.scratch_shapes = list(inputs), list(out_shape), list(scratch_shapes)
        self.aliases, self.begin, self.advance, self.finish = dict(aliases), begin, advance, finish
        self.results = None


_PENDING_JOBS = []


def _ride_next_call(job):
    _PENDING_JOBS.append(job)


def _pcall(body, **kw):
    if not _PENDING_JOBS or "grid" not in kw:
        return pl.pallas_call(body, **kw)
    job = _PENDING_JOBS.pop()
    as_list = lambda v: list(v) if isinstance(v, (list, tuple)) else [v]
    single_out = not isinstance(kw["out_shape"], (list, tuple))
    in_specs, out_specs, out_shape = as_list(kw["in_specs"]), as_list(kw["out_specs"]), as_list(kw["out_shape"])
    scratch = list(kw.get("scratch_shapes", ()))
    grid = kw["grid"]
    n_steps = 1
    for extent in grid:
        n_steps *= extent
    assert n_steps >= 3
    n_in, n_out, n_scr = len(in_specs), len(out_specs), len(scratch)
    j_in, j_out = len(job.inputs), len(job.out_shape)
    any_spec = pl.BlockSpec(memory_space=pl.ANY)

    def hosted(*refs):
        ins, j_ins = refs[:n_in], refs[n_in:n_in + j_in]
        outs = refs[n_in + j_in:n_in + j_in + n_out]
        j_outs = refs[n_in + j_in + n_out:n_in + j_in + n_out + j_out]
        scr = refs[n_in + j_in + n_out + j_out:n_in + j_in + n_out + j_out + n_scr]
        j_scr = refs[n_in + j_in + n_out + j_out + n_scr:]
        step = pl.program_id(0)
        for axis in range(1, len(grid)):
            step = step * grid[axis] + pl.program_id(axis)

        @pl.when(step == 0)
        def _():
            _sibling_handshake()
            job.begin(j_ins, j_outs, j_scr)

        @pl.when(step == 1)
        def _():
            job.advance(j_ins, j_outs, j_scr)

        body(*ins, *outs, *scr)

        @pl.when(step == n_steps - 1)
        def _():
            job.finish(j_ins, j_outs, j_scr)

    aliases = dict(kw.get("input_output_aliases", {}))
    aliases.update({n_in + a: n_out + b for a, b in job.aliases.items()})
    call = pl.pallas_call(
        hosted, name=kw["name"], grid=grid,
        in_specs=in_specs + [any_spec] * j_in, out_specs=out_specs + [any_spec] * j_out,
        out_shape=out_shape + job.out_shape, scratch_shapes=scratch + job.scratch_shapes,
        input_output_aliases=aliases,
        compiler_params=pltpu.CompilerParams(dimension_semantics=("arbitrary",) * len(grid), vmem_limit_bytes=VMEM_LIMIT,
                                             collective_id=SIBLING_COLLECTIVE_ID))

    def run(*args):
        res = call(*args, *job.inputs)
        job.results = list(res[n_out:])
        return res[0] if single_out else list(res[:n_out])

    return run


def _params(*sem):
    return pltpu.CompilerParams(dimension_semantics=sem, vmem_limit_bytes=VMEM_LIMIT)


def _dot(a, b, dims):
    return lax.dot_general(a, b, (dims, ((), ())), preferred_element_type=F32)


def _nn(a, b):
    return _dot(a, b, ((1,), (0,)))


def _nt(a, b):
    return _dot(a, b, ((1,), (1,)))


def _tn(a, b):
    return _dot(a, b, ((0,), (0,)))


def _sigmoid(x):
    return 1.0 / (1.0 + jnp.exp(-x))


MM_VMEM_BUDGET = 40 * 2 ** 20


def _mm_tiles(mode, m, n, k, a_bytes, b_bytes, extra_bytes, out_bytes):
    def divisors(total, unit):
        return [t for t in range(unit, total + 1, unit) if total % t == 0]

    best = None
    for tm in divisors(m, LANES if mode == "tn" else 8):
        for tn in divisors(n, LANES):
            blocks = tm * k * a_bytes + tn * k * b_bytes + tm * tn * (extra_bytes + out_bytes)
            casts = (tm * k * 2 if a_bytes == 4 else 0) + (tn * k * 2 if b_bytes == 4 else 0)
            if 2 * blocks + casts + tm * tn * 4 > MM_VMEM_BUDGET:
                continue
            key = ((m // tm) * (n // tn), (m // tm) * n * k * b_bytes, abs(tm - tn))
            if best is None or key < best[0]:
                best = (key, tm, tn)
    return best[1], best[2]


def _mm(name, mode, a, b, m, n, k, out_dtypes, *, b_off=0, extras=(), epi=None, tie=None):
    tm, tn = _mm_tiles(mode, m, n, k, a.dtype.itemsize, b.dtype.itemsize, sum(e.dtype.itemsize for e in extras),
                       sum(jnp.dtype(dt).itemsize for dt in out_dtypes))
    assert b_off % tn == 0
    b_off //= tn
    if mode == "nn":
        a_spec = pl.BlockSpec((tm, k), lambda i, j: (i, 0))
        b_spec = pl.BlockSpec((k, tn), lambda i, j: (0, j + b_off))
        dims = ((1,), (0,))
    elif mode == "nt":
        a_spec = pl.BlockSpec((tm, k), lambda i, j: (i, 0))
        b_spec = pl.BlockSpec((tn, k), lambda i, j: (j, 0))
        dims = ((1,), (1,))
    else:
        a_spec = pl.BlockSpec((k, tm), lambda i, j: (0, i))
        b_spec = pl.BlockSpec((k, tn), lambda i, j: (0, j))
        dims = ((0,), (0,))
    o_spec = pl.BlockSpec((tm, tn), lambda i, j: (i, j))
    n_extra = len(extras)
    ties = () if tie is None else (tie,)

    def body(a_ref, b_ref, *rest):
        rest = rest[len(ties):]
        acc = _dot(a_ref[...].astype(BF16), b_ref[...].astype(BF16), dims)
        vals = epi(acc, *[e[...] for e in rest[:n_extra]]) if epi is not None else (acc,)
        for o_ref, v in zip(rest[n_extra:], vals):
            o_ref[...] = v.astype(o_ref.dtype)

    outs = _pcall(
        body, name=name, grid=(m // tm, n // tn),
        in_specs=[a_spec, b_spec] + [pl.BlockSpec(TOKEN_SHAPE, lambda i, j: (0, 0))] * len(ties) + [o_spec] * n_extra,
        out_specs=[o_spec] * len(out_dtypes),
        out_shape=[jax.ShapeDtypeStruct((m, n), dt) for dt in out_dtypes],
        compiler_params=_params("parallel", "parallel"),
    )(a, b, *ties, *extras)
    return outs[0] if len(out_dtypes) == 1 else outs


def _row_tile(k, a_bytes, n_row_blocks):
    for tm in (1024, 512, 256, 128):
        if 2 * (tm * k * a_bytes + n_row_blocks * tm * D * 4) + D * k * 2 + tm * D * 4 <= MM_VMEM_BUDGET + 4 * 2 ** 20:
            return tm
    raise ValueError("no row tile fits")


def _resident(shape):
    return pl.BlockSpec(shape, lambda i: (0, 0), pipeline_mode=pl.Buffered(1))


FFN0_DOWN_TILE = 512


def _mm_out_norm(name, a, b, k, res, g_next, tm=None):
    tm = tm or _row_tile(k, a.dtype.itemsize, 3)

    def body(a_ref, b_ref, r_ref, g_ref, h_ref, hn_ref):
        h = _nn(a_ref[...].astype(BF16), b_ref[...]) + r_ref[...]
        h_ref[...] = h
        r = lax.rsqrt(jnp.mean(h * h, axis=-1, keepdims=True) + EPS)
        hn_ref[...] = ((h * r) * g_ref[...]).astype(BF16)

    row = pl.BlockSpec((tm, D), lambda i: (i, 0))
    return _pcall(
        body, name=name, grid=(T // tm,),
        in_specs=[pl.BlockSpec((tm, k), lambda i: (i, 0)), _resident((k, D)), row,
                  pl.BlockSpec((1, D), lambda i: (0, 0))],
        out_specs=[row, row],
        out_shape=[jax.ShapeDtypeStruct((T, D), F32), jax.ShapeDtypeStruct((T, D), BF16)],
        compiler_params=_params("parallel"),
    )(a, b, res, g_next)


def _ffn_last(name, hn, w1, w2, res, g, target):
    tm = FFN_BWD_TILE

    def body(a_ref, w1_ref, w2_ref, r_ref, g_ref, t_ref, f_ref, dh_ref, dg_ref, loss_ref):
        u = jnp.maximum(_nn(a_ref[...], w1_ref[...]), 0.0)
        f = (u * u).astype(BF16)
        f_ref[...] = f
        x = _nn(f, w2_ref[...]) + r_ref[...]
        r = lax.rsqrt(jnp.mean(x * x, axis=-1, keepdims=True) + EPS)
        nrm = x * r
        gain = g_ref[...]
        err = nrm * gain - t_ref[...]
        dy = err * (1.0 / D)
        dn = dy * gain
        dh_ref[...] = r * (dn - nrm * jnp.mean(dn * nrm, axis=-1, keepdims=True))

        @pl.when(pl.program_id(0) == 0)
        def _():
            dg_ref[...] = jnp.zeros_like(dg_ref)
            loss_ref[...] = jnp.zeros_like(loss_ref)

        dg_ref[...] += jnp.sum(dy * nrm, axis=0, keepdims=True)
        part = jnp.sum(jnp.sum(err * err, axis=1, keepdims=True), axis=0, keepdims=True) * (0.5 / D)
        loss_ref[...] += jnp.broadcast_to(part, (1, LANES))

    row = pl.BlockSpec((tm, D), lambda i: (i, 0))
    wide = pl.BlockSpec((tm, D_FF), lambda i: (i, 0))
    vec = pl.BlockSpec((1, D), lambda i: (0, 0))
    return _pcall(
        body, name=name, grid=(T // tm,),
        in_specs=[row, _resident((D, D_FF)), _resident((D_FF, D)), row, vec, row],
        out_specs=[wide, row, vec, pl.BlockSpec((1, LANES), lambda i: (0, 0))],
        out_shape=[jax.ShapeDtypeStruct((T, D_FF), BF16), jax.ShapeDtypeStruct((T, D), F32),
                   jax.ShapeDtypeStruct((1, D), F32), jax.ShapeDtypeStruct((1, LANES), F32)],
        compiler_params=_params("arbitrary"),
    )(hn, w1, w2, res, g, target)


def _mm_dx_norm(name, dz, w, k, h, g, dres, tie=None):
    tm = _row_tile(k, dz.dtype.itemsize, 3)
    ties = () if tie is None else (tie,)

    def body(a_ref, b_ref, *rest):
        h_ref, g_ref, r_ref, dh_ref, dg_ref = rest[len(ties):]
        dy = _nt(a_ref[...].astype(BF16), b_ref[...])
        x = h_ref[...]
        r = lax.rsqrt(jnp.mean(x * x, axis=-1, keepdims=True) + EPS)
        nrm = x * r
        dn = dy * g_ref[...]
        dh_ref[...] = r_ref[...] + r * (dn - nrm * jnp.mean(dn * nrm, axis=-1, keepdims=True))

        @pl.when(pl.program_id(0) == 0)
        def _():
            dg_ref[...] = jnp.zeros_like(dg_ref)

        dg_ref[...] += jnp.sum(dy * nrm, axis=0, keepdims=True)

    row = pl.BlockSpec((tm, D), lambda i: (i, 0))
    vec = pl.BlockSpec((1, D), lambda i: (0, 0))
    return _pcall(
        body, name=name, grid=(T // tm,),
        in_specs=[pl.BlockSpec((tm, k), lambda i: (i, 0)), _resident((D, k))]
        + [pl.BlockSpec(TOKEN_SHAPE, lambda i: (0, 0))] * len(ties) + [row, vec, row],
        out_specs=[row, vec],
        out_shape=[jax.ShapeDtypeStruct((T, D), F32), jax.ShapeDtypeStruct((1, D), F32)],
        compiler_params=_params("arbitrary"),
    )(dz, w, *ties, h, g, dres)


def _rms_fwd(name, h, g, tm=512, tie=None):
    ties = () if tie is None else (tie,)

    def body(h_ref, g_ref, *rest):
        x = h_ref[...]
        r = lax.rsqrt(jnp.mean(x * x, axis=-1, keepdims=True) + EPS)
        rest[-1][...] = ((x * r) * g_ref[...]).astype(BF16)

    return _pcall(
        body, name=name, grid=(T // tm,),
        in_specs=[pl.BlockSpec((tm, D), lambda i: (i, 0)), pl.BlockSpec((1, D), lambda i: (0, 0))]
        + [pl.BlockSpec(TOKEN_SHAPE, lambda i: (0, 0))] * len(ties),
        out_specs=pl.BlockSpec((tm, D), lambda i: (i, 0)),
        out_shape=jax.ShapeDtypeStruct((T, D), BF16),
        compiler_params=_params("parallel"),
    )(h, g, *ties)


CONV_TILE = 256
CONV_HALO = 32


def _glu(z):
    return z[:, :CONV_CH] * _sigmoid(z[:, CONV_CH:])


SUBLANES = 8


def _sublane_shifts(win):
    n = win.shape[0]
    return [win] + [win[r:r + n - SUBLANES, :] for r in range(1, SUBLANES)]


def _rows_from(shifts, off, n):
    q, r = divmod(off, SUBLANES)
    return shifts[r][q * SUBLANES:q * SUBLANES + n, :]


def _econv_fwd(zc, conv_k, conv_b, ln_g, ln_b):
    R, H = CONV_TILE, CONV_HALO

    def body(z_ref, zh_ref, k_ref, b_ref, g_ref, be_ref, cv_ref, cat_ref):
        i = pl.program_id(0)
        glu = _glu(z_ref[...])
        halo = _glu(zh_ref[...]) * (i > 0).astype(F32)
        win = _sublane_shifts(jnp.concatenate([halo, glu], axis=0))
        acc = jnp.zeros((R, CONV_CH), F32) + b_ref[...]
        for j in range(CONV_W):
            acc = acc + k_ref[j:j + 1, :] * _rows_from(win, H - (CONV_W - 1) + j, R)
        cv_ref[...] = acc
        mu = jnp.mean(acc, axis=-1, keepdims=True)
        xc = acc - mu
        rstd = lax.rsqrt(jnp.mean(xc * xc, axis=-1, keepdims=True) + EPS)
        ln = xc * rstd * g_ref[...] + be_ref[...]
        cat_ref[...] = (ln * _sigmoid(ln)).astype(BF16)

    vec = pl.BlockSpec((1, CONV_CH), lambda i: (0, 0))
    return _pcall(
        body, name="econv_fwd", grid=(T // R,),
        in_specs=[pl.BlockSpec((R, 2 * CONV_CH), lambda i: (i, 0)),
                  pl.BlockSpec((H, 2 * CONV_CH), lambda i: (jnp.maximum(i * (R // H) - 1, 0), 0)),
                  pl.BlockSpec((CONV_W, CONV_CH), lambda i: (0, 0)), vec, vec, vec],
        out_specs=[pl.BlockSpec((R, CONV_CH), lambda i: (i, 0)), pl.BlockSpec((R, CONV_CH), lambda i: (i, 0))],
        out_shape=[jax.ShapeDtypeStruct((T, CONV_CH), F32), jax.ShapeDtypeStruct((T, D), BF16)],
        compiler_params=_params("parallel"),
    )(zc, zc, conv_k, conv_b, ln_g, ln_b)


def _econv_bwd_ln(cv, dcat, ln_g, ln_b):
    R = CONV_TILE

    def body(cv_ref, d_ref, g_ref, be_ref, dcv_ref, dg_ref, dbe_ref, dcb_ref):
        cv_t = cv_ref[...]
        mu = jnp.mean(cv_t, axis=-1, keepdims=True)
        xc = cv_t - mu
        rstd = lax.rsqrt(jnp.mean(xc * xc, axis=-1, keepdims=True) + EPS)
        xh = xc * rstd
        ln = xh * g_ref[...] + be_ref[...]
        sg = _sigmoid(ln)
        dln = d_ref[...] * (sg * (1.0 + ln * (1.0 - sg)))
        dxh = dln * g_ref[...]
        dcv = rstd * (dxh - jnp.mean(dxh, axis=-1, keepdims=True) - xh * jnp.mean(dxh * xh, axis=-1, keepdims=True))
        dcv_ref[...] = dcv

        @pl.when(pl.program_id(0) == 0)
        def _():
            dg_ref[...] = jnp.zeros_like(dg_ref)
            dbe_ref[...] = jnp.zeros_like(dbe_ref)
            dcb_ref[...] = jnp.zeros_like(dcb_ref)

        dg_ref[...] += jnp.sum(dln * xh, axis=0, keepdims=True)
        dbe_ref[...] += jnp.sum(dln, axis=0, keepdims=True)
        dcb_ref[...] += jnp.sum(dcv, axis=0, keepdims=True)

    vec = pl.BlockSpec((1, CONV_CH), lambda i: (0, 0))
    row = pl.BlockSpec((R, CONV_CH), lambda i: (i, 0))
    vshape = jax.ShapeDtypeStruct((1, CONV_CH), F32)
    return _pcall(
        body, name="econv_bwd_ln", grid=(T // R,),
        in_specs=[row, row, vec, vec], out_specs=[row, vec, vec, vec],
        out_shape=[jax.ShapeDtypeStruct((T, CONV_CH), F32), vshape, vshape, vshape],
        compiler_params=_params("arbitrary"),
    )(cv, dcat, ln_g, ln_b)


def _econv_bwd_conv(dcv, zc, conv_k):
    R, H = CONV_TILE, CONV_HALO
    last = T // R - 1

    def body(d_ref, dn_ref, z_ref, zh_ref, k_ref, dz_ref, dk_ref):
        i = pl.program_id(0)
        z = z_ref[...]
        a_lin = z[:, :CONV_CH]
        sg = _sigmoid(z[:, CONV_CH:])
        glu = a_lin * sg
        halo = _glu(zh_ref[...]) * (i > 0).astype(F32)
        win = _sublane_shifts(jnp.concatenate([halo, glu], axis=0))
        dcv_t = d_ref[...]
        nxt = dn_ref[...] * (i < last).astype(F32)
        winb = _sublane_shifts(jnp.concatenate([dcv_t, nxt], axis=0))

        @pl.when(i == 0)
        def _():
            dk_ref[...] = jnp.zeros_like(dk_ref)

        dglu = jnp.zeros((R, CONV_CH), F32)
        for j in range(CONV_W):
            dk_ref[j:j + 1, :] += jnp.sum(dcv_t * _rows_from(win, H - (CONV_W - 1) + j, R), axis=0, keepdims=True)
            dglu = dglu + k_ref[j:j + 1, :] * _rows_from(winb, CONV_W - 1 - j, R)
        dz_ref[...] = jnp.concatenate([dglu * sg, dglu * a_lin * sg * (1.0 - sg)], axis=1).astype(BF16)

    return _pcall(
        body, name="econv_bwd_conv", grid=(T // R,),
        in_specs=[pl.BlockSpec((R, CONV_CH), lambda i: (i, 0)),
                  pl.BlockSpec((H, CONV_CH), lambda i: (jnp.minimum((i + 1) * (R // H), T // H - 1), 0)),
                  pl.BlockSpec((R, 2 * CONV_CH), lambda i: (i, 0)),
                  pl.BlockSpec((H, 2 * CONV_CH), lambda i: (jnp.maximum(i * (R // H) - 1, 0), 0)),
                  pl.BlockSpec((CONV_W, CONV_CH), lambda i: (0, 0))],
        out_specs=[pl.BlockSpec((R, 2 * CONV_CH), lambda i: (i, 0)), pl.BlockSpec((CONV_W, CONV_CH), lambda i: (0, 0))],
        out_shape=[jax.ShapeDtypeStruct((T, EVEN_IN), BF16), jax.ShapeDtypeStruct((CONV_W, CONV_CH), F32)],
        compiler_params=_params("arbitrary"),
    )(dcv, dcv, zc, zc, conv_k)


def _swap_halves(v):
    lane = lax.broadcasted_iota(jnp.int32, (1, v.shape[1]), 1)
    return jnp.where((lane % HEAD_DIM) < HEAD_DIM // 2, pltpu.roll(v, LANES - HEAD_DIM // 2, 1),
                     pltpu.roll(v, HEAD_DIM // 2, 1))


def _qkv_proj(hn, w_in, rope_c, rope_s, tm=T):
    tn = 4 * LANES

    def body(a_ref, b_ref, c_ref, s_ref, o_ref):
        j = pl.program_id(1)
        acc = _nn(a_ref[...], b_ref[...])
        for p in range(4):
            v = acc[:, p * LANES:(p + 1) * LANES]
            rot = v * c_ref[...] + _swap_halves(v) * s_ref[...]
            o_ref[p] = jnp.where(j < 6, rot, v)

    tab = pl.BlockSpec((tm, LANES), lambda i, j: (i, 0))
    return _pcall(
        body, name="qkv_proj", grid=(T // tm, 9),
        in_specs=[pl.BlockSpec((tm, D), lambda i, j: (i, 0)),
                  pl.BlockSpec((D, tn), lambda i, j: (0, j + (2 * CONV_CH) // tn)), tab, tab],
        out_specs=pl.BlockSpec((None, 4, tm, LANES), lambda i, j: (j, 0, i, 0)),
        out_shape=jax.ShapeDtypeStruct((9, 4, T, LANES), F32),
        compiler_params=_params("parallel", "parallel"),
    )(hn, w_in, rope_c, rope_s)


ATTN_FWD_UNROLL = 4
ATTN_BWD_UNROLL = 4


def _band_rows(start, d):
    if d == 1:
        return pl.ds(pl.multiple_of(start, BAND), BAND)
    return pl.ds(start, BAND, stride=d)


def _band_masks(n):
    row = lax.broadcasted_iota(jnp.int32, (BAND, BAND), 0)
    col = lax.broadcasted_iota(jnp.int32, (BAND, BAND), 1)
    no_prev = (n == 0).astype(jnp.int32) * (2 * BAND)
    return col <= row, col >= row + no_prev


def _attn_fwd(qkv, g):
    d = DILATIONS[g]
    nb = T // d // BAND
    has_prev = nb > 1

    def body(q_ref, k_ref, v_ref, o_ref, l_ref):
        lane_lo = lax.broadcasted_iota(jnp.int32, (BAND, LANES), 1) < HEAD_DIM

        heads = (lane_lo, jnp.logical_not(lane_lo))
        ones = jnp.ones((BAND, LANES), BF16)

        def step(it, carry):
            tiles = []
            for u in range(ATTN_FWD_UNROLL):
                idx = it * ATTN_FWD_UNROLL + u
                r = idx // nb
                n = idx % nb
                cur = _band_rows(n * (BAND * d) + r, d)
                prev = _band_rows(jnp.maximum(n - 1, 0) * (BAND * d) + r, d)
                mc, mp = _band_masks(n)
                kp = k_ref[prev, :].astype(BF16) if has_prev else None
                vp = v_ref[prev, :].astype(BF16) if has_prev else None
                tiles.append((cur, mc, mp, q_ref[cur, :], k_ref[cur, :].astype(BF16), v_ref[cur, :].astype(BF16), kp, vp))
            scores = []
            for cur, mc, mp, q, kc, vc, kp, vp in tiles:
                for hm in heads:
                    qm = jnp.where(hm, q, 0.0).astype(BF16)
                    sc = jnp.where(mc, _nt(qm, kc) * SCALE, NEG)
                    scores.append((sc, jnp.where(mp, _nt(qm, kp) * SCALE, NEG)) if has_prev else (sc,))
            maxes = [functools.reduce(jnp.maximum, [jnp.max(sx, axis=1, keepdims=True) for sx in ss]) for ss in scores]
            probs = [[jnp.exp(sx - mx).astype(BF16) for sx in ss] for ss, mx in zip(scores, maxes)]
            dens = [functools.reduce(jnp.add, [_nn(px, ones) for px in ps]) for ps in probs]
            for t, (cur, mc, mp, q, kc, vc, kp, vp) in enumerate(tiles):
                outs, lses = [], []
                for h in range(2):
                    ps = probs[2 * t + h]
                    acc = _nn(ps[0], vc) + _nn(ps[1], vp) if has_prev else _nn(ps[0], vc)
                    outs.append(acc / dens[2 * t + h])
                    lses.append(maxes[2 * t + h] + jnp.log(dens[2 * t + h]))
                o_ref[cur, :] = jnp.where(lane_lo, outs[0], outs[1])
                l_ref[cur, :] = jnp.where(lane_lo, lses[0], lses[1])
            return carry

        lax.fori_loop(0, d * nb // ATTN_FWD_UNROLL, step, 0)

    def slab(which):
        return pl.BlockSpec((None, None, T, LANES), lambda p: (which * 3 + g, p, 0, 0))

    out = pl.BlockSpec((None, T, LANES), lambda p: (p, 0, 0))
    shape = jax.ShapeDtypeStruct((4, T, LANES), F32)
    return _pcall(
        body, name=f"attn_fwd{g}", grid=(4,),
        in_specs=[slab(0), slab(1), slab(2)], out_specs=[out, out], out_shape=[shape, shape],
        compiler_params=_params("parallel"),
    )(qkv, qkv, qkv)


def _attn_merge(outs, lses, cat, tm=1024):
    def body(o0, o1, o2, l0, l1, l2, cat_in, cat_ref, att_ref, w0, w1, w2):
        del cat_in
        la, lb, lc = l0[...], l1[...], l2[...]
        mx = jnp.maximum(jnp.maximum(la, lb), lc)
        ea, eb, ec = jnp.exp(la - mx), jnp.exp(lb - mx), jnp.exp(lc - mx)
        inv = 1.0 / (ea + eb + ec)
        wa, wb, wc = ea * inv, eb * inv, ec * inv
        att = wa * o0[...] + wb * o1[...] + wc * o2[...]
        att_ref[...] = att
        cat_ref[...] = att.astype(BF16)
        w0[...] = wa
        w1[...] = wb
        w2[...] = wc

    slab = pl.BlockSpec((None, tm, LANES), lambda p, i: (p, i, 0))
    shape = jax.ShapeDtypeStruct((4, T, LANES), F32)
    return _pcall(
        body, name="attn_merge", grid=(4, T // tm),
        in_specs=[slab] * 6 + [pl.BlockSpec(memory_space=pl.ANY)],
        out_specs=[pl.BlockSpec((tm, LANES), lambda p, i: (i, CONV_CH // LANES + p)), slab, slab, slab, slab],
        out_shape=[jax.ShapeDtypeStruct((T, D), BF16), shape, shape, shape, shape],
        input_output_aliases={6: 0},
        compiler_params=_params("parallel", "parallel"),
    )(*outs, *lses, cat)


def _attn_bwd(qkv, lse, wgt, att, dcat, dqkv, g):
    d = DILATIONS[g]
    nb = T // d // BAND
    has_prev = nb > 1

    def body(q_ref, k_ref, v_ref, l_ref, w_ref, a_ref, da_ref, dq_in, o_ref):
        del dq_in
        lane = lax.broadcasted_iota(jnp.int32, (BAND, LANES), 1)
        lane_lo = lane < HEAD_DIM
        row = lax.broadcasted_iota(jnp.int32, (LANES, LANES), 0)
        same_head = ((row // HEAD_DIM) == (lane // HEAD_DIM)).astype(BF16)
        dq_ref, dk_ref, dv_ref = o_ref.at[0], o_ref.at[1], o_ref.at[2]
        if has_prev:
            dk_ref[...] = jnp.zeros((T, LANES), F32)
            dv_ref[...] = jnp.zeros((T, LANES), F32)

        heads = (lane_lo, jnp.logical_not(lane_lo))

        def step(it, carry):
            tiles = []
            for u in range(ATTN_BWD_UNROLL):
                idx = it * ATTN_BWD_UNROLL + u
                r = idx // nb
                n = idx % nb
                cur = _band_rows(n * (BAND * d) + r, d)
                prev = _band_rows(jnp.maximum(n - 1, 0) * (BAND * d) + r, d)
                mc, mp = _band_masks(n)
                da = da_ref[cur, :]
                prod = da * a_ref[cur, :]
                hi = prod.astype(BF16)
                lo = (prod - hi.astype(F32)).astype(BF16)
                tiles.append(dict(cur=cur, prev=prev, mc=mc, mp=mp, da=da, hi=hi, lo=lo, q=q_ref[cur, :],
                                  kc=k_ref[cur, :].astype(BF16), vc=v_ref[cur, :].astype(BF16),
                                  kp=k_ref[prev, :].astype(BF16) if has_prev else None,
                                  vp=v_ref[prev, :].astype(BF16) if has_prev else None,
                                  lse=l_ref[cur, :], w=w_ref[cur, :]))
            for t in tiles:
                t["csum"] = _nn(t["hi"], same_head) + _nn(t["lo"], same_head)
            chains = []
            for t in tiles:
                for h, hm in enumerate(heads):
                    qm = jnp.where(hm, t["q"], 0.0).astype(BF16)
                    dam = jnp.where(hm, t["da"], 0.0).astype(BF16)
                    ch = dict(t=t, h=h, qm=qm, dam=dam, sc=jnp.where(t["mc"], _nt(qm, t["kc"]) * SCALE, NEG),
                              dpc=_nt(dam, t["vc"]))
                    if has_prev:
                        ch.update(sp=jnp.where(t["mp"], _nt(qm, t["kp"]) * SCALE, NEG), dpp=_nt(dam, t["vp"]))
                    chains.append(ch)
            for ch in chains:
                t, col0 = ch["t"], ch["h"] * HEAD_DIM
                lse_h = t["lse"][:, col0:col0 + 1]
                w_h = t["w"][:, col0:col0 + 1]
                c_h = t["csum"][:, col0:col0 + 1]
                pwc = w_h * jnp.exp(ch["sc"] - lse_h)
                ch["dsc"] = (pwc * (ch["dpc"] - c_h) * SCALE).astype(BF16)
                ch["pwc"] = pwc.astype(BF16)
                if has_prev:
                    pwp = w_h * jnp.exp(ch["sp"] - lse_h)
                    ch["dsp"] = (pwp * (ch["dpp"] - c_h) * SCALE).astype(BF16)
                    ch["pwp"] = pwp.astype(BF16)
            for ch in chains:
                t = ch["t"]
                ch["dq"] = _nn(ch["dsc"], t["kc"])
                ch["dkc"] = _tn(ch["dsc"], ch["qm"])
                ch["dvc"] = _tn(ch["pwc"], ch["dam"])
                if has_prev:
                    ch["dq"] = ch["dq"] + _nn(ch["dsp"], t["kp"])
                    ch["dkp"] = _tn(ch["dsp"], ch["qm"])
                    ch["dvp"] = _tn(ch["pwp"], ch["dam"])
            for i, t in enumerate(tiles):
                c0, c1 = chains[2 * i], chains[2 * i + 1]
                dq_ref[t["cur"], :] = jnp.where(lane_lo, c0["dq"], c1["dq"])
                if has_prev:
                    dk_ref[t["cur"], :] += c0["dkc"] + c1["dkc"]
                    dk_ref[t["prev"], :] += c0["dkp"] + c1["dkp"]
                    dv_ref[t["cur"], :] += c0["dvc"] + c1["dvc"]
                    dv_ref[t["prev"], :] += c0["dvp"] + c1["dvp"]
                else:
                    dk_ref[t["cur"], :] = c0["dkc"] + c1["dkc"]
                    dv_ref[t["cur"], :] = c0["dvc"] + c1["dvc"]
            return carry

        lax.fori_loop(0, d * nb // ATTN_BWD_UNROLL, step, 0)

    def slab(which):
        return pl.BlockSpec((None, None, T, LANES), lambda p: (which * 3 + g, p, 0, 0))

    per_pair = pl.BlockSpec((None, T, LANES), lambda p: (p, 0, 0))
    return _pcall(
        body, name=f"attn_bwd{g}", grid=(4,),
        in_specs=[slab(0), slab(1), slab(2), per_pair, per_pair, per_pair,
                  pl.BlockSpec((T, LANES), lambda p: (0, CONV_CH // LANES + p)),
                  pl.BlockSpec(memory_space=pl.ANY)],
        out_specs=pl.BlockSpec((None, 3, None, T, LANES), lambda p: (g, 0, p, 0, 0)),
        out_shape=jax.ShapeDtypeStruct((3, 3, 4, T, LANES), F32),
        input_output_aliases={7: 0},
        compiler_params=_params("parallel"),
    )(qkv, qkv, qkv, lse, wgt, att, dcat, dqkv)


def _rope_bwd(dqkv, rope_c, rope_s, dz):
    wide = 4 * LANES

    def body(d_ref, c_ref, s_ref, dz_in, o_ref):
        del dz_in
        w = pl.program_id(1)
        for p in range(4):
            v = d_ref[p]
            rot = v * c_ref[...] + _swap_halves(v * s_ref[...])
            o_ref[:, p * LANES:(p + 1) * LANES] = jnp.where(w < 2, rot, v).astype(BF16)

    tab = pl.BlockSpec((T, LANES), lambda g, w: (0, 0))
    return _pcall(
        body, name="rope_bwd", grid=(3, 3),
        in_specs=[pl.BlockSpec((None, None, 4, T, LANES), lambda g, w: (g, w, 0, 0, 0)), tab, tab,
                  pl.BlockSpec(memory_space=pl.ANY)],
        out_specs=pl.BlockSpec((T, wide), lambda g, w: (0, (2 * CONV_CH) // wide + w * 3 + g)),
        out_shape=jax.ShapeDtypeStruct((T, EVEN_IN), BF16),
        input_output_aliases={3: 0},
        compiler_params=_params("parallel", "parallel"),
    )(dqkv, rope_c, rope_s, dz)


ODD_TILE = 256
ODD_HALO = 8
GELU_C = 0.7978845608028654
GELU_A = 0.044715


def _gelu(x):
    return 0.5 * x * (1.0 + jnp.tanh(GELU_C * (x + GELU_A * x * x * x)))


def _gelu_grad(x):
    th = jnp.tanh(GELU_C * (x + GELU_A * x * x * x))
    return 0.5 * (1.0 + th) + 0.5 * x * (1.0 - th * th) * GELU_C * (1.0 + 3.0 * GELU_A * x * x)


def _tril():
    row = lax.broadcasted_iota(jnp.int32, (CHUNK, CHUNK), 0)
    col = lax.broadcasted_iota(jnp.int32, (CHUNK, CHUNK), 1)
    return (col <= row).astype(F32)


def _odd_parts(z, zh, i, k_ref, g_ref, be_ref, w_ref, bt_ref):
    R, H = ODD_TILE, ODD_HALO
    gb, gc, xs, uv = z[:, :512], z[:, 512:1024], z[:, 1024:1536], z[:, 1536:]
    halo = zh[:, 512:1024] * zh[:, 1024:1536] * (i > 0).astype(F32)
    win = jnp.concatenate([halo, gc * xs], axis=0)
    cv = jnp.zeros((R, 512), F32)
    for j in range(SCONV_W):
        off = H - (SCONV_W - 1) + j
        cv = cv + k_ref[j:j + 1, :] * win[off:off + R, :]
    ge = _gelu(uv)
    u, v = ge[:, :512], ge[:, 512:]
    mu = jnp.mean(v, axis=-1, keepdims=True)
    xc = v - mu
    rstd = lax.rsqrt(jnp.mean(xc * xc, axis=-1, keepdims=True) + EPS)
    xh = xc * rstd
    vn = xh * g_ref[...] + be_ref[...]
    tril = _tril()
    wms = [(w_ref[g] * tril).astype(BF16) for g in range(SG_GROUPS)]
    rows = []
    for ci in range(R // CHUNK):
        blocks = []
        for g in range(SG_GROUPS):
            blk = vn[ci * CHUNK:(ci + 1) * CHUNK, g * LANES:(g + 1) * LANES].astype(BF16)
            blocks.append(_nn(wms[g], blk) + bt_ref[:, g:g + 1])
        rows.append(jnp.concatenate(blocks, axis=1))
    vmix = jnp.concatenate(rows, axis=0)
    return gb, gc, xs, uv, win, cv, u, rstd, xh, vn, vmix, wms


def _odd_mid_fwd(z, conv_k, ln_g, ln_b, sg_w, sg_bt):
    R, H = ODD_TILE, ODD_HALO

    def body(z_ref, zh_ref, k_ref, g_ref, be_ref, w_ref, bt_ref, o_ref):
        i = pl.program_id(0)
        gb, _, _, _, _, cv, u, _, _, _, vmix, _ = _odd_parts(z_ref[...], zh_ref[...], i, k_ref, g_ref, be_ref, w_ref, bt_ref)
        o_ref[...] = jnp.concatenate([gb * cv, u * vmix], axis=1).astype(BF16)

    vec = pl.BlockSpec((1, 512), lambda i: (0, 0))
    return _pcall(
        body, name="odd_mid_fwd", grid=(T // R,),
        in_specs=[pl.BlockSpec((R, ODD_IN), lambda i: (i, 0)),
                  pl.BlockSpec((H, ODD_IN), lambda i: (jnp.maximum(i * (R // H) - 1, 0), 0)),
                  pl.BlockSpec((SCONV_W, 512), lambda i: (0, 0)), vec, vec,
                  pl.BlockSpec((SG_GROUPS, CHUNK, CHUNK), lambda i: (0, 0, 0)),
                  pl.BlockSpec((CHUNK, SG_GROUPS), lambda i: (0, 0))],
        out_specs=pl.BlockSpec((R, D), lambda i: (i, 0)),
        out_shape=jax.ShapeDtypeStruct((T, D), BF16),
        compiler_params=_params("parallel"),
    )(z, z, conv_k, ln_g, ln_b, sg_w, sg_bt)


def _odd_mid_bwd(z, dcat, conv_k, ln_g, ln_b, sg_w, sg_bt):
    R, H = ODD_TILE, ODD_HALO
    last = T // R - 1

    def body(z_ref, zh_ref, zn_ref, d_ref, dn_ref, k_ref, g_ref, be_ref, w_ref, bt_ref,
             dz_ref, dk_ref, dg_ref, dbe_ref, dw_ref, dbt_ref):
        i = pl.program_id(0)
        z = z_ref[...]
        gb, gc, xs, uv, win, cv, u, rstd, xh, vn, vmix, wms = _odd_parts(z, zh_ref[...], i, k_ref, g_ref, be_ref, w_ref, bt_ref)
        dcat_t = d_ref[...]
        dc, dd = dcat_t[:, :512], dcat_t[:, 512:]

        @pl.when(i == 0)
        def _():
            dk_ref[...] = jnp.zeros_like(dk_ref)
            dg_ref[...] = jnp.zeros_like(dg_ref)
            dbe_ref[...] = jnp.zeros_like(dbe_ref)
            dw_ref[...] = jnp.zeros_like(dw_ref)
            dbt_ref[...] = jnp.zeros_like(dbt_ref)

        dgb = dc * cv
        dcv = dc * gb
        nxt = dn_ref[:, :512] * zn_ref[:, :512] * (i < last).astype(F32)
        winb = jnp.concatenate([dcv, nxt], axis=0)
        dp = jnp.zeros((R, 512), F32)
        for j in range(SCONV_W):
            off = H - (SCONV_W - 1) + j
            dk_ref[j:j + 1, :] += jnp.sum(dcv * win[off:off + R, :], axis=0, keepdims=True)
            ob = SCONV_W - 1 - j
            dp = dp + k_ref[j:j + 1, :] * winb[ob:ob + R, :]
        dgc = dp * xs
        dxs = dp * gc
        du = dd * vmix
        dvmix = dd * u
        tril = _tril()
        rows = []
        for ci in range(R // CHUNK):
            blocks = []
            for g in range(SG_GROUPS):
                sl = (slice(ci * CHUNK, (ci + 1) * CHUNK), slice(g * LANES, (g + 1) * LANES))
                dblk = dvmix[sl]
                dblk16 = dblk.astype(BF16)
                blocks.append(_tn(wms[g], dblk16))
                dw_ref[g] += _nt(dblk16, vn[sl].astype(BF16)) * tril
                dbt_ref[:, g:g + 1] += jnp.sum(dblk, axis=1, keepdims=True)
            rows.append(jnp.concatenate(blocks, axis=1))
        dvn = jnp.concatenate(rows, axis=0)
        dg_ref[...] += jnp.sum(dvn * xh, axis=0, keepdims=True)
        dbe_ref[...] += jnp.sum(dvn, axis=0, keepdims=True)
        dxh = dvn * g_ref[...]
        dv = rstd * (dxh - jnp.mean(dxh, axis=-1, keepdims=True) - xh * jnp.mean(dxh * xh, axis=-1, keepdims=True))
        duv = jnp.concatenate([du, dv], axis=1) * _gelu_grad(uv)
        dz_ref[...] = jnp.concatenate([dgb, dgc, dxs, duv], axis=1).astype(BF16)

    vec = pl.BlockSpec((1, 512), lambda i: (0, 0))
    kspec = pl.BlockSpec((SCONV_W, 512), lambda i: (0, 0))
    wspec = pl.BlockSpec((SG_GROUPS, CHUNK, CHUNK), lambda i: (0, 0, 0))
    bspec = pl.BlockSpec((CHUNK, SG_GROUPS), lambda i: (0, 0))
    nxt_blk = lambda i: (jnp.minimum((i + 1) * (R // H), T // H - 1), 0)
    return _pcall(
        body, name="odd_mid_bwd", grid=(T // R,),
        in_specs=[pl.BlockSpec((R, ODD_IN), lambda i: (i, 0)),
                  pl.BlockSpec((H, ODD_IN), lambda i: (jnp.maximum(i * (R // H) - 1, 0), 0)),
                  pl.BlockSpec((H, ODD_IN), nxt_blk),
                  pl.BlockSpec((R, D), lambda i: (i, 0)),
                  pl.BlockSpec((H, D), nxt_blk),
                  kspec, vec, vec, wspec, bspec],
        out_specs=[pl.BlockSpec((R, ODD_IN), lambda i: (i, 0)), kspec, vec, vec, wspec, bspec],
        out_shape=[jax.ShapeDtypeStruct((T, ODD_IN), BF16), jax.ShapeDtypeStruct((SCONV_W, 512), F32),
                   jax.ShapeDtypeStruct((1, 512), F32), jax.ShapeDtypeStruct((1, 512), F32),
                   jax.ShapeDtypeStruct((SG_GROUPS, CHUNK, CHUNK), F32), jax.ShapeDtypeStruct((CHUNK, SG_GROUPS), F32)],
        compiler_params=_params("arbitrary"),
    )(z, z, z, dcat, dcat, conv_k, ln_g, ln_b, sg_w, sg_bt)


def _ffn_up(tag, hn, weight):
    def act(acc):
        r = jnp.maximum(acc, 0.0)
        return (r * r,)

    return _mm(f"ffn{tag}_up", "nn", hn, weight(f"ffn_w1_{tag}", hn), T, D_FF, D, (BF16,), epi=act)


FFN_BWD_TILE = 256


def _ffn_dx(name, dout, w2, w1, f, h, g, tie=None):
    tm = FFN_BWD_TILE
    ties = () if tie is None else (tie,)

    def body(d_ref, w2_ref, w1_ref, f_ref, h_ref, g_ref, *rest):
        du_ref, dh_ref, dg_ref = rest[len(ties):]
        dres = d_ref[...]
        du = (_nt(dres.astype(BF16), w2_ref[...]) * (2.0 * jnp.sqrt(f_ref[...].astype(F32)))).astype(BF16)
        du_ref[...] = du
        dy = _nt(du, w1_ref[...])
        x = h_ref[...]
        r = lax.rsqrt(jnp.mean(x * x, axis=-1, keepdims=True) + EPS)
        nrm = x * r
        dn = dy * g_ref[...]
        dh_ref[...] = dres + r * (dn - nrm * jnp.mean(dn * nrm, axis=-1, keepdims=True))

        @pl.when(pl.program_id(0) == 0)
        def _():
            dg_ref[...] = jnp.zeros_like(dg_ref)

        dg_ref[...] += jnp.sum(dy * nrm, axis=0, keepdims=True)

    row = pl.BlockSpec((tm, D), lambda i: (i, 0))
    wide = pl.BlockSpec((tm, D_FF), lambda i: (i, 0))
    vec = pl.BlockSpec((1, D), lambda i: (0, 0))
    return _pcall(
        body, name=name, grid=(T // tm,),
        in_specs=[row, _resident((D_FF, D)), _resident((D, D_FF)), wide, row, vec]
        + [pl.BlockSpec(TOKEN_SHAPE, lambda i: (0, 0))] * len(ties),
        out_specs=[wide, row, vec],
        out_shape=[jax.ShapeDtypeStruct((T, D_FF), BF16), jax.ShapeDtypeStruct((T, D), F32),
                   jax.ShapeDtypeStruct((1, D), F32)],
        compiler_params=_params("arbitrary"),
    )(dout, w2, w1, f, h, g, *ties)


def _ffn_bwd(tag, h, g, weight, emit, saved, dout, tie=None):
    hn, f = saved
    du, dh, dg = _ffn_dx(f"ffn{tag}_dx", dout, weight(f"ffn_w2_{tag}", dout), weight(f"ffn_w1_{tag}", dout), f, h, g, tie)
    emit(f"ffn_w2_{tag}", f, dout)
    return dh, dg, emit(f"ffn_w1_{tag}", hn, du)


def _rope_tables():
    half = HEAD_DIM // 2
    inv = 10000.0 ** (-jnp.arange(half, dtype=F32) / half)
    ang = jnp.arange(T, dtype=F32)[:, None] * inv[None, :]
    cos, sin = jnp.cos(ang), jnp.sin(ang)
    c = jnp.tile(jnp.concatenate([cos, cos], axis=1), (1, LANES // HEAD_DIM))
    s = jnp.tile(jnp.concatenate([-sin, sin], axis=1), (1, LANES // HEAD_DIM))
    return c, s


def _local_step(x, target, p, weight, emit, emit_small, before=lambda name, after: None):
    rope_c, rope_s = _rope_tables()
    grads = {}

    hn0 = _rms_fwd("mix0_norm", x, p["norm_mix_g0"], tie=p.get("first_norm_after"))
    zc = _mm("even_in_conv", "nn", hn0, weight("even_w_in", hn0), T, 2 * CONV_CH, D, (F32,))
    qkv = _qkv_proj(hn0, weight("even_w_in", hn0), rope_c, rope_s)
    before("econv_fwd", qkv)
    cv, cat0 = _econv_fwd(zc, p["even_conv_k"], p["even_conv_b"], p["even_ln_g"], p["even_ln_b"])
    att_parts = [_attn_fwd(qkv, 0)]
    before("attn_fwd1", att_parts[0][0])
    att_parts += [_attn_fwd(qkv, 1), _attn_fwd(qkv, 2)]
    outs = [a[0] for a in att_parts]
    lses = [a[1] for a in att_parts]
    cat0, att, w0, w1, w2 = _attn_merge(outs, lses, cat0)
    wgts = (w0, w1, w2)
    h1, hnf0 = _mm_out_norm("even_out", cat0, weight("even_w_out", cat0), D, x, p["norm_ffn_g0"])
    f0 = _ffn_up(0, hnf0, weight)
    before("ffn0_down", f0)
    h2, hn1 = _mm_out_norm("ffn0_down", f0, weight("ffn_w2_0", f0), D_FF, h1, p["norm_mix_g1"], tm=FFN0_DOWN_TILE)

    z1 = _mm("odd_in", "nn", hn1, weight("odd_w_in", hn1), T, ODD_IN, D, (F32,))
    before("odd_mid_fwd", z1)
    cat1 = _odd_mid_fwd(z1, p["odd_conv_k"], p["odd_ln_g"], p["odd_ln_b"], p["odd_sg_w"], p["odd_sg_bt"])
    h3, hnf1 = _mm_out_norm("odd_out", cat1, weight("odd_w_out", cat1), D, h2, p["norm_ffn_g1"])
    f1, dh4, grads["final_g"], loss = _ffn_last("ffn1_loss", hnf1, weight("ffn_w1_1", hnf1), weight("ffn_w2_1", hnf1),
                                                h3, p["final_g"], target)

    dh3, grads["norm_ffn_g1"], tok = _ffn_bwd(1, h3, p["norm_ffn_g1"], weight, emit, (hnf1, f1), dh4)
    tok = emit("odd_w_out", cat1, dh3, tie=tok)
    dcat1 = _mm("odd_out_dx", "nt", dh3, weight("odd_w_out", dh3), T, D, D, (F32,), tie=tok)
    dz1, grads["odd_conv_k"], grads["odd_ln_g"], grads["odd_ln_b"], grads["odd_sg_w"], grads["odd_sg_bt"] = _odd_mid_bwd(
        z1, dcat1, p["odd_conv_k"], p["odd_ln_g"], p["odd_ln_b"], p["odd_sg_w"], p["odd_sg_bt"])
    tok = emit("odd_w_in", hn1, dz1)
    dh2, grads["norm_mix_g1"] = _mm_dx_norm("odd_in_dx", dz1, weight("odd_w_in", dz1), ODD_IN, h2, p["norm_mix_g1"],
                                            dh3, tie=tok)

    dh1, grads["norm_ffn_g0"], tok = _ffn_bwd(0, h1, p["norm_ffn_g0"], weight, emit, (hnf0, f0), dh2)
    tok = emit("even_w_out", cat0, dh1, tie=tok)
    dcat0 = _mm("even_out_dx", "nt", dh1, weight("even_w_out", dh1), T, D, D, (F32,), tie=tok)
    dcv, grads["even_ln_g"], grads["even_ln_b"], grads["even_conv_b"] = _econv_bwd_ln(
        cv, dcat0, p["even_ln_g"], p["even_ln_b"])
    dz0, grads["even_conv_k"] = _econv_bwd_conv(dcv, zc, p["even_conv_k"])
    tok = emit_small(loss, grads)
    dqkv = lax.empty((3, 3, 4, T, LANES), F32)
    for g in range(3):
        dqkv = _attn_bwd(qkv, lses[g], wgts[g], att, dcat0, dqkv, g)
    before("rope_bwd", dqkv)
    dz0 = _rope_bwd(dqkv, rope_c, rope_s, dz0)
    tok = emit("even_w_in", hn0, dz0, tie=tok)
    dx, dg0 = _mm_dx_norm("even_in_dx", dz0, weight("even_w_in", dz0), EVEN_IN, x, p["norm_mix_g0"], dh1, tie=tok)
    return dx, dg0


def _rowwise(name, fn, ins, out_dtypes, tm=256, tie=None):
    rows, cols = ins[0].shape
    tm = tm if rows % tm == 0 else rows
    n_in = len(ins)
    ties = () if tie is None else (tie,)

    def body(*refs):
        vals = fn(*[r[...] for r in refs[:n_in]])
        for o_ref, v in zip(refs[n_in + len(ties):], vals):
            o_ref[...] = v.astype(o_ref.dtype)

    spec = pl.BlockSpec((tm, cols), lambda i: (i, 0))
    outs = _pcall(
        body, name=name, grid=(rows // tm,),
        in_specs=[spec] * n_in + [pl.BlockSpec(TOKEN_SHAPE, lambda i: (0, 0))] * len(ties),
        out_specs=[spec] * len(out_dtypes),
        out_shape=[jax.ShapeDtypeStruct((rows, cols), dt) for dt in out_dtypes],
        compiler_params=_params("parallel"),
    )(*ins, *ties)
    return outs[0] if len(out_dtypes) == 1 else outs


def _adamw(name, w, g, m, v, with_grad=False, tie=None):
    c1 = 1.0 - ADAM_B1 ** ADAM_STEP
    c2 = 1.0 - ADAM_B2 ** ADAM_STEP

    def fn(w_t, g_t, m_t, v_t):
        m_new = ADAM_B1 * m_t + (1.0 - ADAM_B1) * g_t
        v_new = ADAM_B2 * v_t + (1.0 - ADAM_B2) * (g_t * g_t)
        delta = -ADAM_LR * ((m_new / c1) / (jnp.sqrt(v_new / c2) + ADAM_EPS) + ADAM_WD * w_t)
        return (delta, m_new, v_new, g_t) if with_grad else (delta, m_new, v_new)

    return _rowwise(name, fn, (w, g, m, v), (F32,) * (4 if with_grad else 3), tie=tie)


class _Piece:
    def __init__(self, name, rows, cols, axis, src, src_row0):
        self.name, self.rows, self.cols, self.axis = name, rows, cols, axis
        self.width = (cols if axis == 1 else rows) // 4
        self.src, self.src_row0 = src, src_row0

    @property
    def full_shape(self):
        return (self.rows, self.cols)

    @property
    def half_shape(self):
        return (self.rows // 2, self.cols) if self.axis == 1 else (self.rows, self.cols // 2)

    @property
    def shard_half_shape(self):
        return (self.rows // 2, self.width) if self.axis == 1 else (self.width, self.cols // 2)

    def shard_whole(self, ref):
        n = self.rows if self.axis == 1 else self.width
        return ref.at[pl.ds(self.src_row0, n), :]

    def shard_half(self, ref, h):
        if self.axis == 1:
            return ref.at[pl.ds(self.src_row0 + h * (self.rows // 2), self.rows // 2), :]
        return ref.at[pl.ds(self.src_row0, self.width), pl.ds(h * (self.cols // 2), self.cols // 2)]

    def full_shard(self, ref, s):
        if self.axis == 1:
            return ref.at[:, pl.ds(s * self.width, self.width)]
        return ref.at[pl.ds(s * self.width, self.width), :]

    def full_shard_half(self, ref, s, h):
        if self.axis == 1:
            return ref.at[pl.ds(h * (self.rows // 2), self.rows // 2), pl.ds(s * self.width, self.width)]
        return ref.at[pl.ds(s * self.width, self.width), pl.ds(h * (self.cols // 2), self.cols // 2)]

    def full_half(self, ref, h):
        if self.axis == 1:
            return ref.at[pl.ds(h * (self.rows // 2), self.rows // 2), :]
        return ref.at[:, pl.ds(h * (self.cols // 2), self.cols // 2)]

    def full_half_rows(self, ref, h, r0, n):
        if self.axis == 1:
            return ref.at[pl.ds(h * (self.rows // 2) + r0, n), :]
        return ref.at[pl.ds(r0, n), pl.ds(h * (self.cols // 2), self.cols // 2)]

    def half_shard(self, ref, s):
        return self.full_shard(ref, s)


PIECES = (
    _Piece("even_w_in", D, EVEN_IN, 1, 0, 0),
    _Piece("even_w_out", D, D, 0, 1, 0),
    _Piece("ffn_w1_0", D, D_FF, 1, 4, 0),
    _Piece("ffn_w2_0", D_FF, D, 0, 5, 0),
    _Piece("odd_w_in", D, ODD_IN, 1, 2, 0),
    _Piece("odd_w_out", D, D, 0, 3, 0),
    _Piece("ffn_w1_1", D, D_FF, 1, 4, D),
    _Piece("ffn_w2_1", D_FF, D, 0, 5, D_FF // 4),
)
N_PIECES = len(PIECES)
FORWARD_RIDES = {"attn_fwd1": (1, 2, 3), "ffn0_down": (4, 5), "odd_mid_fwd": (6, 7)}
JOIN_GROUPS = ((0, 1, 2, 3), (4, 5))
JOIN_RIDES_IN = "rope_bwd"
HOLD_BACK = ("ffn_w2_0", "ffn_w2_1", "odd_w_out")
N_SHARD_OPERANDS = 6
ANY = pl.BlockSpec(memory_space=pl.ANY)
MESH = pl.DeviceIdType.MESH


def _mesh_place():
    x, y, c = lax.axis_index("x"), lax.axis_index("y"), lax.axis_index("c")
    chips = [(1 - x, y), (x, 1 - y), (1 - x, 1 - y)]
    return x, y, c, chips


def _remote(src, dst, send_sem, recv_sem, dev):
    return pltpu.make_async_remote_copy(src_ref=src, dst_ref=dst, send_sem=send_sem, recv_sem=recv_sem,
                                        device_id=dev, device_id_type=MESH)


HBM = pl.BlockSpec(memory_space=pltpu.HBM)
SEM = pl.BlockSpec(memory_space=pltpu.SEMAPHORE)
SPLIT_PARAMS = pltpu.CompilerParams(has_side_effects=pltpu.SideEffectType.DATAFLOW_SIDE_EFFECTING)
CAST_TILE = 256


def _in_hbm(a):
    return pltpu.with_memory_space_constraint(a, pltpu.HBM)


def _cast_place(pc, shard_operand, chip, tie=None):
    rows, cols = (pc.rows, pc.width) if pc.axis == 1 else (pc.width, pc.cols)
    nblk = rows // CAST_TILE
    blk0 = pc.src_row0 // CAST_TILE
    ties = () if tie is None else (tie,)

    def body(chip_ref, x_ref, *rest):
        del chip_ref
        rest[-1][...] = x_ref[...].astype(BF16)

    if pc.axis == 1:
        out_map = lambda i, chip_ref: (i, chip_ref[0])
    else:
        out_map = lambda i, chip_ref: (chip_ref[0] * nblk + i, 0)
    return _pcall(
        body, name=f"cast_{pc.name}",
        grid_spec=pltpu.PrefetchScalarGridSpec(
            num_scalar_prefetch=1, grid=(nblk,),
            in_specs=[pl.BlockSpec((CAST_TILE, cols), lambda i, chip_ref: (blk0 + i, 0))]
            + [pl.BlockSpec(TOKEN_SHAPE, lambda i, chip_ref: (0, 0))] * len(ties),
            out_specs=pl.BlockSpec((CAST_TILE, cols), out_map)),
        out_shape=jax.ShapeDtypeStruct(pc.full_shape, BF16),
        compiler_params=_params("parallel"),
    )(chip, shard_operand, *ties)


def _gather_start(name, pieces, fulls):
    n = len(pieces)

    def body(*refs):
        ins = refs[:n]
        sends = refs[2 * n:3 * n]
        recvs = refs[3 * n:4 * n]
        token = refs[4 * n]
        x, y, c, chips = _mesh_place()
        s = 2 * x + y
        for i, pc in enumerate(pieces):
            win = pc.full_shard_half(ins[i], s, c)
            for k, (cx, cy) in enumerate(chips):
                _remote(win, win, sends[i].at[k], recvs[i].at[k], (cx, cy, c)).start()
        token[...] = jnp.zeros(TOKEN_SHAPE, F32)

    sems = [pltpu.SemaphoreType.DMA((3,))] * (2 * n)
    outs = _pcall(
        body, name=name,
        in_specs=[HBM] * n,
        out_specs=[HBM] * n + [SEM] * (2 * n) + [pl.BlockSpec(memory_space=pltpu.VMEM)],
        out_shape=[pltpu.HBM(pc.full_shape, BF16) for pc in pieces] + sems + [jax.ShapeDtypeStruct(TOKEN_SHAPE, F32)],
        input_output_aliases={i: i for i in range(n)},
        compiler_params=SPLIT_PARAMS,
    )(*[_in_hbm(f) for f in fulls])
    return outs[:n], outs[n:2 * n], outs[2 * n:3 * n], outs[3 * n]


def _gather_wait(pc, full, send_sems, recv_sems, after):
    def body(full_ref, send_ref, recv_ref, after_ref, out_ref):
        del after_ref, out_ref
        x, y, c, chips = _mesh_place()
        for k, (cx, cy) in enumerate(chips):
            win = pc.full_shard_half(full_ref, 2 * cx + cy, c)
            cp = _remote(win, win, send_ref.at[k], recv_ref.at[k], (cx, cy, c))
            cp.wait_send()
            cp.wait_recv()

    return _pcall(
        body, name=f"gather_wait_{pc.name}",
        in_specs=[HBM, SEM, SEM, ANY], out_specs=HBM, out_shape=pltpu.HBM(pc.full_shape, BF16),
        input_output_aliases={0: 0}, compiler_params=SPLIT_PARAMS,
    )(full, send_sems, recv_sems, after)


def _core_forward_job(pieces, fulls):
    n = len(pieces)

    def copies(ins, outs, scr):
        send_bufs, recv_bufs = scr[:n], scr[n:2 * n]
        load_sems, send_sems, recv_sems, store_sems = scr[2 * n:]
        x, y, c, chips = _mesh_place()
        loads, sends, stores = [], [], []
        for i, pc in enumerate(pieces):
            for k, (cx, cy) in enumerate(chips):
                j = 3 * i + k
                loads.append(pltpu.make_async_copy(pc.full_shard_half(ins[i], 2 * cx + cy, c), send_bufs[i].at[k],
                                                   load_sems.at[j]))
                sends.append(_remote(send_bufs[i].at[k], recv_bufs[i].at[k], send_sems.at[j], recv_sems.at[j],
                                     (x, y, 1 - c)))
                stores.append(pltpu.make_async_copy(recv_bufs[i].at[k], pc.full_shard_half(outs[i], 2 * cx + cy, 1 - c),
                                                    store_sems.at[j]))
        return loads, sends, stores

    def begin(ins, outs, scr):
        for cp in copies(ins, outs, scr)[0]:
            cp.start()

    def advance(ins, outs, scr):
        loads, sends, _ = copies(ins, outs, scr)
        for load, send in zip(loads, sends):
            load.wait()
            send.start()

    def finish(ins, outs, scr):
        _, sends, stores = copies(ins, outs, scr)
        for send, store in zip(sends, stores):
            send.wait_recv()
            store.start()
        for send, store in zip(sends, stores):
            send.wait_send()
            store.wait()

    sems = pltpu.SemaphoreType.DMA((3 * n,))
    bufs = [pltpu.VMEM((3,) + pc.shard_half_shape, BF16) for pc in pieces]
    return _SideJob(fulls, [jax.ShapeDtypeStruct(pc.full_shape, BF16) for pc in pieces],
                    bufs + bufs + [sems, sems, sems, sems], {i: i for i in range(n)}, begin, advance, finish)


def _run_job(name, job):
    def body(o_ref):
        o_ref[...] = jnp.zeros(TOKEN_SHAPE, F32)

    _ride_next_call(job)
    _pcall(body, name=name, grid=(3,), in_specs=[], out_specs=pl.BlockSpec(TOKEN_SHAPE, lambda i: (0, 0)),
           out_shape=jax.ShapeDtypeStruct(TOKEN_SHAPE, F32))()
    return job.results


def _core_forward(pieces, fulls):
    n = len(pieces)

    def body(*refs):
        ins, outs = refs[:n], refs[n:2 * n]
        send_bufs, recv_bufs = refs[2 * n:3 * n], refs[3 * n:4 * n]
        load_sems, send_sems, recv_sems, store_sems = refs[4 * n:]
        x, y, c, chips = _mesh_place()
        loads, sends, stores = [], [], []
        for i, pc in enumerate(pieces):
            for k, (cx, cy) in enumerate(chips):
                cp = pltpu.make_async_copy(pc.full_shard_half(ins[i], 2 * cx + cy, c), send_bufs[i].at[k],
                                           load_sems.at[3 * i + k])
                cp.start()
                loads.append(cp)
        _sibling_handshake()
        for i in range(n):
            for k in range(3):
                j = 3 * i + k
                loads[j].wait()
                cp = _remote(send_bufs[i].at[k], recv_bufs[i].at[k], send_sems.at[j], recv_sems.at[j], (x, y, 1 - c))
                cp.start()
                sends.append(cp)
        for i, pc in enumerate(pieces):
            for k, (cx, cy) in enumerate(chips):
                j = 3 * i + k
                sends[j].wait_recv()
                cp = pltpu.make_async_copy(recv_bufs[i].at[k], pc.full_shard_half(outs[i], 2 * cx + cy, 1 - c),
                                           store_sems.at[j])
                cp.start()
                stores.append(cp)
        for j in range(3 * n):
            sends[j].wait_send()
            stores[j].wait()

    sems = pltpu.SemaphoreType.DMA((3 * n,))
    bufs = [pltpu.VMEM((3,) + pc.shard_half_shape, BF16) for pc in pieces]
    return _pcall(
        body, name="core_forward_" + pieces[0].name, in_specs=[ANY] * n, out_specs=[ANY] * n,
        out_shape=[jax.ShapeDtypeStruct(pc.full_shape, BF16) for pc in pieces],
        scratch_shapes=bufs + bufs + [sems, sems, sems, sems],
        input_output_aliases={i: i for i in range(n)},
        compiler_params=pltpu.CompilerParams(vmem_limit_bytes=VMEM_LIMIT, collective_id=SIBLING_COLLECTIVE_ID),
    )(*fulls)


def _dw_tile(pc):
    if pc.axis == 1:
        tn = max(t for t in range(LANES, pc.cols + 1, LANES) if pc.cols % t == 0 and t <= 1408)
        return pc.rows // 2, tn
    return min(pc.rows, 1024), pc.cols // 2


def _mm_dw_chipsum(pc, a, b, core, tie=None):
    tm, tn = _dw_tile(pc)
    hr, hc = pc.half_shape
    tiles_r, tiles_c = hr // tm, hc // tn
    th = tiles_r * tiles_c
    ties = () if tie is None else (tie,)

    def tile_of(s, core_ref):
        mine = s >= th
        half = jnp.where(mine, core_ref[0], 1 - core_ref[0])
        local = s % th
        li, lj = local // tiles_c, local % tiles_c
        if pc.axis == 1:
            return half * tiles_r + li, lj, li, lj, mine
        return li, half * tiles_c + lj, li, lj, mine

    def body(core_ref, a_ref, b_ref, *rest):
        o_ref, send_buf, recv_buf, send_sems, recv_sems = rest[len(ties):]
        s = pl.program_id(0)
        local = s % th
        x, y, c = lax.axis_index("x"), lax.axis_index("y"), lax.axis_index("c")
        acc = _tn(a_ref[...].astype(BF16), b_ref[...].astype(BF16))

        def push(slot):
            return _remote(send_buf.at[slot], recv_buf.at[slot], send_sems.at[slot], recv_sems.at[slot], (x, y, 1 - c))

        @pl.when(s == 0)
        def _():
            _sibling_handshake()

        @pl.when(s < th)
        def _():
            send_buf[local] = acc.astype(BF16)
            push(local).start()

        @pl.when(s >= th)
        def _():
            push(local).wait_recv()
            o_ref[...] = (acc + recv_buf[local].astype(F32)).astype(BF16)

        @pl.when(s == 2 * th - 1)
        def _():
            for slot in range(th):
                push(slot).wait_send()

    def a_map(s, core_ref):
        return 0, tile_of(s, core_ref)[0]

    def b_map(s, core_ref):
        return 0, tile_of(s, core_ref)[1]

    def o_map(s, core_ref):
        _, _, li, lj, mine = tile_of(s, core_ref)
        return jnp.where(mine, li, 0), jnp.where(mine, lj, 0)

    return _pcall(
        body, name=f"dw_{pc.name}",
        grid_spec=pltpu.PrefetchScalarGridSpec(
            num_scalar_prefetch=1, grid=(2 * th,),
            in_specs=[pl.BlockSpec((T, tm), a_map), pl.BlockSpec((T, tn), b_map)]
            + [pl.BlockSpec(TOKEN_SHAPE, lambda s, core_ref: (0, 0))] * len(ties),
            out_specs=pl.BlockSpec((tm, tn), o_map),
            scratch_shapes=[pltpu.VMEM((th, tm, tn), BF16), pltpu.VMEM((th, tm, tn), BF16),
                            pltpu.SemaphoreType.DMA((th,)), pltpu.SemaphoreType.DMA((th,))]),
        out_shape=jax.ShapeDtypeStruct((hr, hc), BF16),
        compiler_params=pltpu.CompilerParams(dimension_semantics=("arbitrary",), vmem_limit_bytes=VMEM_LIMIT,
                                             collective_id=SIBLING_COLLECTIVE_ID),
    )(core, a, b, *ties)


def _scatter_start(pieces, chip_sums):
    n = len(pieces)

    def body(*refs):
        sums, lands = refs[:n], refs[n:2 * n]
        sends, recvs = refs[4 * n:5 * n], refs[5 * n:6 * n]
        token = refs[6 * n]
        x, y, c, chips = _mesh_place()
        for i, pc in enumerate(pieces):
            for k, (cx, cy) in enumerate(chips):
                _remote(pc.half_shard(sums[i], 2 * cx + cy), lands[i].at[k], sends[i].at[k], recvs[i].at[k],
                        (cx, cy, c)).start()
        token[...] = jnp.zeros(TOKEN_SHAPE, F32)

    land_shapes = [(3,) + pc.shard_half_shape for pc in pieces]
    sems = [pltpu.SemaphoreType.DMA((3,))] * (2 * n)
    outs = _pcall(
        body, name="scatter_start_" + pieces[0].name,
        in_specs=[HBM] * (2 * n), out_specs=[HBM] * (2 * n) + [SEM] * (2 * n) + [pl.BlockSpec(memory_space=pltpu.VMEM)],
        out_shape=[pltpu.HBM(pc.half_shape, BF16) for pc in pieces] + [pltpu.HBM(sh, BF16) for sh in land_shapes]
        + sems + [jax.ShapeDtypeStruct(TOKEN_SHAPE, F32)],
        input_output_aliases={i: i for i in range(2 * n)}, compiler_params=SPLIT_PARAMS,
    )(*[_in_hbm(cs) for cs in chip_sums], *[_in_hbm(lax.empty(sh, BF16)) for sh in land_shapes])
    return [(outs[i], outs[n + i], outs[2 * n + i], outs[3 * n + i]) for i in range(n)], outs[4 * n]


def _scatter_wait(pc, chip_sum, land, send_sems, recv_sems, after):
    def body(sum_ref, land_ref, send_ref, recv_ref, after_ref, sum_out, land_out):
        del after_ref, sum_out, land_out
        x, y, c, chips = _mesh_place()
        for k, (cx, cy) in enumerate(chips):
            cp = _remote(pc.half_shard(sum_ref, 2 * cx + cy), land_ref.at[k], send_ref.at[k], recv_ref.at[k], (cx, cy, c))
            cp.wait_send()
            cp.wait_recv()

    return _pcall(
        body, name=f"scatter_wait_{pc.name}",
        in_specs=[HBM, HBM, SEM, SEM, ANY], out_specs=[HBM, HBM],
        out_shape=[pltpu.HBM(pc.half_shape, BF16), pltpu.HBM((3,) + pc.shard_half_shape, BF16)],
        input_output_aliases={0: 0, 1: 1}, compiler_params=SPLIT_PARAMS,
    )(chip_sum, land, send_sems, recv_sems, after)


SHARD_OPERAND_SHAPES = ((D, EVEN_IN // 4), (D // 4, D), (D, ODD_IN // 4), (D // 4, D), (2 * D, D_FF // 4), (2 * D_FF // 4, D))


def _allsum_join_job(operands, chip_sums, lands):
    pieces = [pc for pc in PIECES if pc.src in operands]
    n = len(pieces)

    def copies(ins, outs, scr):
        sum_refs, land_refs = ins[:n], ins[n:]
        out_refs = dict(zip(operands, outs))
        in_bufs, fin_bufs, recv_bufs = scr[:n], scr[n:2 * n], scr[2 * n:3 * n]
        load_sems, send_sems, recv_sems, out_sems = scr[3 * n:]
        x, y, c, _ = _mesh_place()
        s = 2 * x + y
        loads, sends, mine, theirs = [], [], [], []
        for j, pc in enumerate(pieces):
            loads.append((pltpu.make_async_copy(land_refs[j], in_bufs[j].at[pl.ds(0, 3)], load_sems.at[2 * j]),
                          pltpu.make_async_copy(pc.half_shard(sum_refs[j], s), in_bufs[j].at[3], load_sems.at[2 * j + 1])))
            sends.append(_remote(fin_bufs[j], recv_bufs[j], send_sems.at[j], recv_sems.at[j], (x, y, 1 - c)))
            mine.append(pltpu.make_async_copy(fin_bufs[j], pc.shard_half(out_refs[pc.src], c), out_sems.at[2 * j]))
            theirs.append(pltpu.make_async_copy(recv_bufs[j], pc.shard_half(out_refs[pc.src], 1 - c), out_sems.at[2 * j + 1]))
        return loads, sends, mine, theirs, in_bufs, fin_bufs

    def begin(ins, outs, scr):
        for a, b in copies(ins, outs, scr)[0]:
            a.start()
            b.start()

    def advance(ins, outs, scr):
        loads, sends, mine, _, in_bufs, fin_bufs = copies(ins, outs, scr)
        for j in range(n):
            loads[j][0].wait()
            loads[j][1].wait()
            acc = in_bufs[j][0].astype(F32)
            for k in range(1, 4):
                acc = acc + in_bufs[j][k].astype(F32)
            fin_bufs[j][...] = acc
            mine[j].start()
            sends[j].start()

    def finish(ins, outs, scr):
        _, sends, mine, theirs, _, _ = copies(ins, outs, scr)
        for j in range(n):
            sends[j].wait_recv()
            theirs[j].start()
        for j in range(n):
            sends[j].wait_send()
            mine[j].wait()
            theirs[j].wait()

    halves = [pc.shard_half_shape for pc in pieces]
    scratch = ([pltpu.VMEM((4,) + sh, BF16) for sh in halves] + [pltpu.VMEM(sh, F32) for sh in halves] * 2
               + [pltpu.SemaphoreType.DMA((2 * n,)), pltpu.SemaphoreType.DMA((n,)), pltpu.SemaphoreType.DMA((n,)),
                  pltpu.SemaphoreType.DMA((2 * n,))])
    return _SideJob(list(chip_sums) + list(lands), [jax.ShapeDtypeStruct(SHARD_OPERAND_SHAPES[o], F32) for o in operands],
                    scratch, {}, begin, advance, finish)


PEER_FLIPS = tuple((a, b, e) for a in (0, 1) for b in (0, 1) for e in (0, 1) if (a, b, e) != (0, 0, 0))


def _peers():
    x, y, c = lax.axis_index("x"), lax.axis_index("y"), lax.axis_index("c")
    me = 4 * x + 2 * y + c
    out = []
    for a, b, e in PEER_FLIPS:
        px, py, pc = (1 - x if a else x), (1 - y if b else y), (1 - c if e else c)
        out.append(((px, py, pc), 4 * px + 2 * py + pc))
    return me, out


def _exchange8_start(name, blk, tie=None):
    m = blk.shape[0]
    ties = () if tie is None else (tie,)

    def body(blk_ref, land_ref, *rest):
        sends, recvs, token = rest[len(ties) + 2:]
        me, peers = _peers()
        for k, (dev, _) in enumerate(peers):
            _remote(blk_ref, land_ref.at[me], sends.at[k], recvs.at[k], dev).start()
        token[...] = jnp.zeros(TOKEN_SHAPE, F32)

    sems = pltpu.SemaphoreType.DMA((7,))
    return _pcall(
        body, name=name,
        in_specs=[HBM, HBM] + [pl.BlockSpec(memory_space=pltpu.VMEM)] * len(ties),
        out_specs=[HBM, HBM, SEM, SEM, pl.BlockSpec(memory_space=pltpu.VMEM)],
        out_shape=[pltpu.HBM((m, LANES), F32), pltpu.HBM((8, m, LANES), F32), sems, sems,
                   jax.ShapeDtypeStruct(TOKEN_SHAPE, F32)],
        input_output_aliases={0: 0, 1: 1}, compiler_params=SPLIT_PARAMS,
    )(_in_hbm(blk), _in_hbm(lax.empty((8, m, LANES), F32)), *ties)


def _exchange8_wait(name, blk, land, send_sems, recv_sems, after):
    def body(blk_ref, land_ref, send_ref, recv_ref, after_ref, blk_out, land_out):
        del after_ref, blk_out, land_out
        _, peers = _peers()
        for k, (dev, slot) in enumerate(peers):
            cp = _remote(blk_ref, land_ref.at[slot], send_ref.at[k], recv_ref.at[k], dev)
            cp.wait_send()
            cp.wait_recv()

    m = blk.shape[0]
    return _pcall(
        body, name=name,
        in_specs=[HBM, HBM, SEM, SEM, ANY], out_specs=[HBM, HBM],
        out_shape=[pltpu.HBM((m, LANES), F32), pltpu.HBM((8, m, LANES), F32)],
        input_output_aliases={0: 0, 1: 1}, compiler_params=SPLIT_PARAMS,
    )(blk, land, send_sems, recv_sems, after)


def _collect8(name, blk, land, with_sum):
    m = blk.shape[0]

    def body(blk_ref, land_ref, out_ref, *scratch):
        sems = scratch[-1]
        dst = scratch[0] if with_sum else out_ref
        me, peers = _peers()
        copies = [pltpu.make_async_copy(blk_ref, dst.at[me], sems.at[7])]
        for k, (_, slot) in enumerate(peers):
            copies.append(pltpu.make_async_copy(land_ref.at[slot], dst.at[slot], sems.at[k]))
        for cp in copies:
            cp.start()
        for cp in copies:
            cp.wait()
        if with_sum:
            acc = dst[0]
            for dev in range(1, 8):
                acc = acc + dst[dev]
            out_ref[...] = acc

    all_shape = (8, m, LANES)
    return _pcall(
        body, name=name, in_specs=[ANY, ANY], out_specs=pl.BlockSpec(memory_space=pltpu.VMEM),
        out_shape=jax.ShapeDtypeStruct((m, LANES) if with_sum else all_shape, F32),
        scratch_shapes=([pltpu.VMEM(all_shape, F32)] if with_sum else []) + [pltpu.SemaphoreType.DMA((8,))],
    )(blk, land)


def _pack(arrays, row_counts):
    rows = []
    for a, n in zip(arrays, row_counts):
        flat = a.reshape(-1, LANES)
        rows.append(jnp.pad(flat, ((0, n - flat.shape[0]), (0, 0))))
    return jnp.concatenate(rows, axis=0)


REPL_NAMES = ("norm_mix_g", "norm_ffn_g", "even_conv_b", "even_ln_g", "even_ln_b", "odd_sg_w", "odd_sg_b", "final_g")
REPL_SHAPES = ((2, D), (2, D), (1, 512), (1, 512), (1, 512), (1, SG_GROUPS, CHUNK, CHUNK), (1, SG_GROUPS, CHUNK), (D,))
REPL_ROWS = (16, 16, 8, 8, 8, 512, 8, 8)
SHARDED_NAMES = ("even_conv_k", "odd_conv_k", "odd_ln_g", "odd_ln_b")
SHARDED_SHARD_SHAPES = ((1, CONV_W, LANES), (1, SCONV_W, LANES), (1, LANES), (1, LANES))
SHARDED_SHARD_ROWS = (32, 8, 8, 8)
SHARDED_FULL_SHAPES = ((CONV_W, 512), (SCONV_W, 512), (1, 512), (1, 512))
SHARDED_FULL_ROWS = (128, 16, 8, 8)
SMALL_NAMES = REPL_NAMES + SHARDED_NAMES
SMALL_ROWS = REPL_ROWS + SHARDED_SHARD_ROWS
SMALL_OUT_SHAPES = REPL_SHAPES[:-1] + ((1, D),) + SHARDED_SHARD_SHAPES
LOSS_ROWS = 8


def _offsets(rows):
    out, r0 = [], 0
    for n in rows:
        out.append(r0)
        r0 += n
    return out


def _adamw_small(w_pack, m_pack, v_pack, grad_sum, first_gain_sum):
    n_rows = sum(SMALL_ROWS)
    state_at = _offsets(SMALL_ROWS)
    grad_at = _offsets((LOSS_ROWS, 8) + REPL_ROWS[1:] + SHARDED_FULL_ROWS)[1:]
    c1 = 1.0 - ADAM_B1 ** ADAM_STEP
    c2 = 1.0 - ADAM_B2 ** ADAM_STEP
    n_repl = len(REPL_NAMES)

    def body(w_ref, m_ref, v_ref, g_ref, g0_ref, *rest):
        outs, gbuf = rest[:-1], rest[-1]
        chip = 2 * lax.axis_index("x") + lax.axis_index("y")
        gbuf[...] = jnp.zeros((n_rows, LANES), F32)
        gbuf[0:8, :] = g0_ref[...]
        gbuf[8:16, :] = g_ref[grad_at[0]:grad_at[0] + 8, :]
        for i in range(1, n_repl):
            gbuf[state_at[i]:state_at[i] + REPL_ROWS[i], :] = g_ref[grad_at[i]:grad_at[i] + REPL_ROWS[i], :]
        for k, shape in enumerate(SHARDED_SHARD_SHAPES):
            used = shape[-2] if len(shape) == 3 else 1
            src = pl.ds(grad_at[n_repl + k] + chip, used, stride=4) if used > 1 else pl.ds(grad_at[n_repl + k] + chip, 1)
            gbuf[state_at[n_repl + k]:state_at[n_repl + k] + used, :] = g_ref[src, :]
        g = gbuf[...]
        m_new = ADAM_B1 * m_ref[...] + (1.0 - ADAM_B1) * g
        v_new = ADAM_B2 * v_ref[...] + (1.0 - ADAM_B2) * (g * g)
        delta = -ADAM_LR * ((m_new / c1) / (jnp.sqrt(v_new / c2) + ADAM_EPS) + ADAM_WD * w_ref[...])
        for i, shape in enumerate(SMALL_OUT_SHAPES):
            for j, val in enumerate((g, delta, m_new, v_new)):
                o_ref = outs[4 * i + j]
                rows = val[state_at[i]:state_at[i] + SMALL_ROWS[i], :]
                if len(shape) == 2 and shape[1] > LANES:
                    per = shape[1] // LANES
                    for r in range(shape[0]):
                        for q in range(per):
                            o_ref[r:r + 1, q * LANES:(q + 1) * LANES] = rows[r * per + q:r * per + q + 1, :]
                elif len(shape) == 4:
                    for grp in range(shape[1]):
                        o_ref[0, grp] = rows[grp * shape[2]:(grp + 1) * shape[2], :]
                elif len(shape) == 3:
                    o_ref[0] = rows[:shape[1], :]
                else:
                    o_ref[...] = rows[:1, :]

    vm = pl.BlockSpec(memory_space=pltpu.VMEM)
    out_shape = [jax.ShapeDtypeStruct(sh, F32) for sh in SMALL_OUT_SHAPES for _ in range(4)]
    outs = _pcall(body, name="adamw_small", in_specs=[vm] * 5, out_specs=[vm] * len(out_shape), out_shape=out_shape,
                  scratch_shapes=[pltpu.VMEM((n_rows, LANES), F32)])(w_pack, m_pack, v_pack, grad_sum, first_gain_sum)
    return {n: outs[4 * i:4 * i + 4] for i, n in enumerate(SMALL_NAMES)}


def _touch(arrays):
    n = len(arrays)

    def body(*refs):
        refs[-1][...] = jnp.zeros(TOKEN_SHAPE, F32)

    outs = _pcall(
        body, name="touch", in_specs=[ANY] * n, out_specs=[ANY] * n + [pl.BlockSpec(memory_space=pltpu.VMEM)],
        out_shape=[jax.ShapeDtypeStruct(a.shape, a.dtype) for a in arrays] + [jax.ShapeDtypeStruct(TOKEN_SHAPE, F32)],
        input_output_aliases={i: i for i in range(n)})(*arrays)
    return outs[:n], outs[n]


def kernel(x, norm_mix_g, norm_ffn_g, even_w_in, even_conv_k, even_conv_b, even_ln_g, even_ln_b, even_w_out, odd_w_in, odd_conv_k, odd_ln_g, odd_ln_b, odd_sg_w, odd_sg_b, odd_w_out, ffn_w1, ffn_w2, final_g, loss_target, m_norm_mix_g, m_norm_ffn_g, m_even_w_in, m_even_conv_k, m_even_conv_b, m_even_ln_g, m_even_ln_b, m_even_w_out, m_odd_w_in, m_odd_conv_k, m_odd_ln_g, m_odd_ln_b, m_odd_sg_w, m_odd_sg_b, m_odd_w_out, m_ffn_w1, m_ffn_w2, m_final_g, v_norm_mix_g, v_norm_ffn_g, v_even_w_in, v_even_conv_k, v_even_conv_b, v_even_ln_g, v_even_ln_b, v_even_w_out, v_odd_w_in, v_odd_conv_k, v_odd_ln_g, v_odd_ln_b, v_odd_sg_w, v_odd_sg_b, v_odd_w_out, v_ffn_w1, v_ffn_w2, v_final_g):
    names = ("norm_mix_g", "norm_ffn_g", "even_w_in", "even_conv_k", "even_conv_b", "even_ln_g", "even_ln_b", "even_w_out",
             "odd_w_in", "odd_conv_k", "odd_ln_g", "odd_ln_b", "odd_sg_w", "odd_sg_b", "odd_w_out", "ffn_w1", "ffn_w2", "final_g")
    w = dict(zip(names, (norm_mix_g, norm_ffn_g, even_w_in, even_conv_k, even_conv_b, even_ln_g, even_ln_b, even_w_out,
                         odd_w_in, odd_conv_k, odd_ln_g, odd_ln_b, odd_sg_w, odd_sg_b, odd_w_out, ffn_w1, ffn_w2, final_g)))
    mom = dict(zip(names, (m_norm_mix_g, m_norm_ffn_g, m_even_w_in, m_even_conv_k, m_even_conv_b, m_even_ln_g, m_even_ln_b,
                           m_even_w_out, m_odd_w_in, m_odd_conv_k, m_odd_ln_g, m_odd_ln_b, m_odd_sg_w, m_odd_sg_b, m_odd_w_out,
                           m_ffn_w1, m_ffn_w2, m_final_g)))
    vel = dict(zip(names, (v_norm_mix_g, v_norm_ffn_g, v_even_w_in, v_even_conv_k, v_even_conv_b, v_even_ln_g, v_even_ln_b,
                           v_even_w_out, v_odd_w_in, v_odd_conv_k, v_odd_ln_g, v_odd_ln_b, v_odd_sg_w, v_odd_sg_b, v_odd_w_out,
                           v_ffn_w1, v_ffn_w2, v_final_g)))
    big_names = ("even_w_in", "even_w_out", "odd_w_in", "odd_w_out", "ffn_w1", "ffn_w2")
    chip = 2 * lax.axis_index("x") + lax.axis_index("y")

    def shard2d(t, name):
        return t[name].reshape(SHARD_OPERAND_SHAPES[big_names.index(name)])

    chip_op = jnp.reshape(chip, (1,)).astype(jnp.int32)
    first = _cast_place(PIECES[0], shard2d(w, big_names[PIECES[0].src]), chip_op)
    fly0, send0, recv0, token = _gather_start("gather_start_first", PIECES[:1], [first])
    small_pack = _pack([w[n] for n in SHARDED_NAMES], SHARDED_SHARD_ROWS)
    small_blk, small_land, small_send, small_recv, small_token = _exchange8_start("gather_small_start", small_pack, token)
    placed = [_cast_place(pc, shard2d(w, big_names[pc.src]), chip_op, tie=small_token) for pc in PIECES[1:]]
    fly1, send1, recv1, all_started = _gather_start("gather_start_rest", PIECES[1:], placed)
    flying, gather_send, gather_recv = fly0 + fly1, send0 + send1, recv0 + recv1
    ready = {}

    names_in_order = [pc.name for pc in PIECES]

    riding = {}

    (*state_packs, _), idle_work_done = _touch(
        [_pack([t[n] for n in SMALL_NAMES], SMALL_ROWS) for t in (w, mom, vel)] + [all_started])

    def weight(name, after):
        if name in riding:
            job, k = riding.pop(name)
            ready[name] = job.results[k]
        if name not in ready:
            landed = _gather_wait(PIECES[0], flying[0], gather_send[0], gather_recv[0], after)
            ready[name], = _core_forward(PIECES[:1], [landed])
        return ready[name]

    def before(call, after):
        if call in FORWARD_RIDES:
            group = FORWARD_RIDES[call]
            landed = [_gather_wait(PIECES[j], flying[j], gather_send[j], gather_recv[j], after) for j in group]
            job = _core_forward_job([PIECES[j] for j in group], landed)
            riding.update((PIECES[j].name, (job, k)) for k, j in enumerate(group))
            _ride_next_call(job)
        elif call == "econv_fwd":
            p.update(small_sharded(after))
        elif call == JOIN_RIDES_IN:
            early_join.append(join_job(JOIN_GROUPS[1], after))
            _ride_next_call(early_join[0])

    early_join = []

    def join_job(operands, after):
        pieces = [pc for pc in PIECES if pc.src in operands]
        done = {}
        for entry in list(scattering):
            if entry[0] in pieces:
                done[entry[0].name] = _scatter_wait(*entry, after)
                scattering.remove(entry)
        return _allsum_join_job(operands, [done[pc.name][0] for pc in pieces], [done[pc.name][1] for pc in pieces])

    scattering = []
    held = []

    core_op = jnp.reshape(lax.axis_index("c"), (1,)).astype(jnp.int32)

    def emit(name, a, b, tie=None):
        pc = PIECES[names_in_order.index(name)]
        held.append((pc, _mm_dw_chipsum(pc, a, b, core_op, tie)))
        if name in HOLD_BACK:
            return None
        pieces = [pc for pc, _ in held]
        started, token = _scatter_start(pieces, [chip_sum for _, chip_sum in held])
        scattering.extend((pc,) + tuple(st) for pc, st in zip(pieces, started))
        held.clear()
        return token

    def small_sharded(after):
        full = {}
        blk, land = _exchange8_wait("gather_small_wait", small_blk, small_land, small_send, small_recv, after)
        gathered = _collect8("gather_small_collect", blk, land, False)
        gathered = gathered.reshape(4, 2, sum(SHARDED_SHARD_ROWS), LANES)[:, 0]
        r0 = 0
        for n, sh, rows, full_sh in zip(SHARDED_NAMES, SHARDED_SHARD_SHAPES, SHARDED_SHARD_ROWS, SHARDED_FULL_SHAPES):
            per_chip = gathered[:, r0:r0 + rows].reshape(4, -1)[:, :full_sh[0] * LANES].reshape(4, full_sh[0], LANES)
            full[n] = jnp.transpose(per_chip, (1, 0, 2)).reshape(full_sh)
            r0 += rows
        return full

    p = {}
    p.update(norm_mix_g0=norm_mix_g[0:1], norm_mix_g1=norm_mix_g[1:2], norm_ffn_g0=norm_ffn_g[0:1], norm_ffn_g1=norm_ffn_g[1:2],
             even_conv_b=even_conv_b, even_ln_g=even_ln_g, even_ln_b=even_ln_b,
             odd_sg_w=odd_sg_w[0], odd_sg_bt=odd_sg_b[0].T, final_g=final_g[None, :],
             first_norm_after=idle_work_done)

    small = {}

    def emit_small(loss_row, g):
        parts = [loss_row, g["norm_mix_g1"], g["norm_ffn_g0"], g["norm_ffn_g1"], g["even_conv_b"], g["even_ln_g"],
                 g["even_ln_b"], g["odd_sg_w"], g["odd_sg_bt"].T, g["final_g"],
                 g["even_conv_k"], g["odd_conv_k"], g["odd_ln_g"], g["odd_ln_b"]]
        pack = _pack(parts, (8, 8, 8, 8) + REPL_ROWS[2:] + SHARDED_FULL_ROWS)
        small["blk"], small["land"], small["send"], small["recv"], token = _exchange8_start("allreduce_small_start", pack)
        return token

    dx, dg0 = _local_step(x[0], loss_target[0], p, weight, emit, emit_small, before)
    last_blk, last_land, last_send, last_recv, grad_token = _exchange8_start("allreduce_last_start", _pack([dg0], (8,)))

    big_grads = dict(zip((big_names[o] for o in JOIN_GROUPS[1]), early_join[0].results))
    delta, new_m, new_v, grads_big = {}, {}, {}, {}

    def adamw_big(n):
        d2, m2, v2, g2 = _adamw(f"adamw_{n}", shard2d(w, n), big_grads[n], shard2d(mom, n), shard2d(vel, n), True,
                                tie=grad_token)
        delta[n], new_m[n], new_v[n], grads_big[n] = (t.reshape(w[n].shape) for t in (d2, m2, v2, g2))

    for o in JOIN_GROUPS[1]:
        adamw_big(big_names[o])
    late_join = join_job(JOIN_GROUPS[0], new_v[big_names[JOIN_GROUPS[1][-1]]])
    big_grads.update(zip((big_names[o] for o in JOIN_GROUPS[0]), _run_job("allsum_join_late", late_join)))
    for o in JOIN_GROUPS[0]:
        adamw_big(big_names[o])

    joined_last = big_grads[big_names[JOIN_GROUPS[0][-1]]]
    grad_blk, grad_land = _exchange8_wait("allreduce_small_wait", small["blk"], small["land"], small["send"],
                                          small["recv"], joined_last)
    grad_sum = _collect8("allreduce_small_sum", grad_blk, grad_land, True)
    last_blk, last_land = _exchange8_wait("allreduce_last_wait", last_blk, last_land, last_send, last_recv, joined_last)
    dg0_sum = _collect8("allreduce_last_sum", last_blk, last_land, True)
    loss = grad_sum[0, 0]

    grads = dict(grads_big)
    for n, results in _adamw_small(*state_packs, grad_sum, dg0_sum).items():
        grads[n], delta[n], new_m[n], new_v[n] = (t.reshape(w[n].shape) for t in results)

    out = [loss, dx[None]]
    for res in (grads, delta, new_m, new_v):
        out.extend(res[n] for n in names)
    return tuple(out)
```

```python
import functools

import jax
import jax.numpy as jnp
from jax import lax
from jax.experimental import pallas as pl
from jax.experimental.pallas import tpu as pltpu

F32 = jnp.float32
BF16 = jnp.bfloat16

T = 2048
D = 1024
CONV_CH = 512
CONV_W = 31
HEAD_DIM = 64
ATT_W = 1536
EVEN_IN = 5632
ODD_IN = 2560
SCONV_W = 3
SG_GROUPS = 4
CHUNK = 128
D_FF = 4096
EPS = 1e-6
DILATIONS = (1, 4, 16)
BAND = 128
SCALE = HEAD_DIM ** -0.5
NEG = -1e30

ADAM_LR = 0.001
ADAM_B1 = 0.9
ADAM_B2 = 0.999
ADAM_EPS = 1e-08
ADAM_WD = 0.01
ADAM_STEP = 10

V7X_VMEM_BYTES = 64 * 2 ** 20
VMEM_LIMIT = V7X_VMEM_BYTES - 8 * 2 ** 20
LANES = 128
TOKEN_SHAPE = (8, LANES)


SIBLING_COLLECTIVE_ID = 0


def _sibling_handshake():
    x, y, c = lax.axis_index("x"), lax.axis_index("y"), lax.axis_index("c")
    barrier = pltpu.get_barrier_semaphore()
    pl.semaphore_signal(barrier, inc=1, device_id=(x, y, 1 - c), device_id_type=pl.DeviceIdType.MESH)
    pl.semaphore_wait(barrier, 1)


class _SideJob:
    def __init__(self, inputs, out_shape, scratch_shapes, aliases, begin, advance, finish):
        self.inputs, self.out_shape, self.scratch_shapes = list(inputs), list(out_shape), list(scratch_shapes)
        self.aliases, self.begin, self.advance, self.finish = dict(aliases), begin, advance, finish
        self.results = None


_PENDING_JOBS = []


def _ride_next_call(job):
    _PENDING_JOBS.append(job)


def _pcall(body, **kw):
    if not _PENDING_JOBS or "grid" not in kw:
        return pl.pallas_call(body, **kw)
    job = _PENDING_JOBS.pop()
    as_list = lambda v: list(v) if isinstance(v, (list, tuple)) else [v]
    single_out = not isinstance(kw["out_shape"], (list, tuple))
    in_specs, out_specs, out_shape = as_list(kw["in_specs"]), as_list(kw["out_specs"]), as_list(kw["out_shape"])
    scratch = list(kw.get("scratch_shapes", ()))
    grid = kw["grid"]
    n_steps = 1
    for extent in grid:
        n_steps *= extent
    assert n_steps >= 3
    n_in, n_out, n_scr = len(in_specs), len(out_specs), len(scratch)
    j_in, j_out = len(job.inputs), len(job.out_shape)
    any_spec = pl.BlockSpec(memory_space=pl.ANY)

    def hosted(*refs):
        ins, j_ins = refs[:n_in], refs[n_in:n_in + j_in]
        outs = refs[n_in + j_in:n_in + j_in + n_out]
        j_outs = refs[n_in + j_in + n_out:n_in + j_in + n_out + j_out]
        scr = refs[n_in + j_in + n_out + j_out:n_in + j_in + n_out + j_out + n_scr]
        j_scr = refs[n_in + j_in + n_out + j_out + n_scr:]
        step = pl.program_id(0)
        for axis in range(1, len(grid)):
            step = step * grid[axis] + pl.program_id(axis)

        @pl.when(step == 0)
        def _():
            _sibling_handshake()
            job.begin(j_ins, j_outs, j_scr)

        @pl.when(step == 1)
        def _():
            job.advance(j_ins, j_outs, j_scr)

        body(*ins, *outs, *scr)

        @pl.when(step == n_steps - 1)
        def _():
            job.finish(j_ins, j_outs, j_scr)

    aliases = dict(kw.get("input_output_aliases", {}))
    aliases.update({n_in + a: n_out + b for a, b in job.aliases.items()})
    call = pl.pallas_call(
        hosted, name=kw["name"], grid=grid,
        in_specs=in_specs + [any_spec] * j_in, out_specs=out_specs + [any_spec] * j_out,
        out_shape=out_shape + job.out_shape, scratch_shapes=scratch + job.scratch_shapes,
        input_output_aliases=aliases,
        compiler_params=pltpu.CompilerParams(dimension_semantics=("arbitrary",) * len(grid), vmem_limit_bytes=VMEM_LIMIT,
                                             collective_id=SIBLING_COLLECTIVE_ID))

    def run(*args):
        res = call(*args, *job.inputs)
        job.results = list(res[n_out:])
        return res[0] if single_out else list(res[:n_out])

    return run


def _params(*sem):
    return pltpu.CompilerParams(dimension_semantics=sem, vmem_limit_bytes=VMEM_LIMIT)


def _dot(a, b, dims):
    return lax.dot_general(a, b, (dims, ((), ())), preferred_element_type=F32)


def _nn(a, b):
    return _dot(a, b, ((1,), (0,)))


def _nt(a, b):
    return _dot(a, b, ((1,), (1,)))


def _tn(a, b):
    return _dot(a, b, ((0,), (0,)))


def _sigmoid(x):
    return 1.0 / (1.0 + jnp.exp(-x))


MM_VMEM_BUDGET = 40 * 2 ** 20


def _mm_tiles(mode, m, n, k, a_bytes, b_bytes, extra_bytes, out_bytes):
    def divisors(total, unit):
        return [t for t in range(unit, total + 1, unit) if total % t == 0]

    best = None
    for tm in divisors(m, LANES if mode == "tn" else 8):
        for tn in divisors(n, LANES):
            blocks = tm * k * a_bytes + tn * k * b_bytes + tm * tn * (extra_bytes + out_bytes)
            casts = (tm * k * 2 if a_bytes == 4 else 0) + (tn * k * 2 if b_bytes == 4 else 0)
            if 2 * blocks + casts + tm * tn * 4 > MM_VMEM_BUDGET:
                continue
            key = ((m // tm) * (n // tn), (m // tm) * n * k * b_bytes, abs(tm - tn))
            if best is None or key < best[0]:
                best = (key, tm, tn)
    return best[1], best[2]


def _mm(name, mode, a, b, m, n, k, out_dtypes, *, b_off=0, extras=(), epi=None, tie=None):
    tm, tn = _mm_tiles(mode, m, n, k, a.dtype.itemsize, b.dtype.itemsize, sum(e.dtype.itemsize for e in extras),
                       sum(jnp.dtype(dt).itemsize for dt in out_dtypes))
    assert b_off % tn == 0
    b_off //= tn
    if mode == "nn":
        a_spec = pl.BlockSpec((tm, k), lambda i, j: (i, 0))
        b_spec = pl.BlockSpec((k, tn), lambda i, j: (0, j + b_off))
        dims = ((1,), (0,))
    elif mode == "nt":
        a_spec = pl.BlockSpec((tm, k), lambda i, j: (i, 0))
        b_spec = pl.BlockSpec((tn, k), lambda i, j: (j, 0))
        dims = ((1,), (1,))
    else:
        a_spec = pl.BlockSpec((k, tm), lambda i, j: (0, i))
        b_spec = pl.BlockSpec((k, tn), lambda i, j: (0, j))
        dims = ((0,), (0,))
    o_spec = pl.BlockSpec((tm, tn), lambda i, j: (i, j))
    n_extra = len(extras)
    ties = () if tie is None else (tie,)

    def body(a_ref, b_ref, *rest):
        rest = rest[len(ties):]
        acc = _dot(a_ref[...].astype(BF16), b_ref[...].astype(BF16), dims)
        vals = epi(acc, *[e[...] for e in rest[:n_extra]]) if epi is not None else (acc,)
        for o_ref, v in zip(rest[n_extra:], vals):
            o_ref[...] = v.astype(o_ref.dtype)

    outs = _pcall(
        body, name=name, grid=(m // tm, n // tn),
        in_specs=[a_spec, b_spec] + [pl.BlockSpec(TOKEN_SHAPE, lambda i, j: (0, 0))] * len(ties) + [o_spec] * n_extra,
        out_specs=[o_spec] * len(out_dtypes),
        out_shape=[jax.ShapeDtypeStruct((m, n), dt) for dt in out_dtypes],
        compiler_params=_params("parallel", "parallel"),
    )(a, b, *ties, *extras)
    return outs[0] if len(out_dtypes) == 1 else outs


def _row_tile(k, a_bytes, n_row_blocks):
    for tm in (1024, 512, 256, 128):
        if 2 * (tm * k * a_bytes + n_row_blocks * tm * D * 4) + D * k * 2 + tm * D * 4 <= MM_VMEM_BUDGET + 4 * 2 ** 20:
            return tm
    raise ValueError("no row tile fits")


def _resident(shape):
    return pl.BlockSpec(shape, lambda i: (0, 0), pipeline_mode=pl.Buffered(1))


FFN0_DOWN_TILE = 512


def _mm_out_norm(name, a, b, k, res, g_next, tm=None):
    tm = tm or _row_tile(k, a.dtype.itemsize, 3)

    def body(a_ref, b_ref, r_ref, g_ref, h_ref, hn_ref):
        h = _nn(a_ref[...].astype(BF16), b_ref[...]) + r_ref[...]
        h_ref[...] = h
        r = lax.rsqrt(jnp.mean(h * h, axis=-1, keepdims=True) + EPS)
        hn_ref[...] = ((h * r) * g_ref[...]).astype(BF16)

    row = pl.BlockSpec((tm, D), lambda i: (i, 0))
    return _pcall(
        body, name=name, grid=(T // tm,),
        in_specs=[pl.BlockSpec((tm, k), lambda i: (i, 0)), _resident((k, D)), row,
                  pl.BlockSpec((1, D), lambda i: (0, 0))],
        out_specs=[row, row],
        out_shape=[jax.ShapeDtypeStruct((T, D), F32), jax.ShapeDtypeStruct((T, D), BF16)],
        compiler_params=_params("parallel"),
    )(a, b, res, g_next)


def _ffn_last(name, hn, w1, w2, res, g, target):
    tm = FFN_BWD_TILE

    def body(a_ref, w1_ref, w2_ref, r_ref, g_ref, t_ref, f_ref, dh_ref, dg_ref, loss_ref):
        u = jnp.maximum(_nn(a_ref[...], w1_ref[...]), 0.0)
        f = (u * u).astype(BF16)
        f_ref[...] = f
        x = _nn(f, w2_ref[...]) + r_ref[...]
        r = lax.rsqrt(jnp.mean(x * x, axis=-1, keepdims=True) + EPS)
        nrm = x * r
        gain = g_ref[...]
        err = nrm * gain - t_ref[...]
        dy = err * (1.0 / D)
        dn = dy * gain
        dh_ref[...] = r * (dn - nrm * jnp.mean(dn * nrm, axis=-1, keepdims=True))

        @pl.when(pl.program_id(0) == 0)
        def _():
            dg_ref[...] = jnp.zeros_like(dg_ref)
            loss_ref[...] = jnp.zeros_like(loss_ref)

        dg_ref[...] += jnp.sum(dy * nrm, axis=0, keepdims=True)
        part = jnp.sum(jnp.sum(err * err, axis=1, keepdims=True), axis=0, keepdims=True) * (0.5 / D)
        loss_ref[...] += jnp.broadcast_to(part, (1, LANES))

    row = pl.BlockSpec((tm, D), lambda i: (i, 0))
    wide = pl.BlockSpec((tm, D_FF), lambda i: (i, 0))
    vec = pl.BlockSpec((1, D), lambda i: (0, 0))
    return _pcall(
        body, name=name, grid=(T // tm,),
        in_specs=[row, _resident((D, D_FF)), _resident((D_FF, D)), row, vec, row],
        out_specs=[wide, row, vec, pl.BlockSpec((1, LANES), lambda i: (0, 0))],
        out_shape=[jax.ShapeDtypeStruct((T, D_FF), BF16), jax.ShapeDtypeStruct((T, D), F32),
                   jax.ShapeDtypeStruct((1, D), F32), jax.ShapeDtypeStruct((1, LANES), F32)],
        compiler_params=_params("arbitrary"),
    )(hn, w1, w2, res, g, target)


def _mm_dx_norm(name, dz, w, k, h, g, dres, tie=None):
    tm = _row_tile(k, dz.dtype.itemsize, 3)
    ties = () if tie is None else (tie,)

    def body(a_ref, b_ref, *rest):
        h_ref, g_ref, r_ref, dh_ref, dg_ref = rest[len(ties):]
        dy = _nt(a_ref[...].astype(BF16), b_ref[...])
        x = h_ref[...]
        r = lax.rsqrt(jnp.mean(x * x, axis=-1, keepdims=True) + EPS)
        nrm = x * r
        dn = dy * g_ref[...]
        dh_ref[...] = r_ref[...] + r * (dn - nrm * jnp.mean(dn * nrm, axis=-1, keepdims=True))

        @pl.when(pl.program_id(0) == 0)
        def _():
            dg_ref[...] = jnp.zeros_like(dg_ref)

        dg_ref[...] += jnp.sum(dy * nrm, axis=0, keepdims=True)

    row = pl.BlockSpec((tm, D), lambda i: (i, 0))
    vec = pl.BlockSpec((1, D), lambda i: (0, 0))
    return _pcall(
        body, name=name, grid=(T // tm,),
        in_specs=[pl.BlockSpec((tm, k), lambda i: (i, 0)), _resident((D, k))]
        + [pl.BlockSpec(TOKEN_SHAPE, lambda i: (0, 0))] * len(ties) + [row, vec, row],
        out_specs=[row, vec],
        out_shape=[jax.ShapeDtypeStruct((T, D), F32), jax.ShapeDtypeStruct((1, D), F32)],
        compiler_params=_params("arbitrary"),
    )(dz, w, *ties, h, g, dres)


def _rms_fwd(name, h, g, tm=512, tie=None):
    ties = () if tie is None else (tie,)

    def body(h_ref, g_ref, *rest):
        x = h_ref[...]
        r = lax.rsqrt(jnp.mean(x * x, axis=-1, keepdims=True) + EPS)
        rest[-1][...] = ((x * r) * g_ref[...]).astype(BF16)

    return _pcall(
        body, name=name, grid=(T // tm,),
        in_specs=[pl.BlockSpec((tm, D), lambda i: (i, 0)), pl.BlockSpec((1, D), lambda i: (0, 0))]
        + [pl.BlockSpec(TOKEN_SHAPE, lambda i: (0, 0))] * len(ties),
        out_specs=pl.BlockSpec((tm, D), lambda i: (i, 0)),
        out_shape=jax.ShapeDtypeStruct((T, D), BF16),
        compiler_params=_params("parallel"),
    )(h, g, *ties)


CONV_TILE = 256
CONV_HALO = 32


def _glu(z):
    return z[:, :CONV_CH] * _sigmoid(z[:, CONV_CH:])


SUBLANES = 8


def _sublane_shifts(win):
    n = win.shape[0]
    return [win] + [win[r:r + n - SUBLANES, :] for r in range(1, SUBLANES)]


def _rows_from(shifts, off, n):
    q, r = divmod(off, SUBLANES)
    return shifts[r][q * SUBLANES:q * SUBLANES + n, :]


def _econv_fwd(zc, conv_k, conv_b, ln_g, ln_b):
    R, H = CONV_TILE, CONV_HALO

    def body(z_ref, zh_ref, k_ref, b_ref, g_ref, be_ref, cv_ref, cat_ref):
        i = pl.program_id(0)
        glu = _glu(z_ref[...])
        halo = _glu(zh_ref[...]) * (i > 0).astype(F32)
        win = _sublane_shifts(jnp.concatenate([halo, glu], axis=0))
        acc = jnp.zeros((R, CONV_CH), F32) + b_ref[...]
        for j in range(CONV_W):
            acc = acc + k_ref[j:j + 1, :] * _rows_from(win, H - (CONV_W - 1) + j, R)
        cv_ref[...] = acc
        mu = jnp.mean(acc, axis=-1, keepdims=True)
        xc = acc - mu
        rstd = lax.rsqrt(jnp.mean(xc * xc, axis=-1, keepdims=True) + EPS)
        ln = xc * rstd * g_ref[...] + be_ref[...]
        cat_ref[...] = (ln * _sigmoid(ln)).astype(BF16)

    vec = pl.BlockSpec((1, CONV_CH), lambda i: (0, 0))
    return _pcall(
        body, name="econv_fwd", grid=(T // R,),
        in_specs=[pl.BlockSpec((R, 2 * CONV_CH), lambda i: (i, 0)),
                  pl.BlockSpec((H, 2 * CONV_CH), lambda i: (jnp.maximum(i * (R // H) - 1, 0), 0)),
                  pl.BlockSpec((CONV_W, CONV_CH), lambda i: (0, 0)), vec, vec, vec],
        out_specs=[pl.BlockSpec((R, CONV_CH), lambda i: (i, 0)), pl.BlockSpec((R, CONV_CH), lambda i: (i, 0))],
        out_shape=[jax.ShapeDtypeStruct((T, CONV_CH), F32), jax.ShapeDtypeStruct((T, D), BF16)],
        compiler_params=_params("parallel"),
    )(zc, zc, conv_k, conv_b, ln_g, ln_b)


def _econv_bwd_ln(cv, dcat, ln_g, ln_b):
    R = CONV_TILE

    def body(cv_ref, d_ref, g_ref, be_ref, dcv_ref, dg_ref, dbe_ref, dcb_ref):
        cv_t = cv_ref[...]
        mu = jnp.mean(cv_t, axis=-1, keepdims=True)
        xc = cv_t - mu
        rstd = lax.rsqrt(jnp.mean(xc * xc, axis=-1, keepdims=True) + EPS)
        xh = xc * rstd
        ln = xh * g_ref[...] + be_ref[...]
        sg = _sigmoid(ln)
        dln = d_ref[...] * (sg * (1.0 + ln * (1.0 - sg)))
        dxh = dln * g_ref[...]
        dcv = rstd * (dxh - jnp.mean(dxh, axis=-1, keepdims=True) - xh * jnp.mean(dxh * xh, axis=-1, keepdims=True))
        dcv_ref[...] = dcv

        @pl.when(pl.program_id(0) == 0)
        def _():
            dg_ref[...] = jnp.zeros_like(dg_ref)
            dbe_ref[...] = jnp.zeros_like(dbe_ref)
            dcb_ref[...] = jnp.zeros_like(dcb_ref)

        dg_ref[...] += jnp.sum(dln * xh, axis=0, keepdims=True)
        dbe_ref[...] += jnp.sum(dln, axis=0, keepdims=True)
        dcb_ref[...] += jnp.sum(dcv, axis=0, keepdims=True)

    vec = pl.BlockSpec((1, CONV_CH), lambda i: (0, 0))
    row = pl.BlockSpec((R, CONV_CH), lambda i: (i, 0))
    vshape = jax.ShapeDtypeStruct((1, CONV_CH), F32)
    return _pcall(
        body, name="econv_bwd_ln", grid=(T // R,),
        in_specs=[row, row, vec, vec], out_specs=[row, vec, vec, vec],
        out_shape=[jax.ShapeDtypeStruct((T, CONV_CH), F32), vshape, vshape, vshape],
        compiler_params=_params("arbitrary"),
    )(cv, dcat, ln_g, ln_b)


def _econv_bwd_conv(dcv, zc, conv_k):
    R, H = CONV_TILE, CONV_HALO
    last = T // R - 1

    def body(d_ref, dn_ref, z_ref, zh_ref, k_ref, dz_ref, dk_ref):
        i = pl.program_id(0)
        z = z_ref[...]
        a_lin = z[:, :CONV_CH]
        sg = _sigmoid(z[:, CONV_CH:])
        glu = a_lin * sg
        halo = _glu(zh_ref[...]) * (i > 0).astype(F32)
        win = _sublane_shifts(jnp.concatenate([halo, glu], axis=0))
        dcv_t = d_ref[...]
        nxt = dn_ref[...] * (i < last).astype(F32)
        winb = _sublane_shifts(jnp.concatenate([dcv_t, nxt], axis=0))

        @pl.when(i == 0)
        def _():
            dk_ref[...] = jnp.zeros_like(dk_ref)

        dglu = jnp.zeros((R, CONV_CH), F32)
        for j in range(CONV_W):
            dk_ref[j:j + 1, :] += jnp.sum(dcv_t * _rows_from(win, H - (CONV_W - 1) + j, R), axis=0, keepdims=True)
            dglu = dglu + k_ref[j:j + 1, :] * _rows_from(winb, CONV_W - 1 - j, R)
        dz_ref[...] = jnp.concatenate([dglu * sg, dglu * a_lin * sg * (1.0 - sg)], axis=1).astype(BF16)

    return _pcall(
        body, name="econv_bwd_conv", grid=(T // R,),
        in_specs=[pl.BlockSpec((R, CONV_CH), lambda i: (i, 0)),
                  pl.BlockSpec((H, CONV_CH), lambda i: (jnp.minimum((i + 1) * (R // H), T // H - 1), 0)),
                  pl.BlockSpec((R, 2 * CONV_CH), lambda i: (i, 0)),
                  pl.BlockSpec((H, 2 * CONV_CH), lambda i: (jnp.maximum(i * (R // H) - 1, 0), 0)),
                  pl.BlockSpec((CONV_W, CONV_CH), lambda i: (0, 0))],
        out_specs=[pl.BlockSpec((R, 2 * CONV_CH), lambda i: (i, 0)), pl.BlockSpec((CONV_W, CONV_CH), lambda i: (0, 0))],
        out_shape=[jax.ShapeDtypeStruct((T, EVEN_IN), BF16), jax.ShapeDtypeStruct((CONV_W, CONV_CH), F32)],
        compiler_params=_params("arbitrary"),
    )(dcv, dcv, zc, zc, conv_k)


def _swap_halves(v):
    lane = lax.broadcasted_iota(jnp.int32, (1, v.shape[1]), 1)
    return jnp.where((lane % HEAD_DIM) < HEAD_DIM // 2, pltpu.roll(v, LANES - HEAD_DIM // 2, 1),
                     pltpu.roll(v, HEAD_DIM // 2, 1))


def _qkv_proj(hn, w_in, rope_c, rope_s, tm=T):
    tn = 4 * LANES

    def body(a_ref, b_ref, c_ref, s_ref, o_ref):
        j = pl.program_id(1)
        acc = _nn(a_ref[...], b_ref[...])
        for p in range(4):
            v = acc[:, p * LANES:(p + 1) * LANES]
            rot = v * c_ref[...] + _swap_halves(v) * s_ref[...]
            o_ref[p] = jnp.where(j < 6, rot, v)

    tab = pl.BlockSpec((tm, LANES), lambda i, j: (i, 0))
    return _pcall(
        body, name="qkv_proj", grid=(T // tm, 9),
        in_specs=[pl.BlockSpec((tm, D), lambda i, j: (i, 0)),
                  pl.BlockSpec((D, tn), lambda i, j: (0, j + (2 * CONV_CH) // tn)), tab, tab],
        out_specs=pl.BlockSpec((None, 4, tm, LANES), lambda i, j: (j, 0, i, 0)),
        out_shape=jax.ShapeDtypeStruct((9, 4, T, LANES), F32),
        compiler_params=_params("parallel", "parallel"),
    )(hn, w_in, rope_c, rope_s)


ATTN_FWD_UNROLL = 4
ATTN_BWD_UNROLL = 4


def _band_rows(start, d):
    if d == 1:
        return pl.ds(pl.multiple_of(start, BAND), BAND)
    return pl.ds(start, BAND, stride=d)


def _band_masks(n):
    row = lax.broadcasted_iota(jnp.int32, (BAND, BAND), 0)
    col = lax.broadcasted_iota(jnp.int32, (BAND, BAND), 1)
    no_prev = (n == 0).astype(jnp.int32) * (2 * BAND)
    return col <= row, col >= row + no_prev


def _attn_fwd(qkv, g):
    d = DILATIONS[g]
    nb = T // d // BAND
    has_prev = nb > 1

    def body(q_ref, k_ref, v_ref, o_ref, l_ref):
        lane_lo = lax.broadcasted_iota(jnp.int32, (BAND, LANES), 1) < HEAD_DIM

        heads = (lane_lo, jnp.logical_not(lane_lo))
        ones = jnp.ones((BAND, LANES), BF16)

        def step(it, carry):
            tiles = []
            for u in range(ATTN_FWD_UNROLL):
                idx = it * ATTN_FWD_UNROLL + u
                r = idx // nb
                n = idx % nb
                cur = _band_rows(n * (BAND * d) + r, d)
                prev = _band_rows(jnp.maximum(n - 1, 0) * (BAND * d) + r, d)
                mc, mp = _band_masks(n)
                kp = k_ref[prev, :].astype(BF16) if has_prev else None
                vp = v_ref[prev, :].astype(BF16) if has_prev else None
                tiles.append((cur, mc, mp, q_ref[cur, :], k_ref[cur, :].astype(BF16), v_ref[cur, :].astype(BF16), kp, vp))
            scores = []
            for cur, mc, mp, q, kc, vc, kp, vp in tiles:
                for hm in heads:
                    qm = jnp.where(hm, q, 0.0).astype(BF16)
                    sc = jnp.where(mc, _nt(qm, kc) * SCALE, NEG)
                    scores.append((sc, jnp.where(mp, _nt(qm, kp) * SCALE, NEG)) if has_prev else (sc,))
            maxes = [functools.reduce(jnp.maximum, [jnp.max(sx, axis=1, keepdims=True) for sx in ss]) for ss in scores]
            probs = [[jnp.exp(sx - mx).astype(BF16) for sx in ss] for ss, mx in zip(scores, maxes)]
            dens = [functools.reduce(jnp.add, [_nn(px, ones) for px in ps]) for ps in probs]
            for t, (cur, mc, mp, q, kc, vc, kp, vp) in enumerate(tiles):
                outs, lses = [], []
                for h in range(2):
                    ps = probs[2 * t + h]
                    acc = _nn(ps[0], vc) + _nn(ps[1], vp) if has_prev else _nn(ps[0], vc)
                    outs.append(acc / dens[2 * t + h])
                    lses.append(maxes[2 * t + h] + jnp.log(dens[2 * t + h]))
                o_ref[cur, :] = jnp.where(lane_lo, outs[0], outs[1])
                l_ref[cur, :] = jnp.where(lane_lo, lses[0], lses[1])
            return carry

        lax.fori_loop(0, d * nb // ATTN_FWD_UNROLL, step, 0)

    def slab(which):
        return pl.BlockSpec((None, None, T, LANES), lambda p: (which * 3 + g, p, 0, 0))

    out = pl.BlockSpec((None, T, LANES), lambda p: (p, 0, 0))
    shape = jax.ShapeDtypeStruct((4, T, LANES), F32)
    return _pcall(
        body, name=f"attn_fwd{g}", grid=(4,),
        in_specs=[slab(0), slab(1), slab(2)], out_specs=[out, out], out_shape=[shape, shape],
        compiler_params=_params("parallel"),
    )(qkv, qkv, qkv)


def _attn_merge(outs, lses, cat, tm=1024):
    def body(o0, o1, o2, l0, l1, l2, cat_in, cat_ref, att_ref, w0, w1, w2):
        del cat_in
        la, lb, lc = l0[...], l1[...], l2[...]
        mx = jnp.maximum(jnp.maximum(la, lb), lc)
        ea, eb, ec = jnp.exp(la - mx), jnp.exp(lb - mx), jnp.exp(lc - mx)
        inv = 1.0 / (ea + eb + ec)
        wa, wb, wc = ea * inv, eb * inv, ec * inv
        att = wa * o0[...] + wb * o1[...] + wc * o2[...]
        att_ref[...] = att
        cat_ref[...] = att.astype(BF16)
        w0[...] = wa
        w1[...] = wb
        w2[...] = wc

    slab = pl.BlockSpec((None, tm, LANES), lambda p, i: (p, i, 0))
    shape = jax.ShapeDtypeStruct((4, T, LANES), F32)
    return _pcall(
        body, name="attn_merge", grid=(4, T // tm),
        in_specs=[slab] * 6 + [pl.BlockSpec(memory_space=pl.ANY)],
        out_specs=[pl.BlockSpec((tm, LANES), lambda p, i: (i, CONV_CH // LANES + p)), slab, slab, slab, slab],
        out_shape=[jax.ShapeDtypeStruct((T, D), BF16), shape, shape, shape, shape],
        input_output_aliases={6: 0},
        compiler_params=_params("parallel", "parallel"),
    )(*outs, *lses, cat)


def _attn_bwd(qkv, lse, wgt, att, dcat, dqkv, g):
    d = DILATIONS[g]
    nb = T // d // BAND
    has_prev = nb > 1

    def body(q_ref, k_ref, v_ref, l_ref, w_ref, a_ref, da_ref, dq_in, o_ref):
        del dq_in
        lane = lax.broadcasted_iota(jnp.int32, (BAND, LANES), 1)
        lane_lo = lane < HEAD_DIM
        row = lax.broadcasted_iota(jnp.int32, (LANES, LANES), 0)
        same_head = ((row // HEAD_DIM) == (lane // HEAD_DIM)).astype(BF16)
        dq_ref, dk_ref, dv_ref = o_ref.at[0], o_ref.at[1], o_ref.at[2]
        if has_prev:
            dk_ref[...] = jnp.zeros((T, LANES), F32)
            dv_ref[...] = jnp.zeros((T, LANES), F32)

        heads = (lane_lo, jnp.logical_not(lane_lo))

        def step(it, carry):
            tiles = []
            for u in range(ATTN_BWD_UNROLL):
                idx = it * ATTN_BWD_UNROLL + u
                r = idx // nb
                n = idx % nb
                cur = _band_rows(n * (BAND * d) + r, d)
                prev = _band_rows(jnp.maximum(n - 1, 0) * (BAND * d) + r, d)
                mc, mp = _band_masks(n)
                da = da_ref[cur, :]
                prod = da * a_ref[cur, :]
                hi = prod.astype(BF16)
                lo = (prod - hi.astype(F32)).astype(BF16)
                tiles.append(dict(cur=cur, prev=prev, mc=mc, mp=mp, da=da, hi=hi, lo=lo, q=q_ref[cur, :],
                                  kc=k_ref[cur, :].astype(BF16), vc=v_ref[cur, :].astype(BF16),
                                  kp=k_ref[prev, :].astype(BF16) if has_prev else None,
                                  vp=v_ref[prev, :].astype(BF16) if has_prev else None,
                                  lse=l_ref[cur, :], w=w_ref[cur, :]))
            for t in tiles:
                t["csum"] = _nn(t["hi"], same_head) + _nn(t["lo"], same_head)
            chains = []
            for t in tiles:
                for h, hm in enumerate(heads):
                    qm = jnp.where(hm, t["q"], 0.0).astype(BF16)
                    dam = jnp.where(hm, t["da"], 0.0).astype(BF16)
                    ch = dict(t=t, h=h, qm=qm, dam=dam, sc=jnp.where(t["mc"], _nt(qm, t["kc"]) * SCALE, NEG),
                              dpc=_nt(dam, t["vc"]))
                    if has_prev:
                        ch.update(sp=jnp.where(t["mp"], _nt(qm, t["kp"]) * SCALE, NEG), dpp=_nt(dam, t["vp"]))
                    chains.append(ch)
            for ch in chains:
                t, col0 = ch["t"], ch["h"] * HEAD_DIM
                lse_h = t["lse"][:, col0:col0 + 1]
                w_h = t["w"][:, col0:col0 + 1]
                c_h = t["csum"][:, col0:col0 + 1]
                pwc = w_h * jnp.exp(ch["sc"] - lse_h)
                ch["dsc"] = (pwc * (ch["dpc"] - c_h) * SCALE).astype(BF16)
                ch["pwc"] = pwc.astype(BF16)
                if has_prev:
                    pwp = w_h * jnp.exp(ch["sp"] - lse_h)
                    ch["dsp"] = (pwp * (ch["dpp"] - c_h) * SCALE).astype(BF16)
                    ch["pwp"] = pwp.astype(BF16)
            for ch in chains:
                t = ch["t"]
                ch["dq"] = _nn(ch["dsc"], t["kc"])
                ch["dkc"] = _tn(ch["dsc"], ch["qm"])
                ch["dvc"] = _tn(ch["pwc"], ch["dam"])
                if has_prev:
                    ch["dq"] = ch["dq"] + _nn(ch["dsp"], t["kp"])
                    ch["dkp"] = _tn(ch["dsp"], ch["qm"])
                    ch["dvp"] = _tn(ch["pwp"], ch["dam"])
            for i, t in enumerate(tiles):
                c0, c1 = chains[2 * i], chains[2 * i + 1]
                dq_ref[t["cur"], :] = jnp.where(lane_lo, c0["dq"], c1["dq"])
                if has_prev:
                    dk_ref[t["cur"], :] += c0["dkc"] + c1["dkc"]
                    dk_ref[t["prev"], :] += c0["dkp"] + c1["dkp"]
                    dv_ref[t["cur"], :] += c0["dvc"] + c1["dvc"]
                    dv_ref[t["prev"], :] += c0["dvp"] + c1["dvp"]
                else:
                    dk_ref[t["cur"], :] = c0["dkc"] + c1["dkc"]
                    dv_ref[t["cur"], :] = c0["dvc"] + c1["dvc"]
            return carry

        lax.fori_loop(0, d * nb // ATTN_BWD_UNROLL, step, 0)

    def slab(which):
        return pl.BlockSpec((None, None, T, LANES), lambda p: (which * 3 + g, p, 0, 0))

    per_pair = pl.BlockSpec((None, T, LANES), lambda p: (p, 0, 0))
    return _pcall(
        body, name=f"attn_bwd{g}", grid=(4,),
        in_specs=[slab(0), slab(1), slab(2), per_pair, per_pair, per_pair,
                  pl.BlockSpec((T, LANES), lambda p: (0, CONV_CH // LANES + p)),
                  pl.BlockSpec(memory_space=pl.ANY)],
        out_specs=pl.BlockSpec((None, 3, None, T, LANES), lambda p: (g, 0, p, 0, 0)),
        out_shape=jax.ShapeDtypeStruct((3, 3, 4, T, LANES), F32),
        input_output_aliases={7: 0},
        compiler_params=_params("parallel"),
    )(qkv, qkv, qkv, lse, wgt, att, dcat, dqkv)


def _rope_bwd(dqkv, rope_c, rope_s, dz):
    wide = 4 * LANES

    def body(d_ref, c_ref, s_ref, dz_in, o_ref):
        del dz_in
        w = pl.program_id(1)
        for p in range(4):
            v = d_ref[p]
            rot = v * c_ref[...] + _swap_halves(v * s_ref[...])
            o_ref[:, p * LANES:(p + 1) * LANES] = jnp.where(w < 2, rot, v).astype(BF16)

    tab = pl.BlockSpec((T, LANES), lambda g, w: (0, 0))
    return _pcall(
        body, name="rope_bwd", grid=(3, 3),
        in_specs=[pl.BlockSpec((None, None, 4, T, LANES), lambda g, w: (g, w, 0, 0, 0)), tab, tab,
                  pl.BlockSpec(memory_space=pl.ANY)],
        out_specs=pl.BlockSpec((T, wide), lambda g, w: (0, (2 * CONV_CH) // wide + w * 3 + g)),
        out_shape=jax.ShapeDtypeStruct((T, EVEN_IN), BF16),
        input_output_aliases={3: 0},
        compiler_params=_params("parallel", "parallel"),
    )(dqkv, rope_c, rope_s, dz)


ODD_TILE = 256
ODD_HALO = 8
GELU_C = 0.7978845608028654
GELU_A = 0.044715


def _gelu(x):
    return 0.5 * x * (1.0 + jnp.tanh(GELU_C * (x + GELU_A * x * x * x)))


def _gelu_grad(x):
    th = jnp.tanh(GELU_C * (x + GELU_A * x * x * x))
    return 0.5 * (1.0 + th) + 0.5 * x * (1.0 - th * th) * GELU_C * (1.0 + 3.0 * GELU_A * x * x)


def _tril():
    row = lax.broadcasted_iota(jnp.int32, (CHUNK, CHUNK), 0)
    col = lax.broadcasted_iota(jnp.int32, (CHUNK, CHUNK), 1)
    return (col <= row).astype(F32)


def _odd_parts(z, zh, i, k_ref, g_ref, be_ref, w_ref, bt_ref):
    R, H = ODD_TILE, ODD_HALO
    gb, gc, xs, uv = z[:, :512], z[:, 512:1024], z[:, 1024:1536], z[:, 1536:]
    halo = zh[:, 512:1024] * zh[:, 1024:1536] * (i > 0).astype(F32)
    win = jnp.concatenate([halo, gc * xs], axis=0)
    cv = jnp.zeros((R, 512), F32)
    for j in range(SCONV_W):
        off = H - (SCONV_W - 1) + j
        cv = cv + k_ref[j:j + 1, :] * win[off:off + R, :]
    ge = _gelu(uv)
    u, v = ge[:, :512], ge[:, 512:]
    mu = jnp.mean(v, axis=-1, keepdims=True)
    xc = v - mu
    rstd = lax.rsqrt(jnp.mean(xc * xc, axis=-1, keepdims=True) + EPS)
    xh = xc * rstd
    vn = xh * g_ref[...] + be_ref[...]
    tril = _tril()
    wms = [(w_ref[g] * tril).astype(BF16) for g in range(SG_GROUPS)]
    rows = []
    for ci in range(R // CHUNK):
        blocks = []
        for g in range(SG_GROUPS):
            blk = vn[ci * CHUNK:(ci + 1) * CHUNK, g * LANES:(g + 1) * LANES].astype(BF16)
            blocks.append(_nn(wms[g], blk) + bt_ref[:, g:g + 1])
        rows.append(jnp.concatenate(blocks, axis=1))
    vmix = jnp.concatenate(rows, axis=0)
    return gb, gc, xs, uv, win, cv, u, rstd, xh, vn, vmix, wms


def _odd_mid_fwd(z, conv_k, ln_g, ln_b, sg_w, sg_bt):
    R, H = ODD_TILE, ODD_HALO

    def body(z_ref, zh_ref, k_ref, g_ref, be_ref, w_ref, bt_ref, o_ref):
        i = pl.program_id(0)
        gb, _, _, _, _, cv, u, _, _, _, vmix, _ = _odd_parts(z_ref[...], zh_ref[...], i, k_ref, g_ref, be_ref, w_ref, bt_ref)
        o_ref[...] = jnp.concatenate([gb * cv, u * vmix], axis=1).astype(BF16)

    vec = pl.BlockSpec((1, 512), lambda i: (0, 0))
    return _pcall(
        body, name="odd_mid_fwd", grid=(T // R,),
        in_specs=[pl.BlockSpec((R, ODD_IN), lambda i: (i, 0)),
                  pl.BlockSpec((H, ODD_IN), lambda i: (jnp.maximum(i * (R // H) - 1, 0), 0)),
                  pl.BlockSpec((SCONV_W, 512), lambda i: (0, 0)), vec, vec,
                  pl.BlockSpec((SG_GROUPS, CHUNK, CHUNK), lambda i: (0, 0, 0)),
                  pl.BlockSpec((CHUNK, SG_GROUPS), lambda i: (0, 0))],
        out_specs=pl.BlockSpec((R, D), lambda i: (i, 0)),
        out_shape=jax.ShapeDtypeStruct((T, D), BF16),
        compiler_params=_params("parallel"),
    )(z, z, conv_k, ln_g, ln_b, sg_w, sg_bt)


def _odd_mid_bwd(z, dcat, conv_k, ln_g, ln_b, sg_w, sg_bt):
    R, H = ODD_TILE, ODD_HALO
    last = T // R - 1

    def body(z_ref, zh_ref, zn_ref, d_ref, dn_ref, k_ref, g_ref, be_ref, w_ref, bt_ref,
             dz_ref, dk_ref, dg_ref, dbe_ref, dw_ref, dbt_ref):
        i = pl.program_id(0)
        z = z_ref[...]
        gb, gc, xs, uv, win, cv, u, rstd, xh, vn, vmix, wms = _odd_parts(z, zh_ref[...], i, k_ref, g_ref, be_ref, w_ref, bt_ref)
        dcat_t = d_ref[...]
        dc, dd = dcat_t[:, :512], dcat_t[:, 512:]

        @pl.when(i == 0)
        def _():
            dk_ref[...] = jnp.zeros_like(dk_ref)
            dg_ref[...] = jnp.zeros_like(dg_ref)
            dbe_ref[...] = jnp.zeros_like(dbe_ref)
            dw_ref[...] = jnp.zeros_like(dw_ref)
            dbt_ref[...] = jnp.zeros_like(dbt_ref)

        dgb = dc * cv
        dcv = dc * gb
        nxt = dn_ref[:, :512] * zn_ref[:, :512] * (i < last).astype(F32)
        winb = jnp.concatenate([dcv, nxt], axis=0)
        dp = jnp.zeros((R, 512), F32)
        for j in range(SCONV_W):
            off = H - (SCONV_W - 1) + j
            dk_ref[j:j + 1, :] += jnp.sum(dcv * win[off:off + R, :], axis=0, keepdims=True)
            ob = SCONV_W - 1 - j
            dp = dp + k_ref[j:j + 1, :] * winb[ob:ob + R, :]
        dgc = dp * xs
        dxs = dp * gc
        du = dd * vmix
        dvmix = dd * u
        tril = _tril()
        rows = []
        for ci in range(R // CHUNK):
            blocks = []
            for g in range(SG_GROUPS):
                sl = (slice(ci * CHUNK, (ci + 1) * CHUNK), slice(g * LANES, (g + 1) * LANES))
                dblk = dvmix[sl]
                dblk16 = dblk.astype(BF16)
                blocks.append(_tn(wms[g], dblk16))
                dw_ref[g] += _nt(dblk16, vn[sl].astype(BF16)) * tril
                dbt_ref[:, g:g + 1] += jnp.sum(dblk, axis=1, keepdims=True)
            rows.append(jnp.concatenate(blocks, axis=1))
        dvn = jnp.concatenate(rows, axis=0)
        dg_ref[...] += jnp.sum(dvn * xh, axis=0, keepdims=True)
        dbe_ref[...] += jnp.sum(dvn, axis=0, keepdims=True)
        dxh = dvn * g_ref[...]
        dv = rstd * (dxh - jnp.mean(dxh, axis=-1, keepdims=True) - xh * jnp.mean(dxh * xh, axis=-1, keepdims=True))
        duv = jnp.concatenate([du, dv], axis=1) * _gelu_grad(uv)
        dz_ref[...] = jnp.concatenate([dgb, dgc, dxs, duv], axis=1).astype(BF16)

    vec = pl.BlockSpec((1, 512), lambda i: (0, 0))
    kspec = pl.BlockSpec((SCONV_W, 512), lambda i: (0, 0))
    wspec = pl.BlockSpec((SG_GROUPS, CHUNK, CHUNK), lambda i: (0, 0, 0))
    bspec = pl.BlockSpec((CHUNK, SG_GROUPS), lambda i: (0, 0))
    nxt_blk = lambda i: (jnp.minimum((i + 1) * (R // H), T // H - 1), 0)
    return _pcall(
        body, name="odd_mid_bwd", grid=(T // R,),
        in_specs=[pl.BlockSpec((R, ODD_IN), lambda i: (i, 0)),
                  pl.BlockSpec((H, ODD_IN), lambda i: (jnp.maximum(i * (R // H) - 1, 0), 0)),
                  pl.BlockSpec((H, ODD_IN), nxt_blk),
                  pl.BlockSpec((R, D), lambda i: (i, 0)),
                  pl.BlockSpec((H, D), nxt_blk),
                  kspec, vec, vec, wspec, bspec],
        out_specs=[pl.BlockSpec((R, ODD_IN), lambda i: (i, 0)), kspec, vec, vec, wspec, bspec],
        out_shape=[jax.ShapeDtypeStruct((T, ODD_IN), BF16), jax.ShapeDtypeStruct((SCONV_W, 512), F32),
                   jax.ShapeDtypeStruct((1, 512), F32), jax.ShapeDtypeStruct((1, 512), F32),
                   jax.ShapeDtypeStruct((SG_GROUPS, CHUNK, CHUNK), F32), jax.ShapeDtypeStruct((CHUNK, SG_GROUPS), F32)],
        compiler_params=_params("arbitrary"),
    )(z, z, z, dcat, dcat, conv_k, ln_g, ln_b, sg_w, sg_bt)


def _ffn_up(tag, hn, weight):
    def act(acc):
        r = jnp.maximum(acc, 0.0)
        return (r * r,)

    return _mm(f"ffn{tag}_up", "nn", hn, weight(f"ffn_w1_{tag}", hn), T, D_FF, D, (BF16,), epi=act)


FFN_BWD_TILE = 256


def _ffn_dx(name, dout, w2, w1, f, h, g, tie=None):
    tm = FFN_BWD_TILE
    ties = () if tie is None else (tie,)

    def body(d_ref, w2_ref, w1_ref, f_ref, h_ref, g_ref, *rest):
        du_ref, dh_ref, dg_ref = rest[len(ties):]
        dres = d_ref[...]
        du = (_nt(dres.astype(BF16), w2_ref[...]) * (2.0 * jnp.sqrt(f_ref[...].astype(F32)))).astype(BF16)
        du_ref[...] = du
        dy = _nt(du, w1_ref[...])
        x = h_ref[...]
        r = lax.rsqrt(jnp.mean(x * x, axis=-1, keepdims=True) + EPS)
        nrm = x * r
        dn = dy * g_ref[...]
        dh_ref[...] = dres + r * (dn - nrm * jnp.mean(dn * nrm, axis=-1, keepdims=True))

        @pl.when(pl.program_id(0) == 0)
        def _():
            dg_ref[...] = jnp.zeros_like(dg_ref)

        dg_ref[...] += jnp.sum(dy * nrm, axis=0, keepdims=True)

    row = pl.BlockSpec((tm, D), lambda i: (i, 0))
    wide = pl.BlockSpec((tm, D_FF), lambda i: (i, 0))
    vec = pl.BlockSpec((1, D), lambda i: (0, 0))
    return _pcall(
        body, name=name, grid=(T // tm,),
        in_specs=[row, _resident((D_FF, D)), _resident((D, D_FF)), wide, row, vec]
        + [pl.BlockSpec(TOKEN_SHAPE, lambda i: (0, 0))] * len(ties),
        out_specs=[wide, row, vec],
        out_shape=[jax.ShapeDtypeStruct((T, D_FF), BF16), jax.ShapeDtypeStruct((T, D), F32),
                   jax.ShapeDtypeStruct((1, D), F32)],
        compiler_params=_params("arbitrary"),
    )(dout, w2, w1, f, h, g, *ties)


def _ffn_bwd(tag, h, g, weight, emit, saved, dout, tie=None):
    hn, f = saved
    du, dh, dg = _ffn_dx(f"ffn{tag}_dx", dout, weight(f"ffn_w2_{tag}", dout), weight(f"ffn_w1_{tag}", dout), f, h, g, tie)
    emit(f"ffn_w2_{tag}", f, dout)
    return dh, dg, emit(f"ffn_w1_{tag}", hn, du)


def _rope_tables():
    half = HEAD_DIM // 2
    inv = 10000.0 ** (-jnp.arange(half, dtype=F32) / half)
    ang = jnp.arange(T, dtype=F32)[:, None] * inv[None, :]
    cos, sin = jnp.cos(ang), jnp.sin(ang)
    c = jnp.tile(jnp.concatenate([cos, cos], axis=1), (1, LANES // HEAD_DIM))
    s = jnp.tile(jnp.concatenate([-sin, sin], axis=1), (1, LANES // HEAD_DIM))
    return c, s


def _local_step(x, target, p, weight, emit, emit_small, before=lambda name, after: None):
    rope_c, rope_s = _rope_tables()
    grads = {}

    hn0 = _rms_fwd("mix0_norm", x, p["norm_mix_g0"], tie=p.get("first_norm_after"))
    zc = _mm("even_in_conv", "nn", hn0, weight("even_w_in", hn0), T, 2 * CONV_CH, D, (F32,))
    qkv = _qkv_proj(hn0, weight("even_w_in", hn0), rope_c, rope_s)
    before("econv_fwd", qkv)
    cv, cat0 = _econv_fwd(zc, p["even_conv_k"], p["even_conv_b"], p["even_ln_g"], p["even_ln_b"])
    att_parts = [_attn_fwd(qkv, 0)]
    before("attn_fwd1", att_parts[0][0])
    att_parts += [_attn_fwd(qkv, 1), _attn_fwd(qkv, 2)]
    outs = [a[0] for a in att_parts]
    lses = [a[1] for a in att_parts]
    cat0, att, w0, w1, w2 = _attn_merge(outs, lses, cat0)
    wgts = (w0, w1, w2)
    h1, hnf0 = _mm_out_norm("even_out", cat0, weight("even_w_out", cat0), D, x, p["norm_ffn_g0"])
    f0 = _ffn_up(0, hnf0, weight)
    before("ffn0_down", f0)
    h2, hn1 = _mm_out_norm("ffn0_down", f0, weight("ffn_w2_0", f0), D_FF, h1, p["norm_mix_g1"], tm=FFN0_DOWN_TILE)

    z1 = _mm("odd_in", "nn", hn1, weight("odd_w_in", hn1), T, ODD_IN, D, (F32,))
    before("odd_mid_fwd", z1)
    cat1 = _odd_mid_fwd(z1, p["odd_conv_k"], p["odd_ln_g"], p["odd_ln_b"], p["odd_sg_w"], p["odd_sg_bt"])
    h3, hnf1 = _mm_out_norm("odd_out", cat1, weight("odd_w_out", cat1), D, h2, p["norm_ffn_g1"])
    f1, dh4, grads["final_g"], loss = _ffn_last("ffn1_loss", hnf1, weight("ffn_w1_1", hnf1), weight("ffn_w2_1", hnf1),
                                                h3, p["final_g"], target)

    dh3, grads["norm_ffn_g1"], tok = _ffn_bwd(1, h3, p["norm_ffn_g1"], weight, emit, (hnf1, f1), dh4)
    tok = emit("odd_w_out", cat1, dh3, tie=tok)
    dcat1 = _mm("odd_out_dx", "nt", dh3, weight("odd_w_out", dh3), T, D, D, (F32,), tie=tok)
    dz1, grads["odd_conv_k"], grads["odd_ln_g"], grads["odd_ln_b"], grads["odd_sg_w"], grads["odd_sg_bt"] = _odd_mid_bwd(
        z1, dcat1, p["odd_conv_k"], p["odd_ln_g"], p["odd_ln_b"], p["odd_sg_w"], p["odd_sg_bt"])
    tok = emit("odd_w_in", hn1, dz1)
    dh2, grads["norm_mix_g1"] = _mm_dx_norm("odd_in_dx", dz1, weight("odd_w_in", dz1), ODD_IN, h2, p["norm_mix_g1"],
                                            dh3, tie=tok)

    dh1, grads["norm_ffn_g0"], tok = _ffn_bwd(0, h1, p["norm_ffn_g0"], weight, emit, (hnf0, f0), dh2)
    tok = emit("even_w_out", cat0, dh1, tie=tok)
    dcat0 = _mm("even_out_dx", "nt", dh1, weight("even_w_out", dh1), T, D, D, (F32,), tie=tok)
    dcv, grads["even_ln_g"], grads["even_ln_b"], grads["even_conv_b"] = _econv_bwd_ln(
        cv, dcat0, p["even_ln_g"], p["even_ln_b"])
    dz0, grads["even_conv_k"] = _econv_bwd_conv(dcv, zc, p["even_conv_k"])
    tok = emit_small(loss, grads)
    dqkv = lax.empty((3, 3, 4, T, LANES), F32)
    for g in range(3):
        dqkv = _attn_bwd(qkv, lses[g], wgts[g], att, dcat0, dqkv, g)
    before("rope_bwd", dqkv)
    dz0 = _rope_bwd(dqkv, rope_c, rope_s, dz0)
    tok = emit("even_w_in", hn0, dz0, tie=tok)
    dx, dg0 = _mm_dx_norm("even_in_dx", dz0, weight("even_w_in", dz0), EVEN_IN, x, p["norm_mix_g0"], dh1, tie=tok)
    return dx, dg0


def _rowwise(name, fn, ins, out_dtypes, tm=256, tie=None):
    rows, cols = ins[0].shape
    tm = tm if rows % tm == 0 else rows
    n_in = len(ins)
    ties = () if tie is None else (tie,)

    def body(*refs):
        vals = fn(*[r[...] for r in refs[:n_in]])
        for o_ref, v in zip(refs[n_in + len(ties):], vals):
            o_ref[...] = v.astype(o_ref.dtype)

    spec = pl.BlockSpec((tm, cols), lambda i: (i, 0))
    outs = _pcall(
        body, name=name, grid=(rows // tm,),
        in_specs=[spec] * n_in + [pl.BlockSpec(TOKEN_SHAPE, lambda i: (0, 0))] * len(ties),
        out_specs=[spec] * len(out_dtypes),
        out_shape=[jax.ShapeDtypeStruct((rows, cols), dt) for dt in out_dtypes],
        compiler_params=_params("parallel"),
    )(*ins, *ties)
    return outs[0] if len(out_dtypes) == 1 else outs


def _adamw(name, w, g, m, v, with_grad=False, tie=None):
    c1 = 1.0 - ADAM_B1 ** ADAM_STEP
    c2 = 1.0 - ADAM_B2 ** ADAM_STEP

    def fn(w_t, g_t, m_t, v_t):
        m_new = ADAM_B1 * m_t + (1.0 - ADAM_B1) * g_t
        v_new = ADAM_B2 * v_t + (1.0 - ADAM_B2) * (g_t * g_t)
        delta = -ADAM_LR * ((m_new / c1) / (jnp.sqrt(v_new / c2) + ADAM_EPS) + ADAM_WD * w_t)
        return (delta, m_new, v_new, g_t) if with_grad else (delta, m_new, v_new)

    return _rowwise(name, fn, (w, g, m, v), (F32,) * (4 if with_grad else 3), tie=tie)


class _Piece:
    def __init__(self, name, rows, cols, axis, src, src_row0):
        self.name, self.rows, self.cols, self.axis = name, rows, cols, axis
        self.width = (cols if axis == 1 else rows) // 4
        self.src, self.src_row0 = src, src_row0

    @property
    def full_shape(self):
        return (self.rows, self.cols)

    @property
    def half_shape(self):
        return (self.rows // 2, self.cols) if self.axis == 1 else (self.rows, self.cols // 2)

    @property
    def shard_half_shape(self):
        return (self.rows // 2, self.width) if self.axis == 1 else (self.width, self.cols // 2)

    def shard_whole(self, ref):
        n = self.rows if self.axis == 1 else self.width
        return ref.at[pl.ds(self.src_row0, n), :]

    def shard_half(self, ref, h):
        if self.axis == 1:
            return ref.at[pl.ds(self.src_row0 + h * (self.rows // 2), self.rows // 2), :]
        return ref.at[pl.ds(self.src_row0, self.width), pl.ds(h * (self.cols // 2), self.cols // 2)]

    def full_shard(self, ref, s):
        if self.axis == 1:
            return ref.at[:, pl.ds(s * self.width, self.width)]
        return ref.at[pl.ds(s * self.width, self.width), :]

    def full_shard_half(self, ref, s, h):
        if self.axis == 1:
            return ref.at[pl.ds(h * (self.rows // 2), self.rows // 2), pl.ds(s * self.width, self.width)]
        return ref.at[pl.ds(s * self.width, self.width), pl.ds(h * (self.cols // 2), self.cols // 2)]

    def full_half(self, ref, h):
        if self.axis == 1:
            return ref.at[pl.ds(h * (self.rows // 2), self.rows // 2), :]
        return ref.at[:, pl.ds(h * (self.cols // 2), self.cols // 2)]

    def full_half_rows(self, ref, h, r0, n):
        if self.axis == 1:
            return ref.at[pl.ds(h * (self.rows // 2) + r0, n), :]
        return ref.at[pl.ds(r0, n), pl.ds(h * (self.cols // 2), self.cols // 2)]

    def half_shard(self, ref, s):
        return self.full_shard(ref, s)


PIECES = (
    _Piece("even_w_in", D, EVEN_IN, 1, 0, 0),
    _Piece("even_w_out", D, D, 0, 1, 0),
    _Piece("ffn_w1_0", D, D_FF, 1, 4, 0),
    _Piece("ffn_w2_0", D_FF, D, 0, 5, 0),
    _Piece("odd_w_in", D, ODD_IN, 1, 2, 0),
    _Piece("odd_w_out", D, D, 0, 3, 0),
    _Piece("ffn_w1_1", D, D_FF, 1, 4, D),
    _Piece("ffn_w2_1", D_FF, D, 0, 5, D_FF // 4),
)
N_PIECES = len(PIECES)
FORWARD_RIDES = {"attn_fwd1": (1, 2, 3), "ffn0_down": (4, 5), "odd_mid_fwd": (6, 7)}
JOIN_GROUPS = ((0, 1, 2, 3), (4, 5))
JOIN_RIDES_IN = "rope_bwd"
HOLD_BACK = ("ffn_w2_0", "ffn_w2_1", "odd_w_out")
N_SHARD_OPERANDS = 6
ANY = pl.BlockSpec(memory_space=pl.ANY)
MESH = pl.DeviceIdType.MESH


def _mesh_place():
    x, y, c = lax.axis_index("x"), lax.axis_index("y"), lax.axis_index("c")
    chips = [(1 - x, y), (x, 1 - y), (1 - x, 1 - y)]
    return x, y, c, chips


def _remote(src, dst, send_sem, recv_sem, dev):
    return pltpu.make_async_remote_copy(src_ref=src, dst_ref=dst, send_sem=send_sem, recv_sem=recv_sem,
                                        device_id=dev, device_id_type=MESH)


HBM = pl.BlockSpec(memory_space=pltpu.HBM)
SEM = pl.BlockSpec(memory_space=pltpu.SEMAPHORE)
SPLIT_PARAMS = pltpu.CompilerParams(has_side_effects=pltpu.SideEffectType.DATAFLOW_SIDE_EFFECTING)
CAST_TILE = 256
SIDE_JOB_COPY_PRIORITY = 1


def _in_hbm(a):
    return pltpu.with_memory_space_constraint(a, pltpu.HBM)


def _cast_place(pc, shard_operand, chip, tie=None):
    rows, cols = (pc.rows, pc.width) if pc.axis == 1 else (pc.width, pc.cols)
    nblk = rows // CAST_TILE
    blk0 = pc.src_row0 // CAST_TILE
    ties = () if tie is None else (tie,)

    def body(chip_ref, x_ref, *rest):
        del chip_ref
        rest[-1][...] = x_ref[...].astype(BF16)

    if pc.axis == 1:
        out_map = lambda i, chip_ref: (i, chip_ref[0])
    else:
        out_map = lambda i, chip_ref: (chip_ref[0] * nblk + i, 0)
    return _pcall(
        body, name=f"cast_{pc.name}",
        grid_spec=pltpu.PrefetchScalarGridSpec(
            num_scalar_prefetch=1, grid=(nblk,),
            in_specs=[pl.BlockSpec((CAST_TILE, cols), lambda i, chip_ref: (blk0 + i, 0))]
            + [pl.BlockSpec(TOKEN_SHAPE, lambda i, chip_ref: (0, 0))] * len(ties),
            out_specs=pl.BlockSpec((CAST_TILE, cols), out_map)),
        out_shape=jax.ShapeDtypeStruct(pc.full_shape, BF16),
        compiler_params=_params("parallel"),
    )(chip, shard_operand, *ties)


def _gather_start(name, pieces, fulls):
    n = len(pieces)

    def body(*refs):
        ins = refs[:n]
        sends = refs[2 * n:3 * n]
        recvs = refs[3 * n:4 * n]
        token = refs[4 * n]
        x, y, c, chips = _mesh_place()
        s = 2 * x + y
        for i, pc in enumerate(pieces):
            win = pc.full_shard_half(ins[i], s, c)
            for k, (cx, cy) in enumerate(chips):
                _remote(win, win, sends[i].at[k], recvs[i].at[k], (cx, cy, c)).start()
        token[...] = jnp.zeros(TOKEN_SHAPE, F32)

    sems = [pltpu.SemaphoreType.DMA((3,))] * (2 * n)
    outs = _pcall(
        body, name=name,
        in_specs=[HBM] * n,
        out_specs=[HBM] * n + [SEM] * (2 * n) + [pl.BlockSpec(memory_space=pltpu.VMEM)],
        out_shape=[pltpu.HBM(pc.full_shape, BF16) for pc in pieces] + sems + [jax.ShapeDtypeStruct(TOKEN_SHAPE, F32)],
        input_output_aliases={i: i for i in range(n)},
        compiler_params=SPLIT_PARAMS,
    )(*[_in_hbm(f) for f in fulls])
    return outs[:n], outs[n:2 * n], outs[2 * n:3 * n], outs[3 * n]


def _gather_wait(pc, full, send_sems, recv_sems, after):
    def body(full_ref, send_ref, recv_ref, after_ref, out_ref):
        del after_ref, out_ref
        x, y, c, chips = _mesh_place()
        for k, (cx, cy) in enumerate(chips):
            win = pc.full_shard_half(full_ref, 2 * cx + cy, c)
            cp = _remote(win, win, send_ref.at[k], recv_ref.at[k], (cx, cy, c))
            cp.wait_send()
            cp.wait_recv()

    return _pcall(
        body, name=f"gather_wait_{pc.name}",
        in_specs=[HBM, SEM, SEM, ANY], out_specs=HBM, out_shape=pltpu.HBM(pc.full_shape, BF16),
        input_output_aliases={0: 0}, compiler_params=SPLIT_PARAMS,
    )(full, send_sems, recv_sems, after)


def _core_forward_job(pieces, fulls):
    n = len(pieces)

    def copies(ins, outs, scr):
        send_bufs, recv_bufs = scr[:n], scr[n:2 * n]
        load_sems, send_sems, recv_sems, store_sems = scr[2 * n:]
        x, y, c, chips = _mesh_place()
        loads, sends, stores = [], [], []
        for i, pc in enumerate(pieces):
            for k, (cx, cy) in enumerate(chips):
                j = 3 * i + k
                loads.append(pltpu.make_async_copy(pc.full_shard_half(ins[i], 2 * cx + cy, c), send_bufs[i].at[k],
                                                   load_sems.at[j]))
                sends.append(_remote(send_bufs[i].at[k], recv_bufs[i].at[k], send_sems.at[j], recv_sems.at[j],
                                     (x, y, 1 - c)))
                stores.append(pltpu.make_async_copy(recv_bufs[i].at[k], pc.full_shard_half(outs[i], 2 * cx + cy, 1 - c),
                                                    store_sems.at[j]))
        return loads, sends, stores

    def begin(ins, outs, scr):
        for cp in copies(ins, outs, scr)[0]:
            cp.start(priority=SIDE_JOB_COPY_PRIORITY)

    def advance(ins, outs, scr):
        loads, sends, _ = copies(ins, outs, scr)
        for load, send in zip(loads, sends):
            load.wait()
            send.start()

    def finish(ins, outs, scr):
        _, sends, stores = copies(ins, outs, scr)
        for send, store in zip(sends, stores):
            send.wait_recv()
            store.start(priority=SIDE_JOB_COPY_PRIORITY)
        for send, store in zip(sends, stores):
            send.wait_send()
            store.wait()

    sems = pltpu.SemaphoreType.DMA((3 * n,))
    bufs = [pltpu.VMEM((3,) + pc.shard_half_shape, BF16) for pc in pieces]
    return _SideJob(fulls, [jax.ShapeDtypeStruct(pc.full_shape, BF16) for pc in pieces],
                    bufs + bufs + [sems, sems, sems, sems], {i: i for i in range(n)}, begin, advance, finish)


def _run_job(name, job):
    def body(o_ref):
        o_ref[...] = jnp.zeros(TOKEN_SHAPE, F32)

    _ride_next_call(job)
    _pcall(body, name=name, grid=(3,), in_specs=[], out_specs=pl.BlockSpec(TOKEN_SHAPE, lambda i: (0, 0)),
           out_shape=jax.ShapeDtypeStruct(TOKEN_SHAPE, F32))()
    return job.results


def _core_forward(pieces, fulls):
    n = len(pieces)

    def body(*refs):
        ins, outs = refs[:n], refs[n:2 * n]
        send_bufs, recv_bufs = refs[2 * n:3 * n], refs[3 * n:4 * n]
        load_sems, send_sems, recv_sems, store_sems = refs[4 * n:]
        x, y, c, chips = _mesh_place()
        loads, sends, stores = [], [], []
        for i, pc in enumerate(pieces):
            for k, (cx, cy) in enumerate(chips):
                cp = pltpu.make_async_copy(pc.full_shard_half(ins[i], 2 * cx + cy, c), send_bufs[i].at[k],
                                           load_sems.at[3 * i + k])
                cp.start()
                loads.append(cp)
        _sibling_handshake()
        for i in range(n):
            for k in range(3):
                j = 3 * i + k
                loads[j].wait()
                cp = _remote(send_bufs[i].at[k], recv_bufs[i].at[k], send_sems.at[j], recv_sems.at[j], (x, y, 1 - c))
                cp.start()
                sends.append(cp)
        for i, pc in enumerate(pieces):
            for k, (cx, cy) in enumerate(chips):
                j = 3 * i + k
                sends[j].wait_recv()
                cp = pltpu.make_async_copy(recv_bufs[i].at[k], pc.full_shard_half(outs[i], 2 * cx + cy, 1 - c),
                                           store_sems.at[j])
                cp.start()
                stores.append(cp)
        for j in range(3 * n):
            sends[j].wait_send()
            stores[j].wait()

    sems = pltpu.SemaphoreType.DMA((3 * n,))
    bufs = [pltpu.VMEM((3,) + pc.shard_half_shape, BF16) for pc in pieces]
    return _pcall(
        body, name="core_forward_" + pieces[0].name, in_specs=[ANY] * n, out_specs=[ANY] * n,
        out_shape=[jax.ShapeDtypeStruct(pc.full_shape, BF16) for pc in pieces],
        scratch_shapes=bufs + bufs + [sems, sems, sems, sems],
        input_output_aliases={i: i for i in range(n)},
        compiler_params=pltpu.CompilerParams(vmem_limit_bytes=VMEM_LIMIT, collective_id=SIBLING_COLLECTIVE_ID),
    )(*fulls)


def _dw_tile(pc):
    if pc.axis == 1:
        tn = max(t for t in range(LANES, pc.cols + 1, LANES) if pc.cols % t == 0 and t <= 1408)
        return pc.rows // 2, tn
    return min(pc.rows, 1024), pc.cols // 2


def _mm_dw_chipsum(pc, a, b, core, tie=None):
    tm, tn = _dw_tile(pc)
    hr, hc = pc.half_shape
    tiles_r, tiles_c = hr // tm, hc // tn
    th = tiles_r * tiles_c
    ties = () if tie is None else (tie,)

    def tile_of(s, core_ref):
        mine = s >= th
        half = jnp.where(mine, core_ref[0], 1 - core_ref[0])
        local = s % th
        li, lj = local // tiles_c, local % tiles_c
        if pc.axis == 1:
            return half * tiles_r + li, lj, li, lj, mine
        return li, half * tiles_c + lj, li, lj, mine

    def body(core_ref, a_ref, b_ref, *rest):
        o_ref, send_buf, recv_buf, send_sems, recv_sems = rest[len(ties):]
        s = pl.program_id(0)
        local = s % th
        x, y, c = lax.axis_index("x"), lax.axis_index("y"), lax.axis_index("c")
        acc = _tn(a_ref[...].astype(BF16), b_ref[...].astype(BF16))

        def push(slot):
            return _remote(send_buf.at[slot], recv_buf.at[slot], send_sems.at[slot], recv_sems.at[slot], (x, y, 1 - c))

        @pl.when(s == 0)
        def _():
            _sibling_handshake()

        @pl.when(s < th)
        def _():
            send_buf[local] = acc.astype(BF16)
            push(local).start()

        @pl.when(s >= th)
        def _():
            push(local).wait_recv()
            o_ref[...] = (acc + recv_buf[local].astype(F32)).astype(BF16)

        @pl.when(s == 2 * th - 1)
        def _():
            for slot in range(th):
                push(slot).wait_send()

    def a_map(s, core_ref):
        return 0, tile_of(s, core_ref)[0]

    def b_map(s, core_ref):
        return 0, tile_of(s, core_ref)[1]

    def o_map(s, core_ref):
        _, _, li, lj, mine = tile_of(s, core_ref)
        return jnp.where(mine, li, 0), jnp.where(mine, lj, 0)

    return _pcall(
        body, name=f"dw_{pc.name}",
        grid_spec=pltpu.PrefetchScalarGridSpec(
            num_scalar_prefetch=1, grid=(2 * th,),
            in_specs=[pl.BlockSpec((T, tm), a_map), pl.BlockSpec((T, tn), b_map)]
            + [pl.BlockSpec(TOKEN_SHAPE, lambda s, core_ref: (0, 0))] * len(ties),
            out_specs=pl.BlockSpec((tm, tn), o_map),
            scratch_shapes=[pltpu.VMEM((th, tm, tn), BF16), pltpu.VMEM((th, tm, tn), BF16),
                            pltpu.SemaphoreType.DMA((th,)), pltpu.SemaphoreType.DMA((th,))]),
        out_shape=jax.ShapeDtypeStruct((hr, hc), BF16),
        compiler_params=pltpu.CompilerParams(dimension_semantics=("arbitrary",), vmem_limit_bytes=VMEM_LIMIT,
                                             collective_id=SIBLING_COLLECTIVE_ID),
    )(core, a, b, *ties)


def _scatter_start(pieces, chip_sums):
    n = len(pieces)

    def body(*refs):
        sums, lands = refs[:n], refs[n:2 * n]
        sends, recvs = refs[4 * n:5 * n], refs[5 * n:6 * n]
        token = refs[6 * n]
        x, y, c, chips = _mesh_place()
        for i, pc in enumerate(pieces):
            for k, (cx, cy) in enumerate(chips):
                _remote(pc.half_shard(sums[i], 2 * cx + cy), lands[i].at[k], sends[i].at[k], recvs[i].at[k],
                        (cx, cy, c)).start()
        token[...] = jnp.zeros(TOKEN_SHAPE, F32)

    land_shapes = [(3,) + pc.shard_half_shape for pc in pieces]
    sems = [pltpu.SemaphoreType.DMA((3,))] * (2 * n)
    outs = _pcall(
        body, name="scatter_start_" + pieces[0].name,
        in_specs=[HBM] * (2 * n), out_specs=[HBM] * (2 * n) + [SEM] * (2 * n) + [pl.BlockSpec(memory_space=pltpu.VMEM)],
        out_shape=[pltpu.HBM(pc.half_shape, BF16) for pc in pieces] + [pltpu.HBM(sh, BF16) for sh in land_shapes]
        + sems + [jax.ShapeDtypeStruct(TOKEN_SHAPE, F32)],
        input_output_aliases={i: i for i in range(2 * n)}, compiler_params=SPLIT_PARAMS,
    )(*[_in_hbm(cs) for cs in chip_sums], *[_in_hbm(lax.empty(sh, BF16)) for sh in land_shapes])
    return [(outs[i], outs[n + i], outs[2 * n + i], outs[3 * n + i]) for i in range(n)], outs[4 * n]


def _scatter_wait(pc, chip_sum, land, send_sems, recv_sems, after):
    def body(sum_ref, land_ref, send_ref, recv_ref, after_ref, sum_out, land_out):
        del after_ref, sum_out, land_out
        x, y, c, chips = _mesh_place()
        for k, (cx, cy) in enumerate(chips):
            cp = _remote(pc.half_shard(sum_ref, 2 * cx + cy), land_ref.at[k], send_ref.at[k], recv_ref.at[k], (cx, cy, c))
            cp.wait_send()
            cp.wait_recv()

    return _pcall(
        body, name=f"scatter_wait_{pc.name}",
        in_specs=[HBM, HBM, SEM, SEM, ANY], out_specs=[HBM, HBM],
        out_shape=[pltpu.HBM(pc.half_shape, BF16), pltpu.HBM((3,) + pc.shard_half_shape, BF16)],
        input_output_aliases={0: 0, 1: 1}, compiler_params=SPLIT_PARAMS,
    )(chip_sum, land, send_sems, recv_sems, after)


SHARD_OPERAND_SHAPES = ((D, EVEN_IN // 4), (D // 4, D), (D, ODD_IN // 4), (D // 4, D), (2 * D, D_FF // 4), (2 * D_FF // 4, D))


def _allsum_join_job(operands, chip_sums, lands):
    pieces = [pc for pc in PIECES if pc.src in operands]
    n = len(pieces)

    def copies(ins, outs, scr):
        sum_refs, land_refs = ins[:n], ins[n:]
        out_refs = dict(zip(operands, outs))
        in_bufs, fin_bufs, recv_bufs = scr[:n], scr[n:2 * n], scr[2 * n:3 * n]
        load_sems, send_sems, recv_sems, out_sems = scr[3 * n:]
        x, y, c, _ = _mesh_place()
        s = 2 * x + y
        loads, sends, mine, theirs = [], [], [], []
        for j, pc in enumerate(pieces):
            loads.append((pltpu.make_async_copy(land_refs[j], in_bufs[j].at[pl.ds(0, 3)], load_sems.at[2 * j]),
                          pltpu.make_async_copy(pc.half_shard(sum_refs[j], s), in_bufs[j].at[3], load_sems.at[2 * j + 1])))
            sends.append(_remote(fin_bufs[j], recv_bufs[j], send_sems.at[j], recv_sems.at[j], (x, y, 1 - c)))
            mine.append(pltpu.make_async_copy(fin_bufs[j], pc.shard_half(out_refs[pc.src], c), out_sems.at[2 * j]))
            theirs.append(pltpu.make_async_copy(recv_bufs[j], pc.shard_half(out_refs[pc.src], 1 - c), out_sems.at[2 * j + 1]))
        return loads, sends, mine, theirs, in_bufs, fin_bufs

    def begin(ins, outs, scr):
        for a, b in copies(ins, outs, scr)[0]:
            a.start()
            b.start()

    def advance(ins, outs, scr):
        loads, sends, mine, _, in_bufs, fin_bufs = copies(ins, outs, scr)
        for j in range(n):
            loads[j][0].wait()
            loads[j][1].wait()
            acc = in_bufs[j][0].astype(F32)
            for k in range(1, 4):
                acc = acc + in_bufs[j][k].astype(F32)
            fin_bufs[j][...] = acc
            mine[j].start()
            sends[j].start()

    def finish(ins, outs, scr):
        _, sends, mine, theirs, _, _ = copies(ins, outs, scr)
        for j in range(n):
            sends[j].wait_recv()
            theirs[j].start()
        for j in range(n):
            sends[j].wait_send()
            mine[j].wait()
            theirs[j].wait()

    halves = [pc.shard_half_shape for pc in pieces]
    scratch = ([pltpu.VMEM((4,) + sh, BF16) for sh in halves] + [pltpu.VMEM(sh, F32) for sh in halves] * 2
               + [pltpu.SemaphoreType.DMA((2 * n,)), pltpu.SemaphoreType.DMA((n,)), pltpu.SemaphoreType.DMA((n,)),
                  pltpu.SemaphoreType.DMA((2 * n,))])
    return _SideJob(list(chip_sums) + list(lands), [jax.ShapeDtypeStruct(SHARD_OPERAND_SHAPES[o], F32) for o in operands],
                    scratch, {}, begin, advance, finish)


PEER_FLIPS = tuple((a, b, e) for a in (0, 1) for b in (0, 1) for e in (0, 1) if (a, b, e) != (0, 0, 0))


def _peers():
    x, y, c = lax.axis_index("x"), lax.axis_index("y"), lax.axis_index("c")
    me = 4 * x + 2 * y + c
    out = []
    for a, b, e in PEER_FLIPS:
        px, py, pc = (1 - x if a else x), (1 - y if b else y), (1 - c if e else c)
        out.append(((px, py, pc), 4 * px + 2 * py + pc))
    return me, out


def _exchange8_start(name, blk, tie=None):
    m = blk.shape[0]
    ties = () if tie is None else (tie,)

    def body(blk_ref, land_ref, *rest):
        sends, recvs, token = rest[len(ties) + 2:]
        me, peers = _peers()
        for k, (dev, _) in enumerate(peers):
            _remote(blk_ref, land_ref.at[me], sends.at[k], recvs.at[k], dev).start()
        token[...] = jnp.zeros(TOKEN_SHAPE, F32)

    sems = pltpu.SemaphoreType.DMA((7,))
    return _pcall(
        body, name=name,
        in_specs=[HBM, HBM] + [pl.BlockSpec(memory_space=pltpu.VMEM)] * len(ties),
        out_specs=[HBM, HBM, SEM, SEM, pl.BlockSpec(memory_space=pltpu.VMEM)],
        out_shape=[pltpu.HBM((m, LANES), F32), pltpu.HBM((8, m, LANES), F32), sems, sems,
                   jax.ShapeDtypeStruct(TOKEN_SHAPE, F32)],
        input_output_aliases={0: 0, 1: 1}, compiler_params=SPLIT_PARAMS,
    )(_in_hbm(blk), _in_hbm(lax.empty((8, m, LANES), F32)), *ties)


def _exchange8_wait(name, blk, land, send_sems, recv_sems, after):
    def body(blk_ref, land_ref, send_ref, recv_ref, after_ref, blk_out, land_out):
        del after_ref, blk_out, land_out
        _, peers = _peers()
        for k, (dev, slot) in enumerate(peers):
            cp = _remote(blk_ref, land_ref.at[slot], send_ref.at[k], recv_ref.at[k], dev)
            cp.wait_send()
            cp.wait_recv()

    m = blk.shape[0]
    return _pcall(
        body, name=name,
        in_specs=[HBM, HBM, SEM, SEM, ANY], out_specs=[HBM, HBM],
        out_shape=[pltpu.HBM((m, LANES), F32), pltpu.HBM((8, m, LANES), F32)],
        input_output_aliases={0: 0, 1: 1}, compiler_params=SPLIT_PARAMS,
    )(blk, land, send_sems, recv_sems, after)


def _collect8(name, blk, land, with_sum):
    m = blk.shape[0]

    def body(blk_ref, land_ref, out_ref, *scratch):
        sems = scratch[-1]
        dst = scratch[0] if with_sum else out_ref
        me, peers = _peers()
        copies = [pltpu.make_async_copy(blk_ref, dst.at[me], sems.at[7])]
        for k, (_, slot) in enumerate(peers):
            copies.append(pltpu.make_async_copy(land_ref.at[slot], dst.at[slot], sems.at[k]))
        for cp in copies:
            cp.start()
        for cp in copies:
            cp.wait()
        if with_sum:
            acc = dst[0]
            for dev in range(1, 8):
                acc = acc + dst[dev]
            out_ref[...] = acc

    all_shape = (8, m, LANES)
    return _pcall(
        body, name=name, in_specs=[ANY, ANY], out_specs=pl.BlockSpec(memory_space=pltpu.VMEM),
        out_shape=jax.ShapeDtypeStruct((m, LANES) if with_sum else all_shape, F32),
        scratch_shapes=([pltpu.VMEM(all_shape, F32)] if with_sum else []) + [pltpu.SemaphoreType.DMA((8,))],
    )(blk, land)


def _pack(arrays, row_counts):
    rows = []
    for a, n in zip(arrays, row_counts):
        flat = a.reshape(-1, LANES)
        rows.append(jnp.pad(flat, ((0, n - flat.shape[0]), (0, 0))))
    return jnp.concatenate(rows, axis=0)


REPL_NAMES = ("norm_mix_g", "norm_ffn_g", "even_conv_b", "even_ln_g", "even_ln_b", "odd_sg_w", "odd_sg_b", "final_g")
REPL_SHAPES = ((2, D), (2, D), (1, 512), (1, 512), (1, 512), (1, SG_GROUPS, CHUNK, CHUNK), (1, SG_GROUPS, CHUNK), (D,))
REPL_ROWS = (16, 16, 8, 8, 8, 512, 8, 8)
SHARDED_NAMES = ("even_conv_k", "odd_conv_k", "odd_ln_g", "odd_ln_b")
SHARDED_SHARD_SHAPES = ((1, CONV_W, LANES), (1, SCONV_W, LANES), (1, LANES), (1, LANES))
SHARDED_SHARD_ROWS = (32, 8, 8, 8)
SHARDED_FULL_SHAPES = ((CONV_W, 512), (SCONV_W, 512), (1, 512), (1, 512))
SHARDED_FULL_ROWS = (128, 16, 8, 8)
SMALL_NAMES = REPL_NAMES + SHARDED_NAMES
SMALL_ROWS = REPL_ROWS + SHARDED_SHARD_ROWS
SMALL_OUT_SHAPES = REPL_SHAPES[:-1] + ((1, D),) + SHARDED_SHARD_SHAPES
LOSS_ROWS = 8


def _offsets(rows):
    out, r0 = [], 0
    for n in rows:
        out.append(r0)
        r0 += n
    return out


def _adamw_small(w_pack, m_pack, v_pack, grad_sum, first_gain_sum):
    n_rows = sum(SMALL_ROWS)
    state_at = _offsets(SMALL_ROWS)
    grad_at = _offsets((LOSS_ROWS, 8) + REPL_ROWS[1:] + SHARDED_FULL_ROWS)[1:]
    c1 = 1.0 - ADAM_B1 ** ADAM_STEP
    c2 = 1.0 - ADAM_B2 ** ADAM_STEP
    n_repl = len(REPL_NAMES)

    def body(w_ref, m_ref, v_ref, g_ref, g0_ref, *rest):
        outs, gbuf = rest[:-1], rest[-1]
        chip = 2 * lax.axis_index("x") + lax.axis_index("y")
        gbuf[...] = jnp.zeros((n_rows, LANES), F32)
        gbuf[0:8, :] = g0_ref[...]
        gbuf[8:16, :] = g_ref[grad_at[0]:grad_at[0] + 8, :]
        for i in range(1, n_repl):
            gbuf[state_at[i]:state_at[i] + REPL_ROWS[i], :] = g_ref[grad_at[i]:grad_at[i] + REPL_ROWS[i], :]
        for k, shape in enumerate(SHARDED_SHARD_SHAPES):
            used = shape[-2] if len(shape) == 3 else 1
            src = pl.ds(grad_at[n_repl + k] + chip, used, stride=4) if used > 1 else pl.ds(grad_at[n_repl + k] + chip, 1)
            gbuf[state_at[n_repl + k]:state_at[n_repl + k] + used, :] = g_ref[src, :]
        g = gbuf[...]
        m_new = ADAM_B1 * m_ref[...] + (1.0 - ADAM_B1) * g
        v_new = ADAM_B2 * v_ref[...] + (1.0 - ADAM_B2) * (g * g)
        delta = -ADAM_LR * ((m_new / c1) / (jnp.sqrt(v_new / c2) + ADAM_EPS) + ADAM_WD * w_ref[...])
        for i, shape in enumerate(SMALL_OUT_SHAPES):
            for j, val in enumerate((g, delta, m_new, v_new)):
                o_ref = outs[4 * i + j]
                rows = val[state_at[i]:state_at[i] + SMALL_ROWS[i], :]
                if len(shape) == 2 and shape[1] > LANES:
                    per = shape[1] // LANES
                    for r in range(shape[0]):
                        for q in range(per):
                            o_ref[r:r + 1, q * LANES:(q + 1) * LANES] = rows[r * per + q:r * per + q + 1, :]
                elif len(shape) == 4:
                    for grp in range(shape[1]):
                        o_ref[0, grp] = rows[grp * shape[2]:(grp + 1) * shape[2], :]
                elif len(shape) == 3:
                    o_ref[0] = rows[:shape[1], :]
                else:
                    o_ref[...] = rows[:1, :]

    vm = pl.BlockSpec(memory_space=pltpu.VMEM)
    out_shape = [jax.ShapeDtypeStruct(sh, F32) for sh in SMALL_OUT_SHAPES for _ in range(4)]
    outs = _pcall(body, name="adamw_small", in_specs=[vm] * 5, out_specs=[vm] * len(out_shape), out_shape=out_shape,
                  scratch_shapes=[pltpu.VMEM((n_rows, LANES), F32)])(w_pack, m_pack, v_pack, grad_sum, first_gain_sum)
    return {n: outs[4 * i:4 * i + 4] for i, n in enumerate(SMALL_NAMES)}


def _touch(arrays):
    n = len(arrays)

    def body(*refs):
        refs[-1][...] = jnp.zeros(TOKEN_SHAPE, F32)

    outs = _pcall(
        body, name="touch", in_specs=[ANY] * n, out_specs=[ANY] * n + [pl.BlockSpec(memory_space=pltpu.VMEM)],
        out_shape=[jax.ShapeDtypeStruct(a.shape, a.dtype) for a in arrays] + [jax.ShapeDtypeStruct(TOKEN_SHAPE, F32)],
        input_output_aliases={i: i for i in range(n)})(*arrays)
    return outs[:n], outs[n]


def kernel(x, norm_mix_g, norm_ffn_g, even_w_in, even_conv_k, even_conv_b, even_ln_g, even_ln_b, even_w_out, odd_w_in, odd_conv_k, odd_ln_g, odd_ln_b, odd_sg_w, odd_sg_b, odd_w_out, ffn_w1, ffn_w2, final_g, loss_target, m_norm_mix_g, m_norm_ffn_g, m_even_w_in, m_even_conv_k, m_even_conv_b, m_even_ln_g, m_even_ln_b, m_even_w_out, m_odd_w_in, m_odd_conv_k, m_odd_ln_g, m_odd_ln_b, m_odd_sg_w, m_odd_sg_b, m_odd_w_out, m_ffn_w1, m_ffn_w2, m_final_g, v_norm_mix_g, v_norm_ffn_g, v_even_w_in, v_even_conv_k, v_even_conv_b, v_even_ln_g, v_even_ln_b, v_even_w_out, v_odd_w_in, v_odd_conv_k, v_odd_ln_g, v_odd_ln_b, v_odd_sg_w, v_odd_sg_b, v_odd_w_out, v_ffn_w1, v_ffn_w2, v_final_g):
    names = ("norm_mix_g", "norm_ffn_g", "even_w_in", "even_conv_k", "even_conv_b", "even_ln_g", "even_ln_b", "even_w_out",
             "odd_w_in", "odd_conv_k", "odd_ln_g", "odd_ln_b", "odd_sg_w", "odd_sg_b", "odd_w_out", "ffn_w1", "ffn_w2", "final_g")
    w = dict(zip(names, (norm_mix_g, norm_ffn_g, even_w_in, even_conv_k, even_conv_b, even_ln_g, even_ln_b, even_w_out,
                         odd_w_in, odd_conv_k, odd_ln_g, odd_ln_b, odd_sg_w, odd_sg_b, odd_w_out, ffn_w1, ffn_w2, final_g)))
    mom = dict(zip(names, (m_norm_mix_g, m_norm_ffn_g, m_even_w_in, m_even_conv_k, m_even_conv_b, m_even_ln_g, m_even_ln_b,
                           m_even_w_out, m_odd_w_in, m_odd_conv_k, m_odd_ln_g, m_odd_ln_b, m_odd_sg_w, m_odd_sg_b, m_odd_w_out,
                           m_ffn_w1, m_ffn_w2, m_final_g)))
    vel = dict(zip(names, (v_norm_mix_g, v_norm_ffn_g, v_even_w_in, v_even_conv_k, v_even_conv_b, v_even_ln_g, v_even_ln_b,
                           v_even_w_out, v_odd_w_in, v_odd_conv_k, v_odd_ln_g, v_odd_ln_b, v_odd_sg_w, v_odd_sg_b, v_odd_w_out,
                           v_ffn_w1, v_ffn_w2, v_final_g)))
    big_names = ("even_w_in", "even_w_out", "odd_w_in", "odd_w_out", "ffn_w1", "ffn_w2")
    chip = 2 * lax.axis_index("x") + lax.axis_index("y")

    def shard2d(t, name):
        return t[name].reshape(SHARD_OPERAND_SHAPES[big_names.index(name)])

    chip_op = jnp.reshape(chip, (1,)).astype(jnp.int32)
    first = _cast_place(PIECES[0], shard2d(w, big_names[PIECES[0].src]), chip_op)
    fly0, send0, recv0, token = _gather_start("gather_start_first", PIECES[:1], [first])
    small_pack = _pack([w[n] for n in SHARDED_NAMES], SHARDED_SHARD_ROWS)
    small_blk, small_land, small_send, small_recv, small_token = _exchange8_start("gather_small_start", small_pack, token)
    placed = [_cast_place(pc, shard2d(w, big_names[pc.src]), chip_op, tie=small_token) for pc in PIECES[1:]]
    fly1, send1, recv1, all_started = _gather_start("gather_start_rest", PIECES[1:], placed)
    flying, gather_send, gather_recv = fly0 + fly1, send0 + send1, recv0 + recv1
    ready = {}

    names_in_order = [pc.name for pc in PIECES]

    riding = {}

    (*state_packs, _), idle_work_done = _touch(
        [_pack([t[n] for n in SMALL_NAMES], SMALL_ROWS) for t in (w, mom, vel)] + [all_started])

    def weight(name, after):
        if name in riding:
            job, k = riding.pop(name)
            ready[name] = job.results[k]
        if name not in ready:
            landed = _gather_wait(PIECES[0], flying[0], gather_send[0], gather_recv[0], after)
            ready[name], = _core_forward(PIECES[:1], [landed])
        return ready[name]

    def before(call, after):
        if call in FORWARD_RIDES:
            group = FORWARD_RIDES[call]
            landed = [_gather_wait(PIECES[j], flying[j], gather_send[j], gather_recv[j], after) for j in group]
            job = _core_forward_job([PIECES[j] for j in group], landed)
            riding.update((PIECES[j].name, (job, k)) for k, j in enumerate(group))
            _ride_next_call(job)
        elif call == "econv_fwd":
            p.update(small_sharded(after))
        elif call == JOIN_RIDES_IN:
            early_join.append(join_job(JOIN_GROUPS[1], after))
            _ride_next_call(early_join[0])

    early_join = []

    def join_job(operands, after):
        pieces = [pc for pc in PIECES if pc.src in operands]
        done = {}
        for entry in list(scattering):
            if entry[0] in pieces:
                done[entry[0].name] = _scatter_wait(*entry, after)
                scattering.remove(entry)
        return _allsum_join_job(operands, [done[pc.name][0] for pc in pieces], [done[pc.name][1] for pc in pieces])

    scattering = []
    held = []

    core_op = jnp.reshape(lax.axis_index("c"), (1,)).astype(jnp.int32)

    def emit(name, a, b, tie=None):
        pc = PIECES[names_in_order.index(name)]
        held.append((pc, _mm_dw_chipsum(pc, a, b, core_op, tie)))
        if name in HOLD_BACK:
            return None
        pieces = [pc for pc, _ in held]
        started, token = _scatter_start(pieces, [chip_sum for _, chip_sum in held])
        scattering.extend((pc,) + tuple(st) for pc, st in zip(pieces, started))
        held.clear()
        return token

    def small_sharded(after):
        full = {}
        blk, land = _exchange8_wait("gather_small_wait", small_blk, small_land, small_send, small_recv, after)
        gathered = _collect8("gather_small_collect", blk, land, False)
        gathered = gathered.reshape(4, 2, sum(SHARDED_SHARD_ROWS), LANES)[:, 0]
        r0 = 0
        for n, sh, rows, full_sh in zip(SHARDED_NAMES, SHARDED_SHARD_SHAPES, SHARDED_SHARD_ROWS, SHARDED_FULL_SHAPES):
            per_chip = gathered[:, r0:r0 + rows].reshape(4, -1)[:, :full_sh[0] * LANES].reshape(4, full_sh[0], LANES)
            full[n] = jnp.transpose(per_chip, (1, 0, 2)).reshape(full_sh)
            r0 += rows
        return full

    p = {}
    p.update(norm_mix_g0=norm_mix_g[0:1], norm_mix_g1=norm_mix_g[1:2], norm_ffn_g0=norm_ffn_g[0:1], norm_ffn_g1=norm_ffn_g[1:2],
             even_conv_b=even_conv_b, even_ln_g=even_ln_g, even_ln_b=even_ln_b,
             odd_sg_w=odd_sg_w[0], odd_sg_bt=odd_sg_b[0].T, final_g=final_g[None, :],
             first_norm_after=idle_work_done)

    small = {}

    def emit_small(loss_row, g):
        parts = [loss_row, g["norm_mix_g1"], g["norm_ffn_g0"], g["norm_ffn_g1"], g["even_conv_b"], g["even_ln_g"],
                 g["even_ln_b"], g["odd_sg_w"], g["odd_sg_bt"].T, g["final_g"],
                 g["even_conv_k"], g["odd_conv_k"], g["odd_ln_g"], g["odd_ln_b"]]
        pack = _pack(parts, (8, 8, 8, 8) + REPL_ROWS[2:] + SHARDED_FULL_ROWS)
        small["blk"], small["land"], small["send"], small["recv"], token = _exchange8_start("allreduce_small_start", pack)
        return token

    dx, dg0 = _local_step(x[0], loss_target[0], p, weight, emit, emit_small, before)
    last_blk, last_land, last_send, last_recv, grad_token = _exchange8_start("allreduce_last_start", _pack([dg0], (8,)))

    big_grads = dict(zip((big_names[o] for o in JOIN_GROUPS[1]), early_join[0].results))
    delta, new_m, new_v, grads_big = {}, {}, {}, {}

    def adamw_big(n):
        d2, m2, v2, g2 = _adamw(f"adamw_{n}", shard2d(w, n), big_grads[n], shard2d(mom, n), shard2d(vel, n), True,
                                tie=grad_token)
        delta[n], new_m[n], new_v[n], grads_big[n] = (t.reshape(w[n].shape) for t in (d2, m2, v2, g2))

    for o in JOIN_GROUPS[1]:
        adamw_big(big_names[o])
    late_join = join_job(JOIN_GROUPS[0], new_v[big_names[JOIN_GROUPS[1][-1]]])
    big_grads.update(zip((big_names[o] for o in JOIN_GROUPS[0]), _run_job("allsum_join_late", late_join)))
    for o in JOIN_GROUPS[0]:
        adamw_big(big_names[o])

    joined_last = big_grads[big_names[JOIN_GROUPS[0][-1]]]
    grad_blk, grad_land = _exchange8_wait("allreduce_small_wait", small["blk"], small["land"], small["send"],
                                          small["recv"], joined_last)
    grad_sum = _collect8("allreduce_small_sum", grad_blk, grad_land, True)
    last_blk, last_land = _exchange8_wait("allreduce_last_wait", last_blk, last_land, last_send, last_recv, joined_last)
    dg0_sum = _collect8("allreduce_last_sum", last_blk, last_land, True)
    loss = grad_sum[0, 0]

    grads = dict(grads_big)
    for n, results in _adamw_small(*state_packs, grad_sum, dg0_sum).items():
        grads[n], delta[n], new_m[n], new_v[n] = (t.reshape(w[n].shape) for t in results)

    out = [loss, dx[None]]
    for res in (grads, delta, new_m, new_v):
        out.extend(res[n] for n in names)
    return tuple(out)
```
